```python
import math
import jax
import jax.numpy as jnp
from jax import lax
import numpy as np

D_MODEL = 2048
BATCH = 8
SEQ = 2048
DEPTH = 1

MIX_WIDTH = D_MODEL
S5_WIDTH = MIX_WIDTH // 2
S5_GROUP = 16
S5_GROUPS = S5_WIDTH // S5_GROUP
S5_STATE = 64
RWKV_WIDTH = MIX_WIDTH - S5_WIDTH
RWKV_HEAD = 64
RWKV_HEADS = RWKV_WIDTH // RWKV_HEAD
DECAY_LORA = max(32, int(round(1.8 * math.sqrt(RWKV_WIDTH) / 32)) * 32)
ICLR_LORA = DECAY_LORA
GATE_LORA = max(32, int(round(0.6 * RWKV_WIDTH ** 0.8 / 32)) * 32)
N_DIR = 2
RWKV_SPLITS = (RWKV_WIDTH, RWKV_WIDTH, RWKV_WIDTH, N_DIR * DECAY_LORA, N_DIR * ICLR_LORA, GATE_LORA)
RWKV_IN = sum(RWKV_SPLITS)
PROJ_WIDTH = S5_WIDTH + RWKV_IN
FFN_HIDDEN = 4 * D_MODEL
N_MOD = 6
NORM_EPS = 1e-6
GN_EPS = 64e-5
L2_EPS = 1e-12

kernel_name = 'hymba_s5_rwkv7_adaln_encoder_block'


def rms_norm(x, gain):
    xf = x.astype(jnp.float32)
    y = xf * lax.rsqrt(jnp.mean(xf * xf, axis=-1, keepdims=True) + NORM_EPS)
    return (y * gain).astype(x.dtype)


def modulate(h, shift, scale):
    return h * (1.0 + scale[:, None, :]) + shift[:, None, :]


def centred_token_shift(p, mu_prev, mu_next):
    prev = jnp.pad(p[:, :-1], ((0, 0), (1, 0), (0, 0)))
    nxt = jnp.pad(p[:, 1:], ((0, 0), (0, 1), (0, 0)))
    return p + mu_prev * (prev - p) + mu_next * (nxt - p)


def _complex_scan_combine(earlier, later):
    a1r, a1i, b1r, b1i = earlier
    a2r, a2i, b2r, b2i = later
    return (a1r * a2r - a1i * a2i,
            a1r * a2i + a1i * a2r,
            a2r * b1r - a2i * b1i + b2r,
            a2r * b1i + a2i * b1r + b2i)


def s5_mixer(u, lambda_re, lambda_im, log_step, b_re, b_im, c_re, c_im, d_skip, w_glu, b_glu):
    bsz, seq, _ = u.shape
    ug = u.reshape(bsz, seq, S5_GROUPS, S5_GROUP)
    states_re = []
    states_im = []
    for d in range(N_DIR):
        step = jnp.exp(log_step[d])[:, None]
        lr, li = lambda_re[d], lambda_im[d]
        mag = jnp.exp(lr * step)
        lbar_re = mag * jnp.cos(li * step)
        lbar_im = mag * jnp.sin(li * step)
        den = lr * lr + li * li
        nr = lbar_re - 1.0
        ni = lbar_im
        coef_re = ((nr * lr + ni * li) / den)[..., None]
        coef_im = ((ni * lr - nr * li) / den)[..., None]
        bbar_re = coef_re * b_re - coef_im * b_im
        bbar_im = coef_re * b_im + coef_im * b_re
        bu_re = jnp.einsum('bsgh,gph->bsgp', ug, bbar_re)
        bu_im = jnp.einsum('bsgh,gph->bsgp', ug, bbar_im)
        a_re = jnp.broadcast_to(lbar_re, (1, seq, S5_GROUPS, S5_STATE))
        a_im = jnp.broadcast_to(lbar_im, (1, seq, S5_GROUPS, S5_STATE))
        _, _, s_re, s_im = lax.associative_scan(
            _complex_scan_combine, (a_re, a_im, bu_re, bu_im), reverse=(d == 1), axis=1)
        states_re.append(s_re)
        states_im.append(s_im)
    x_re = states_re[0] + states_re[1]
    x_im = states_im[0] + states_im[1]
    y = jnp.einsum('bsgp,ghp->bsgh', x_re, c_re) - jnp.einsum('bsgp,ghp->bsgh', x_im, c_im)
    y = y.reshape(bsz, seq, S5_WIDTH) + d_skip * u
    y = jax.nn.gelu(y)
    return y * jax.nn.sigmoid(y @ w_glu + b_glu)


def _heads(t):
    return t.reshape(*t.shape[:-1], RWKV_HEADS, RWKV_HEAD)


def _shared_dir_time_major(t):
    t = jnp.stack([t, jnp.flip(t, axis=1)], axis=2)
    return jnp.transpose(t, (1, 2, 0, 3, 4))


def _dir_time_major(t):
    t = jnp.stack([t[:, :, 0], jnp.flip(t[:, :, 1], axis=1)], axis=2)
    return jnp.transpose(t, (1, 2, 0, 3, 4))


def _rwkv7_step(state, inp):
    r_t, w_t, k_t, v_t, kk_t, a_t = inp
    sa = jnp.einsum('dbhij,dbhj->dbhi', state, -kk_t)
    state = (state * w_t[..., None, :]
             + sa[..., :, None] * (kk_t * a_t)[..., None, :]
             + v_t[..., :, None] * k_t[..., None, :])
    y = jnp.einsum('dbhij,dbhj->dbhi', state, r_t)
    return state, y


def rwkv7_mixer(p, mu_prev, mu_next, w0, w_up, a0, a_up, g_up, k_k, k_a, r_k, ln_gain, ln_bias):
    bsz, seq, _ = p.shape
    p = centred_token_shift(p, mu_prev, mu_next)
    cut = [sum(RWKV_SPLITS[:i + 1]) for i in range(len(RWKV_SPLITS) - 1)]
    r, k, v, w_dn, a_dn, g_dn = jnp.split(p, cut, axis=-1)
    w_dn = w_dn.reshape(bsz, seq, N_DIR, DECAY_LORA)
    a_dn = a_dn.reshape(bsz, seq, N_DIR, ICLR_LORA)
    w = -jax.nn.softplus(-(w0 + jnp.einsum('bsdl,dlc->bsdc', jnp.tanh(w_dn), w_up))) - 0.5
    decay = jnp.exp(-jnp.exp(w))
    a = jax.nn.sigmoid(a0 + jnp.einsum('bsdl,dlc->bsdc', a_dn, a_up))
    g = jax.nn.sigmoid(g_dn) @ g_up
    kk = _heads(k * k_k).astype(jnp.float32)
    kk = kk / jnp.maximum(jnp.sqrt(jnp.sum(kk * kk, axis=-1, keepdims=True)), L2_EPS)
    k_dir = k[:, :, None, :] * (1.0 + (a - 1.0) * k_a)
    r_h, v_h = _heads(r), _heads(v)
    k_dir_h, decay_h, a_h = _heads(k_dir), _heads(decay), _heads(a)
    xs = (_shared_dir_time_major(r_h), _dir_time_major(decay_h), _dir_time_major(k_dir_h),
          _shared_dir_time_major(v_h), _shared_dir_time_major(kk), _dir_time_major(a_h))
    state0 = jnp.zeros((N_DIR, bsz, RWKV_HEADS, RWKV_HEAD, RWKV_HEAD), jnp.float32)
    _, ys = lax.scan(_rwkv7_step, state0, xs)
    ys = jnp.transpose(ys, (2, 0, 1, 3, 4))
    y = ys[:, :, 0] + jnp.flip(ys[:, :, 1], axis=1)
    mu = jnp.mean(y, axis=-1, keepdims=True)
    var = jnp.mean(jnp.square(y - mu), axis=-1, keepdims=True)
    y = ((y - mu) * lax.rsqrt(var + GN_EPS)).reshape(bsz, seq, RWKV_WIDTH) * ln_gain + ln_bias
    bonus_coef = jnp.sum(r_h[:, :, None] * k_dir_h * r_k, axis=(2, 4))[..., None]
    y = y + (bonus_coef * v_h).reshape(bsz, seq, RWKV_WIDTH)
    return y * g


def _fwd_setup_inputs(seed: int = 0) -> dict:
    key = jax.random.key(seed)
    ks = iter(jax.random.split(key, 48))

    def nrm(shape, scale):
        return scale * jax.random.normal(next(ks), shape, jnp.float32)

    def unif(shape, lo, hi):
        return jax.random.uniform(next(ks), shape, jnp.float32, lo, hi)

    L, C, G, P, Hg = DEPTH, RWKV_WIDTH, S5_GROUPS, S5_STATE, S5_GROUP
    w0_base = -6.0 + 5.0 * jnp.linspace(0.0, 1.0, C) ** 0.85
    lam_im_base = jnp.pi * jnp.arange(P, dtype=jnp.float32)
    return {
        'x': nrm((BATCH, SEQ, D_MODEL), 1.0),
        'c': nrm((BATCH, D_MODEL), 1.0),
        'ada_w': nrm((L, D_MODEL, N_MOD * D_MODEL), 0.5 * D_MODEL ** -0.5),
        'ada_b': nrm((L, N_MOD * D_MODEL), 0.02),
        'norm1_gain': 1.0 + nrm((L, D_MODEL), 0.05),
        'norm2_gain': 1.0 + nrm((L, D_MODEL), 0.05),
        'final_gain': 1.0 + nrm((D_MODEL,), 0.05),
        'w_in': nrm((L, D_MODEL, PROJ_WIDTH), D_MODEL ** -0.5),
        'w_out': nrm((L, MIX_WIDTH, D_MODEL), MIX_WIDTH ** -0.5),
        's5_lambda_re': -0.5 + nrm((L, N_DIR, G, P), 0.01),
        's5_lambda_im': lam_im_base + nrm((L, N_DIR, G, P), 0.01),
        's5_log_step': unif((L, N_DIR, G), math.log(1e-3), math.log(1e-1)),
        's5_b_re': nrm((L, G, P, Hg), (2 * Hg) ** -0.5),
        's5_b_im': nrm((L, G, P, Hg), (2 * Hg) ** -0.5),
        's5_c_re': nrm((L, G, Hg, P), 0.5),
        's5_c_im': nrm((L, G, Hg, P), 0.5),
        's5_d': nrm((L, S5_WIDTH), 0.5),
        's5_w_glu': nrm((L, S5_WIDTH, S5_WIDTH), S5_WIDTH ** -0.5),
        's5_b_glu': nrm((L, S5_WIDTH), 0.02),
        'rk_shift_prev': unif((L, RWKV_IN), 0.1, 0.5),
        'rk_shift_next': unif((L, RWKV_IN), 0.1, 0.5),
        'rk_w0': w0_base + nrm((L, N_DIR, C), 0.1),
        'rk_w_up': nrm((L, N_DIR, DECAY_LORA, C), 0.1),
        'rk_a0': nrm((L, N_DIR, C), 0.1),
        'rk_a_up': nrm((L, N_DIR, ICLR_LORA, C), 0.5 * ICLR_LORA ** -0.5),
        'rk_g_up': nrm((L, GATE_LORA, C), GATE_LORA ** -0.5),
        'rk_k_k': 0.85 + nrm((L, C), 0.05),
        'rk_k_a': 1.0 + nrm((L, C), 0.05),
        'rk_r_k': nrm((L, RWKV_HEADS, RWKV_HEAD), 0.1),
        'rk_ln_gain': 1.0 + nrm((L, C), 0.05),
        'rk_ln_bias': nrm((L, C), 0.02),
        'ffn_w1': nrm((L, D_MODEL, FFN_HIDDEN), D_MODEL ** -0.5),
        'ffn_w2': nrm((L, FFN_HIDDEN, D_MODEL), FFN_HIDDEN ** -0.5),
    }


def _fwd_reference(x, c, ada_w, ada_b, norm1_gain, norm2_gain, final_gain, w_in, w_out,
              s5_lambda_re, s5_lambda_im, s5_log_step, s5_b_re, s5_b_im, s5_c_re, s5_c_im,
              s5_d, s5_w_glu, s5_b_glu, rk_shift_prev, rk_shift_next, rk_w0, rk_w_up,
              rk_a0, rk_a_up, rk_g_up, rk_k_k, rk_k_a, rk_r_k, rk_ln_gain, rk_ln_bias,
              ffn_w1, ffn_w2):
    c_act = jax.nn.silu(c)
    for l in range(DEPTH):
        mod = c_act @ ada_w[l] + ada_b[l]
        shift1, scale1, gate1, shift2, scale2, gate2 = jnp.split(mod, N_MOD, axis=-1)
        h = modulate(rms_norm(x, norm1_gain[l]), shift1, scale1)
        proj = h @ w_in[l]
        u_s5, p_rwkv = jnp.split(proj, [S5_WIDTH], axis=-1)
        y_s5 = s5_mixer(u_s5, s5_lambda_re[l], s5_lambda_im[l], s5_log_step[l],
                        s5_b_re[l], s5_b_im[l], s5_c_re[l], s5_c_im[l], s5_d[l],
                        s5_w_glu[l], s5_b_glu[l])
        y_rk = rwkv7_mixer(p_rwkv, rk_shift_prev[l], rk_shift_next[l], rk_w0[l], rk_w_up[l],
                           rk_a0[l], rk_a_up[l], rk_g_up[l], rk_k_k[l], rk_k_a[l], rk_r_k[l],
                           rk_ln_gain[l], rk_ln_bias[l])
        mixed = jnp.concatenate([y_s5, y_rk], axis=-1) @ w_out[l]
        x = x + gate1[:, None, :] * mixed
        h = modulate(rms_norm(x, norm2_gain[l]), shift2, scale2)
        ffn = jnp.square(jax.nn.relu(h @ ffn_w1[l])) @ ffn_w2[l]
        x = x + gate2[:, None, :] * ffn
    return rms_norm(x, final_gain)


import jax as _jax
import jax.numpy as _jnp

TWIN_FORMAT = 'train_step'
FWD_PARAMS = ['x', 'c', 'ada_w', 'ada_b', 'norm1_gain', 'norm2_gain', 'final_gain', 'w_in', 'w_out', 's5_lambda_re', 's5_lambda_im', 's5_log_step', 's5_b_re', 's5_b_im', 's5_c_re', 's5_c_im', 's5_d', 's5_w_glu', 's5_b_glu', 'rk_shift_prev', 'rk_shift_next', 'rk_w0', 'rk_w_up', 'rk_a0', 'rk_a_up', 'rk_g_up', 'rk_k_k', 'rk_k_a', 'rk_r_k', 'rk_ln_gain', 'rk_ln_bias', 'ffn_w1', 'ffn_w2']
TWIN_WEIGHTS = ['ada_w', 'ada_b', 'norm1_gain', 'norm2_gain', 'final_gain', 'w_in', 'w_out', 's5_lambda_re', 's5_lambda_im', 's5_log_step', 's5_b_re', 's5_b_im', 's5_c_re', 's5_c_im', 's5_d', 's5_w_glu', 's5_b_glu', 'rk_shift_prev', 'rk_shift_next', 'rk_w0', 'rk_w_up', 'rk_a0', 'rk_a_up', 'rk_g_up', 'rk_k_k', 'rk_k_a', 'rk_r_k', 'rk_ln_gain', 'rk_ln_bias', 'ffn_w1', 'ffn_w2']
TWIN_DIFF_INPUT = 'x'
TWIN_INPUTS = ['x', 'c', 'ada_w', 'ada_b', 'norm1_gain', 'norm2_gain', 'final_gain', 'w_in', 'w_out', 's5_lambda_re', 's5_lambda_im', 's5_log_step', 's5_b_re', 's5_b_im', 's5_c_re', 's5_c_im', 's5_d', 's5_w_glu', 's5_b_glu', 'rk_shift_prev', 'rk_shift_next', 'rk_w0', 'rk_w_up', 'rk_a0', 'rk_a_up', 'rk_g_up', 'rk_k_k', 'rk_k_a', 'rk_r_k', 'rk_ln_gain', 'rk_ln_bias', 'ffn_w1', 'ffn_w2', 'loss_target', 'm_ada_w', 'm_ada_b', 'm_norm1_gain', 'm_norm2_gain', 'm_final_gain', 'm_w_in', 'm_w_out', 'm_s5_lambda_re', 'm_s5_lambda_im', 'm_s5_log_step', 'm_s5_b_re', 'm_s5_b_im', 'm_s5_c_re', 'm_s5_c_im', 'm_s5_d', 'm_s5_w_glu', 'm_s5_b_glu', 'm_rk_shift_prev', 'm_rk_shift_next', 'm_rk_w0', 'm_rk_w_up', 'm_rk_a0', 'm_rk_a_up', 'm_rk_g_up', 'm_rk_k_k', 'm_rk_k_a', 'm_rk_r_k', 'm_rk_ln_gain', 'm_rk_ln_bias', 'm_ffn_w1', 'm_ffn_w2', 'v_ada_w', 'v_ada_b', 'v_norm1_gain', 'v_norm2_gain', 'v_final_gain', 'v_w_in', 'v_w_out', 'v_s5_lambda_re', 'v_s5_lambda_im', 'v_s5_log_step', 'v_s5_b_re', 'v_s5_b_im', 'v_s5_c_re', 'v_s5_c_im', 'v_s5_d', 'v_s5_w_glu', 'v_s5_b_glu', 'v_rk_shift_prev', 'v_rk_shift_next', 'v_rk_w0', 'v_rk_w_up', 'v_rk_a0', 'v_rk_a_up', 'v_rk_g_up', 'v_rk_k_k', 'v_rk_k_a', 'v_rk_r_k', 'v_rk_ln_gain', 'v_rk_ln_bias', 'v_ffn_w1', 'v_ffn_w2']
TWIN_OUTPUTS = ['loss', 'grad_x', 'grad_ada_w', 'grad_ada_b', 'grad_norm1_gain', 'grad_norm2_gain', 'grad_final_gain', 'grad_w_in', 'grad_w_out', 'grad_s5_lambda_re', 'grad_s5_lambda_im', 'grad_s5_log_step', 'grad_s5_b_re', 'grad_s5_b_im', 'grad_s5_c_re', 'grad_s5_c_im', 'grad_s5_d', 'grad_s5_w_glu', 'grad_s5_b_glu', 'grad_rk_shift_prev', 'grad_rk_shift_next', 'grad_rk_w0', 'grad_rk_w_up', 'grad_rk_a0', 'grad_rk_a_up', 'grad_rk_g_up', 'grad_rk_k_k', 'grad_rk_k_a', 'grad_rk_r_k', 'grad_rk_ln_gain', 'grad_rk_ln_bias', 'grad_ffn_w1', 'grad_ffn_w2', 'delta_ada_w', 'delta_ada_b', 'delta_norm1_gain', 'delta_norm2_gain', 'delta_final_gain', 'delta_w_in', 'delta_w_out', 'delta_s5_lambda_re', 'delta_s5_lambda_im', 'delta_s5_log_step', 'delta_s5_b_re', 'delta_s5_b_im', 'delta_s5_c_re', 'delta_s5_c_im', 'delta_s5_d', 'delta_s5_w_glu', 'delta_s5_b_glu', 'delta_rk_shift_prev', 'delta_rk_shift_next', 'delta_rk_w0', 'delta_rk_w_up', 'delta_rk_a0', 'delta_rk_a_up', 'delta_rk_g_up', 'delta_rk_k_k', 'delta_rk_k_a', 'delta_rk_r_k', 'delta_rk_ln_gain', 'delta_rk_ln_bias', 'delta_ffn_w1', 'delta_ffn_w2', 'new_m_ada_w', 'new_m_ada_b', 'new_m_norm1_gain', 'new_m_norm2_gain', 'new_m_final_gain', 'new_m_w_in', 'new_m_w_out', 'new_m_s5_lambda_re', 'new_m_s5_lambda_im', 'new_m_s5_log_step', 'new_m_s5_b_re', 'new_m_s5_b_im', 'new_m_s5_c_re', 'new_m_s5_c_im', 'new_m_s5_d', 'new_m_s5_w_glu', 'new_m_s5_b_glu', 'new_m_rk_shift_prev', 'new_m_rk_shift_next', 'new_m_rk_w0', 'new_m_rk_w_up', 'new_m_rk_a0', 'new_m_rk_a_up', 'new_m_rk_g_up', 'new_m_rk_k_k', 'new_m_rk_k_a', 'new_m_rk_r_k', 'new_m_rk_ln_gain', 'new_m_rk_ln_bias', 'new_m_ffn_w1', 'new_m_ffn_w2', 'new_v_ada_w', 'new_v_ada_b', 'new_v_norm1_gain', 'new_v_norm2_gain', 'new_v_final_gain', 'new_v_w_in', 'new_v_w_out', 'new_v_s5_lambda_re', 'new_v_s5_lambda_im', 'new_v_s5_log_step', 'new_v_s5_b_re', 'new_v_s5_b_im', 'new_v_s5_c_re', 'new_v_s5_c_im', 'new_v_s5_d', 'new_v_s5_w_glu', 'new_v_s5_b_glu', 'new_v_rk_shift_prev', 'new_v_rk_shift_next', 'new_v_rk_w0', 'new_v_rk_w_up', 'new_v_rk_a0', 'new_v_rk_a_up', 'new_v_rk_g_up', 'new_v_rk_k_k', 'new_v_rk_k_a', 'new_v_rk_r_k', 'new_v_rk_ln_gain', 'new_v_rk_ln_bias', 'new_v_ffn_w1', 'new_v_ffn_w2']
TWIN_LEAF_KINDS = {'loss': 'loss', 'grad_x': 'grad_x', 'grad_ada_w': 'grad_w', 'grad_ada_b': 'grad_w', 'grad_norm1_gain': 'grad_w', 'grad_norm2_gain': 'grad_w', 'grad_final_gain': 'grad_w', 'grad_w_in': 'grad_w', 'grad_w_out': 'grad_w', 'grad_s5_lambda_re': 'grad_w', 'grad_s5_lambda_im': 'grad_w', 'grad_s5_log_step': 'grad_w', 'grad_s5_b_re': 'grad_w', 'grad_s5_b_im': 'grad_w', 'grad_s5_c_re': 'grad_w', 'grad_s5_c_im': 'grad_w', 'grad_s5_d': 'grad_w', 'grad_s5_w_glu': 'grad_w', 'grad_s5_b_glu': 'grad_w', 'grad_rk_shift_prev': 'grad_w', 'grad_rk_shift_next': 'grad_w', 'grad_rk_w0': 'grad_w', 'grad_rk_w_up': 'grad_w', 'grad_rk_a0': 'grad_w', 'grad_rk_a_up': 'grad_w', 'grad_rk_g_up': 'grad_w', 'grad_rk_k_k': 'grad_w', 'grad_rk_k_a': 'grad_w', 'grad_rk_r_k': 'grad_w', 'grad_rk_ln_gain': 'grad_w', 'grad_rk_ln_bias': 'grad_w', 'grad_ffn_w1': 'grad_w', 'grad_ffn_w2': 'grad_w', 'delta_ada_w': 'delta_w', 'delta_ada_b': 'delta_w', 'delta_norm1_gain': 'delta_w', 'delta_norm2_gain': 'delta_w', 'delta_final_gain': 'delta_w', 'delta_w_in': 'delta_w', 'delta_w_out': 'delta_w', 'delta_s5_lambda_re': 'delta_w', 'delta_s5_lambda_im': 'delta_w', 'delta_s5_log_step': 'delta_w', 'delta_s5_b_re': 'delta_w', 'delta_s5_b_im': 'delta_w', 'delta_s5_c_re': 'delta_w', 'delta_s5_c_im': 'delta_w', 'delta_s5_d': 'delta_w', 'delta_s5_w_glu': 'delta_w', 'delta_s5_b_glu': 'delta_w', 'delta_rk_shift_prev': 'delta_w', 'delta_rk_shift_next': 'delta_w', 'delta_rk_w0': 'delta_w', 'delta_rk_w_up': 'delta_w', 'delta_rk_a0': 'delta_w', 'delta_rk_a_up': 'delta_w', 'delta_rk_g_up': 'delta_w', 'delta_rk_k_k': 'delta_w', 'delta_rk_k_a': 'delta_w', 'delta_rk_r_k': 'delta_w', 'delta_rk_ln_gain': 'delta_w', 'delta_rk_ln_bias': 'delta_w', 'delta_ffn_w1': 'delta_w', 'delta_ffn_w2': 'delta_w', 'new_m_ada_w': 'new_m', 'new_m_ada_b': 'new_m', 'new_m_norm1_gain': 'new_m', 'new_m_norm2_gain': 'new_m', 'new_m_final_gain': 'new_m', 'new_m_w_in': 'new_m', 'new_m_w_out': 'new_m', 'new_m_s5_lambda_re': 'new_m', 'new_m_s5_lambda_im': 'new_m', 'new_m_s5_log_step': 'new_m', 'new_m_s5_b_re': 'new_m', 'new_m_s5_b_im': 'new_m', 'new_m_s5_c_re': 'new_m', 'new_m_s5_c_im': 'new_m', 'new_m_s5_d': 'new_m', 'new_m_s5_w_glu': 'new_m', 'new_m_s5_b_glu': 'new_m', 'new_m_rk_shift_prev': 'new_m', 'new_m_rk_shift_next': 'new_m', 'new_m_rk_w0': 'new_m', 'new_m_rk_w_up': 'new_m', 'new_m_rk_a0': 'new_m', 'new_m_rk_a_up': 'new_m', 'new_m_rk_g_up': 'new_m', 'new_m_rk_k_k': 'new_m', 'new_m_rk_k_a': 'new_m', 'new_m_rk_r_k': 'new_m', 'new_m_rk_ln_gain': 'new_m', 'new_m_rk_ln_bias': 'new_m', 'new_m_ffn_w1': 'new_m', 'new_m_ffn_w2': 'new_m', 'new_v_ada_w': 'new_v', 'new_v_ada_b': 'new_v', 'new_v_norm1_gain': 'new_v', 'new_v_norm2_gain': 'new_v', 'new_v_final_gain': 'new_v', 'new_v_w_in': 'new_v', 'new_v_w_out': 'new_v', 'new_v_s5_lambda_re': 'new_v', 'new_v_s5_lambda_im': 'new_v', 'new_v_s5_log_step': 'new_v', 'new_v_s5_b_re': 'new_v', 'new_v_s5_b_im': 'new_v', 'new_v_s5_c_re': 'new_v', 'new_v_s5_c_im': 'new_v', 'new_v_s5_d': 'new_v', 'new_v_s5_w_glu': 'new_v', 'new_v_s5_b_glu': 'new_v', 'new_v_rk_shift_prev': 'new_v', 'new_v_rk_shift_next': 'new_v', 'new_v_rk_w0': 'new_v', 'new_v_rk_w_up': 'new_v', 'new_v_rk_a0': 'new_v', 'new_v_rk_a_up': 'new_v', 'new_v_rk_g_up': 'new_v', 'new_v_rk_k_k': 'new_v', 'new_v_rk_k_a': 'new_v', 'new_v_rk_r_k': 'new_v', 'new_v_rk_ln_gain': 'new_v', 'new_v_rk_ln_bias': 'new_v', 'new_v_ffn_w1': 'new_v', 'new_v_ffn_w2': 'new_v'}


def _forward(args):
    return _fwd_reference(*[args[k] for k in FWD_PARAMS])


def _output_shape():
    out = _jax.eval_shape(lambda: _forward(_fwd_setup_inputs(0)))
    return out.shape, out.dtype

N_MICROBATCH = 1
ADAM_LR = 0.001
ADAM_B1 = 0.9
ADAM_B2 = 0.999
ADAM_EPS = 1e-08
ADAM_WD = 0.01
ADAM_STEP = 10
PER_EXAMPLE_BATCH_AXIS = {'x': 0, 'c': 0, 'loss_target': 0}
SHARED_INPUTS = []
_WEIGHT_DTYPES = {'ada_w': _jnp.float32, 'ada_b': _jnp.float32, 'norm1_gain': _jnp.float32, 'norm2_gain': _jnp.float32, 'final_gain': _jnp.float32, 'w_in': _jnp.float32, 'w_out': _jnp.float32, 's5_lambda_re': _jnp.float32, 's5_lambda_im': _jnp.float32, 's5_log_step': _jnp.float32, 's5_b_re': _jnp.float32, 's5_b_im': _jnp.float32, 's5_c_re': _jnp.float32, 's5_c_im': _jnp.float32, 's5_d': _jnp.float32, 's5_w_glu': _jnp.float32, 's5_b_glu': _jnp.float32, 'rk_shift_prev': _jnp.float32, 'rk_shift_next': _jnp.float32, 'rk_w0': _jnp.float32, 'rk_w_up': _jnp.float32, 'rk_a0': _jnp.float32, 'rk_a_up': _jnp.float32, 'rk_g_up': _jnp.float32, 'rk_k_k': _jnp.float32, 'rk_k_a': _jnp.float32, 'rk_r_k': _jnp.float32, 'rk_ln_gain': _jnp.float32, 'rk_ln_bias': _jnp.float32, 'ffn_w1': _jnp.float32, 'ffn_w2': _jnp.float32}
MOMENT_SCALE = {'ada_w': 5.203731e-02, 'ada_b': 9.831520e-02, 'norm1_gain': 1.339870e-02, 'norm2_gain': 2.798198e-02, 'final_gain': 8.069260e+00, 'w_in': 1.016000e-02, 'w_out': 9.072462e-03, 's5_lambda_re': 2.847286e-03, 's5_lambda_im': 3.239922e-03, 's5_log_step': 1.559338e+00, 's5_b_re': 3.264620e-03, 's5_b_im': 3.221637e-03, 's5_c_re': 1.199277e-03, 's5_c_im': 1.164415e-03, 's5_d': 8.063278e-03, 's5_w_glu': 1.952169e-03, 's5_b_glu': 3.051318e-03, 'rk_shift_prev': 1.877453e-02, 'rk_shift_next': 1.848352e-02, 'rk_w0': 3.817236e-03, 'rk_w_up': 1.001425e-03, 'rk_a0': 3.083277e-03, 'rk_a_up': 2.418228e-03, 'rk_g_up': 1.104697e-02, 'rk_k_k': 3.657524e-02, 'rk_k_a': 3.438869e-02, 'rk_r_k': 3.053976e-02, 'rk_ln_gain': 1.030607e-02, 'rk_ln_bias': 1.344467e-02, 'ffn_w1': 1.505893e-02, 'ffn_w2': 3.335499e-02}


def _to_microbatches(a, axis):
    t = _jnp.moveaxis(a, axis, 0)
    t = t.reshape((N_MICROBATCH, t.shape[0] // N_MICROBATCH) + t.shape[1:])
    return _jnp.moveaxis(t, 1, axis + 1)


def setup_inputs(seed: int = 0) -> dict:
    inp = _fwd_setup_inputs(seed)
    key = _jax.random.fold_in(_jax.random.key(seed), 7919)
    shape, _ = _output_shape()
    out = dict(inp)
    out["loss_target"] = _jax.random.normal(_jax.random.fold_in(key, 0), shape, _jnp.float32)
    for i, name in enumerate(TWIN_WEIGHTS):
        w = inp[name].astype(_jnp.float32)
        if MOMENT_SCALE is None:
            s = _jnp.sqrt(_jnp.mean(_jnp.square(w)) + 1e-30)
        else:
            s = MOMENT_SCALE[name]
        km, kv = _jax.random.split(_jax.random.fold_in(key, i + 1))
        out[name] = w
        out["m_" + name] = s * _jax.random.normal(km, w.shape, _jnp.float32)
        out["v_" + name] = (s * s) * _jax.random.uniform(kv, w.shape, _jnp.float32, 0.5, 1.5)
    if N_MICROBATCH > 1:
        for name, axis in PER_EXAMPLE_BATCH_AXIS.items():
            out[name] = _to_microbatches(out[name], axis)
    return {'x': out['x'], 'c': out['c'], 'ada_w': out['ada_w'], 'ada_b': out['ada_b'], 'norm1_gain': out['norm1_gain'], 'norm2_gain': out['norm2_gain'], 'final_gain': out['final_gain'], 'w_in': out['w_in'], 'w_out': out['w_out'], 's5_lambda_re': out['s5_lambda_re'], 's5_lambda_im': out['s5_lambda_im'], 's5_log_step': out['s5_log_step'], 's5_b_re': out['s5_b_re'], 's5_b_im': out['s5_b_im'], 's5_c_re': out['s5_c_re'], 's5_c_im': out['s5_c_im'], 's5_d': out['s5_d'], 's5_w_glu': out['s5_w_glu'], 's5_b_glu': out['s5_b_glu'], 'rk_shift_prev': out['rk_shift_prev'], 'rk_shift_next': out['rk_shift_next'], 'rk_w0': out['rk_w0'], 'rk_w_up': out['rk_w_up'], 'rk_a0': out['rk_a0'], 'rk_a_up': out['rk_a_up'], 'rk_g_up': out['rk_g_up'], 'rk_k_k': out['rk_k_k'], 'rk_k_a': out['rk_k_a'], 'rk_r_k': out['rk_r_k'], 'rk_ln_gain': out['rk_ln_gain'], 'rk_ln_bias': out['rk_ln_bias'], 'ffn_w1': out['ffn_w1'], 'ffn_w2': out['ffn_w2'], 'loss_target': out['loss_target'], 'm_ada_w': out['m_ada_w'], 'm_ada_b': out['m_ada_b'], 'm_norm1_gain': out['m_norm1_gain'], 'm_norm2_gain': out['m_norm2_gain'], 'm_final_gain': out['m_final_gain'], 'm_w_in': out['m_w_in'], 'm_w_out': out['m_w_out'], 'm_s5_lambda_re': out['m_s5_lambda_re'], 'm_s5_lambda_im': out['m_s5_lambda_im'], 'm_s5_log_step': out['m_s5_log_step'], 'm_s5_b_re': out['m_s5_b_re'], 'm_s5_b_im': out['m_s5_b_im'], 'm_s5_c_re': out['m_s5_c_re'], 'm_s5_c_im': out['m_s5_c_im'], 'm_s5_d': out['m_s5_d'], 'm_s5_w_glu': out['m_s5_w_glu'], 'm_s5_b_glu': out['m_s5_b_glu'], 'm_rk_shift_prev': out['m_rk_shift_prev'], 'm_rk_shift_next': out['m_rk_shift_next'], 'm_rk_w0': out['m_rk_w0'], 'm_rk_w_up': out['m_rk_w_up'], 'm_rk_a0': out['m_rk_a0'], 'm_rk_a_up': out['m_rk_a_up'], 'm_rk_g_up': out['m_rk_g_up'], 'm_rk_k_k': out['m_rk_k_k'], 'm_rk_k_a': out['m_rk_k_a'], 'm_rk_r_k': out['m_rk_r_k'], 'm_rk_ln_gain': out['m_rk_ln_gain'], 'm_rk_ln_bias': out['m_rk_ln_bias'], 'm_ffn_w1': out['m_ffn_w1'], 'm_ffn_w2': out['m_ffn_w2'], 'v_ada_w': out['v_ada_w'], 'v_ada_b': out['v_ada_b'], 'v_norm1_gain': out['v_norm1_gain'], 'v_norm2_gain': out['v_norm2_gain'], 'v_final_gain': out['v_final_gain'], 'v_w_in': out['v_w_in'], 'v_w_out': out['v_w_out'], 'v_s5_lambda_re': out['v_s5_lambda_re'], 'v_s5_lambda_im': out['v_s5_lambda_im'], 'v_s5_log_step': out['v_s5_log_step'], 'v_s5_b_re': out['v_s5_b_re'], 'v_s5_b_im': out['v_s5_b_im'], 'v_s5_c_re': out['v_s5_c_re'], 'v_s5_c_im': out['v_s5_c_im'], 'v_s5_d': out['v_s5_d'], 'v_s5_w_glu': out['v_s5_w_glu'], 'v_s5_b_glu': out['v_s5_b_glu'], 'v_rk_shift_prev': out['v_rk_shift_prev'], 'v_rk_shift_next': out['v_rk_shift_next'], 'v_rk_w0': out['v_rk_w0'], 'v_rk_w_up': out['v_rk_w_up'], 'v_rk_a0': out['v_rk_a0'], 'v_rk_a_up': out['v_rk_a_up'], 'v_rk_g_up': out['v_rk_g_up'], 'v_rk_k_k': out['v_rk_k_k'], 'v_rk_k_a': out['v_rk_k_a'], 'v_rk_r_k': out['v_rk_r_k'], 'v_rk_ln_gain': out['v_rk_ln_gain'], 'v_rk_ln_bias': out['v_rk_ln_bias'], 'v_ffn_w1': out['v_ffn_w1'], 'v_ffn_w2': out['v_ffn_w2']}


def _loss(weights, diff, rest, loss_target):
    with _jax.named_scope("forward"):
        args = {**rest, TWIN_DIFF_INPUT: diff, **{k: w.astype(_WEIGHT_DTYPES[k]) for k, w in weights.items()}}
        y = _forward(args)
    with _jax.named_scope("loss_head"):
        err = _jnp.square(y.astype(_jnp.float32) - loss_target)
        return 0.5 * _jnp.sum(_jnp.mean(err, axis=-1)) if err.ndim else 0.5 * err


def _adamw(w, g, m, v):
    m = ADAM_B1 * m + (1.0 - ADAM_B1) * g
    v = ADAM_B2 * v + (1.0 - ADAM_B2) * _jnp.square(g)
    m_hat = m / (1.0 - ADAM_B1 ** ADAM_STEP)
    v_hat = v / (1.0 - ADAM_B2 ** ADAM_STEP)
    delta = -ADAM_LR * (m_hat / (_jnp.sqrt(v_hat) + ADAM_EPS) + ADAM_WD * w)
    return delta, m, v


def reference(x, c, ada_w, ada_b, norm1_gain, norm2_gain, final_gain, w_in, w_out, s5_lambda_re, s5_lambda_im, s5_log_step, s5_b_re, s5_b_im, s5_c_re, s5_c_im, s5_d, s5_w_glu, s5_b_glu, rk_shift_prev, rk_shift_next, rk_w0, rk_w_up, rk_a0, rk_a_up, rk_g_up, rk_k_k, rk_k_a, rk_r_k, rk_ln_gain, rk_ln_bias, ffn_w1, ffn_w2, loss_target, m_ada_w, m_ada_b, m_norm1_gain, m_norm2_gain, m_final_gain, m_w_in, m_w_out, m_s5_lambda_re, m_s5_lambda_im, m_s5_log_step, m_s5_b_re, m_s5_b_im, m_s5_c_re, m_s5_c_im, m_s5_d, m_s5_w_glu, m_s5_b_glu, m_rk_shift_prev, m_rk_shift_next, m_rk_w0, m_rk_w_up, m_rk_a0, m_rk_a_up, m_rk_g_up, m_rk_k_k, m_rk_k_a, m_rk_r_k, m_rk_ln_gain, m_rk_ln_bias, m_ffn_w1, m_ffn_w2, v_ada_w, v_ada_b, v_norm1_gain, v_norm2_gain, v_final_gain, v_w_in, v_w_out, v_s5_lambda_re, v_s5_lambda_im, v_s5_log_step, v_s5_b_re, v_s5_b_im, v_s5_c_re, v_s5_c_im, v_s5_d, v_s5_w_glu, v_s5_b_glu, v_rk_shift_prev, v_rk_shift_next, v_rk_w0, v_rk_w_up, v_rk_a0, v_rk_a_up, v_rk_g_up, v_rk_k_k, v_rk_k_a, v_rk_r_k, v_rk_ln_gain, v_rk_ln_bias, v_ffn_w1, v_ffn_w2):
    given = dict(x=x, c=c, ada_w=ada_w, ada_b=ada_b, norm1_gain=norm1_gain, norm2_gain=norm2_gain, final_gain=final_gain, w_in=w_in, w_out=w_out, s5_lambda_re=s5_lambda_re, s5_lambda_im=s5_lambda_im, s5_log_step=s5_log_step, s5_b_re=s5_b_re, s5_b_im=s5_b_im, s5_c_re=s5_c_re, s5_c_im=s5_c_im, s5_d=s5_d, s5_w_glu=s5_w_glu, s5_b_glu=s5_b_glu, rk_shift_prev=rk_shift_prev, rk_shift_next=rk_shift_next, rk_w0=rk_w0, rk_w_up=rk_w_up, rk_a0=rk_a0, rk_a_up=rk_a_up, rk_g_up=rk_g_up, rk_k_k=rk_k_k, rk_k_a=rk_k_a, rk_r_k=rk_r_k, rk_ln_gain=rk_ln_gain, rk_ln_bias=rk_ln_bias, ffn_w1=ffn_w1, ffn_w2=ffn_w2, loss_target=loss_target, m_ada_w=m_ada_w, m_ada_b=m_ada_b, m_norm1_gain=m_norm1_gain, m_norm2_gain=m_norm2_gain, m_final_gain=m_final_gain, m_w_in=m_w_in, m_w_out=m_w_out, m_s5_lambda_re=m_s5_lambda_re, m_s5_lambda_im=m_s5_lambda_im, m_s5_log_step=m_s5_log_step, m_s5_b_re=m_s5_b_re, m_s5_b_im=m_s5_b_im, m_s5_c_re=m_s5_c_re, m_s5_c_im=m_s5_c_im, m_s5_d=m_s5_d, m_s5_w_glu=m_s5_w_glu, m_s5_b_glu=m_s5_b_glu, m_rk_shift_prev=m_rk_shift_prev, m_rk_shift_next=m_rk_shift_next, m_rk_w0=m_rk_w0, m_rk_w_up=m_rk_w_up, m_rk_a0=m_rk_a0, m_rk_a_up=m_rk_a_up, m_rk_g_up=m_rk_g_up, m_rk_k_k=m_rk_k_k, m_rk_k_a=m_rk_k_a, m_rk_r_k=m_rk_r_k, m_rk_ln_gain=m_rk_ln_gain, m_rk_ln_bias=m_rk_ln_bias, m_ffn_w1=m_ffn_w1, m_ffn_w2=m_ffn_w2, v_ada_w=v_ada_w, v_ada_b=v_ada_b, v_norm1_gain=v_norm1_gain, v_norm2_gain=v_norm2_gain, v_final_gain=v_final_gain, v_w_in=v_w_in, v_w_out=v_w_out, v_s5_lambda_re=v_s5_lambda_re, v_s5_lambda_im=v_s5_lambda_im, v_s5_log_step=v_s5_log_step, v_s5_b_re=v_s5_b_re, v_s5_b_im=v_s5_b_im, v_s5_c_re=v_s5_c_re, v_s5_c_im=v_s5_c_im, v_s5_d=v_s5_d, v_s5_w_glu=v_s5_w_glu, v_s5_b_glu=v_s5_b_glu, v_rk_shift_prev=v_rk_shift_prev, v_rk_shift_next=v_rk_shift_next, v_rk_w0=v_rk_w0, v_rk_w_up=v_rk_w_up, v_rk_a0=v_rk_a0, v_rk_a_up=v_rk_a_up, v_rk_g_up=v_rk_g_up, v_rk_k_k=v_rk_k_k, v_rk_k_a=v_rk_k_a, v_rk_r_k=v_rk_r_k, v_rk_ln_gain=v_rk_ln_gain, v_rk_ln_bias=v_rk_ln_bias, v_ffn_w1=v_ffn_w1, v_ffn_w2=v_ffn_w2)
    weights = {n: given[n] for n in TWIN_WEIGHTS}
    shared = {n: given[n] for n in SHARED_INPUTS}
    per_example = {n: given[n] for n in ['x', 'c']}
    grad_fn = _jax.value_and_grad(_loss, argnums=(0, 1))

    def one_microbatch(ex, loss_target):
        ex = dict(ex)
        diff = ex.pop(TWIN_DIFF_INPUT)
        return grad_fn(weights, diff, {**shared, **ex}, loss_target)

    if N_MICROBATCH == 1:
        loss, (grad_w, grad_x) = one_microbatch(per_example, given["loss_target"])
    else:
        def body(carry, xs):
            loss_sum, grad_sum = carry
            l_k, (gw_k, gx_k) = one_microbatch(xs[0], xs[1])
            with _jax.named_scope("update"):
                return (loss_sum + l_k, _jax.tree.map(_jnp.add, grad_sum, gw_k)), gx_k

        init = (_jnp.zeros((), _jnp.float32), _jax.tree.map(_jnp.zeros_like, weights))
        (loss, grad_w), grad_x = _jax.lax.scan(body, init, (per_example, given["loss_target"]))
    with _jax.named_scope("update"):
        delta_w, new_m, new_v = {}, {}, {}
        for n in TWIN_WEIGHTS:
            delta_w[n], new_m[n], new_v[n] = _adamw(weights[n], grad_w[n], given["m_" + n], given["v_" + n])
    return (loss, grad_x, *[grad_w[n] for n in TWIN_WEIGHTS], *[delta_w[n] for n in TWIN_WEIGHTS],
            *[new_m[n] for n in TWIN_WEIGHTS], *[new_v[n] for n in TWIN_WEIGHTS])
```

```python
import functools
import math

import jax
import jax.numpy as jnp
from jax import lax
from jax.experimental import pallas as pl
from jax.experimental.pallas import tpu as pltpu

F32 = jnp.float32
BF16 = jnp.bfloat16

D_MODEL = 2048
S5_WIDTH = 1024
S5_GROUP = 16
S5_GROUPS = 64
S5_STATE = 64
S5_CH = S5_GROUPS * S5_STATE
S5_BLK = 256
RK_WIDTH = 1024
RK_HEAD = 64
RK_HEADS = 16
LORA = 64
GATE_LORA = 160
GATE_PAD = 256
RK_IN = 3488
RK_PAD = 3584
PROJ = 4512
PROJ_PAD = 4608
FFN = 8192
N_MOD = 6
NORM_EPS = 1e-6
GN_EPS = 64e-5
L2_EPS = 1e-12
RK_CHUNK = 64
LW_SCALE = math.exp(-0.5)
ADAM_LR, ADAM_B1, ADAM_B2, ADAM_EPS, ADAM_WD, ADAM_STEP = 0.001, 0.9, 0.999, 1e-08, 0.01, 10
VMEM_LIMIT = 56 * 1024 * 1024
HI = lax.Precision.HIGHEST


def _params(sem=None):
    return pltpu.CompilerParams(dimension_semantics=sem, vmem_limit_bytes=VMEM_LIMIT)


def _full(a):
    nd = a.ndim
    return pl.BlockSpec(a.shape, lambda *_: (0,) * nd)


@jax.custom_vjp
def _bdot(a, b):
    return jnp.dot(a.astype(BF16), b.astype(BF16), preferred_element_type=F32)


def _bdot_fwd(a, b):
    return _bdot(a, b), (a, b)


def _bdot_bwd(res, g):
    a, b = res
    gb = g.astype(BF16)
    da = lax.dot_general(gb, b.astype(BF16), (((1,), (1,)), ((), ())), preferred_element_type=F32)
    db = lax.dot_general(a.astype(BF16), gb, (((0,), (0,)), ((), ())), preferred_element_type=F32)
    return da, db


_bdot.defvjp(_bdot_fwd, _bdot_bwd)


def _fdot(a, b):
    return jnp.dot(a, b, precision=HI, preferred_element_type=F32)


def _sigmoid(z):
    return 1.0 / (1.0 + jnp.exp(-z))


def _gelu(y):
    return 0.5 * y * (1.0 + jnp.tanh(0.7978845608028654 * (y + 0.044715 * (y * y * y))))


def _rms(x):
    return x * lax.rsqrt(jnp.mean(x * x, axis=-1, keepdims=True) + NORM_EPS)


def _tile(n, prefs):
    for t in prefs:
        if n % t == 0:
            return t
    return n


def _matmul(name, a, b, ta=False, tb=False, epilogue=None, extras=(), out_dtypes=(F32,)):
    m = a.shape[1] if ta else a.shape[0]
    k = a.shape[0] if ta else a.shape[1]
    n = b.shape[0] if tb else b.shape[1]
    assert k == (b.shape[1] if tb else b.shape[0]), (a.shape, b.shape, ta, tb)
    tm = _tile(m, (1024, 512, 256, 128))
    tn = _tile(n, (1024, 768, 512, 256, 128))
    tk = _tile(k, (512, 256, 128))
    nk = k // tk
    n_ex, n_out = len(extras), len(out_dtypes)
    dims = (((0 if ta else 1,), (1 if tb else 0,)), ((), ()))

    def body(a_ref, b_ref, *rest):
        ex_refs, out_refs, acc = rest[:n_ex], rest[n_ex:n_ex + n_out], rest[-1]
        kk = pl.program_id(2)

        @pl.when(kk == 0)
        def _():
            acc[...] = jnp.zeros_like(acc)

        acc[...] += lax.dot_general(a_ref[...].astype(BF16), b_ref[...].astype(BF16), dims,
                                    preferred_element_type=F32)

        @pl.when(kk == nk - 1)
        def _():
            res = acc[...]
            outs = epilogue(res, *[e[...] for e in ex_refs]) if epilogue is not None else (res,)
            for o_ref, val in zip(out_refs, outs):
                o_ref[...] = val.astype(o_ref.dtype)

    a_spec = pl.BlockSpec((tk, tm), lambda i, j, q: (q, i)) if ta else pl.BlockSpec((tm, tk), lambda i, j, q: (i, q))
    b_spec = pl.BlockSpec((tn, tk), lambda i, j, q: (j, q)) if tb else pl.BlockSpec((tk, tn), lambda i, j, q: (q, j))
    mn_spec = pl.BlockSpec((tm, tn), lambda i, j, q: (i, j))
    outs = pl.pallas_call(
        body, name=name, grid=(m // tm, n // tn, nk),
        in_specs=[a_spec, b_spec] + [mn_spec] * n_ex,
        out_specs=[mn_spec] * n_out,
        out_shape=[jax.ShapeDtypeStruct((m, n), dt) for dt in out_dtypes],
        scratch_shapes=[pltpu.VMEM((tm, tn), F32)],
        compiler_params=_params(("parallel", "parallel", "arbitrary")),
    )(a, b, *extras)
    return outs[0] if n_out == 1 else outs


def _row_spec(a, tm):
    return pl.BlockSpec((tm, a.shape[1]), lambda i: (i, 0))


def _rowwise(name, fn, rows, params, outs, tm):
    t = rows[0].shape[0]
    tm = min(tm, t)
    n_r, n_p = len(rows), len(params)

    def body(*refs):
        vals = [r[...] for r in refs[:n_r + n_p]]
        res = fn(*vals)
        for o_ref, val in zip(refs[n_r + n_p:], res):
            o_ref[...] = val.astype(o_ref.dtype)

    res = pl.pallas_call(
        body, name=name, grid=(t // tm,),
        in_specs=[_row_spec(r, tm) for r in rows] + [_full(p) for p in params],
        out_specs=[pl.BlockSpec((tm, n), lambda i: (i, 0)) for n, _ in outs],
        out_shape=[jax.ShapeDtypeStruct((t, n), dt) for n, dt in outs],
        compiler_params=_params(("parallel",)),
    )(*rows, *params)
    return res


def _rowwise_vjp(name, fn, rows, params, cts, row_grads, param_grads, tm, consts=(), addends=None,
                 emit=(), row_grad_dtypes=None):
    t = rows[0].shape[0]
    tm = min(tm, t)
    addends = addends or {}
    n_r, n_p, n_c = len(rows), len(params), len(consts)
    ct_flat = [c for group in cts for c in group]
    add_list = [addends[q] for q in sorted(addends)]
    n_ct, n_add = len(ct_flat), len(add_list)
    row_grad_dtypes = row_grad_dtypes or [F32] * len(row_grads)

    def body(*refs):
        pos = 0
        row_v = [r[...].astype(F32) for r in refs[pos:pos + n_r]]; pos += n_r
        par_v = [r[...].astype(F32) for r in refs[pos:pos + n_p]]; pos += n_p
        con_v = [r[...] for r in refs[pos:pos + n_c]]; pos += n_c
        ct_v = [r[...].astype(F32) for r in refs[pos:pos + n_ct]]; pos += n_ct
        add_v = [r[...] for r in refs[pos:pos + n_add]]; pos += n_add
        emit_refs = refs[pos:pos + len(emit)]; pos += len(emit)
        rg_refs = refs[pos:pos + len(row_grads)]; pos += len(row_grads)
        pg_refs = refs[pos:pos + len(param_grads)]

        def diff_fn(*dargs):
            rv, pv = list(row_v), list(par_v)
            for q, i in enumerate(row_grads):
                rv[i] = dargs[q]
            for q, j in enumerate(param_grads):
                pv[j] = dargs[len(row_grads) + q]
            return fn(*rv, *pv, *con_v)

        prim = [row_v[i] for i in row_grads] + [par_v[j] for j in param_grads]
        res, vjp = jax.vjp(diff_fn, *prim)
        ct_vals, q = [], 0
        for o, group in zip(res, cts):
            tot = jnp.zeros_like(o)
            for _ in group:
                tot = tot + ct_v[q]
                q += 1
            ct_vals.append(tot)
        grads = vjp(tuple(ct_vals))
        for e_ref, idx in zip(emit_refs, emit):
            e_ref[...] = res[idx].astype(e_ref.dtype)
        add_pos = {p: q for q, p in enumerate(sorted(addends))}
        for q, g_ref in enumerate(rg_refs):
            g = grads[q]
            if q in add_pos:
                g = g + add_v[add_pos[q]]
            g_ref[...] = g.astype(g_ref.dtype)

        @pl.when(pl.program_id(0) == 0)
        def _():
            for g_ref in pg_refs:
                g_ref[...] = jnp.zeros_like(g_ref)

        for q, g_ref in enumerate(pg_refs):
            g_ref[...] += grads[len(row_grads) + q]

    emit_shapes = []
    if emit:
        probe = jax.eval_shape(lambda *a: fn(*a), *[jax.ShapeDtypeStruct((tm, r.shape[1]), F32) for r in rows],
                               *[jax.ShapeDtypeStruct(p.shape, p.dtype) for p in params],
                               *[jax.ShapeDtypeStruct(c.shape, c.dtype) for c in consts])
        emit_shapes = [probe[idx].shape[1] for idx in emit]
    out_specs = ([pl.BlockSpec((tm, n), lambda i: (i, 0)) for n in emit_shapes]
                 + [_row_spec(rows[i], tm) for i in row_grads]
                 + [_full(params[j]) for j in param_grads])
    out_shape = ([jax.ShapeDtypeStruct((t, n), F32) for n in emit_shapes]
                 + [jax.ShapeDtypeStruct(rows[i].shape, dt) for i, dt in zip(row_grads, row_grad_dtypes)]
                 + [jax.ShapeDtypeStruct(params[j].shape, F32) for j in param_grads])
    return pl.pallas_call(
        body, name=name, grid=(t // tm,),
        in_specs=([_row_spec(r, tm) for r in rows] + [_full(p) for p in params] + [_full(c) for c in consts]
                  + [_row_spec(c, tm) for c in ct_flat] + [_row_spec(a, tm) for a in add_list]),
        out_specs=out_specs, out_shape=out_shape,
        compiler_params=_params(("arbitrary",)),
    )(*rows, *params, *consts, *ct_flat, *add_list)


def _norm_mod_fn(x, gain, scale, shift):
    return (_rms(x) * gain * (1.0 + scale) + shift,)


def _resid_norm_mod_fn(x, mixed, gate, gain, scale, shift):
    x1 = x + gate * mixed
    return x1, _rms(x1) * gain * (1.0 + scale) + shift


def _loss_fn(x1, ffn, target, gate, gain):
    y = _rms(x1 + gate * ffn) * gain
    err = y - target
    return (0.5 * jnp.mean(err * err, axis=-1, keepdims=True),)


def _s5_out_fn(ylin, u, d_skip, w_glu, b_glu):
    z = _gelu(ylin + d_skip * u)
    return (z * _sigmoid(_bdot(z, w_glu) + b_glu),)


def _rk_pre_fn(k, wdn, adn, gdn, w0_0, w0_1, wup_0, wup_1, a0_0, a0_1, aup_0, aup_1, g_up, k_k, k_a, seg, seg_t):
    kkr = k * k_k
    inv = 1.0 / jnp.sqrt(jnp.maximum(_fdot(kkr * kkr, seg), L2_EPS * L2_EPS))
    kk = kkr * _fdot(inv, seg_t)
    tw = jnp.tanh(wdn)
    lws, kds, acts = [], [], []
    for w0, wup, a0, aup in ((w0_0, wup_0, a0_0, aup_0), (w0_1, wup_1, a0_1, aup_1)):
        lws.append(-LW_SCALE * _sigmoid(w0 + _bdot(tw, wup)))
        act = _sigmoid(a0 + _bdot(adn, aup))
        acts.append(act)
        kds.append(k * (1.0 + (act - 1.0) * k_a))
    gate = _bdot(_sigmoid(gdn), g_up)
    return (kk, lws[0], lws[1], kds[0], kds[1], acts[0], acts[1], gate)


def _rk_post_fn(y0, y1, r, v, kd0, kd1, gate, ln_gain, ln_bias, r_k, seg, seg_t):
    y = y0 + y1
    mu = _fdot(_fdot(y, seg) * (1.0 / RK_HEAD), seg_t)
    yc = y - mu
    var = _fdot(yc * yc, seg) * (1.0 / RK_HEAD)
    yn = yc * _fdot(lax.rsqrt(var + GN_EPS), seg_t) * ln_gain + ln_bias
    bonus = _fdot(_fdot(r * (kd0 + kd1) * r_k, seg), seg_t)
    return ((yn + bonus * v) * gate,)


def _s5_prep_fn(lr0, li0, ls0, lr1, li1, ls1, b_re, b_im):
    outs = []
    for lam_re, lam_im, ls in ((lr0, li0, ls0), (lr1, li1, ls1)):
        step = jnp.exp(ls)
        mag = jnp.exp(lam_re * step)
        lbar_re = mag * jnp.cos(lam_im * step)
        lbar_im = mag * jnp.sin(lam_im * step)
        den = lam_re * lam_re + lam_im * lam_im
        nr = lbar_re - 1.0
        coef_re = (nr * lam_re + lbar_im * lam_im) / den
        coef_im = (lbar_im * lam_re - nr * lam_im) / den
        outs += [lbar_re, lbar_im, coef_re * b_re - coef_im * b_im, coef_re * b_im + coef_im * b_re]
    return tuple(outs)


def _shift_rows(x, down):
    t = x.shape[0]
    rows = lax.broadcasted_iota(jnp.int32, x.shape, 0)
    if down:
        return jnp.where(rows >= 1, pltpu.roll(x, 1, 0), 0.0)
    return jnp.where(rows < t - 1, pltpu.roll(x, t - 1, 0), 0.0)


def _token_shift(p, mu_prev, mu_next):
    t, n = p.shape

    def body(p_ref, mp_ref, mn_ref, o_ref):
        x = p_ref[...]
        o_ref[...] = x + mp_ref[...] * (_shift_rows(x, True) - x) + mn_ref[...] * (_shift_rows(x, False) - x)

    col = pl.BlockSpec((t, 128), lambda j: (0, j))
    par = pl.BlockSpec((1, 128), lambda j: (0, j))
    return pl.pallas_call(
        body, name="token_shift", grid=(n // 128,), in_specs=[col, par, par], out_specs=col,
        out_shape=jax.ShapeDtypeStruct((t, n), F32), compiler_params=_params(("parallel",)),
    )(p, mu_prev, mu_next)


def _token_shift_bwd(p, mu_prev, mu_next, dps):
    t, n = p.shape

    def body(p_ref, mp_ref, mn_ref, d_ref, dp_ref, dmp_ref, dmn_ref):
        x, d, mp, mn = p_ref[...], d_ref[...], mp_ref[...], mn_ref[...]
        dp_ref[...] = d * (1.0 - mp - mn) + _shift_rows(d * mp, False) + _shift_rows(d * mn, True)
        dmp_ref[...] = jnp.sum(d * (_shift_rows(x, True) - x), axis=0, keepdims=True)
        dmn_ref[...] = jnp.sum(d * (_shift_rows(x, False) - x), axis=0, keepdims=True)

    col = pl.BlockSpec((t, 128), lambda j: (0, j))
    par = pl.BlockSpec((1, 128), lambda j: (0, j))
    return pl.pallas_call(
        body, name="token_shift_bwd", grid=(n // 128,), in_specs=[col, par, par, col],
        out_specs=[col, par, par],
        out_shape=[jax.ShapeDtypeStruct((t, n), F32), jax.ShapeDtypeStruct((1, n), F32),
                   jax.ShapeDtypeStruct((1, n), F32)],
        compiler_params=_params(("parallel",)),
    )(p, mu_prev, mu_next, dps)


N_SEG = 8
SCAN_COLS = 1


def _s5_scan(name, b_re, b_im, l_re, l_im, reverse, conj=False):
    t, n = b_re.shape
    seg_len = t // N_SEG
    nq = SCAN_COLS

    def body(*refs):
        br_refs, bi_refs = refs[0:nq], refs[nq:2 * nq]
        lr_refs, li_refs = refs[2 * nq:3 * nq], refs[3 * nq:4 * nq]
        sr_refs, si_refs = refs[4 * nq:5 * nq], refs[5 * nq:6 * nq]
        carry_ref = refs[-1]

        def rows(i):
            return pl.ds(seg_len - 1 - i if reverse else i, N_SEG, stride=seg_len)

        for q in range(nq):
            ar = jnp.broadcast_to(lr_refs[q][...], (N_SEG, 128))
            ai = jnp.broadcast_to(li_refs[q][...], (N_SEG, 128))
            if conj:
                ai = -ai
            zero = jnp.zeros((N_SEG, 128), F32)
            one = jnp.ones((N_SEG, 128), F32)
            br_ref, bi_ref, sr_ref, si_ref = br_refs[q], bi_refs[q], sr_refs[q], si_refs[q]

            def local(i, c, ar=ar, ai=ai, br_ref=br_ref, bi_ref=bi_ref, sr_ref=sr_ref, si_ref=si_ref):
                sr, si, pr, pi = c
                nr = ar * sr - ai * si + br_ref[rows(i), :]
                ni = ar * si + ai * sr + bi_ref[rows(i), :]
                sr_ref[rows(i), :] = nr
                si_ref[rows(i), :] = ni
                return nr, ni, ar * pr - ai * pi, ar * pi + ai * pr

            er, ei, qr, qi = lax.fori_loop(0, seg_len, local, (zero, zero, one, zero))
            order = list(range(N_SEG - 1, -1, -1)) if reverse else list(range(N_SEG))
            cr = jnp.zeros((1, 128), F32)
            ci = jnp.zeros((1, 128), F32)
            for j in order:
                carry_ref[j:j + 1, :] = cr
                carry_ref[N_SEG + j:N_SEG + j + 1, :] = ci
                lr, li = er[j:j + 1], ei[j:j + 1]
                cr, ci = lr + qr[0:1] * cr - qi[0:1] * ci, li + qr[0:1] * ci + qi[0:1] * cr
            cr8, ci8 = carry_ref[0:N_SEG, :], carry_ref[N_SEG:2 * N_SEG, :]

            def fix(i, c, ar=ar, ai=ai, sr_ref=sr_ref, si_ref=si_ref, cr8=cr8, ci8=ci8):
                pr, pi = c
                npr, npi = ar * pr - ai * pi, ar * pi + ai * pr
                sr_ref[rows(i), :] = sr_ref[rows(i), :] + npr * cr8 - npi * ci8
                si_ref[rows(i), :] = si_ref[rows(i), :] + npr * ci8 + npi * cr8
                return npr, npi

            lax.fori_loop(0, seg_len, fix, (one, zero))

    def cols(q):
        return pl.BlockSpec((t, 128), lambda j: (0, nq * j + q))

    def pars(q):
        return pl.BlockSpec((1, 128), lambda j: (0, nq * j + q))

    col_specs = [cols(q) for q in range(nq)]
    par_specs = [pars(q) for q in range(nq)]
    res = pl.pallas_call(
        body, name=name, grid=(n // (128 * nq),),
        in_specs=col_specs * 2 + par_specs * 2,
        out_specs=col_specs * 2,
        out_shape=[jax.ShapeDtypeStruct((t, n), F32)] * 2,
        scratch_shapes=[pltpu.VMEM((2 * N_SEG, 128), F32)],
        compiler_params=_params(("parallel",)),
    )(*([b_re] * nq + [b_im] * nq + [l_re] * nq + [l_im] * nq))
    return res


def _bt(x):
    return jnp.swapaxes(x, 1, 2)


@jax.custom_vjp
def _bmm(a, b):
    return jnp.einsum('hik,hkj->hij', a, b, precision=HI, preferred_element_type=F32)


def _bmm_fwd(a, b):
    return _bmm(a, b), (a, b)


def _bmm_bwd(res, g):
    a, b = res
    return _bmm(g, _bt(b)), _bmm(_bt(a), g)


_bmm.defvjp(_bmm_fwd, _bmm_bwd)


def _rk_chunk(s0, r, lw, k, v, kk, a, reverse):
    h, c, n = r.shape
    row = lax.broadcasted_iota(jnp.int32, (c, c), 0)
    col = lax.broadcasted_iota(jnp.int32, (c, c), 1)
    incl = (row <= col) if reverse else (row >= col)
    strict = (row < col) if reverse else (row > col)
    cum = _bmm(jnp.broadcast_to(incl.astype(F32), (h, c, c)), lw)
    g_in = jnp.exp(cum)
    g_inv = jnp.exp(-cum)
    kap = kk * jnp.exp(cum - lw)
    beta = kk * a * g_inv
    kt = k * g_inv
    rt = r * g_in
    l_mat = jnp.where(strict, _bmm(kap, _bt(beta)), 0.0)
    a_kk = jnp.where(strict, _bmm(kap, _bt(kt)), 0.0)
    rhs = _bmm(kap, _bt(s0)) + _bmm(a_kk, v)
    x = -l_mat
    inv = jnp.where(row == col, 1.0, 0.0) + x
    span = 1
    while 2 * span < c:
        x = _bmm(x, x)
        inv = inv + _bmm(inv, x)
        span *= 2
    u = _bmm(inv, rhs)
    a_rk = jnp.where(incl, _bmm(rt, _bt(kt)), 0.0)
    a_rb = jnp.where(incl, _bmm(rt, _bt(beta)), 0.0)
    y = _bmm(rt, _bt(s0)) + _bmm(a_rk, v) - _bmm(a_rb, u)
    s1 = (s0 + _bmm(_bt(v), kt) - _bmm(_bt(u), beta)) * jnp.exp(jnp.sum(lw, axis=1, keepdims=True))
    return y, s1


def _rk_core_fwd(name, r, lw, k, v, kk, a, reverse, chunk):
    h, t, n = r.shape
    nc = t // chunk

    def idx(i):
        return nc - 1 - i if reverse else i

    def body(r_ref, lw_ref, k_ref, v_ref, kk_ref, a_ref, y_ref, ck_ref, s_ref):
        @pl.when(pl.program_id(0) == 0)
        def _():
            s_ref[...] = jnp.zeros_like(s_ref)

        s0 = s_ref[...]
        ck_ref[0] = s0
        y, s1 = _rk_chunk(s0, r_ref[...], lw_ref[...], k_ref[...], v_ref[...], kk_ref[...], a_ref[...], reverse)
        y_ref[...] = y
        s_ref[...] = s1

    blk = pl.BlockSpec((h, chunk, n), lambda i: (0, idx(i), 0))
    return pl.pallas_call(
        body, name=name, grid=(nc,), in_specs=[blk] * 6,
        out_specs=[blk, pl.BlockSpec((1, h, n, n), lambda i: (idx(i), 0, 0, 0))],
        out_shape=[jax.ShapeDtypeStruct((h, t, n), F32), jax.ShapeDtypeStruct((nc, h, n, n), F32)],
        scratch_shapes=[pltpu.VMEM((h, n, n), F32)],
        compiler_params=_params(("arbitrary",)),
    )(r, lw, k, v, kk, a)


def _rk_core_bwd(name, r, lw, k, v, kk, a, ck, dy, reverse, chunk):
    h, t, n = r.shape
    nc = t // chunk

    def idx(i):
        return i if reverse else nc - 1 - i

    def body(r_ref, lw_ref, k_ref, v_ref, kk_ref, a_ref, ck_ref, dy_ref, *rest):
        out_refs, ds_ref = rest[:6], rest[6]

        @pl.when(pl.program_id(0) == 0)
        def _():
            ds_ref[...] = jnp.zeros_like(ds_ref)

        fn = functools.partial(_rk_chunk, reverse=reverse)
        _, vjp = jax.vjp(fn, ck_ref[0], r_ref[...], lw_ref[...], k_ref[...], v_ref[...], kk_ref[...], a_ref[...])
        grads = vjp((dy_ref[...], ds_ref[...]))
        ds_ref[...] = grads[0]
        for o_ref, g in zip(out_refs, grads[1:]):
            o_ref[...] = g

    blk = pl.BlockSpec((h, chunk, n), lambda i: (0, idx(i), 0))
    return pl.pallas_call(
        body, name=name, grid=(nc,),
        in_specs=[blk] * 6 + [pl.BlockSpec((1, h, n, n), lambda i: (idx(i), 0, 0, 0)), blk],
        out_specs=[blk] * 6,
        out_shape=[jax.ShapeDtypeStruct((h, t, n), F32)] * 6,
        scratch_shapes=[pltpu.VMEM((h, n, n), F32)],
        compiler_params=_params(("arbitrary",)),
    )(r, lw, k, v, kk, a, ck, dy)


def _s5_dlbar(name, lam_re, lam_im, s_re, s_im, prev_is_down):
    t, n = lam_re.shape
    w = 512

    def body(lr_ref, li_ref, sr_ref, si_ref, or_ref, oi_ref):
        sr, si = _shift_rows(sr_ref[...], prev_is_down), _shift_rows(si_ref[...], prev_is_down)
        lr, li = lr_ref[...], li_ref[...]
        or_ref[...] = jnp.sum(lr * sr + li * si, axis=0, keepdims=True)
        oi_ref[...] = jnp.sum(li * sr - lr * si, axis=0, keepdims=True)

    col = pl.BlockSpec((t, w), lambda j: (0, j))
    par = pl.BlockSpec((1, w), lambda j: (0, j))
    return pl.pallas_call(
        body, name=name, grid=(n // w,), in_specs=[col] * 4, out_specs=[par, par],
        out_shape=[jax.ShapeDtypeStruct((1, n), F32)] * 2, compiler_params=_params(("parallel",)),
    )(lam_re, lam_im, s_re, s_im)


def _to_heads(x):
    return jnp.transpose(x.reshape(x.shape[0], RK_HEADS, RK_HEAD), (1, 0, 2))


def _from_heads(x):
    return jnp.transpose(x, (1, 0, 2)).reshape(x.shape[1], RK_WIDTH)


def _s5_in_dense(bbar):
    b = jnp.transpose(bbar.reshape(S5_GROUPS, S5_STATE, S5_GROUP), (0, 2, 1))
    return jnp.einsum('ghp,gk->ghkp', b, jnp.eye(S5_GROUPS, dtype=bbar.dtype)).reshape(S5_WIDTH, S5_CH)


def _s5_in_diag(dense):
    d = dense.reshape(S5_GROUPS, S5_GROUP, S5_GROUPS, S5_STATE)
    return jnp.transpose(jnp.einsum('ghgp->ghp', d), (0, 2, 1)).reshape(S5_CH, S5_GROUP)


def _s5_out_dense(c):
    ct = jnp.transpose(c, (0, 2, 1))
    return jnp.einsum('gph,gk->gpkh', ct, jnp.eye(S5_GROUPS, dtype=c.dtype)).reshape(S5_CH, S5_WIDTH)


def _s5_out_diag(dense):
    d = dense.reshape(S5_GROUPS, S5_STATE, S5_GROUPS, S5_GROUP)
    return jnp.transpose(jnp.einsum('gpgh->gph', d), (0, 2, 1))


def _head_indicator():
    ch = lax.broadcasted_iota(jnp.int32, (RK_WIDTH, 128), 0) // RK_HEAD
    hd = lax.broadcasted_iota(jnp.int32, (RK_WIDTH, 128), 1)
    seg = (ch == hd).astype(F32)
    return seg, seg.T


def _add_epilogue(acc, e):
    return (acc + e,)


def _local_step(x, target, mod, wt, chunk=RK_CHUNK):
    t = x.shape[0]
    sh1, sc1, gt1, sh2, sc2, gt2 = mod
    seg, seg_t = _head_indicator()
    g = {}

    (h1,) = _rowwise("norm1", _norm_mod_fn, [x], [wt["norm1_gain"], sc1, sh1], [(D_MODEL, BF16)], 256)
    proj = _matmul("proj", h1, wt["w_in"])
    u, p = proj[:, :S5_WIDTH], proj[:, S5_WIDTH:]
    ps = _token_shift(p, wt["mu_prev"], wt["mu_next"])
    r, k, v = ps[:, :1024], ps[:, 1024:2048], ps[:, 2048:3072]
    wdn, adn, gdn = ps[:, 3072:3200], ps[:, 3200:3328], ps[:, 3328:RK_PAD]

    prep_rows = [wt["lam_re"][0], wt["lam_im"][0], wt["log_step"][0], wt["lam_re"][1], wt["lam_im"][1],
                 wt["log_step"][1], wt["b_re"], wt["b_im"]]
    col1, col16 = (1, F32), (S5_GROUP, F32)
    prep = _rowwise("s5_prep", _s5_prep_fn, prep_rows, [], [col1, col1, col16, col16] * 2, 512)
    lbar = [(prep[4 * d].reshape(1, S5_CH), prep[4 * d + 1].reshape(1, S5_CH)) for d in range(2)]
    b_dense = [(_s5_in_dense(prep[4 * d + 2]), _s5_in_dense(prep[4 * d + 3])) for d in range(2)]
    c_dense = (_s5_out_dense(wt["c_re"]), -_s5_out_dense(wt["c_im"]))
    states = []
    for d in range(2):
        bu_re = _matmul(f"s5_bu_re{d}", u, b_dense[d][0])
        bu_im = _matmul(f"s5_bu_im{d}", u, b_dense[d][1])
        states.append(_s5_scan(f"s5_scan{d}", bu_re, bu_im, lbar[d][0], lbar[d][1], reverse=(d == 1)))
    xs_re, xs_im = _rowwise("s5_sum", lambda a, b, c, e: (a + c, b + e),
                            [states[0][0], states[0][1], states[1][0], states[1][1]], [], [(S5_CH, F32)] * 2, 256)
    ylin = _matmul("s5_c_re", xs_re, c_dense[0])
    ylin = _matmul("s5_c_im", xs_im, c_dense[1], epilogue=_add_epilogue, extras=(ylin,))
    s5_par = [wt["s5_d"], wt["s5_w_glu"], wt["s5_b_glu"]]
    (o_s5,) = _rowwise("s5_out", _s5_out_fn, [ylin, u], s5_par, [(S5_WIDTH, BF16)], 256)

    pre_par = [wt["w0"][0], wt["w0"][1], wt["w_up"][0], wt["w_up"][1], wt["a0"][0], wt["a0"][1],
               wt["a_up"][0], wt["a_up"][1], wt["g_up"], wt["k_k"], wt["k_a"]]
    pre = _rowwise("rk_pre", _rk_pre_fn, [k, wdn, adn, gdn], pre_par + [seg, seg_t], [(RK_WIDTH, F32)] * 8, 256)
    kk, lw, kd, act, gate = pre[0], pre[1:3], pre[3:5], pre[5:7], pre[7]
    rh, vh, kkh = _to_heads(r), _to_heads(v), _to_heads(kk)
    core_in, ys, cks = [], [], []
    for d in range(2):
        ops = (rh, _to_heads(lw[d]), _to_heads(kd[d]), vh, kkh, _to_heads(act[d]))
        y_h, ck = _rk_core_fwd(f"rk_core{d}", *ops, reverse=(d == 1), chunk=min(chunk, t))
        core_in.append(ops)
        ys.append(_from_heads(y_h))
        cks.append(ck)
    post_rows = [ys[0], ys[1], r, v, kd[0], kd[1], gate]
    post_par = [wt["ln_gain"], wt["ln_bias"], wt["r_k"]]
    (o_rk,) = _rowwise("rk_post", _rk_post_fn, post_rows, post_par + [seg, seg_t], [(RK_WIDTH, BF16)], 256)

    o = jnp.concatenate([o_s5, o_rk], axis=1)
    mixed = _matmul("mix_out", o, wt["w_out"])
    n2_par = [gt1, wt["norm2_gain"], sc2, sh2]
    x1, h2 = _rowwise("norm2", _resid_norm_mod_fn, [x, mixed], n2_par, [(D_MODEL, F32), (D_MODEL, BF16)], 256)
    f1, hid = _matmul("ffn1", h2, wt["ffn_w1"], out_dtypes=(F32, BF16),
                      epilogue=lambda acc: (acc, jnp.square(jnp.maximum(acc, 0.0))))
    ffn = _matmul("ffn2", hid, wt["ffn_w2"])

    ones = jnp.ones((t, 1), F32)
    loss_rows, dx1, dffn, g_gt2, g["final_gain"] = _rowwise_vjp(
        "loss", _loss_fn, [x1, ffn, target], [gt2, wt["final_gain"]], [[ones]], [0, 1], [0, 1], 256, emit=(0,))
    df1 = _matmul("ffn2_dx", dffn, wt["ffn_w2"], tb=True, extras=(f1,), out_dtypes=(BF16,),
                  epilogue=lambda acc, f: (acc * (2.0 * jnp.maximum(f, 0.0)),))
    g["ffn_w2"] = _matmul("ffn2_dw", hid, dffn, ta=True)
    dh2 = _matmul("ffn1_dx", df1, wt["ffn_w1"], tb=True)
    g["ffn_w1"] = _matmul("ffn1_dw", h2, df1, ta=True)
    dx_a, dmixed, g_gt1, g["norm2_gain"], g_sc2, g_sh2 = _rowwise_vjp(
        "norm2_bwd", _resid_norm_mod_fn, [x, mixed], n2_par, [[dx1], [dh2]], [0, 1], [0, 1, 2, 3], 256)
    do = _matmul("mix_out_dx", dmixed, wt["w_out"], tb=True)
    g["w_out"] = _matmul("mix_out_dw", o, dmixed, ta=True)
    do_s5, do_rk = do[:, :S5_WIDTH], do[:, S5_WIDTH:]

    dylin, du, g["s5_d"], g["s5_w_glu"], g["s5_b_glu"] = _rowwise_vjp(
        "s5_out_bwd", _s5_out_fn, [ylin, u], s5_par, [[do_s5]], [0, 1], [0, 1, 2], 256)
    dxs_re = _matmul("s5_c_re_dx", dylin, c_dense[0], tb=True)
    dxs_im = _matmul("s5_c_im_dx", dylin, c_dense[1], tb=True)
    g["c_re"] = _s5_out_diag(_matmul("s5_c_re_dw", xs_re, dylin, ta=True))
    g["c_im"] = -_s5_out_diag(_matmul("s5_c_im_dw", xs_im, dylin, ta=True))
    prep_cts = []
    for d in range(2):
        lam_re, lam_im = _s5_scan(f"s5_adj{d}", dxs_re, dxs_im, lbar[d][0], lbar[d][1], reverse=(d == 0), conj=True)
        dl_re, dl_im = _s5_dlbar(f"s5_dlbar{d}", lam_re, lam_im, states[d][0], states[d][1], prev_is_down=(d == 0))
        du = _matmul(f"s5_bu_re{d}_dx", lam_re, b_dense[d][0], tb=True, epilogue=_add_epilogue, extras=(du,))
        du = _matmul(f"s5_bu_im{d}_dx", lam_im, b_dense[d][1], tb=True, epilogue=_add_epilogue, extras=(du,))
        db_re = _s5_in_diag(_matmul(f"s5_bu_re{d}_dw", u, lam_re, ta=True))
        db_im = _s5_in_diag(_matmul(f"s5_bu_im{d}_dw", u, lam_im, ta=True))
        prep_cts += [[dl_re.reshape(S5_CH, 1)], [dl_im.reshape(S5_CH, 1)], [db_re], [db_im]]
    pg = _rowwise_vjp("s5_prep_bwd", _s5_prep_fn, prep_rows, [], prep_cts, list(range(8)), [], 512)
    g["lam_re"], g["lam_im"], g["log_step"] = (pg[0], pg[3]), (pg[1], pg[4]), (pg[2], pg[5])
    g["b_re"], g["b_im"] = pg[6], pg[7]

    pb = _rowwise_vjp("rk_post_bwd", _rk_post_fn, post_rows, post_par, [[do_rk]], [0, 2, 3, 4, 5, 6], [0, 1, 2],
                      128, consts=[seg, seg_t])
    dy, dr_b, dv_b, dkd_b, dgate = pb[0], pb[1], pb[2], pb[3:5], pb[5]
    g["ln_gain"], g["ln_bias"], g["r_k"] = pb[6], pb[7], pb[8]
    dyh = _to_heads(dy)
    cg = []
    for d in range(2):
        grads = _rk_core_bwd(f"rk_core{d}_bwd", *core_in[d], cks[d], dyh, reverse=(d == 1), chunk=min(chunk, t))
        cg.append([_from_heads(q) for q in grads])
    pre_cts = [[cg[0][4], cg[1][4]], [cg[0][1]], [cg[1][1]], [cg[0][2], dkd_b[0]], [cg[1][2], dkd_b[1]],
               [cg[0][5]], [cg[1][5]], [dgate]]
    qb = _rowwise_vjp("rk_pre_bwd", _rk_pre_fn, [k, wdn, adn, gdn], pre_par, pre_cts, [0, 1, 2, 3],
                      list(range(11)), 128, consts=[seg, seg_t])
    dk, dwdn, dadn, dgdn = qb[:4]
    g["w0"], g["w_up"], g["a0"], g["a_up"] = (qb[4], qb[5]), (qb[6], qb[7]), (qb[8], qb[9]), (qb[10], qb[11])
    g["g_up"], g["k_k"], g["k_a"] = qb[12], qb[13], qb[14]
    dr, dv = _rowwise("rk_sum", lambda a, b, c, e, f, h: (a + b + c, e + f + h),
                      [cg[0][0], cg[1][0], dr_b, cg[0][3], cg[1][3], dv_b], [], [(RK_WIDTH, F32)] * 2, 256)
    dps = jnp.concatenate([dr, dk, dv, dwdn, dadn, dgdn], axis=1)
    dp, g["mu_prev"], g["mu_next"] = _token_shift_bwd(p, wt["mu_prev"], wt["mu_next"], dps)

    dproj = jnp.concatenate([du, dp], axis=1)
    dh1 = _matmul("proj_dx", dproj, wt["w_in"], tb=True)
    g["w_in"] = _matmul("proj_dw", h1, dproj, ta=True)
    grad_x, g["norm1_gain"], g_sc1, g_sh1 = _rowwise_vjp(
        "norm1_bwd", _norm_mod_fn, [x], [wt["norm1_gain"], sc1, sh1], [[dh1]], [0], [0, 1, 2], 256,
        addends={0: dx_a})
    g["mod"] = [g_sh1, g_sc1, g_gt1, g_sh2, g_sc2, g_gt2]
    return loss_rows, grad_x, g


CHIP_PEERS = ((1, 0, 0), (0, 1, 0), (1, 1, 0))
ALL_PEERS = ((0, 0, 1), (0, 1, 0), (0, 1, 1), (1, 0, 0), (1, 0, 1), (1, 1, 0), (1, 1, 1))
CORE_PEER = ((0, 0, 1),)


def _exchange(name, arrays, peers, n_slots, scatter=False):
    na, nm = len(arrays), len(peers)

    def ident(px, py, pc):
        return {8: 4 * px + 2 * py + pc, 4: 2 * px + py, 2: pc}[n_slots]

    def body(*refs):
        in_refs, out_refs = refs[:na], refs[na:2 * na]
        send_sems, recv_sems, local_sems = refs[2 * na:]
        x, y, c = lax.axis_index("x"), lax.axis_index("y"), lax.axis_index("c")
        me = ident(x, y, c)
        started = []
        for i in range(na):
            own = in_refs[i].at[me] if scatter else in_refs[i]
            local = pltpu.make_async_copy(own, out_refs[i].at[me], local_sems.at[i])
            local.start()
            started.append(local)
            for j, (fx, fy, fc) in enumerate(peers):
                px, py, pc = (1 - x if fx else x), (1 - y if fy else y), (1 - c if fc else c)
                src = in_refs[i].at[ident(px, py, pc)] if scatter else in_refs[i]
                copy = pltpu.make_async_remote_copy(
                    src_ref=src, dst_ref=out_refs[i].at[me],
                    send_sem=send_sems.at[i * nm + j], recv_sem=recv_sems.at[i * nm + j],
                    device_id=(px, py, pc), device_id_type=pl.DeviceIdType.MESH)
                copy.start()
                started.append(copy)
        for copy in started:
            copy.wait()

    any_spec = pl.BlockSpec(memory_space=pl.ANY)
    out_shape = [jax.ShapeDtypeStruct(((n_slots,) + a.shape[1:]) if scatter else ((n_slots,) + a.shape), a.dtype)
                 for a in arrays]
    return pl.pallas_call(
        body, name=name, in_specs=[any_spec] * na, out_specs=[any_spec] * na, out_shape=out_shape,
        scratch_shapes=[pltpu.SemaphoreType.DMA((na * nm,)), pltpu.SemaphoreType.DMA((na * nm,)),
                        pltpu.SemaphoreType.DMA((na,))],
    )(*arrays)


def _adam_math(w, g, m, v):
    m = ADAM_B1 * m + (1.0 - ADAM_B1) * g
    v = ADAM_B2 * v + (1.0 - ADAM_B2) * jnp.square(g)
    m_hat = m / (1.0 - ADAM_B1 ** ADAM_STEP)
    v_hat = v / (1.0 - ADAM_B2 ** ADAM_STEP)
    delta = -ADAM_LR * (m_hat / (jnp.sqrt(v_hat) + ADAM_EPS) + ADAM_WD * w)
    return delta, m, v


def _row_tile(r):
    return _tile(r, (256, 128, 64, 32, 16, 8))


def _sum_parts(name, parts):
    n, r, c = parts.shape
    tr = _row_tile(r)

    def body(p_ref, o_ref):
        tot = p_ref[0]
        for i in range(1, n):
            tot = tot + p_ref[i]
        o_ref[...] = tot

    return pl.pallas_call(
        body, name=name, grid=(r // tr,), in_specs=[pl.BlockSpec((n, tr, c), lambda i: (0, i, 0))],
        out_specs=pl.BlockSpec((tr, c), lambda i: (i, 0)), out_shape=jax.ShapeDtypeStruct((r, c), F32),
        compiler_params=_params(("parallel",)),
    )(parts)


def _adamw(name, w, parts, m, v):
    n, r, c = parts.shape
    tr = _row_tile(r)

    def body(w_ref, p_ref, m_ref, v_ref, g_ref, d_ref, nm_ref, nv_ref):
        g = p_ref[0]
        for i in range(1, n):
            g = g + p_ref[i]
        delta, nm, nv = _adam_math(w_ref[...], g, m_ref[...], v_ref[...])
        g_ref[...], d_ref[...], nm_ref[...], nv_ref[...] = g, delta, nm, nv

    blk = pl.BlockSpec((tr, c), lambda i: (i, 0))
    return pl.pallas_call(
        body, name=name, grid=(r // tr,),
        in_specs=[blk, pl.BlockSpec((n, tr, c), lambda i: (0, i, 0)), blk, blk], out_specs=[blk] * 4,
        out_shape=[jax.ShapeDtypeStruct((r, c), F32)] * 4, compiler_params=_params(("parallel",)),
    )(w, parts, m, v)


def _ada_w_update(act_t, dmod, w, m, v):
    r, c = w.shape
    nb = act_t.shape[1]
    tr, tc = 256, 1024

    def body(a_ref, d_ref, w_ref, m_ref, v_ref, g_ref, dl_ref, nm_ref, nv_ref):
        a, dm = a_ref[...], d_ref[...]
        g = a[:, 0:1] * dm[0:1, :]
        for b in range(1, nb):
            g = g + a[:, b:b + 1] * dm[b:b + 1, :]
        delta, nm, nv = _adam_math(w_ref[...], g, m_ref[...], v_ref[...])
        g_ref[...], dl_ref[...], nm_ref[...], nv_ref[...] = g, delta, nm, nv

    blk = pl.BlockSpec((tr, tc), lambda i, j: (i, j))
    return pl.pallas_call(
        body, name="ada_w_update", grid=(r // tr, c // tc),
        in_specs=[pl.BlockSpec((tr, nb), lambda i, j: (i, 0)), pl.BlockSpec((nb, tc), lambda i, j: (0, j)),
                  blk, blk, blk],
        out_specs=[blk] * 4, out_shape=[jax.ShapeDtypeStruct((r, c), F32)] * 4,
        compiler_params=_params(("parallel", "parallel")),
    )(act_t, dmod, w, m, v)


WEIGHTS = ['ada_w', 'ada_b', 'norm1_gain', 'norm2_gain', 'final_gain', 'w_in', 'w_out', 's5_lambda_re',
           's5_lambda_im', 's5_log_step', 's5_b_re', 's5_b_im', 's5_c_re', 's5_c_im', 's5_d', 's5_w_glu',
           's5_b_glu', 'rk_shift_prev', 'rk_shift_next', 'rk_w0', 'rk_w_up', 'rk_a0', 'rk_a_up', 'rk_g_up',
           'rk_k_k', 'rk_k_a', 'rk_r_k', 'rk_ln_gain', 'rk_ln_bias', 'ffn_w1', 'ffn_w2']
BIG_SHARDED = ['w_in', 'w_out', 's5_w_glu', 'ffn_w1', 'ffn_w2']
RK_SHARDED = ['rk_w0', 'rk_a0', 'rk_w_up', 'rk_a_up', 'rk_g_up']
REPLICATED = ['ada_b', 'norm1_gain', 'norm2_gain', 'final_gain', 's5_lambda_re', 's5_lambda_im', 's5_log_step',
              's5_b_re', 's5_b_im', 's5_c_re', 's5_c_im', 's5_d', 's5_b_glu', 'rk_shift_prev', 'rk_shift_next',
              'rk_k_k', 'rk_k_a', 'rk_r_k', 'rk_ln_gain', 'rk_ln_bias']
PACK_COLS = 1024
N_CHIPS = 4


def _pack_rows(arrays, cols):
    return jnp.concatenate([a.reshape(-1, cols) for a in arrays], axis=0)


def _pack_flat(arrays):
    flat = jnp.concatenate([a.reshape(-1) for a in arrays])
    rows = -(-flat.shape[0] // PACK_COLS)
    return jnp.pad(flat, (0, rows * PACK_COLS - flat.shape[0])).reshape(rows, PACK_COLS)


def _unpack_flat(packed, like):
    flat, out, pos = packed.reshape(-1), [], 0
    for a in like:
        out.append(flat[pos:pos + a.size].reshape(a.shape))
        pos += a.size
    return out


def _cols_to_chips(full, n_rows):
    return jnp.transpose(full.reshape(n_rows, N_CHIPS, -1), (1, 0, 2))


def _chips_to_cols(parts):
    return jnp.transpose(parts, (1, 0, 2)).reshape(parts.shape[1], -1)


def kernel(x, c, ada_w, ada_b, norm1_gain, norm2_gain, final_gain, w_in, w_out, s5_lambda_re, s5_lambda_im, s5_log_step, s5_b_re, s5_b_im, s5_c_re, s5_c_im, s5_d, s5_w_glu, s5_b_glu, rk_shift_prev, rk_shift_next, rk_w0, rk_w_up, rk_a0, rk_a_up, rk_g_up, rk_k_k, rk_k_a, rk_r_k, rk_ln_gain, rk_ln_bias, ffn_w1, ffn_w2, loss_target, m_ada_w, m_ada_b, m_norm1_gain, m_norm2_gain, m_final_gain, m_w_in, m_w_out, m_s5_lambda_re, m_s5_lambda_im, m_s5_log_step, m_s5_b_re, m_s5_b_im, m_s5_c_re, m_s5_c_im, m_s5_d, m_s5_w_glu, m_s5_b_glu, m_rk_shift_prev, m_rk_shift_next, m_rk_w0, m_rk_w_up, m_rk_a0, m_rk_a_up, m_rk_g_up, m_rk_k_k, m_rk_k_a, m_rk_r_k, m_rk_ln_gain, m_rk_ln_bias, m_ffn_w1, m_ffn_w2, v_ada_w, v_ada_b, v_norm1_gain, v_norm2_gain, v_final_gain, v_w_in, v_w_out, v_s5_lambda_re, v_s5_lambda_im, v_s5_log_step, v_s5_b_re, v_s5_b_im, v_s5_c_re, v_s5_c_im, v_s5_d, v_s5_w_glu, v_s5_b_glu, v_rk_shift_prev, v_rk_shift_next, v_rk_w0, v_rk_w_up, v_rk_a0, v_rk_a_up, v_rk_g_up, v_rk_k_k, v_rk_k_a, v_rk_r_k, v_rk_ln_gain, v_rk_ln_bias, v_ffn_w1, v_ffn_w2):
    given = dict(locals())
    w = {n: given[n] for n in WEIGHTS}
    m = {n: given["m_" + n] for n in WEIGHTS}
    v = {n: given["v_" + n] for n in WEIGHTS}
    mx, my, mc = lax.axis_index("x"), lax.axis_index("y"), lax.axis_index("c")
    chip = 2 * mx + my
    dev = 2 * chip + mc
    xt, target = x[0], loss_target[0]

    def rk_rows(d):
        return _pack_rows([d[n] for n in RK_SHARDED], 256)

    (c_all,) = _exchange("gather_c", [c], ALL_PEERS, 8)
    shards = [w[n][0].astype(BF16) for n in BIG_SHARDED] + [rk_rows(w)]
    gathered = _exchange("gather_w", shards, CHIP_PEERS, N_CHIPS)
    full = dict(zip(BIG_SHARDED, gathered[:5]))
    rk_full = gathered[5]

    (act,) = _rowwise("ada_act", lambda q: (q * _sigmoid(q),), [c_all.reshape(8, D_MODEL)], [], [(D_MODEL, F32)], 8)
    n_mod_cols = N_MOD * D_MODEL // N_CHIPS
    bias = jnp.broadcast_to(lax.dynamic_slice(ada_b, (0, chip * n_mod_cols), (1, n_mod_cols)), (8, n_mod_cols))
    mod_shard = _matmul("ada_fwd", act, ada_w[0], epilogue=_add_epilogue, extras=(bias,))
    (mod_parts,) = _exchange("gather_mod", [mod_shard], CHIP_PEERS, N_CHIPS)
    mod_all = _chips_to_cols(mod_parts)
    mod_mine = lax.dynamic_slice(mod_all, (dev, 0), (1, N_MOD * D_MODEL))
    mod = [mod_mine[:, i * D_MODEL:(i + 1) * D_MODEL] for i in range(N_MOD)]

    def rk_piece(lo, hi, lead):
        return _chips_to_cols(rk_full[:, lo:hi]).reshape(lead + (RK_WIDTH,))

    zeros = jnp.zeros((LORA, RK_WIDTH), F32)
    w_up, a_up = rk_piece(4, 132, (2, LORA)), rk_piece(132, 260, (2, LORA))
    wt = {
        "norm1_gain": norm1_gain, "norm2_gain": norm2_gain, "final_gain": final_gain.reshape(1, D_MODEL),
        "w_in": jnp.pad(_chips_to_cols(full["w_in"]), ((0, 0), (0, PROJ_PAD - PROJ))),
        "w_out": full["w_out"].reshape(D_MODEL, D_MODEL),
        "s5_w_glu": full["s5_w_glu"].reshape(S5_WIDTH, S5_WIDTH),
        "ffn_w1": _chips_to_cols(full["ffn_w1"]), "ffn_w2": full["ffn_w2"].reshape(FFN, D_MODEL),
        "mu_prev": jnp.pad(rk_shift_prev, ((0, 0), (0, RK_PAD - RK_IN))),
        "mu_next": jnp.pad(rk_shift_next, ((0, 0), (0, RK_PAD - RK_IN))),
        "lam_re": [s5_lambda_re[0, d].reshape(S5_CH, 1) for d in range(2)],
        "lam_im": [s5_lambda_im[0, d].reshape(S5_CH, 1) for d in range(2)],
        "log_step": [jnp.repeat(s5_log_step[0, d], S5_STATE).reshape(S5_CH, 1) for d in range(2)],
        "b_re": s5_b_re.reshape(S5_CH, S5_GROUP), "b_im": s5_b_im.reshape(S5_CH, S5_GROUP),
        "c_re": s5_c_re[0], "c_im": s5_c_im[0],
        "s5_d": s5_d, "s5_b_glu": s5_b_glu,
        "w0": list(rk_piece(0, 2, (2,))[:, None, :]), "a0": list(rk_piece(2, 4, (2,))[:, None, :]),
        "w_up": [jnp.concatenate([w_up[0], zeros]), jnp.concatenate([zeros, w_up[1]])],
        "a_up": [jnp.concatenate([a_up[0], zeros]), jnp.concatenate([zeros, a_up[1]])],
        "g_up": jnp.pad(rk_piece(260, 420, (GATE_LORA,)), ((0, GATE_PAD - GATE_LORA), (0, 0))),
        "k_k": rk_k_k, "k_a": rk_k_a, "r_k": rk_r_k.reshape(1, RK_WIDTH),
        "ln_gain": rk_ln_gain, "ln_bias": rk_ln_bias,
    }

    loss_rows, grad_x, g = _local_step(xt, target, mod, wt)
    loss = lax.psum(jnp.sum(loss_rows), ("x", "y", "c"))

    big_grads = {
        "w_in": _cols_to_chips(g["w_in"][:, :PROJ], D_MODEL),
        "w_out": g["w_out"].reshape(N_CHIPS, -1, D_MODEL),
        "s5_w_glu": g["s5_w_glu"].reshape(N_CHIPS, -1, S5_WIDTH),
        "ffn_w1": _cols_to_chips(g["ffn_w1"], D_MODEL),
        "ffn_w2": g["ffn_w2"].reshape(N_CHIPS, -1, D_MODEL),
    }
    rk_grads = jnp.concatenate([
        _cols_to_chips(jnp.concatenate(g["w0"]), 2), _cols_to_chips(jnp.concatenate(g["a0"]), 2),
        _cols_to_chips(jnp.concatenate([g["w_up"][0][:LORA], g["w_up"][1][LORA:]]), 2 * LORA),
        _cols_to_chips(jnp.concatenate([g["a_up"][0][:LORA], g["a_up"][1][LORA:]]), 2 * LORA),
        _cols_to_chips(g["g_up"][:GATE_LORA], GATE_LORA)], axis=1)
    names = BIG_SHARDED + ["rk"]
    arrived = _exchange("scatter_grads", [big_grads[n] for n in BIG_SHARDED] + [rk_grads], CHIP_PEERS, N_CHIPS,
                        scatter=True)
    core_sums = [_sum_parts("sum_" + n, a) for n, a in zip(names, arrived)]
    pairs = _exchange("swap_sums", core_sums, CORE_PEER, 2)

    out = {}
    for n, pair in zip(BIG_SHARDED, pairs[:5]):
        res = _adamw("adamw_" + n, w[n][0], pair, m[n][0], v[n][0])
        out[n] = [r[None] for r in res]
    rk_res = _adamw("adamw_rk", rk_rows(w), pairs[5], rk_rows(m), rk_rows(v))
    for q in range(4):
        pieces, pos = [], 0
        for n in RK_SHARDED:
            rows = w[n].size // 256
            pieces.append(rk_res[q][pos:pos + rows].reshape(w[n].shape))
            pos += rows
        for n, piece in zip(RK_SHARDED, pieces):
            out.setdefault(n, []).append(piece)

    local_small = {
        "ada_b": jnp.concatenate(g["mod"], axis=1),
        "norm1_gain": g["norm1_gain"], "norm2_gain": g["norm2_gain"], "final_gain": g["final_gain"],
        "s5_lambda_re": jnp.concatenate(g["lam_re"]), "s5_lambda_im": jnp.concatenate(g["lam_im"]),
        "s5_log_step": jnp.concatenate([q.reshape(S5_GROUPS, S5_STATE).sum(axis=1) for q in g["log_step"]]),
        "s5_b_re": g["b_re"], "s5_b_im": g["b_im"], "s5_c_re": g["c_re"], "s5_c_im": g["c_im"],
        "s5_d": g["s5_d"], "s5_b_glu": g["s5_b_glu"],
        "rk_shift_prev": g["mu_prev"][:, :RK_IN], "rk_shift_next": g["mu_next"][:, :RK_IN],
        "rk_k_k": g["k_k"], "rk_k_a": g["k_a"], "rk_r_k": g["r_k"],
        "rk_ln_gain": g["ln_gain"], "rk_ln_bias": g["ln_bias"],
    }
    (small_all,) = _exchange("gather_small", [_pack_flat([local_small[n] for n in REPLICATED])], ALL_PEERS, 8)
    small_res = _adamw("adamw_small", _pack_flat([w[n] for n in REPLICATED]), small_all,
                       _pack_flat([m[n] for n in REPLICATED]), _pack_flat([v[n] for n in REPLICATED]))
    for q in range(4):
        for n, piece in zip(REPLICATED, _unpack_flat(small_res[q], [w[n] for n in REPLICATED])):
            out.setdefault(n, []).append(piece)

    mod_rows = N_MOD * D_MODEL // PACK_COLS
    dmod_all = small_all[:, :mod_rows].reshape(8, N_MOD * D_MODEL)
    dmod = lax.dynamic_slice(dmod_all, (0, chip * n_mod_cols), (8, n_mod_cols))
    res = _ada_w_update(act.T, dmod, ada_w[0], m_ada_w[0], v_ada_w[0])
    out["ada_w"] = [r[None] for r in res]

    return (loss, grad_x[None], *[out[n][0] for n in WEIGHTS], *[out[n][1] for n in WEIGHTS],
            *[out[n][2] for n in WEIGHTS], *[out[n][3] for n in WEIGHTS])
```

```python
import functools
import math

import jax
import jax.numpy as jnp
from jax import lax
from jax.experimental import pallas as pl
from jax.experimental.pallas import tpu as pltpu

F32 = jnp.float32
BF16 = jnp.bfloat16

D_MODEL = 2048
S5_WIDTH = 1024
S5_GROUP = 16
S5_GROUPS = 64
S5_STATE = 64
S5_CH = S5_GROUPS * S5_STATE
S5_BLK = 256
RK_WIDTH = 1024
RK_HEAD = 64
RK_HEADS = 16
LORA = 64
GATE_LORA = 160
GATE_PAD = 256
RK_IN = 3488
RK_PAD = 3584
PROJ = 4512
PROJ_PAD = 4608
FFN = 8192
N_MOD = 6
NORM_EPS = 1e-6
GN_EPS = 64e-5
L2_EPS = 1e-12
RK_CHUNK = 64
RK_PASSES = 3
RK_SOLVE_PASSES = 3
LW_SCALE = math.exp(-0.5)
ADAM_LR, ADAM_B1, ADAM_B2, ADAM_EPS, ADAM_WD, ADAM_STEP = 0.001, 0.9, 0.999, 1e-08, 0.01, 10
VMEM_LIMIT = 56 * 1024 * 1024
HI = lax.Precision.HIGHEST


def _params(sem=None):
    return pltpu.CompilerParams(dimension_semantics=sem, vmem_limit_bytes=VMEM_LIMIT)


def _full(a):
    nd = a.ndim
    return pl.BlockSpec(a.shape, lambda *_: (0,) * nd)


@jax.custom_vjp
def _bdot(a, b):
    return jnp.dot(a.astype(BF16), b.astype(BF16), preferred_element_type=F32)


def _bdot_fwd(a, b):
    return _bdot(a, b), (a, b)


def _bdot_bwd(res, g):
    a, b = res
    gb = g.astype(BF16)
    da = lax.dot_general(gb, b.astype(BF16), (((1,), (1,)), ((), ())), preferred_element_type=F32)
    db = lax.dot_general(a.astype(BF16), gb, (((0,), (0,)), ((), ())), preferred_element_type=F32)
    return da, db


_bdot.defvjp(_bdot_fwd, _bdot_bwd)


def _fdot(a, b):
    return jnp.dot(a, b, precision=HI, preferred_element_type=F32)


def _sigmoid(z):
    return 1.0 / (1.0 + jnp.exp(-z))


def _gelu(y):
    return 0.5 * y * (1.0 + jnp.tanh(0.7978845608028654 * (y + 0.044715 * (y * y * y))))


def _rms(x):
    return x * lax.rsqrt(jnp.mean(x * x, axis=-1, keepdims=True) + NORM_EPS)


def _tile(n, prefs):
    for t in prefs:
        if n % t == 0:
            return t
    return n


def _matmul(name, a, b, ta=False, tb=False, epilogue=None, extras=(), out_dtypes=(F32,)):
    m = a.shape[1] if ta else a.shape[0]
    k = a.shape[0] if ta else a.shape[1]
    n = b.shape[0] if tb else b.shape[1]
    assert k == (b.shape[1] if tb else b.shape[0]), (a.shape, b.shape, ta, tb)
    tm = _tile(m, (1024, 512, 256, 128))
    tn = _tile(n, (1024, 768, 512, 256, 128))
    tk = _tile(k, (512, 256, 128))
    nk = k // tk
    n_ex, n_out = len(extras), len(out_dtypes)
    dims = (((0 if ta else 1,), (1 if tb else 0,)), ((), ()))

    def body(a_ref, b_ref, *rest):
        ex_refs, out_refs, acc = rest[:n_ex], rest[n_ex:n_ex + n_out], rest[-1]
        kk = pl.program_id(2)

        @pl.when(kk == 0)
        def _():
            acc[...] = jnp.zeros_like(acc)

        acc[...] += lax.dot_general(a_ref[...].astype(BF16), b_ref[...].astype(BF16), dims,
                                    preferred_element_type=F32)

        @pl.when(kk == nk - 1)
        def _():
            res = acc[...]
            outs = epilogue(res, *[e[...] for e in ex_refs]) if epilogue is not None else (res,)
            for o_ref, val in zip(out_refs, outs):
                o_ref[...] = val.astype(o_ref.dtype)

    a_spec = pl.BlockSpec((tk, tm), lambda i, j, q: (q, i)) if ta else pl.BlockSpec((tm, tk), lambda i, j, q: (i, q))
    b_spec = pl.BlockSpec((tn, tk), lambda i, j, q: (j, q)) if tb else pl.BlockSpec((tk, tn), lambda i, j, q: (q, j))
    mn_spec = pl.BlockSpec((tm, tn), lambda i, j, q: (i, j))
    outs = pl.pallas_call(
        body, name=name, grid=(m // tm, n // tn, nk),
        in_specs=[a_spec, b_spec] + [mn_spec] * n_ex,
        out_specs=[mn_spec] * n_out,
        out_shape=[jax.ShapeDtypeStruct((m, n), dt) for dt in out_dtypes],
        scratch_shapes=[pltpu.VMEM((tm, tn), F32)],
        compiler_params=_params(("parallel", "parallel", "arbitrary")),
    )(a, b, *extras)
    return outs[0] if n_out == 1 else outs


def _row_spec(a, tm):
    return pl.BlockSpec((tm, a.shape[1]), lambda i: (i, 0))


def _rowwise(name, fn, rows, params, outs, tm):
    t = rows[0].shape[0]
    tm = min(tm, t)
    n_r, n_p = len(rows), len(params)

    def body(*refs):
        vals = [r[...] for r in refs[:n_r + n_p]]
        res = fn(*vals)
        for o_ref, val in zip(refs[n_r + n_p:], res):
            o_ref[...] = val.astype(o_ref.dtype)

    res = pl.pallas_call(
        body, name=name, grid=(t // tm,),
        in_specs=[_row_spec(r, tm) for r in rows] + [_full(p) for p in params],
        out_specs=[pl.BlockSpec((tm, n), lambda i: (i, 0)) for n, _ in outs],
        out_shape=[jax.ShapeDtypeStruct((t, n), dt) for n, dt in outs],
        compiler_params=_params(("parallel",)),
    )(*rows, *params)
    return res


def _rowwise_vjp(name, fn, rows, params, cts, row_grads, param_grads, tm, consts=(), addends=None,
                 emit=(), row_grad_dtypes=None):
    t = rows[0].shape[0]
    tm = min(tm, t)
    addends = addends or {}
    n_r, n_p, n_c = len(rows), len(params), len(consts)
    ct_flat = [c for group in cts for c in group]
    add_list = [addends[q] for q in sorted(addends)]
    n_ct, n_add = len(ct_flat), len(add_list)
    row_grad_dtypes = row_grad_dtypes or [F32] * len(row_grads)

    def body(*refs):
        pos = 0
        row_v = [r[...].astype(F32) for r in refs[pos:pos + n_r]]; pos += n_r
        par_v = [r[...].astype(F32) for r in refs[pos:pos + n_p]]; pos += n_p
        con_v = [r[...] for r in refs[pos:pos + n_c]]; pos += n_c
        ct_v = [r[...].astype(F32) for r in refs[pos:pos + n_ct]]; pos += n_ct
        add_v = [r[...] for r in refs[pos:pos + n_add]]; pos += n_add
        emit_refs = refs[pos:pos + len(emit)]; pos += len(emit)
        rg_refs = refs[pos:pos + len(row_grads)]; pos += len(row_grads)
        pg_refs = refs[pos:pos + len(param_grads)]

        def diff_fn(*dargs):
            rv, pv = list(row_v), list(par_v)
            for q, i in enumerate(row_grads):
                rv[i] = dargs[q]
            for q, j in enumerate(param_grads):
                pv[j] = dargs[len(row_grads) + q]
            return fn(*rv, *pv, *con_v)

        prim = [row_v[i] for i in row_grads] + [par_v[j] for j in param_grads]
        res, vjp = jax.vjp(diff_fn, *prim)
        ct_vals, q = [], 0
        for o, group in zip(res, cts):
            tot = jnp.zeros_like(o)
            for _ in group:
                tot = tot + ct_v[q]
                q += 1
            ct_vals.append(tot)
        grads = vjp(tuple(ct_vals))
        for e_ref, idx in zip(emit_refs, emit):
            e_ref[...] = res[idx].astype(e_ref.dtype)
        add_pos = {p: q for q, p in enumerate(sorted(addends))}
        for q, g_ref in enumerate(rg_refs):
            g = grads[q]
            if q in add_pos:
                g = g + add_v[add_pos[q]]
            g_ref[...] = g.astype(g_ref.dtype)

        @pl.when(pl.program_id(0) == 0)
        def _():
            for g_ref in pg_refs:
                g_ref[...] = jnp.zeros_like(g_ref)

        for q, g_ref in enumerate(pg_refs):
            g_ref[...] += grads[len(row_grads) + q]

    emit_shapes = []
    if emit:
        probe = jax.eval_shape(lambda *a: fn(*a), *[jax.ShapeDtypeStruct((tm, r.shape[1]), F32) for r in rows],
                               *[jax.ShapeDtypeStruct(p.shape, p.dtype) for p in params],
                               *[jax.ShapeDtypeStruct(c.shape, c.dtype) for c in consts])
        emit_shapes = [probe[idx].shape[1] for idx in emit]
    out_specs = ([pl.BlockSpec((tm, n), lambda i: (i, 0)) for n in emit_shapes]
                 + [_row_spec(rows[i], tm) for i in row_grads]
                 + [_full(params[j]) for j in param_grads])
    out_shape = ([jax.ShapeDtypeStruct((t, n), F32) for n in emit_shapes]
                 + [jax.ShapeDtypeStruct(rows[i].shape, dt) for i, dt in zip(row_grads, row_grad_dtypes)]
                 + [jax.ShapeDtypeStruct(params[j].shape, F32) for j in param_grads])
    return pl.pallas_call(
        body, name=name, grid=(t // tm,),
        in_specs=([_row_spec(r, tm) for r in rows] + [_full(p) for p in params] + [_full(c) for c in consts]
                  + [_row_spec(c, tm) for c in ct_flat] + [_row_spec(a, tm) for a in add_list]),
        out_specs=out_specs, out_shape=out_shape,
        compiler_params=_params(("arbitrary",)),
    )(*rows, *params, *consts, *ct_flat, *add_list)


def _norm_mod_fn(x, gain, scale, shift):
    return (_rms(x) * gain * (1.0 + scale) + shift,)


def _resid_norm_mod_fn(x, mixed, gate, gain, scale, shift):
    x1 = x + gate * mixed
    return x1, _rms(x1) * gain * (1.0 + scale) + shift


def _loss_fn(x1, ffn, target, gate, gain):
    y = _rms(x1 + gate * ffn) * gain
    err = y - target
    return (0.5 * jnp.mean(err * err, axis=-1, keepdims=True),)


def _s5_out_fn(ylin, u, d_skip, w_glu, b_glu):
    z = _gelu(ylin + d_skip * u)
    return (z * _sigmoid(_bdot(z, w_glu) + b_glu),)


def _rk_pre_fn(k, wdn, adn, gdn, w0_0, w0_1, wup_0, wup_1, a0_0, a0_1, aup_0, aup_1, g_up, k_k, k_a, seg, seg_t):
    kkr = k * k_k
    inv = 1.0 / jnp.sqrt(jnp.maximum(_fdot(kkr * kkr, seg), L2_EPS * L2_EPS))
    kk = kkr * _fdot(inv, seg_t)
    tw = jnp.tanh(wdn)
    lws, kds, acts = [], [], []
    for w0, wup, a0, aup in ((w0_0, wup_0, a0_0, aup_0), (w0_1, wup_1, a0_1, aup_1)):
        lws.append(-LW_SCALE * _sigmoid(w0 + _bdot(tw, wup)))
        act = _sigmoid(a0 + _bdot(adn, aup))
        acts.append(act)
        kds.append(k * (1.0 + (act - 1.0) * k_a))
    gate = _bdot(_sigmoid(gdn), g_up)
    return (kk, lws[0], lws[1], kds[0], kds[1], acts[0], acts[1], gate)


def _rk_post_fn(y0, y1, r, v, kd0, kd1, gate, ln_gain, ln_bias, r_k, seg, seg_t):
    y = y0 + y1
    mu = _fdot(_fdot(y, seg) * (1.0 / RK_HEAD), seg_t)
    yc = y - mu
    var = _fdot(yc * yc, seg) * (1.0 / RK_HEAD)
    yn = yc * _fdot(lax.rsqrt(var + GN_EPS), seg_t) * ln_gain + ln_bias
    bonus = _fdot(_fdot(r * (kd0 + kd1) * r_k, seg), seg_t)
    return ((yn + bonus * v) * gate,)


def _s5_prep_fn(lr0, li0, ls0, lr1, li1, ls1, b_re, b_im):
    outs = []
    for lam_re, lam_im, ls in ((lr0, li0, ls0), (lr1, li1, ls1)):
        step = jnp.exp(ls)
        mag = jnp.exp(lam_re * step)
        lbar_re = mag * jnp.cos(lam_im * step)
        lbar_im = mag * jnp.sin(lam_im * step)
        den = lam_re * lam_re + lam_im * lam_im
        nr = lbar_re - 1.0
        coef_re = (nr * lam_re + lbar_im * lam_im) / den
        coef_im = (lbar_im * lam_re - nr * lam_im) / den
        outs += [lbar_re, lbar_im, coef_re * b_re - coef_im * b_im, coef_re * b_im + coef_im * b_re]
    return tuple(outs)


def _shift_rows(x, down):
    t = x.shape[0]
    rows = lax.broadcasted_iota(jnp.int32, x.shape, 0)
    if down:
        return jnp.where(rows >= 1, pltpu.roll(x, 1, 0), 0.0)
    return jnp.where(rows < t - 1, pltpu.roll(x, t - 1, 0), 0.0)


def _token_shift(p, mu_prev, mu_next):
    t, n = p.shape

    def body(p_ref, mp_ref, mn_ref, o_ref):
        x = p_ref[...]
        o_ref[...] = x + mp_ref[...] * (_shift_rows(x, True) - x) + mn_ref[...] * (_shift_rows(x, False) - x)

    col = pl.BlockSpec((t, 128), lambda j: (0, j))
    par = pl.BlockSpec((1, 128), lambda j: (0, j))
    return pl.pallas_call(
        body, name="token_shift", grid=(n // 128,), in_specs=[col, par, par], out_specs=col,
        out_shape=jax.ShapeDtypeStruct((t, n), F32), compiler_params=_params(("parallel",)),
    )(p, mu_prev, mu_next)


def _token_shift_bwd(p, mu_prev, mu_next, dps):
    t, n = p.shape

    def body(p_ref, mp_ref, mn_ref, d_ref, dp_ref, dmp_ref, dmn_ref):
        x, d, mp, mn = p_ref[...], d_ref[...], mp_ref[...], mn_ref[...]
        dp_ref[...] = d * (1.0 - mp - mn) + _shift_rows(d * mp, False) + _shift_rows(d * mn, True)
        dmp_ref[...] = jnp.sum(d * (_shift_rows(x, True) - x), axis=0, keepdims=True)
        dmn_ref[...] = jnp.sum(d * (_shift_rows(x, False) - x), axis=0, keepdims=True)

    col = pl.BlockSpec((t, 128), lambda j: (0, j))
    par = pl.BlockSpec((1, 128), lambda j: (0, j))
    return pl.pallas_call(
        body, name="token_shift_bwd", grid=(n // 128,), in_specs=[col, par, par, col],
        out_specs=[col, par, par],
        out_shape=[jax.ShapeDtypeStruct((t, n), F32), jax.ShapeDtypeStruct((1, n), F32),
                   jax.ShapeDtypeStruct((1, n), F32)],
        compiler_params=_params(("parallel",)),
    )(p, mu_prev, mu_next, dps)


N_SEG = 32


def _s5_scan(name, b_re, b_im, l_re, l_im, reverse, conj=False):
    t, n = b_re.shape
    seg_len = t // N_SEG

    ng = N_SEG // 8

    def body(br_ref, bi_ref, lr_ref, li_ref, sr_ref, si_ref, carry_ref):
        def rows(i, grp):
            first = (seg_len - 1 - i if reverse else i) + grp * 8 * seg_len
            return pl.ds(first, 8, stride=seg_len)

        ar = jnp.broadcast_to(lr_ref[...], (8, 128))
        ai = jnp.broadcast_to(li_ref[...], (8, 128))
        if conj:
            ai = -ai
        zero = jnp.zeros((8, 128), F32)
        one = jnp.ones((8, 128), F32)

        def local(i, c):
            pr, pi = c[-2:]
            out = []
            for grp in range(ng):
                sr, si = c[2 * grp], c[2 * grp + 1]
                nr = ar * sr - ai * si + br_ref[rows(i, grp), :]
                ni = ar * si + ai * sr + bi_ref[rows(i, grp), :]
                sr_ref[rows(i, grp), :] = nr
                si_ref[rows(i, grp), :] = ni
                out += [nr, ni]
            return tuple(out) + (ar * pr - ai * pi, ar * pi + ai * pr)

        ends = lax.fori_loop(0, seg_len, local, (zero,) * (2 * ng) + (one, zero))
        qr, qi = ends[-2][0:1], ends[-1][0:1]
        order = list(range(N_SEG - 1, -1, -1)) if reverse else list(range(N_SEG))
        cr = jnp.zeros((1, 128), F32)
        ci = jnp.zeros((1, 128), F32)
        for j in order:
            carry_ref[j:j + 1, :] = cr
            carry_ref[N_SEG + j:N_SEG + j + 1, :] = ci
            grp, sub = divmod(j, 8)
            lr, li = ends[2 * grp][sub:sub + 1], ends[2 * grp + 1][sub:sub + 1]
            cr, ci = lr + qr * cr - qi * ci, li + qr * ci + qi * cr
        carries = [(carry_ref[8 * grp:8 * grp + 8, :], carry_ref[N_SEG + 8 * grp:N_SEG + 8 * grp + 8, :])
                   for grp in range(ng)]

        def fix(i, c):
            pr, pi = c
            npr, npi = ar * pr - ai * pi, ar * pi + ai * pr
            for grp in range(ng):
                cr8, ci8 = carries[grp]
                sr_ref[rows(i, grp), :] = sr_ref[rows(i, grp), :] + npr * cr8 - npi * ci8
                si_ref[rows(i, grp), :] = si_ref[rows(i, grp), :] + npr * ci8 + npi * cr8
            return npr, npi

        lax.fori_loop(0, seg_len, fix, (one, zero))

    col = pl.BlockSpec((t, 128), lambda j: (0, j))
    par = pl.BlockSpec((1, 128), lambda j: (0, j))
    return pl.pallas_call(
        body, name=name, grid=(n // 128,),
        in_specs=[col, col, par, par], out_specs=[col, col],
        out_shape=[jax.ShapeDtypeStruct((t, n), F32)] * 2,
        scratch_shapes=[pltpu.VMEM((2 * N_SEG, 128), F32)],
        compiler_params=_params(("parallel",)),
    )(b_re, b_im, l_re, l_im)


def _ein(passes, spec, a, b):
    if passes == 6:
        return jnp.einsum(spec, a, b, precision=HI, preferred_element_type=F32)
    a_hi, b_hi = a.astype(BF16), b.astype(BF16)
    if passes == 1:
        return jnp.einsum(spec, a_hi, b_hi, preferred_element_type=F32)
    a_lo = (a - a_hi.astype(F32)).astype(BF16)
    b_lo = (b - b_hi.astype(F32)).astype(BF16)
    cross = jnp.einsum(spec, a_hi, b_lo, preferred_element_type=F32)
    if spec.startswith('hik'):
        m = a.shape[1]
        stacked = jnp.einsum(spec, jnp.concatenate([a_hi, a_lo], axis=1), b_hi, preferred_element_type=F32)
        return stacked[:, :m] + stacked[:, m:] + cross
    return (jnp.einsum(spec, a_hi, b_hi, preferred_element_type=F32) + cross
            + jnp.einsum(spec, a_lo, b_hi, preferred_element_type=F32))


@jax.custom_vjp
def _tri_mm(tri, tri_t, z):
    return jnp.einsum('hik,hkj->hij', tri, z, precision=HI, preferred_element_type=F32)


def _tri_mm_bwd(res, g):
    tri, tri_t = res
    return jnp.zeros_like(tri), jnp.zeros_like(tri_t), _tri_mm(tri_t, tri, g)


_tri_mm.defvjp(lambda tri, tri_t, z: (_tri_mm(tri, tri_t, z), (tri, tri_t)), _tri_mm_bwd)


def _chunk_cumsum(lw, incl, incl_t):
    shape = (lw.shape[0],) + incl.shape
    return _tri_mm(jnp.broadcast_to(incl.astype(F32), shape), jnp.broadcast_to(incl_t.astype(F32), shape), lw)


@functools.partial(jax.custom_vjp, nondiff_argnums=(0,))
def _bmm(p, a, b):
    return _ein(p, 'hik,hkj->hij', a, b)


@functools.partial(jax.custom_vjp, nondiff_argnums=(0,))
def _bmm_nt(p, a, b):
    return _ein(p, 'hik,hjk->hij', a, b)


@functools.partial(jax.custom_vjp, nondiff_argnums=(0,))
def _bmm_tn(p, a, b):
    return _ein(p, 'hki,hkj->hij', a, b)


_bmm.defvjp(lambda p, a, b: (_bmm(p, a, b), (a, b)),
            lambda p, res, g: (_bmm_nt(p, g, res[1]), _bmm_tn(p, res[0], g)))
_bmm_nt.defvjp(lambda p, a, b: (_bmm_nt(p, a, b), (a, b)),
               lambda p, res, g: (_bmm(p, g, res[1]), _bmm_tn(p, g, res[0])))
_bmm_tn.defvjp(lambda p, a, b: (_bmm_tn(p, a, b), (a, b)),
               lambda p, res, g: (_bmm_nt(p, res[1], g), _bmm(p, res[0], g)))


@jax.custom_vjp
def _split_rows(x):
    c = x.shape[1] // 2
    return x[:, :c], x[:, c:]


_split_rows.defvjp(lambda x: (_split_rows(x), None), lambda _, g: (jnp.concatenate(g, axis=1),))


def _stack_rows(a, b):
    return jnp.concatenate([a, b], axis=1)


def _rk_chunk(s0, r, lw, k, v, kk, a, reverse):
    h, c, n = r.shape
    row = lax.broadcasted_iota(jnp.int32, (c, c), 0)
    col = lax.broadcasted_iota(jnp.int32, (c, c), 1)
    incl = (row <= col) if reverse else (row >= col)
    strict = (row < col) if reverse else (row > col)
    cum = _chunk_cumsum(lw, incl, (row >= col) if reverse else (row <= col))
    g_in = jnp.exp(cum)
    g_inv = jnp.exp(-cum)
    kap = kk * jnp.exp(cum - lw)
    beta = kk * a * g_inv
    kt = k * g_inv
    rt = r * g_in
    pm, ps = RK_PASSES, RK_SOLVE_PASSES
    both = _stack_rows(kap, rt)
    kap_beta, rt_beta = _split_rows(_bmm_nt(ps, both, beta))
    kap_kt, rt_kt = _split_rows(_bmm_nt(pm, both, kt))
    kap_s0, rt_s0 = _split_rows(_bmm_nt(pm, both, s0))
    l_mat = jnp.where(strict, kap_beta, 0.0)
    rhs = kap_s0 + _bmm(pm, jnp.where(strict, kap_kt, 0.0), v)
    x = -l_mat
    inv = jnp.where(row == col, 1.0, 0.0) + x
    power = _bmm(ps, x, x)
    span = 2
    while 2 * span < c:
        step, power = _split_rows(_bmm(ps, _stack_rows(inv, power), power))
        inv = inv + step
        span *= 2
    inv = inv + _bmm(ps, inv, power)
    u = _bmm(ps, inv, rhs)
    y = rt_s0 + _bmm(pm, jnp.where(incl, rt_kt, 0.0), v) - _bmm(pm, jnp.where(incl, rt_beta, 0.0), u)
    s1 = ((s0 + _bmm_tn(pm, _stack_rows(v, -u), _stack_rows(kt, beta)))
          * jnp.exp(jnp.sum(lw, axis=1, keepdims=True)))
    return y, s1


def _rk_core_fwd(name, r, lw, k, v, kk, a, reverse, chunk):
    h, t, n = r.shape
    nc = t // chunk

    def idx(i):
        return nc - 1 - i if reverse else i

    def body(r_ref, lw_ref, k_ref, v_ref, kk_ref, a_ref, y_ref, ck_ref, s_ref):
        @pl.when(pl.program_id(0) == 0)
        def _():
            s_ref[...] = jnp.zeros_like(s_ref)

        s0 = s_ref[...]
        ck_ref[0] = s0
        y, s1 = _rk_chunk(s0, r_ref[...], lw_ref[...], k_ref[...], v_ref[...], kk_ref[...], a_ref[...], reverse)
        y_ref[...] = y
        s_ref[...] = s1

    blk = pl.BlockSpec((h, chunk, n), lambda i: (0, idx(i), 0))
    return pl.pallas_call(
        body, name=name, grid=(nc,), in_specs=[blk] * 6,
        out_specs=[blk, pl.BlockSpec((1, h, n, n), lambda i: (idx(i), 0, 0, 0))],
        out_shape=[jax.ShapeDtypeStruct((h, t, n), F32), jax.ShapeDtypeStruct((nc, h, n, n), F32)],
        scratch_shapes=[pltpu.VMEM((h, n, n), F32)],
        compiler_params=_params(("arbitrary",)),
    )(r, lw, k, v, kk, a)


def _rk_core_bwd(name, r, lw, k, v, kk, a, ck, dy, reverse, chunk):
    h, t, n = r.shape
    nc = t // chunk

    def idx(i):
        return i if reverse else nc - 1 - i

    def body(r_ref, lw_ref, k_ref, v_ref, kk_ref, a_ref, ck_ref, dy_ref, *rest):
        out_refs, ds_ref = rest[:6], rest[6]

        @pl.when(pl.program_id(0) == 0)
        def _():
            ds_ref[...] = jnp.zeros_like(ds_ref)

        fn = functools.partial(_rk_chunk, reverse=reverse)
        _, vjp = jax.vjp(fn, ck_ref[0], r_ref[...], lw_ref[...], k_ref[...], v_ref[...], kk_ref[...], a_ref[...])
        grads = vjp((dy_ref[...], ds_ref[...]))
        ds_ref[...] = grads[0]
        for o_ref, g in zip(out_refs, grads[1:]):
            o_ref[...] = g

    blk = pl.BlockSpec((h, chunk, n), lambda i: (0, idx(i), 0))
    return pl.pallas_call(
        body, name=name, grid=(nc,),
        in_specs=[blk] * 6 + [pl.BlockSpec((1, h, n, n), lambda i: (idx(i), 0, 0, 0)), blk],
        out_specs=[blk] * 6,
        out_shape=[jax.ShapeDtypeStruct((h, t, n), F32)] * 6,
        scratch_shapes=[pltpu.VMEM((h, n, n), F32)],
        compiler_params=_params(("arbitrary",)),
    )(r, lw, k, v, kk, a, ck, dy)


def _s5_dlbar(name, lam_re, lam_im, s_re, s_im, prev_is_down):
    t, n = lam_re.shape
    w = 512

    def body(lr_ref, li_ref, sr_ref, si_ref, or_ref, oi_ref):
        sr, si = _shift_rows(sr_ref[...], prev_is_down), _shift_rows(si_ref[...], prev_is_down)
        lr, li = lr_ref[...], li_ref[...]
        or_ref[...] = jnp.sum(lr * sr + li * si, axis=0, keepdims=True)
        oi_ref[...] = jnp.sum(li * sr - lr * si, axis=0, keepdims=True)

    col = pl.BlockSpec((t, w), lambda j: (0, j))
    par = pl.BlockSpec((1, w), lambda j: (0, j))
    return pl.pallas_call(
        body, name=name, grid=(n // w,), in_specs=[col] * 4, out_specs=[par, par],
        out_shape=[jax.ShapeDtypeStruct((1, n), F32)] * 2, compiler_params=_params(("parallel",)),
    )(lam_re, lam_im, s_re, s_im)


def _to_heads(x):
    return jnp.transpose(x.reshape(x.shape[0], RK_HEADS, RK_HEAD), (1, 0, 2))


def _from_heads(x):
    return jnp.transpose(x, (1, 0, 2)).reshape(x.shape[1], RK_WIDTH)


def _s5_in_dense(bbar):
    b = jnp.transpose(bbar.reshape(S5_GROUPS, S5_STATE, S5_GROUP), (0, 2, 1))
    return jnp.einsum('ghp,gk->ghkp', b, jnp.eye(S5_GROUPS, dtype=bbar.dtype)).reshape(S5_WIDTH, S5_CH)


def _s5_in_diag(dense):
    d = dense.reshape(S5_GROUPS, S5_GROUP, S5_GROUPS, S5_STATE)
    return jnp.transpose(jnp.einsum('ghgp->ghp', d), (0, 2, 1)).reshape(S5_CH, S5_GROUP)


def _s5_out_dense(c):
    ct = jnp.transpose(c, (0, 2, 1))
    return jnp.einsum('gph,gk->gpkh', ct, jnp.eye(S5_GROUPS, dtype=c.dtype)).reshape(S5_CH, S5_WIDTH)


def _s5_out_diag(dense):
    d = dense.reshape(S5_GROUPS, S5_STATE, S5_GROUPS, S5_GROUP)
    return jnp.transpose(jnp.einsum('gpgh->gph', d), (0, 2, 1))


def _head_indicator():
    ch = lax.broadcasted_iota(jnp.int32, (RK_WIDTH, 128), 0) // RK_HEAD
    hd = lax.broadcasted_iota(jnp.int32, (RK_WIDTH, 128), 1)
    seg = (ch == hd).astype(F32)
    return seg, seg.T


def _add_epilogue(acc, e):
    return (acc + e,)


def _local_step(x, target, mod, wt, chunk=RK_CHUNK):
    t = x.shape[0]
    sh1, sc1, gt1, sh2, sc2, gt2 = mod
    seg, seg_t = _head_indicator()
    g = {}

    (h1,) = _rowwise("norm1", _norm_mod_fn, [x], [wt["norm1_gain"], sc1, sh1], [(D_MODEL, BF16)], 256)
    proj = _matmul("proj", h1, wt["w_in"])
    u, p = proj[:, :S5_WIDTH], proj[:, S5_WIDTH:]
    ps = _token_shift(p, wt["mu_prev"], wt["mu_next"])
    r, k, v = ps[:, :1024], ps[:, 1024:2048], ps[:, 2048:3072]
    wdn, adn, gdn = ps[:, 3072:3200], ps[:, 3200:3328], ps[:, 3328:RK_PAD]

    prep_rows = [wt["lam_re"][0], wt["lam_im"][0], wt["log_step"][0], wt["lam_re"][1], wt["lam_im"][1],
                 wt["log_step"][1], wt["b_re"], wt["b_im"]]
    col1, col16 = (1, F32), (S5_GROUP, F32)
    prep = _rowwise("s5_prep", _s5_prep_fn, prep_rows, [], [col1, col1, col16, col16] * 2, 512)
    lbar = [(prep[4 * d].reshape(1, S5_CH), prep[4 * d + 1].reshape(1, S5_CH)) for d in range(2)]
    b_dense = [(_s5_in_dense(prep[4 * d + 2]), _s5_in_dense(prep[4 * d + 3])) for d in range(2)]
    c_dense = (_s5_out_dense(wt["c_re"]), -_s5_out_dense(wt["c_im"]))
    states = []
    for d in range(2):
        bu_re = _matmul(f"s5_bu_re{d}", u, b_dense[d][0])
        bu_im = _matmul(f"s5_bu_im{d}", u, b_dense[d][1])
        states.append(_s5_scan(f"s5_scan{d}", bu_re, bu_im, lbar[d][0], lbar[d][1], reverse=(d == 1)))
    xs_re, xs_im = _rowwise("s5_sum", lambda a, b, c, e: (a + c, b + e),
                            [states[0][0], states[0][1], states[1][0], states[1][1]], [], [(S5_CH, F32)] * 2, 256)
    ylin = _matmul("s5_c_re", xs_re, c_dense[0])
    ylin = _matmul("s5_c_im", xs_im, c_dense[1], epilogue=_add_epilogue, extras=(ylin,))
    s5_par = [wt["s5_d"], wt["s5_w_glu"], wt["s5_b_glu"]]
    (o_s5,) = _rowwise("s5_out", _s5_out_fn, [ylin, u], s5_par, [(S5_WIDTH, BF16)], 256)

    pre_par = [wt["w0"][0], wt["w0"][1], wt["w_up"][0], wt["w_up"][1], wt["a0"][0], wt["a0"][1],
               wt["a_up"][0], wt["a_up"][1], wt["g_up"], wt["k_k"], wt["k_a"]]
    pre = _rowwise("rk_pre", _rk_pre_fn, [k, wdn, adn, gdn], pre_par + [seg, seg_t], [(RK_WIDTH, F32)] * 8, 256)
    kk, lw, kd, act, gate = pre[0], pre[1:3], pre[3:5], pre[5:7], pre[7]
    rh, vh, kkh = _to_heads(r), _to_heads(v), _to_heads(kk)
    core_in, ys, cks = [], [], []
    for d in range(2):
        ops = (rh, _to_heads(lw[d]), _to_heads(kd[d]), vh, kkh, _to_heads(act[d]))
        y_h, ck = _rk_core_fwd(f"rk_core{d}", *ops, reverse=(d == 1), chunk=min(chunk, t))
        core_in.append(ops)
        ys.append(_from_heads(y_h))
        cks.append(ck)
    post_rows = [ys[0], ys[1], r, v, kd[0], kd[1], gate]
    post_par = [wt["ln_gain"], wt["ln_bias"], wt["r_k"]]
    (o_rk,) = _rowwise("rk_post", _rk_post_fn, post_rows, post_par + [seg, seg_t], [(RK_WIDTH, BF16)], 256)

    o = jnp.concatenate([o_s5, o_rk], axis=1)
    mixed = _matmul("mix_out", o, wt["w_out"])
    n2_par = [gt1, wt["norm2_gain"], sc2, sh2]
    x1, h2 = _rowwise("norm2", _resid_norm_mod_fn, [x, mixed], n2_par, [(D_MODEL, F32), (D_MODEL, BF16)], 256)
    f1, hid = _matmul("ffn1", h2, wt["ffn_w1"], out_dtypes=(F32, BF16),
                      epilogue=lambda acc: (acc, jnp.square(jnp.maximum(acc, 0.0))))
    ffn = _matmul("ffn2", hid, wt["ffn_w2"])

    ones = jnp.ones((t, 1), F32)
    loss_rows, dx1, dffn, g_gt2, g["final_gain"] = _rowwise_vjp(
        "loss", _loss_fn, [x1, ffn, target], [gt2, wt["final_gain"]], [[ones]], [0, 1], [0, 1], 256, emit=(0,))
    df1 = _matmul("ffn2_dx", dffn, wt["ffn_w2"], tb=True, extras=(f1,), out_dtypes=(BF16,),
                  epilogue=lambda acc, f: (acc * (2.0 * jnp.maximum(f, 0.0)),))
    g["ffn_w2"] = _matmul("ffn2_dw", hid, dffn, ta=True)
    dh2 = _matmul("ffn1_dx", df1, wt["ffn_w1"], tb=True)
    g["ffn_w1"] = _matmul("ffn1_dw", h2, df1, ta=True)
    dx_a, dmixed, g_gt1, g["norm2_gain"], g_sc2, g_sh2 = _rowwise_vjp(
        "norm2_bwd", _resid_norm_mod_fn, [x, mixed], n2_par, [[dx1], [dh2]], [0, 1], [0, 1, 2, 3], 256)
    do = _matmul("mix_out_dx", dmixed, wt["w_out"], tb=True)
    g["w_out"] = _matmul("mix_out_dw", o, dmixed, ta=True)
    do_s5, do_rk = do[:, :S5_WIDTH], do[:, S5_WIDTH:]

    dylin, du, g["s5_d"], g["s5_w_glu"], g["s5_b_glu"] = _rowwise_vjp(
        "s5_out_bwd", _s5_out_fn, [ylin, u], s5_par, [[do_s5]], [0, 1], [0, 1, 2], 256)
    dxs_re = _matmul("s5_c_re_dx", dylin, c_dense[0], tb=True)
    dxs_im = _matmul("s5_c_im_dx", dylin, c_dense[1], tb=True)
    g["c_re"] = _s5_out_diag(_matmul("s5_c_re_dw", xs_re, dylin, ta=True))
    g["c_im"] = -_s5_out_diag(_matmul("s5_c_im_dw", xs_im, dylin, ta=True))
    prep_cts = []
    for d in range(2):
        lam_re, lam_im = _s5_scan(f"s5_adj{d}", dxs_re, dxs_im, lbar[d][0], lbar[d][1], reverse=(d == 0), conj=True)
        dl_re, dl_im = _s5_dlbar(f"s5_dlbar{d}", lam_re, lam_im, states[d][0], states[d][1], prev_is_down=(d == 0))
        du = _matmul(f"s5_bu_re{d}_dx", lam_re, b_dense[d][0], tb=True, epilogue=_add_epilogue, extras=(du,))
        du = _matmul(f"s5_bu_im{d}_dx", lam_im, b_dense[d][1], tb=True, epilogue=_add_epilogue, extras=(du,))
        db_re = _s5_in_diag(_matmul(f"s5_bu_re{d}_dw", u, lam_re, ta=True))
        db_im = _s5_in_diag(_matmul(f"s5_bu_im{d}_dw", u, lam_im, ta=True))
        prep_cts += [[dl_re.reshape(S5_CH, 1)], [dl_im.reshape(S5_CH, 1)], [db_re], [db_im]]
    pg = _rowwise_vjp("s5_prep_bwd", _s5_prep_fn, prep_rows, [], prep_cts, list(range(8)), [], 512)
    g["lam_re"], g["lam_im"], g["log_step"] = (pg[0], pg[3]), (pg[1], pg[4]), (pg[2], pg[5])
    g["b_re"], g["b_im"] = pg[6], pg[7]

    pb = _rowwise_vjp("rk_post_bwd", _rk_post_fn, post_rows, post_par, [[do_rk]], [0, 2, 3, 4, 5, 6], [0, 1, 2],
                      128, consts=[seg, seg_t])
    dy, dr_b, dv_b, dkd_b, dgate = pb[0], pb[1], pb[2], pb[3:5], pb[5]
    g["ln_gain"], g["ln_bias"], g["r_k"] = pb[6], pb[7], pb[8]
    dyh = _to_heads(dy)
    cg = []
    for d in range(2):
        grads = _rk_core_bwd(f"rk_core{d}_bwd", *core_in[d], cks[d], dyh, reverse=(d == 1), chunk=min(chunk, t))
        cg.append([_from_heads(q) for q in grads])
    pre_cts = [[cg[0][4], cg[1][4]], [cg[0][1]], [cg[1][1]], [cg[0][2], dkd_b[0]], [cg[1][2], dkd_b[1]],
               [cg[0][5]], [cg[1][5]], [dgate]]
    qb = _rowwise_vjp("rk_pre_bwd", _rk_pre_fn, [k, wdn, adn, gdn], pre_par, pre_cts, [0, 1, 2, 3],
                      list(range(11)), 128, consts=[seg, seg_t])
    dk, dwdn, dadn, dgdn = qb[:4]
    g["w0"], g["w_up"], g["a0"], g["a_up"] = (qb[4], qb[5]), (qb[6], qb[7]), (qb[8], qb[9]), (qb[10], qb[11])
    g["g_up"], g["k_k"], g["k_a"] = qb[12], qb[13], qb[14]
    dr, dv = _rowwise("rk_sum", lambda a, b, c, e, f, h: (a + b + c, e + f + h),
                      [cg[0][0], cg[1][0], dr_b, cg[0][3], cg[1][3], dv_b], [], [(RK_WIDTH, F32)] * 2, 256)
    dps = jnp.concatenate([dr, dk, dv, dwdn, dadn, dgdn], axis=1)
    dp, g["mu_prev"], g["mu_next"] = _token_shift_bwd(p, wt["mu_prev"], wt["mu_next"], dps)

    dproj = jnp.concatenate([du, dp], axis=1)
    dh1 = _matmul("proj_dx", dproj, wt["w_in"], tb=True)
    g["w_in"] = _matmul("proj_dw", h1, dproj, ta=True)
    grad_x, g["norm1_gain"], g_sc1, g_sh1 = _rowwise_vjp(
        "norm1_bwd", _norm_mod_fn, [x], [wt["norm1_gain"], sc1, sh1], [[dh1]], [0], [0, 1, 2], 256,
        addends={0: dx_a})
    g["mod"] = [g_sh1, g_sc1, g_gt1, g_sh2, g_sc2, g_gt2]
    return loss_rows, grad_x, g


CHIP_PEERS = ((1, 0, 0), (0, 1, 0), (1, 1, 0))
ALL_PEERS = ((0, 0, 1), (0, 1, 0), (0, 1, 1), (1, 0, 0), (1, 0, 1), (1, 1, 0), (1, 1, 1))
CORE_PEER = ((0, 0, 1),)


def _exchange(name, arrays, peers, n_slots, scatter=False):
    na, nm = len(arrays), len(peers)

    def ident(px, py, pc):
        return {8: 4 * px + 2 * py + pc, 4: 2 * px + py, 2: pc}[n_slots]

    def body(*refs):
        in_refs, out_refs = refs[:na], refs[na:2 * na]
        send_sems, recv_sems, local_sems = refs[2 * na:]
        x, y, c = lax.axis_index("x"), lax.axis_index("y"), lax.axis_index("c")
        me = ident(x, y, c)
        started = []
        for i in range(na):
            own = in_refs[i].at[me] if scatter else in_refs[i]
            local = pltpu.make_async_copy(own, out_refs[i].at[me], local_sems.at[i])
            local.start()
            started.append(local)
            for j, (fx, fy, fc) in enumerate(peers):
                px, py, pc = (1 - x if fx else x), (1 - y if fy else y), (1 - c if fc else c)
                src = in_refs[i].at[ident(px, py, pc)] if scatter else in_refs[i]
                copy = pltpu.make_async_remote_copy(
                    src_ref=src, dst_ref=out_refs[i].at[me],
                    send_sem=send_sems.at[i * nm + j], recv_sem=recv_sems.at[i * nm + j],
                    device_id=(px, py, pc), device_id_type=pl.DeviceIdType.MESH)
                copy.start()
                started.append(copy)
        for copy in started:
            copy.wait()

    any_spec = pl.BlockSpec(memory_space=pl.ANY)
    out_shape = [jax.ShapeDtypeStruct(((n_slots,) + a.shape[1:]) if scatter else ((n_slots,) + a.shape), a.dtype)
                 for a in arrays]
    return pl.pallas_call(
        body, name=name, in_specs=[any_spec] * na, out_specs=[any_spec] * na, out_shape=out_shape,
        scratch_shapes=[pltpu.SemaphoreType.DMA((na * nm,)), pltpu.SemaphoreType.DMA((na * nm,)),
                        pltpu.SemaphoreType.DMA((na,))],
    )(*arrays)


def _gather_halves(name, arrays):
    na = len(arrays)
    chips = ((1, 0), (0, 1), (1, 1))

    def body(*refs):
        in_refs, out_refs = refs[:na], refs[na:2 * na]
        ici_send, ici_recv, d2d_send, d2d_recv, local_sems = refs[2 * na:]
        x, y, c = lax.axis_index("x"), lax.axis_index("y"), lax.axis_index("c")
        me = 2 * x + y
        peers = [((1 - x if fx else x), (1 - y if fy else y)) for fx, fy in chips]
        pending = []
        for i in range(na):
            half = arrays[i].shape[0] // 2
            mine = pl.ds(pl.multiple_of(c * half, 8), half)
            theirs = pl.ds(pl.multiple_of((1 - c) * half, 8), half)
            local = pltpu.make_async_copy(in_refs[i], out_refs[i].at[me], local_sems.at[i])
            local.start()
            pending.append(local.wait)
            for j, (px, py) in enumerate(peers):
                k = len(chips) * i + j
                out = pltpu.make_async_remote_copy(
                    src_ref=in_refs[i].at[mine], dst_ref=out_refs[i].at[me, mine],
                    send_sem=ici_send.at[k], recv_sem=ici_recv.at[k],
                    device_id=(px, py, c), device_id_type=pl.DeviceIdType.MESH)
                out.start()
                pending.append(out.wait_send)
        for i in range(na):
            half = arrays[i].shape[0] // 2
            mine = pl.ds(pl.multiple_of(c * half, 8), half)
            theirs = pl.ds(pl.multiple_of((1 - c) * half, 8), half)
            for j, (px, py) in enumerate(peers):
                k = len(chips) * i + j
                landed = out_refs[i].at[2 * px + py, mine]
                pltpu.make_async_remote_copy(
                    src_ref=in_refs[i].at[mine], dst_ref=landed, send_sem=ici_send.at[k], recv_sem=ici_recv.at[k],
                    device_id=(px, py, c), device_id_type=pl.DeviceIdType.MESH).wait_recv()
                passed = pltpu.make_async_remote_copy(
                    src_ref=landed, dst_ref=landed, send_sem=d2d_send.at[k], recv_sem=d2d_recv.at[k],
                    device_id=(x, y, 1 - c), device_id_type=pl.DeviceIdType.MESH)
                passed.start()
                pending.append(passed.wait_send)
                from_sibling = out_refs[i].at[2 * px + py, theirs]
                pending.append(pltpu.make_async_remote_copy(
                    src_ref=from_sibling, dst_ref=from_sibling, send_sem=d2d_send.at[k], recv_sem=d2d_recv.at[k],
                    device_id=(x, y, 1 - c), device_id_type=pl.DeviceIdType.MESH).wait_recv)
        for wait in pending:
            wait()

    any_spec = pl.BlockSpec(memory_space=pl.ANY)
    n_sem = na * len(chips)
    return pl.pallas_call(
        body, name=name, in_specs=[any_spec] * na, out_specs=[any_spec] * na,
        out_shape=[jax.ShapeDtypeStruct((N_CHIPS,) + a.shape, a.dtype) for a in arrays],
        scratch_shapes=[pltpu.SemaphoreType.DMA((n_sem,))] * 4 + [pltpu.SemaphoreType.DMA((na,))],
    )(*arrays)


def _send_other_half(name, arrays):
    na = len(arrays)

    def body(*refs):
        in_refs, out_refs, send_sems, recv_sems = refs[:na], refs[na:2 * na], refs[-2], refs[-1]
        x, y, c = lax.axis_index("x"), lax.axis_index("y"), lax.axis_index("c")
        copies = []
        for i in range(na):
            half = arrays[i].shape[1] // 2
            theirs = pl.ds(pl.multiple_of((1 - c) * half, 8), half)
            copy = pltpu.make_async_remote_copy(
                src_ref=in_refs[i].at[:, theirs], dst_ref=out_refs[i], send_sem=send_sems.at[i],
                recv_sem=recv_sems.at[i], device_id=(x, y, 1 - c), device_id_type=pl.DeviceIdType.MESH)
            copy.start()
            copies.append(copy)
        for copy in copies:
            copy.wait()

    any_spec = pl.BlockSpec(memory_space=pl.ANY)
    return pl.pallas_call(
        body, name=name, in_specs=[any_spec] * na, out_specs=[any_spec] * na,
        out_shape=[jax.ShapeDtypeStruct((a.shape[0], a.shape[1] // 2, a.shape[2]), a.dtype) for a in arrays],
        scratch_shapes=[pltpu.SemaphoreType.DMA((na,)), pltpu.SemaphoreType.DMA((na,))],
    )(*arrays)


def _adam_math(w, g, m, v):
    m = ADAM_B1 * m + (1.0 - ADAM_B1) * g
    v = ADAM_B2 * v + (1.0 - ADAM_B2) * jnp.square(g)
    m_hat = m / (1.0 - ADAM_B1 ** ADAM_STEP)
    v_hat = v / (1.0 - ADAM_B2 ** ADAM_STEP)
    delta = -ADAM_LR * (m_hat / (jnp.sqrt(v_hat) + ADAM_EPS) + ADAM_WD * w)
    return delta, m, v


def _row_tile(r):
    return _tile(r, (256, 128, 64, 32, 16, 8))


def _sum_parts(name, parts):
    n, r, c = parts.shape
    tr = _row_tile(r)

    def body(p_ref, o_ref):
        tot = p_ref[0]
        for i in range(1, n):
            tot = tot + p_ref[i]
        o_ref[...] = tot

    return pl.pallas_call(
        body, name=name, grid=(r // tr,), in_specs=[pl.BlockSpec((n, tr, c), lambda i: (0, i, 0))],
        out_specs=pl.BlockSpec((tr, c), lambda i: (i, 0)), out_shape=jax.ShapeDtypeStruct((r, c), F32),
        compiler_params=_params(("parallel",)),
    )(parts)


def _adamw(name, w, parts, m, v):
    n, r, c = parts.shape
    tr = _row_tile(r)

    def body(w_ref, p_ref, m_ref, v_ref, g_ref, d_ref, nm_ref, nv_ref):
        g = p_ref[0]
        for i in range(1, n):
            g = g + p_ref[i]
        delta, nm, nv = _adam_math(w_ref[...], g, m_ref[...], v_ref[...])
        g_ref[...], d_ref[...], nm_ref[...], nv_ref[...] = g, delta, nm, nv

    blk = pl.BlockSpec((tr, c), lambda i: (i, 0))
    return pl.pallas_call(
        body, name=name, grid=(r // tr,),
        in_specs=[blk, pl.BlockSpec((n, tr, c), lambda i: (0, i, 0)), blk, blk], out_specs=[blk] * 4,
        out_shape=[jax.ShapeDtypeStruct((r, c), F32)] * 4, compiler_params=_params(("parallel",)),
    )(w, parts, m, v)


def _ada_w_update(act_t, dmod, w, m, v):
    r, c = w.shape
    nb = act_t.shape[1]
    tr, tc = 256, 1024

    def body(a_ref, d_ref, w_ref, m_ref, v_ref, g_ref, dl_ref, nm_ref, nv_ref):
        a, dm = a_ref[...], d_ref[...]
        g = a[:, 0:1] * dm[0:1, :]
        for b in range(1, nb):
            g = g + a[:, b:b + 1] * dm[b:b + 1, :]
        delta, nm, nv = _adam_math(w_ref[...], g, m_ref[...], v_ref[...])
        g_ref[...], dl_ref[...], nm_ref[...], nv_ref[...] = g, delta, nm, nv

    blk = pl.BlockSpec((tr, tc), lambda i, j: (i, j))
    return pl.pallas_call(
        body, name="ada_w_update", grid=(r // tr, c // tc),
        in_specs=[pl.BlockSpec((tr, nb), lambda i, j: (i, 0)), pl.BlockSpec((nb, tc), lambda i, j: (0, j)),
                  blk, blk, blk],
        out_specs=[blk] * 4, out_shape=[jax.ShapeDtypeStruct((r, c), F32)] * 4,
        compiler_params=_params(("parallel", "parallel")),
    )(act_t, dmod, w, m, v)


WEIGHTS = ['ada_w', 'ada_b', 'norm1_gain', 'norm2_gain', 'final_gain', 'w_in', 'w_out', 's5_lambda_re',
           's5_lambda_im', 's5_log_step', 's5_b_re', 's5_b_im', 's5_c_re', 's5_c_im', 's5_d', 's5_w_glu',
           's5_b_glu', 'rk_shift_prev', 'rk_shift_next', 'rk_w0', 'rk_w_up', 'rk_a0', 'rk_a_up', 'rk_g_up',
           'rk_k_k', 'rk_k_a', 'rk_r_k', 'rk_ln_gain', 'rk_ln_bias', 'ffn_w1', 'ffn_w2']
BIG_SHARDED = ['w_in', 'w_out', 's5_w_glu', 'ffn_w1', 'ffn_w2']
RK_SHARDED = ['rk_w0', 'rk_a0', 'rk_w_up', 'rk_a_up', 'rk_g_up']
REPLICATED = ['ada_b', 'norm1_gain', 'norm2_gain', 'final_gain', 's5_lambda_re', 's5_lambda_im', 's5_log_step',
              's5_b_re', 's5_b_im', 's5_c_re', 's5_c_im', 's5_d', 's5_b_glu', 'rk_shift_prev', 'rk_shift_next',
              'rk_k_k', 'rk_k_a', 'rk_r_k', 'rk_ln_gain', 'rk_ln_bias']
PACK_COLS = 1024
N_CHIPS = 4
RK_ROWS = 420
RK_ROWS_PAD = 432


def _pack_rows(arrays, cols):
    return jnp.concatenate([a.reshape(-1, cols) for a in arrays], axis=0)


def _pack_flat(arrays):
    flat = jnp.concatenate([a.reshape(-1) for a in arrays])
    rows = -(-flat.shape[0] // PACK_COLS)
    return jnp.pad(flat, (0, rows * PACK_COLS - flat.shape[0])).reshape(rows, PACK_COLS)


def _unpack_flat(packed, like):
    flat, out, pos = packed.reshape(-1), [], 0
    for a in like:
        out.append(flat[pos:pos + a.size].reshape(a.shape))
        pos += a.size
    return out


def _cols_to_chips(full, n_rows):
    return jnp.transpose(full.reshape(n_rows, N_CHIPS, -1), (1, 0, 2))


def _chips_to_cols(parts):
    return jnp.transpose(parts, (1, 0, 2)).reshape(parts.shape[1], -1)


def kernel(x, c, ada_w, ada_b, norm1_gain, norm2_gain, final_gain, w_in, w_out, s5_lambda_re, s5_lambda_im, s5_log_step, s5_b_re, s5_b_im, s5_c_re, s5_c_im, s5_d, s5_w_glu, s5_b_glu, rk_shift_prev, rk_shift_next, rk_w0, rk_w_up, rk_a0, rk_a_up, rk_g_up, rk_k_k, rk_k_a, rk_r_k, rk_ln_gain, rk_ln_bias, ffn_w1, ffn_w2, loss_target, m_ada_w, m_ada_b, m_norm1_gain, m_norm2_gain, m_final_gain, m_w_in, m_w_out, m_s5_lambda_re, m_s5_lambda_im, m_s5_log_step, m_s5_b_re, m_s5_b_im, m_s5_c_re, m_s5_c_im, m_s5_d, m_s5_w_glu, m_s5_b_glu, m_rk_shift_prev, m_rk_shift_next, m_rk_w0, m_rk_w_up, m_rk_a0, m_rk_a_up, m_rk_g_up, m_rk_k_k, m_rk_k_a, m_rk_r_k, m_rk_ln_gain, m_rk_ln_bias, m_ffn_w1, m_ffn_w2, v_ada_w, v_ada_b, v_norm1_gain, v_norm2_gain, v_final_gain, v_w_in, v_w_out, v_s5_lambda_re, v_s5_lambda_im, v_s5_log_step, v_s5_b_re, v_s5_b_im, v_s5_c_re, v_s5_c_im, v_s5_d, v_s5_w_glu, v_s5_b_glu, v_rk_shift_prev, v_rk_shift_next, v_rk_w0, v_rk_w_up, v_rk_a0, v_rk_a_up, v_rk_g_up, v_rk_k_k, v_rk_k_a, v_rk_r_k, v_rk_ln_gain, v_rk_ln_bias, v_ffn_w1, v_ffn_w2):
    given = dict(locals())
    w = {n: given[n] for n in WEIGHTS}
    m = {n: given["m_" + n] for n in WEIGHTS}
    v = {n: given["v_" + n] for n in WEIGHTS}
    mx, my, mc = lax.axis_index("x"), lax.axis_index("y"), lax.axis_index("c")
    chip = 2 * mx + my
    dev = 2 * chip + mc
    xt, target = x[0], loss_target[0]

    def rk_rows(d):
        rows = _pack_rows([d[n] for n in RK_SHARDED], 256)
        return jnp.pad(rows, ((0, RK_ROWS_PAD - rows.shape[0]), (0, 0)))

    (c_all,) = _exchange("gather_c", [c], ALL_PEERS, 8)
    shards = [w[n][0].astype(BF16) for n in BIG_SHARDED] + [rk_rows(w)]
    gathered = _gather_halves("gather_w", shards)
    full = dict(zip(BIG_SHARDED, gathered[:5]))
    rk_full = gathered[5]

    (act,) = _rowwise("ada_act", lambda q: (q * _sigmoid(q),), [c_all.reshape(8, D_MODEL)], [], [(D_MODEL, F32)], 8)
    n_mod_cols = N_MOD * D_MODEL // N_CHIPS
    bias = jnp.broadcast_to(lax.dynamic_slice(ada_b, (0, chip * n_mod_cols), (1, n_mod_cols)), (8, n_mod_cols))
    mod_shard = _matmul("ada_fwd", act, ada_w[0], epilogue=_add_epilogue, extras=(bias,))
    (mod_parts,) = _exchange("gather_mod", [mod_shard], CHIP_PEERS, N_CHIPS)
    mod_all = _chips_to_cols(mod_parts)
    mod_mine = lax.dynamic_slice(mod_all, (dev, 0), (1, N_MOD * D_MODEL))
    mod = [mod_mine[:, i * D_MODEL:(i + 1) * D_MODEL] for i in range(N_MOD)]

    def rk_piece(lo, hi, lead):
        return _chips_to_cols(rk_full[:, lo:hi]).reshape(lead + (RK_WIDTH,))

    zeros = jnp.zeros((LORA, RK_WIDTH), F32)
    w_up, a_up = rk_piece(4, 132, (2, LORA)), rk_piece(132, 260, (2, LORA))
    wt = {
        "norm1_gain": norm1_gain, "norm2_gain": norm2_gain, "final_gain": final_gain.reshape(1, D_MODEL),
        "w_in": jnp.pad(_chips_to_cols(full["w_in"]), ((0, 0), (0, PROJ_PAD - PROJ))),
        "w_out": full["w_out"].reshape(D_MODEL, D_MODEL),
        "s5_w_glu": full["s5_w_glu"].reshape(S5_WIDTH, S5_WIDTH),
        "ffn_w1": _chips_to_cols(full["ffn_w1"]), "ffn_w2": full["ffn_w2"].reshape(FFN, D_MODEL),
        "mu_prev": jnp.pad(rk_shift_prev, ((0, 0), (0, RK_PAD - RK_IN))),
        "mu_next": jnp.pad(rk_shift_next, ((0, 0), (0, RK_PAD - RK_IN))),
        "lam_re": [s5_lambda_re[0, d].reshape(S5_CH, 1) for d in range(2)],
        "lam_im": [s5_lambda_im[0, d].reshape(S5_CH, 1) for d in range(2)],
        "log_step": [jnp.repeat(s5_log_step[0, d], S5_STATE).reshape(S5_CH, 1) for d in range(2)],
        "b_re": s5_b_re.reshape(S5_CH, S5_GROUP), "b_im": s5_b_im.reshape(S5_CH, S5_GROUP),
        "c_re": s5_c_re[0], "c_im": s5_c_im[0],
        "s5_d": s5_d, "s5_b_glu": s5_b_glu,
        "w0": list(rk_piece(0, 2, (2,))[:, None, :]), "a0": list(rk_piece(2, 4, (2,))[:, None, :]),
        "w_up": [jnp.concatenate([w_up[0], zeros]), jnp.concatenate([zeros, w_up[1]])],
        "a_up": [jnp.concatenate([a_up[0], zeros]), jnp.concatenate([zeros, a_up[1]])],
        "g_up": jnp.pad(rk_piece(260, 420, (GATE_LORA,)), ((0, GATE_PAD - GATE_LORA), (0, 0))),
        "k_k": rk_k_k, "k_a": rk_k_a, "r_k": rk_r_k.reshape(1, RK_WIDTH),
        "ln_gain": rk_ln_gain, "ln_bias": rk_ln_bias,
    }

    loss_rows, grad_x, g = _local_step(xt, target, mod, wt)
    loss = lax.psum(jnp.sum(loss_rows), ("x", "y", "c"))

    big_grads = {
        "w_in": _cols_to_chips(g["w_in"][:, :PROJ], D_MODEL),
        "w_out": g["w_out"].reshape(N_CHIPS, -1, D_MODEL),
        "s5_w_glu": g["s5_w_glu"].reshape(N_CHIPS, -1, S5_WIDTH),
        "ffn_w1": _cols_to_chips(g["ffn_w1"], D_MODEL),
        "ffn_w2": g["ffn_w2"].reshape(N_CHIPS, -1, D_MODEL),
    }
    rk_grads = jnp.concatenate([
        _cols_to_chips(jnp.concatenate(g["w0"]), 2), _cols_to_chips(jnp.concatenate(g["a0"]), 2),
        _cols_to_chips(jnp.concatenate([g["w_up"][0][:LORA], g["w_up"][1][LORA:]]), 2 * LORA),
        _cols_to_chips(jnp.concatenate([g["a_up"][0][:LORA], g["a_up"][1][LORA:]]), 2 * LORA),
        _cols_to_chips(g["g_up"][:GATE_LORA], GATE_LORA),
        jnp.zeros((N_CHIPS, RK_ROWS_PAD - RK_ROWS, 256), F32)], axis=1)
    names = BIG_SHARDED + ["rk"]
    pieces = [big_grads[n] for n in BIG_SHARDED] + [rk_grads]
    from_sibling = _send_other_half("swap_halves", pieces)
    chip_sums = []
    for n, piece, other in zip(names, pieces, from_sibling):
        half, cols = other.shape[1], other.shape[2]
        own = lax.dynamic_slice_in_dim(piece, mc * half, half, axis=1)
        (both,) = _rowwise("pair_" + n, lambda a, b: (a + b,), [own.reshape(-1, cols), other.reshape(-1, cols)], [],
                           [(cols, F32)], _row_tile(N_CHIPS * half))
        chip_sums.append(both.reshape(N_CHIPS, half, cols))
    arrived = _exchange("scatter_grads", chip_sums, CHIP_PEERS, N_CHIPS, scatter=True)
    half_sums = [_sum_parts("sum_" + n, a) for n, a in zip(names, arrived)]
    pairs = [p.reshape(1, 2 * p.shape[1], p.shape[2]) for p in _exchange("swap_sums", half_sums, CORE_PEER, 2)]

    out = {}
    for n, pair in zip(BIG_SHARDED, pairs[:5]):
        res = _adamw("adamw_" + n, w[n][0], pair, m[n][0], v[n][0])
        out[n] = [r[None] for r in res]
    rk_res = _adamw("adamw_rk", rk_rows(w), pairs[5], rk_rows(m), rk_rows(v))
    for q in range(4):
        pieces, pos = [], 0
        for n in RK_SHARDED:
            rows = w[n].size // 256
            pieces.append(rk_res[q][pos:pos + rows].reshape(w[n].shape))
            pos += rows
        for n, piece in zip(RK_SHARDED, pieces):
            out.setdefault(n, []).append(piece)

    local_small = {
        "ada_b": jnp.concatenate(g["mod"], axis=1),
        "norm1_gain": g["norm1_gain"], "norm2_gain": g["norm2_gain"], "final_gain": g["final_gain"],
        "s5_lambda_re": jnp.concatenate(g["lam_re"]), "s5_lambda_im": jnp.concatenate(g["lam_im"]),
        "s5_log_step": jnp.concatenate([q.reshape(S5_GROUPS, S5_STATE).sum(axis=1) for q in g["log_step"]]),
        "s5_b_re": g["b_re"], "s5_b_im": g["b_im"], "s5_c_re": g["c_re"], "s5_c_im": g["c_im"],
        "s5_d": g["s5_d"], "s5_b_glu": g["s5_b_glu"],
        "rk_shift_prev": g["mu_prev"][:, :RK_IN], "rk_shift_next": g["mu_next"][:, :RK_IN],
        "rk_k_k": g["k_k"], "rk_k_a": g["k_a"], "rk_r_k": g["r_k"],
        "rk_ln_gain": g["ln_gain"], "rk_ln_bias": g["ln_bias"],
    }
    (small_all,) = _exchange("gather_small", [_pack_flat([local_small[n] for n in REPLICATED])], ALL_PEERS, 8)
    small_res = _adamw("adamw_small", _pack_flat([w[n] for n in REPLICATED]), small_all,
                       _pack_flat([m[n] for n in REPLICATED]), _pack_flat([v[n] for n in REPLICATED]))
    for q in range(4):
        for n, piece in zip(REPLICATED, _unpack_flat(small_res[q], [w[n] for n in REPLICATED])):
            out.setdefault(n, []).append(piece)

    mod_rows = N_MOD * D_MODEL // PACK_COLS
    dmod_all = small_all[:, :mod_rows].reshape(8, N_MOD * D_MODEL)
    dmod = lax.dynamic_slice(dmod_all, (0, chip * n_mod_cols), (8, n_mod_cols))
    res = _ada_w_update(act.T, dmod, ada_w[0], m_ada_w[0], v_ada_w[0])
    out["ada_w"] = [r[None] for r in res]

    return (loss, grad_x[None], *[out[n][0] for n in WEIGHTS], *[out[n][1] for n in WEIGHTS],
            *[out[n][2] for n in WEIGHTS], *[out[n][3] for n in WEIGHTS])
```

```python
import functools
import math

import jax
import jax.numpy as jnp
from jax import lax
from jax.experimental import pallas as pl
from jax.experimental.pallas import tpu as pltpu

F32 = jnp.float32
BF16 = jnp.bfloat16

D_MODEL = 2048
S5_WIDTH = 1024
S5_GROUP = 16
S5_GROUPS = 64
S5_STATE = 64
S5_CH = S5_GROUPS * S5_STATE
S5_BLK = 256
RK_WIDTH = 1024
RK_HEAD = 64
RK_HEADS = 16
LORA = 64
GATE_LORA = 160
GATE_PAD = 256
RK_IN = 3488
RK_PAD = 3584
PROJ = 4512
PROJ_PAD = 4608
FFN = 8192
N_MOD = 6
NORM_EPS = 1e-6
GN_EPS = 64e-5
L2_EPS = 1e-12
RK_CHUNK = 64
RK_PASSES = 3
RK_SOLVE_PASSES = 3
LW_SCALE = math.exp(-0.5)
ADAM_LR, ADAM_B1, ADAM_B2, ADAM_EPS, ADAM_WD, ADAM_STEP = 0.001, 0.9, 0.999, 1e-08, 0.01, 10
VMEM_LIMIT = 56 * 1024 * 1024
HI = lax.Precision.HIGHEST


def _params(sem=None):
    return pltpu.CompilerParams(dimension_semantics=sem, vmem_limit_bytes=VMEM_LIMIT)


def _full(a):
    nd = a.ndim
    return pl.BlockSpec(a.shape, lambda *_: (0,) * nd)


@jax.custom_vjp
def _bdot(a, b):
    return jnp.dot(a.astype(BF16), b.astype(BF16), preferred_element_type=F32)


def _bdot_fwd(a, b):
    return _bdot(a, b), (a, b)


def _bdot_bwd(res, g):
    a, b = res
    gb = g.astype(BF16)
    da = lax.dot_general(gb, b.astype(BF16), (((1,), (1,)), ((), ())), preferred_element_type=F32)
    db = lax.dot_general(a.astype(BF16), gb, (((0,), (0,)), ((), ())), preferred_element_type=F32)
    return da, db


_bdot.defvjp(_bdot_fwd, _bdot_bwd)


def _fdot(a, b):
    return jnp.dot(a, b, precision=HI, preferred_element_type=F32)


def _sigmoid(z):
    return 1.0 / (1.0 + jnp.exp(-z))


def _gelu(y):
    return 0.5 * y * (1.0 + jnp.tanh(0.7978845608028654 * (y + 0.044715 * (y * y * y))))


def _rms(x):
    return x * lax.rsqrt(jnp.mean(x * x, axis=-1, keepdims=True) + NORM_EPS)


def _tile(n, prefs):
    for t in prefs:
        if n % t == 0:
            return t
    return n


def _matmul(name, a, b, ta=False, tb=False, epilogue=None, extras=(), out_dtypes=(F32,)):
    m = a.shape[1] if ta else a.shape[0]
    k = a.shape[0] if ta else a.shape[1]
    n = b.shape[0] if tb else b.shape[1]
    assert k == (b.shape[1] if tb else b.shape[0]), (a.shape, b.shape, ta, tb)
    tm = _tile(m, (1024, 512, 256, 128))
    tn = _tile(n, (1024, 768, 512, 256, 128))
    tk = _tile(k, (512, 256, 128))
    nk = k // tk
    n_ex, n_out = len(extras), len(out_dtypes)
    dims = (((0 if ta else 1,), (1 if tb else 0,)), ((), ()))

    def body(a_ref, b_ref, *rest):
        ex_refs, out_refs, acc = rest[:n_ex], rest[n_ex:n_ex + n_out], rest[-1]
        kk = pl.program_id(2)

        @pl.when(kk == 0)
        def _():
            acc[...] = jnp.zeros_like(acc)

        acc[...] += lax.dot_general(a_ref[...].astype(BF16), b_ref[...].astype(BF16), dims,
                                    preferred_element_type=F32)

        @pl.when(kk == nk - 1)
        def _():
            res = acc[...]
            outs = epilogue(res, *[e[...] for e in ex_refs]) if epilogue is not None else (res,)
            for o_ref, val in zip(out_refs, outs):
                o_ref[...] = val.astype(o_ref.dtype)

    a_spec = pl.BlockSpec((tk, tm), lambda i, j, q: (q, i)) if ta else pl.BlockSpec((tm, tk), lambda i, j, q: (i, q))
    b_spec = pl.BlockSpec((tn, tk), lambda i, j, q: (j, q)) if tb else pl.BlockSpec((tk, tn), lambda i, j, q: (q, j))
    mn_spec = pl.BlockSpec((tm, tn), lambda i, j, q: (i, j))
    outs = pl.pallas_call(
        body, name=name, grid=(m // tm, n // tn, nk),
        in_specs=[a_spec, b_spec] + [mn_spec] * n_ex,
        out_specs=[mn_spec] * n_out,
        out_shape=[jax.ShapeDtypeStruct((m, n), dt) for dt in out_dtypes],
        scratch_shapes=[pltpu.VMEM((tm, tn), F32)],
        compiler_params=_params(("parallel", "parallel", "arbitrary")),
    )(a, b, *extras)
    return outs[0] if n_out == 1 else outs


def _row_spec(a, tm):
    return pl.BlockSpec((tm, a.shape[1]), lambda i: (i, 0))


def _rowwise(name, fn, rows, params, outs, tm):
    t = rows[0].shape[0]
    tm = min(tm, t)
    n_r, n_p = len(rows), len(params)

    def body(*refs):
        vals = [r[...] for r in refs[:n_r + n_p]]
        res = fn(*vals)
        for o_ref, val in zip(refs[n_r + n_p:], res):
            o_ref[...] = val.astype(o_ref.dtype)

    res = pl.pallas_call(
        body, name=name, grid=(t // tm,),
        in_specs=[_row_spec(r, tm) for r in rows] + [_full(p) for p in params],
        out_specs=[pl.BlockSpec((tm, n), lambda i: (i, 0)) for n, _ in outs],
        out_shape=[jax.ShapeDtypeStruct((t, n), dt) for n, dt in outs],
        compiler_params=_params(("parallel",)),
    )(*rows, *params)
    return res


def _rowwise_vjp(name, fn, rows, params, cts, row_grads, param_grads, tm, consts=(), addends=None,
                 emit=(), row_grad_dtypes=None):
    t = rows[0].shape[0]
    tm = min(tm, t)
    addends = addends or {}
    n_r, n_p, n_c = len(rows), len(params), len(consts)
    ct_flat = [c for group in cts for c in group]
    add_list = [addends[q] for q in sorted(addends)]
    n_ct, n_add = len(ct_flat), len(add_list)
    row_grad_dtypes = row_grad_dtypes or [F32] * len(row_grads)

    def body(*refs):
        pos = 0
        row_v = [r[...].astype(F32) for r in refs[pos:pos + n_r]]; pos += n_r
        par_v = [r[...].astype(F32) for r in refs[pos:pos + n_p]]; pos += n_p
        con_v = [r[...] for r in refs[pos:pos + n_c]]; pos += n_c
        ct_v = [r[...].astype(F32) for r in refs[pos:pos + n_ct]]; pos += n_ct
        add_v = [r[...] for r in refs[pos:pos + n_add]]; pos += n_add
        emit_refs = refs[pos:pos + len(emit)]; pos += len(emit)
        rg_refs = refs[pos:pos + len(row_grads)]; pos += len(row_grads)
        pg_refs = refs[pos:pos + len(param_grads)]

        def diff_fn(*dargs):
            rv, pv = list(row_v), list(par_v)
            for q, i in enumerate(row_grads):
                rv[i] = dargs[q]
            for q, j in enumerate(param_grads):
                pv[j] = dargs[len(row_grads) + q]
            return fn(*rv, *pv, *con_v)

        prim = [row_v[i] for i in row_grads] + [par_v[j] for j in param_grads]
        res, vjp = jax.vjp(diff_fn, *prim)
        ct_vals, q = [], 0
        for o, group in zip(res, cts):
            tot = jnp.zeros_like(o)
            for _ in group:
                tot = tot + ct_v[q]
                q += 1
            ct_vals.append(tot)
        grads = vjp(tuple(ct_vals))
        for e_ref, idx in zip(emit_refs, emit):
            e_ref[...] = res[idx].astype(e_ref.dtype)
        add_pos = {p: q for q, p in enumerate(sorted(addends))}
        for q, g_ref in enumerate(rg_refs):
            g = grads[q]
            if q in add_pos:
                g = g + add_v[add_pos[q]]
            g_ref[...] = g.astype(g_ref.dtype)

        @pl.when(pl.program_id(0) == 0)
        def _():
            for g_ref in pg_refs:
                g_ref[...] = jnp.zeros_like(g_ref)

        for q, g_ref in enumerate(pg_refs):
            g_ref[...] += grads[len(row_grads) + q]

    emit_shapes = []
    if emit:
        probe = jax.eval_shape(lambda *a: fn(*a), *[jax.ShapeDtypeStruct((tm, r.shape[1]), F32) for r in rows],
                               *[jax.ShapeDtypeStruct(p.shape, p.dtype) for p in params],
                               *[jax.ShapeDtypeStruct(c.shape, c.dtype) for c in consts])
        emit_shapes = [probe[idx].shape[1] for idx in emit]
    out_specs = ([pl.BlockSpec((tm, n), lambda i: (i, 0)) for n in emit_shapes]
                 + [_row_spec(rows[i], tm) for i in row_grads]
                 + [_full(params[j]) for j in param_grads])
    out_shape = ([jax.ShapeDtypeStruct((t, n), F32) for n in emit_shapes]
                 + [jax.ShapeDtypeStruct(rows[i].shape, dt) for i, dt in zip(row_grads, row_grad_dtypes)]
                 + [jax.ShapeDtypeStruct(params[j].shape, F32) for j in param_grads])
    return pl.pallas_call(
        body, name=name, grid=(t // tm,),
        in_specs=([_row_spec(r, tm) for r in rows] + [_full(p) for p in params] + [_full(c) for c in consts]
                  + [_row_spec(c, tm) for c in ct_flat] + [_row_spec(a, tm) for a in add_list]),
        out_specs=out_specs, out_shape=out_shape,
        compiler_params=_params(("arbitrary",)),
    )(*rows, *params, *consts, *ct_flat, *add_list)


def _norm_mod_fn(x, gain, scale, shift):
    return (_rms(x) * gain * (1.0 + scale) + shift,)


def _resid_norm_mod_fn(x, mixed, gate, gain, scale, shift):
    x1 = x + gate * mixed
    return x1, _rms(x1) * gain * (1.0 + scale) + shift


def _loss_fn(x1, ffn, target, gate, gain):
    y = _rms(x1 + gate * ffn) * gain
    err = y - target
    return (0.5 * jnp.mean(err * err, axis=-1, keepdims=True),)


def _s5_out_fn(ylin, u, d_skip, w_glu, b_glu):
    z = _gelu(ylin + d_skip * u)
    return (z * _sigmoid(_bdot(z, w_glu) + b_glu),)


def _rk_pre_fn(k, wdn, adn, gdn, w0_0, w0_1, wup_0, wup_1, a0_0, a0_1, aup_0, aup_1, g_up, k_k, k_a, seg, seg_t):
    kkr = k * k_k
    inv = 1.0 / jnp.sqrt(jnp.maximum(_fdot(kkr * kkr, seg), L2_EPS * L2_EPS))
    kk = kkr * _fdot(inv, seg_t)
    tw = jnp.tanh(wdn)
    lws, kds, acts = [], [], []
    for w0, wup, a0, aup in ((w0_0, wup_0, a0_0, aup_0), (w0_1, wup_1, a0_1, aup_1)):
        lws.append(-LW_SCALE * _sigmoid(w0 + _bdot(tw, wup)))
        act = _sigmoid(a0 + _bdot(adn, aup))
        acts.append(act)
        kds.append(k * (1.0 + (act - 1.0) * k_a))
    gate = _bdot(_sigmoid(gdn), g_up)
    return (kk, lws[0], lws[1], kds[0], kds[1], acts[0], acts[1], gate)


def _rk_post_fn(y0, y1, r, v, kd0, kd1, gate, ln_gain, ln_bias, r_k, seg, seg_t):
    y = y0 + y1
    mu = _fdot(_fdot(y, seg) * (1.0 / RK_HEAD), seg_t)
    yc = y - mu
    var = _fdot(yc * yc, seg) * (1.0 / RK_HEAD)
    yn = yc * _fdot(lax.rsqrt(var + GN_EPS), seg_t) * ln_gain + ln_bias
    bonus = _fdot(_fdot(r * (kd0 + kd1) * r_k, seg), seg_t)
    return ((yn + bonus * v) * gate,)


def _s5_prep_fn(lr0, li0, ls0, lr1, li1, ls1, b_re, b_im):
    outs = []
    for lam_re, lam_im, ls in ((lr0, li0, ls0), (lr1, li1, ls1)):
        step = jnp.exp(ls)
        mag = jnp.exp(lam_re * step)
        lbar_re = mag * jnp.cos(lam_im * step)
        lbar_im = mag * jnp.sin(lam_im * step)
        den = lam_re * lam_re + lam_im * lam_im
        nr = lbar_re - 1.0
        coef_re = (nr * lam_re + lbar_im * lam_im) / den
        coef_im = (lbar_im * lam_re - nr * lam_im) / den
        outs += [lbar_re, lbar_im, coef_re * b_re - coef_im * b_im, coef_re * b_im + coef_im * b_re]
    return tuple(outs)


def _shift_rows(x, down):
    t = x.shape[0]
    rows = lax.broadcasted_iota(jnp.int32, x.shape, 0)
    if down:
        return jnp.where(rows >= 1, pltpu.roll(x, 1, 0), 0.0)
    return jnp.where(rows < t - 1, pltpu.roll(x, t - 1, 0), 0.0)


def _token_shift(p, mu_prev, mu_next):
    t, n = p.shape

    def body(p_ref, mp_ref, mn_ref, o_ref):
        x = p_ref[...]
        o_ref[...] = x + mp_ref[...] * (_shift_rows(x, True) - x) + mn_ref[...] * (_shift_rows(x, False) - x)

    col = pl.BlockSpec((t, 128), lambda j: (0, j))
    par = pl.BlockSpec((1, 128), lambda j: (0, j))
    return pl.pallas_call(
        body, name="token_shift", grid=(n // 128,), in_specs=[col, par, par], out_specs=col,
        out_shape=jax.ShapeDtypeStruct((t, n), F32), compiler_params=_params(("parallel",)),
    )(p, mu_prev, mu_next)


def _token_shift_bwd(p, mu_prev, mu_next, dps):
    t, n = p.shape

    def body(p_ref, mp_ref, mn_ref, d_ref, dp_ref, dmp_ref, dmn_ref):
        x, d, mp, mn = p_ref[...], d_ref[...], mp_ref[...], mn_ref[...]
        dp_ref[...] = d * (1.0 - mp - mn) + _shift_rows(d * mp, False) + _shift_rows(d * mn, True)
        dmp_ref[...] = jnp.sum(d * (_shift_rows(x, True) - x), axis=0, keepdims=True)
        dmn_ref[...] = jnp.sum(d * (_shift_rows(x, False) - x), axis=0, keepdims=True)

    col = pl.BlockSpec((t, 128), lambda j: (0, j))
    par = pl.BlockSpec((1, 128), lambda j: (0, j))
    return pl.pallas_call(
        body, name="token_shift_bwd", grid=(n // 128,), in_specs=[col, par, par, col],
        out_specs=[col, par, par],
        out_shape=[jax.ShapeDtypeStruct((t, n), F32), jax.ShapeDtypeStruct((1, n), F32),
                   jax.ShapeDtypeStruct((1, n), F32)],
        compiler_params=_params(("parallel",)),
    )(p, mu_prev, mu_next, dps)


N_SEG = 32
S5_BLOCKS = 32
S5_PER_IN = 4


def _scan_in_place(sr_ref, si_ref, ar, ai, carry_ref, reverse):
    seg_len = sr_ref.shape[0] // N_SEG
    ng = N_SEG // 8

    def rows(i, grp):
        first = (seg_len - 1 - i if reverse else i) + grp * 8 * seg_len
        return pl.ds(first, 8, stride=seg_len)

    zero = jnp.zeros((8, 128), F32)
    one = jnp.ones((8, 128), F32)

    def local(i, c):
        pr, pi = c[-2:]
        out = []
        for grp in range(ng):
            sr, si = c[2 * grp], c[2 * grp + 1]
            nr = ar * sr - ai * si + sr_ref[rows(i, grp), :]
            ni = ar * si + ai * sr + si_ref[rows(i, grp), :]
            sr_ref[rows(i, grp), :] = nr
            si_ref[rows(i, grp), :] = ni
            out += [nr, ni]
        return tuple(out) + (ar * pr - ai * pi, ar * pi + ai * pr)

    ends = lax.fori_loop(0, seg_len, local, (zero,) * (2 * ng) + (one, zero))
    qr, qi = ends[-2][0:1], ends[-1][0:1]
    order = list(range(N_SEG - 1, -1, -1)) if reverse else list(range(N_SEG))
    cr = jnp.zeros((1, 128), F32)
    ci = jnp.zeros((1, 128), F32)
    for j in order:
        carry_ref[j:j + 1, :] = cr
        carry_ref[N_SEG + j:N_SEG + j + 1, :] = ci
        grp, sub = divmod(j, 8)
        lr, li = ends[2 * grp][sub:sub + 1], ends[2 * grp + 1][sub:sub + 1]
        cr, ci = lr + qr * cr - qi * ci, li + qr * ci + qi * cr
    carries = [(carry_ref[8 * grp:8 * grp + 8, :], carry_ref[N_SEG + 8 * grp:N_SEG + 8 * grp + 8, :])
               for grp in range(ng)]

    def fix(i, c):
        pr, pi = c
        npr, npi = ar * pr - ai * pi, ar * pi + ai * pr
        for grp in range(ng):
            cr8, ci8 = carries[grp]
            sr_ref[rows(i, grp), :] = sr_ref[rows(i, grp), :] + npr * cr8 - npi * ci8
            si_ref[rows(i, grp), :] = si_ref[rows(i, grp), :] + npr * ci8 + npi * cr8
        return npr, npi

    lax.fori_loop(0, seg_len, fix, (one, zero))


def _dot_bf16(a, b, dims=(((1,), (0,)), ((), ()))):
    return lax.dot_general(a.astype(BF16), b.astype(BF16), dims, preferred_element_type=F32)


NT_DIMS = (((1,), (1,)), ((), ()))
TN_DIMS = (((0,), (0,)), ((), ()))


def _s5_specs(t):
    blk = pl.BlockSpec((None, t, 128), lambda i, q: (S5_PER_IN * i + q, 0, 0))
    mat = pl.BlockSpec((None, 128, 128), lambda i, q: (S5_PER_IN * i + q, 0, 0))
    vec = pl.BlockSpec((None, 1, 128), lambda i, q: (S5_PER_IN * i + q, 0, 0))
    chan = pl.BlockSpec((t, 128), lambda i, q: (0, i))
    return blk, mat, vec, chan


S5_GRID = (S5_BLOCKS // S5_PER_IN, S5_PER_IN)


def _s5_forward(name, u, b_re, b_im, l_re, l_im, reverse, other=None, c_re=None, c_im_neg=None):
    t = u.shape[0]
    project = other is not None
    blk, mat, vec, chan = _s5_specs(t)

    def body(*refs):
        u_ref, br_ref, bi_ref, lr_ref, li_ref = refs[:5]
        if project:
            or_ref, oi_ref, cr_ref, ci_ref, sr_ref, si_ref, y_ref, carry_ref = refs[5:]
        else:
            sr_ref, si_ref, carry_ref = refs[5:]
        uv = u_ref[...]
        sr_ref[...] = _dot_bf16(uv, br_ref[...])
        si_ref[...] = _dot_bf16(uv, bi_ref[...])
        ar = jnp.broadcast_to(lr_ref[...], (8, 128))
        ai = jnp.broadcast_to(li_ref[...], (8, 128))
        _scan_in_place(sr_ref, si_ref, ar, ai, carry_ref, reverse)
        if project:
            y = (_dot_bf16(sr_ref[...] + or_ref[...], cr_ref[...])
                 + _dot_bf16(si_ref[...] + oi_ref[...], ci_ref[...]))

            @pl.when(pl.program_id(1) == 0)
            def _():
                y_ref[...] = y

            @pl.when(pl.program_id(1) != 0)
            def _():
                y_ref[...] += y

    state = jax.ShapeDtypeStruct((S5_BLOCKS, t, 128), F32)
    ins = [u, b_re, b_im, l_re, l_im] + ([other[0], other[1], c_re, c_im_neg] if project else [])
    in_specs = [chan, mat, mat, vec, vec] + ([blk, blk, mat, mat] if project else [])
    return pl.pallas_call(
        body, name=name, grid=S5_GRID, in_specs=in_specs,
        out_specs=[blk, blk] + ([chan] if project else []),
        out_shape=[state, state] + ([jax.ShapeDtypeStruct((t, S5_WIDTH), F32)] if project else []),
        scratch_shapes=[pltpu.VMEM((2 * N_SEG, 128), F32)],
        compiler_params=_params(("arbitrary", "arbitrary")),
    )(*ins)


def _s5_backward(name, dy, u, du_in, states, other, b_re, b_im, c_re, c_im_neg, l_re, l_im, reverse):
    t = u.shape[0]
    with_c = other is not None
    blk, mat, vec, chan = _s5_specs(t)

    def body(*refs):
        dy_ref, u_ref, du_in_ref, sr_ref, si_ref = refs[:5]
        pos = 5
        if with_c:
            or_ref, oi_ref = refs[5:7]
            pos = 7
        br_ref, bi_ref, cr_ref, ci_ref, lr_ref, li_ref = refs[pos:pos + 6]
        outs = refs[pos + 6:]
        du_ref, dbr_ref, dbi_ref, dlr_ref, dli_ref = outs[:5]
        lam_r, lam_i, carry_ref = outs[-3:]
        dyv, uv = dy_ref[...], u_ref[...]
        lam_r[...] = _dot_bf16(dyv, cr_ref[...], NT_DIMS)
        lam_i[...] = _dot_bf16(dyv, ci_ref[...], NT_DIMS)
        ar = jnp.broadcast_to(lr_ref[...], (8, 128))
        ai = -jnp.broadcast_to(li_ref[...], (8, 128))
        _scan_in_place(lam_r, lam_i, ar, ai, carry_ref, not reverse)
        lr, li = lam_r[...], lam_i[...]
        pr, pi = _shift_rows(sr_ref[...], not reverse), _shift_rows(si_ref[...], not reverse)
        dlr_ref[...] = jnp.sum(lr * pr + li * pi, axis=0, keepdims=True)
        dli_ref[...] = jnp.sum(li * pr - lr * pi, axis=0, keepdims=True)
        dbr_ref[...] = _dot_bf16(uv, lr, TN_DIMS)
        dbi_ref[...] = _dot_bf16(uv, li, TN_DIMS)
        du = _dot_bf16(lr, br_ref[...], NT_DIMS) + _dot_bf16(li, bi_ref[...], NT_DIMS)

        @pl.when(pl.program_id(1) == 0)
        def _():
            du_ref[...] = du_in_ref[...] + du

        @pl.when(pl.program_id(1) != 0)
        def _():
            du_ref[...] += du

        if with_c:
            dcr_ref, dci_ref = outs[5:7]
            dcr_ref[...] = _dot_bf16(sr_ref[...] + or_ref[...], dyv, TN_DIMS)
            dci_ref[...] = _dot_bf16(si_ref[...] + oi_ref[...], dyv, TN_DIMS)

    mats = jax.ShapeDtypeStruct((S5_BLOCKS, 128, 128), F32)
    vecs = jax.ShapeDtypeStruct((S5_BLOCKS, 1, 128), F32)
    ins = [dy, u, du_in, states[0], states[1]] + ([other[0], other[1]] if with_c else [])
    ins += [b_re, b_im, c_re, c_im_neg, l_re, l_im]
    in_specs = [chan, chan, chan, blk, blk] + ([blk, blk] if with_c else []) + [mat] * 4 + [vec] * 2
    return pl.pallas_call(
        body, name=name, grid=S5_GRID, in_specs=in_specs,
        out_specs=[chan, mat, mat, vec, vec] + ([mat, mat] if with_c else []),
        out_shape=[jax.ShapeDtypeStruct((t, S5_WIDTH), F32), mats, mats, vecs, vecs] + ([mats, mats] if with_c else []),
        scratch_shapes=[pltpu.VMEM((t, 128), F32), pltpu.VMEM((t, 128), F32), pltpu.VMEM((2 * N_SEG, 128), F32)],
        compiler_params=_params(("arbitrary", "arbitrary")),
    )(*ins)


def _ein(passes, spec, a, b):
    if passes == 6:
        return jnp.einsum(spec, a, b, precision=HI, preferred_element_type=F32)
    a_hi, b_hi = a.astype(BF16), b.astype(BF16)
    if passes == 1:
        return jnp.einsum(spec, a_hi, b_hi, preferred_element_type=F32)
    a_lo = (a - a_hi.astype(F32)).astype(BF16)
    b_lo = (b - b_hi.astype(F32)).astype(BF16)
    cross = jnp.einsum(spec, a_hi, b_lo, preferred_element_type=F32)
    if spec.startswith('hik'):
        m = a.shape[1]
        stacked = jnp.einsum(spec, jnp.concatenate([a_hi, a_lo], axis=1), b_hi, preferred_element_type=F32)
        return stacked[:, :m] + stacked[:, m:] + cross
    return (jnp.einsum(spec, a_hi, b_hi, preferred_element_type=F32) + cross
            + jnp.einsum(spec, a_lo, b_hi, preferred_element_type=F32))


@jax.custom_vjp
def _tri_mm(tri, tri_t, z):
    return jnp.einsum('hik,hkj->hij', tri, z, precision=HI, preferred_element_type=F32)


def _tri_mm_bwd(res, g):
    tri, tri_t = res
    return jnp.zeros_like(tri), jnp.zeros_like(tri_t), _tri_mm(tri_t, tri, g)


_tri_mm.defvjp(lambda tri, tri_t, z: (_tri_mm(tri, tri_t, z), (tri, tri_t)), _tri_mm_bwd)


def _chunk_cumsum(lw, incl, incl_t):
    shape = (lw.shape[0],) + incl.shape
    return _tri_mm(jnp.broadcast_to(incl.astype(F32), shape), jnp.broadcast_to(incl_t.astype(F32), shape), lw)


@functools.partial(jax.custom_vjp, nondiff_argnums=(0,))
def _bmm(p, a, b):
    return _ein(p, 'hik,hkj->hij', a, b)


@functools.partial(jax.custom_vjp, nondiff_argnums=(0,))
def _bmm_nt(p, a, b):
    return _ein(p, 'hik,hjk->hij', a, b)


@functools.partial(jax.custom_vjp, nondiff_argnums=(0,))
def _bmm_tn(p, a, b):
    return _ein(p, 'hki,hkj->hij', a, b)


_bmm.defvjp(lambda p, a, b: (_bmm(p, a, b), (a, b)),
            lambda p, res, g: (_bmm_nt(p, g, res[1]), _bmm_tn(p, res[0], g)))
_bmm_nt.defvjp(lambda p, a, b: (_bmm_nt(p, a, b), (a, b)),
               lambda p, res, g: (_bmm(p, g, res[1]), _bmm_tn(p, g, res[0])))
_bmm_tn.defvjp(lambda p, a, b: (_bmm_tn(p, a, b), (a, b)),
               lambda p, res, g: (_bmm_nt(p, res[1], g), _bmm(p, res[0], g)))


@jax.custom_vjp
def _split_rows(x):
    c = x.shape[1] // 2
    return x[:, :c], x[:, c:]


_split_rows.defvjp(lambda x: (_split_rows(x), None), lambda _, g: (jnp.concatenate(g, axis=1),))


def _stack_rows(a, b):
    return jnp.concatenate([a, b], axis=1)


def _rk_chunk(s0, r, lw, k, v, kk, a, reverse):
    h, c, n = r.shape
    row = lax.broadcasted_iota(jnp.int32, (c, c), 0)
    col = lax.broadcasted_iota(jnp.int32, (c, c), 1)
    incl = (row <= col) if reverse else (row >= col)
    strict = (row < col) if reverse else (row > col)
    cum = _chunk_cumsum(lw, incl, (row >= col) if reverse else (row <= col))
    g_in = jnp.exp(cum)
    g_inv = jnp.exp(-cum)
    kap = kk * jnp.exp(cum - lw)
    beta = kk * a * g_inv
    kt = k * g_inv
    rt = r * g_in
    pm, ps = RK_PASSES, RK_SOLVE_PASSES
    both = _stack_rows(kap, rt)
    kap_beta, rt_beta = _split_rows(_bmm_nt(ps, both, beta))
    kap_kt, rt_kt = _split_rows(_bmm_nt(pm, both, kt))
    kap_s0, rt_s0 = _split_rows(_bmm_nt(pm, both, s0))
    l_mat = jnp.where(strict, kap_beta, 0.0)
    rhs = kap_s0 + _bmm(pm, jnp.where(strict, kap_kt, 0.0), v)
    x = -l_mat
    inv = jnp.where(row == col, 1.0, 0.0) + x
    power = _bmm(ps, x, x)
    span = 2
    while 2 * span < c:
        step, power = _split_rows(_bmm(ps, _stack_rows(inv, power), power))
        inv = inv + step
        span *= 2
    inv = inv + _bmm(ps, inv, power)
    u = _bmm(ps, inv, rhs)
    y = rt_s0 + _bmm(pm, jnp.where(incl, rt_kt, 0.0), v) - _bmm(pm, jnp.where(incl, rt_beta, 0.0), u)
    s1 = ((s0 + _bmm_tn(pm, _stack_rows(v, -u), _stack_rows(kt, beta)))
          * jnp.exp(jnp.sum(lw, axis=1, keepdims=True)))
    return y, s1


def _rk_core_fwd(name, r, lw, k, v, kk, a, reverse, chunk):
    h, t, n = r.shape
    nc = t // chunk

    def idx(i):
        return nc - 1 - i if reverse else i

    def body(r_ref, lw_ref, k_ref, v_ref, kk_ref, a_ref, y_ref, ck_ref, s_ref):
        @pl.when(pl.program_id(0) == 0)
        def _():
            s_ref[...] = jnp.zeros_like(s_ref)

        s0 = s_ref[...]
        ck_ref[0] = s0
        y, s1 = _rk_chunk(s0, r_ref[...], lw_ref[...], k_ref[...], v_ref[...], kk_ref[...], a_ref[...], reverse)
        y_ref[...] = y
        s_ref[...] = s1

    blk = pl.BlockSpec((h, chunk, n), lambda i: (0, idx(i), 0))
    return pl.pallas_call(
        body, name=name, grid=(nc,), in_specs=[blk] * 6,
        out_specs=[blk, pl.BlockSpec((1, h, n, n), lambda i: (idx(i), 0, 0, 0))],
        out_shape=[jax.ShapeDtypeStruct((h, t, n), F32), jax.ShapeDtypeStruct((nc, h, n, n), F32)],
        scratch_shapes=[pltpu.VMEM((h, n, n), F32)],
        compiler_params=_params(("arbitrary",)),
    )(r, lw, k, v, kk, a)


def _rk_core_bwd(name, r, lw, k, v, kk, a, ck, dy, reverse, chunk):
    h, t, n = r.shape
    nc = t // chunk

    def idx(i):
        return i if reverse else nc - 1 - i

    def body(r_ref, lw_ref, k_ref, v_ref, kk_ref, a_ref, ck_ref, dy_ref, *rest):
        out_refs, ds_ref = rest[:6], rest[6]

        @pl.when(pl.program_id(0) == 0)
        def _():
            ds_ref[...] = jnp.zeros_like(ds_ref)

        fn = functools.partial(_rk_chunk, reverse=reverse)
        _, vjp = jax.vjp(fn, ck_ref[0], r_ref[...], lw_ref[...], k_ref[...], v_ref[...], kk_ref[...], a_ref[...])
        grads = vjp((dy_ref[...], ds_ref[...]))
        ds_ref[...] = grads[0]
        for o_ref, g in zip(out_refs, grads[1:]):
            o_ref[...] = g

    blk = pl.BlockSpec((h, chunk, n), lambda i: (0, idx(i), 0))
    return pl.pallas_call(
        body, name=name, grid=(nc,),
        in_specs=[blk] * 6 + [pl.BlockSpec((1, h, n, n), lambda i: (idx(i), 0, 0, 0)), blk],
        out_specs=[blk] * 6,
        out_shape=[jax.ShapeDtypeStruct((h, t, n), F32)] * 6,
        scratch_shapes=[pltpu.VMEM((h, n, n), F32)],
        compiler_params=_params(("arbitrary",)),
    )(r, lw, k, v, kk, a, ck, dy)


def _to_heads(x):
    return jnp.transpose(x.reshape(x.shape[0], RK_HEADS, RK_HEAD), (1, 0, 2))


def _from_heads(x):
    return jnp.transpose(x, (1, 0, 2)).reshape(x.shape[1], RK_WIDTH)


def _s5_band_place():
    return jax.nn.one_hot(jnp.arange(S5_BLOCKS) % S5_PER_IN, S5_PER_IN, dtype=F32)


def _s5_in_blocks(bbar):
    b = jnp.transpose(bbar.reshape(S5_BLOCKS, 2, S5_STATE, S5_GROUP), (0, 1, 3, 2))
    band = jnp.einsum('jghp,gk->jghkp', b, jnp.eye(2, dtype=F32)).reshape(S5_BLOCKS, 32, 128)
    return jnp.einsum('jrc,jq->jqrc', band, _s5_band_place()).reshape(S5_BLOCKS, 128, 128)


def _s5_in_unblock(mats):
    band = jnp.einsum('jqrc,jq->jrc', mats.reshape(S5_BLOCKS, S5_PER_IN, 32, 128), _s5_band_place())
    diag = jnp.einsum('jghgp->jghp', band.reshape(S5_BLOCKS, 2, S5_GROUP, 2, S5_STATE))
    return jnp.transpose(diag, (0, 1, 3, 2)).reshape(S5_CH, S5_GROUP)


def _s5_out_blocks(c):
    ct = jnp.transpose(c.reshape(S5_BLOCKS, 2, S5_GROUP, S5_STATE), (0, 1, 3, 2))
    band = jnp.einsum('jgph,gk->jgpkh', ct, jnp.eye(2, dtype=F32)).reshape(S5_BLOCKS, 128, 32)
    return jnp.einsum('jrc,jq->jrqc', band, _s5_band_place()).reshape(S5_BLOCKS, 128, 128)


def _s5_out_unblock(mats):
    band = jnp.einsum('jrqc,jq->jrc', mats.reshape(S5_BLOCKS, 128, S5_PER_IN, 32), _s5_band_place())
    diag = jnp.einsum('jgpgh->jgph', band.reshape(S5_BLOCKS, 2, S5_STATE, 2, S5_GROUP))
    return jnp.transpose(diag, (0, 1, 3, 2)).reshape(S5_GROUPS, S5_GROUP, S5_STATE)


def _head_indicator():
    ch = lax.broadcasted_iota(jnp.int32, (RK_WIDTH, 128), 0) // RK_HEAD
    hd = lax.broadcasted_iota(jnp.int32, (RK_WIDTH, 128), 1)
    seg = (ch == hd).astype(F32)
    return seg, seg.T


def _add_epilogue(acc, e):
    return (acc + e,)


def _local_step(x, target, mod, wt, chunk=RK_CHUNK):
    t = x.shape[0]
    sh1, sc1, gt1, sh2, sc2, gt2 = mod
    seg, seg_t = _head_indicator()
    g = {}

    (h1,) = _rowwise("norm1", _norm_mod_fn, [x], [wt["norm1_gain"], sc1, sh1], [(D_MODEL, BF16)], 256)
    proj = _matmul("proj", h1, wt["w_in"])
    u, p = proj[:, :S5_WIDTH], proj[:, S5_WIDTH:]
    ps = _token_shift(p, wt["mu_prev"], wt["mu_next"])
    r, k, v = ps[:, :1024], ps[:, 1024:2048], ps[:, 2048:3072]
    wdn, adn, gdn = ps[:, 3072:3200], ps[:, 3200:3328], ps[:, 3328:RK_PAD]

    prep_rows = [wt["lam_re"][0], wt["lam_im"][0], wt["log_step"][0], wt["lam_re"][1], wt["lam_im"][1],
                 wt["log_step"][1], wt["b_re"], wt["b_im"]]
    col1, col16 = (1, F32), (S5_GROUP, F32)
    prep = _rowwise("s5_prep", _s5_prep_fn, prep_rows, [], [col1, col1, col16, col16] * 2, 512)
    lbar = [tuple(prep[4 * d + q].reshape(S5_BLOCKS, 1, 128) for q in range(2)) for d in range(2)]
    b_blk = [tuple(_s5_in_blocks(prep[4 * d + 2 + q]) for q in range(2)) for d in range(2)]
    c_blk = (_s5_out_blocks(wt["c_re"]), -_s5_out_blocks(wt["c_im"]))
    state0 = _s5_forward("s5_fwd0", u, *b_blk[0], *lbar[0], reverse=False)
    s1_re, s1_im, ylin = _s5_forward("s5_fwd1", u, *b_blk[1], *lbar[1], reverse=True, other=state0,
                                     c_re=c_blk[0], c_im_neg=c_blk[1])
    states = [tuple(state0), (s1_re, s1_im)]
    s5_par = [wt["s5_d"], wt["s5_w_glu"], wt["s5_b_glu"]]
    (o_s5,) = _rowwise("s5_out", _s5_out_fn, [ylin, u], s5_par, [(S5_WIDTH, BF16)], 256)

    pre_par = [wt["w0"][0], wt["w0"][1], wt["w_up"][0], wt["w_up"][1], wt["a0"][0], wt["a0"][1],
               wt["a_up"][0], wt["a_up"][1], wt["g_up"], wt["k_k"], wt["k_a"]]
    pre = _rowwise("rk_pre", _rk_pre_fn, [k, wdn, adn, gdn], pre_par + [seg, seg_t], [(RK_WIDTH, F32)] * 8, 256)
    kk, lw, kd, act, gate = pre[0], pre[1:3], pre[3:5], pre[5:7], pre[7]
    rh, vh, kkh = _to_heads(r), _to_heads(v), _to_heads(kk)
    core_in, ys, cks = [], [], []
    for d in range(2):
        ops = (rh, _to_heads(lw[d]), _to_heads(kd[d]), vh, kkh, _to_heads(act[d]))
        y_h, ck = _rk_core_fwd(f"rk_core{d}", *ops, reverse=(d == 1), chunk=min(chunk, t))
        core_in.append(ops)
        ys.append(_from_heads(y_h))
        cks.append(ck)
    post_rows = [ys[0], ys[1], r, v, kd[0], kd[1], gate]
    post_par = [wt["ln_gain"], wt["ln_bias"], wt["r_k"]]
    (o_rk,) = _rowwise("rk_post", _rk_post_fn, post_rows, post_par + [seg, seg_t], [(RK_WIDTH, BF16)], 256)

    o = jnp.concatenate([o_s5, o_rk], axis=1)
    mixed = _matmul("mix_out", o, wt["w_out"])
    n2_par = [gt1, wt["norm2_gain"], sc2, sh2]
    x1, h2 = _rowwise("norm2", _resid_norm_mod_fn, [x, mixed], n2_par, [(D_MODEL, F32), (D_MODEL, BF16)], 256)
    f1, hid = _matmul("ffn1", h2, wt["ffn_w1"], out_dtypes=(F32, BF16),
                      epilogue=lambda acc: (acc, jnp.square(jnp.maximum(acc, 0.0))))
    ffn = _matmul("ffn2", hid, wt["ffn_w2"])

    ones = jnp.ones((t, 1), F32)
    loss_rows, dx1, dffn, g_gt2, g["final_gain"] = _rowwise_vjp(
        "loss", _loss_fn, [x1, ffn, target], [gt2, wt["final_gain"]], [[ones]], [0, 1], [0, 1], 256, emit=(0,))
    df1 = _matmul("ffn2_dx", dffn, wt["ffn_w2"], tb=True, extras=(f1,), out_dtypes=(BF16,),
                  epilogue=lambda acc, f: (acc * (2.0 * jnp.maximum(f, 0.0)),))
    g["ffn_w2"] = _matmul("ffn2_dw", hid, dffn, ta=True)
    dh2 = _matmul("ffn1_dx", df1, wt["ffn_w1"], tb=True)
    g["ffn_w1"] = _matmul("ffn1_dw", h2, df1, ta=True)
    dx_a, dmixed, g_gt1, g["norm2_gain"], g_sc2, g_sh2 = _rowwise_vjp(
        "norm2_bwd", _resid_norm_mod_fn, [x, mixed], n2_par, [[dx1], [dh2]], [0, 1], [0, 1, 2, 3], 256)
    do = _matmul("mix_out_dx", dmixed, wt["w_out"], tb=True)
    g["w_out"] = _matmul("mix_out_dw", o, dmixed, ta=True)
    do_s5, do_rk = do[:, :S5_WIDTH], do[:, S5_WIDTH:]

    dylin, du, g["s5_d"], g["s5_w_glu"], g["s5_b_glu"] = _rowwise_vjp(
        "s5_out_bwd", _s5_out_fn, [ylin, u], s5_par, [[do_s5]], [0, 1], [0, 1, 2], 256)
    prep_cts = []
    for d in range(2):
        res = _s5_backward(f"s5_bwd{d}", dylin, u, du, states[d], states[1] if d == 0 else None,
                           *b_blk[d], *c_blk, *lbar[d], reverse=(d == 1))
        du, db_re, db_im, dl_re, dl_im = res[:5]
        if d == 0:
            g["c_re"], g["c_im"] = _s5_out_unblock(res[5]), -_s5_out_unblock(res[6])
        prep_cts += [[dl_re.reshape(S5_CH, 1)], [dl_im.reshape(S5_CH, 1)], [_s5_in_unblock(db_re)],
                     [_s5_in_unblock(db_im)]]
    pg = _rowwise_vjp("s5_prep_bwd", _s5_prep_fn, prep_rows, [], prep_cts, list(range(8)), [], 512)
    g["lam_re"], g["lam_im"], g["log_step"] = (pg[0], pg[3]), (pg[1], pg[4]), (pg[2], pg[5])
    g["b_re"], g["b_im"] = pg[6], pg[7]

    pb = _rowwise_vjp("rk_post_bwd", _rk_post_fn, post_rows, post_par, [[do_rk]], [0, 2, 3, 4, 5, 6], [0, 1, 2],
                      128, consts=[seg, seg_t])
    dy, dr_b, dv_b, dkd_b, dgate = pb[0], pb[1], pb[2], pb[3:5], pb[5]
    g["ln_gain"], g["ln_bias"], g["r_k"] = pb[6], pb[7], pb[8]
    dyh = _to_heads(dy)
    cg = []
    for d in range(2):
        grads = _rk_core_bwd(f"rk_core{d}_bwd", *core_in[d], cks[d], dyh, reverse=(d == 1), chunk=min(chunk, t))
        cg.append([_from_heads(q) for q in grads])
    pre_cts = [[cg[0][4], cg[1][4]], [cg[0][1]], [cg[1][1]], [cg[0][2], dkd_b[0]], [cg[1][2], dkd_b[1]],
               [cg[0][5]], [cg[1][5]], [dgate]]
    qb = _rowwise_vjp("rk_pre_bwd", _rk_pre_fn, [k, wdn, adn, gdn], pre_par, pre_cts, [0, 1, 2, 3],
                      list(range(11)), 128, consts=[seg, seg_t])
    dk, dwdn, dadn, dgdn = qb[:4]
    g["w0"], g["w_up"], g["a0"], g["a_up"] = (qb[4], qb[5]), (qb[6], qb[7]), (qb[8], qb[9]), (qb[10], qb[11])
    g["g_up"], g["k_k"], g["k_a"] = qb[12], qb[13], qb[14]
    dr, dv = _rowwise("rk_sum", lambda a, b, c, e, f, h: (a + b + c, e + f + h),
                      [cg[0][0], cg[1][0], dr_b, cg[0][3], cg[1][3], dv_b], [], [(RK_WIDTH, F32)] * 2, 256)
    dps = jnp.concatenate([dr, dk, dv, dwdn, dadn, dgdn], axis=1)
    dp, g["mu_prev"], g["mu_next"] = _token_shift_bwd(p, wt["mu_prev"], wt["mu_next"], dps)

    dproj = jnp.concatenate([du, dp], axis=1)
    dh1 = _matmul("proj_dx", dproj, wt["w_in"], tb=True)
    g["w_in"] = _matmul("proj_dw", h1, dproj, ta=True)
    grad_x, g["norm1_gain"], g_sc1, g_sh1 = _rowwise_vjp(
        "norm1_bwd", _norm_mod_fn, [x], [wt["norm1_gain"], sc1, sh1], [[dh1]], [0], [0, 1, 2], 256,
        addends={0: dx_a})
    g["mod"] = [g_sh1, g_sc1, g_gt1, g_sh2, g_sc2, g_gt2]
    return loss_rows, grad_x, g


CHIP_PEERS = ((1, 0, 0), (0, 1, 0), (1, 1, 0))
ALL_PEERS = ((0, 0, 1), (0, 1, 0), (0, 1, 1), (1, 0, 0), (1, 0, 1), (1, 1, 0), (1, 1, 1))
CORE_PEER = ((0, 0, 1),)


def _exchange(name, arrays, peers, n_slots, scatter=False):
    na, nm = len(arrays), len(peers)

    def ident(px, py, pc):
        return {8: 4 * px + 2 * py + pc, 4: 2 * px + py, 2: pc}[n_slots]

    def body(*refs):
        in_refs, out_refs = refs[:na], refs[na:2 * na]
        send_sems, recv_sems, local_sems = refs[2 * na:]
        x, y, c = lax.axis_index("x"), lax.axis_index("y"), lax.axis_index("c")
        me = ident(x, y, c)
        started = []
        for i in range(na):
            own = in_refs[i].at[me] if scatter else in_refs[i]
            local = pltpu.make_async_copy(own, out_refs[i].at[me], local_sems.at[i])
            local.start()
            started.append(local)
            for j, (fx, fy, fc) in enumerate(peers):
                px, py, pc = (1 - x if fx else x), (1 - y if fy else y), (1 - c if fc else c)
                src = in_refs[i].at[ident(px, py, pc)] if scatter else in_refs[i]
                copy = pltpu.make_async_remote_copy(
                    src_ref=src, dst_ref=out_refs[i].at[me],
                    send_sem=send_sems.at[i * nm + j], recv_sem=recv_sems.at[i * nm + j],
                    device_id=(px, py, pc), device_id_type=pl.DeviceIdType.MESH)
                copy.start()
                started.append(copy)
        for copy in started:
            copy.wait()

    any_spec = pl.BlockSpec(memory_space=pl.ANY)
    out_shape = [jax.ShapeDtypeStruct(((n_slots,) + a.shape[1:]) if scatter else ((n_slots,) + a.shape), a.dtype)
                 for a in arrays]
    return pl.pallas_call(
        body, name=name, in_specs=[any_spec] * na, out_specs=[any_spec] * na, out_shape=out_shape,
        scratch_shapes=[pltpu.SemaphoreType.DMA((na * nm,)), pltpu.SemaphoreType.DMA((na * nm,)),
                        pltpu.SemaphoreType.DMA((na,))],
    )(*arrays)


def _gather_halves(name, arrays):
    na = len(arrays)
    chips = ((1, 0), (0, 1), (1, 1))

    def body(*refs):
        in_refs, out_refs = refs[:na], refs[na:2 * na]
        ici_send, ici_recv, d2d_send, d2d_recv, local_sems = refs[2 * na:]
        x, y, c = lax.axis_index("x"), lax.axis_index("y"), lax.axis_index("c")
        me = 2 * x + y
        peers = [((1 - x if fx else x), (1 - y if fy else y)) for fx, fy in chips]
        pending = []
        for i in range(na):
            half = arrays[i].shape[0] // 2
            mine = pl.ds(pl.multiple_of(c * half, 8), half)
            theirs = pl.ds(pl.multiple_of((1 - c) * half, 8), half)
            local = pltpu.make_async_copy(in_refs[i], out_refs[i].at[me], local_sems.at[i])
            local.start()
            pending.append(local.wait)
            for j, (px, py) in enumerate(peers):
                k = len(chips) * i + j
                out = pltpu.make_async_remote_copy(
                    src_ref=in_refs[i].at[mine], dst_ref=out_refs[i].at[me, mine],
                    send_sem=ici_send.at[k], recv_sem=ici_recv.at[k],
                    device_id=(px, py, c), device_id_type=pl.DeviceIdType.MESH)
                out.start()
                pending.append(out.wait_send)
        for i in range(na):
            half = arrays[i].shape[0] // 2
            mine = pl.ds(pl.multiple_of(c * half, 8), half)
            theirs = pl.ds(pl.multiple_of((1 - c) * half, 8), half)
            for j, (px, py) in enumerate(peers):
                k = len(chips) * i + j
                landed = out_refs[i].at[2 * px + py, mine]
                pltpu.make_async_remote_copy(
                    src_ref=in_refs[i].at[mine], dst_ref=landed, send_sem=ici_send.at[k], recv_sem=ici_recv.at[k],
                    device_id=(px, py, c), device_id_type=pl.DeviceIdType.MESH).wait_recv()
                passed = pltpu.make_async_remote_copy(
                    src_ref=landed, dst_ref=landed, send_sem=d2d_send.at[k], recv_sem=d2d_recv.at[k],
                    device_id=(x, y, 1 - c), device_id_type=pl.DeviceIdType.MESH)
                passed.start()
                pending.append(passed.wait_send)
                from_sibling = out_refs[i].at[2 * px + py, theirs]
                pending.append(pltpu.make_async_remote_copy(
                    src_ref=from_sibling, dst_ref=from_sibling, send_sem=d2d_send.at[k], recv_sem=d2d_recv.at[k],
                    device_id=(x, y, 1 - c), device_id_type=pl.DeviceIdType.MESH).wait_recv)
        for wait in pending:
            wait()

    any_spec = pl.BlockSpec(memory_space=pl.ANY)
    n_sem = na * len(chips)
    return pl.pallas_call(
        body, name=name, in_specs=[any_spec] * na, out_specs=[any_spec] * na,
        out_shape=[jax.ShapeDtypeStruct((N_CHIPS,) + a.shape, a.dtype) for a in arrays],
        scratch_shapes=[pltpu.SemaphoreType.DMA((n_sem,))] * 4 + [pltpu.SemaphoreType.DMA((na,))],
    )(*arrays)


def _send_other_half(name, arrays):
    na = len(arrays)

    def body(*refs):
        in_refs, out_refs, send_sems, recv_sems = refs[:na], refs[na:2 * na], refs[-2], refs[-1]
        x, y, c = lax.axis_index("x"), lax.axis_index("y"), lax.axis_index("c")
        copies = []
        for i in range(na):
            half = arrays[i].shape[1] // 2
            theirs = pl.ds(pl.multiple_of((1 - c) * half, 8), half)
            copy = pltpu.make_async_remote_copy(
                src_ref=in_refs[i].at[:, theirs], dst_ref=out_refs[i], send_sem=send_sems.at[i],
                recv_sem=recv_sems.at[i], device_id=(x, y, 1 - c), device_id_type=pl.DeviceIdType.MESH)
            copy.start()
            copies.append(copy)
        for copy in copies:
            copy.wait()

    any_spec = pl.BlockSpec(memory_space=pl.ANY)
    return pl.pallas_call(
        body, name=name, in_specs=[any_spec] * na, out_specs=[any_spec] * na,
        out_shape=[jax.ShapeDtypeStruct((a.shape[0], a.shape[1] // 2, a.shape[2]), a.dtype) for a in arrays],
        scratch_shapes=[pltpu.SemaphoreType.DMA((na,)), pltpu.SemaphoreType.DMA((na,))],
    )(*arrays)


def _adam_math(w, g, m, v):
    m = ADAM_B1 * m + (1.0 - ADAM_B1) * g
    v = ADAM_B2 * v + (1.0 - ADAM_B2) * jnp.square(g)
    m_hat = m / (1.0 - ADAM_B1 ** ADAM_STEP)
    v_hat = v / (1.0 - ADAM_B2 ** ADAM_STEP)
    delta = -ADAM_LR * (m_hat / (jnp.sqrt(v_hat) + ADAM_EPS) + ADAM_WD * w)
    return delta, m, v


def _row_tile(r):
    return _tile(r, (256, 128, 64, 32, 16, 8))


def _sum_parts(name, parts):
    n, r, c = parts.shape
    tr = _row_tile(r)

    def body(p_ref, o_ref):
        tot = p_ref[0].astype(F32)
        for i in range(1, n):
            tot = tot + p_ref[i].astype(F32)
        o_ref[...] = tot

    return pl.pallas_call(
        body, name=name, grid=(r // tr,), in_specs=[pl.BlockSpec((n, tr, c), lambda i: (0, i, 0))],
        out_specs=pl.BlockSpec((tr, c), lambda i: (i, 0)), out_shape=jax.ShapeDtypeStruct((r, c), F32),
        compiler_params=_params(("parallel",)),
    )(parts)


def _adamw(name, w, parts, m, v):
    n, r, c = parts.shape
    tr = _row_tile(r)

    def body(w_ref, p_ref, m_ref, v_ref, g_ref, d_ref, nm_ref, nv_ref):
        g = p_ref[0]
        for i in range(1, n):
            g = g + p_ref[i]
        delta, nm, nv = _adam_math(w_ref[...], g, m_ref[...], v_ref[...])
        g_ref[...], d_ref[...], nm_ref[...], nv_ref[...] = g, delta, nm, nv

    blk = pl.BlockSpec((tr, c), lambda i: (i, 0))
    return pl.pallas_call(
        body, name=name, grid=(r // tr,),
        in_specs=[blk, pl.BlockSpec((n, tr, c), lambda i: (0, i, 0)), blk, blk], out_specs=[blk] * 4,
        out_shape=[jax.ShapeDtypeStruct((r, c), F32)] * 4, compiler_params=_params(("parallel",)),
    )(w, parts, m, v)


def _ada_w_update(act_t, dmod, w, m, v):
    r, c = w.shape
    nb = act_t.shape[1]
    tr, tc = 256, 1024

    def body(a_ref, d_ref, w_ref, m_ref, v_ref, g_ref, dl_ref, nm_ref, nv_ref):
        a, dm = a_ref[...], d_ref[...]
        g = a[:, 0:1] * dm[0:1, :]
        for b in range(1, nb):
            g = g + a[:, b:b + 1] * dm[b:b + 1, :]
        delta, nm, nv = _adam_math(w_ref[...], g, m_ref[...], v_ref[...])
        g_ref[...], dl_ref[...], nm_ref[...], nv_ref[...] = g, delta, nm, nv

    blk = pl.BlockSpec((tr, tc), lambda i, j: (i, j))
    return pl.pallas_call(
        body, name="ada_w_update", grid=(r // tr, c // tc),
        in_specs=[pl.BlockSpec((tr, nb), lambda i, j: (i, 0)), pl.BlockSpec((nb, tc), lambda i, j: (0, j)),
                  blk, blk, blk],
        out_specs=[blk] * 4, out_shape=[jax.ShapeDtypeStruct((r, c), F32)] * 4,
        compiler_params=_params(("parallel", "parallel")),
    )(act_t, dmod, w, m, v)


WEIGHTS = ['ada_w', 'ada_b', 'norm1_gain', 'norm2_gain', 'final_gain', 'w_in', 'w_out', 's5_lambda_re',
           's5_lambda_im', 's5_log_step', 's5_b_re', 's5_b_im', 's5_c_re', 's5_c_im', 's5_d', 's5_w_glu',
           's5_b_glu', 'rk_shift_prev', 'rk_shift_next', 'rk_w0', 'rk_w_up', 'rk_a0', 'rk_a_up', 'rk_g_up',
           'rk_k_k', 'rk_k_a', 'rk_r_k', 'rk_ln_gain', 'rk_ln_bias', 'ffn_w1', 'ffn_w2']
BIG_SHARDED = ['w_in', 'w_out', 's5_w_glu', 'ffn_w1', 'ffn_w2']
RK_SHARDED = ['rk_w0', 'rk_a0', 'rk_w_up', 'rk_a_up', 'rk_g_up']
REPLICATED = ['ada_b', 'norm1_gain', 'norm2_gain', 'final_gain', 's5_lambda_re', 's5_lambda_im', 's5_log_step',
              's5_b_re', 's5_b_im', 's5_c_re', 's5_c_im', 's5_d', 's5_b_glu', 'rk_shift_prev', 'rk_shift_next',
              'rk_k_k', 'rk_k_a', 'rk_r_k', 'rk_ln_gain', 'rk_ln_bias']
PACK_COLS = 1024
N_CHIPS = 4
RK_ROWS = 420
RK_ROWS_PAD = 432


def _pack_rows(arrays, cols):
    return jnp.concatenate([a.reshape(-1, cols) for a in arrays], axis=0)


def _pack_flat(arrays):
    flat = jnp.concatenate([a.reshape(-1) for a in arrays])
    rows = -(-flat.shape[0] // PACK_COLS)
    return jnp.pad(flat, (0, rows * PACK_COLS - flat.shape[0])).reshape(rows, PACK_COLS)


def _unpack_flat(packed, like):
    flat, out, pos = packed.reshape(-1), [], 0
    for a in like:
        out.append(flat[pos:pos + a.size].reshape(a.shape))
        pos += a.size
    return out


def _cols_to_chips(full, n_rows):
    return jnp.transpose(full.reshape(n_rows, N_CHIPS, -1), (1, 0, 2))


def _chips_to_cols(parts):
    return jnp.transpose(parts, (1, 0, 2)).reshape(parts.shape[1], -1)


def kernel(x, c, ada_w, ada_b, norm1_gain, norm2_gain, final_gain, w_in, w_out, s5_lambda_re, s5_lambda_im, s5_log_step, s5_b_re, s5_b_im, s5_c_re, s5_c_im, s5_d, s5_w_glu, s5_b_glu, rk_shift_prev, rk_shift_next, rk_w0, rk_w_up, rk_a0, rk_a_up, rk_g_up, rk_k_k, rk_k_a, rk_r_k, rk_ln_gain, rk_ln_bias, ffn_w1, ffn_w2, loss_target, m_ada_w, m_ada_b, m_norm1_gain, m_norm2_gain, m_final_gain, m_w_in, m_w_out, m_s5_lambda_re, m_s5_lambda_im, m_s5_log_step, m_s5_b_re, m_s5_b_im, m_s5_c_re, m_s5_c_im, m_s5_d, m_s5_w_glu, m_s5_b_glu, m_rk_shift_prev, m_rk_shift_next, m_rk_w0, m_rk_w_up, m_rk_a0, m_rk_a_up, m_rk_g_up, m_rk_k_k, m_rk_k_a, m_rk_r_k, m_rk_ln_gain, m_rk_ln_bias, m_ffn_w1, m_ffn_w2, v_ada_w, v_ada_b, v_norm1_gain, v_norm2_gain, v_final_gain, v_w_in, v_w_out, v_s5_lambda_re, v_s5_lambda_im, v_s5_log_step, v_s5_b_re, v_s5_b_im, v_s5_c_re, v_s5_c_im, v_s5_d, v_s5_w_glu, v_s5_b_glu, v_rk_shift_prev, v_rk_shift_next, v_rk_w0, v_rk_w_up, v_rk_a0, v_rk_a_up, v_rk_g_up, v_rk_k_k, v_rk_k_a, v_rk_r_k, v_rk_ln_gain, v_rk_ln_bias, v_ffn_w1, v_ffn_w2):
    given = dict(locals())
    w = {n: given[n] for n in WEIGHTS}
    m = {n: given["m_" + n] for n in WEIGHTS}
    v = {n: given["v_" + n] for n in WEIGHTS}
    mx, my, mc = lax.axis_index("x"), lax.axis_index("y"), lax.axis_index("c")
    chip = 2 * mx + my
    dev = 2 * chip + mc
    xt, target = x[0], loss_target[0]

    def rk_rows(d):
        rows = _pack_rows([d[n] for n in RK_SHARDED], 256)
        return jnp.pad(rows, ((0, RK_ROWS_PAD - rows.shape[0]), (0, 0)))

    (c_all,) = _exchange("gather_c", [c], ALL_PEERS, 8)
    shards = [w[n][0].astype(BF16) for n in BIG_SHARDED] + [rk_rows(w)]
    gathered = _gather_halves("gather_w", shards)
    full = dict(zip(BIG_SHARDED, gathered[:5]))
    rk_full = gathered[5]

    (act,) = _rowwise("ada_act", lambda q: (q * _sigmoid(q),), [c_all.reshape(8, D_MODEL)], [], [(D_MODEL, F32)], 8)
    n_mod_cols = N_MOD * D_MODEL // N_CHIPS
    bias = jnp.broadcast_to(lax.dynamic_slice(ada_b, (0, chip * n_mod_cols), (1, n_mod_cols)), (8, n_mod_cols))
    mod_shard = _matmul("ada_fwd", act, ada_w[0], epilogue=_add_epilogue, extras=(bias,))
    (mod_parts,) = _exchange("gather_mod", [mod_shard], CHIP_PEERS, N_CHIPS)
    mod_all = _chips_to_cols(mod_parts)
    mod_mine = lax.dynamic_slice(mod_all, (dev, 0), (1, N_MOD * D_MODEL))
    mod = [mod_mine[:, i * D_MODEL:(i + 1) * D_MODEL] for i in range(N_MOD)]

    def rk_piece(lo, hi, lead):
        return _chips_to_cols(rk_full[:, lo:hi]).reshape(lead + (RK_WIDTH,))

    zeros = jnp.zeros((LORA, RK_WIDTH), F32)
    w_up, a_up = rk_piece(4, 132, (2, LORA)), rk_piece(132, 260, (2, LORA))
    wt = {
        "norm1_gain": norm1_gain, "norm2_gain": norm2_gain, "final_gain": final_gain.reshape(1, D_MODEL),
        "w_in": jnp.pad(_chips_to_cols(full["w_in"]), ((0, 0), (0, PROJ_PAD - PROJ))),
        "w_out": full["w_out"].reshape(D_MODEL, D_MODEL),
        "s5_w_glu": full["s5_w_glu"].reshape(S5_WIDTH, S5_WIDTH),
        "ffn_w1": _chips_to_cols(full["ffn_w1"]), "ffn_w2": full["ffn_w2"].reshape(FFN, D_MODEL),
        "mu_prev": jnp.pad(rk_shift_prev, ((0, 0), (0, RK_PAD - RK_IN))),
        "mu_next": jnp.pad(rk_shift_next, ((0, 0), (0, RK_PAD - RK_IN))),
        "lam_re": [s5_lambda_re[0, d].reshape(S5_CH, 1) for d in range(2)],
        "lam_im": [s5_lambda_im[0, d].reshape(S5_CH, 1) for d in range(2)],
        "log_step": [jnp.repeat(s5_log_step[0, d], S5_STATE).reshape(S5_CH, 1) for d in range(2)],
        "b_re": s5_b_re.reshape(S5_CH, S5_GROUP), "b_im": s5_b_im.reshape(S5_CH, S5_GROUP),
        "c_re": s5_c_re[0], "c_im": s5_c_im[0],
        "s5_d": s5_d, "s5_b_glu": s5_b_glu,
        "w0": list(rk_piece(0, 2, (2,))[:, None, :]), "a0": list(rk_piece(2, 4, (2,))[:, None, :]),
        "w_up": [jnp.concatenate([w_up[0], zeros]), jnp.concatenate([zeros, w_up[1]])],
        "a_up": [jnp.concatenate([a_up[0], zeros]), jnp.concatenate([zeros, a_up[1]])],
        "g_up": jnp.pad(rk_piece(260, 420, (GATE_LORA,)), ((0, GATE_PAD - GATE_LORA), (0, 0))),
        "k_k": rk_k_k, "k_a": rk_k_a, "r_k": rk_r_k.reshape(1, RK_WIDTH),
        "ln_gain": rk_ln_gain, "ln_bias": rk_ln_bias,
    }

    loss_rows, grad_x, g = _local_step(xt, target, mod, wt)
    loss = lax.psum(jnp.sum(loss_rows), ("x", "y", "c"))

    big_grads = {
        "w_in": _cols_to_chips(g["w_in"][:, :PROJ], D_MODEL),
        "w_out": g["w_out"].reshape(N_CHIPS, -1, D_MODEL),
        "s5_w_glu": g["s5_w_glu"].reshape(N_CHIPS, -1, S5_WIDTH),
        "ffn_w1": _cols_to_chips(g["ffn_w1"], D_MODEL),
        "ffn_w2": g["ffn_w2"].reshape(N_CHIPS, -1, D_MODEL),
    }
    rk_grads = jnp.concatenate([
        _cols_to_chips(jnp.concatenate(g["w0"]), 2), _cols_to_chips(jnp.concatenate(g["a0"]), 2),
        _cols_to_chips(jnp.concatenate([g["w_up"][0][:LORA], g["w_up"][1][LORA:]]), 2 * LORA),
        _cols_to_chips(jnp.concatenate([g["a_up"][0][:LORA], g["a_up"][1][LORA:]]), 2 * LORA),
        _cols_to_chips(g["g_up"][:GATE_LORA], GATE_LORA),
        jnp.zeros((N_CHIPS, RK_ROWS_PAD - RK_ROWS, 256), F32)], axis=1)
    names = BIG_SHARDED + ["rk"]
    pieces = [big_grads[n] for n in BIG_SHARDED] + [rk_grads]
    from_sibling = _send_other_half("swap_halves", pieces)
    chip_sums = []
    for n, piece, other in zip(names, pieces, from_sibling):
        half, cols = other.shape[1], other.shape[2]
        own = lax.dynamic_slice_in_dim(piece, mc * half, half, axis=1)
        wire = F32 if n == "rk" else BF16
        (both,) = _rowwise("pair_" + n, lambda a, b: (a + b,), [own.reshape(-1, cols), other.reshape(-1, cols)], [],
                           [(cols, wire)], _row_tile(N_CHIPS * half))
        chip_sums.append(both.reshape(N_CHIPS, half, cols))
    arrived = _exchange("scatter_grads", chip_sums, CHIP_PEERS, N_CHIPS, scatter=True)
    half_sums = [_sum_parts("sum_" + n, a) for n, a in zip(names, arrived)]
    pairs = [p.reshape(1, 2 * p.shape[1], p.shape[2]) for p in _exchange("swap_sums", half_sums, CORE_PEER, 2)]

    out = {}
    for n, pair in zip(BIG_SHARDED, pairs[:5]):
        res = _adamw("adamw_" + n, w[n][0], pair, m[n][0], v[n][0])
        out[n] = [r[None] for r in res]
    rk_res = _adamw("adamw_rk", rk_rows(w), pairs[5], rk_rows(m), rk_rows(v))
    for q in range(4):
        pieces, pos = [], 0
        for n in RK_SHARDED:
            rows = w[n].size // 256
            pieces.append(rk_res[q][pos:pos + rows].reshape(w[n].shape))
            pos += rows
        for n, piece in zip(RK_SHARDED, pieces):
            out.setdefault(n, []).append(piece)

    local_small = {
        "ada_b": jnp.concatenate(g["mod"], axis=1),
        "norm1_gain": g["norm1_gain"], "norm2_gain": g["norm2_gain"], "final_gain": g["final_gain"],
        "s5_lambda_re": jnp.concatenate(g["lam_re"]), "s5_lambda_im": jnp.concatenate(g["lam_im"]),
        "s5_log_step": jnp.concatenate([q.reshape(S5_GROUPS, S5_STATE).sum(axis=1) for q in g["log_step"]]),
        "s5_b_re": g["b_re"], "s5_b_im": g["b_im"], "s5_c_re": g["c_re"], "s5_c_im": g["c_im"],
        "s5_d": g["s5_d"], "s5_b_glu": g["s5_b_glu"],
        "rk_shift_prev": g["mu_prev"][:, :RK_IN], "rk_shift_next": g["mu_next"][:, :RK_IN],
        "rk_k_k": g["k_k"], "rk_k_a": g["k_a"], "rk_r_k": g["r_k"],
        "rk_ln_gain": g["ln_gain"], "rk_ln_bias": g["ln_bias"],
    }
    (small_all,) = _exchange("gather_small", [_pack_flat([local_small[n] for n in REPLICATED])], ALL_PEERS, 8)
    small_res = _adamw("adamw_small", _pack_flat([w[n] for n in REPLICATED]), small_all,
                       _pack_flat([m[n] for n in REPLICATED]), _pack_flat([v[n] for n in REPLICATED]))
    for q in range(4):
        for n, piece in zip(REPLICATED, _unpack_flat(small_res[q], [w[n] for n in REPLICATED])):
            out.setdefault(n, []).append(piece)

    mod_rows = N_MOD * D_MODEL // PACK_COLS
    dmod_all = small_all[:, :mod_rows].reshape(8, N_MOD * D_MODEL)
    dmod = lax.dynamic_slice(dmod_all, (0, chip * n_mod_cols), (8, n_mod_cols))
    res = _ada_w_update(act.T, dmod, ada_w[0], m_ada_w[0], v_ada_w[0])
    out["ada_w"] = [r[None] for r in res]

    return (loss, grad_x[None], *[out[n][0] for n in WEIGHTS], *[out[n][1] for n in WEIGHTS],
            *[out[n][2] for n in WEIGHTS], *[out[n][3] for n in WEIGHTS])
```

```python
import functools
import math

import jax
import jax.numpy as jnp
from jax import lax
from jax.experimental import pallas as pl
from jax.experimental.pallas import tpu as pltpu

F32 = jnp.float32
BF16 = jnp.bfloat16

D_MODEL = 2048
S5_WIDTH = 1024
S5_GROUP = 16
S5_GROUPS = 64
S5_STATE = 64
S5_CH = S5_GROUPS * S5_STATE
S5_BLK = 256
RK_WIDTH = 1024
RK_HEAD = 64
RK_HEADS = 16
LORA = 64
GATE_LORA = 160
GATE_PAD = 256
RK_IN = 3488
RK_PAD = 3584
PROJ = 4512
PROJ_PAD = 4608
FFN = 8192
N_MOD = 6
NORM_EPS = 1e-6
GN_EPS = 64e-5
L2_EPS = 1e-12
RK_CHUNK = 64
RK_PASSES = 3
RK_SOLVE_PASSES = 3
LW_SCALE = math.exp(-0.5)
ADAM_LR, ADAM_B1, ADAM_B2, ADAM_EPS, ADAM_WD, ADAM_STEP = 0.001, 0.9, 0.999, 1e-08, 0.01, 10
VMEM_LIMIT = 56 * 1024 * 1024
HI = lax.Precision.HIGHEST


def _params(sem=None):
    return pltpu.CompilerParams(dimension_semantics=sem, vmem_limit_bytes=VMEM_LIMIT)


def _full(a):
    nd = a.ndim
    return pl.BlockSpec(a.shape, lambda *_: (0,) * nd)


@jax.custom_vjp
def _bdot(a, b):
    return jnp.dot(a.astype(BF16), b.astype(BF16), preferred_element_type=F32)


def _bdot_fwd(a, b):
    return _bdot(a, b), (a, b)


def _bdot_bwd(res, g):
    a, b = res
    gb = g.astype(BF16)
    da = lax.dot_general(gb, b.astype(BF16), (((1,), (1,)), ((), ())), preferred_element_type=F32)
    db = lax.dot_general(a.astype(BF16), gb, (((0,), (0,)), ((), ())), preferred_element_type=F32)
    return da, db


_bdot.defvjp(_bdot_fwd, _bdot_bwd)


def _fdot(a, b):
    return jnp.dot(a, b, precision=HI, preferred_element_type=F32)


def _sigmoid(z):
    return 1.0 / (1.0 + jnp.exp(-z))


def _gelu(y):
    return 0.5 * y * (1.0 + jnp.tanh(0.7978845608028654 * (y + 0.044715 * (y * y * y))))


def _rms(x):
    return x * lax.rsqrt(jnp.mean(x * x, axis=-1, keepdims=True) + NORM_EPS)


def _tile(n, prefs):
    for t in prefs:
        if n % t == 0:
            return t
    return n


def _matmul(name, a, b, ta=False, tb=False, epilogue=None, extras=(), out_dtypes=(F32,)):
    m = a.shape[1] if ta else a.shape[0]
    k = a.shape[0] if ta else a.shape[1]
    n = b.shape[0] if tb else b.shape[1]
    assert k == (b.shape[1] if tb else b.shape[0]), (a.shape, b.shape, ta, tb)
    tm = _tile(m, (1024, 512, 256, 128))
    tn = _tile(n, (1024, 768, 512, 256, 128))
    tk = _tile(k, (512, 256, 128))
    nk = k // tk
    n_ex, n_out = len(extras), len(out_dtypes)
    dims = (((0 if ta else 1,), (1 if tb else 0,)), ((), ()))

    def body(a_ref, b_ref, *rest):
        ex_refs, out_refs, acc = rest[:n_ex], rest[n_ex:n_ex + n_out], rest[-1]
        kk = pl.program_id(2)

        @pl.when(kk == 0)
        def _():
            acc[...] = jnp.zeros_like(acc)

        acc[...] += lax.dot_general(a_ref[...].astype(BF16), b_ref[...].astype(BF16), dims,
                                    preferred_element_type=F32)

        @pl.when(kk == nk - 1)
        def _():
            res = acc[...]
            outs = epilogue(res, *[e[...] for e in ex_refs]) if epilogue is not None else (res,)
            for o_ref, val in zip(out_refs, outs):
                o_ref[...] = val.astype(o_ref.dtype)

    a_spec = pl.BlockSpec((tk, tm), lambda i, j, q: (q, i)) if ta else pl.BlockSpec((tm, tk), lambda i, j, q: (i, q))
    b_spec = pl.BlockSpec((tn, tk), lambda i, j, q: (j, q)) if tb else pl.BlockSpec((tk, tn), lambda i, j, q: (q, j))
    mn_spec = pl.BlockSpec((tm, tn), lambda i, j, q: (i, j))
    outs = pl.pallas_call(
        body, name=name, grid=(m // tm, n // tn, nk),
        in_specs=[a_spec, b_spec] + [mn_spec] * n_ex,
        out_specs=[mn_spec] * n_out,
        out_shape=[jax.ShapeDtypeStruct((m, n), dt) for dt in out_dtypes],
        scratch_shapes=[pltpu.VMEM((tm, tn), F32)],
        compiler_params=_params(("parallel", "parallel", "arbitrary")),
    )(a, b, *extras)
    return outs[0] if n_out == 1 else outs


def _row_spec(a, tm):
    return pl.BlockSpec((tm, a.shape[1]), lambda i: (i, 0))


def _rowwise(name, fn, rows, params, outs, tm):
    t = rows[0].shape[0]
    tm = min(tm, t)
    n_r, n_p = len(rows), len(params)

    def body(*refs):
        vals = [r[...] for r in refs[:n_r + n_p]]
        res = fn(*vals)
        for o_ref, val in zip(refs[n_r + n_p:], res):
            o_ref[...] = val.astype(o_ref.dtype)

    res = pl.pallas_call(
        body, name=name, grid=(t // tm,),
        in_specs=[_row_spec(r, tm) for r in rows] + [_full(p) for p in params],
        out_specs=[pl.BlockSpec((tm, n), lambda i: (i, 0)) for n, _ in outs],
        out_shape=[jax.ShapeDtypeStruct((t, n), dt) for n, dt in outs],
        compiler_params=_params(("parallel",)),
    )(*rows, *params)
    return res


def _rowwise_vjp(name, fn, rows, params, cts, row_grads, param_grads, tm, consts=(), addends=None,
                 emit=(), row_grad_dtypes=None):
    t = rows[0].shape[0]
    tm = min(tm, t)
    addends = addends or {}
    n_r, n_p, n_c = len(rows), len(params), len(consts)
    ct_flat = [c for group in cts for c in group]
    add_list = [addends[q] for q in sorted(addends)]
    n_ct, n_add = len(ct_flat), len(add_list)
    row_grad_dtypes = row_grad_dtypes or [F32] * len(row_grads)

    def body(*refs):
        pos = 0
        row_v = [r[...].astype(F32) for r in refs[pos:pos + n_r]]; pos += n_r
        par_v = [r[...].astype(F32) for r in refs[pos:pos + n_p]]; pos += n_p
        con_v = [r[...] for r in refs[pos:pos + n_c]]; pos += n_c
        ct_v = [r[...].astype(F32) for r in refs[pos:pos + n_ct]]; pos += n_ct
        add_v = [r[...] for r in refs[pos:pos + n_add]]; pos += n_add
        emit_refs = refs[pos:pos + len(emit)]; pos += len(emit)
        rg_refs = refs[pos:pos + len(row_grads)]; pos += len(row_grads)
        pg_refs = refs[pos:pos + len(param_grads)]

        def diff_fn(*dargs):
            rv, pv = list(row_v), list(par_v)
            for q, i in enumerate(row_grads):
                rv[i] = dargs[q]
            for q, j in enumerate(param_grads):
                pv[j] = dargs[len(row_grads) + q]
            return fn(*rv, *pv, *con_v)

        prim = [row_v[i] for i in row_grads] + [par_v[j] for j in param_grads]
        res, vjp = jax.vjp(diff_fn, *prim)
        ct_vals, q = [], 0
        for o, group in zip(res, cts):
            tot = jnp.zeros_like(o)
            for _ in group:
                tot = tot + ct_v[q]
                q += 1
            ct_vals.append(tot)
        grads = vjp(tuple(ct_vals))
        for e_ref, idx in zip(emit_refs, emit):
            e_ref[...] = res[idx].astype(e_ref.dtype)
        add_pos = {p: q for q, p in enumerate(sorted(addends))}
        for q, g_ref in enumerate(rg_refs):
            g = grads[q]
            if q in add_pos:
                g = g + add_v[add_pos[q]]
            g_ref[...] = g.astype(g_ref.dtype)

        @pl.when(pl.program_id(0) == 0)
        def _():
            for g_ref in pg_refs:
                g_ref[...] = jnp.zeros_like(g_ref)

        for q, g_ref in enumerate(pg_refs):
            g_ref[...] += grads[len(row_grads) + q]

    emit_shapes = []
    if emit:
        probe = jax.eval_shape(lambda *a: fn(*a), *[jax.ShapeDtypeStruct((tm, r.shape[1]), F32) for r in rows],
                               *[jax.ShapeDtypeStruct(p.shape, p.dtype) for p in params],
                               *[jax.ShapeDtypeStruct(c.shape, c.dtype) for c in consts])
        emit_shapes = [probe[idx].shape[1] for idx in emit]
    out_specs = ([pl.BlockSpec((tm, n), lambda i: (i, 0)) for n in emit_shapes]
                 + [_row_spec(rows[i], tm) for i in row_grads]
                 + [_full(params[j]) for j in param_grads])
    out_shape = ([jax.ShapeDtypeStruct((t, n), F32) for n in emit_shapes]
                 + [jax.ShapeDtypeStruct(rows[i].shape, dt) for i, dt in zip(row_grads, row_grad_dtypes)]
                 + [jax.ShapeDtypeStruct(params[j].shape, F32) for j in param_grads])
    return pl.pallas_call(
        body, name=name, grid=(t // tm,),
        in_specs=([_row_spec(r, tm) for r in rows] + [_full(p) for p in params] + [_full(c) for c in consts]
                  + [_row_spec(c, tm) for c in ct_flat] + [_row_spec(a, tm) for a in add_list]),
        out_specs=out_specs, out_shape=out_shape,
        compiler_params=_params(("arbitrary",)),
    )(*rows, *params, *consts, *ct_flat, *add_list)


def _norm_mod_fn(x, gain, scale, shift):
    return (_rms(x) * gain * (1.0 + scale) + shift,)


def _resid_norm_mod_fn(x, mixed, gate, gain, scale, shift):
    x1 = x + gate * mixed
    return x1, _rms(x1) * gain * (1.0 + scale) + shift


def _loss_fn(x1, ffn, target, gate, gain):
    y = _rms(x1 + gate * ffn) * gain
    err = y - target
    return (0.5 * jnp.mean(err * err, axis=-1, keepdims=True),)


def _s5_out_fn(ylin, u, d_skip, w_glu, b_glu):
    z = _gelu(ylin + d_skip * u)
    return (z * _sigmoid(_bdot(z, w_glu) + b_glu),)


def _rk_pre_fn(k, wdn, adn, gdn, w0_0, w0_1, wup_0, wup_1, a0_0, a0_1, aup_0, aup_1, g_up, k_k, k_a, seg, seg_t):
    kkr = k * k_k
    inv = 1.0 / jnp.sqrt(jnp.maximum(_fdot(kkr * kkr, seg), L2_EPS * L2_EPS))
    kk = kkr * _fdot(inv, seg_t)
    tw = jnp.tanh(wdn)
    lws, kds, acts = [], [], []
    for w0, wup, a0, aup in ((w0_0, wup_0, a0_0, aup_0), (w0_1, wup_1, a0_1, aup_1)):
        lws.append(-LW_SCALE * _sigmoid(w0 + _bdot(tw, wup)))
        act = _sigmoid(a0 + _bdot(adn, aup))
        acts.append(act)
        kds.append(k * (1.0 + (act - 1.0) * k_a))
    gate = _bdot(_sigmoid(gdn), g_up)
    return (kk, lws[0], lws[1], kds[0], kds[1], acts[0], acts[1], gate)


def _rk_post_fn(y0, y1, r, v, kd0, kd1, gate, ln_gain, ln_bias, r_k, seg, seg_t):
    y = y0 + y1
    mu = _fdot(_fdot(y, seg) * (1.0 / RK_HEAD), seg_t)
    yc = y - mu
    var = _fdot(yc * yc, seg) * (1.0 / RK_HEAD)
    yn = yc * _fdot(lax.rsqrt(var + GN_EPS), seg_t) * ln_gain + ln_bias
    bonus = _fdot(_fdot(r * (kd0 + kd1) * r_k, seg), seg_t)
    return ((yn + bonus * v) * gate,)


def _s5_prep_fn(lr0, li0, ls0, lr1, li1, ls1, b_re, b_im):
    outs = []
    for lam_re, lam_im, ls in ((lr0, li0, ls0), (lr1, li1, ls1)):
        step = jnp.exp(ls)
        mag = jnp.exp(lam_re * step)
        lbar_re = mag * jnp.cos(lam_im * step)
        lbar_im = mag * jnp.sin(lam_im * step)
        den = lam_re * lam_re + lam_im * lam_im
        nr = lbar_re - 1.0
        coef_re = (nr * lam_re + lbar_im * lam_im) / den
        coef_im = (lbar_im * lam_re - nr * lam_im) / den
        outs += [lbar_re, lbar_im, coef_re * b_re - coef_im * b_im, coef_re * b_im + coef_im * b_re]
    return tuple(outs)


def _shift_rows(x, down):
    t = x.shape[0]
    rows = lax.broadcasted_iota(jnp.int32, x.shape, 0)
    if down:
        return jnp.where(rows >= 1, pltpu.roll(x, 1, 0), 0.0)
    return jnp.where(rows < t - 1, pltpu.roll(x, t - 1, 0), 0.0)


def _token_shift(p, mu_prev, mu_next):
    t, n = p.shape

    def body(p_ref, mp_ref, mn_ref, o_ref):
        x = p_ref[...]
        o_ref[...] = x + mp_ref[...] * (_shift_rows(x, True) - x) + mn_ref[...] * (_shift_rows(x, False) - x)

    col = pl.BlockSpec((t, 128), lambda j: (0, j))
    par = pl.BlockSpec((1, 128), lambda j: (0, j))
    return pl.pallas_call(
        body, name="token_shift", grid=(n // 128,), in_specs=[col, par, par], out_specs=col,
        out_shape=jax.ShapeDtypeStruct((t, n), F32), compiler_params=_params(("parallel",)),
    )(p, mu_prev, mu_next)


def _token_shift_bwd(p, mu_prev, mu_next, dps):
    t, n = p.shape

    def body(p_ref, mp_ref, mn_ref, d_ref, dp_ref, dmp_ref, dmn_ref):
        x, d, mp, mn = p_ref[...], d_ref[...], mp_ref[...], mn_ref[...]
        dp_ref[...] = d * (1.0 - mp - mn) + _shift_rows(d * mp, False) + _shift_rows(d * mn, True)
        dmp_ref[...] = jnp.sum(d * (_shift_rows(x, True) - x), axis=0, keepdims=True)
        dmn_ref[...] = jnp.sum(d * (_shift_rows(x, False) - x), axis=0, keepdims=True)

    col = pl.BlockSpec((t, 128), lambda j: (0, j))
    par = pl.BlockSpec((1, 128), lambda j: (0, j))
    return pl.pallas_call(
        body, name="token_shift_bwd", grid=(n // 128,), in_specs=[col, par, par, col],
        out_specs=[col, par, par],
        out_shape=[jax.ShapeDtypeStruct((t, n), F32), jax.ShapeDtypeStruct((1, n), F32),
                   jax.ShapeDtypeStruct((1, n), F32)],
        compiler_params=_params(("parallel",)),
    )(p, mu_prev, mu_next, dps)


N_SEG = 32
S5_BLOCKS = 32
S5_PER_IN = 4


def _scan_in_place(sr_ref, si_ref, ar, ai, carry_ref, reverse):
    seg_len = sr_ref.shape[0] // N_SEG
    ng = N_SEG // 8

    def rows(i, grp):
        first = (seg_len - 1 - i if reverse else i) * N_SEG + 8 * grp
        return pl.ds(pl.multiple_of(first, 8), 8)

    zero = jnp.zeros((8, 128), F32)
    one = jnp.ones((8, 128), F32)

    def local(i, c):
        pr, pi = c[-2:]
        out = []
        for grp in range(ng):
            sr, si = c[2 * grp], c[2 * grp + 1]
            nr = ar * sr - ai * si + sr_ref[rows(i, grp), :]
            ni = ar * si + ai * sr + si_ref[rows(i, grp), :]
            sr_ref[rows(i, grp), :] = nr
            si_ref[rows(i, grp), :] = ni
            out += [nr, ni]
        return tuple(out) + (ar * pr - ai * pi, ar * pi + ai * pr)

    ends = lax.fori_loop(0, seg_len, local, (zero,) * (2 * ng) + (one, zero))
    qr, qi = ends[-2][0:1], ends[-1][0:1]
    order = list(range(N_SEG - 1, -1, -1)) if reverse else list(range(N_SEG))
    cr = jnp.zeros((1, 128), F32)
    ci = jnp.zeros((1, 128), F32)
    for j in order:
        carry_ref[j:j + 1, :] = cr
        carry_ref[N_SEG + j:N_SEG + j + 1, :] = ci
        grp, sub = divmod(j, 8)
        lr, li = ends[2 * grp][sub:sub + 1], ends[2 * grp + 1][sub:sub + 1]
        cr, ci = lr + qr * cr - qi * ci, li + qr * ci + qi * cr
    carries = [(carry_ref[8 * grp:8 * grp + 8, :], carry_ref[N_SEG + 8 * grp:N_SEG + 8 * grp + 8, :])
               for grp in range(ng)]

    def fix(i, c):
        pr, pi = c
        npr, npi = ar * pr - ai * pi, ar * pi + ai * pr
        for grp in range(ng):
            cr8, ci8 = carries[grp]
            sr_ref[rows(i, grp), :] = sr_ref[rows(i, grp), :] + npr * cr8 - npi * ci8
            si_ref[rows(i, grp), :] = si_ref[rows(i, grp), :] + npr * ci8 + npi * cr8
        return npr, npi

    lax.fori_loop(0, seg_len, fix, (one, zero))


def _interleave(x):
    t, c = x.shape
    return jnp.transpose(x.reshape(N_SEG, t // N_SEG, c), (1, 0, 2)).reshape(t, c)


def _deinterleave(x):
    t, c = x.shape
    return jnp.transpose(x.reshape(t // N_SEG, N_SEG, c), (1, 0, 2)).reshape(t, c)


def _step_neighbour(s, earlier):
    t = s.shape[0]
    rows = lax.broadcasted_iota(jnp.int32, s.shape, 0)
    if earlier:
        return jnp.where(rows >= N_SEG, pltpu.roll(s, N_SEG, 0),
                         jnp.where(rows >= 1, pltpu.roll(s, N_SEG + 1, 0), 0.0))
    return jnp.where(rows < t - N_SEG, pltpu.roll(s, t - N_SEG, 0),
                     jnp.where(rows < t - 1, pltpu.roll(s, t - N_SEG - 1, 0), 0.0))


def _dot_bf16(a, b, dims=(((1,), (0,)), ((), ()))):
    return lax.dot_general(a.astype(BF16), b.astype(BF16), dims, preferred_element_type=F32)


NT_DIMS = (((1,), (1,)), ((), ()))
TN_DIMS = (((0,), (0,)), ((), ()))


def _s5_specs(t):
    blk = pl.BlockSpec((None, t, 128), lambda i, q: (S5_PER_IN * i + q, 0, 0))
    mat = pl.BlockSpec((None, 128, 128), lambda i, q: (S5_PER_IN * i + q, 0, 0))
    vec = pl.BlockSpec((None, 1, 128), lambda i, q: (S5_PER_IN * i + q, 0, 0))
    chan = pl.BlockSpec((t, 128), lambda i, q: (0, i))
    return blk, mat, vec, chan


S5_GRID = (S5_BLOCKS // S5_PER_IN, S5_PER_IN)


def _s5_forward(name, u, b_re, b_im, l_re, l_im, reverse, other=None, c_re=None, c_im_neg=None):
    t = u.shape[0]
    project = other is not None
    blk, mat, vec, chan = _s5_specs(t)

    def body(*refs):
        u_ref, br_ref, bi_ref, lr_ref, li_ref = refs[:5]
        if project:
            or_ref, oi_ref, cr_ref, ci_ref, sr_ref, si_ref, y_ref, carry_ref = refs[5:]
        else:
            sr_ref, si_ref, carry_ref = refs[5:]
        uv = u_ref[...]
        sr_ref[...] = _dot_bf16(uv, br_ref[...])
        si_ref[...] = _dot_bf16(uv, bi_ref[...])
        ar = jnp.broadcast_to(lr_ref[...], (8, 128))
        ai = jnp.broadcast_to(li_ref[...], (8, 128))
        _scan_in_place(sr_ref, si_ref, ar, ai, carry_ref, reverse)
        if project:
            y = (_dot_bf16(sr_ref[...] + or_ref[...], cr_ref[...])
                 + _dot_bf16(si_ref[...] + oi_ref[...], ci_ref[...]))

            @pl.when(pl.program_id(1) == 0)
            def _():
                y_ref[...] = y

            @pl.when(pl.program_id(1) != 0)
            def _():
                y_ref[...] += y

    state = jax.ShapeDtypeStruct((S5_BLOCKS, t, 128), F32)
    ins = [u, b_re, b_im, l_re, l_im] + ([other[0], other[1], c_re, c_im_neg] if project else [])
    in_specs = [chan, mat, mat, vec, vec] + ([blk, blk, mat, mat] if project else [])
    return pl.pallas_call(
        body, name=name, grid=S5_GRID, in_specs=in_specs,
        out_specs=[blk, blk] + ([chan] if project else []),
        out_shape=[state, state] + ([jax.ShapeDtypeStruct((t, S5_WIDTH), F32)] if project else []),
        scratch_shapes=[pltpu.VMEM((2 * N_SEG, 128), F32)],
        compiler_params=_params(("arbitrary", "arbitrary")),
    )(*ins)


def _s5_backward(name, dy, u, du_in, states, other, b_re, b_im, c_re, c_im_neg, l_re, l_im, reverse):
    t = u.shape[0]
    with_c = other is not None
    blk, mat, vec, chan = _s5_specs(t)

    def body(*refs):
        dy_ref, u_ref, du_in_ref, sr_ref, si_ref = refs[:5]
        pos = 5
        if with_c:
            or_ref, oi_ref = refs[5:7]
            pos = 7
        br_ref, bi_ref, cr_ref, ci_ref, lr_ref, li_ref = refs[pos:pos + 6]
        outs = refs[pos + 6:]
        du_ref, dbr_ref, dbi_ref, dlr_ref, dli_ref = outs[:5]
        lam_r, lam_i, carry_ref = outs[-3:]
        dyv, uv = dy_ref[...], u_ref[...]
        lam_r[...] = _dot_bf16(dyv, cr_ref[...], NT_DIMS)
        lam_i[...] = _dot_bf16(dyv, ci_ref[...], NT_DIMS)
        ar = jnp.broadcast_to(lr_ref[...], (8, 128))
        ai = -jnp.broadcast_to(li_ref[...], (8, 128))
        _scan_in_place(lam_r, lam_i, ar, ai, carry_ref, not reverse)
        lr, li = lam_r[...], lam_i[...]
        pr, pi = _step_neighbour(sr_ref[...], not reverse), _step_neighbour(si_ref[...], not reverse)
        dlr_ref[...] = jnp.sum(lr * pr + li * pi, axis=0, keepdims=True)
        dli_ref[...] = jnp.sum(li * pr - lr * pi, axis=0, keepdims=True)
        dbr_ref[...] = _dot_bf16(uv, lr, TN_DIMS)
        dbi_ref[...] = _dot_bf16(uv, li, TN_DIMS)
        du = _dot_bf16(lr, br_ref[...], NT_DIMS) + _dot_bf16(li, bi_ref[...], NT_DIMS)

        @pl.when(pl.program_id(1) == 0)
        def _():
            du_ref[...] = du_in_ref[...] + du

        @pl.when(pl.program_id(1) != 0)
        def _():
            du_ref[...] += du

        if with_c:
            dcr_ref, dci_ref = outs[5:7]
            dcr_ref[...] = _dot_bf16(sr_ref[...] + or_ref[...], dyv, TN_DIMS)
            dci_ref[...] = _dot_bf16(si_ref[...] + oi_ref[...], dyv, TN_DIMS)

    mats = jax.ShapeDtypeStruct((S5_BLOCKS, 128, 128), F32)
    vecs = jax.ShapeDtypeStruct((S5_BLOCKS, 1, 128), F32)
    ins = [dy, u, du_in, states[0], states[1]] + ([other[0], other[1]] if with_c else [])
    ins += [b_re, b_im, c_re, c_im_neg, l_re, l_im]
    in_specs = [chan, chan, chan, blk, blk] + ([blk, blk] if with_c else []) + [mat] * 4 + [vec] * 2
    return pl.pallas_call(
        body, name=name, grid=S5_GRID, in_specs=in_specs,
        out_specs=[chan, mat, mat, vec, vec] + ([mat, mat] if with_c else []),
        out_shape=[jax.ShapeDtypeStruct((t, S5_WIDTH), F32), mats, mats, vecs, vecs] + ([mats, mats] if with_c else []),
        scratch_shapes=[pltpu.VMEM((t, 128), F32), pltpu.VMEM((t, 128), F32), pltpu.VMEM((2 * N_SEG, 128), F32)],
        compiler_params=_params(("arbitrary", "arbitrary")),
    )(*ins)


def _ein(passes, spec, a, b):
    if passes == 6:
        return jnp.einsum(spec, a, b, precision=HI, preferred_element_type=F32)
    a_hi, b_hi = a.astype(BF16), b.astype(BF16)
    if passes == 1:
        return jnp.einsum(spec, a_hi, b_hi, preferred_element_type=F32)
    a_lo = (a - a_hi.astype(F32)).astype(BF16)
    b_lo = (b - b_hi.astype(F32)).astype(BF16)
    cross = jnp.einsum(spec, a_hi, b_lo, preferred_element_type=F32)
    if spec.startswith('hik'):
        m = a.shape[1]
        stacked = jnp.einsum(spec, jnp.concatenate([a_hi, a_lo], axis=1), b_hi, preferred_element_type=F32)
        return stacked[:, :m] + stacked[:, m:] + cross
    return (jnp.einsum(spec, a_hi, b_hi, preferred_element_type=F32) + cross
            + jnp.einsum(spec, a_lo, b_hi, preferred_element_type=F32))


@jax.custom_vjp
def _tri_mm(tri, tri_t, z):
    return jnp.einsum('hik,hkj->hij', tri, z, precision=HI, preferred_element_type=F32)


def _tri_mm_bwd(res, g):
    tri, tri_t = res
    return jnp.zeros_like(tri), jnp.zeros_like(tri_t), _tri_mm(tri_t, tri, g)


_tri_mm.defvjp(lambda tri, tri_t, z: (_tri_mm(tri, tri_t, z), (tri, tri_t)), _tri_mm_bwd)


def _chunk_cumsum(lw, incl, incl_t):
    shape = (lw.shape[0],) + incl.shape
    return _tri_mm(jnp.broadcast_to(incl.astype(F32), shape), jnp.broadcast_to(incl_t.astype(F32), shape), lw)


@functools.partial(jax.custom_vjp, nondiff_argnums=(0,))
def _bmm(p, a, b):
    return _ein(p, 'hik,hkj->hij', a, b)


@functools.partial(jax.custom_vjp, nondiff_argnums=(0,))
def _bmm_nt(p, a, b):
    return _ein(p, 'hik,hjk->hij', a, b)


@functools.partial(jax.custom_vjp, nondiff_argnums=(0,))
def _bmm_tn(p, a, b):
    return _ein(p, 'hki,hkj->hij', a, b)


_bmm.defvjp(lambda p, a, b: (_bmm(p, a, b), (a, b)),
            lambda p, res, g: (_bmm_nt(p, g, res[1]), _bmm_tn(p, res[0], g)))
_bmm_nt.defvjp(lambda p, a, b: (_bmm_nt(p, a, b), (a, b)),
               lambda p, res, g: (_bmm(p, g, res[1]), _bmm_tn(p, g, res[0])))
_bmm_tn.defvjp(lambda p, a, b: (_bmm_tn(p, a, b), (a, b)),
               lambda p, res, g: (_bmm_nt(p, res[1], g), _bmm(p, res[0], g)))


@jax.custom_vjp
def _split_rows(x):
    c = x.shape[1] // 2
    return x[:, :c], x[:, c:]


_split_rows.defvjp(lambda x: (_split_rows(x), None), lambda _, g: (jnp.concatenate(g, axis=1),))


def _stack_rows(a, b):
    return jnp.concatenate([a, b], axis=1)


def _rk_chunk(s0, r, lw, k, v, kk, a, reverse):
    h, c, n = r.shape
    row = lax.broadcasted_iota(jnp.int32, (c, c), 0)
    col = lax.broadcasted_iota(jnp.int32, (c, c), 1)
    incl = (row <= col) if reverse else (row >= col)
    strict = (row < col) if reverse else (row > col)
    cum = _chunk_cumsum(lw, incl, (row >= col) if reverse else (row <= col))
    g_in = jnp.exp(cum)
    g_inv = jnp.exp(-cum)
    kap = kk * jnp.exp(cum - lw)
    beta = kk * a * g_inv
    kt = k * g_inv
    rt = r * g_in
    pm, ps = RK_PASSES, RK_SOLVE_PASSES
    both = _stack_rows(kap, rt)
    kap_beta, rt_beta = _split_rows(_bmm_nt(ps, both, beta))
    kap_kt, rt_kt = _split_rows(_bmm_nt(pm, both, kt))
    kap_s0, rt_s0 = _split_rows(_bmm_nt(pm, both, s0))
    l_mat = jnp.where(strict, kap_beta, 0.0)
    rhs = kap_s0 + _bmm(pm, jnp.where(strict, kap_kt, 0.0), v)
    x = -l_mat
    inv = jnp.where(row == col, 1.0, 0.0) + x
    power = _bmm(ps, x, x)
    span = 2
    while 2 * span < c:
        step, power = _split_rows(_bmm(ps, _stack_rows(inv, power), power))
        inv = inv + step
        span *= 2
    inv = inv + _bmm(ps, inv, power)
    u = _bmm(ps, inv, rhs)
    y = rt_s0 + _bmm(pm, jnp.where(incl, rt_kt, 0.0), v) - _bmm(pm, jnp.where(incl, rt_beta, 0.0), u)
    s1 = ((s0 + _bmm_tn(pm, _stack_rows(v, -u), _stack_rows(kt, beta)))
          * jnp.exp(jnp.sum(lw, axis=1, keepdims=True)))
    return y, s1


def _rk_core_fwd(name, r, lw, k, v, kk, a, reverse, chunk):
    h, t, n = r.shape
    nc = t // chunk

    def idx(i):
        return nc - 1 - i if reverse else i

    def body(r_ref, lw_ref, k_ref, v_ref, kk_ref, a_ref, y_ref, ck_ref, s_ref):
        @pl.when(pl.program_id(0) == 0)
        def _():
            s_ref[...] = jnp.zeros_like(s_ref)

        s0 = s_ref[...]
        ck_ref[0] = s0
        y, s1 = _rk_chunk(s0, r_ref[...], lw_ref[...], k_ref[...], v_ref[...], kk_ref[...], a_ref[...], reverse)
        y_ref[...] = y
        s_ref[...] = s1

    blk = pl.BlockSpec((h, chunk, n), lambda i: (0, idx(i), 0))
    return pl.pallas_call(
        body, name=name, grid=(nc,), in_specs=[blk] * 6,
        out_specs=[blk, pl.BlockSpec((1, h, n, n), lambda i: (idx(i), 0, 0, 0))],
        out_shape=[jax.ShapeDtypeStruct((h, t, n), F32), jax.ShapeDtypeStruct((nc, h, n, n), F32)],
        scratch_shapes=[pltpu.VMEM((h, n, n), F32)],
        compiler_params=_params(("arbitrary",)),
    )(r, lw, k, v, kk, a)


def _rk_core_bwd(name, r, lw, k, v, kk, a, ck, dy, reverse, chunk):
    h, t, n = r.shape
    nc = t // chunk

    def idx(i):
        return i if reverse else nc - 1 - i

    def body(r_ref, lw_ref, k_ref, v_ref, kk_ref, a_ref, ck_ref, dy_ref, *rest):
        out_refs, ds_ref = rest[:6], rest[6]

        @pl.when(pl.program_id(0) == 0)
        def _():
            ds_ref[...] = jnp.zeros_like(ds_ref)

        fn = functools.partial(_rk_chunk, reverse=reverse)
        _, vjp = jax.vjp(fn, ck_ref[0], r_ref[...], lw_ref[...], k_ref[...], v_ref[...], kk_ref[...], a_ref[...])
        grads = vjp((dy_ref[...], ds_ref[...]))
        ds_ref[...] = grads[0]
        for o_ref, g in zip(out_refs, grads[1:]):
            o_ref[...] = g

    blk = pl.BlockSpec((h, chunk, n), lambda i: (0, idx(i), 0))
    return pl.pallas_call(
        body, name=name, grid=(nc,),
        in_specs=[blk] * 6 + [pl.BlockSpec((1, h, n, n), lambda i: (idx(i), 0, 0, 0)), blk],
        out_specs=[blk] * 6,
        out_shape=[jax.ShapeDtypeStruct((h, t, n), F32)] * 6,
        scratch_shapes=[pltpu.VMEM((h, n, n), F32)],
        compiler_params=_params(("arbitrary",)),
    )(r, lw, k, v, kk, a, ck, dy)


def _to_heads(x):
    return jnp.transpose(x.reshape(x.shape[0], RK_HEADS, RK_HEAD), (1, 0, 2))


def _from_heads(x):
    return jnp.transpose(x, (1, 0, 2)).reshape(x.shape[1], RK_WIDTH)


def _s5_band_place():
    return jax.nn.one_hot(jnp.arange(S5_BLOCKS) % S5_PER_IN, S5_PER_IN, dtype=F32)


def _s5_in_blocks(bbar):
    b = jnp.transpose(bbar.reshape(S5_BLOCKS, 2, S5_STATE, S5_GROUP), (0, 1, 3, 2))
    band = jnp.einsum('jghp,gk->jghkp', b, jnp.eye(2, dtype=F32)).reshape(S5_BLOCKS, 32, 128)
    return jnp.einsum('jrc,jq->jqrc', band, _s5_band_place()).reshape(S5_BLOCKS, 128, 128)


def _s5_in_unblock(mats):
    band = jnp.einsum('jqrc,jq->jrc', mats.reshape(S5_BLOCKS, S5_PER_IN, 32, 128), _s5_band_place())
    diag = jnp.einsum('jghgp->jghp', band.reshape(S5_BLOCKS, 2, S5_GROUP, 2, S5_STATE))
    return jnp.transpose(diag, (0, 1, 3, 2)).reshape(S5_CH, S5_GROUP)


def _s5_out_blocks(c):
    ct = jnp.transpose(c.reshape(S5_BLOCKS, 2, S5_GROUP, S5_STATE), (0, 1, 3, 2))
    band = jnp.einsum('jgph,gk->jgpkh', ct, jnp.eye(2, dtype=F32)).reshape(S5_BLOCKS, 128, 32)
    return jnp.einsum('jrc,jq->jrqc', band, _s5_band_place()).reshape(S5_BLOCKS, 128, 128)


def _s5_out_unblock(mats):
    band = jnp.einsum('jrqc,jq->jrc', mats.reshape(S5_BLOCKS, 128, S5_PER_IN, 32), _s5_band_place())
    diag = jnp.einsum('jgpgh->jgph', band.reshape(S5_BLOCKS, 2, S5_STATE, 2, S5_GROUP))
    return jnp.transpose(diag, (0, 1, 3, 2)).reshape(S5_GROUPS, S5_GROUP, S5_STATE)


def _head_indicator():
    ch = lax.broadcasted_iota(jnp.int32, (RK_WIDTH, 128), 0) // RK_HEAD
    hd = lax.broadcasted_iota(jnp.int32, (RK_WIDTH, 128), 1)
    seg = (ch == hd).astype(F32)
    return seg, seg.T


def _add_epilogue(acc, e):
    return (acc + e,)


def _local_step(x, target, mod, wt, chunk=RK_CHUNK):
    t = x.shape[0]
    sh1, sc1, gt1, sh2, sc2, gt2 = mod
    seg, seg_t = _head_indicator()
    g = {}

    (h1,) = _rowwise("norm1", _norm_mod_fn, [x], [wt["norm1_gain"], sc1, sh1], [(D_MODEL, BF16)], 256)
    proj = _matmul("proj", h1, wt["w_in"])
    u, p = proj[:, :S5_WIDTH], proj[:, S5_WIDTH:]
    ps = _token_shift(p, wt["mu_prev"], wt["mu_next"])
    r, k, v = ps[:, :1024], ps[:, 1024:2048], ps[:, 2048:3072]
    wdn, adn, gdn = ps[:, 3072:3200], ps[:, 3200:3328], ps[:, 3328:RK_PAD]

    prep_rows = [wt["lam_re"][0], wt["lam_im"][0], wt["log_step"][0], wt["lam_re"][1], wt["lam_im"][1],
                 wt["log_step"][1], wt["b_re"], wt["b_im"]]
    col1, col16 = (1, F32), (S5_GROUP, F32)
    prep = _rowwise("s5_prep", _s5_prep_fn, prep_rows, [], [col1, col1, col16, col16] * 2, 512)
    lbar = [tuple(prep[4 * d + q].reshape(S5_BLOCKS, 1, 128) for q in range(2)) for d in range(2)]
    b_blk = [tuple(_s5_in_blocks(prep[4 * d + 2 + q]) for q in range(2)) for d in range(2)]
    c_blk = (_s5_out_blocks(wt["c_re"]), -_s5_out_blocks(wt["c_im"]))
    u_il = _interleave(u)
    state0 = _s5_forward("s5_fwd0", u_il, *b_blk[0], *lbar[0], reverse=False)
    s1_re, s1_im, ylin_il = _s5_forward("s5_fwd1", u_il, *b_blk[1], *lbar[1], reverse=True, other=state0,
                                        c_re=c_blk[0], c_im_neg=c_blk[1])
    ylin = _deinterleave(ylin_il)
    states = [tuple(state0), (s1_re, s1_im)]
    s5_par = [wt["s5_d"], wt["s5_w_glu"], wt["s5_b_glu"]]
    (o_s5,) = _rowwise("s5_out", _s5_out_fn, [ylin, u], s5_par, [(S5_WIDTH, BF16)], 256)

    pre_par = [wt["w0"][0], wt["w0"][1], wt["w_up"][0], wt["w_up"][1], wt["a0"][0], wt["a0"][1],
               wt["a_up"][0], wt["a_up"][1], wt["g_up"], wt["k_k"], wt["k_a"]]
    pre = _rowwise("rk_pre", _rk_pre_fn, [k, wdn, adn, gdn], pre_par + [seg, seg_t], [(RK_WIDTH, F32)] * 8, 256)
    kk, lw, kd, act, gate = pre[0], pre[1:3], pre[3:5], pre[5:7], pre[7]
    rh, vh, kkh = _to_heads(r), _to_heads(v), _to_heads(kk)
    core_in, ys, cks = [], [], []
    for d in range(2):
        ops = (rh, _to_heads(lw[d]), _to_heads(kd[d]), vh, kkh, _to_heads(act[d]))
        y_h, ck = _rk_core_fwd(f"rk_core{d}", *ops, reverse=(d == 1), chunk=min(chunk, t))
        core_in.append(ops)
        ys.append(_from_heads(y_h))
        cks.append(ck)
    post_rows = [ys[0], ys[1], r, v, kd[0], kd[1], gate]
    post_par = [wt["ln_gain"], wt["ln_bias"], wt["r_k"]]
    (o_rk,) = _rowwise("rk_post", _rk_post_fn, post_rows, post_par + [seg, seg_t], [(RK_WIDTH, BF16)], 256)

    o = jnp.concatenate([o_s5, o_rk], axis=1)
    mixed = _matmul("mix_out", o, wt["w_out"])
    n2_par = [gt1, wt["norm2_gain"], sc2, sh2]
    x1, h2 = _rowwise("norm2", _resid_norm_mod_fn, [x, mixed], n2_par, [(D_MODEL, F32), (D_MODEL, BF16)], 256)
    f1, hid = _matmul("ffn1", h2, wt["ffn_w1"], out_dtypes=(F32, BF16),
                      epilogue=lambda acc: (acc, jnp.square(jnp.maximum(acc, 0.0))))
    ffn = _matmul("ffn2", hid, wt["ffn_w2"])

    ones = jnp.ones((t, 1), F32)
    loss_rows, dx1, dffn, g_gt2, g["final_gain"] = _rowwise_vjp(
        "loss", _loss_fn, [x1, ffn, target], [gt2, wt["final_gain"]], [[ones]], [0, 1], [0, 1], 256, emit=(0,))
    df1 = _matmul("ffn2_dx", dffn, wt["ffn_w2"], tb=True, extras=(f1,), out_dtypes=(BF16,),
                  epilogue=lambda acc, f: (acc * (2.0 * jnp.maximum(f, 0.0)),))
    g["ffn_w2"] = _matmul("ffn2_dw", hid, dffn, ta=True)
    dh2 = _matmul("ffn1_dx", df1, wt["ffn_w1"], tb=True)
    g["ffn_w1"] = _matmul("ffn1_dw", h2, df1, ta=True)
    dx_a, dmixed, g_gt1, g["norm2_gain"], g_sc2, g_sh2 = _rowwise_vjp(
        "norm2_bwd", _resid_norm_mod_fn, [x, mixed], n2_par, [[dx1], [dh2]], [0, 1], [0, 1, 2, 3], 256)
    do = _matmul("mix_out_dx", dmixed, wt["w_out"], tb=True)
    g["w_out"] = _matmul("mix_out_dw", o, dmixed, ta=True)
    do_s5, do_rk = do[:, :S5_WIDTH], do[:, S5_WIDTH:]

    dylin, du, g["s5_d"], g["s5_w_glu"], g["s5_b_glu"] = _rowwise_vjp(
        "s5_out_bwd", _s5_out_fn, [ylin, u], s5_par, [[do_s5]], [0, 1], [0, 1, 2], 256)
    prep_cts = []
    dylin_il, du_il = _interleave(dylin), _interleave(du)
    for d in range(2):
        res = _s5_backward(f"s5_bwd{d}", dylin_il, u_il, du_il, states[d], states[1] if d == 0 else None,
                           *b_blk[d], *c_blk, *lbar[d], reverse=(d == 1))
        du_il, db_re, db_im, dl_re, dl_im = res[:5]
        if d == 0:
            g["c_re"], g["c_im"] = _s5_out_unblock(res[5]), -_s5_out_unblock(res[6])
        prep_cts += [[dl_re.reshape(S5_CH, 1)], [dl_im.reshape(S5_CH, 1)], [_s5_in_unblock(db_re)],
                     [_s5_in_unblock(db_im)]]
    du = _deinterleave(du_il)
    pg = _rowwise_vjp("s5_prep_bwd", _s5_prep_fn, prep_rows, [], prep_cts, list(range(8)), [], 512)
    g["lam_re"], g["lam_im"], g["log_step"] = (pg[0], pg[3]), (pg[1], pg[4]), (pg[2], pg[5])
    g["b_re"], g["b_im"] = pg[6], pg[7]

    pb = _rowwise_vjp("rk_post_bwd", _rk_post_fn, post_rows, post_par, [[do_rk]], [0, 2, 3, 4, 5, 6], [0, 1, 2],
                      128, consts=[seg, seg_t])
    dy, dr_b, dv_b, dkd_b, dgate = pb[0], pb[1], pb[2], pb[3:5], pb[5]
    g["ln_gain"], g["ln_bias"], g["r_k"] = pb[6], pb[7], pb[8]
    dyh = _to_heads(dy)
    cg = []
    for d in range(2):
        grads = _rk_core_bwd(f"rk_core{d}_bwd", *core_in[d], cks[d], dyh, reverse=(d == 1), chunk=min(chunk, t))
        cg.append([_from_heads(q) for q in grads])
    pre_cts = [[cg[0][4], cg[1][4]], [cg[0][1]], [cg[1][1]], [cg[0][2], dkd_b[0]], [cg[1][2], dkd_b[1]],
               [cg[0][5]], [cg[1][5]], [dgate]]
    qb = _rowwise_vjp("rk_pre_bwd", _rk_pre_fn, [k, wdn, adn, gdn], pre_par, pre_cts, [0, 1, 2, 3],
                      list(range(11)), 128, consts=[seg, seg_t])
    dk, dwdn, dadn, dgdn = qb[:4]
    g["w0"], g["w_up"], g["a0"], g["a_up"] = (qb[4], qb[5]), (qb[6], qb[7]), (qb[8], qb[9]), (qb[10], qb[11])
    g["g_up"], g["k_k"], g["k_a"] = qb[12], qb[13], qb[14]
    dr, dv = _rowwise("rk_sum", lambda a, b, c, e, f, h: (a + b + c, e + f + h),
                      [cg[0][0], cg[1][0], dr_b, cg[0][3], cg[1][3], dv_b], [], [(RK_WIDTH, F32)] * 2, 256)
    dps = jnp.concatenate([dr, dk, dv, dwdn, dadn, dgdn], axis=1)
    dp, g["mu_prev"], g["mu_next"] = _token_shift_bwd(p, wt["mu_prev"], wt["mu_next"], dps)

    dproj = jnp.concatenate([du, dp], axis=1)
    dh1 = _matmul("proj_dx", dproj, wt["w_in"], tb=True)
    g["w_in"] = _matmul("proj_dw", h1, dproj, ta=True)
    grad_x, g["norm1_gain"], g_sc1, g_sh1 = _rowwise_vjp(
        "norm1_bwd", _norm_mod_fn, [x], [wt["norm1_gain"], sc1, sh1], [[dh1]], [0], [0, 1, 2], 256,
        addends={0: dx_a})
    g["mod"] = [g_sh1, g_sc1, g_gt1, g_sh2, g_sc2, g_gt2]
    return loss_rows, grad_x, g


CHIP_PEERS = ((1, 0, 0), (0, 1, 0), (1, 1, 0))
ALL_PEERS = ((0, 0, 1), (0, 1, 0), (0, 1, 1), (1, 0, 0), (1, 0, 1), (1, 1, 0), (1, 1, 1))
CORE_PEER = ((0, 0, 1),)


def _exchange(name, arrays, peers, n_slots, scatter=False):
    na, nm = len(arrays), len(peers)

    def ident(px, py, pc):
        return {8: 4 * px + 2 * py + pc, 4: 2 * px + py, 2: pc}[n_slots]

    def body(*refs):
        in_refs, out_refs = refs[:na], refs[na:2 * na]
        send_sems, recv_sems = refs[2 * na:]
        x, y, c = lax.axis_index("x"), lax.axis_index("y"), lax.axis_index("c")
        me = ident(x, y, c)
        started = []
        for i in range(na):
            for j, (fx, fy, fc) in enumerate(peers):
                px, py, pc = (1 - x if fx else x), (1 - y if fy else y), (1 - c if fc else c)
                src = in_refs[i].at[ident(px, py, pc)] if scatter else in_refs[i]
                copy = pltpu.make_async_remote_copy(
                    src_ref=src, dst_ref=out_refs[i].at[me],
                    send_sem=send_sems.at[i * nm + j], recv_sem=recv_sems.at[i * nm + j],
                    device_id=(px, py, pc), device_id_type=pl.DeviceIdType.MESH)
                copy.start()
                started.append(copy)
        for copy in started:
            copy.wait()

    any_spec = pl.BlockSpec(memory_space=pl.ANY)
    out_shape = [jax.ShapeDtypeStruct(((n_slots,) + a.shape[1:]) if scatter else ((n_slots,) + a.shape), a.dtype)
                 for a in arrays]
    outs = pl.pallas_call(
        body, name=name, in_specs=[any_spec] * na, out_specs=[any_spec] * na, out_shape=out_shape,
        scratch_shapes=[pltpu.SemaphoreType.DMA((na * nm,)), pltpu.SemaphoreType.DMA((na * nm,))],
    )(*arrays)
    me = ident(lax.axis_index("x"), lax.axis_index("y"), lax.axis_index("c"))
    return [lax.dynamic_update_slice_in_dim(
        o, lax.dynamic_index_in_dim(a, me, 0, keepdims=True) if scatter else a[None], me, axis=0)
        for a, o in zip(arrays, outs)]


def _gather_halves(name, arrays):
    na = len(arrays)
    chips = ((1, 0), (0, 1), (1, 1))

    def body(*refs):
        in_refs, out_refs = refs[:na], refs[na:2 * na]
        ici_send, ici_recv, d2d_send, d2d_recv = refs[2 * na:]
        x, y, c = lax.axis_index("x"), lax.axis_index("y"), lax.axis_index("c")
        me = 2 * x + y
        peers = [((1 - x if fx else x), (1 - y if fy else y)) for fx, fy in chips]
        pending = []
        for i in range(na):
            half = arrays[i].shape[0] // 2
            mine = pl.ds(pl.multiple_of(c * half, 8), half)
            for j, (px, py) in enumerate(peers):
                k = len(chips) * i + j
                out = pltpu.make_async_remote_copy(
                    src_ref=in_refs[i].at[mine], dst_ref=out_refs[i].at[me, mine],
                    send_sem=ici_send.at[k], recv_sem=ici_recv.at[k],
                    device_id=(px, py, c), device_id_type=pl.DeviceIdType.MESH)
                out.start()
                pending.append(out.wait_send)
        for i in range(na):
            half = arrays[i].shape[0] // 2
            mine = pl.ds(pl.multiple_of(c * half, 8), half)
            theirs = pl.ds(pl.multiple_of((1 - c) * half, 8), half)
            for j, (px, py) in enumerate(peers):
                k = len(chips) * i + j
                landed = out_refs[i].at[2 * px + py, mine]
                pltpu.make_async_remote_copy(
                    src_ref=in_refs[i].at[mine], dst_ref=landed, send_sem=ici_send.at[k], recv_sem=ici_recv.at[k],
                    device_id=(px, py, c), device_id_type=pl.DeviceIdType.MESH).wait_recv()
                passed = pltpu.make_async_remote_copy(
                    src_ref=landed, dst_ref=landed, send_sem=d2d_send.at[k], recv_sem=d2d_recv.at[k],
                    device_id=(x, y, 1 - c), device_id_type=pl.DeviceIdType.MESH)
                passed.start()
                pending.append(passed.wait_send)
                from_sibling = out_refs[i].at[2 * px + py, theirs]
                pending.append(pltpu.make_async_remote_copy(
                    src_ref=from_sibling, dst_ref=from_sibling, send_sem=d2d_send.at[k], recv_sem=d2d_recv.at[k],
                    device_id=(x, y, 1 - c), device_id_type=pl.DeviceIdType.MESH).wait_recv)
        for wait in pending:
            wait()

    any_spec = pl.BlockSpec(memory_space=pl.ANY)
    n_sem = na * len(chips)
    outs = pl.pallas_call(
        body, name=name, in_specs=[any_spec] * na, out_specs=[any_spec] * na,
        out_shape=[jax.ShapeDtypeStruct((N_CHIPS,) + a.shape, a.dtype) for a in arrays],
        scratch_shapes=[pltpu.SemaphoreType.DMA((n_sem,))] * 4,
    )(*arrays)
    me = 2 * lax.axis_index("x") + lax.axis_index("y")
    return [lax.dynamic_update_slice_in_dim(o, a[None], me, axis=0) for a, o in zip(arrays, outs)]


def _send_other_half(name, arrays):
    na = len(arrays)

    def body(*refs):
        in_refs, out_refs, send_sems, recv_sems = refs[:na], refs[na:2 * na], refs[-2], refs[-1]
        x, y, c = lax.axis_index("x"), lax.axis_index("y"), lax.axis_index("c")
        copies = []
        for i in range(na):
            half = arrays[i].shape[1] // 2
            theirs = pl.ds(pl.multiple_of((1 - c) * half, 8), half)
            copy = pltpu.make_async_remote_copy(
                src_ref=in_refs[i].at[:, theirs], dst_ref=out_refs[i], send_sem=send_sems.at[i],
                recv_sem=recv_sems.at[i], device_id=(x, y, 1 - c), device_id_type=pl.DeviceIdType.MESH)
            copy.start()
            copies.append(copy)
        for copy in copies:
            copy.wait()

    any_spec = pl.BlockSpec(memory_space=pl.ANY)
    return pl.pallas_call(
        body, name=name, in_specs=[any_spec] * na, out_specs=[any_spec] * na,
        out_shape=[jax.ShapeDtypeStruct((a.shape[0], a.shape[1] // 2, a.shape[2]), a.dtype) for a in arrays],
        scratch_shapes=[pltpu.SemaphoreType.DMA((na,)), pltpu.SemaphoreType.DMA((na,))],
    )(*arrays)


def _adam_math(w, g, m, v):
    m = ADAM_B1 * m + (1.0 - ADAM_B1) * g
    v = ADAM_B2 * v + (1.0 - ADAM_B2) * jnp.square(g)
    m_hat = m / (1.0 - ADAM_B1 ** ADAM_STEP)
    v_hat = v / (1.0 - ADAM_B2 ** ADAM_STEP)
    delta = -ADAM_LR * (m_hat / (jnp.sqrt(v_hat) + ADAM_EPS) + ADAM_WD * w)
    return delta, m, v


def _row_tile(r):
    return _tile(r, (256, 128, 64, 32, 16, 8))


def _sum_parts(name, parts):
    n, r, c = parts.shape
    tr = _row_tile(r)

    def body(p_ref, o_ref):
        tot = p_ref[0].astype(F32)
        for i in range(1, n):
            tot = tot + p_ref[i].astype(F32)
        o_ref[...] = tot

    return pl.pallas_call(
        body, name=name, grid=(r // tr,), in_specs=[pl.BlockSpec((n, tr, c), lambda i: (0, i, 0))],
        out_specs=pl.BlockSpec((tr, c), lambda i: (i, 0)), out_shape=jax.ShapeDtypeStruct((r, c), F32),
        compiler_params=_params(("parallel",)),
    )(parts)


def _adamw(name, w, parts, m, v):
    n, r, c = parts.shape
    tr = _row_tile(r)

    def body(w_ref, p_ref, m_ref, v_ref, g_ref, d_ref, nm_ref, nv_ref):
        g = p_ref[0]
        for i in range(1, n):
            g = g + p_ref[i]
        delta, nm, nv = _adam_math(w_ref[...], g, m_ref[...], v_ref[...])
        g_ref[...], d_ref[...], nm_ref[...], nv_ref[...] = g, delta, nm, nv

    blk = pl.BlockSpec((tr, c), lambda i: (i, 0))
    return pl.pallas_call(
        body, name=name, grid=(r // tr,),
        in_specs=[blk, pl.BlockSpec((n, tr, c), lambda i: (0, i, 0)), blk, blk], out_specs=[blk] * 4,
        out_shape=[jax.ShapeDtypeStruct((r, c), F32)] * 4, compiler_params=_params(("parallel",)),
    )(w, parts, m, v)


def _ada_w_update(act_t, dmod, w, m, v):
    r, c = w.shape
    nb = act_t.shape[1]
    tr, tc = 256, 1024

    def body(a_ref, d_ref, w_ref, m_ref, v_ref, g_ref, dl_ref, nm_ref, nv_ref):
        a, dm = a_ref[...], d_ref[...]
        g = a[:, 0:1] * dm[0:1, :]
        for b in range(1, nb):
            g = g + a[:, b:b + 1] * dm[b:b + 1, :]
        delta, nm, nv = _adam_math(w_ref[...], g, m_ref[...], v_ref[...])
        g_ref[...], dl_ref[...], nm_ref[...], nv_ref[...] = g, delta, nm, nv

    blk = pl.BlockSpec((tr, tc), lambda i, j: (i, j))
    return pl.pallas_call(
        body, name="ada_w_update", grid=(r // tr, c // tc),
        in_specs=[pl.BlockSpec((tr, nb), lambda i, j: (i, 0)), pl.BlockSpec((nb, tc), lambda i, j: (0, j)),
                  blk, blk, blk],
        out_specs=[blk] * 4, out_shape=[jax.ShapeDtypeStruct((r, c), F32)] * 4,
        compiler_params=_params(("parallel", "parallel")),
    )(act_t, dmod, w, m, v)


WEIGHTS = ['ada_w', 'ada_b', 'norm1_gain', 'norm2_gain', 'final_gain', 'w_in', 'w_out', 's5_lambda_re',
           's5_lambda_im', 's5_log_step', 's5_b_re', 's5_b_im', 's5_c_re', 's5_c_im', 's5_d', 's5_w_glu',
           's5_b_glu', 'rk_shift_prev', 'rk_shift_next', 'rk_w0', 'rk_w_up', 'rk_a0', 'rk_a_up', 'rk_g_up',
           'rk_k_k', 'rk_k_a', 'rk_r_k', 'rk_ln_gain', 'rk_ln_bias', 'ffn_w1', 'ffn_w2']
BIG_SHARDED = ['w_in', 'w_out', 's5_w_glu', 'ffn_w1', 'ffn_w2']
RK_SHARDED = ['rk_w0', 'rk_a0', 'rk_w_up', 'rk_a_up', 'rk_g_up']
REPLICATED = ['ada_b', 'norm1_gain', 'norm2_gain', 'final_gain', 's5_lambda_re', 's5_lambda_im', 's5_log_step',
              's5_b_re', 's5_b_im', 's5_c_re', 's5_c_im', 's5_d', 's5_b_glu', 'rk_shift_prev', 'rk_shift_next',
              'rk_k_k', 'rk_k_a', 'rk_r_k', 'rk_ln_gain', 'rk_ln_bias']
PACK_COLS = 1024
N_CHIPS = 4
RK_ROWS = 420
RK_ROWS_PAD = 432


def _pack_rows(arrays, cols):
    return jnp.concatenate([a.reshape(-1, cols) for a in arrays], axis=0)


def _pack_flat(arrays):
    flat = jnp.concatenate([a.reshape(-1) for a in arrays])
    rows = -(-flat.shape[0] // PACK_COLS)
    return jnp.pad(flat, (0, rows * PACK_COLS - flat.shape[0])).reshape(rows, PACK_COLS)


def _unpack_flat(packed, like):
    flat, out, pos = packed.reshape(-1), [], 0
    for a in like:
        out.append(flat[pos:pos + a.size].reshape(a.shape))
        pos += a.size
    return out


def _cols_to_chips(full, n_rows):
    return jnp.transpose(full.reshape(n_rows, N_CHIPS, -1), (1, 0, 2))


def _chips_to_cols(parts):
    return jnp.transpose(parts, (1, 0, 2)).reshape(parts.shape[1], -1)


def kernel(x, c, ada_w, ada_b, norm1_gain, norm2_gain, final_gain, w_in, w_out, s5_lambda_re, s5_lambda_im, s5_log_step, s5_b_re, s5_b_im, s5_c_re, s5_c_im, s5_d, s5_w_glu, s5_b_glu, rk_shift_prev, rk_shift_next, rk_w0, rk_w_up, rk_a0, rk_a_up, rk_g_up, rk_k_k, rk_k_a, rk_r_k, rk_ln_gain, rk_ln_bias, ffn_w1, ffn_w2, loss_target, m_ada_w, m_ada_b, m_norm1_gain, m_norm2_gain, m_final_gain, m_w_in, m_w_out, m_s5_lambda_re, m_s5_lambda_im, m_s5_log_step, m_s5_b_re, m_s5_b_im, m_s5_c_re, m_s5_c_im, m_s5_d, m_s5_w_glu, m_s5_b_glu, m_rk_shift_prev, m_rk_shift_next, m_rk_w0, m_rk_w_up, m_rk_a0, m_rk_a_up, m_rk_g_up, m_rk_k_k, m_rk_k_a, m_rk_r_k, m_rk_ln_gain, m_rk_ln_bias, m_ffn_w1, m_ffn_w2, v_ada_w, v_ada_b, v_norm1_gain, v_norm2_gain, v_final_gain, v_w_in, v_w_out, v_s5_lambda_re, v_s5_lambda_im, v_s5_log_step, v_s5_b_re, v_s5_b_im, v_s5_c_re, v_s5_c_im, v_s5_d, v_s5_w_glu, v_s5_b_glu, v_rk_shift_prev, v_rk_shift_next, v_rk_w0, v_rk_w_up, v_rk_a0, v_rk_a_up, v_rk_g_up, v_rk_k_k, v_rk_k_a, v_rk_r_k, v_rk_ln_gain, v_rk_ln_bias, v_ffn_w1, v_ffn_w2):
    given = dict(locals())
    w = {n: given[n] for n in WEIGHTS}
    m = {n: given["m_" + n] for n in WEIGHTS}
    v = {n: given["v_" + n] for n in WEIGHTS}
    mx, my, mc = lax.axis_index("x"), lax.axis_index("y"), lax.axis_index("c")
    chip = 2 * mx + my
    dev = 2 * chip + mc
    xt, target = x[0], loss_target[0]

    def rk_rows(d):
        rows = _pack_rows([d[n] for n in RK_SHARDED], 256)
        return jnp.pad(rows, ((0, RK_ROWS_PAD - rows.shape[0]), (0, 0)))

    (c_all,) = _exchange("gather_c", [c], ALL_PEERS, 8)
    shards = [w[n][0].astype(BF16) for n in BIG_SHARDED] + [rk_rows(w)]
    gathered = _gather_halves("gather_w", shards)
    full = dict(zip(BIG_SHARDED, gathered[:5]))
    rk_full = gathered[5]

    (act,) = _rowwise("ada_act", lambda q: (q * _sigmoid(q),), [c_all.reshape(8, D_MODEL)], [], [(D_MODEL, F32)], 8)
    n_mod_cols = N_MOD * D_MODEL // N_CHIPS
    bias = jnp.broadcast_to(lax.dynamic_slice(ada_b, (0, chip * n_mod_cols), (1, n_mod_cols)), (8, n_mod_cols))
    mod_shard = _matmul("ada_fwd", act, ada_w[0], epilogue=_add_epilogue, extras=(bias,))
    (mod_parts,) = _exchange("gather_mod", [mod_shard], CHIP_PEERS, N_CHIPS)
    mod_all = _chips_to_cols(mod_parts)
    mod_mine = lax.dynamic_slice(mod_all, (dev, 0), (1, N_MOD * D_MODEL))
    mod = [mod_mine[:, i * D_MODEL:(i + 1) * D_MODEL] for i in range(N_MOD)]

    def rk_piece(lo, hi, lead):
        return _chips_to_cols(rk_full[:, lo:hi]).reshape(lead + (RK_WIDTH,))

    zeros = jnp.zeros((LORA, RK_WIDTH), F32)
    w_up, a_up = rk_piece(4, 132, (2, LORA)), rk_piece(132, 260, (2, LORA))
    wt = {
        "norm1_gain": norm1_gain, "norm2_gain": norm2_gain, "final_gain": final_gain.reshape(1, D_MODEL),
        "w_in": jnp.pad(_chips_to_cols(full["w_in"]), ((0, 0), (0, PROJ_PAD - PROJ))),
        "w_out": full["w_out"].reshape(D_MODEL, D_MODEL),
        "s5_w_glu": full["s5_w_glu"].reshape(S5_WIDTH, S5_WIDTH),
        "ffn_w1": _chips_to_cols(full["ffn_w1"]), "ffn_w2": full["ffn_w2"].reshape(FFN, D_MODEL),
        "mu_prev": jnp.pad(rk_shift_prev, ((0, 0), (0, RK_PAD - RK_IN))),
        "mu_next": jnp.pad(rk_shift_next, ((0, 0), (0, RK_PAD - RK_IN))),
        "lam_re": [s5_lambda_re[0, d].reshape(S5_CH, 1) for d in range(2)],
        "lam_im": [s5_lambda_im[0, d].reshape(S5_CH, 1) for d in range(2)],
        "log_step": [jnp.repeat(s5_log_step[0, d], S5_STATE).reshape(S5_CH, 1) for d in range(2)],
        "b_re": s5_b_re.reshape(S5_CH, S5_GROUP), "b_im": s5_b_im.reshape(S5_CH, S5_GROUP),
        "c_re": s5_c_re[0], "c_im": s5_c_im[0],
        "s5_d": s5_d, "s5_b_glu": s5_b_glu,
        "w0": list(rk_piece(0, 2, (2,))[:, None, :]), "a0": list(rk_piece(2, 4, (2,))[:, None, :]),
        "w_up": [jnp.concatenate([w_up[0], zeros]), jnp.concatenate([zeros, w_up[1]])],
        "a_up": [jnp.concatenate([a_up[0], zeros]), jnp.concatenate([zeros, a_up[1]])],
        "g_up": jnp.pad(rk_piece(260, 420, (GATE_LORA,)), ((0, GATE_PAD - GATE_LORA), (0, 0))),
        "k_k": rk_k_k, "k_a": rk_k_a, "r_k": rk_r_k.reshape(1, RK_WIDTH),
        "ln_gain": rk_ln_gain, "ln_bias": rk_ln_bias,
    }

    loss_rows, grad_x, g = _local_step(xt, target, mod, wt)
    loss = lax.psum(jnp.sum(loss_rows), ("x", "y", "c"))

    big_grads = {
        "w_in": _cols_to_chips(g["w_in"][:, :PROJ], D_MODEL),
        "w_out": g["w_out"].reshape(N_CHIPS, -1, D_MODEL),
        "s5_w_glu": g["s5_w_glu"].reshape(N_CHIPS, -1, S5_WIDTH),
        "ffn_w1": _cols_to_chips(g["ffn_w1"], D_MODEL),
        "ffn_w2": g["ffn_w2"].reshape(N_CHIPS, -1, D_MODEL),
    }
    rk_grads = jnp.concatenate([
        _cols_to_chips(jnp.concatenate(g["w0"]), 2), _cols_to_chips(jnp.concatenate(g["a0"]), 2),
        _cols_to_chips(jnp.concatenate([g["w_up"][0][:LORA], g["w_up"][1][LORA:]]), 2 * LORA),
        _cols_to_chips(jnp.concatenate([g["a_up"][0][:LORA], g["a_up"][1][LORA:]]), 2 * LORA),
        _cols_to_chips(g["g_up"][:GATE_LORA], GATE_LORA),
        jnp.zeros((N_CHIPS, RK_ROWS_PAD - RK_ROWS, 256), F32)], axis=1)
    names = BIG_SHARDED + ["rk"]
    pieces = [big_grads[n] for n in BIG_SHARDED] + [rk_grads]
    from_sibling = _send_other_half("swap_halves", pieces)
    chip_sums = []
    for n, piece, other in zip(names, pieces, from_sibling):
        half, cols = other.shape[1], other.shape[2]
        own = lax.dynamic_slice_in_dim(piece, mc * half, half, axis=1)
        wire = F32 if n == "rk" else BF16
        (both,) = _rowwise("pair_" + n, lambda a, b: (a + b,), [own.reshape(-1, cols), other.reshape(-1, cols)], [],
                           [(cols, wire)], _row_tile(N_CHIPS * half))
        chip_sums.append(both.reshape(N_CHIPS, half, cols))
    arrived = _exchange("scatter_grads", chip_sums, CHIP_PEERS, N_CHIPS, scatter=True)
    half_sums = [_sum_parts("sum_" + n, a) for n, a in zip(names, arrived)]
    pairs = [p.reshape(1, 2 * p.shape[1], p.shape[2]) for p in _exchange("swap_sums", half_sums, CORE_PEER, 2)]

    out = {}
    for n, pair in zip(BIG_SHARDED, pairs[:5]):
        res = _adamw("adamw_" + n, w[n][0], pair, m[n][0], v[n][0])
        out[n] = [r[None] for r in res]
    rk_res = _adamw("adamw_rk", rk_rows(w), pairs[5], rk_rows(m), rk_rows(v))
    for q in range(4):
        pieces, pos = [], 0
        for n in RK_SHARDED:
            rows = w[n].size // 256
            pieces.append(rk_res[q][pos:pos + rows].reshape(w[n].shape))
            pos += rows
        for n, piece in zip(RK_SHARDED, pieces):
            out.setdefault(n, []).append(piece)

    local_small = {
        "ada_b": jnp.concatenate(g["mod"], axis=1),
        "norm1_gain": g["norm1_gain"], "norm2_gain": g["norm2_gain"], "final_gain": g["final_gain"],
        "s5_lambda_re": jnp.concatenate(g["lam_re"]), "s5_lambda_im": jnp.concatenate(g["lam_im"]),
        "s5_log_step": jnp.concatenate([q.reshape(S5_GROUPS, S5_STATE).sum(axis=1) for q in g["log_step"]]),
        "s5_b_re": g["b_re"], "s5_b_im": g["b_im"], "s5_c_re": g["c_re"], "s5_c_im": g["c_im"],
        "s5_d": g["s5_d"], "s5_b_glu": g["s5_b_glu"],
        "rk_shift_prev": g["mu_prev"][:, :RK_IN], "rk_shift_next": g["mu_next"][:, :RK_IN],
        "rk_k_k": g["k_k"], "rk_k_a": g["k_a"], "rk_r_k": g["r_k"],
        "rk_ln_gain": g["ln_gain"], "rk_ln_bias": g["ln_bias"],
    }
    (small_all,) = _exchange("gather_small", [_pack_flat([local_small[n] for n in REPLICATED])], ALL_PEERS, 8)
    small_res = _adamw("adamw_small", _pack_flat([w[n] for n in REPLICATED]), small_all,
                       _pack_flat([m[n] for n in REPLICATED]), _pack_flat([v[n] for n in REPLICATED]))
    for q in range(4):
        for n, piece in zip(REPLICATED, _unpack_flat(small_res[q], [w[n] for n in REPLICATED])):
            out.setdefault(n, []).append(piece)

    mod_rows = N_MOD * D_MODEL // PACK_COLS
    dmod_all = small_all[:, :mod_rows].reshape(8, N_MOD * D_MODEL)
    dmod = lax.dynamic_slice(dmod_all, (0, chip * n_mod_cols), (8, n_mod_cols))
    res = _ada_w_update(act.T, dmod, ada_w[0], m_ada_w[0], v_ada_w[0])
    out["ada_w"] = [r[None] for r in res]

    return (loss, grad_x[None], *[out[n][0] for n in WEIGHTS], *[out[n][1] for n in WEIGHTS],
            *[out[n][2] for n in WEIGHTS], *[out[n][3] for n in WEIGHTS])
```

```python
import functools
import math

import jax
import jax.numpy as jnp
from jax import lax
from jax.experimental import pallas as pl
from jax.experimental.pallas import tpu as pltpu

F32 = jnp.float32
BF16 = jnp.bfloat16

D_MODEL = 2048
S5_WIDTH = 1024
S5_GROUP = 16
S5_GROUPS = 64
S5_STATE = 64
S5_CH = S5_GROUPS * S5_STATE
S5_BLK = 256
RK_WIDTH = 1024
RK_HEAD = 64
RK_HEADS = 16
LORA = 64
GATE_LORA = 160
GATE_PAD = 256
RK_IN = 3488
RK_PAD = 3584
PROJ = 4512
PROJ_PAD = 4608
FFN = 8192
N_MOD = 6
NORM_EPS = 1e-6
GN_EPS = 64e-5
L2_EPS = 1e-12
RK_CHUNK = 64
RK_PASSES = 3
RK_SOLVE_PASSES = 3
LW_SCALE = math.exp(-0.5)
ADAM_LR, ADAM_B1, ADAM_B2, ADAM_EPS, ADAM_WD, ADAM_STEP = 0.001, 0.9, 0.999, 1e-08, 0.01, 10
VMEM_LIMIT = 56 * 1024 * 1024
HI = lax.Precision.HIGHEST


def _params(sem=None):
    return pltpu.CompilerParams(dimension_semantics=sem, vmem_limit_bytes=VMEM_LIMIT)


def _full(a):
    nd = a.ndim
    return pl.BlockSpec(a.shape, lambda *_: (0,) * nd)


@jax.custom_vjp
def _bdot(a, b):
    return jnp.dot(a.astype(BF16), b.astype(BF16), preferred_element_type=F32)


def _bdot_fwd(a, b):
    return _bdot(a, b), (a, b)


def _bdot_bwd(res, g):
    a, b = res
    gb = g.astype(BF16)
    da = lax.dot_general(gb, b.astype(BF16), (((1,), (1,)), ((), ())), preferred_element_type=F32)
    db = lax.dot_general(a.astype(BF16), gb, (((0,), (0,)), ((), ())), preferred_element_type=F32)
    return da, db


_bdot.defvjp(_bdot_fwd, _bdot_bwd)


def _fdot(a, b):
    return jnp.dot(a, b, precision=HI, preferred_element_type=F32)


def _sigmoid(z):
    return 1.0 / (1.0 + jnp.exp(-z))


def _gelu(y):
    return 0.5 * y * (1.0 + jnp.tanh(0.7978845608028654 * (y + 0.044715 * (y * y * y))))


def _rms(x):
    return x * lax.rsqrt(jnp.mean(x * x, axis=-1, keepdims=True) + NORM_EPS)


def _tile(n, prefs):
    for t in prefs:
        if n % t == 0:
            return t
    return n


def _matmul(name, a, b, ta=False, tb=False, epilogue=None, extras=(), out_dtypes=(F32,)):
    m = a.shape[1] if ta else a.shape[0]
    k = a.shape[0] if ta else a.shape[1]
    n = b.shape[0] if tb else b.shape[1]
    assert k == (b.shape[1] if tb else b.shape[0]), (a.shape, b.shape, ta, tb)
    tm = _tile(m, (1024, 512, 256, 128))
    tn = _tile(n, (1024, 768, 512, 256, 128))
    tk = _tile(k, (2048, 1024, 512, 256, 128))
    nk = k // tk
    n_ex, n_out = len(extras), len(out_dtypes)
    dims = (((0 if ta else 1,), (1 if tb else 0,)), ((), ()))

    def body(a_ref, b_ref, *rest):
        ex_refs, out_refs, acc = rest[:n_ex], rest[n_ex:n_ex + n_out], rest[-1]
        kk = pl.program_id(2)

        @pl.when(kk == 0)
        def _():
            acc[...] = jnp.zeros_like(acc)

        acc[...] += lax.dot_general(a_ref[...].astype(BF16), b_ref[...].astype(BF16), dims,
                                    preferred_element_type=F32)

        @pl.when(kk == nk - 1)
        def _():
            res = acc[...]
            outs = epilogue(res, *[e[...] for e in ex_refs]) if epilogue is not None else (res,)
            for o_ref, val in zip(out_refs, outs):
                o_ref[...] = val.astype(o_ref.dtype)

    a_spec = pl.BlockSpec((tk, tm), lambda i, j, q: (q, i)) if ta else pl.BlockSpec((tm, tk), lambda i, j, q: (i, q))
    b_spec = pl.BlockSpec((tn, tk), lambda i, j, q: (j, q)) if tb else pl.BlockSpec((tk, tn), lambda i, j, q: (q, j))
    mn_spec = pl.BlockSpec((tm, tn), lambda i, j, q: (i, j))
    outs = pl.pallas_call(
        body, name=name, grid=(m // tm, n // tn, nk),
        in_specs=[a_spec, b_spec] + [mn_spec] * n_ex,
        out_specs=[mn_spec] * n_out,
        out_shape=[jax.ShapeDtypeStruct((m, n), dt) for dt in out_dtypes],
        scratch_shapes=[pltpu.VMEM((tm, tn), F32)],
        compiler_params=_params(("parallel", "parallel", "arbitrary")),
    )(a, b, *extras)
    return outs[0] if n_out == 1 else outs


def _row_spec(a, tm):
    return pl.BlockSpec((tm, a.shape[1]), lambda i: (i, 0))


def _rowwise(name, fn, rows, params, outs, tm):
    t = rows[0].shape[0]
    tm = min(tm, t)
    n_r, n_p = len(rows), len(params)

    def body(*refs):
        vals = [r[...] for r in refs[:n_r + n_p]]
        res = fn(*vals)
        for o_ref, val in zip(refs[n_r + n_p:], res):
            o_ref[...] = val.astype(o_ref.dtype)

    res = pl.pallas_call(
        body, name=name, grid=(t // tm,),
        in_specs=[_row_spec(r, tm) for r in rows] + [_full(p) for p in params],
        out_specs=[pl.BlockSpec((tm, n), lambda i: (i, 0)) for n, _ in outs],
        out_shape=[jax.ShapeDtypeStruct((t, n), dt) for n, dt in outs],
        compiler_params=_params(("parallel",)),
    )(*rows, *params)
    return res


def _rowwise_vjp(name, fn, rows, params, cts, row_grads, param_grads, tm, consts=(), addends=None,
                 emit=(), row_grad_dtypes=None):
    t = rows[0].shape[0]
    tm = min(tm, t)
    addends = addends or {}
    n_r, n_p, n_c = len(rows), len(params), len(consts)
    ct_flat = [c for group in cts for c in group]
    add_list = [addends[q] for q in sorted(addends)]
    n_ct, n_add = len(ct_flat), len(add_list)
    row_grad_dtypes = row_grad_dtypes or [F32] * len(row_grads)

    def body(*refs):
        pos = 0
        row_v = [r[...].astype(F32) for r in refs[pos:pos + n_r]]; pos += n_r
        par_v = [r[...].astype(F32) for r in refs[pos:pos + n_p]]; pos += n_p
        con_v = [r[...] for r in refs[pos:pos + n_c]]; pos += n_c
        ct_v = [r[...].astype(F32) for r in refs[pos:pos + n_ct]]; pos += n_ct
        add_v = [r[...] for r in refs[pos:pos + n_add]]; pos += n_add
        emit_refs = refs[pos:pos + len(emit)]; pos += len(emit)
        rg_refs = refs[pos:pos + len(row_grads)]; pos += len(row_grads)
        pg_refs = refs[pos:pos + len(param_grads)]

        def diff_fn(*dargs):
            rv, pv = list(row_v), list(par_v)
            for q, i in enumerate(row_grads):
                rv[i] = dargs[q]
            for q, j in enumerate(param_grads):
                pv[j] = dargs[len(row_grads) + q]
            return fn(*rv, *pv, *con_v)

        prim = [row_v[i] for i in row_grads] + [par_v[j] for j in param_grads]
        res, vjp = jax.vjp(diff_fn, *prim)
        ct_vals, q = [], 0
        for o, group in zip(res, cts):
            tot = jnp.zeros_like(o)
            for _ in group:
                tot = tot + ct_v[q]
                q += 1
            ct_vals.append(tot)
        grads = vjp(tuple(ct_vals))
        for e_ref, idx in zip(emit_refs, emit):
            e_ref[...] = res[idx].astype(e_ref.dtype)
        add_pos = {p: q for q, p in enumerate(sorted(addends))}
        for q, g_ref in enumerate(rg_refs):
            g = grads[q]
            if q in add_pos:
                g = g + add_v[add_pos[q]]
            g_ref[...] = g.astype(g_ref.dtype)

        @pl.when(pl.program_id(0) == 0)
        def _():
            for g_ref in pg_refs:
                g_ref[...] = jnp.zeros_like(g_ref)

        for q, g_ref in enumerate(pg_refs):
            g_ref[...] += grads[len(row_grads) + q]

    emit_shapes = []
    if emit:
        probe = jax.eval_shape(lambda *a: fn(*a), *[jax.ShapeDtypeStruct((tm, r.shape[1]), F32) for r in rows],
                               *[jax.ShapeDtypeStruct(p.shape, p.dtype) for p in params],
                               *[jax.ShapeDtypeStruct(c.shape, c.dtype) for c in consts])
        emit_shapes = [probe[idx].shape[1] for idx in emit]
    out_specs = ([pl.BlockSpec((tm, n), lambda i: (i, 0)) for n in emit_shapes]
                 + [_row_spec(rows[i], tm) for i in row_grads]
                 + [_full(params[j]) for j in param_grads])
    out_shape = ([jax.ShapeDtypeStruct((t, n), F32) for n in emit_shapes]
                 + [jax.ShapeDtypeStruct(rows[i].shape, dt) for i, dt in zip(row_grads, row_grad_dtypes)]
                 + [jax.ShapeDtypeStruct(params[j].shape, F32) for j in param_grads])
    return pl.pallas_call(
        body, name=name, grid=(t // tm,),
        in_specs=([_row_spec(r, tm) for r in rows] + [_full(p) for p in params] + [_full(c) for c in consts]
                  + [_row_spec(c, tm) for c in ct_flat] + [_row_spec(a, tm) for a in add_list]),
        out_specs=out_specs, out_shape=out_shape,
        compiler_params=_params(("arbitrary",)),
    )(*rows, *params, *consts, *ct_flat, *add_list)


def _norm_mod_fn(x, gain, scale, shift):
    return (_rms(x) * gain * (1.0 + scale) + shift,)


def _resid_norm_mod_fn(x, mixed, gate, gain, scale, shift):
    x1 = x + gate * mixed
    return x1, _rms(x1) * gain * (1.0 + scale) + shift


def _loss_fn(x1, ffn, target, gate, gain):
    y = _rms(x1 + gate * ffn) * gain
    err = y - target
    return (0.5 * jnp.mean(err * err, axis=-1, keepdims=True),)


def _s5_out_fn(ylin, u, d_skip, w_glu, b_glu):
    z = _gelu(ylin + d_skip * u)
    return (z * _sigmoid(_bdot(z, w_glu) + b_glu),)


def _rk_pre_fn(k, wdn, adn, gdn, w0_0, w0_1, wup_0, wup_1, a0_0, a0_1, aup_0, aup_1, g_up, k_k, k_a, seg, seg_t):
    kkr = k * k_k
    inv = 1.0 / jnp.sqrt(jnp.maximum(_fdot(kkr * kkr, seg), L2_EPS * L2_EPS))
    kk = kkr * _fdot(inv, seg_t)
    tw = jnp.tanh(wdn)
    lws, kds, acts = [], [], []
    for w0, wup, a0, aup in ((w0_0, wup_0, a0_0, aup_0), (w0_1, wup_1, a0_1, aup_1)):
        lws.append(-LW_SCALE * _sigmoid(w0 + _bdot(tw, wup)))
        act = _sigmoid(a0 + _bdot(adn, aup))
        acts.append(act)
        kds.append(k * (1.0 + (act - 1.0) * k_a))
    gate = _bdot(_sigmoid(gdn), g_up)
    return (kk, lws[0], lws[1], kds[0], kds[1], acts[0], acts[1], gate)


def _rk_post_fn(y0, y1, r, v, kd0, kd1, gate, ln_gain, ln_bias, r_k, seg, seg_t):
    y = y0 + y1
    mu = _fdot(_fdot(y, seg) * (1.0 / RK_HEAD), seg_t)
    yc = y - mu
    var = _fdot(yc * yc, seg) * (1.0 / RK_HEAD)
    yn = yc * _fdot(lax.rsqrt(var + GN_EPS), seg_t) * ln_gain + ln_bias
    bonus = _fdot(_fdot(r * (kd0 + kd1) * r_k, seg), seg_t)
    return ((yn + bonus * v) * gate,)


def _s5_prep_fn(lr0, li0, ls0, lr1, li1, ls1, b_re, b_im):
    outs = []
    for lam_re, lam_im, ls in ((lr0, li0, ls0), (lr1, li1, ls1)):
        step = jnp.exp(ls)
        mag = jnp.exp(lam_re * step)
        lbar_re = mag * jnp.cos(lam_im * step)
        lbar_im = mag * jnp.sin(lam_im * step)
        den = lam_re * lam_re + lam_im * lam_im
        nr = lbar_re - 1.0
        coef_re = (nr * lam_re + lbar_im * lam_im) / den
        coef_im = (lbar_im * lam_re - nr * lam_im) / den
        outs += [lbar_re, lbar_im, coef_re * b_re - coef_im * b_im, coef_re * b_im + coef_im * b_re]
    return tuple(outs)


def _shift_rows(x, down):
    t = x.shape[0]
    rows = lax.broadcasted_iota(jnp.int32, x.shape, 0)
    if down:
        return jnp.where(rows >= 1, pltpu.roll(x, 1, 0), 0.0)
    return jnp.where(rows < t - 1, pltpu.roll(x, t - 1, 0), 0.0)


def _token_shift(p, mu_prev, mu_next):
    t, n = p.shape

    def body(p_ref, mp_ref, mn_ref, o_ref):
        x = p_ref[...]
        o_ref[...] = x + mp_ref[...] * (_shift_rows(x, True) - x) + mn_ref[...] * (_shift_rows(x, False) - x)

    col = pl.BlockSpec((t, 128), lambda j: (0, j))
    par = pl.BlockSpec((1, 128), lambda j: (0, j))
    return pl.pallas_call(
        body, name="token_shift", grid=(n // 128,), in_specs=[col, par, par], out_specs=col,
        out_shape=jax.ShapeDtypeStruct((t, n), F32), compiler_params=_params(("parallel",)),
    )(p, mu_prev, mu_next)


def _token_shift_bwd(p, mu_prev, mu_next, dps):
    t, n = p.shape

    def body(p_ref, mp_ref, mn_ref, d_ref, dp_ref, dmp_ref, dmn_ref):
        x, d, mp, mn = p_ref[...], d_ref[...], mp_ref[...], mn_ref[...]
        dp_ref[...] = d * (1.0 - mp - mn) + _shift_rows(d * mp, False) + _shift_rows(d * mn, True)
        dmp_ref[...] = jnp.sum(d * (_shift_rows(x, True) - x), axis=0, keepdims=True)
        dmn_ref[...] = jnp.sum(d * (_shift_rows(x, False) - x), axis=0, keepdims=True)

    col = pl.BlockSpec((t, 128), lambda j: (0, j))
    par = pl.BlockSpec((1, 128), lambda j: (0, j))
    return pl.pallas_call(
        body, name="token_shift_bwd", grid=(n // 128,), in_specs=[col, par, par, col],
        out_specs=[col, par, par],
        out_shape=[jax.ShapeDtypeStruct((t, n), F32), jax.ShapeDtypeStruct((1, n), F32),
                   jax.ShapeDtypeStruct((1, n), F32)],
        compiler_params=_params(("parallel",)),
    )(p, mu_prev, mu_next, dps)


N_SEG = 32
S5_BLOCKS = 32
S5_PER_IN = 4


def _scan_in_place(sr_ref, si_ref, ar, ai, carry_ref, reverse):
    seg_len = sr_ref.shape[0] // N_SEG
    ng = N_SEG // 8

    def rows(i, grp):
        first = (seg_len - 1 - i if reverse else i) * N_SEG + 8 * grp
        return pl.ds(pl.multiple_of(first, 8), 8)

    zero = jnp.zeros((8, 128), F32)
    one = jnp.ones((8, 128), F32)

    def local(i, c):
        pr, pi = c[-2:]
        out = []
        for grp in range(ng):
            sr, si = c[2 * grp], c[2 * grp + 1]
            nr = ar * sr - ai * si + sr_ref[rows(i, grp), :]
            ni = ar * si + ai * sr + si_ref[rows(i, grp), :]
            sr_ref[rows(i, grp), :] = nr
            si_ref[rows(i, grp), :] = ni
            out += [nr, ni]
        return tuple(out) + (ar * pr - ai * pi, ar * pi + ai * pr)

    ends = lax.fori_loop(0, seg_len, local, (zero,) * (2 * ng) + (one, zero))
    qr, qi = ends[-2][0:1], ends[-1][0:1]
    order = list(range(N_SEG - 1, -1, -1)) if reverse else list(range(N_SEG))
    cr = jnp.zeros((1, 128), F32)
    ci = jnp.zeros((1, 128), F32)
    for j in order:
        carry_ref[j:j + 1, :] = cr
        carry_ref[N_SEG + j:N_SEG + j + 1, :] = ci
        grp, sub = divmod(j, 8)
        lr, li = ends[2 * grp][sub:sub + 1], ends[2 * grp + 1][sub:sub + 1]
        cr, ci = lr + qr * cr - qi * ci, li + qr * ci + qi * cr
    carries = [(carry_ref[8 * grp:8 * grp + 8, :], carry_ref[N_SEG + 8 * grp:N_SEG + 8 * grp + 8, :])
               for grp in range(ng)]

    def fix(i, c):
        pr, pi = c
        npr, npi = ar * pr - ai * pi, ar * pi + ai * pr
        for grp in range(ng):
            cr8, ci8 = carries[grp]
            sr_ref[rows(i, grp), :] = sr_ref[rows(i, grp), :] + npr * cr8 - npi * ci8
            si_ref[rows(i, grp), :] = si_ref[rows(i, grp), :] + npr * ci8 + npi * cr8
        return npr, npi

    lax.fori_loop(0, seg_len, fix, (one, zero))


def _interleave(x):
    t, c = x.shape
    return jnp.transpose(x.reshape(N_SEG, t // N_SEG, c), (1, 0, 2)).reshape(t, c)


def _deinterleave(x):
    t, c = x.shape
    return jnp.transpose(x.reshape(t // N_SEG, N_SEG, c), (1, 0, 2)).reshape(t, c)


def _step_neighbour(s, earlier):
    t = s.shape[0]
    rows = lax.broadcasted_iota(jnp.int32, s.shape, 0)
    if earlier:
        return jnp.where(rows >= N_SEG, pltpu.roll(s, N_SEG, 0),
                         jnp.where(rows >= 1, pltpu.roll(s, N_SEG + 1, 0), 0.0))
    return jnp.where(rows < t - N_SEG, pltpu.roll(s, t - N_SEG, 0),
                     jnp.where(rows < t - 1, pltpu.roll(s, t - N_SEG - 1, 0), 0.0))


def _dot_bf16(a, b, dims=(((1,), (0,)), ((), ()))):
    return lax.dot_general(a.astype(BF16), b.astype(BF16), dims, preferred_element_type=F32)


NT_DIMS = (((1,), (1,)), ((), ()))
TN_DIMS = (((0,), (0,)), ((), ()))


def _s5_specs(t):
    blk = pl.BlockSpec((None, t, 128), lambda i, q: (S5_PER_IN * i + q, 0, 0))
    mat = pl.BlockSpec((None, 128, 128), lambda i, q: (S5_PER_IN * i + q, 0, 0))
    vec = pl.BlockSpec((None, 1, 128), lambda i, q: (S5_PER_IN * i + q, 0, 0))
    chan = pl.BlockSpec((t, 128), lambda i, q: (0, i))
    return blk, mat, vec, chan


S5_GRID = (S5_BLOCKS // S5_PER_IN, S5_PER_IN)


def _s5_forward(name, u, b_re, b_im, l_re, l_im, reverse, other=None, c_re=None, c_im_neg=None):
    t = u.shape[0]
    project = other is not None
    blk, mat, vec, chan = _s5_specs(t)

    def body(*refs):
        u_ref, br_ref, bi_ref, lr_ref, li_ref = refs[:5]
        if project:
            or_ref, oi_ref, cr_ref, ci_ref, sr_ref, si_ref, y_ref, carry_ref = refs[5:]
        else:
            sr_ref, si_ref, carry_ref = refs[5:]
        uv = u_ref[...]
        sr_ref[...] = _dot_bf16(uv, br_ref[...])
        si_ref[...] = _dot_bf16(uv, bi_ref[...])
        ar = jnp.broadcast_to(lr_ref[...], (8, 128))
        ai = jnp.broadcast_to(li_ref[...], (8, 128))
        _scan_in_place(sr_ref, si_ref, ar, ai, carry_ref, reverse)
        if project:
            y = (_dot_bf16(sr_ref[...] + or_ref[...], cr_ref[...])
                 + _dot_bf16(si_ref[...] + oi_ref[...], ci_ref[...]))

            @pl.when(pl.program_id(1) == 0)
            def _():
                y_ref[...] = y

            @pl.when(pl.program_id(1) != 0)
            def _():
                y_ref[...] += y

    state = jax.ShapeDtypeStruct((S5_BLOCKS, t, 128), F32)
    ins = [u, b_re, b_im, l_re, l_im] + ([other[0], other[1], c_re, c_im_neg] if project else [])
    in_specs = [chan, mat, mat, vec, vec] + ([blk, blk, mat, mat] if project else [])
    return pl.pallas_call(
        body, name=name, grid=S5_GRID, in_specs=in_specs,
        out_specs=[blk, blk] + ([chan] if project else []),
        out_shape=[state, state] + ([jax.ShapeDtypeStruct((t, S5_WIDTH), F32)] if project else []),
        scratch_shapes=[pltpu.VMEM((2 * N_SEG, 128), F32)],
        compiler_params=_params(("arbitrary", "arbitrary")),
    )(*ins)


def _s5_backward(name, dy, u, du_in, states, other, b_re, b_im, c_re, c_im_neg, l_re, l_im, reverse):
    t = u.shape[0]
    with_c = other is not None
    blk, mat, vec, chan = _s5_specs(t)

    def body(*refs):
        dy_ref, u_ref, du_in_ref, sr_ref, si_ref = refs[:5]
        pos = 5
        if with_c:
            or_ref, oi_ref = refs[5:7]
            pos = 7
        br_ref, bi_ref, cr_ref, ci_ref, lr_ref, li_ref = refs[pos:pos + 6]
        outs = refs[pos + 6:]
        du_ref, dbr_ref, dbi_ref, dlr_ref, dli_ref = outs[:5]
        lam_r, lam_i, carry_ref = outs[-3:]
        dyv, uv = dy_ref[...], u_ref[...]
        lam_r[...] = _dot_bf16(dyv, cr_ref[...], NT_DIMS)
        lam_i[...] = _dot_bf16(dyv, ci_ref[...], NT_DIMS)
        ar = jnp.broadcast_to(lr_ref[...], (8, 128))
        ai = -jnp.broadcast_to(li_ref[...], (8, 128))
        _scan_in_place(lam_r, lam_i, ar, ai, carry_ref, not reverse)
        lr, li = lam_r[...], lam_i[...]
        pr, pi = _step_neighbour(sr_ref[...], not reverse), _step_neighbour(si_ref[...], not reverse)
        dlr_ref[...] = jnp.sum(lr * pr + li * pi, axis=0, keepdims=True)
        dli_ref[...] = jnp.sum(li * pr - lr * pi, axis=0, keepdims=True)
        dbr_ref[...] = _dot_bf16(uv, lr, TN_DIMS)
        dbi_ref[...] = _dot_bf16(uv, li, TN_DIMS)
        du = _dot_bf16(lr, br_ref[...], NT_DIMS) + _dot_bf16(li, bi_ref[...], NT_DIMS)

        @pl.when(pl.program_id(1) == 0)
        def _():
            du_ref[...] = du_in_ref[...] + du

        @pl.when(pl.program_id(1) != 0)
        def _():
            du_ref[...] += du

        if with_c:
            dcr_ref, dci_ref = outs[5:7]
            dcr_ref[...] = _dot_bf16(sr_ref[...] + or_ref[...], dyv, TN_DIMS)
            dci_ref[...] = _dot_bf16(si_ref[...] + oi_ref[...], dyv, TN_DIMS)

    mats = jax.ShapeDtypeStruct((S5_BLOCKS, 128, 128), F32)
    vecs = jax.ShapeDtypeStruct((S5_BLOCKS, 1, 128), F32)
    ins = [dy, u, du_in, states[0], states[1]] + ([other[0], other[1]] if with_c else [])
    ins += [b_re, b_im, c_re, c_im_neg, l_re, l_im]
    in_specs = [chan, chan, chan, blk, blk] + ([blk, blk] if with_c else []) + [mat] * 4 + [vec] * 2
    return pl.pallas_call(
        body, name=name, grid=S5_GRID, in_specs=in_specs,
        out_specs=[chan, mat, mat, vec, vec] + ([mat, mat] if with_c else []),
        out_shape=[jax.ShapeDtypeStruct((t, S5_WIDTH), F32), mats, mats, vecs, vecs] + ([mats, mats] if with_c else []),
        scratch_shapes=[pltpu.VMEM((t, 128), F32), pltpu.VMEM((t, 128), F32), pltpu.VMEM((2 * N_SEG, 128), F32)],
        compiler_params=_params(("arbitrary", "arbitrary")),
    )(*ins)


def _ein(passes, spec, a, b):
    if passes == 6:
        return jnp.einsum(spec, a, b, precision=HI, preferred_element_type=F32)
    a_hi, b_hi = a.astype(BF16), b.astype(BF16)
    if passes == 1:
        return jnp.einsum(spec, a_hi, b_hi, preferred_element_type=F32)
    a_lo = (a - a_hi.astype(F32)).astype(BF16)
    b_lo = (b - b_hi.astype(F32)).astype(BF16)
    cross = jnp.einsum(spec, a_hi, b_lo, preferred_element_type=F32)
    if spec.startswith('hik'):
        m = a.shape[1]
        stacked = jnp.einsum(spec, jnp.concatenate([a_hi, a_lo], axis=1), b_hi, preferred_element_type=F32)
        return stacked[:, :m] + stacked[:, m:] + cross
    return (jnp.einsum(spec, a_hi, b_hi, preferred_element_type=F32) + cross
            + jnp.einsum(spec, a_lo, b_hi, preferred_element_type=F32))


@jax.custom_vjp
def _tri_mm(tri, tri_t, z):
    return jnp.einsum('hik,hkj->hij', tri, z, precision=HI, preferred_element_type=F32)


def _tri_mm_bwd(res, g):
    tri, tri_t = res
    return jnp.zeros_like(tri), jnp.zeros_like(tri_t), _tri_mm(tri_t, tri, g)


_tri_mm.defvjp(lambda tri, tri_t, z: (_tri_mm(tri, tri_t, z), (tri, tri_t)), _tri_mm_bwd)


def _chunk_cumsum(lw, incl, incl_t):
    shape = (lw.shape[0],) + incl.shape
    return _tri_mm(jnp.broadcast_to(incl.astype(F32), shape), jnp.broadcast_to(incl_t.astype(F32), shape), lw)


@functools.partial(jax.custom_vjp, nondiff_argnums=(0,))
def _bmm(p, a, b):
    return _ein(p, 'hik,hkj->hij', a, b)


@functools.partial(jax.custom_vjp, nondiff_argnums=(0,))
def _bmm_nt(p, a, b):
    return _ein(p, 'hik,hjk->hij', a, b)


@functools.partial(jax.custom_vjp, nondiff_argnums=(0,))
def _bmm_tn(p, a, b):
    return _ein(p, 'hki,hkj->hij', a, b)


_bmm.defvjp(lambda p, a, b: (_bmm(p, a, b), (a, b)),
            lambda p, res, g: (_bmm_nt(p, g, res[1]), _bmm_tn(p, res[0], g)))
_bmm_nt.defvjp(lambda p, a, b: (_bmm_nt(p, a, b), (a, b)),
               lambda p, res, g: (_bmm(p, g, res[1]), _bmm_tn(p, g, res[0])))
_bmm_tn.defvjp(lambda p, a, b: (_bmm_tn(p, a, b), (a, b)),
               lambda p, res, g: (_bmm_nt(p, res[1], g), _bmm(p, res[0], g)))


@jax.custom_vjp
def _split_rows(x):
    c = x.shape[1] // 2
    return x[:, :c], x[:, c:]


_split_rows.defvjp(lambda x: (_split_rows(x), None), lambda _, g: (jnp.concatenate(g, axis=1),))


def _stack_rows(a, b):
    return jnp.concatenate([a, b], axis=1)


def _rk_chunk(s0, r, lw, k, v, kk, a, reverse):
    h, c, n = r.shape
    row = lax.broadcasted_iota(jnp.int32, (c, c), 0)
    col = lax.broadcasted_iota(jnp.int32, (c, c), 1)
    incl = (row <= col) if reverse else (row >= col)
    strict = (row < col) if reverse else (row > col)
    cum = _chunk_cumsum(lw, incl, (row >= col) if reverse else (row <= col))
    g_in = jnp.exp(cum)
    g_inv = jnp.exp(-cum)
    kap = kk * jnp.exp(cum - lw)
    beta = kk * a * g_inv
    kt = k * g_inv
    rt = r * g_in
    pm, ps = RK_PASSES, RK_SOLVE_PASSES
    both = _stack_rows(kap, rt)
    kap_beta, rt_beta = _split_rows(_bmm_nt(ps, both, beta))
    kap_kt, rt_kt = _split_rows(_bmm_nt(pm, both, kt))
    kap_s0, rt_s0 = _split_rows(_bmm_nt(pm, both, s0))
    l_mat = jnp.where(strict, kap_beta, 0.0)
    rhs = kap_s0 + _bmm(pm, jnp.where(strict, kap_kt, 0.0), v)
    x = -l_mat
    inv = jnp.where(row == col, 1.0, 0.0) + x
    power = _bmm(ps, x, x)
    span = 2
    while 2 * span < c:
        step, power = _split_rows(_bmm(ps, _stack_rows(inv, power), power))
        inv = inv + step
        span *= 2
    inv = inv + _bmm(ps, inv, power)
    u = _bmm(ps, inv, rhs)
    y = rt_s0 + _bmm(pm, jnp.where(incl, rt_kt, 0.0), v) - _bmm(pm, jnp.where(incl, rt_beta, 0.0), u)
    s1 = ((s0 + _bmm_tn(pm, _stack_rows(v, -u), _stack_rows(kt, beta)))
          * jnp.exp(jnp.sum(lw, axis=1, keepdims=True)))
    return y, s1


class _Plan:
    def __init__(self, arrays, out_shape, sems, start, wait, finish):
        self.arrays, self.out_shape, self.sems = list(arrays), list(out_shape), list(sems)
        self.start, self.wait, self.finish = start, wait, finish


_NO_PLAN = _Plan([], [], [], lambda *_: None, lambda *_: None, lambda outs: [])


def _rk_core_fwd(name, r, lw, k, v, kk, a, reverse, chunk, hosted=None):
    h, t, n = r.shape
    nc = t // chunk

    def idx(i):
        return nc - 1 - i if reverse else i

    hosted = hosted or _NO_PLAN
    nh = len(hosted.arrays)

    def body(r_ref, lw_ref, k_ref, v_ref, kk_ref, a_ref, *rest):
        host_in, (y_ref, ck_ref), host_out = rest[:nh], rest[nh:nh + 2], rest[nh + 2:2 * nh + 2]
        s_ref, sems = rest[2 * nh + 2], rest[2 * nh + 3:]

        @pl.when(pl.program_id(0) == 0)
        def _():
            s_ref[...] = jnp.zeros_like(s_ref)
            hosted.start(host_in, host_out, sems)

        s0 = s_ref[...]
        ck_ref[0] = s0
        y, s1 = _rk_chunk(s0, r_ref[...], lw_ref[...], k_ref[...], v_ref[...], kk_ref[...], a_ref[...], reverse)
        y_ref[...] = y
        s_ref[...] = s1

        @pl.when(pl.program_id(0) == nc - 1)
        def _():
            hosted.wait(host_in, host_out, sems)

    blk = pl.BlockSpec((h, chunk, n), lambda i: (0, idx(i), 0))
    any_spec = pl.BlockSpec(memory_space=pl.ANY)
    res = pl.pallas_call(
        body, name=name, grid=(nc,), in_specs=[blk] * 6 + [any_spec] * nh,
        out_specs=[blk, pl.BlockSpec((1, h, n, n), lambda i: (idx(i), 0, 0, 0))] + [any_spec] * nh,
        out_shape=[jax.ShapeDtypeStruct((h, t, n), F32), jax.ShapeDtypeStruct((nc, h, n, n), F32)] + hosted.out_shape,
        scratch_shapes=[pltpu.VMEM((h, n, n), F32)] + hosted.sems,
        compiler_params=_params(("arbitrary",)),
    )(r, lw, k, v, kk, a, *hosted.arrays)
    return res[0], res[1], hosted.finish(res[2:])


def _rk_core_bwd(name, r, lw, k, v, kk, a, ck, dy, reverse, chunk, hosted=None):
    h, t, n = r.shape
    nc = t // chunk
    hosted = hosted or _NO_PLAN
    nh = len(hosted.arrays)

    def idx(i):
        return i if reverse else nc - 1 - i

    def body(r_ref, lw_ref, k_ref, v_ref, kk_ref, a_ref, ck_ref, dy_ref, *rest):
        host_in, out_refs, host_out = rest[:nh], rest[nh:nh + 6], rest[nh + 6:2 * nh + 6]
        ds_ref, sems = rest[2 * nh + 6], rest[2 * nh + 7:]

        @pl.when(pl.program_id(0) == 0)
        def _():
            ds_ref[...] = jnp.zeros_like(ds_ref)
            hosted.start(host_in, host_out, sems)

        fn = functools.partial(_rk_chunk, reverse=reverse)
        _, vjp = jax.vjp(fn, ck_ref[0], r_ref[...], lw_ref[...], k_ref[...], v_ref[...], kk_ref[...], a_ref[...])
        grads = vjp((dy_ref[...], ds_ref[...]))
        ds_ref[...] = grads[0]
        for o_ref, g in zip(out_refs, grads[1:]):
            o_ref[...] = g

        @pl.when(pl.program_id(0) == nc - 1)
        def _():
            hosted.wait(host_in, host_out, sems)

    blk = pl.BlockSpec((h, chunk, n), lambda i: (0, idx(i), 0))
    any_spec = pl.BlockSpec(memory_space=pl.ANY)
    res = pl.pallas_call(
        body, name=name, grid=(nc,),
        in_specs=[blk] * 6 + [pl.BlockSpec((1, h, n, n), lambda i: (idx(i), 0, 0, 0)), blk] + [any_spec] * nh,
        out_specs=[blk] * 6 + [any_spec] * nh,
        out_shape=[jax.ShapeDtypeStruct((h, t, n), F32)] * 6 + hosted.out_shape,
        scratch_shapes=[pltpu.VMEM((h, n, n), F32)] + hosted.sems,
        compiler_params=_params(("arbitrary",)),
    )(r, lw, k, v, kk, a, ck, dy, *hosted.arrays)
    return res[:6], hosted.finish(res[6:])


def _to_heads(x):
    return jnp.transpose(x.reshape(x.shape[0], RK_HEADS, RK_HEAD), (1, 0, 2))


def _from_heads(x):
    return jnp.transpose(x, (1, 0, 2)).reshape(x.shape[1], RK_WIDTH)


def _s5_band_place():
    return jax.nn.one_hot(jnp.arange(S5_BLOCKS) % S5_PER_IN, S5_PER_IN, dtype=F32)


def _s5_in_blocks(bbar):
    b = jnp.transpose(bbar.reshape(S5_BLOCKS, 2, S5_STATE, S5_GROUP), (0, 1, 3, 2))
    band = jnp.einsum('jghp,gk->jghkp', b, jnp.eye(2, dtype=F32)).reshape(S5_BLOCKS, 32, 128)
    return jnp.einsum('jrc,jq->jqrc', band, _s5_band_place()).reshape(S5_BLOCKS, 128, 128)


def _s5_in_unblock(mats):
    band = jnp.einsum('jqrc,jq->jrc', mats.reshape(S5_BLOCKS, S5_PER_IN, 32, 128), _s5_band_place())
    diag = jnp.einsum('jghgp->jghp', band.reshape(S5_BLOCKS, 2, S5_GROUP, 2, S5_STATE))
    return jnp.transpose(diag, (0, 1, 3, 2)).reshape(S5_CH, S5_GROUP)


def _s5_out_blocks(c):
    ct = jnp.transpose(c.reshape(S5_BLOCKS, 2, S5_GROUP, S5_STATE), (0, 1, 3, 2))
    band = jnp.einsum('jgph,gk->jgpkh', ct, jnp.eye(2, dtype=F32)).reshape(S5_BLOCKS, 128, 32)
    return jnp.einsum('jrc,jq->jrqc', band, _s5_band_place()).reshape(S5_BLOCKS, 128, 128)


def _s5_out_unblock(mats):
    band = jnp.einsum('jrqc,jq->jrc', mats.reshape(S5_BLOCKS, 128, S5_PER_IN, 32), _s5_band_place())
    diag = jnp.einsum('jgpgh->jgph', band.reshape(S5_BLOCKS, 2, S5_STATE, 2, S5_GROUP))
    return jnp.transpose(diag, (0, 1, 3, 2)).reshape(S5_GROUPS, S5_GROUP, S5_STATE)


def _head_indicator():
    ch = lax.broadcasted_iota(jnp.int32, (RK_WIDTH, 128), 0) // RK_HEAD
    hd = lax.broadcasted_iota(jnp.int32, (RK_WIDTH, 128), 1)
    seg = (ch == hd).astype(F32)
    return seg, seg.T


def _add_epilogue(acc, e):
    return (acc + e,)


def _local_step(x, target, mod, wt, chunk=RK_CHUNK, ffn_shards=None, ffn_reduce=None):
    t = x.shape[0]
    wt = dict(wt)
    sh1, sc1, gt1, sh2, sc2, gt2 = mod
    seg, seg_t = _head_indicator()
    g = {}

    (h1,) = _rowwise("norm1", _norm_mod_fn, [x], [wt["norm1_gain"], sc1, sh1], [(D_MODEL, BF16)], 256)
    proj = _matmul("proj", h1, wt["w_in"])
    u, p = proj[:, :S5_WIDTH], proj[:, S5_WIDTH:]
    ps = _token_shift(p, wt["mu_prev"], wt["mu_next"])
    r, k, v = ps[:, :1024], ps[:, 1024:2048], ps[:, 2048:3072]
    wdn, adn, gdn = ps[:, 3072:3200], ps[:, 3200:3328], ps[:, 3328:RK_PAD]

    prep_rows = [wt["lam_re"][0], wt["lam_im"][0], wt["log_step"][0], wt["lam_re"][1], wt["lam_im"][1],
                 wt["log_step"][1], wt["b_re"], wt["b_im"]]
    col1, col16 = (1, F32), (S5_GROUP, F32)
    prep = _rowwise("s5_prep", _s5_prep_fn, prep_rows, [], [col1, col1, col16, col16] * 2, 512)
    lbar = [tuple(prep[4 * d + q].reshape(S5_BLOCKS, 1, 128) for q in range(2)) for d in range(2)]
    b_blk = [tuple(_s5_in_blocks(prep[4 * d + 2 + q]) for q in range(2)) for d in range(2)]
    c_blk = (_s5_out_blocks(wt["c_re"]), -_s5_out_blocks(wt["c_im"]))
    u_il = _interleave(u)
    state0 = _s5_forward("s5_fwd0", u_il, *b_blk[0], *lbar[0], reverse=False)
    s1_re, s1_im, ylin_il = _s5_forward("s5_fwd1", u_il, *b_blk[1], *lbar[1], reverse=True, other=state0,
                                        c_re=c_blk[0], c_im_neg=c_blk[1])
    ylin = _deinterleave(ylin_il)
    states = [tuple(state0), (s1_re, s1_im)]
    s5_par = [wt["s5_d"], wt["s5_w_glu"], wt["s5_b_glu"]]
    (o_s5,) = _rowwise("s5_out", _s5_out_fn, [ylin, u], s5_par, [(S5_WIDTH, BF16)], 256)

    pre_par = [wt["w0"][0], wt["w0"][1], wt["w_up"][0], wt["w_up"][1], wt["a0"][0], wt["a0"][1],
               wt["a_up"][0], wt["a_up"][1], wt["g_up"], wt["k_k"], wt["k_a"]]
    pre = _rowwise("rk_pre", _rk_pre_fn, [k, wdn, adn, gdn], pre_par + [seg, seg_t], [(RK_WIDTH, F32)] * 8, 256)
    kk, lw, kd, act, gate = pre[0], pre[1:3], pre[3:5], pre[5:7], pre[7]
    rh, vh, kkh = _to_heads(r), _to_heads(v), _to_heads(kk)
    core_in, ys, cks = [], [], []
    for d in range(2):
        ops = (rh, _to_heads(lw[d]), _to_heads(kd[d]), vh, kkh, _to_heads(act[d]))
        plan = _gather_halves_plan([ffn_shards[d]]) if ffn_shards is not None else None
        y_h, ck, gathered = _rk_core_fwd(f"rk_core{d}", *ops, reverse=(d == 1), chunk=min(chunk, t), hosted=plan)
        if gathered:
            wt["ffn_w1" if d == 0 else "ffn_w2"] = (_chips_to_cols(gathered[0]) if d == 0
                                                    else gathered[0].reshape(FFN, D_MODEL))
        core_in.append(ops)
        ys.append(_from_heads(y_h))
        cks.append(ck)
    post_rows = [ys[0], ys[1], r, v, kd[0], kd[1], gate]
    post_par = [wt["ln_gain"], wt["ln_bias"], wt["r_k"]]
    (o_rk,) = _rowwise("rk_post", _rk_post_fn, post_rows, post_par + [seg, seg_t], [(RK_WIDTH, BF16)], 256)

    o = jnp.concatenate([o_s5, o_rk], axis=1)
    mixed = _matmul("mix_out", o, wt["w_out"])
    n2_par = [gt1, wt["norm2_gain"], sc2, sh2]
    x1, h2 = _rowwise("norm2", _resid_norm_mod_fn, [x, mixed], n2_par, [(D_MODEL, F32), (D_MODEL, BF16)], 256)
    f1, hid = _matmul("ffn1", h2, wt["ffn_w1"], out_dtypes=(F32, BF16),
                      epilogue=lambda acc: (acc, jnp.square(jnp.maximum(acc, 0.0))))
    ffn = _matmul("ffn2", hid, wt["ffn_w2"])

    ones = jnp.ones((t, 1), F32)
    loss_rows, dx1, dffn, g_gt2, g["final_gain"] = _rowwise_vjp(
        "loss", _loss_fn, [x1, ffn, target], [gt2, wt["final_gain"]], [[ones]], [0, 1], [0, 1], 256, emit=(0,),
        row_grad_dtypes=[F32, BF16])
    df1 = _matmul("ffn2_dx", dffn, wt["ffn_w2"], tb=True, extras=(f1,), out_dtypes=(BF16,),
                  epilogue=lambda acc, f: (acc * (2.0 * jnp.maximum(f, 0.0)),))
    g["ffn_w2"] = _matmul("ffn2_dw", hid, dffn, ta=True)
    dh2 = _matmul("ffn1_dx", df1, wt["ffn_w1"], tb=True)
    g["ffn_w1"] = _matmul("ffn1_dw", h2, df1, ta=True)
    dx_a, dmixed, g_gt1, g["norm2_gain"], g_sc2, g_sh2 = _rowwise_vjp(
        "norm2_bwd", _resid_norm_mod_fn, [x, mixed], n2_par, [[dx1], [dh2]], [0, 1], [0, 1, 2, 3], 256,
        row_grad_dtypes=[F32, BF16])
    do = _matmul("mix_out_dx", dmixed, wt["w_out"], tb=True)
    g["w_out"] = _matmul("mix_out_dw", o, dmixed, ta=True)
    do_s5, do_rk = do[:, :S5_WIDTH], do[:, S5_WIDTH:]

    dylin, du, g["s5_d"], g["s5_w_glu"], g["s5_b_glu"] = _rowwise_vjp(
        "s5_out_bwd", _s5_out_fn, [ylin, u], s5_par, [[do_s5]], [0, 1], [0, 1, 2], 256)
    prep_cts = []
    dylin_il, du_il = _interleave(dylin), _interleave(du)
    for d in range(2):
        res = _s5_backward(f"s5_bwd{d}", dylin_il, u_il, du_il, states[d], states[1] if d == 0 else None,
                           *b_blk[d], *c_blk, *lbar[d], reverse=(d == 1))
        du_il, db_re, db_im, dl_re, dl_im = res[:5]
        if d == 0:
            g["c_re"], g["c_im"] = _s5_out_unblock(res[5]), -_s5_out_unblock(res[6])
        prep_cts += [[dl_re.reshape(S5_CH, 1)], [dl_im.reshape(S5_CH, 1)], [_s5_in_unblock(db_re)],
                     [_s5_in_unblock(db_im)]]
    du = _deinterleave(du_il)
    pg = _rowwise_vjp("s5_prep_bwd", _s5_prep_fn, prep_rows, [], prep_cts, list(range(8)), [], 512)
    g["lam_re"], g["lam_im"], g["log_step"] = (pg[0], pg[3]), (pg[1], pg[4]), (pg[2], pg[5])
    g["b_re"], g["b_im"] = pg[6], pg[7]

    pb = _rowwise_vjp("rk_post_bwd", _rk_post_fn, post_rows, post_par, [[do_rk]], [0, 2, 3, 4, 5, 6], [0, 1, 2],
                      128, consts=[seg, seg_t])
    dy, dr_b, dv_b, dkd_b, dgate = pb[0], pb[1], pb[2], pb[3:5], pb[5]
    g["ln_gain"], g["ln_bias"], g["r_k"] = pb[6], pb[7], pb[8]
    dyh = _to_heads(dy)
    cg = []
    for d in range(2):
        plan = None
        if d == 0 and ffn_reduce is not None:
            plan = _exchange_plan(ffn_reduce(g.pop("ffn_w1"), g.pop("ffn_w2")), CHIP_PEERS, N_CHIPS, scatter=True)
        grads, arrived = _rk_core_bwd(f"rk_core{d}_bwd", *core_in[d], cks[d], dyh, reverse=(d == 1),
                                      chunk=min(chunk, t), hosted=plan)
        if arrived:
            g["ffn_arrived"] = arrived
        cg.append([_from_heads(q) for q in grads])
    pre_cts = [[cg[0][4], cg[1][4]], [cg[0][1]], [cg[1][1]], [cg[0][2], dkd_b[0]], [cg[1][2], dkd_b[1]],
               [cg[0][5]], [cg[1][5]], [dgate]]
    qb = _rowwise_vjp("rk_pre_bwd", _rk_pre_fn, [k, wdn, adn, gdn], pre_par, pre_cts, [0, 1, 2, 3],
                      list(range(11)), 128, consts=[seg, seg_t])
    dk, dwdn, dadn, dgdn = qb[:4]
    g["w0"], g["w_up"], g["a0"], g["a_up"] = (qb[4], qb[5]), (qb[6], qb[7]), (qb[8], qb[9]), (qb[10], qb[11])
    g["g_up"], g["k_k"], g["k_a"] = qb[12], qb[13], qb[14]
    dr, dv = _rowwise("rk_sum", lambda a, b, c, e, f, h: (a + b + c, e + f + h),
                      [cg[0][0], cg[1][0], dr_b, cg[0][3], cg[1][3], dv_b], [], [(RK_WIDTH, F32)] * 2, 256)
    dps = jnp.concatenate([dr, dk, dv, dwdn, dadn, dgdn], axis=1)
    dp, g["mu_prev"], g["mu_next"] = _token_shift_bwd(p, wt["mu_prev"], wt["mu_next"], dps)

    dproj = jnp.concatenate([du, dp], axis=1).astype(BF16)
    dh1 = _matmul("proj_dx", dproj, wt["w_in"], tb=True)
    g["w_in"] = _matmul("proj_dw", h1, dproj, ta=True)
    grad_x, g["norm1_gain"], g_sc1, g_sh1 = _rowwise_vjp(
        "norm1_bwd", _norm_mod_fn, [x], [wt["norm1_gain"], sc1, sh1], [[dh1]], [0], [0, 1, 2], 256,
        addends={0: dx_a})
    g["mod"] = [g_sh1, g_sc1, g_gt1, g_sh2, g_sc2, g_gt2]
    return loss_rows, grad_x, g


CHIP_PEERS = ((1, 0, 0), (0, 1, 0), (1, 1, 0))
ALL_PEERS = ((0, 0, 1), (0, 1, 0), (0, 1, 1), (1, 0, 0), (1, 0, 1), (1, 1, 0), (1, 1, 1))
CORE_PEER = ((0, 0, 1),)


def _exchange(name, arrays, peers, n_slots, scatter=False):
    return _run_plan(name, _exchange_plan(arrays, peers, n_slots, scatter))


def _run_plan(name, plan):
    na = len(plan.arrays)

    def body(*refs):
        plan.start(refs[:na], refs[na:2 * na], refs[2 * na:])
        plan.wait(refs[:na], refs[na:2 * na], refs[2 * na:])

    any_spec = pl.BlockSpec(memory_space=pl.ANY)
    return plan.finish(pl.pallas_call(
        body, name=name, in_specs=[any_spec] * na, out_specs=[any_spec] * na, out_shape=plan.out_shape,
        scratch_shapes=plan.sems,
    )(*plan.arrays))


def _exchange_plan(arrays, peers, n_slots, scatter=False):
    na, nm = len(arrays), len(peers)

    def ident(px, py, pc):
        return {8: 4 * px + 2 * py + pc, 4: 2 * px + py, 2: pc}[n_slots]

    def copies(in_refs, out_refs, sems):
        send_sems, recv_sems = sems
        x, y, c = lax.axis_index("x"), lax.axis_index("y"), lax.axis_index("c")
        me = ident(x, y, c)
        made = []
        for i in range(na):
            for j, (fx, fy, fc) in enumerate(peers):
                px, py, pc = (1 - x if fx else x), (1 - y if fy else y), (1 - c if fc else c)
                src = in_refs[i].at[ident(px, py, pc)] if scatter else in_refs[i]
                made.append(pltpu.make_async_remote_copy(
                    src_ref=src, dst_ref=out_refs[i].at[me],
                    send_sem=send_sems.at[i * nm + j], recv_sem=recv_sems.at[i * nm + j],
                    device_id=(px, py, pc), device_id_type=pl.DeviceIdType.MESH))
        return made

    def start(in_refs, out_refs, sems):
        for copy in copies(in_refs, out_refs, sems):
            copy.start()

    def wait(in_refs, out_refs, sems):
        for copy in copies(in_refs, out_refs, sems):
            copy.wait()

    def finish(outs):
        me = ident(lax.axis_index("x"), lax.axis_index("y"), lax.axis_index("c"))
        return [lax.dynamic_update_slice_in_dim(
            o, lax.dynamic_index_in_dim(a, me, 0, keepdims=True) if scatter else a[None], me, axis=0)
            for a, o in zip(arrays, outs)]

    out_shape = [jax.ShapeDtypeStruct(((n_slots,) + a.shape[1:]) if scatter else ((n_slots,) + a.shape), a.dtype)
                 for a in arrays]
    sems = [pltpu.SemaphoreType.DMA((na * nm,)), pltpu.SemaphoreType.DMA((na * nm,))]
    return _Plan(arrays, out_shape, sems, start, wait, finish)


def _gather_halves(name, arrays):
    return _run_plan(name, _gather_halves_plan(arrays))


def _gather_halves_plan(arrays):
    na = len(arrays)
    chips = ((1, 0), (0, 1), (1, 1))

    def over_ici(in_refs, out_refs, sems):
        ici_send, ici_recv = sems[:2]
        x, y, c = lax.axis_index("x"), lax.axis_index("y"), lax.axis_index("c")
        made = []
        for i in range(na):
            half = arrays[i].shape[0] // 2
            mine = pl.ds(pl.multiple_of(c * half, 8), half)
            for j, (fx, fy) in enumerate(chips):
                px, py = (1 - x if fx else x), (1 - y if fy else y)
                k = len(chips) * i + j
                made.append([pltpu.make_async_remote_copy(
                    src_ref=in_refs[i].at[mine], dst_ref=out_refs[i].at[chip, mine],
                    send_sem=ici_send.at[k], recv_sem=ici_recv.at[k],
                    device_id=(px, py, c), device_id_type=pl.DeviceIdType.MESH)
                    for chip in (2 * x + y, 2 * px + py)])
        return made

    def start(in_refs, out_refs, sems):
        for outgoing, _ in over_ici(in_refs, out_refs, sems):
            outgoing.start()

    def wait(in_refs, out_refs, sems):
        d2d_send, d2d_recv = sems[2:]
        x, y, c = lax.axis_index("x"), lax.axis_index("y"), lax.axis_index("c")
        pending = []
        ici = over_ici(in_refs, out_refs, sems)
        for i in range(na):
            half = arrays[i].shape[0] // 2
            mine = pl.ds(pl.multiple_of(c * half, 8), half)
            theirs = pl.ds(pl.multiple_of((1 - c) * half, 8), half)
            for j, (fx, fy) in enumerate(chips):
                px, py = (1 - x if fx else x), (1 - y if fy else y)
                k = len(chips) * i + j
                outgoing, landing = ici[k]
                landing.wait_recv()
                landed = out_refs[i].at[2 * px + py, mine]
                passed = pltpu.make_async_remote_copy(
                    src_ref=landed, dst_ref=landed, send_sem=d2d_send.at[k], recv_sem=d2d_recv.at[k],
                    device_id=(x, y, 1 - c), device_id_type=pl.DeviceIdType.MESH)
                passed.start()
                from_sibling = out_refs[i].at[2 * px + py, theirs]
                pending += [outgoing.wait_send, passed.wait_send, pltpu.make_async_remote_copy(
                    src_ref=from_sibling, dst_ref=from_sibling, send_sem=d2d_send.at[k], recv_sem=d2d_recv.at[k],
                    device_id=(x, y, 1 - c), device_id_type=pl.DeviceIdType.MESH).wait_recv]
        for done in pending:
            done()

    def finish(outs):
        me = 2 * lax.axis_index("x") + lax.axis_index("y")
        return [lax.dynamic_update_slice_in_dim(o, a[None], me, axis=0) for a, o in zip(arrays, outs)]

    out_shape = [jax.ShapeDtypeStruct((N_CHIPS,) + a.shape, a.dtype) for a in arrays]
    return _Plan(arrays, out_shape, [pltpu.SemaphoreType.DMA((na * len(chips),))] * 4, start, wait, finish)


def _send_other_half(name, arrays):
    na = len(arrays)

    def body(*refs):
        in_refs, out_refs, send_sems, recv_sems = refs[:na], refs[na:2 * na], refs[-2], refs[-1]
        x, y, c = lax.axis_index("x"), lax.axis_index("y"), lax.axis_index("c")
        copies = []
        for i in range(na):
            half = arrays[i].shape[1] // 2
            theirs = pl.ds(pl.multiple_of((1 - c) * half, 8), half)
            copy = pltpu.make_async_remote_copy(
                src_ref=in_refs[i].at[:, theirs], dst_ref=out_refs[i], send_sem=send_sems.at[i],
                recv_sem=recv_sems.at[i], device_id=(x, y, 1 - c), device_id_type=pl.DeviceIdType.MESH)
            copy.start()
            copies.append(copy)
        for copy in copies:
            copy.wait()

    any_spec = pl.BlockSpec(memory_space=pl.ANY)
    return pl.pallas_call(
        body, name=name, in_specs=[any_spec] * na, out_specs=[any_spec] * na,
        out_shape=[jax.ShapeDtypeStruct((a.shape[0], a.shape[1] // 2, a.shape[2]), a.dtype) for a in arrays],
        scratch_shapes=[pltpu.SemaphoreType.DMA((na,)), pltpu.SemaphoreType.DMA((na,))],
    )(*arrays)


def _adam_math(w, g, m, v):
    m = ADAM_B1 * m + (1.0 - ADAM_B1) * g
    v = ADAM_B2 * v + (1.0 - ADAM_B2) * jnp.square(g)
    m_hat = m / (1.0 - ADAM_B1 ** ADAM_STEP)
    v_hat = v / (1.0 - ADAM_B2 ** ADAM_STEP)
    delta = -ADAM_LR * (m_hat / (jnp.sqrt(v_hat) + ADAM_EPS) + ADAM_WD * w)
    return delta, m, v


def _row_tile(r):
    return _tile(r, (256, 128, 64, 32, 16, 8))


def _sum_parts(name, parts):
    n, r, c = parts.shape
    tr = _row_tile(r)

    def body(p_ref, o_ref):
        tot = p_ref[0].astype(F32)
        for i in range(1, n):
            tot = tot + p_ref[i].astype(F32)
        o_ref[...] = tot

    return pl.pallas_call(
        body, name=name, grid=(r // tr,), in_specs=[pl.BlockSpec((n, tr, c), lambda i: (0, i, 0))],
        out_specs=pl.BlockSpec((tr, c), lambda i: (i, 0)), out_shape=jax.ShapeDtypeStruct((r, c), F32),
        compiler_params=_params(("parallel",)),
    )(parts)


def _adamw(name, w, parts, m, v):
    n, r, c = parts.shape
    tr = _row_tile(r)

    def body(w_ref, p_ref, m_ref, v_ref, g_ref, d_ref, nm_ref, nv_ref):
        g = p_ref[0]
        for i in range(1, n):
            g = g + p_ref[i]
        delta, nm, nv = _adam_math(w_ref[...], g, m_ref[...], v_ref[...])
        g_ref[...], d_ref[...], nm_ref[...], nv_ref[...] = g, delta, nm, nv

    blk = pl.BlockSpec((tr, c), lambda i: (i, 0))
    return pl.pallas_call(
        body, name=name, grid=(r // tr,),
        in_specs=[blk, pl.BlockSpec((n, tr, c), lambda i: (0, i, 0)), blk, blk], out_specs=[blk] * 4,
        out_shape=[jax.ShapeDtypeStruct((r, c), F32)] * 4, compiler_params=_params(("parallel",)),
    )(w, parts, m, v)


def _ada_w_update(act_t, dmod, w, m, v):
    r, c = w.shape
    nb = act_t.shape[1]
    tr, tc = 256, 1024

    def body(a_ref, d_ref, w_ref, m_ref, v_ref, g_ref, dl_ref, nm_ref, nv_ref):
        a, dm = a_ref[...], d_ref[...]
        g = a[:, 0:1] * dm[0:1, :]
        for b in range(1, nb):
            g = g + a[:, b:b + 1] * dm[b:b + 1, :]
        delta, nm, nv = _adam_math(w_ref[...], g, m_ref[...], v_ref[...])
        g_ref[...], dl_ref[...], nm_ref[...], nv_ref[...] = g, delta, nm, nv

    blk = pl.BlockSpec((tr, tc), lambda i, j: (i, j))
    return pl.pallas_call(
        body, name="ada_w_update", grid=(r // tr, c // tc),
        in_specs=[pl.BlockSpec((tr, nb), lambda i, j: (i, 0)), pl.BlockSpec((nb, tc), lambda i, j: (0, j)),
                  blk, blk, blk],
        out_specs=[blk] * 4, out_shape=[jax.ShapeDtypeStruct((r, c), F32)] * 4,
        compiler_params=_params(("parallel", "parallel")),
    )(act_t, dmod, w, m, v)


WEIGHTS = ['ada_w', 'ada_b', 'norm1_gain', 'norm2_gain', 'final_gain', 'w_in', 'w_out', 's5_lambda_re',
           's5_lambda_im', 's5_log_step', 's5_b_re', 's5_b_im', 's5_c_re', 's5_c_im', 's5_d', 's5_w_glu',
           's5_b_glu', 'rk_shift_prev', 'rk_shift_next', 'rk_w0', 'rk_w_up', 'rk_a0', 'rk_a_up', 'rk_g_up',
           'rk_k_k', 'rk_k_a', 'rk_r_k', 'rk_ln_gain', 'rk_ln_bias', 'ffn_w1', 'ffn_w2']
BIG_SHARDED = ['w_in', 'w_out', 's5_w_glu', 'ffn_w1', 'ffn_w2']
FFN_SHARDED = ['ffn_w1', 'ffn_w2']
RK_SHARDED = ['rk_w0', 'rk_a0', 'rk_w_up', 'rk_a_up', 'rk_g_up']
REPLICATED = ['ada_b', 'norm1_gain', 'norm2_gain', 'final_gain', 's5_lambda_re', 's5_lambda_im', 's5_log_step',
              's5_b_re', 's5_b_im', 's5_c_re', 's5_c_im', 's5_d', 's5_b_glu', 'rk_shift_prev', 'rk_shift_next',
              'rk_k_k', 'rk_k_a', 'rk_r_k', 'rk_ln_gain', 'rk_ln_bias']
PACK_COLS = 1024
N_CHIPS = 4
RK_ROWS = 420
RK_ROWS_PAD = 432


def _pack_rows(arrays, cols):
    return jnp.concatenate([a.reshape(-1, cols) for a in arrays], axis=0)


def _pack_flat(arrays):
    flat = jnp.concatenate([a.reshape(-1) for a in arrays])
    rows = -(-flat.shape[0] // PACK_COLS)
    return jnp.pad(flat, (0, rows * PACK_COLS - flat.shape[0])).reshape(rows, PACK_COLS)


def _unpack_flat(packed, like):
    flat, out, pos = packed.reshape(-1), [], 0
    for a in like:
        out.append(flat[pos:pos + a.size].reshape(a.shape))
        pos += a.size
    return out


def _cols_to_chips(full, n_rows):
    return jnp.transpose(full.reshape(n_rows, N_CHIPS, -1), (1, 0, 2))


def _chips_to_cols(parts):
    return jnp.transpose(parts, (1, 0, 2)).reshape(parts.shape[1], -1)


def kernel(x, c, ada_w, ada_b, norm1_gain, norm2_gain, final_gain, w_in, w_out, s5_lambda_re, s5_lambda_im, s5_log_step, s5_b_re, s5_b_im, s5_c_re, s5_c_im, s5_d, s5_w_glu, s5_b_glu, rk_shift_prev, rk_shift_next, rk_w0, rk_w_up, rk_a0, rk_a_up, rk_g_up, rk_k_k, rk_k_a, rk_r_k, rk_ln_gain, rk_ln_bias, ffn_w1, ffn_w2, loss_target, m_ada_w, m_ada_b, m_norm1_gain, m_norm2_gain, m_final_gain, m_w_in, m_w_out, m_s5_lambda_re, m_s5_lambda_im, m_s5_log_step, m_s5_b_re, m_s5_b_im, m_s5_c_re, m_s5_c_im, m_s5_d, m_s5_w_glu, m_s5_b_glu, m_rk_shift_prev, m_rk_shift_next, m_rk_w0, m_rk_w_up, m_rk_a0, m_rk_a_up, m_rk_g_up, m_rk_k_k, m_rk_k_a, m_rk_r_k, m_rk_ln_gain, m_rk_ln_bias, m_ffn_w1, m_ffn_w2, v_ada_w, v_ada_b, v_norm1_gain, v_norm2_gain, v_final_gain, v_w_in, v_w_out, v_s5_lambda_re, v_s5_lambda_im, v_s5_log_step, v_s5_b_re, v_s5_b_im, v_s5_c_re, v_s5_c_im, v_s5_d, v_s5_w_glu, v_s5_b_glu, v_rk_shift_prev, v_rk_shift_next, v_rk_w0, v_rk_w_up, v_rk_a0, v_rk_a_up, v_rk_g_up, v_rk_k_k, v_rk_k_a, v_rk_r_k, v_rk_ln_gain, v_rk_ln_bias, v_ffn_w1, v_ffn_w2):
    given = dict(locals())
    w = {n: given[n] for n in WEIGHTS}
    m = {n: given["m_" + n] for n in WEIGHTS}
    v = {n: given["v_" + n] for n in WEIGHTS}
    mx, my, mc = lax.axis_index("x"), lax.axis_index("y"), lax.axis_index("c")
    chip = 2 * mx + my
    dev = 2 * chip + mc
    xt, target = x[0], loss_target[0]

    def rk_rows(d):
        rows = _pack_rows([d[n] for n in RK_SHARDED], 256)
        return jnp.pad(rows, ((0, RK_ROWS_PAD - rows.shape[0]), (0, 0)))

    (c_all,) = _exchange("gather_c", [c], ALL_PEERS, 8)
    early = [n for n in BIG_SHARDED if n not in FFN_SHARDED]
    gathered = _gather_halves("gather_w", [w[n][0].astype(BF16) for n in early] + [rk_rows(w)])
    full = dict(zip(early, gathered[:len(early)]))
    rk_full = gathered[len(early)]

    (act,) = _rowwise("ada_act", lambda q: (q * _sigmoid(q),), [c_all.reshape(8, D_MODEL)], [], [(D_MODEL, F32)], 8)
    n_mod_cols = N_MOD * D_MODEL // N_CHIPS
    bias = jnp.broadcast_to(lax.dynamic_slice(ada_b, (0, chip * n_mod_cols), (1, n_mod_cols)), (8, n_mod_cols))
    mod_shard = _matmul("ada_fwd", act, ada_w[0], epilogue=_add_epilogue, extras=(bias,))
    (mod_parts,) = _exchange("gather_mod", [mod_shard], CHIP_PEERS, N_CHIPS)
    mod_all = _chips_to_cols(mod_parts)
    mod_mine = lax.dynamic_slice(mod_all, (dev, 0), (1, N_MOD * D_MODEL))
    mod = [mod_mine[:, i * D_MODEL:(i + 1) * D_MODEL] for i in range(N_MOD)]

    def rk_piece(lo, hi, lead):
        return _chips_to_cols(rk_full[:, lo:hi]).reshape(lead + (RK_WIDTH,))

    zeros = jnp.zeros((LORA, RK_WIDTH), F32)
    w_up, a_up = rk_piece(4, 132, (2, LORA)), rk_piece(132, 260, (2, LORA))
    wt = {
        "norm1_gain": norm1_gain, "norm2_gain": norm2_gain, "final_gain": final_gain.reshape(1, D_MODEL),
        "w_in": jnp.pad(_chips_to_cols(full["w_in"]), ((0, 0), (0, PROJ_PAD - PROJ))),
        "w_out": full["w_out"].reshape(D_MODEL, D_MODEL),
        "s5_w_glu": full["s5_w_glu"].reshape(S5_WIDTH, S5_WIDTH),
        "mu_prev": jnp.pad(rk_shift_prev, ((0, 0), (0, RK_PAD - RK_IN))),
        "mu_next": jnp.pad(rk_shift_next, ((0, 0), (0, RK_PAD - RK_IN))),
        "lam_re": [s5_lambda_re[0, d].reshape(S5_CH, 1) for d in range(2)],
        "lam_im": [s5_lambda_im[0, d].reshape(S5_CH, 1) for d in range(2)],
        "log_step": [jnp.repeat(s5_log_step[0, d], S5_STATE).reshape(S5_CH, 1) for d in range(2)],
        "b_re": s5_b_re.reshape(S5_CH, S5_GROUP), "b_im": s5_b_im.reshape(S5_CH, S5_GROUP),
        "c_re": s5_c_re[0], "c_im": s5_c_im[0],
        "s5_d": s5_d, "s5_b_glu": s5_b_glu,
        "w0": list(rk_piece(0, 2, (2,))[:, None, :]), "a0": list(rk_piece(2, 4, (2,))[:, None, :]),
        "w_up": [jnp.concatenate([w_up[0], zeros]), jnp.concatenate([zeros, w_up[1]])],
        "a_up": [jnp.concatenate([a_up[0], zeros]), jnp.concatenate([zeros, a_up[1]])],
        "g_up": jnp.pad(rk_piece(260, 420, (GATE_LORA,)), ((0, GATE_PAD - GATE_LORA), (0, 0))),
        "k_k": rk_k_k, "k_a": rk_k_a, "r_k": rk_r_k.reshape(1, RK_WIDTH),
        "ln_gain": rk_ln_gain, "ln_bias": rk_ln_bias,
    }

    def chip_sums(tag, names, pieces):
        sums = []
        for n, piece, other in zip(names, pieces, _send_other_half("swap_halves_" + tag, pieces)):
            half, cols = other.shape[1], other.shape[2]
            own = lax.dynamic_slice_in_dim(piece, mc * half, half, axis=1)
            wire = F32 if n == "rk" else BF16
            (both,) = _rowwise("pair_" + n, lambda a, b: (a + b,), [own.reshape(-1, cols), other.reshape(-1, cols)],
                               [], [(cols, wire)], _row_tile(N_CHIPS * half))
            sums.append(both.reshape(N_CHIPS, half, cols))
        return sums

    def ffn_reduce(g_w1, g_w2):
        return chip_sums("ffn", FFN_SHARDED, [_cols_to_chips(g_w1, D_MODEL), g_w2.reshape(N_CHIPS, -1, D_MODEL)])

    ffn_shards = [w[n][0].astype(BF16) for n in FFN_SHARDED]
    loss_rows, grad_x, g = _local_step(xt, target, mod, wt, ffn_shards=ffn_shards, ffn_reduce=ffn_reduce)
    loss = lax.psum(jnp.sum(loss_rows), ("x", "y", "c"))

    big_grads = {
        "w_in": _cols_to_chips(g["w_in"][:, :PROJ], D_MODEL),
        "w_out": g["w_out"].reshape(N_CHIPS, -1, D_MODEL),
        "s5_w_glu": g["s5_w_glu"].reshape(N_CHIPS, -1, S5_WIDTH),
    }
    rk_grads = jnp.concatenate([
        _cols_to_chips(jnp.concatenate(g["w0"]), 2), _cols_to_chips(jnp.concatenate(g["a0"]), 2),
        _cols_to_chips(jnp.concatenate([g["w_up"][0][:LORA], g["w_up"][1][LORA:]]), 2 * LORA),
        _cols_to_chips(jnp.concatenate([g["a_up"][0][:LORA], g["a_up"][1][LORA:]]), 2 * LORA),
        _cols_to_chips(g["g_up"][:GATE_LORA], GATE_LORA),
        jnp.zeros((N_CHIPS, RK_ROWS_PAD - RK_ROWS, 256), F32)], axis=1)
    late = [n for n in BIG_SHARDED if n not in FFN_SHARDED]
    arrived = _exchange("scatter_grads", chip_sums("late", late + ["rk"], [big_grads[n] for n in late] + [rk_grads]),
                        CHIP_PEERS, N_CHIPS, scatter=True)
    names = late + ["rk"] + FFN_SHARDED
    half_sums = [_sum_parts("sum_" + n, a) for n, a in zip(names, arrived + g["ffn_arrived"])]
    pairs = dict(zip(names, [p.reshape(1, 2 * p.shape[1], p.shape[2])
                             for p in _exchange("swap_sums", half_sums, CORE_PEER, 2)]))

    out = {}
    for n in BIG_SHARDED:
        res = _adamw("adamw_" + n, w[n][0], pairs[n], m[n][0], v[n][0])
        out[n] = [r[None] for r in res]
    rk_res = _adamw("adamw_rk", rk_rows(w), pairs["rk"], rk_rows(m), rk_rows(v))
    for q in range(4):
        pieces, pos = [], 0
        for n in RK_SHARDED:
            rows = w[n].size // 256
            pieces.append(rk_res[q][pos:pos + rows].reshape(w[n].shape))
            pos += rows
        for n, piece in zip(RK_SHARDED, pieces):
            out.setdefault(n, []).append(piece)

    local_small = {
        "ada_b": jnp.concatenate(g["mod"], axis=1),
        "norm1_gain": g["norm1_gain"], "norm2_gain": g["norm2_gain"], "final_gain": g["final_gain"],
        "s5_lambda_re": jnp.concatenate(g["lam_re"]), "s5_lambda_im": jnp.concatenate(g["lam_im"]),
        "s5_log_step": jnp.concatenate([q.reshape(S5_GROUPS, S5_STATE).sum(axis=1) for q in g["log_step"]]),
        "s5_b_re": g["b_re"], "s5_b_im": g["b_im"], "s5_c_re": g["c_re"], "s5_c_im": g["c_im"],
        "s5_d": g["s5_d"], "s5_b_glu": g["s5_b_glu"],
        "rk_shift_prev": g["mu_prev"][:, :RK_IN], "rk_shift_next": g["mu_next"][:, :RK_IN],
        "rk_k_k": g["k_k"], "rk_k_a": g["k_a"], "rk_r_k": g["r_k"],
        "rk_ln_gain": g["ln_gain"], "rk_ln_bias": g["ln_bias"],
    }
    (small_all,) = _exchange("gather_small", [_pack_flat([local_small[n] for n in REPLICATED])], ALL_PEERS, 8)
    small_res = _adamw("adamw_small", _pack_flat([w[n] for n in REPLICATED]), small_all,
                       _pack_flat([m[n] for n in REPLICATED]), _pack_flat([v[n] for n in REPLICATED]))
    for q in range(4):
        for n, piece in zip(REPLICATED, _unpack_flat(small_res[q], [w[n] for n in REPLICATED])):
            out.setdefault(n, []).append(piece)

    mod_rows = N_MOD * D_MODEL // PACK_COLS
    dmod_all = small_all[:, :mod_rows].reshape(8, N_MOD * D_MODEL)
    dmod = lax.dynamic_slice(dmod_all, (0, chip * n_mod_cols), (8, n_mod_cols))
    res = _ada_w_update(act.T, dmod, ada_w[0], m_ada_w[0], v_ada_w[0])
    out["ada_w"] = [r[None] for r in res]

    return (loss, grad_x[None], *[out[n][0] for n in WEIGHTS], *[out[n][1] for n in WEIGHTS],
            *[out[n][2] for n in WEIGHTS], *[out[n][3] for n in WEIGHTS])
```

```python
import functools
import math

import jax
import jax.numpy as jnp
from jax import lax
from jax.experimental import pallas as pl
from jax.experimental.pallas import tpu as pltpu

F32 = jnp.float32
BF16 = jnp.bfloat16

D_MODEL = 2048
S5_WIDTH = 1024
S5_GROUP = 16
S5_GROUPS = 64
S5_STATE = 64
S5_CH = S5_GROUPS * S5_STATE
S5_BLK = 256
RK_WIDTH = 1024
RK_HEAD = 64
RK_HEADS = 16
LORA = 64
GATE_LORA = 160
GATE_PAD = 256
RK_IN = 3488
RK_PAD = 3584
PROJ = 4512
PROJ_PAD = 4608
FFN = 8192
N_MOD = 6
NORM_EPS = 1e-6
GN_EPS = 64e-5
L2_EPS = 1e-12
RK_CHUNK = 64
RK_PASSES = {"solve": 3, "kt": 3, "s0": 1, "akk_v": 1, "ark_v": 1, "arb_u": 1, "state": 3}
LW_SCALE = math.exp(-0.5)
ADAM_LR, ADAM_B1, ADAM_B2, ADAM_EPS, ADAM_WD, ADAM_STEP = 0.001, 0.9, 0.999, 1e-08, 0.01, 10
VMEM_LIMIT = 56 * 1024 * 1024
HI = lax.Precision.HIGHEST


def _params(sem=None):
    return pltpu.CompilerParams(dimension_semantics=sem, vmem_limit_bytes=VMEM_LIMIT)


def _full(a):
    nd = a.ndim
    return pl.BlockSpec(a.shape, lambda *_: (0,) * nd)


@jax.custom_vjp
def _bdot(a, b):
    return jnp.dot(a.astype(BF16), b.astype(BF16), preferred_element_type=F32)


def _bdot_fwd(a, b):
    return _bdot(a, b), (a, b)


def _bdot_bwd(res, g):
    a, b = res
    gb = g.astype(BF16)
    da = lax.dot_general(gb, b.astype(BF16), (((1,), (1,)), ((), ())), preferred_element_type=F32)
    db = lax.dot_general(a.astype(BF16), gb, (((0,), (0,)), ((), ())), preferred_element_type=F32)
    return da, db


_bdot.defvjp(_bdot_fwd, _bdot_bwd)


def _fdot(a, b):
    return jnp.dot(a, b, precision=HI, preferred_element_type=F32)


def _sigmoid(z):
    return 1.0 / (1.0 + jnp.exp(-z))


def _gelu(y):
    return 0.5 * y * (1.0 + jnp.tanh(0.7978845608028654 * (y + 0.044715 * (y * y * y))))


def _rms(x):
    return x * lax.rsqrt(jnp.mean(x * x, axis=-1, keepdims=True) + NORM_EPS)


def _tile(n, prefs):
    for t in prefs:
        if n % t == 0:
            return t
    return n


def _matmul(name, a, b, ta=False, tb=False, epilogue=None, extras=(), out_dtypes=(F32,)):
    m = a.shape[1] if ta else a.shape[0]
    k = a.shape[0] if ta else a.shape[1]
    n = b.shape[0] if tb else b.shape[1]
    assert k == (b.shape[1] if tb else b.shape[0]), (a.shape, b.shape, ta, tb)
    tm = _tile(m, (1024, 512, 256, 128))
    tn = _tile(n, (1024, 768, 512, 256, 128))
    tk = _tile(k, (2048, 1024, 512, 256, 128))
    nk = k // tk
    n_ex, n_out = len(extras), len(out_dtypes)
    dims = (((0 if ta else 1,), (1 if tb else 0,)), ((), ()))

    def body(a_ref, b_ref, *rest):
        ex_refs, out_refs, acc = rest[:n_ex], rest[n_ex:n_ex + n_out], rest[-1]
        kk = pl.program_id(2)

        @pl.when(kk == 0)
        def _():
            acc[...] = jnp.zeros_like(acc)

        acc[...] += lax.dot_general(a_ref[...].astype(BF16), b_ref[...].astype(BF16), dims,
                                    preferred_element_type=F32)

        @pl.when(kk == nk - 1)
        def _():
            res = acc[...]
            outs = epilogue(res, *[e[...] for e in ex_refs]) if epilogue is not None else (res,)
            for o_ref, val in zip(out_refs, outs):
                o_ref[...] = val.astype(o_ref.dtype)

    a_spec = pl.BlockSpec((tk, tm), lambda i, j, q: (q, i)) if ta else pl.BlockSpec((tm, tk), lambda i, j, q: (i, q))
    b_spec = pl.BlockSpec((tn, tk), lambda i, j, q: (j, q)) if tb else pl.BlockSpec((tk, tn), lambda i, j, q: (q, j))
    mn_spec = pl.BlockSpec((tm, tn), lambda i, j, q: (i, j))
    outs = pl.pallas_call(
        body, name=name, grid=(m // tm, n // tn, nk),
        in_specs=[a_spec, b_spec] + [mn_spec] * n_ex,
        out_specs=[mn_spec] * n_out,
        out_shape=[jax.ShapeDtypeStruct((m, n), dt) for dt in out_dtypes],
        scratch_shapes=[pltpu.VMEM((tm, tn), F32)],
        compiler_params=_params(("parallel", "parallel", "arbitrary")),
    )(a, b, *extras)
    return outs[0] if n_out == 1 else outs


def _row_spec(a, tm):
    return pl.BlockSpec((tm, a.shape[1]), lambda i: (i, 0))


def _rowwise(name, fn, rows, params, outs, tm):
    t = rows[0].shape[0]
    tm = min(tm, t)
    n_r, n_p = len(rows), len(params)

    def body(*refs):
        vals = [r[...] for r in refs[:n_r + n_p]]
        res = fn(*vals)
        for o_ref, val in zip(refs[n_r + n_p:], res):
            o_ref[...] = val.astype(o_ref.dtype)

    res = pl.pallas_call(
        body, name=name, grid=(t // tm,),
        in_specs=[_row_spec(r, tm) for r in rows] + [_full(p) for p in params],
        out_specs=[pl.BlockSpec((tm, n), lambda i: (i, 0)) for n, _ in outs],
        out_shape=[jax.ShapeDtypeStruct((t, n), dt) for n, dt in outs],
        compiler_params=_params(("parallel",)),
    )(*rows, *params)
    return res


def _rowwise_vjp(name, fn, rows, params, cts, row_grads, param_grads, tm, consts=(), addends=None,
                 emit=(), row_grad_dtypes=None):
    t = rows[0].shape[0]
    tm = min(tm, t)
    addends = addends or {}
    n_r, n_p, n_c = len(rows), len(params), len(consts)
    ct_flat = [c for group in cts for c in group]
    add_list = [addends[q] for q in sorted(addends)]
    n_ct, n_add = len(ct_flat), len(add_list)
    row_grad_dtypes = row_grad_dtypes or [F32] * len(row_grads)

    def body(*refs):
        pos = 0
        row_v = [r[...].astype(F32) for r in refs[pos:pos + n_r]]; pos += n_r
        par_v = [r[...].astype(F32) for r in refs[pos:pos + n_p]]; pos += n_p
        con_v = [r[...] for r in refs[pos:pos + n_c]]; pos += n_c
        ct_v = [r[...].astype(F32) for r in refs[pos:pos + n_ct]]; pos += n_ct
        add_v = [r[...] for r in refs[pos:pos + n_add]]; pos += n_add
        emit_refs = refs[pos:pos + len(emit)]; pos += len(emit)
        rg_refs = refs[pos:pos + len(row_grads)]; pos += len(row_grads)
        pg_refs = refs[pos:pos + len(param_grads)]

        def diff_fn(*dargs):
            rv, pv = list(row_v), list(par_v)
            for q, i in enumerate(row_grads):
                rv[i] = dargs[q]
            for q, j in enumerate(param_grads):
                pv[j] = dargs[len(row_grads) + q]
            return fn(*rv, *pv, *con_v)

        prim = [row_v[i] for i in row_grads] + [par_v[j] for j in param_grads]
        res, vjp = jax.vjp(diff_fn, *prim)
        ct_vals, q = [], 0
        for o, group in zip(res, cts):
            tot = jnp.zeros_like(o)
            for _ in group:
                tot = tot + ct_v[q]
                q += 1
            ct_vals.append(tot)
        grads = vjp(tuple(ct_vals))
        for e_ref, idx in zip(emit_refs, emit):
            e_ref[...] = res[idx].astype(e_ref.dtype)
        add_pos = {p: q for q, p in enumerate(sorted(addends))}
        for q, g_ref in enumerate(rg_refs):
            g = grads[q]
            if q in add_pos:
                g = g + add_v[add_pos[q]]
            g_ref[...] = g.astype(g_ref.dtype)

        @pl.when(pl.program_id(0) == 0)
        def _():
            for g_ref in pg_refs:
                g_ref[...] = jnp.zeros_like(g_ref)

        for q, g_ref in enumerate(pg_refs):
            g_ref[...] += grads[len(row_grads) + q]

    emit_shapes = []
    if emit:
        probe = jax.eval_shape(lambda *a: fn(*a), *[jax.ShapeDtypeStruct((tm, r.shape[1]), F32) for r in rows],
                               *[jax.ShapeDtypeStruct(p.shape, p.dtype) for p in params],
                               *[jax.ShapeDtypeStruct(c.shape, c.dtype) for c in consts])
        emit_shapes = [probe[idx].shape[1] for idx in emit]
    out_specs = ([pl.BlockSpec((tm, n), lambda i: (i, 0)) for n in emit_shapes]
                 + [_row_spec(rows[i], tm) for i in row_grads]
                 + [_full(params[j]) for j in param_grads])
    out_shape = ([jax.ShapeDtypeStruct((t, n), F32) for n in emit_shapes]
                 + [jax.ShapeDtypeStruct(rows[i].shape, dt) for i, dt in zip(row_grads, row_grad_dtypes)]
                 + [jax.ShapeDtypeStruct(params[j].shape, F32) for j in param_grads])
    return pl.pallas_call(
        body, name=name, grid=(t // tm,),
        in_specs=([_row_spec(r, tm) for r in rows] + [_full(p) for p in params] + [_full(c) for c in consts]
                  + [_row_spec(c, tm) for c in ct_flat] + [_row_spec(a, tm) for a in add_list]),
        out_specs=out_specs, out_shape=out_shape,
        compiler_params=_params(("arbitrary",)),
    )(*rows, *params, *consts, *ct_flat, *add_list)


def _norm_mod_fn(x, gain, scale, shift):
    return (_rms(x) * gain * (1.0 + scale) + shift,)


def _resid_norm_mod_fn(x, mixed, gate, gain, scale, shift):
    x1 = x + gate * mixed
    return x1, _rms(x1) * gain * (1.0 + scale) + shift


def _loss_fn(x1, ffn, target, gate, gain):
    y = _rms(x1 + gate * ffn) * gain
    err = y - target
    return (0.5 * jnp.mean(err * err, axis=-1, keepdims=True),)


def _s5_out_fn(ylin, u, d_skip, w_glu, b_glu):
    z = _gelu(ylin + d_skip * u)
    return (z * _sigmoid(_bdot(z, w_glu) + b_glu),)


def _rk_pre_fn(k, wdn, adn, gdn, w0_0, w0_1, wup_0, wup_1, a0_0, a0_1, aup_0, aup_1, g_up, k_k, k_a, seg, seg_t):
    kkr = k * k_k
    inv = 1.0 / jnp.sqrt(jnp.maximum(_fdot(kkr * kkr, seg), L2_EPS * L2_EPS))
    kk = kkr * _fdot(inv, seg_t)
    tw = jnp.tanh(wdn)
    lws, kds, acts = [], [], []
    for w0, wup, a0, aup in ((w0_0, wup_0, a0_0, aup_0), (w0_1, wup_1, a0_1, aup_1)):
        lws.append(-LW_SCALE * _sigmoid(w0 + _bdot(tw, wup)))
        act = _sigmoid(a0 + _bdot(adn, aup))
        acts.append(act)
        kds.append(k * (1.0 + (act - 1.0) * k_a))
    gate = _bdot(_sigmoid(gdn), g_up)
    return (kk, lws[0], lws[1], kds[0], kds[1], acts[0], acts[1], gate)


def _rk_post_fn(y0, y1, r, v, kd0, kd1, gate, ln_gain, ln_bias, r_k, seg, seg_t):
    y = y0 + y1
    mu = _fdot(_fdot(y, seg) * (1.0 / RK_HEAD), seg_t)
    yc = y - mu
    var = _fdot(yc * yc, seg) * (1.0 / RK_HEAD)
    yn = yc * _fdot(lax.rsqrt(var + GN_EPS), seg_t) * ln_gain + ln_bias
    bonus = _fdot(_fdot(r * (kd0 + kd1) * r_k, seg), seg_t)
    return ((yn + bonus * v) * gate,)


def _s5_prep_fn(lr0, li0, ls0, lr1, li1, ls1, b_re, b_im):
    outs = []
    for lam_re, lam_im, ls in ((lr0, li0, ls0), (lr1, li1, ls1)):
        step = jnp.exp(ls)
        mag = jnp.exp(lam_re * step)
        lbar_re = mag * jnp.cos(lam_im * step)
        lbar_im = mag * jnp.sin(lam_im * step)
        den = lam_re * lam_re + lam_im * lam_im
        nr = lbar_re - 1.0
        coef_re = (nr * lam_re + lbar_im * lam_im) / den
        coef_im = (lbar_im * lam_re - nr * lam_im) / den
        outs += [lbar_re, lbar_im, coef_re * b_re - coef_im * b_im, coef_re * b_im + coef_im * b_re]
    return tuple(outs)


def _shift_rows(x, down):
    t = x.shape[0]
    rows = lax.broadcasted_iota(jnp.int32, x.shape, 0)
    if down:
        return jnp.where(rows >= 1, pltpu.roll(x, 1, 0), 0.0)
    return jnp.where(rows < t - 1, pltpu.roll(x, t - 1, 0), 0.0)


def _token_shift(p, mu_prev, mu_next):
    t, n = p.shape

    def body(p_ref, mp_ref, mn_ref, o_ref):
        x = p_ref[...]
        o_ref[...] = x + mp_ref[...] * (_shift_rows(x, True) - x) + mn_ref[...] * (_shift_rows(x, False) - x)

    col = pl.BlockSpec((t, 128), lambda j: (0, j))
    par = pl.BlockSpec((1, 128), lambda j: (0, j))
    return pl.pallas_call(
        body, name="token_shift", grid=(n // 128,), in_specs=[col, par, par], out_specs=col,
        out_shape=jax.ShapeDtypeStruct((t, n), F32), compiler_params=_params(("parallel",)),
    )(p, mu_prev, mu_next)


def _token_shift_bwd(p, mu_prev, mu_next, dps):
    t, n = p.shape

    def body(p_ref, mp_ref, mn_ref, d_ref, dp_ref, dmp_ref, dmn_ref):
        x, d, mp, mn = p_ref[...], d_ref[...], mp_ref[...], mn_ref[...]
        dp_ref[...] = d * (1.0 - mp - mn) + _shift_rows(d * mp, False) + _shift_rows(d * mn, True)
        dmp_ref[...] = jnp.sum(d * (_shift_rows(x, True) - x), axis=0, keepdims=True)
        dmn_ref[...] = jnp.sum(d * (_shift_rows(x, False) - x), axis=0, keepdims=True)

    col = pl.BlockSpec((t, 128), lambda j: (0, j))
    par = pl.BlockSpec((1, 128), lambda j: (0, j))
    return pl.pallas_call(
        body, name="token_shift_bwd", grid=(n // 128,), in_specs=[col, par, par, col],
        out_specs=[col, par, par],
        out_shape=[jax.ShapeDtypeStruct((t, n), F32), jax.ShapeDtypeStruct((1, n), F32),
                   jax.ShapeDtypeStruct((1, n), F32)],
        compiler_params=_params(("parallel",)),
    )(p, mu_prev, mu_next, dps)


N_SEG = 32
S5_BLOCKS = 32
S5_PER_IN = 4


def _scan_in_place(sr_ref, si_ref, ar, ai, carry_ref, reverse):
    seg_len = sr_ref.shape[0] // N_SEG
    ng = N_SEG // 8

    def rows(i, grp):
        first = (seg_len - 1 - i if reverse else i) * N_SEG + 8 * grp
        return pl.ds(pl.multiple_of(first, 8), 8)

    zero = jnp.zeros((8, 128), F32)
    one = jnp.ones((8, 128), F32)

    def local(i, c):
        pr, pi = c[-2:]
        out = []
        for grp in range(ng):
            sr, si = c[2 * grp], c[2 * grp + 1]
            nr = ar * sr - ai * si + sr_ref[rows(i, grp), :]
            ni = ar * si + ai * sr + si_ref[rows(i, grp), :]
            sr_ref[rows(i, grp), :] = nr
            si_ref[rows(i, grp), :] = ni
            out += [nr, ni]
        return tuple(out) + (ar * pr - ai * pi, ar * pi + ai * pr)

    ends = lax.fori_loop(0, seg_len, local, (zero,) * (2 * ng) + (one, zero))
    qr, qi = ends[-2][0:1], ends[-1][0:1]
    order = list(range(N_SEG - 1, -1, -1)) if reverse else list(range(N_SEG))
    cr = jnp.zeros((1, 128), F32)
    ci = jnp.zeros((1, 128), F32)
    for j in order:
        carry_ref[j:j + 1, :] = cr
        carry_ref[N_SEG + j:N_SEG + j + 1, :] = ci
        grp, sub = divmod(j, 8)
        lr, li = ends[2 * grp][sub:sub + 1], ends[2 * grp + 1][sub:sub + 1]
        cr, ci = lr + qr * cr - qi * ci, li + qr * ci + qi * cr
    carries = [(carry_ref[8 * grp:8 * grp + 8, :], carry_ref[N_SEG + 8 * grp:N_SEG + 8 * grp + 8, :])
               for grp in range(ng)]

    def fix(i, c):
        pr, pi = c
        npr, npi = ar * pr - ai * pi, ar * pi + ai * pr
        for grp in range(ng):
            cr8, ci8 = carries[grp]
            sr_ref[rows(i, grp), :] = sr_ref[rows(i, grp), :] + npr * cr8 - npi * ci8
            si_ref[rows(i, grp), :] = si_ref[rows(i, grp), :] + npr * ci8 + npi * cr8
        return npr, npi

    lax.fori_loop(0, seg_len, fix, (one, zero))


def _interleave(x):
    t, c = x.shape
    return jnp.transpose(x.reshape(N_SEG, t // N_SEG, c), (1, 0, 2)).reshape(t, c)


def _deinterleave(x):
    t, c = x.shape
    return jnp.transpose(x.reshape(t // N_SEG, N_SEG, c), (1, 0, 2)).reshape(t, c)


def _step_neighbour(s, earlier):
    t = s.shape[0]
    rows = lax.broadcasted_iota(jnp.int32, s.shape, 0)
    if earlier:
        return jnp.where(rows >= N_SEG, pltpu.roll(s, N_SEG, 0),
                         jnp.where(rows >= 1, pltpu.roll(s, N_SEG + 1, 0), 0.0))
    return jnp.where(rows < t - N_SEG, pltpu.roll(s, t - N_SEG, 0),
                     jnp.where(rows < t - 1, pltpu.roll(s, t - N_SEG - 1, 0), 0.0))


def _dot_bf16(a, b, dims=(((1,), (0,)), ((), ()))):
    return lax.dot_general(a.astype(BF16), b.astype(BF16), dims, preferred_element_type=F32)


NT_DIMS = (((1,), (1,)), ((), ()))
TN_DIMS = (((0,), (0,)), ((), ()))


def _s5_specs(t):
    blk = pl.BlockSpec((None, t, 128), lambda i, q: (S5_PER_IN * i + q, 0, 0))
    mat = pl.BlockSpec((None, 128, 128), lambda i, q: (S5_PER_IN * i + q, 0, 0))
    vec = pl.BlockSpec((None, 1, 128), lambda i, q: (S5_PER_IN * i + q, 0, 0))
    chan = pl.BlockSpec((t, 128), lambda i, q: (0, i))
    return blk, mat, vec, chan


S5_GRID = (S5_BLOCKS // S5_PER_IN, S5_PER_IN)


def _s5_forward(name, u, b_re, b_im, l_re, l_im, reverse, other=None, c_re=None, c_im_neg=None):
    t = u.shape[0]
    project = other is not None
    blk, mat, vec, chan = _s5_specs(t)

    def body(*refs):
        u_ref, br_ref, bi_ref, lr_ref, li_ref = refs[:5]
        if project:
            or_ref, oi_ref, cr_ref, ci_ref, sr_ref, si_ref, y_ref, carry_ref = refs[5:]
        else:
            sr_ref, si_ref, carry_ref = refs[5:]
        uv = u_ref[...]
        sr_ref[...] = _dot_bf16(uv, br_ref[...])
        si_ref[...] = _dot_bf16(uv, bi_ref[...])
        ar = jnp.broadcast_to(lr_ref[...], (8, 128))
        ai = jnp.broadcast_to(li_ref[...], (8, 128))
        _scan_in_place(sr_ref, si_ref, ar, ai, carry_ref, reverse)
        if project:
            y = (_dot_bf16(sr_ref[...] + or_ref[...], cr_ref[...])
                 + _dot_bf16(si_ref[...] + oi_ref[...], ci_ref[...]))

            @pl.when(pl.program_id(1) == 0)
            def _():
                y_ref[...] = y

            @pl.when(pl.program_id(1) != 0)
            def _():
                y_ref[...] += y

    state = jax.ShapeDtypeStruct((S5_BLOCKS, t, 128), F32)
    ins = [u, b_re, b_im, l_re, l_im] + ([other[0], other[1], c_re, c_im_neg] if project else [])
    in_specs = [chan, mat, mat, vec, vec] + ([blk, blk, mat, mat] if project else [])
    return pl.pallas_call(
        body, name=name, grid=S5_GRID, in_specs=in_specs,
        out_specs=[blk, blk] + ([chan] if project else []),
        out_shape=[state, state] + ([jax.ShapeDtypeStruct((t, S5_WIDTH), F32)] if project else []),
        scratch_shapes=[pltpu.VMEM((2 * N_SEG, 128), F32)],
        compiler_params=_params(("arbitrary", "arbitrary")),
    )(*ins)


def _s5_backward(name, dy, u, du_in, states, other, b_re, b_im, c_re, c_im_neg, l_re, l_im, reverse):
    t = u.shape[0]
    with_c = other is not None
    blk, mat, vec, chan = _s5_specs(t)

    def body(*refs):
        dy_ref, u_ref, du_in_ref, sr_ref, si_ref = refs[:5]
        pos = 5
        if with_c:
            or_ref, oi_ref = refs[5:7]
            pos = 7
        br_ref, bi_ref, cr_ref, ci_ref, lr_ref, li_ref = refs[pos:pos + 6]
        outs = refs[pos + 6:]
        du_ref, dbr_ref, dbi_ref, dlr_ref, dli_ref = outs[:5]
        lam_r, lam_i, carry_ref = outs[-3:]
        dyv, uv = dy_ref[...], u_ref[...]
        lam_r[...] = _dot_bf16(dyv, cr_ref[...], NT_DIMS)
        lam_i[...] = _dot_bf16(dyv, ci_ref[...], NT_DIMS)
        ar = jnp.broadcast_to(lr_ref[...], (8, 128))
        ai = -jnp.broadcast_to(li_ref[...], (8, 128))
        _scan_in_place(lam_r, lam_i, ar, ai, carry_ref, not reverse)
        lr, li = lam_r[...], lam_i[...]
        pr, pi = _step_neighbour(sr_ref[...], not reverse), _step_neighbour(si_ref[...], not reverse)
        dlr_ref[...] = jnp.sum(lr * pr + li * pi, axis=0, keepdims=True)
        dli_ref[...] = jnp.sum(li * pr - lr * pi, axis=0, keepdims=True)
        dbr_ref[...] = _dot_bf16(uv, lr, TN_DIMS)
        dbi_ref[...] = _dot_bf16(uv, li, TN_DIMS)
        du = _dot_bf16(lr, br_ref[...], NT_DIMS) + _dot_bf16(li, bi_ref[...], NT_DIMS)

        @pl.when(pl.program_id(1) == 0)
        def _():
            du_ref[...] = du_in_ref[...] + du

        @pl.when(pl.program_id(1) != 0)
        def _():
            du_ref[...] += du

        if with_c:
            dcr_ref, dci_ref = outs[5:7]
            dcr_ref[...] = _dot_bf16(sr_ref[...] + or_ref[...], dyv, TN_DIMS)
            dci_ref[...] = _dot_bf16(si_ref[...] + oi_ref[...], dyv, TN_DIMS)

    mats = jax.ShapeDtypeStruct((S5_BLOCKS, 128, 128), F32)
    vecs = jax.ShapeDtypeStruct((S5_BLOCKS, 1, 128), F32)
    ins = [dy, u, du_in, states[0], states[1]] + ([other[0], other[1]] if with_c else [])
    ins += [b_re, b_im, c_re, c_im_neg, l_re, l_im]
    in_specs = [chan, chan, chan, blk, blk] + ([blk, blk] if with_c else []) + [mat] * 4 + [vec] * 2
    return pl.pallas_call(
        body, name=name, grid=S5_GRID, in_specs=in_specs,
        out_specs=[chan, mat, mat, vec, vec] + ([mat, mat] if with_c else []),
        out_shape=[jax.ShapeDtypeStruct((t, S5_WIDTH), F32), mats, mats, vecs, vecs] + ([mats, mats] if with_c else []),
        scratch_shapes=[pltpu.VMEM((t, 128), F32), pltpu.VMEM((t, 128), F32), pltpu.VMEM((2 * N_SEG, 128), F32)],
        compiler_params=_params(("arbitrary", "arbitrary")),
    )(*ins)


def _ein(passes, spec, a, b):
    if passes == 6:
        return jnp.einsum(spec, a, b, precision=HI, preferred_element_type=F32)
    a_hi, b_hi = a.astype(BF16), b.astype(BF16)
    if passes == 1:
        return jnp.einsum(spec, a_hi, b_hi, preferred_element_type=F32)
    a_lo = (a - a_hi.astype(F32)).astype(BF16)
    b_lo = (b - b_hi.astype(F32)).astype(BF16)
    cross = jnp.einsum(spec, a_hi, b_lo, preferred_element_type=F32)
    if spec.startswith('hik'):
        m = a.shape[1]
        stacked = jnp.einsum(spec, jnp.concatenate([a_hi, a_lo], axis=1), b_hi, preferred_element_type=F32)
        return stacked[:, :m] + stacked[:, m:] + cross
    return (jnp.einsum(spec, a_hi, b_hi, preferred_element_type=F32) + cross
            + jnp.einsum(spec, a_lo, b_hi, preferred_element_type=F32))


@jax.custom_vjp
def _tri_mm(tri, tri_t, z):
    return jnp.einsum('hik,hkj->hij', tri, z, precision=HI, preferred_element_type=F32)


def _tri_mm_bwd(res, g):
    tri, tri_t = res
    return jnp.zeros_like(tri), jnp.zeros_like(tri_t), _tri_mm(tri_t, tri, g)


_tri_mm.defvjp(lambda tri, tri_t, z: (_tri_mm(tri, tri_t, z), (tri, tri_t)), _tri_mm_bwd)


def _chunk_cumsum(lw, incl, incl_t):
    shape = (lw.shape[0],) + incl.shape
    return _tri_mm(jnp.broadcast_to(incl.astype(F32), shape), jnp.broadcast_to(incl_t.astype(F32), shape), lw)


@functools.partial(jax.custom_vjp, nondiff_argnums=(0,))
def _bmm(p, a, b):
    return _ein(p, 'hik,hkj->hij', a, b)


@functools.partial(jax.custom_vjp, nondiff_argnums=(0,))
def _bmm_nt(p, a, b):
    return _ein(p, 'hik,hjk->hij', a, b)


@functools.partial(jax.custom_vjp, nondiff_argnums=(0,))
def _bmm_tn(p, a, b):
    return _ein(p, 'hki,hkj->hij', a, b)


_bmm.defvjp(lambda p, a, b: (_bmm(p, a, b), (a, b)),
            lambda p, res, g: (_bmm_nt(p, g, res[1]), _bmm_tn(p, res[0], g)))
_bmm_nt.defvjp(lambda p, a, b: (_bmm_nt(p, a, b), (a, b)),
               lambda p, res, g: (_bmm(p, g, res[1]), _bmm_tn(p, g, res[0])))
_bmm_tn.defvjp(lambda p, a, b: (_bmm_tn(p, a, b), (a, b)),
               lambda p, res, g: (_bmm_nt(p, res[1], g), _bmm(p, res[0], g)))


@jax.custom_vjp
def _split_rows(x):
    c = x.shape[1] // 2
    return x[:, :c], x[:, c:]


_split_rows.defvjp(lambda x: (_split_rows(x), None), lambda _, g: (jnp.concatenate(g, axis=1),))


def _stack_rows(a, b):
    return jnp.concatenate([a, b], axis=1)


def _rk_chunk(s0, r, lw, k, v, kk, a, reverse):
    h, c, n = r.shape
    row = lax.broadcasted_iota(jnp.int32, (c, c), 0)
    col = lax.broadcasted_iota(jnp.int32, (c, c), 1)
    incl = (row <= col) if reverse else (row >= col)
    strict = (row < col) if reverse else (row > col)
    cum = _chunk_cumsum(lw, incl, (row >= col) if reverse else (row <= col))
    g_in = jnp.exp(cum)
    g_inv = jnp.exp(-cum)
    kap = kk * jnp.exp(cum - lw)
    beta = kk * a * g_inv
    kt = k * g_inv
    rt = r * g_in
    p, ps = RK_PASSES, RK_PASSES["solve"]
    both = _stack_rows(kap, rt)
    kap_beta, rt_beta = _split_rows(_bmm_nt(ps, both, beta))
    kap_kt, rt_kt = _split_rows(_bmm_nt(p["kt"], both, kt))
    kap_s0, rt_s0 = _split_rows(_bmm_nt(p["s0"], both, s0))
    l_mat = jnp.where(strict, kap_beta, 0.0)
    rhs = kap_s0 + _bmm(p["akk_v"], jnp.where(strict, kap_kt, 0.0), v)
    x = -l_mat
    inv = jnp.where(row == col, 1.0, 0.0) + x
    power = _bmm(ps, x, x)
    span = 2
    while 2 * span < c:
        step, power = _split_rows(_bmm(ps, _stack_rows(inv, power), power))
        inv = inv + step
        span *= 2
    inv = inv + _bmm(ps, inv, power)
    u = _bmm(ps, inv, rhs)
    y = (rt_s0 + _bmm(p["ark_v"], jnp.where(incl, rt_kt, 0.0), v)
         - _bmm(p["arb_u"], jnp.where(incl, rt_beta, 0.0), u))
    s1 = ((s0 + _bmm_tn(p["state"], _stack_rows(v, -u), _stack_rows(kt, beta)))
          * jnp.exp(jnp.sum(lw, axis=1, keepdims=True)))
    return y, s1


class _Plan:
    def __init__(self, arrays, out_shape, sems, start, wait, finish):
        self.arrays, self.out_shape, self.sems = list(arrays), list(out_shape), list(sems)
        self.start, self.wait, self.finish = start, wait, finish


_NO_PLAN = _Plan([], [], [], lambda *_: None, lambda *_: None, lambda outs: [])


def _split_heads(x):
    return jnp.stack([x[:, RK_HEAD * i:RK_HEAD * (i + 1)] for i in range(RK_HEADS)], axis=0)


def _store_heads(ref, x):
    for i in range(RK_HEADS):
        ref[:, RK_HEAD * i:RK_HEAD * (i + 1)] = x[i]


def _rk_core_fwd(name, r, lw, k, v, kk, a, reverse, chunk, hosted=None):
    t = r.shape[0]
    h, n = RK_HEADS, RK_HEAD
    nc = t // chunk

    def idx(i):
        return nc - 1 - i if reverse else i

    hosted = hosted or _NO_PLAN
    nh = len(hosted.arrays)

    def body(r_ref, lw_ref, k_ref, v_ref, kk_ref, a_ref, *rest):
        host_in, (y_ref, ck_ref), host_out = rest[:nh], rest[nh:nh + 2], rest[nh + 2:2 * nh + 2]
        s_ref, sems = rest[2 * nh + 2], rest[2 * nh + 3:]

        @pl.when(pl.program_id(0) == 0)
        def _():
            s_ref[...] = jnp.zeros_like(s_ref)
            hosted.start(host_in, host_out, sems)

        s0 = s_ref[...]
        ck_ref[0] = s0
        ops = [_split_heads(ref[...]) for ref in (r_ref, lw_ref, k_ref, v_ref, kk_ref, a_ref)]
        y, s1 = _rk_chunk(s0, *ops, reverse)
        _store_heads(y_ref, y)
        s_ref[...] = s1

        @pl.when(pl.program_id(0) == nc - 1)
        def _():
            hosted.wait(host_in, host_out, sems)

    blk = pl.BlockSpec((chunk, RK_WIDTH), lambda i: (idx(i), 0))
    any_spec = pl.BlockSpec(memory_space=pl.ANY)
    res = pl.pallas_call(
        body, name=name, grid=(nc,), in_specs=[blk] * 6 + [any_spec] * nh,
        out_specs=[blk, pl.BlockSpec((1, h, n, n), lambda i: (idx(i), 0, 0, 0))] + [any_spec] * nh,
        out_shape=[jax.ShapeDtypeStruct((t, RK_WIDTH), F32), jax.ShapeDtypeStruct((nc, h, n, n), F32)]
        + hosted.out_shape,
        scratch_shapes=[pltpu.VMEM((h, n, n), F32)] + hosted.sems,
        compiler_params=_params(("arbitrary",)),
    )(r, lw, k, v, kk, a, *hosted.arrays)
    return res[0], res[1], hosted.finish(res[2:])


def _rk_core_bwd(name, r, lw, k, v, kk, a, ck, dy, reverse, chunk, hosted=None):
    t = r.shape[0]
    h, n = RK_HEADS, RK_HEAD
    nc = t // chunk
    hosted = hosted or _NO_PLAN
    nh = len(hosted.arrays)

    def idx(i):
        return i if reverse else nc - 1 - i

    def body(r_ref, lw_ref, k_ref, v_ref, kk_ref, a_ref, ck_ref, dy_ref, *rest):
        host_in, out_refs, host_out = rest[:nh], rest[nh:nh + 6], rest[nh + 6:2 * nh + 6]
        ds_ref, sems = rest[2 * nh + 6], rest[2 * nh + 7:]

        @pl.when(pl.program_id(0) == 0)
        def _():
            ds_ref[...] = jnp.zeros_like(ds_ref)
            hosted.start(host_in, host_out, sems)

        fn = functools.partial(_rk_chunk, reverse=reverse)
        ops = [_split_heads(ref[...]) for ref in (r_ref, lw_ref, k_ref, v_ref, kk_ref, a_ref)]
        _, vjp = jax.vjp(fn, ck_ref[0], *ops)
        grads = vjp((_split_heads(dy_ref[...]), ds_ref[...]))
        ds_ref[...] = grads[0]
        for o_ref, g in zip(out_refs, grads[1:]):
            _store_heads(o_ref, g)

        @pl.when(pl.program_id(0) == nc - 1)
        def _():
            hosted.wait(host_in, host_out, sems)

    blk = pl.BlockSpec((chunk, RK_WIDTH), lambda i: (idx(i), 0))
    any_spec = pl.BlockSpec(memory_space=pl.ANY)
    res = pl.pallas_call(
        body, name=name, grid=(nc,),
        in_specs=[blk] * 6 + [pl.BlockSpec((1, h, n, n), lambda i: (idx(i), 0, 0, 0)), blk] + [any_spec] * nh,
        out_specs=[blk] * 6 + [any_spec] * nh,
        out_shape=[jax.ShapeDtypeStruct((t, RK_WIDTH), F32)] * 6 + hosted.out_shape,
        scratch_shapes=[pltpu.VMEM((h, n, n), F32)] + hosted.sems,
        compiler_params=_params(("arbitrary",)),
    )(r, lw, k, v, kk, a, ck, dy, *hosted.arrays)
    return res[:6], hosted.finish(res[6:])


def _s5_band_place():
    return jax.nn.one_hot(jnp.arange(S5_BLOCKS) % S5_PER_IN, S5_PER_IN, dtype=F32)


def _s5_in_blocks(bbar):
    b = jnp.transpose(bbar.reshape(S5_BLOCKS, 2, S5_STATE, S5_GROUP), (0, 1, 3, 2))
    band = jnp.einsum('jghp,gk->jghkp', b, jnp.eye(2, dtype=F32)).reshape(S5_BLOCKS, 32, 128)
    return jnp.einsum('jrc,jq->jqrc', band, _s5_band_place()).reshape(S5_BLOCKS, 128, 128)


def _s5_in_unblock(mats):
    band = jnp.einsum('jqrc,jq->jrc', mats.reshape(S5_BLOCKS, S5_PER_IN, 32, 128), _s5_band_place())
    diag = jnp.einsum('jghgp->jghp', band.reshape(S5_BLOCKS, 2, S5_GROUP, 2, S5_STATE))
    return jnp.transpose(diag, (0, 1, 3, 2)).reshape(S5_CH, S5_GROUP)


def _s5_out_blocks(c):
    ct = jnp.transpose(c.reshape(S5_BLOCKS, 2, S5_GROUP, S5_STATE), (0, 1, 3, 2))
    band = jnp.einsum('jgph,gk->jgpkh', ct, jnp.eye(2, dtype=F32)).reshape(S5_BLOCKS, 128, 32)
    return jnp.einsum('jrc,jq->jrqc', band, _s5_band_place()).reshape(S5_BLOCKS, 128, 128)


def _s5_out_unblock(mats):
    band = jnp.einsum('jrqc,jq->jrc', mats.reshape(S5_BLOCKS, 128, S5_PER_IN, 32), _s5_band_place())
    diag = jnp.einsum('jgpgh->jgph', band.reshape(S5_BLOCKS, 2, S5_STATE, 2, S5_GROUP))
    return jnp.transpose(diag, (0, 1, 3, 2)).reshape(S5_GROUPS, S5_GROUP, S5_STATE)


def _head_indicator():
    ch = lax.broadcasted_iota(jnp.int32, (RK_WIDTH, 128), 0) // RK_HEAD
    hd = lax.broadcasted_iota(jnp.int32, (RK_WIDTH, 128), 1)
    seg = (ch == hd).astype(F32)
    return seg, seg.T


def _add_epilogue(acc, e):
    return (acc + e,)


def _local_step(x, target, mod, wt, chunk=RK_CHUNK, ffn_shards=None, ffn_reduce=None):
    t = x.shape[0]
    wt = dict(wt)
    sh1, sc1, gt1, sh2, sc2, gt2 = mod
    seg, seg_t = _head_indicator()
    g = {}

    (h1,) = _rowwise("norm1", _norm_mod_fn, [x], [wt["norm1_gain"], sc1, sh1], [(D_MODEL, BF16)], 256)
    proj = _matmul("proj", h1, wt["w_in"])
    u, p = proj[:, :S5_WIDTH], proj[:, S5_WIDTH:]
    ps = _token_shift(p, wt["mu_prev"], wt["mu_next"])
    r, k, v = ps[:, :1024], ps[:, 1024:2048], ps[:, 2048:3072]
    wdn, adn, gdn = ps[:, 3072:3200], ps[:, 3200:3328], ps[:, 3328:RK_PAD]

    prep_rows = [wt["lam_re"][0], wt["lam_im"][0], wt["log_step"][0], wt["lam_re"][1], wt["lam_im"][1],
                 wt["log_step"][1], wt["b_re"], wt["b_im"]]
    col1, col16 = (1, F32), (S5_GROUP, F32)
    prep = _rowwise("s5_prep", _s5_prep_fn, prep_rows, [], [col1, col1, col16, col16] * 2, 512)
    lbar = [tuple(prep[4 * d + q].reshape(S5_BLOCKS, 1, 128) for q in range(2)) for d in range(2)]
    b_blk = [tuple(_s5_in_blocks(prep[4 * d + 2 + q]) for q in range(2)) for d in range(2)]
    c_blk = (_s5_out_blocks(wt["c_re"]), -_s5_out_blocks(wt["c_im"]))
    u_il = _interleave(u)
    state0 = _s5_forward("s5_fwd0", u_il, *b_blk[0], *lbar[0], reverse=False)
    s1_re, s1_im, ylin_il = _s5_forward("s5_fwd1", u_il, *b_blk[1], *lbar[1], reverse=True, other=state0,
                                        c_re=c_blk[0], c_im_neg=c_blk[1])
    ylin = _deinterleave(ylin_il)
    states = [tuple(state0), (s1_re, s1_im)]
    s5_par = [wt["s5_d"], wt["s5_w_glu"], wt["s5_b_glu"]]
    (o_s5,) = _rowwise("s5_out", _s5_out_fn, [ylin, u], s5_par, [(S5_WIDTH, BF16)], 256)

    pre_par = [wt["w0"][0], wt["w0"][1], wt["w_up"][0], wt["w_up"][1], wt["a0"][0], wt["a0"][1],
               wt["a_up"][0], wt["a_up"][1], wt["g_up"], wt["k_k"], wt["k_a"]]
    pre = _rowwise("rk_pre", _rk_pre_fn, [k, wdn, adn, gdn], pre_par + [seg, seg_t], [(RK_WIDTH, F32)] * 8, 256)
    kk, lw, kd, act, gate = pre[0], pre[1:3], pre[3:5], pre[5:7], pre[7]
    core_in, ys, cks = [], [], []
    for d in range(2):
        ops = (r, lw[d], kd[d], v, kk, act[d])
        plan = _gather_halves_plan([ffn_shards[d]]) if ffn_shards is not None else None
        y, ck, gathered = _rk_core_fwd(f"rk_core{d}", *ops, reverse=(d == 1), chunk=min(chunk, t), hosted=plan)
        if gathered:
            wt["ffn_w1" if d == 0 else "ffn_w2"] = (_chips_to_cols(gathered[0]) if d == 0
                                                    else gathered[0].reshape(FFN, D_MODEL))
        core_in.append(ops)
        ys.append(y)
        cks.append(ck)
    post_rows = [ys[0], ys[1], r, v, kd[0], kd[1], gate]
    post_par = [wt["ln_gain"], wt["ln_bias"], wt["r_k"]]
    (o_rk,) = _rowwise("rk_post", _rk_post_fn, post_rows, post_par + [seg, seg_t], [(RK_WIDTH, BF16)], 256)

    o = jnp.concatenate([o_s5, o_rk], axis=1)
    mixed = _matmul("mix_out", o, wt["w_out"])
    n2_par = [gt1, wt["norm2_gain"], sc2, sh2]
    x1, h2 = _rowwise("norm2", _resid_norm_mod_fn, [x, mixed], n2_par, [(D_MODEL, F32), (D_MODEL, BF16)], 256)
    f1, hid = _matmul("ffn1", h2, wt["ffn_w1"], out_dtypes=(F32, BF16),
                      epilogue=lambda acc: (acc, jnp.square(jnp.maximum(acc, 0.0))))
    ffn = _matmul("ffn2", hid, wt["ffn_w2"])

    ones = jnp.ones((t, 1), F32)
    loss_rows, dx1, dffn, g_gt2, g["final_gain"] = _rowwise_vjp(
        "loss", _loss_fn, [x1, ffn, target], [gt2, wt["final_gain"]], [[ones]], [0, 1], [0, 1], 256, emit=(0,),
        row_grad_dtypes=[F32, BF16])
    df1 = _matmul("ffn2_dx", dffn, wt["ffn_w2"], tb=True, extras=(f1,), out_dtypes=(BF16,),
                  epilogue=lambda acc, f: (acc * (2.0 * jnp.maximum(f, 0.0)),))
    g["ffn_w2"] = _matmul("ffn2_dw", hid, dffn, ta=True)
    dh2 = _matmul("ffn1_dx", df1, wt["ffn_w1"], tb=True)
    g["ffn_w1"] = _matmul("ffn1_dw", h2, df1, ta=True)
    dx_a, dmixed, g_gt1, g["norm2_gain"], g_sc2, g_sh2 = _rowwise_vjp(
        "norm2_bwd", _resid_norm_mod_fn, [x, mixed], n2_par, [[dx1], [dh2]], [0, 1], [0, 1, 2, 3], 256,
        row_grad_dtypes=[F32, BF16])
    do = _matmul("mix_out_dx", dmixed, wt["w_out"], tb=True)
    g["w_out"] = _matmul("mix_out_dw", o, dmixed, ta=True)
    do_s5, do_rk = do[:, :S5_WIDTH], do[:, S5_WIDTH:]

    dylin, du, g["s5_d"], g["s5_w_glu"], g["s5_b_glu"] = _rowwise_vjp(
        "s5_out_bwd", _s5_out_fn, [ylin, u], s5_par, [[do_s5]], [0, 1], [0, 1, 2], 256)
    prep_cts = []
    dylin_il, du_il = _interleave(dylin), _interleave(du)
    for d in range(2):
        res = _s5_backward(f"s5_bwd{d}", dylin_il, u_il, du_il, states[d], states[1] if d == 0 else None,
                           *b_blk[d], *c_blk, *lbar[d], reverse=(d == 1))
        du_il, db_re, db_im, dl_re, dl_im = res[:5]
        if d == 0:
            g["c_re"], g["c_im"] = _s5_out_unblock(res[5]), -_s5_out_unblock(res[6])
        prep_cts += [[dl_re.reshape(S5_CH, 1)], [dl_im.reshape(S5_CH, 1)], [_s5_in_unblock(db_re)],
                     [_s5_in_unblock(db_im)]]
    du = _deinterleave(du_il)
    pg = _rowwise_vjp("s5_prep_bwd", _s5_prep_fn, prep_rows, [], prep_cts, list(range(8)), [], 512)
    g["lam_re"], g["lam_im"], g["log_step"] = (pg[0], pg[3]), (pg[1], pg[4]), (pg[2], pg[5])
    g["b_re"], g["b_im"] = pg[6], pg[7]

    pb = _rowwise_vjp("rk_post_bwd", _rk_post_fn, post_rows, post_par, [[do_rk]], [0, 2, 3, 4, 5, 6], [0, 1, 2],
                      128, consts=[seg, seg_t])
    dy, dr_b, dv_b, dkd_b, dgate = pb[0], pb[1], pb[2], pb[3:5], pb[5]
    g["ln_gain"], g["ln_bias"], g["r_k"] = pb[6], pb[7], pb[8]
    cg = []
    for d in range(2):
        plan = None
        if d == 0 and ffn_reduce is not None:
            plan = _exchange_plan(ffn_reduce(g.pop("ffn_w1"), g.pop("ffn_w2")), CHIP_PEERS, N_CHIPS, scatter=True)
        grads, arrived = _rk_core_bwd(f"rk_core{d}_bwd", *core_in[d], cks[d], dy, reverse=(d == 1),
                                      chunk=min(chunk, t), hosted=plan)
        if arrived:
            g["ffn_arrived"] = arrived
        cg.append(grads)
    pre_cts = [[cg[0][4], cg[1][4]], [cg[0][1]], [cg[1][1]], [cg[0][2], dkd_b[0]], [cg[1][2], dkd_b[1]],
               [cg[0][5]], [cg[1][5]], [dgate]]
    qb = _rowwise_vjp("rk_pre_bwd", _rk_pre_fn, [k, wdn, adn, gdn], pre_par, pre_cts, [0, 1, 2, 3],
                      list(range(11)), 128, consts=[seg, seg_t])
    dk, dwdn, dadn, dgdn = qb[:4]
    g["w0"], g["w_up"], g["a0"], g["a_up"] = (qb[4], qb[5]), (qb[6], qb[7]), (qb[8], qb[9]), (qb[10], qb[11])
    g["g_up"], g["k_k"], g["k_a"] = qb[12], qb[13], qb[14]
    dr, dv = _rowwise("rk_sum", lambda a, b, c, e, f, h: (a + b + c, e + f + h),
                      [cg[0][0], cg[1][0], dr_b, cg[0][3], cg[1][3], dv_b], [], [(RK_WIDTH, F32)] * 2, 256)
    dps = jnp.concatenate([dr, dk, dv, dwdn, dadn, dgdn], axis=1)
    dp, g["mu_prev"], g["mu_next"] = _token_shift_bwd(p, wt["mu_prev"], wt["mu_next"], dps)

    dproj = jnp.concatenate([du, dp], axis=1).astype(BF16)
    dh1 = _matmul("proj_dx", dproj, wt["w_in"], tb=True)
    g["w_in"] = _matmul("proj_dw", h1, dproj, ta=True)
    grad_x, g["norm1_gain"], g_sc1, g_sh1 = _rowwise_vjp(
        "norm1_bwd", _norm_mod_fn, [x], [wt["norm1_gain"], sc1, sh1], [[dh1]], [0], [0, 1, 2], 256,
        addends={0: dx_a})
    g["mod"] = [g_sh1, g_sc1, g_gt1, g_sh2, g_sc2, g_gt2]
    return loss_rows, grad_x, g


CHIP_PEERS = ((1, 0, 0), (0, 1, 0), (1, 1, 0))
ALL_PEERS = ((0, 0, 1), (0, 1, 0), (0, 1, 1), (1, 0, 0), (1, 0, 1), (1, 1, 0), (1, 1, 1))
CORE_PEER = ((0, 0, 1),)


def _exchange(name, arrays, peers, n_slots, scatter=False):
    return _run_plan(name, _exchange_plan(arrays, peers, n_slots, scatter))


def _run_plan(name, plan):
    na = len(plan.arrays)

    def body(*refs):
        plan.start(refs[:na], refs[na:2 * na], refs[2 * na:])
        plan.wait(refs[:na], refs[na:2 * na], refs[2 * na:])

    any_spec = pl.BlockSpec(memory_space=pl.ANY)
    return plan.finish(pl.pallas_call(
        body, name=name, in_specs=[any_spec] * na, out_specs=[any_spec] * na, out_shape=plan.out_shape,
        scratch_shapes=plan.sems,
    )(*plan.arrays))


def _exchange_plan(arrays, peers, n_slots, scatter=False):
    na, nm = len(arrays), len(peers)

    def ident(px, py, pc):
        return {8: 4 * px + 2 * py + pc, 4: 2 * px + py, 2: pc}[n_slots]

    def copies(in_refs, out_refs, sems):
        send_sems, recv_sems = sems
        x, y, c = lax.axis_index("x"), lax.axis_index("y"), lax.axis_index("c")
        me = ident(x, y, c)
        made = []
        for i in range(na):
            for j, (fx, fy, fc) in enumerate(peers):
                px, py, pc = (1 - x if fx else x), (1 - y if fy else y), (1 - c if fc else c)
                src = in_refs[i].at[ident(px, py, pc)] if scatter else in_refs[i]
                made.append(pltpu.make_async_remote_copy(
                    src_ref=src, dst_ref=out_refs[i].at[me],
                    send_sem=send_sems.at[i * nm + j], recv_sem=recv_sems.at[i * nm + j],
                    device_id=(px, py, pc), device_id_type=pl.DeviceIdType.MESH))
        return made

    def start(in_refs, out_refs, sems):
        for copy in copies(in_refs, out_refs, sems):
            copy.start()

    def wait(in_refs, out_refs, sems):
        for copy in copies(in_refs, out_refs, sems):
            copy.wait()

    def finish(outs):
        me = ident(lax.axis_index("x"), lax.axis_index("y"), lax.axis_index("c"))
        return [lax.dynamic_update_slice_in_dim(
            o, lax.dynamic_index_in_dim(a, me, 0, keepdims=True) if scatter else a[None], me, axis=0)
            for a, o in zip(arrays, outs)]

    out_shape = [jax.ShapeDtypeStruct(((n_slots,) + a.shape[1:]) if scatter else ((n_slots,) + a.shape), a.dtype)
                 for a in arrays]
    sems = [pltpu.SemaphoreType.DMA((na * nm,)), pltpu.SemaphoreType.DMA((na * nm,))]
    return _Plan(arrays, out_shape, sems, start, wait, finish)


def _gather_halves(name, arrays):
    return _run_plan(name, _gather_halves_plan(arrays))


def _gather_halves_plan(arrays):
    na = len(arrays)
    chips = ((1, 0), (0, 1), (1, 1))

    def over_ici(in_refs, out_refs, sems):
        ici_send, ici_recv = sems[:2]
        x, y, c = lax.axis_index("x"), lax.axis_index("y"), lax.axis_index("c")
        made = []
        for i in range(na):
            half = arrays[i].shape[0] // 2
            mine = pl.ds(pl.multiple_of(c * half, 8), half)
            for j, (fx, fy) in enumerate(chips):
                px, py = (1 - x if fx else x), (1 - y if fy else y)
                k = len(chips) * i + j
                made.append([pltpu.make_async_remote_copy(
                    src_ref=in_refs[i].at[mine], dst_ref=out_refs[i].at[chip, mine],
                    send_sem=ici_send.at[k], recv_sem=ici_recv.at[k],
                    device_id=(px, py, c), device_id_type=pl.DeviceIdType.MESH)
                    for chip in (2 * x + y, 2 * px + py)])
        return made

    def start(in_refs, out_refs, sems):
        for outgoing, _ in over_ici(in_refs, out_refs, sems):
            outgoing.start()

    def wait(in_refs, out_refs, sems):
        d2d_send, d2d_recv = sems[2:]
        x, y, c = lax.axis_index("x"), lax.axis_index("y"), lax.axis_index("c")
        pending = []
        ici = over_ici(in_refs, out_refs, sems)
        for i in range(na):
            half = arrays[i].shape[0] // 2
            mine = pl.ds(pl.multiple_of(c * half, 8), half)
            theirs = pl.ds(pl.multiple_of((1 - c) * half, 8), half)
            for j, (fx, fy) in enumerate(chips):
                px, py = (1 - x if fx else x), (1 - y if fy else y)
                k = len(chips) * i + j
                outgoing, landing = ici[k]
                landing.wait_recv()
                landed = out_refs[i].at[2 * px + py, mine]
                passed = pltpu.make_async_remote_copy(
                    src_ref=landed, dst_ref=landed, send_sem=d2d_send.at[k], recv_sem=d2d_recv.at[k],
                    device_id=(x, y, 1 - c), device_id_type=pl.DeviceIdType.MESH)
                passed.start()
                from_sibling = out_refs[i].at[2 * px + py, theirs]
                pending += [outgoing.wait_send, passed.wait_send, pltpu.make_async_remote_copy(
                    src_ref=from_sibling, dst_ref=from_sibling, send_sem=d2d_send.at[k], recv_sem=d2d_recv.at[k],
                    device_id=(x, y, 1 - c), device_id_type=pl.DeviceIdType.MESH).wait_recv]
        for done in pending:
            done()

    def finish(outs):
        me = 2 * lax.axis_index("x") + lax.axis_index("y")
        return [lax.dynamic_update_slice_in_dim(o, a[None], me, axis=0) for a, o in zip(arrays, outs)]

    out_shape = [jax.ShapeDtypeStruct((N_CHIPS,) + a.shape, a.dtype) for a in arrays]
    return _Plan(arrays, out_shape, [pltpu.SemaphoreType.DMA((na * len(chips),))] * 4, start, wait, finish)


def _send_other_half(name, arrays):
    na = len(arrays)

    def body(*refs):
        in_refs, out_refs, send_sems, recv_sems = refs[:na], refs[na:2 * na], refs[-2], refs[-1]
        x, y, c = lax.axis_index("x"), lax.axis_index("y"), lax.axis_index("c")
        copies = []
        for i in range(na):
            half = arrays[i].shape[1] // 2
            theirs = pl.ds(pl.multiple_of((1 - c) * half, 8), half)
            copy = pltpu.make_async_remote_copy(
                src_ref=in_refs[i].at[:, theirs], dst_ref=out_refs[i], send_sem=send_sems.at[i],
                recv_sem=recv_sems.at[i], device_id=(x, y, 1 - c), device_id_type=pl.DeviceIdType.MESH)
            copy.start()
            copies.append(copy)
        for copy in copies:
            copy.wait()

    any_spec = pl.BlockSpec(memory_space=pl.ANY)
    return pl.pallas_call(
        body, name=name, in_specs=[any_spec] * na, out_specs=[any_spec] * na,
        out_shape=[jax.ShapeDtypeStruct((a.shape[0], a.shape[1] // 2, a.shape[2]), a.dtype) for a in arrays],
        scratch_shapes=[pltpu.SemaphoreType.DMA((na,)), pltpu.SemaphoreType.DMA((na,))],
    )(*arrays)


def _adam_math(w, g, m, v):
    m = ADAM_B1 * m + (1.0 - ADAM_B1) * g
    v = ADAM_B2 * v + (1.0 - ADAM_B2) * jnp.square(g)
    m_hat = m / (1.0 - ADAM_B1 ** ADAM_STEP)
    v_hat = v / (1.0 - ADAM_B2 ** ADAM_STEP)
    delta = -ADAM_LR * (m_hat / (jnp.sqrt(v_hat) + ADAM_EPS) + ADAM_WD * w)
    return delta, m, v


def _row_tile(r):
    return _tile(r, (256, 128, 64, 32, 16, 8))


def _sum_parts(name, parts):
    n, r, c = parts.shape
    tr = _row_tile(r)

    def body(p_ref, o_ref):
        tot = p_ref[0].astype(F32)
        for i in range(1, n):
            tot = tot + p_ref[i].astype(F32)
        o_ref[...] = tot

    return pl.pallas_call(
        body, name=name, grid=(r // tr,), in_specs=[pl.BlockSpec((n, tr, c), lambda i: (0, i, 0))],
        out_specs=pl.BlockSpec((tr, c), lambda i: (i, 0)), out_shape=jax.ShapeDtypeStruct((r, c), F32),
        compiler_params=_params(("parallel",)),
    )(parts)


def _pair_sum(name, piece, other, dtype):
    n, r, c = piece.shape
    half = r // 2
    tr = _row_tile(half)

    def body(lo_ref, hi_ref, other_ref, o_ref):
        own = jnp.where(lax.axis_index("c") == 0, lo_ref[...], hi_ref[...])
        o_ref[...] = (own + other_ref[...]).astype(o_ref.dtype)

    blk = pl.BlockSpec((None, tr, c), lambda j, i: (j, i, 0))
    return pl.pallas_call(
        body, name=name, grid=(n, half // tr),
        in_specs=[pl.BlockSpec((None, None, tr, c), lambda j, i: (j, 0, i, 0)),
                  pl.BlockSpec((None, None, tr, c), lambda j, i: (j, 1, i, 0)), blk],
        out_specs=blk, out_shape=jax.ShapeDtypeStruct((n, half, c), dtype),
        compiler_params=_params(("parallel", "parallel")),
    )(piece.reshape(n, 2, half, c), piece.reshape(n, 2, half, c), other)


def _adamw(name, w, parts, m, v):
    n, r, c = parts.shape
    tr = _row_tile(r)

    def body(w_ref, p_ref, m_ref, v_ref, g_ref, d_ref, nm_ref, nv_ref):
        g = p_ref[0]
        for i in range(1, n):
            g = g + p_ref[i]
        delta, nm, nv = _adam_math(w_ref[...], g, m_ref[...], v_ref[...])
        g_ref[...], d_ref[...], nm_ref[...], nv_ref[...] = g, delta, nm, nv

    blk = pl.BlockSpec((tr, c), lambda i: (i, 0))
    return pl.pallas_call(
        body, name=name, grid=(r // tr,),
        in_specs=[blk, pl.BlockSpec((n, tr, c), lambda i: (0, i, 0)), blk, blk], out_specs=[blk] * 4,
        out_shape=[jax.ShapeDtypeStruct((r, c), F32)] * 4, compiler_params=_params(("parallel",)),
    )(w, parts, m, v)


def _ada_w_update(act_t, dmod, w, m, v):
    r, c = w.shape
    nb = act_t.shape[1]
    tr, tc = 256, 1024

    def body(a_ref, d_ref, w_ref, m_ref, v_ref, g_ref, dl_ref, nm_ref, nv_ref):
        a, dm = a_ref[...], d_ref[...]
        g = a[:, 0:1] * dm[0:1, :]
        for b in range(1, nb):
            g = g + a[:, b:b + 1] * dm[b:b + 1, :]
        delta, nm, nv = _adam_math(w_ref[...], g, m_ref[...], v_ref[...])
        g_ref[...], dl_ref[...], nm_ref[...], nv_ref[...] = g, delta, nm, nv

    blk = pl.BlockSpec((tr, tc), lambda i, j: (i, j))
    return pl.pallas_call(
        body, name="ada_w_update", grid=(r // tr, c // tc),
        in_specs=[pl.BlockSpec((tr, nb), lambda i, j: (i, 0)), pl.BlockSpec((nb, tc), lambda i, j: (0, j)),
                  blk, blk, blk],
        out_specs=[blk] * 4, out_shape=[jax.ShapeDtypeStruct((r, c), F32)] * 4,
        compiler_params=_params(("parallel", "parallel")),
    )(act_t, dmod, w, m, v)


WEIGHTS = ['ada_w', 'ada_b', 'norm1_gain', 'norm2_gain', 'final_gain', 'w_in', 'w_out', 's5_lambda_re',
           's5_lambda_im', 's5_log_step', 's5_b_re', 's5_b_im', 's5_c_re', 's5_c_im', 's5_d', 's5_w_glu',
           's5_b_glu', 'rk_shift_prev', 'rk_shift_next', 'rk_w0', 'rk_w_up', 'rk_a0', 'rk_a_up', 'rk_g_up',
           'rk_k_k', 'rk_k_a', 'rk_r_k', 'rk_ln_gain', 'rk_ln_bias', 'ffn_w1', 'ffn_w2']
BIG_SHARDED = ['w_in', 'w_out', 's5_w_glu', 'ffn_w1', 'ffn_w2']
FFN_SHARDED = ['ffn_w1', 'ffn_w2']
RK_SHARDED = ['rk_w0', 'rk_a0', 'rk_w_up', 'rk_a_up', 'rk_g_up']
REPLICATED = ['ada_b', 'norm1_gain', 'norm2_gain', 'final_gain', 's5_lambda_re', 's5_lambda_im', 's5_log_step',
              's5_b_re', 's5_b_im', 's5_c_re', 's5_c_im', 's5_d', 's5_b_glu', 'rk_shift_prev', 'rk_shift_next',
              'rk_k_k', 'rk_k_a', 'rk_r_k', 'rk_ln_gain', 'rk_ln_bias']
PACK_COLS = 1024
N_CHIPS = 4
RK_ROWS = 420
RK_ROWS_PAD = 432


def _pack_rows(arrays, cols):
    return jnp.concatenate([a.reshape(-1, cols) for a in arrays], axis=0)


def _pack_flat(arrays):
    flat = jnp.concatenate([a.reshape(-1) for a in arrays])
    rows = -(-flat.shape[0] // PACK_COLS)
    return jnp.pad(flat, (0, rows * PACK_COLS - flat.shape[0])).reshape(rows, PACK_COLS)


def _unpack_flat(packed, like):
    flat, out, pos = packed.reshape(-1), [], 0
    for a in like:
        out.append(flat[pos:pos + a.size].reshape(a.shape))
        pos += a.size
    return out


def _cols_to_chips(full, n_rows):
    return jnp.transpose(full.reshape(n_rows, N_CHIPS, -1), (1, 0, 2))


def _chips_to_cols(parts):
    return jnp.transpose(parts, (1, 0, 2)).reshape(parts.shape[1], -1)


def kernel(x, c, ada_w, ada_b, norm1_gain, norm2_gain, final_gain, w_in, w_out, s5_lambda_re, s5_lambda_im, s5_log_step, s5_b_re, s5_b_im, s5_c_re, s5_c_im, s5_d, s5_w_glu, s5_b_glu, rk_shift_prev, rk_shift_next, rk_w0, rk_w_up, rk_a0, rk_a_up, rk_g_up, rk_k_k, rk_k_a, rk_r_k, rk_ln_gain, rk_ln_bias, ffn_w1, ffn_w2, loss_target, m_ada_w, m_ada_b, m_norm1_gain, m_norm2_gain, m_final_gain, m_w_in, m_w_out, m_s5_lambda_re, m_s5_lambda_im, m_s5_log_step, m_s5_b_re, m_s5_b_im, m_s5_c_re, m_s5_c_im, m_s5_d, m_s5_w_glu, m_s5_b_glu, m_rk_shift_prev, m_rk_shift_next, m_rk_w0, m_rk_w_up, m_rk_a0, m_rk_a_up, m_rk_g_up, m_rk_k_k, m_rk_k_a, m_rk_r_k, m_rk_ln_gain, m_rk_ln_bias, m_ffn_w1, m_ffn_w2, v_ada_w, v_ada_b, v_norm1_gain, v_norm2_gain, v_final_gain, v_w_in, v_w_out, v_s5_lambda_re, v_s5_lambda_im, v_s5_log_step, v_s5_b_re, v_s5_b_im, v_s5_c_re, v_s5_c_im, v_s5_d, v_s5_w_glu, v_s5_b_glu, v_rk_shift_prev, v_rk_shift_next, v_rk_w0, v_rk_w_up, v_rk_a0, v_rk_a_up, v_rk_g_up, v_rk_k_k, v_rk_k_a, v_rk_r_k, v_rk_ln_gain, v_rk_ln_bias, v_ffn_w1, v_ffn_w2):
    given = dict(locals())
    w = {n: given[n] for n in WEIGHTS}
    m = {n: given["m_" + n] for n in WEIGHTS}
    v = {n: given["v_" + n] for n in WEIGHTS}
    mx, my, mc = lax.axis_index("x"), lax.axis_index("y"), lax.axis_index("c")
    chip = 2 * mx + my
    dev = 2 * chip + mc
    xt, target = x[0], loss_target[0]

    def rk_rows(d):
        rows = _pack_rows([d[n] for n in RK_SHARDED], 256)
        return jnp.pad(rows, ((0, RK_ROWS_PAD - rows.shape[0]), (0, 0)))

    (c_all,) = _exchange("gather_c", [c], ALL_PEERS, 8)
    early = [n for n in BIG_SHARDED if n not in FFN_SHARDED]
    gathered = _gather_halves("gather_w", [w[n][0].astype(BF16) for n in early] + [rk_rows(w)])
    full = dict(zip(early, gathered[:len(early)]))
    rk_full = gathered[len(early)]

    (act,) = _rowwise("ada_act", lambda q: (q * _sigmoid(q),), [c_all.reshape(8, D_MODEL)], [], [(D_MODEL, F32)], 8)
    n_mod_cols = N_MOD * D_MODEL // N_CHIPS
    bias = jnp.broadcast_to(lax.dynamic_slice(ada_b, (0, chip * n_mod_cols), (1, n_mod_cols)), (8, n_mod_cols))
    mod_shard = _matmul("ada_fwd", act, ada_w[0], epilogue=_add_epilogue, extras=(bias,))
    (mod_parts,) = _exchange("gather_mod", [mod_shard], CHIP_PEERS, N_CHIPS)
    mod_all = _chips_to_cols(mod_parts)
    mod_mine = lax.dynamic_slice(mod_all, (dev, 0), (1, N_MOD * D_MODEL))
    mod = [mod_mine[:, i * D_MODEL:(i + 1) * D_MODEL] for i in range(N_MOD)]

    def rk_piece(lo, hi, lead):
        return _chips_to_cols(rk_full[:, lo:hi]).reshape(lead + (RK_WIDTH,))

    zeros = jnp.zeros((LORA, RK_WIDTH), F32)
    w_up, a_up = rk_piece(4, 132, (2, LORA)), rk_piece(132, 260, (2, LORA))
    wt = {
        "norm1_gain": norm1_gain, "norm2_gain": norm2_gain, "final_gain": final_gain.reshape(1, D_MODEL),
        "w_in": jnp.pad(_chips_to_cols(full["w_in"]), ((0, 0), (0, PROJ_PAD - PROJ))),
        "w_out": full["w_out"].reshape(D_MODEL, D_MODEL),
        "s5_w_glu": full["s5_w_glu"].reshape(S5_WIDTH, S5_WIDTH),
        "mu_prev": jnp.pad(rk_shift_prev, ((0, 0), (0, RK_PAD - RK_IN))),
        "mu_next": jnp.pad(rk_shift_next, ((0, 0), (0, RK_PAD - RK_IN))),
        "lam_re": [s5_lambda_re[0, d].reshape(S5_CH, 1) for d in range(2)],
        "lam_im": [s5_lambda_im[0, d].reshape(S5_CH, 1) for d in range(2)],
        "log_step": [jnp.repeat(s5_log_step[0, d], S5_STATE).reshape(S5_CH, 1) for d in range(2)],
        "b_re": s5_b_re.reshape(S5_CH, S5_GROUP), "b_im": s5_b_im.reshape(S5_CH, S5_GROUP),
        "c_re": s5_c_re[0], "c_im": s5_c_im[0],
        "s5_d": s5_d, "s5_b_glu": s5_b_glu,
        "w0": list(rk_piece(0, 2, (2,))[:, None, :]), "a0": list(rk_piece(2, 4, (2,))[:, None, :]),
        "w_up": [jnp.concatenate([w_up[0], zeros]), jnp.concatenate([zeros, w_up[1]])],
        "a_up": [jnp.concatenate([a_up[0], zeros]), jnp.concatenate([zeros, a_up[1]])],
        "g_up": jnp.pad(rk_piece(260, 420, (GATE_LORA,)), ((0, GATE_PAD - GATE_LORA), (0, 0))),
        "k_k": rk_k_k, "k_a": rk_k_a, "r_k": rk_r_k.reshape(1, RK_WIDTH),
        "ln_gain": rk_ln_gain, "ln_bias": rk_ln_bias,
    }

    def chip_sums(tag, names, pieces):
        sums = []
        for n, piece, other in zip(names, pieces, _send_other_half("swap_halves_" + tag, pieces)):
            sums.append(_pair_sum("pair_" + n, piece, other, F32 if n == "rk" else BF16))
        return sums

    def ffn_reduce(g_w1, g_w2):
        return chip_sums("ffn", FFN_SHARDED, [_cols_to_chips(g_w1, D_MODEL), g_w2.reshape(N_CHIPS, -1, D_MODEL)])

    ffn_shards = [w[n][0].astype(BF16) for n in FFN_SHARDED]
    loss_rows, grad_x, g = _local_step(xt, target, mod, wt, ffn_shards=ffn_shards, ffn_reduce=ffn_reduce)
    loss = lax.psum(jnp.sum(loss_rows), ("x", "y", "c"))

    big_grads = {
        "w_in": _cols_to_chips(g["w_in"][:, :PROJ], D_MODEL),
        "w_out": g["w_out"].reshape(N_CHIPS, -1, D_MODEL),
        "s5_w_glu": g["s5_w_glu"].reshape(N_CHIPS, -1, S5_WIDTH),
    }
    rk_grads = jnp.concatenate([
        _cols_to_chips(jnp.concatenate(g["w0"]), 2), _cols_to_chips(jnp.concatenate(g["a0"]), 2),
        _cols_to_chips(jnp.concatenate([g["w_up"][0][:LORA], g["w_up"][1][LORA:]]), 2 * LORA),
        _cols_to_chips(jnp.concatenate([g["a_up"][0][:LORA], g["a_up"][1][LORA:]]), 2 * LORA),
        _cols_to_chips(g["g_up"][:GATE_LORA], GATE_LORA),
        jnp.zeros((N_CHIPS, RK_ROWS_PAD - RK_ROWS, 256), F32)], axis=1)
    late = [n for n in BIG_SHARDED if n not in FFN_SHARDED]
    arrived = _exchange("scatter_grads", chip_sums("late", late + ["rk"], [big_grads[n] for n in late] + [rk_grads]),
                        CHIP_PEERS, N_CHIPS, scatter=True)
    names = late + ["rk"] + FFN_SHARDED
    half_sums = [_sum_parts("sum_" + n, a) for n, a in zip(names, arrived + g["ffn_arrived"])]
    pairs = dict(zip(names, [p.reshape(1, 2 * p.shape[1], p.shape[2])
                             for p in _exchange("swap_sums", half_sums, CORE_PEER, 2)]))

    out = {}
    for n in BIG_SHARDED:
        res = _adamw("adamw_" + n, w[n][0], pairs[n], m[n][0], v[n][0])
        out[n] = [r[None] for r in res]
    rk_res = _adamw("adamw_rk", rk_rows(w), pairs["rk"], rk_rows(m), rk_rows(v))
    for q in range(4):
        pieces, pos = [], 0
        for n in RK_SHARDED:
            rows = w[n].size // 256
            pieces.append(rk_res[q][pos:pos + rows].reshape(w[n].shape))
            pos += rows
        for n, piece in zip(RK_SHARDED, pieces):
            out.setdefault(n, []).append(piece)

    local_small = {
        "ada_b": jnp.concatenate(g["mod"], axis=1),
        "norm1_gain": g["norm1_gain"], "norm2_gain": g["norm2_gain"], "final_gain": g["final_gain"],
        "s5_lambda_re": jnp.concatenate(g["lam_re"]), "s5_lambda_im": jnp.concatenate(g["lam_im"]),
        "s5_log_step": jnp.concatenate([q.reshape(S5_GROUPS, S5_STATE).sum(axis=1) for q in g["log_step"]]),
        "s5_b_re": g["b_re"], "s5_b_im": g["b_im"], "s5_c_re": g["c_re"], "s5_c_im": g["c_im"],
        "s5_d": g["s5_d"], "s5_b_glu": g["s5_b_glu"],
        "rk_shift_prev": g["mu_prev"][:, :RK_IN], "rk_shift_next": g["mu_next"][:, :RK_IN],
        "rk_k_k": g["k_k"], "rk_k_a": g["k_a"], "rk_r_k": g["r_k"],
        "rk_ln_gain": g["ln_gain"], "rk_ln_bias": g["ln_bias"],
    }
    (small_all,) = _exchange("gather_small", [_pack_flat([local_small[n] for n in REPLICATED])], ALL_PEERS, 8)
    small_res = _adamw("adamw_small", _pack_flat([w[n] for n in REPLICATED]), small_all,
                       _pack_flat([m[n] for n in REPLICATED]), _pack_flat([v[n] for n in REPLICATED]))
    for q in range(4):
        for n, piece in zip(REPLICATED, _unpack_flat(small_res[q], [w[n] for n in REPLICATED])):
            out.setdefault(n, []).append(piece)

    mod_rows = N_MOD * D_MODEL // PACK_COLS
    dmod_all = small_all[:, :mod_rows].reshape(8, N_MOD * D_MODEL)
    dmod = lax.dynamic_slice(dmod_all, (0, chip * n_mod_cols), (8, n_mod_cols))
    res = _ada_w_update(act.T, dmod, ada_w[0], m_ada_w[0], v_ada_w[0])
    out["ada_w"] = [r[None] for r in res]

    return (loss, grad_x[None], *[out[n][0] for n in WEIGHTS], *[out[n][1] for n in WEIGHTS],
            *[out[n][2] for n in WEIGHTS], *[out[n][3] for n in WEIGHTS])
```

```python
import functools
import math

import jax
import jax.numpy as jnp
from jax import lax
from jax.experimental import pallas as pl
from jax.experimental.pallas import tpu as pltpu

F32 = jnp.float32
BF16 = jnp.bfloat16

D_MODEL = 2048
S5_WIDTH = 1024
S5_GROUP = 16
S5_GROUPS = 64
S5_STATE = 64
S5_CH = S5_GROUPS * S5_STATE
S5_BLK = 256
RK_WIDTH = 1024
RK_HEAD = 64
RK_HEADS = 16
LORA = 64
GATE_LORA = 160
GATE_PAD = 256
RK_IN = 3488
RK_PAD = 3584
PROJ = 4512
PROJ_PAD = 4608
FFN = 8192
N_MOD = 6
NORM_EPS = 1e-6
GN_EPS = 64e-5
L2_EPS = 1e-12
RK_CHUNK = 64
RK_PASSES = {"solve": 3, "kt": 3, "s0": 1, "akk_v": 1, "ark_v": 1, "arb_u": 1, "state": 3}
LW_SCALE = math.exp(-0.5)
ADAM_LR, ADAM_B1, ADAM_B2, ADAM_EPS, ADAM_WD, ADAM_STEP = 0.001, 0.9, 0.999, 1e-08, 0.01, 10
VMEM_LIMIT = 56 * 1024 * 1024
HI = lax.Precision.HIGHEST


def _params(sem=None):
    return pltpu.CompilerParams(dimension_semantics=sem, vmem_limit_bytes=VMEM_LIMIT)


def _full(a):
    nd = a.ndim
    return pl.BlockSpec(a.shape, lambda *_: (0,) * nd)


@jax.custom_vjp
def _bdot(a, b):
    return jnp.dot(a.astype(BF16), b.astype(BF16), preferred_element_type=F32)


def _bdot_fwd(a, b):
    return _bdot(a, b), (a, b)


def _bdot_bwd(res, g):
    a, b = res
    gb = g.astype(BF16)
    da = lax.dot_general(gb, b.astype(BF16), (((1,), (1,)), ((), ())), preferred_element_type=F32)
    db = lax.dot_general(a.astype(BF16), gb, (((0,), (0,)), ((), ())), preferred_element_type=F32)
    return da, db


_bdot.defvjp(_bdot_fwd, _bdot_bwd)


@jax.custom_vjp
def _seg_dot(x, ind, ind_t):
    hi = x.astype(BF16)
    lo = (x - hi.astype(F32)).astype(BF16)
    both = jnp.dot(jnp.concatenate([hi, lo], axis=0), ind.astype(BF16), preferred_element_type=F32)
    return both[:x.shape[0]] + both[x.shape[0]:]


_seg_dot.defvjp(lambda x, ind, ind_t: (_seg_dot(x, ind, ind_t), (ind, ind_t)),
                lambda res, g: (_seg_dot(g, res[1], res[0]), jnp.zeros_like(res[0]), jnp.zeros_like(res[1])))


def _sigmoid(z):
    return 1.0 / (1.0 + jnp.exp(-z))


def _gelu(y):
    return 0.5 * y * (1.0 + jnp.tanh(0.7978845608028654 * (y + 0.044715 * (y * y * y))))


def _rms(x):
    return x * lax.rsqrt(jnp.mean(x * x, axis=-1, keepdims=True) + NORM_EPS)


def _tile(n, prefs):
    for t in prefs:
        if n % t == 0:
            return t
    return n


def _matmul(name, a, b, ta=False, tb=False, epilogue=None, extras=(), out_dtypes=(F32,), chips=None):
    m = a.shape[1] if ta else a.shape[0]
    k = a.shape[0] if ta else a.shape[1]
    if chips == "b":
        assert not tb and b.shape[1] == k
        n = N_CHIPS * b.shape[2]
    elif chips == "b_t":
        assert tb and N_CHIPS * b.shape[2] == k
        n = b.shape[1]
    else:
        n = b.shape[0] if tb else b.shape[1]
        assert k == (b.shape[1] if tb else b.shape[0]), (a.shape, b.shape, ta, tb)
    split = N_CHIPS if chips in ("b", "out") else 1
    tm = _tile(m, (1024, 512, 256, 128))
    tn = _tile(n // split, (1024, 768, 512, 256, 128))
    tk = k // N_CHIPS if chips == "b_t" else _tile(k, (2048, 1024, 512, 256, 128))
    nk = k // tk
    per = n // split // tn
    n_ex, n_out = len(extras), len(out_dtypes)
    dims = (((0 if ta else 1,), (1 if tb else 0,)), ((), ()))

    def body(a_ref, b_ref, *rest):
        ex_refs, out_refs, acc = rest[:n_ex], rest[n_ex:n_ex + n_out], rest[-1]
        kk = pl.program_id(3)

        @pl.when(kk == 0)
        def _():
            acc[...] = jnp.zeros_like(acc)

        acc[...] += lax.dot_general(a_ref[...].astype(BF16), b_ref[...].astype(BF16), dims,
                                    preferred_element_type=F32)

        @pl.when(kk == nk - 1)
        def _():
            res = acc[...]
            outs = epilogue(res, *[e[...] for e in ex_refs]) if epilogue is not None else (res,)
            for o_ref, val in zip(out_refs, outs):
                o_ref[...] = val.astype(o_ref.dtype)

    if ta:
        a_spec = pl.BlockSpec((tk, tm), lambda i, c, j, q: (q, i))
    else:
        a_spec = pl.BlockSpec((tm, tk), lambda i, c, j, q: (i, q))
    if chips == "b":
        b_spec = pl.BlockSpec((None, tk, tn), lambda i, c, j, q: (c, q, j))
    elif chips == "b_t":
        b_spec = pl.BlockSpec((None, tn, tk), lambda i, c, j, q: (q, j, 0))
    elif tb:
        b_spec = pl.BlockSpec((tn, tk), lambda i, c, j, q: (c * per + j, q))
    else:
        b_spec = pl.BlockSpec((tk, tn), lambda i, c, j, q: (q, c * per + j))
    mn_spec = pl.BlockSpec((tm, tn), lambda i, c, j, q: (i, c * per + j))
    if chips == "out":
        out_spec = pl.BlockSpec((None, tm, tn), lambda i, c, j, q: (c, i, j))
        out_shape = [jax.ShapeDtypeStruct((N_CHIPS, m, n // N_CHIPS), dt) for dt in out_dtypes]
    else:
        out_spec, out_shape = mn_spec, [jax.ShapeDtypeStruct((m, n), dt) for dt in out_dtypes]
    outs = pl.pallas_call(
        body, name=name, grid=(m // tm, split, per, nk),
        in_specs=[a_spec, b_spec] + [mn_spec] * n_ex,
        out_specs=[out_spec] * n_out, out_shape=out_shape,
        scratch_shapes=[pltpu.VMEM((tm, tn), F32)],
        compiler_params=_params(("parallel", "parallel", "parallel", "arbitrary")),
    )(a, b, *extras)
    return outs[0] if n_out == 1 else outs


def _row_spec(a, tm):
    return pl.BlockSpec((tm, a.shape[1]), lambda i: (i, 0))


def _rowwise(name, fn, rows, params, outs, tm):
    t = rows[0].shape[0]
    tm = min(tm, t)
    n_r, n_p = len(rows), len(params)

    def body(*refs):
        vals = [r[...] for r in refs[:n_r + n_p]]
        res = fn(*vals)
        for o_ref, val in zip(refs[n_r + n_p:], res):
            o_ref[...] = val.astype(o_ref.dtype)

    res = pl.pallas_call(
        body, name=name, grid=(t // tm,),
        in_specs=[_row_spec(r, tm) for r in rows] + [_full(p) for p in params],
        out_specs=[pl.BlockSpec((tm, n), lambda i: (i, 0)) for n, _ in outs],
        out_shape=[jax.ShapeDtypeStruct((t, n), dt) for n, dt in outs],
        compiler_params=_params(("parallel",)),
    )(*rows, *params)
    return res


def _rowwise_vjp(name, fn, rows, params, cts, row_grads, param_grads, tm, consts=(), addends=None,
                 emit=(), row_grad_dtypes=None):
    t = rows[0].shape[0]
    tm = min(tm, t)
    addends = addends or {}
    n_r, n_p, n_c = len(rows), len(params), len(consts)
    ct_flat = [c for group in cts for c in group]
    add_list = [addends[q] for q in sorted(addends)]
    n_ct, n_add = len(ct_flat), len(add_list)
    row_grad_dtypes = row_grad_dtypes or [F32] * len(row_grads)

    def body(*refs):
        pos = 0
        row_v = [r[...].astype(F32) for r in refs[pos:pos + n_r]]; pos += n_r
        par_v = [r[...].astype(F32) for r in refs[pos:pos + n_p]]; pos += n_p
        con_v = [r[...] for r in refs[pos:pos + n_c]]; pos += n_c
        ct_v = [r[...].astype(F32) for r in refs[pos:pos + n_ct]]; pos += n_ct
        add_v = [r[...] for r in refs[pos:pos + n_add]]; pos += n_add
        emit_refs = refs[pos:pos + len(emit)]; pos += len(emit)
        rg_refs = refs[pos:pos + len(row_grads)]; pos += len(row_grads)
        pg_refs = refs[pos:pos + len(param_grads)]

        def diff_fn(*dargs):
            rv, pv = list(row_v), list(par_v)
            for q, i in enumerate(row_grads):
                rv[i] = dargs[q]
            for q, j in enumerate(param_grads):
                pv[j] = dargs[len(row_grads) + q]
            return fn(*rv, *pv, *con_v)

        prim = [row_v[i] for i in row_grads] + [par_v[j] for j in param_grads]
        res, vjp = jax.vjp(diff_fn, *prim)
        ct_vals, q = [], 0
        for o, group in zip(res, cts):
            tot = jnp.zeros_like(o)
            for _ in group:
                tot = tot + ct_v[q]
                q += 1
            ct_vals.append(tot)
        grads = vjp(tuple(ct_vals))
        for e_ref, idx in zip(emit_refs, emit):
            e_ref[...] = res[idx].astype(e_ref.dtype)
        add_pos = {p: q for q, p in enumerate(sorted(addends))}
        for q, g_ref in enumerate(rg_refs):
            g = grads[q]
            if q in add_pos:
                g = g + add_v[add_pos[q]]
            g_ref[...] = g.astype(g_ref.dtype)

        @pl.when(pl.program_id(0) == 0)
        def _():
            for g_ref in pg_refs:
                g_ref[...] = jnp.zeros_like(g_ref)

        for q, g_ref in enumerate(pg_refs):
            g_ref[...] += grads[len(row_grads) + q]

    emit_shapes = []
    if emit:
        probe = jax.eval_shape(lambda *a: fn(*a), *[jax.ShapeDtypeStruct((tm, r.shape[1]), F32) for r in rows],
                               *[jax.ShapeDtypeStruct(p.shape, p.dtype) for p in params],
                               *[jax.ShapeDtypeStruct(c.shape, c.dtype) for c in consts])
        emit_shapes = [probe[idx].shape[1] for idx in emit]
    out_specs = ([pl.BlockSpec((tm, n), lambda i: (i, 0)) for n in emit_shapes]
                 + [_row_spec(rows[i], tm) for i in row_grads]
                 + [_full(params[j]) for j in param_grads])
    out_shape = ([jax.ShapeDtypeStruct((t, n), F32) for n in emit_shapes]
                 + [jax.ShapeDtypeStruct(rows[i].shape, dt) for i, dt in zip(row_grads, row_grad_dtypes)]
                 + [jax.ShapeDtypeStruct(params[j].shape, F32) for j in param_grads])
    return pl.pallas_call(
        body, name=name, grid=(t // tm,),
        in_specs=([_row_spec(r, tm) for r in rows] + [_full(p) for p in params] + [_full(c) for c in consts]
                  + [_row_spec(c, tm) for c in ct_flat] + [_row_spec(a, tm) for a in add_list]),
        out_specs=out_specs, out_shape=out_shape,
        compiler_params=_params(("arbitrary",)),
    )(*rows, *params, *consts, *ct_flat, *add_list)


def _norm_mod_fn(x, gain, scale, shift):
    return (_rms(x) * gain * (1.0 + scale) + shift,)


def _resid_norm_mod_fn(x, mixed, gate, gain, scale, shift):
    x1 = x + gate * mixed
    return x1, _rms(x1) * gain * (1.0 + scale) + shift


def _loss_fn(x1, ffn, target, gate, gain):
    y = _rms(x1 + gate * ffn) * gain
    err = y - target
    return (0.5 * jnp.mean(err * err, axis=-1, keepdims=True),)


def _s5_out_fn(ylin, u, d_skip, w_glu, b_glu):
    z = _gelu(ylin + d_skip * u)
    return (z * _sigmoid(_bdot(z, w_glu) + b_glu),)


def _rk_pre_fn(k, wdn, adn, gdn, w0_0, w0_1, wup_0, wup_1, a0_0, a0_1, aup_0, aup_1, g_up, k_k, k_a, seg, seg_t):
    kkr = k * k_k
    inv = 1.0 / jnp.sqrt(jnp.maximum(_seg_dot(kkr * kkr, seg, seg_t), L2_EPS * L2_EPS))
    kk = kkr * _seg_dot(inv, seg_t, seg)
    tw = jnp.tanh(wdn)
    lws, kds, acts = [], [], []
    for w0, wup, a0, aup in ((w0_0, wup_0, a0_0, aup_0), (w0_1, wup_1, a0_1, aup_1)):
        lws.append(-LW_SCALE * _sigmoid(w0 + _bdot(tw, wup)))
        act = _sigmoid(a0 + _bdot(adn, aup))
        acts.append(act)
        kds.append(k * (1.0 + (act - 1.0) * k_a))
    gate = _bdot(_sigmoid(gdn), g_up)
    return (kk, lws[0], lws[1], kds[0], kds[1], acts[0], acts[1], gate)


def _rk_post_fn(y0, y1, r, v, kd0, kd1, gate, ln_gain, ln_bias, r_k, seg, seg_t):
    y = y0 + y1
    mu = _seg_dot(_seg_dot(y, seg, seg_t) * (1.0 / RK_HEAD), seg_t, seg)
    yc = y - mu
    var = _seg_dot(yc * yc, seg, seg_t) * (1.0 / RK_HEAD)
    yn = yc * _seg_dot(lax.rsqrt(var + GN_EPS), seg_t, seg) * ln_gain + ln_bias
    bonus = _seg_dot(_seg_dot(r * (kd0 + kd1) * r_k, seg, seg_t), seg_t, seg)
    return ((yn + bonus * v) * gate,)


def _s5_prep_fn(lr0, li0, ls0, lr1, li1, ls1, b_re, b_im):
    outs = []
    for lam_re, lam_im, ls in ((lr0, li0, ls0), (lr1, li1, ls1)):
        step = jnp.exp(ls)
        mag = jnp.exp(lam_re * step)
        lbar_re = mag * jnp.cos(lam_im * step)
        lbar_im = mag * jnp.sin(lam_im * step)
        den = lam_re * lam_re + lam_im * lam_im
        nr = lbar_re - 1.0
        coef_re = (nr * lam_re + lbar_im * lam_im) / den
        coef_im = (lbar_im * lam_re - nr * lam_im) / den
        outs += [lbar_re, lbar_im, coef_re * b_re - coef_im * b_im, coef_re * b_im + coef_im * b_re]
    return tuple(outs)


def _shift_rows(x, down):
    t = x.shape[0]
    rows = lax.broadcasted_iota(jnp.int32, x.shape, 0)
    if down:
        return jnp.where(rows >= 1, pltpu.roll(x, 1, 0), 0.0)
    return jnp.where(rows < t - 1, pltpu.roll(x, t - 1, 0), 0.0)


def _token_shift(p, mu_prev, mu_next):
    t, n = p.shape

    def body(p_ref, mp_ref, mn_ref, o_ref):
        x = p_ref[...]
        o_ref[...] = x + mp_ref[...] * (_shift_rows(x, True) - x) + mn_ref[...] * (_shift_rows(x, False) - x)

    col = pl.BlockSpec((t, 128), lambda j: (0, j))
    par = pl.BlockSpec((1, 128), lambda j: (0, j))
    return pl.pallas_call(
        body, name="token_shift", grid=(n // 128,), in_specs=[col, par, par], out_specs=col,
        out_shape=jax.ShapeDtypeStruct((t, n), F32), compiler_params=_params(("parallel",)),
    )(p, mu_prev, mu_next)


def _token_shift_bwd(p, mu_prev, mu_next, dps):
    t, n = p.shape

    def body(p_ref, mp_ref, mn_ref, d_ref, dp_ref, dmp_ref, dmn_ref):
        x, d, mp, mn = p_ref[...], d_ref[...], mp_ref[...], mn_ref[...]
        dp_ref[...] = d * (1.0 - mp - mn) + _shift_rows(d * mp, False) + _shift_rows(d * mn, True)
        dmp_ref[...] = jnp.sum(d * (_shift_rows(x, True) - x), axis=0, keepdims=True)
        dmn_ref[...] = jnp.sum(d * (_shift_rows(x, False) - x), axis=0, keepdims=True)

    col = pl.BlockSpec((t, 128), lambda j: (0, j))
    par = pl.BlockSpec((1, 128), lambda j: (0, j))
    return pl.pallas_call(
        body, name="token_shift_bwd", grid=(n // 128,), in_specs=[col, par, par, col],
        out_specs=[col, par, par],
        out_shape=[jax.ShapeDtypeStruct((t, n), F32), jax.ShapeDtypeStruct((1, n), F32),
                   jax.ShapeDtypeStruct((1, n), F32)],
        compiler_params=_params(("parallel",)),
    )(p, mu_prev, mu_next, dps)


N_SEG = 32
S5_BLOCKS = 32
S5_PER_IN = 4


def _scan_in_place(sr_ref, si_ref, ar, ai, carry_ref, reverse):
    seg_len = sr_ref.shape[0] // N_SEG
    ng = N_SEG // 8

    def rows(i, grp):
        first = (seg_len - 1 - i if reverse else i) * N_SEG + 8 * grp
        return pl.ds(pl.multiple_of(first, 8), 8)

    zero = jnp.zeros((8, 128), F32)
    one = jnp.ones((8, 128), F32)

    def local(i, c):
        pr, pi = c[-2:]
        out = []
        for grp in range(ng):
            sr, si = c[2 * grp], c[2 * grp + 1]
            nr = ar * sr - ai * si + sr_ref[rows(i, grp), :]
            ni = ar * si + ai * sr + si_ref[rows(i, grp), :]
            sr_ref[rows(i, grp), :] = nr
            si_ref[rows(i, grp), :] = ni
            out += [nr, ni]
        return tuple(out) + (ar * pr - ai * pi, ar * pi + ai * pr)

    ends = lax.fori_loop(0, seg_len, local, (zero,) * (2 * ng) + (one, zero))
    qr, qi = ends[-2][0:1], ends[-1][0:1]
    order = list(range(N_SEG - 1, -1, -1)) if reverse else list(range(N_SEG))
    cr = jnp.zeros((1, 128), F32)
    ci = jnp.zeros((1, 128), F32)
    for j in order:
        carry_ref[j:j + 1, :] = cr
        carry_ref[N_SEG + j:N_SEG + j + 1, :] = ci
        grp, sub = divmod(j, 8)
        lr, li = ends[2 * grp][sub:sub + 1], ends[2 * grp + 1][sub:sub + 1]
        cr, ci = lr + qr * cr - qi * ci, li + qr * ci + qi * cr
    carries = [(carry_ref[8 * grp:8 * grp + 8, :], carry_ref[N_SEG + 8 * grp:N_SEG + 8 * grp + 8, :])
               for grp in range(ng)]

    def fix(i, c):
        pr, pi = c
        npr, npi = ar * pr - ai * pi, ar * pi + ai * pr
        for grp in range(ng):
            cr8, ci8 = carries[grp]
            sr_ref[rows(i, grp), :] = sr_ref[rows(i, grp), :] + npr * cr8 - npi * ci8
            si_ref[rows(i, grp), :] = si_ref[rows(i, grp), :] + npr * ci8 + npi * cr8
        return npr, npi

    lax.fori_loop(0, seg_len, fix, (one, zero))


def _interleave(x):
    t, c = x.shape
    return jnp.transpose(x.reshape(N_SEG, t // N_SEG, c), (1, 0, 2)).reshape(t, c)


def _deinterleave(x):
    t, c = x.shape
    return jnp.transpose(x.reshape(t // N_SEG, N_SEG, c), (1, 0, 2)).reshape(t, c)


def _step_neighbour(s, earlier):
    t = s.shape[0]
    rows = lax.broadcasted_iota(jnp.int32, s.shape, 0)
    if earlier:
        return jnp.where(rows >= N_SEG, pltpu.roll(s, N_SEG, 0),
                         jnp.where(rows >= 1, pltpu.roll(s, N_SEG + 1, 0), 0.0))
    return jnp.where(rows < t - N_SEG, pltpu.roll(s, t - N_SEG, 0),
                     jnp.where(rows < t - 1, pltpu.roll(s, t - N_SEG - 1, 0), 0.0))


def _dot_bf16(a, b, dims=(((1,), (0,)), ((), ()))):
    return lax.dot_general(a.astype(BF16), b.astype(BF16), dims, preferred_element_type=F32)


NT_DIMS = (((1,), (1,)), ((), ()))
TN_DIMS = (((0,), (0,)), ((), ()))


def _s5_specs(t):
    blk = pl.BlockSpec((None, t, 128), lambda i, q: (S5_PER_IN * i + q, 0, 0))
    mat = pl.BlockSpec((None, 128, 128), lambda i, q: (S5_PER_IN * i + q, 0, 0))
    vec = pl.BlockSpec((None, 1, 128), lambda i, q: (S5_PER_IN * i + q, 0, 0))
    chan = pl.BlockSpec((t, 128), lambda i, q: (0, i))
    return blk, mat, vec, chan


S5_GRID = (S5_BLOCKS // S5_PER_IN, S5_PER_IN)


def _s5_forward(name, u, b_re, b_im, l_re, l_im, reverse, other=None, c_re=None, c_im_neg=None):
    t = u.shape[0]
    project = other is not None
    blk, mat, vec, chan = _s5_specs(t)

    def body(*refs):
        u_ref, br_ref, bi_ref, lr_ref, li_ref = refs[:5]
        if project:
            or_ref, oi_ref, cr_ref, ci_ref, sr_ref, si_ref, y_ref, carry_ref = refs[5:]
        else:
            sr_ref, si_ref, carry_ref = refs[5:]
        uv = u_ref[...]
        sr_ref[...] = _dot_bf16(uv, br_ref[...])
        si_ref[...] = _dot_bf16(uv, bi_ref[...])
        ar = jnp.broadcast_to(lr_ref[...], (8, 128))
        ai = jnp.broadcast_to(li_ref[...], (8, 128))
        _scan_in_place(sr_ref, si_ref, ar, ai, carry_ref, reverse)
        if project:
            y = (_dot_bf16(sr_ref[...] + or_ref[...], cr_ref[...])
                 + _dot_bf16(si_ref[...] + oi_ref[...], ci_ref[...]))

            @pl.when(pl.program_id(1) == 0)
            def _():
                y_ref[...] = y

            @pl.when(pl.program_id(1) != 0)
            def _():
                y_ref[...] += y

    state = jax.ShapeDtypeStruct((S5_BLOCKS, t, 128), F32)
    ins = [u, b_re, b_im, l_re, l_im] + ([other[0], other[1], c_re, c_im_neg] if project else [])
    in_specs = [chan, mat, mat, vec, vec] + ([blk, blk, mat, mat] if project else [])
    return pl.pallas_call(
        body, name=name, grid=S5_GRID, in_specs=in_specs,
        out_specs=[blk, blk] + ([chan] if project else []),
        out_shape=[state, state] + ([jax.ShapeDtypeStruct((t, S5_WIDTH), F32)] if project else []),
        scratch_shapes=[pltpu.VMEM((2 * N_SEG, 128), F32)],
        compiler_params=_params(("arbitrary", "arbitrary")),
    )(*ins)


def _s5_backward(name, dy, u, du_in, states, other, b_re, b_im, c_re, c_im_neg, l_re, l_im, reverse):
    t = u.shape[0]
    with_c = other is not None
    blk, mat, vec, chan = _s5_specs(t)

    def body(*refs):
        dy_ref, u_ref, du_in_ref, sr_ref, si_ref = refs[:5]
        pos = 5
        if with_c:
            or_ref, oi_ref = refs[5:7]
            pos = 7
        br_ref, bi_ref, cr_ref, ci_ref, lr_ref, li_ref = refs[pos:pos + 6]
        outs = refs[pos + 6:]
        du_ref, dbr_ref, dbi_ref, dlr_ref, dli_ref = outs[:5]
        lam_r, lam_i, carry_ref = outs[-3:]
        dyv, uv = dy_ref[...], u_ref[...]
        lam_r[...] = _dot_bf16(dyv, cr_ref[...], NT_DIMS)
        lam_i[...] = _dot_bf16(dyv, ci_ref[...], NT_DIMS)
        ar = jnp.broadcast_to(lr_ref[...], (8, 128))
        ai = -jnp.broadcast_to(li_ref[...], (8, 128))
        _scan_in_place(lam_r, lam_i, ar, ai, carry_ref, not reverse)
        lr, li = lam_r[...], lam_i[...]
        pr, pi = _step_neighbour(sr_ref[...], not reverse), _step_neighbour(si_ref[...], not reverse)
        dlr_ref[...] = jnp.sum(lr * pr + li * pi, axis=0, keepdims=True)
        dli_ref[...] = jnp.sum(li * pr - lr * pi, axis=0, keepdims=True)
        dbr_ref[...] = _dot_bf16(uv, lr, TN_DIMS)
        dbi_ref[...] = _dot_bf16(uv, li, TN_DIMS)
        du = _dot_bf16(lr, br_ref[...], NT_DIMS) + _dot_bf16(li, bi_ref[...], NT_DIMS)

        @pl.when(pl.program_id(1) == 0)
        def _():
            du_ref[...] = du_in_ref[...] + du

        @pl.when(pl.program_id(1) != 0)
        def _():
            du_ref[...] += du

        if with_c:
            dcr_ref, dci_ref = outs[5:7]
            dcr_ref[...] = _dot_bf16(sr_ref[...] + or_ref[...], dyv, TN_DIMS)
            dci_ref[...] = _dot_bf16(si_ref[...] + oi_ref[...], dyv, TN_DIMS)

    mats = jax.ShapeDtypeStruct((S5_BLOCKS, 128, 128), F32)
    vecs = jax.ShapeDtypeStruct((S5_BLOCKS, 1, 128), F32)
    ins = [dy, u, du_in, states[0], states[1]] + ([other[0], other[1]] if with_c else [])
    ins += [b_re, b_im, c_re, c_im_neg, l_re, l_im]
    in_specs = [chan, chan, chan, blk, blk] + ([blk, blk] if with_c else []) + [mat] * 4 + [vec] * 2
    return pl.pallas_call(
        body, name=name, grid=S5_GRID, in_specs=in_specs,
        out_specs=[chan, mat, mat, vec, vec] + ([mat, mat] if with_c else []),
        out_shape=[jax.ShapeDtypeStruct((t, S5_WIDTH), F32), mats, mats, vecs, vecs] + ([mats, mats] if with_c else []),
        scratch_shapes=[pltpu.VMEM((t, 128), F32), pltpu.VMEM((t, 128), F32), pltpu.VMEM((2 * N_SEG, 128), F32)],
        compiler_params=_params(("arbitrary", "arbitrary")),
    )(*ins)


def _ein(passes, spec, a, b):
    if passes == 6:
        return jnp.einsum(spec, a, b, precision=HI, preferred_element_type=F32)
    a_hi, b_hi = a.astype(BF16), b.astype(BF16)
    if passes == 1:
        return jnp.einsum(spec, a_hi, b_hi, preferred_element_type=F32)
    a_lo = (a - a_hi.astype(F32)).astype(BF16)
    b_lo = (b - b_hi.astype(F32)).astype(BF16)
    cross = jnp.einsum(spec, a_hi, b_lo, preferred_element_type=F32)
    if spec.startswith('hik'):
        m = a.shape[1]
        stacked = jnp.einsum(spec, jnp.concatenate([a_hi, a_lo], axis=1), b_hi, preferred_element_type=F32)
        return stacked[:, :m] + stacked[:, m:] + cross
    return (jnp.einsum(spec, a_hi, b_hi, preferred_element_type=F32) + cross
            + jnp.einsum(spec, a_lo, b_hi, preferred_element_type=F32))


@jax.custom_vjp
def _tri_mm(tri, tri_t, z):
    return jnp.einsum('hik,hkj->hij', tri, z, precision=HI, preferred_element_type=F32)


def _tri_mm_bwd(res, g):
    tri, tri_t = res
    return jnp.zeros_like(tri), jnp.zeros_like(tri_t), _tri_mm(tri_t, tri, g)


_tri_mm.defvjp(lambda tri, tri_t, z: (_tri_mm(tri, tri_t, z), (tri, tri_t)), _tri_mm_bwd)


def _chunk_cumsum(lw, incl, incl_t):
    shape = (lw.shape[0],) + incl.shape
    return _tri_mm(jnp.broadcast_to(incl.astype(F32), shape), jnp.broadcast_to(incl_t.astype(F32), shape), lw)


@functools.partial(jax.custom_vjp, nondiff_argnums=(0,))
def _bmm(p, a, b):
    return _ein(p, 'hik,hkj->hij', a, b)


@functools.partial(jax.custom_vjp, nondiff_argnums=(0,))
def _bmm_nt(p, a, b):
    return _ein(p, 'hik,hjk->hij', a, b)


@functools.partial(jax.custom_vjp, nondiff_argnums=(0,))
def _bmm_tn(p, a, b):
    return _ein(p, 'hki,hkj->hij', a, b)


_bmm.defvjp(lambda p, a, b: (_bmm(p, a, b), (a, b)),
            lambda p, res, g: (_bmm_nt(p, g, res[1]), _bmm_tn(p, res[0], g)))
_bmm_nt.defvjp(lambda p, a, b: (_bmm_nt(p, a, b), (a, b)),
               lambda p, res, g: (_bmm(p, g, res[1]), _bmm_tn(p, g, res[0])))
_bmm_tn.defvjp(lambda p, a, b: (_bmm_tn(p, a, b), (a, b)),
               lambda p, res, g: (_bmm_nt(p, res[1], g), _bmm(p, res[0], g)))


@jax.custom_vjp
def _split_rows(x):
    c = x.shape[1] // 2
    return x[:, :c], x[:, c:]


_split_rows.defvjp(lambda x: (_split_rows(x), None), lambda _, g: (jnp.concatenate(g, axis=1),))


def _stack_rows(a, b):
    return jnp.concatenate([a, b], axis=1)


def _rk_chunk(s0, r, lw, k, v, kk, a, reverse):
    h, c, n = r.shape
    row = lax.broadcasted_iota(jnp.int32, (c, c), 0)
    col = lax.broadcasted_iota(jnp.int32, (c, c), 1)
    incl = (row <= col) if reverse else (row >= col)
    strict = (row < col) if reverse else (row > col)
    cum = _chunk_cumsum(lw, incl, (row >= col) if reverse else (row <= col))
    g_in = jnp.exp(cum)
    g_inv = jnp.exp(-cum)
    kap = kk * jnp.exp(cum - lw)
    beta = kk * a * g_inv
    kt = k * g_inv
    rt = r * g_in
    p, ps = RK_PASSES, RK_PASSES["solve"]
    both = _stack_rows(kap, rt)
    kap_beta, rt_beta = _split_rows(_bmm_nt(ps, both, beta))
    kap_kt, rt_kt = _split_rows(_bmm_nt(p["kt"], both, kt))
    kap_s0, rt_s0 = _split_rows(_bmm_nt(p["s0"], both, s0))
    l_mat = jnp.where(strict, kap_beta, 0.0)
    rhs = kap_s0 + _bmm(p["akk_v"], jnp.where(strict, kap_kt, 0.0), v)
    x = -l_mat
    inv = jnp.where(row == col, 1.0, 0.0) + x
    power = _bmm(ps, x, x)
    span = 2
    while 2 * span < c:
        step, power = _split_rows(_bmm(ps, _stack_rows(inv, power), power))
        inv = inv + step
        span *= 2
    inv = inv + _bmm(ps, inv, power)
    u = _bmm(ps, inv, rhs)
    y = (rt_s0 + _bmm(p["ark_v"], jnp.where(incl, rt_kt, 0.0), v)
         - _bmm(p["arb_u"], jnp.where(incl, rt_beta, 0.0), u))
    s1 = ((s0 + _bmm_tn(p["state"], _stack_rows(v, -u), _stack_rows(kt, beta)))
          * jnp.exp(jnp.sum(lw, axis=1, keepdims=True)))
    return y, s1


class _Plan:
    def __init__(self, arrays, out_shape, sems, start, wait, finish):
        self.arrays, self.out_shape, self.sems = list(arrays), list(out_shape), list(sems)
        self.start, self.wait, self.finish = start, wait, finish


_NO_PLAN = _Plan([], [], [], lambda *_: None, lambda *_: None, lambda outs: [])


def _join_plans(plans):
    def cut(seq, sizes):
        out, pos = [], 0
        for s in sizes:
            out.append(seq[pos:pos + s])
            pos += s
        return out

    n_arr, n_sem = [len(p.arrays) for p in plans], [len(p.sems) for p in plans]

    def run(which):
        def go(in_refs, out_refs, sems):
            for p, i, o, s in zip(plans, cut(in_refs, n_arr), cut(out_refs, n_arr), cut(sems, n_sem)):
                getattr(p, which)(i, o, s)
        return go

    return _Plan([a for p in plans for a in p.arrays], [s for p in plans for s in p.out_shape],
                 [s for p in plans for s in p.sems], run("start"), run("wait"),
                 lambda outs: [p.finish(o) for p, o in zip(plans, cut(outs, n_arr))])


def _split_heads(x):
    return jnp.stack([x[:, RK_HEAD * i:RK_HEAD * (i + 1)] for i in range(RK_HEADS)], axis=0)


def _store_heads(ref, x):
    for i in range(RK_HEADS):
        ref[:, RK_HEAD * i:RK_HEAD * (i + 1)] = x[i]


def _rk_core_fwd(name, r, lw, k, v, kk, a, reverse, chunk, hosted=None):
    t = r.shape[0]
    h, n = RK_HEADS, RK_HEAD
    nc = t // chunk

    def idx(i):
        return nc - 1 - i if reverse else i

    hosted = hosted or _NO_PLAN
    nh = len(hosted.arrays)

    def body(r_ref, lw_ref, k_ref, v_ref, kk_ref, a_ref, *rest):
        host_in, (y_ref, ck_ref), host_out = rest[:nh], rest[nh:nh + 2], rest[nh + 2:2 * nh + 2]
        s_ref, sems = rest[2 * nh + 2], rest[2 * nh + 3:]

        @pl.when(pl.program_id(0) == 0)
        def _():
            s_ref[...] = jnp.zeros_like(s_ref)
            hosted.start(host_in, host_out, sems)

        s0 = s_ref[...]
        ck_ref[0] = s0
        ops = [_split_heads(ref[...]) for ref in (r_ref, lw_ref, k_ref, v_ref, kk_ref, a_ref)]
        y, s1 = _rk_chunk(s0, *ops, reverse)
        _store_heads(y_ref, y)
        s_ref[...] = s1

        @pl.when(pl.program_id(0) == nc - 1)
        def _():
            hosted.wait(host_in, host_out, sems)

    blk = pl.BlockSpec((chunk, RK_WIDTH), lambda i: (idx(i), 0))
    any_spec = pl.BlockSpec(memory_space=pl.ANY)
    res = pl.pallas_call(
        body, name=name, grid=(nc,), in_specs=[blk] * 6 + [any_spec] * nh,
        out_specs=[blk, pl.BlockSpec((1, h, n, n), lambda i: (idx(i), 0, 0, 0))] + [any_spec] * nh,
        out_shape=[jax.ShapeDtypeStruct((t, RK_WIDTH), F32), jax.ShapeDtypeStruct((nc, h, n, n), F32)]
        + hosted.out_shape,
        scratch_shapes=[pltpu.VMEM((h, n, n), F32)] + hosted.sems,
        compiler_params=_params(("arbitrary",)),
    )(r, lw, k, v, kk, a, *hosted.arrays)
    return res[0], res[1], hosted.finish(res[2:])


def _rk_core_bwd(name, r, lw, k, v, kk, a, ck, dy, reverse, chunk, hosted=None):
    t = r.shape[0]
    h, n = RK_HEADS, RK_HEAD
    nc = t // chunk
    hosted = hosted or _NO_PLAN
    nh = len(hosted.arrays)

    def idx(i):
        return i if reverse else nc - 1 - i

    def body(r_ref, lw_ref, k_ref, v_ref, kk_ref, a_ref, ck_ref, dy_ref, *rest):
        host_in, out_refs, host_out = rest[:nh], rest[nh:nh + 6], rest[nh + 6:2 * nh + 6]
        ds_ref, sems = rest[2 * nh + 6], rest[2 * nh + 7:]

        @pl.when(pl.program_id(0) == 0)
        def _():
            ds_ref[...] = jnp.zeros_like(ds_ref)
            hosted.start(host_in, host_out, sems)

        fn = functools.partial(_rk_chunk, reverse=reverse)
        ops = [_split_heads(ref[...]) for ref in (r_ref, lw_ref, k_ref, v_ref, kk_ref, a_ref)]
        _, vjp = jax.vjp(fn, ck_ref[0], *ops)
        grads = vjp((_split_heads(dy_ref[...]), ds_ref[...]))
        ds_ref[...] = grads[0]
        for o_ref, g in zip(out_refs, grads[1:]):
            _store_heads(o_ref, g)

        @pl.when(pl.program_id(0) == nc - 1)
        def _():
            hosted.wait(host_in, host_out, sems)

    blk = pl.BlockSpec((chunk, RK_WIDTH), lambda i: (idx(i), 0))
    any_spec = pl.BlockSpec(memory_space=pl.ANY)
    res = pl.pallas_call(
        body, name=name, grid=(nc,),
        in_specs=[blk] * 6 + [pl.BlockSpec((1, h, n, n), lambda i: (idx(i), 0, 0, 0)), blk] + [any_spec] * nh,
        out_specs=[blk] * 6 + [any_spec] * nh,
        out_shape=[jax.ShapeDtypeStruct((t, RK_WIDTH), F32)] * 6 + hosted.out_shape,
        scratch_shapes=[pltpu.VMEM((h, n, n), F32)] + hosted.sems,
        compiler_params=_params(("arbitrary",)),
    )(r, lw, k, v, kk, a, ck, dy, *hosted.arrays)
    return res[:6], hosted.finish(res[6:])


def _s5_band_place():
    return jax.nn.one_hot(jnp.arange(S5_BLOCKS) % S5_PER_IN, S5_PER_IN, dtype=F32)


def _s5_in_blocks(bbar):
    b = jnp.transpose(bbar.reshape(S5_BLOCKS, 2, S5_STATE, S5_GROUP), (0, 1, 3, 2))
    band = jnp.einsum('jghp,gk->jghkp', b, jnp.eye(2, dtype=F32)).reshape(S5_BLOCKS, 32, 128)
    return jnp.einsum('jrc,jq->jqrc', band, _s5_band_place()).reshape(S5_BLOCKS, 128, 128)


def _s5_in_unblock(mats):
    band = jnp.einsum('jqrc,jq->jrc', mats.reshape(S5_BLOCKS, S5_PER_IN, 32, 128), _s5_band_place())
    diag = jnp.einsum('jghgp->jghp', band.reshape(S5_BLOCKS, 2, S5_GROUP, 2, S5_STATE))
    return jnp.transpose(diag, (0, 1, 3, 2)).reshape(S5_CH, S5_GROUP)


def _s5_out_blocks(c):
    ct = jnp.transpose(c.reshape(S5_BLOCKS, 2, S5_GROUP, S5_STATE), (0, 1, 3, 2))
    band = jnp.einsum('jgph,gk->jgpkh', ct, jnp.eye(2, dtype=F32)).reshape(S5_BLOCKS, 128, 32)
    return jnp.einsum('jrc,jq->jrqc', band, _s5_band_place()).reshape(S5_BLOCKS, 128, 128)


def _s5_out_unblock(mats):
    band = jnp.einsum('jrqc,jq->jrc', mats.reshape(S5_BLOCKS, 128, S5_PER_IN, 32), _s5_band_place())
    diag = jnp.einsum('jgpgh->jgph', band.reshape(S5_BLOCKS, 2, S5_STATE, 2, S5_GROUP))
    return jnp.transpose(diag, (0, 1, 3, 2)).reshape(S5_GROUPS, S5_GROUP, S5_STATE)


def _head_indicator():
    ch = lax.broadcasted_iota(jnp.int32, (RK_WIDTH, 128), 0) // RK_HEAD
    hd = lax.broadcasted_iota(jnp.int32, (RK_WIDTH, 128), 1)
    seg = (ch == hd).astype(F32)
    return seg, seg.T


def _add_epilogue(acc, e):
    return (acc + e,)


def _local_step(x, target, mod, wt, chunk=RK_CHUNK, ffn_shards=None, ffn_reduce=None):
    t = x.shape[0]
    wt = dict(wt)
    sh1, sc1, gt1, sh2, sc2, gt2 = mod
    seg, seg_t = _head_indicator()
    g = {}

    (h1,) = _rowwise("norm1", _norm_mod_fn, [x], [wt["norm1_gain"], sc1, sh1], [(D_MODEL, BF16)], 256)
    proj = _matmul("proj", h1, wt["w_in"])
    u, p = proj[:, :S5_WIDTH], proj[:, S5_WIDTH:]
    ps = _token_shift(p, wt["mu_prev"], wt["mu_next"])
    r, k, v = ps[:, :1024], ps[:, 1024:2048], ps[:, 2048:3072]
    wdn, adn, gdn = ps[:, 3072:3200], ps[:, 3200:3328], ps[:, 3328:RK_PAD]

    prep_rows = [wt["lam_re"][0], wt["lam_im"][0], wt["log_step"][0], wt["lam_re"][1], wt["lam_im"][1],
                 wt["log_step"][1], wt["b_re"], wt["b_im"]]
    col1, col16 = (1, F32), (S5_GROUP, F32)
    prep = _rowwise("s5_prep", _s5_prep_fn, prep_rows, [], [col1, col1, col16, col16] * 2, 512)
    lbar = [tuple(prep[4 * d + q].reshape(S5_BLOCKS, 1, 128) for q in range(2)) for d in range(2)]
    b_blk = [tuple(_s5_in_blocks(prep[4 * d + 2 + q]) for q in range(2)) for d in range(2)]
    c_blk = (_s5_out_blocks(wt["c_re"]), -_s5_out_blocks(wt["c_im"]))
    u_il = _interleave(u)
    state0 = _s5_forward("s5_fwd0", u_il, *b_blk[0], *lbar[0], reverse=False)
    s1_re, s1_im, ylin_il = _s5_forward("s5_fwd1", u_il, *b_blk[1], *lbar[1], reverse=True, other=state0,
                                        c_re=c_blk[0], c_im_neg=c_blk[1])
    ylin = _deinterleave(ylin_il)
    states = [tuple(state0), (s1_re, s1_im)]
    s5_par = [wt["s5_d"], wt["s5_w_glu"], wt["s5_b_glu"]]
    (o_s5,) = _rowwise("s5_out", _s5_out_fn, [ylin, u], s5_par, [(S5_WIDTH, BF16)], 256)

    pre_par = [wt["w0"][0], wt["w0"][1], wt["w_up"][0], wt["w_up"][1], wt["a0"][0], wt["a0"][1],
               wt["a_up"][0], wt["a_up"][1], wt["g_up"], wt["k_k"], wt["k_a"]]
    pre = _rowwise("rk_pre", _rk_pre_fn, [k, wdn, adn, gdn], pre_par + [seg, seg_t], [(RK_WIDTH, F32)] * 8, 256)
    kk, lw, kd, act, gate = pre[0], pre[1:3], pre[3:5], pre[5:7], pre[7]
    core_in, ys, cks = [], [], []
    for d in range(2):
        ops = (r, lw[d], kd[d], v, kk, act[d])
        plan = _gather_halves_plan([ffn_shards[d]]) if ffn_shards is not None else None
        y, ck, gathered = _rk_core_fwd(f"rk_core{d}", *ops, reverse=(d == 1), chunk=min(chunk, t), hosted=plan)
        if gathered:
            wt["ffn_w1" if d == 0 else "ffn_w2"] = gathered[0] if d == 0 else gathered[0].reshape(FFN, D_MODEL)
        core_in.append(ops)
        ys.append(y)
        cks.append(ck)
    post_rows = [ys[0], ys[1], r, v, kd[0], kd[1], gate]
    post_par = [wt["ln_gain"], wt["ln_bias"], wt["r_k"]]
    (o_rk,) = _rowwise("rk_post", _rk_post_fn, post_rows, post_par + [seg, seg_t], [(RK_WIDTH, BF16)], 256)

    o = jnp.concatenate([o_s5, o_rk], axis=1)
    mixed = _matmul("mix_out", o, wt["w_out"])
    n2_par = [gt1, wt["norm2_gain"], sc2, sh2]
    x1, h2 = _rowwise("norm2", _resid_norm_mod_fn, [x, mixed], n2_par, [(D_MODEL, F32), (D_MODEL, BF16)], 256)
    f1, hid = _matmul("ffn1", h2, wt["ffn_w1"], out_dtypes=(F32, BF16), chips="b",
                      epilogue=lambda acc: (acc, jnp.square(jnp.maximum(acc, 0.0))))
    ffn = _matmul("ffn2", hid, wt["ffn_w2"])

    ones = jnp.ones((t, 1), F32)
    loss_rows, dx1, dffn, g_gt2, g["final_gain"] = _rowwise_vjp(
        "loss", _loss_fn, [x1, ffn, target], [gt2, wt["final_gain"]], [[ones]], [0, 1], [0, 1], 256, emit=(0,),
        row_grad_dtypes=[F32, BF16])
    df1 = _matmul("ffn2_dx", dffn, wt["ffn_w2"], tb=True, extras=(f1,), out_dtypes=(BF16,),
                  epilogue=lambda acc, f: (acc * (2.0 * jnp.maximum(f, 0.0)),))
    g["ffn_w2"] = _matmul("ffn2_dw", hid, dffn, ta=True)
    dh2 = _matmul("ffn1_dx", df1, wt["ffn_w1"], tb=True, chips="b_t")
    g["ffn_w1"] = _matmul("ffn1_dw", h2, df1, ta=True, chips="out")
    dx_a, dmixed, g_gt1, g["norm2_gain"], g_sc2, g_sh2 = _rowwise_vjp(
        "norm2_bwd", _resid_norm_mod_fn, [x, mixed], n2_par, [[dx1], [dh2]], [0, 1], [0, 1, 2, 3], 256,
        row_grad_dtypes=[F32, BF16])
    do = _matmul("mix_out_dx", dmixed, wt["w_out"], tb=True)
    g["w_out"] = _matmul("mix_out_dw", o, dmixed, ta=True)
    do_s5, do_rk = do[:, :S5_WIDTH], do[:, S5_WIDTH:]

    dylin, du, g["s5_d"], g["s5_w_glu"], g["s5_b_glu"] = _rowwise_vjp(
        "s5_out_bwd", _s5_out_fn, [ylin, u], s5_par, [[do_s5]], [0, 1], [0, 1, 2], 256)
    prep_cts = []
    dylin_il, du_il = _interleave(dylin), _interleave(du)
    for d in range(2):
        res = _s5_backward(f"s5_bwd{d}", dylin_il, u_il, du_il, states[d], states[1] if d == 0 else None,
                           *b_blk[d], *c_blk, *lbar[d], reverse=(d == 1))
        du_il, db_re, db_im, dl_re, dl_im = res[:5]
        if d == 0:
            g["c_re"], g["c_im"] = _s5_out_unblock(res[5]), -_s5_out_unblock(res[6])
        prep_cts += [[dl_re.reshape(S5_CH, 1)], [dl_im.reshape(S5_CH, 1)], [_s5_in_unblock(db_re)],
                     [_s5_in_unblock(db_im)]]
    du = _deinterleave(du_il)
    pg = _rowwise_vjp("s5_prep_bwd", _s5_prep_fn, prep_rows, [], prep_cts, list(range(8)), [], 512)
    g["lam_re"], g["lam_im"], g["log_step"] = (pg[0], pg[3]), (pg[1], pg[4]), (pg[2], pg[5])
    g["b_re"], g["b_im"] = pg[6], pg[7]

    pb = _rowwise_vjp("rk_post_bwd", _rk_post_fn, post_rows, post_par, [[do_rk]], [0, 2, 3, 4, 5, 6], [0, 1, 2],
                      128, consts=[seg, seg_t])
    dy, dr_b, dv_b, dkd_b, dgate = pb[0], pb[1], pb[2], pb[3:5], pb[5]
    g["ln_gain"], g["ln_bias"], g["r_k"] = pb[6], pb[7], pb[8]
    cg = []
    for d in range(2):
        plan = None
        if d == 0 and ffn_reduce is not None:
            plan = _exchange_plan(ffn_reduce(g.pop("ffn_w1"), g.pop("ffn_w2")), CHIP_PEERS, N_CHIPS, scatter=True)
        grads, arrived = _rk_core_bwd(f"rk_core{d}_bwd", *core_in[d], cks[d], dy, reverse=(d == 1),
                                      chunk=min(chunk, t), hosted=plan)
        if arrived:
            g["ffn_arrived"] = arrived
        cg.append(grads)
    pre_cts = [[cg[0][4], cg[1][4]], [cg[0][1]], [cg[1][1]], [cg[0][2], dkd_b[0]], [cg[1][2], dkd_b[1]],
               [cg[0][5]], [cg[1][5]], [dgate]]
    qb = _rowwise_vjp("rk_pre_bwd", _rk_pre_fn, [k, wdn, adn, gdn], pre_par, pre_cts, [0, 1, 2, 3],
                      list(range(11)), 128, consts=[seg, seg_t])
    dk, dwdn, dadn, dgdn = qb[:4]
    g["w0"], g["w_up"], g["a0"], g["a_up"] = (qb[4], qb[5]), (qb[6], qb[7]), (qb[8], qb[9]), (qb[10], qb[11])
    g["g_up"], g["k_k"], g["k_a"] = qb[12], qb[13], qb[14]
    dr, dv = _rowwise("rk_sum", lambda a, b, c, e, f, h: (a + b + c, e + f + h),
                      [cg[0][0], cg[1][0], dr_b, cg[0][3], cg[1][3], dv_b], [], [(RK_WIDTH, F32)] * 2, 256)
    dps = jnp.concatenate([dr, dk, dv, dwdn, dadn, dgdn], axis=1)
    dp, g["mu_prev"], g["mu_next"] = _token_shift_bwd(p, wt["mu_prev"], wt["mu_next"], dps)

    dproj = jnp.concatenate([du, dp], axis=1).astype(BF16)
    dh1 = _matmul("proj_dx", dproj, wt["w_in"], tb=True)
    g["w_in"] = _matmul("proj_dw", h1, dproj, ta=True)
    grad_x, g["norm1_gain"], g_sc1, g_sh1 = _rowwise_vjp(
        "norm1_bwd", _norm_mod_fn, [x], [wt["norm1_gain"], sc1, sh1], [[dh1]], [0], [0, 1, 2], 256,
        addends={0: dx_a})
    g["mod"] = [g_sh1, g_sc1, g_gt1, g_sh2, g_sc2, g_gt2]
    return loss_rows, grad_x, g


CHIP_PEERS = ((1, 0, 0), (0, 1, 0), (1, 1, 0))
ALL_PEERS = ((0, 0, 1), (0, 1, 0), (0, 1, 1), (1, 0, 0), (1, 0, 1), (1, 1, 0), (1, 1, 1))
CORE_PEER = ((0, 0, 1),)


def _exchange(name, arrays, peers, n_slots, scatter=False):
    return _run_plan(name, _exchange_plan(arrays, peers, n_slots, scatter))


def _run_plan(name, plan):
    na = len(plan.arrays)

    def body(*refs):
        plan.start(refs[:na], refs[na:2 * na], refs[2 * na:])
        plan.wait(refs[:na], refs[na:2 * na], refs[2 * na:])

    any_spec = pl.BlockSpec(memory_space=pl.ANY)
    return plan.finish(pl.pallas_call(
        body, name=name, in_specs=[any_spec] * na, out_specs=[any_spec] * na, out_shape=plan.out_shape,
        scratch_shapes=plan.sems,
    )(*plan.arrays))


def _exchange_plan(arrays, peers, n_slots, scatter=False):
    na, nm = len(arrays), len(peers)

    def ident(px, py, pc):
        return {8: 4 * px + 2 * py + pc, 4: 2 * px + py, 2: pc}[n_slots]

    def copies(in_refs, out_refs, sems):
        send_sems, recv_sems = sems
        x, y, c = lax.axis_index("x"), lax.axis_index("y"), lax.axis_index("c")
        me = ident(x, y, c)
        made = []
        for i in range(na):
            for j, (fx, fy, fc) in enumerate(peers):
                px, py, pc = (1 - x if fx else x), (1 - y if fy else y), (1 - c if fc else c)
                src = in_refs[i].at[ident(px, py, pc)] if scatter else in_refs[i]
                made.append(pltpu.make_async_remote_copy(
                    src_ref=src, dst_ref=out_refs[i].at[me],
                    send_sem=send_sems.at[i * nm + j], recv_sem=recv_sems.at[i * nm + j],
                    device_id=(px, py, pc), device_id_type=pl.DeviceIdType.MESH))
        return made

    def start(in_refs, out_refs, sems):
        for copy in copies(in_refs, out_refs, sems):
            copy.start()

    def wait(in_refs, out_refs, sems):
        for copy in copies(in_refs, out_refs, sems):
            copy.wait()

    def finish(outs):
        me = ident(lax.axis_index("x"), lax.axis_index("y"), lax.axis_index("c"))
        return [lax.dynamic_update_slice_in_dim(
            o, lax.dynamic_index_in_dim(a, me, 0, keepdims=True) if scatter else a[None], me, axis=0)
            for a, o in zip(arrays, outs)]

    out_shape = [jax.ShapeDtypeStruct(((n_slots,) + a.shape[1:]) if scatter else ((n_slots,) + a.shape), a.dtype)
                 for a in arrays]
    sems = [pltpu.SemaphoreType.DMA((na * nm,)), pltpu.SemaphoreType.DMA((na * nm,))]
    return _Plan(arrays, out_shape, sems, start, wait, finish)


def _gather_halves(name, arrays):
    return _run_plan(name, _gather_halves_plan(arrays))


def _gather_halves_plan(arrays):
    na = len(arrays)
    chips = ((1, 0), (0, 1), (1, 1))

    def over_ici(in_refs, out_refs, sems):
        ici_send, ici_recv = sems[:2]
        x, y, c = lax.axis_index("x"), lax.axis_index("y"), lax.axis_index("c")
        made = []
        for i in range(na):
            half = arrays[i].shape[0] // 2
            mine = pl.ds(pl.multiple_of(c * half, 8), half)
            for j, (fx, fy) in enumerate(chips):
                px, py = (1 - x if fx else x), (1 - y if fy else y)
                k = len(chips) * i + j
                made.append([pltpu.make_async_remote_copy(
                    src_ref=in_refs[i].at[mine], dst_ref=out_refs[i].at[chip, mine],
                    send_sem=ici_send.at[k], recv_sem=ici_recv.at[k],
                    device_id=(px, py, c), device_id_type=pl.DeviceIdType.MESH)
                    for chip in (2 * x + y, 2 * px + py)])
        return made

    def start(in_refs, out_refs, sems):
        for outgoing, _ in over_ici(in_refs, out_refs, sems):
            outgoing.start()

    def wait(in_refs, out_refs, sems):
        d2d_send, d2d_recv = sems[2:]
        x, y, c = lax.axis_index("x"), lax.axis_index("y"), lax.axis_index("c")
        pending = []
        ici = over_ici(in_refs, out_refs, sems)
        for i in range(na):
            half = arrays[i].shape[0] // 2
            mine = pl.ds(pl.multiple_of(c * half, 8), half)
            theirs = pl.ds(pl.multiple_of((1 - c) * half, 8), half)
            for j, (fx, fy) in enumerate(chips):
                px, py = (1 - x if fx else x), (1 - y if fy else y)
                k = len(chips) * i + j
                outgoing, landing = ici[k]
                landing.wait_recv()
                landed = out_refs[i].at[2 * px + py, mine]
                passed = pltpu.make_async_remote_copy(
                    src_ref=landed, dst_ref=landed, send_sem=d2d_send.at[k], recv_sem=d2d_recv.at[k],
                    device_id=(x, y, 1 - c), device_id_type=pl.DeviceIdType.MESH)
                passed.start()
                from_sibling = out_refs[i].at[2 * px + py, theirs]
                pending += [outgoing.wait_send, passed.wait_send, pltpu.make_async_remote_copy(
                    src_ref=from_sibling, dst_ref=from_sibling, send_sem=d2d_send.at[k], recv_sem=d2d_recv.at[k],
                    device_id=(x, y, 1 - c), device_id_type=pl.DeviceIdType.MESH).wait_recv]
        for done in pending:
            done()

    def finish(outs):
        me = 2 * lax.axis_index("x") + lax.axis_index("y")
        return [lax.dynamic_update_slice_in_dim(o, a[None], me, axis=0) for a, o in zip(arrays, outs)]

    out_shape = [jax.ShapeDtypeStruct((N_CHIPS,) + a.shape, a.dtype) for a in arrays]
    return _Plan(arrays, out_shape, [pltpu.SemaphoreType.DMA((na * len(chips),))] * 4, start, wait, finish)


def _send_other_half(name, arrays):
    na = len(arrays)

    def body(*refs):
        in_refs, out_refs, send_sems, recv_sems = refs[:na], refs[na:2 * na], refs[-2], refs[-1]
        x, y, c = lax.axis_index("x"), lax.axis_index("y"), lax.axis_index("c")
        copies = []
        for i in range(na):
            half = arrays[i].shape[1] // 2
            theirs = pl.ds(pl.multiple_of((1 - c) * half, 8), half)
            copy = pltpu.make_async_remote_copy(
                src_ref=in_refs[i].at[:, theirs], dst_ref=out_refs[i], send_sem=send_sems.at[i],
                recv_sem=recv_sems.at[i], device_id=(x, y, 1 - c), device_id_type=pl.DeviceIdType.MESH)
            copy.start()
            copies.append(copy)
        for copy in copies:
            copy.wait()

    any_spec = pl.BlockSpec(memory_space=pl.ANY)
    return pl.pallas_call(
        body, name=name, in_specs=[any_spec] * na, out_specs=[any_spec] * na,
        out_shape=[jax.ShapeDtypeStruct((a.shape[0], a.shape[1] // 2, a.shape[2]), a.dtype) for a in arrays],
        scratch_shapes=[pltpu.SemaphoreType.DMA((na,)), pltpu.SemaphoreType.DMA((na,))],
    )(*arrays)


def _adam_math(w, g, m, v):
    m = ADAM_B1 * m + (1.0 - ADAM_B1) * g
    v = ADAM_B2 * v + (1.0 - ADAM_B2) * jnp.square(g)
    m_hat = m / (1.0 - ADAM_B1 ** ADAM_STEP)
    v_hat = v / (1.0 - ADAM_B2 ** ADAM_STEP)
    delta = -ADAM_LR * (m_hat / (jnp.sqrt(v_hat) + ADAM_EPS) + ADAM_WD * w)
    return delta, m, v


WHOLE_BLOCK_BYTES = 2 * 1024 * 1024


def _row_tile(r, c):
    return r if 4 * r * c <= WHOLE_BLOCK_BYTES else _tile(r, (256, 128, 64, 32, 16, 8))


def _sum_parts(name, parts):
    n, r, c = parts.shape
    tr = _row_tile(r, c)

    def body(p_ref, o_ref):
        tot = p_ref[0].astype(F32)
        for i in range(1, n):
            tot = tot + p_ref[i].astype(F32)
        o_ref[...] = tot

    return pl.pallas_call(
        body, name=name, grid=(r // tr,), in_specs=[pl.BlockSpec((n, tr, c), lambda i: (0, i, 0))],
        out_specs=pl.BlockSpec((tr, c), lambda i: (i, 0)), out_shape=jax.ShapeDtypeStruct((r, c), F32),
        compiler_params=_params(("parallel",)),
    )(parts)


def _pair_sum(name, piece, other, dtype):
    n, r, c = piece.shape
    half = r // 2
    tr = _row_tile(half, c)

    def body(lo_ref, hi_ref, other_ref, o_ref):
        own = jnp.where(lax.axis_index("c") == 0, lo_ref[...], hi_ref[...])
        o_ref[...] = (own + other_ref[...]).astype(o_ref.dtype)

    blk = pl.BlockSpec((None, tr, c), lambda j, i: (j, i, 0))
    return pl.pallas_call(
        body, name=name, grid=(n, half // tr),
        in_specs=[pl.BlockSpec((None, None, tr, c), lambda j, i: (j, 0, i, 0)),
                  pl.BlockSpec((None, None, tr, c), lambda j, i: (j, 1, i, 0)), blk],
        out_specs=blk, out_shape=jax.ShapeDtypeStruct((n, half, c), dtype),
        compiler_params=_params(("parallel", "parallel")),
    )(piece.reshape(n, 2, half, c), piece.reshape(n, 2, half, c), other)


def _adamw(name, w, parts, m, v):
    n, r, c = parts.shape
    tr = _row_tile(r, c)

    def body(w_ref, p_ref, m_ref, v_ref, g_ref, d_ref, nm_ref, nv_ref):
        g = p_ref[0]
        for i in range(1, n):
            g = g + p_ref[i]
        delta, nm, nv = _adam_math(w_ref[...], g, m_ref[...], v_ref[...])
        g_ref[...], d_ref[...], nm_ref[...], nv_ref[...] = g, delta, nm, nv

    blk = pl.BlockSpec((tr, c), lambda i: (i, 0))
    return pl.pallas_call(
        body, name=name, grid=(r // tr,),
        in_specs=[blk, pl.BlockSpec((n, tr, c), lambda i: (0, i, 0)), blk, blk], out_specs=[blk] * 4,
        out_shape=[jax.ShapeDtypeStruct((r, c), F32)] * 4, compiler_params=_params(("parallel",)),
    )(w, parts, m, v)


def _ada_w_update(act_t, dmod, w, m, v):
    r, c = w.shape
    nb = act_t.shape[1]
    tr, tc = 256, 1024

    def body(a_ref, d_ref, w_ref, m_ref, v_ref, g_ref, dl_ref, nm_ref, nv_ref):
        a, dm = a_ref[...], d_ref[...]
        g = a[:, 0:1] * dm[0:1, :]
        for b in range(1, nb):
            g = g + a[:, b:b + 1] * dm[b:b + 1, :]
        delta, nm, nv = _adam_math(w_ref[...], g, m_ref[...], v_ref[...])
        g_ref[...], dl_ref[...], nm_ref[...], nv_ref[...] = g, delta, nm, nv

    blk = pl.BlockSpec((tr, tc), lambda i, j: (i, j))
    return pl.pallas_call(
        body, name="ada_w_update", grid=(r // tr, c // tc),
        in_specs=[pl.BlockSpec((tr, nb), lambda i, j: (i, 0)), pl.BlockSpec((nb, tc), lambda i, j: (0, j)),
                  blk, blk, blk],
        out_specs=[blk] * 4, out_shape=[jax.ShapeDtypeStruct((r, c), F32)] * 4,
        compiler_params=_params(("parallel", "parallel")),
    )(act_t, dmod, w, m, v)


WEIGHTS = ['ada_w', 'ada_b', 'norm1_gain', 'norm2_gain', 'final_gain', 'w_in', 'w_out', 's5_lambda_re',
           's5_lambda_im', 's5_log_step', 's5_b_re', 's5_b_im', 's5_c_re', 's5_c_im', 's5_d', 's5_w_glu',
           's5_b_glu', 'rk_shift_prev', 'rk_shift_next', 'rk_w0', 'rk_w_up', 'rk_a0', 'rk_a_up', 'rk_g_up',
           'rk_k_k', 'rk_k_a', 'rk_r_k', 'rk_ln_gain', 'rk_ln_bias', 'ffn_w1', 'ffn_w2']
BIG_SHARDED = ['w_in', 'w_out', 's5_w_glu', 'ffn_w1', 'ffn_w2']
FFN_SHARDED = ['ffn_w1', 'ffn_w2']
RK_SHARDED = ['rk_w0', 'rk_a0', 'rk_w_up', 'rk_a_up', 'rk_g_up']
REPLICATED = ['ada_b', 'norm1_gain', 'norm2_gain', 'final_gain', 's5_lambda_re', 's5_lambda_im', 's5_log_step',
              's5_b_re', 's5_b_im', 's5_c_re', 's5_c_im', 's5_d', 's5_b_glu', 'rk_shift_prev', 'rk_shift_next',
              'rk_k_k', 'rk_k_a', 'rk_r_k', 'rk_ln_gain', 'rk_ln_bias']
PACK_COLS = 1024
N_CHIPS = 4
RK_ROWS = 420
RK_ROWS_PAD = 432


def _pack_rows(arrays, cols):
    return jnp.concatenate([a.reshape(-1, cols) for a in arrays], axis=0)


def _pack_flat(arrays):
    flat = jnp.concatenate([a.reshape(-1) for a in arrays])
    rows = -(-flat.shape[0] // PACK_COLS)
    return jnp.pad(flat, (0, rows * PACK_COLS - flat.shape[0])).reshape(rows, PACK_COLS)


def _unpack_flat(packed, like):
    flat, out, pos = packed.reshape(-1), [], 0
    for a in like:
        out.append(flat[pos:pos + a.size].reshape(a.shape))
        pos += a.size
    return out


def _cols_to_chips(full, n_rows):
    return jnp.transpose(full.reshape(n_rows, N_CHIPS, -1), (1, 0, 2))


def _chips_to_cols(parts):
    return jnp.transpose(parts, (1, 0, 2)).reshape(parts.shape[1], -1)


def kernel(x, c, ada_w, ada_b, norm1_gain, norm2_gain, final_gain, w_in, w_out, s5_lambda_re, s5_lambda_im, s5_log_step, s5_b_re, s5_b_im, s5_c_re, s5_c_im, s5_d, s5_w_glu, s5_b_glu, rk_shift_prev, rk_shift_next, rk_w0, rk_w_up, rk_a0, rk_a_up, rk_g_up, rk_k_k, rk_k_a, rk_r_k, rk_ln_gain, rk_ln_bias, ffn_w1, ffn_w2, loss_target, m_ada_w, m_ada_b, m_norm1_gain, m_norm2_gain, m_final_gain, m_w_in, m_w_out, m_s5_lambda_re, m_s5_lambda_im, m_s5_log_step, m_s5_b_re, m_s5_b_im, m_s5_c_re, m_s5_c_im, m_s5_d, m_s5_w_glu, m_s5_b_glu, m_rk_shift_prev, m_rk_shift_next, m_rk_w0, m_rk_w_up, m_rk_a0, m_rk_a_up, m_rk_g_up, m_rk_k_k, m_rk_k_a, m_rk_r_k, m_rk_ln_gain, m_rk_ln_bias, m_ffn_w1, m_ffn_w2, v_ada_w, v_ada_b, v_norm1_gain, v_norm2_gain, v_final_gain, v_w_in, v_w_out, v_s5_lambda_re, v_s5_lambda_im, v_s5_log_step, v_s5_b_re, v_s5_b_im, v_s5_c_re, v_s5_c_im, v_s5_d, v_s5_w_glu, v_s5_b_glu, v_rk_shift_prev, v_rk_shift_next, v_rk_w0, v_rk_w_up, v_rk_a0, v_rk_a_up, v_rk_g_up, v_rk_k_k, v_rk_k_a, v_rk_r_k, v_rk_ln_gain, v_rk_ln_bias, v_ffn_w1, v_ffn_w2):
    given = dict(locals())
    w = {n: given[n] for n in WEIGHTS}
    m = {n: given["m_" + n] for n in WEIGHTS}
    v = {n: given["v_" + n] for n in WEIGHTS}
    mx, my, mc = lax.axis_index("x"), lax.axis_index("y"), lax.axis_index("c")
    chip = 2 * mx + my
    dev = 2 * chip + mc
    xt, target = x[0], loss_target[0]

    def rk_rows(d):
        rows = _pack_rows([d[n] for n in RK_SHARDED], 256)
        return jnp.pad(rows, ((0, RK_ROWS_PAD - rows.shape[0]), (0, 0)))

    (c_all,) = _exchange("gather_c", [c], ALL_PEERS, 8)
    early = [n for n in BIG_SHARDED if n not in FFN_SHARDED]
    gathered = _gather_halves("gather_w", [w[n][0].astype(BF16) for n in early] + [rk_rows(w)])
    full = dict(zip(early, gathered[:len(early)]))
    rk_full = gathered[len(early)]

    (act,) = _rowwise("ada_act", lambda q: (q * _sigmoid(q),), [c_all.reshape(8, D_MODEL)], [], [(D_MODEL, F32)], 8)
    n_mod_cols = N_MOD * D_MODEL // N_CHIPS
    bias = jnp.broadcast_to(lax.dynamic_slice(ada_b, (0, chip * n_mod_cols), (1, n_mod_cols)), (8, n_mod_cols))
    mod_shard = _matmul("ada_fwd", act, ada_w[0], epilogue=_add_epilogue, extras=(bias,))
    (mod_parts,) = _exchange("gather_mod", [mod_shard], CHIP_PEERS, N_CHIPS)
    mod_all = _chips_to_cols(mod_parts)
    mod_mine = lax.dynamic_slice(mod_all, (dev, 0), (1, N_MOD * D_MODEL))
    mod = [mod_mine[:, i * D_MODEL:(i + 1) * D_MODEL] for i in range(N_MOD)]

    def rk_piece(lo, hi, lead):
        return _chips_to_cols(rk_full[:, lo:hi]).reshape(lead + (RK_WIDTH,))

    zeros = jnp.zeros((LORA, RK_WIDTH), F32)
    w_up, a_up = rk_piece(4, 132, (2, LORA)), rk_piece(132, 260, (2, LORA))
    wt = {
        "norm1_gain": norm1_gain, "norm2_gain": norm2_gain, "final_gain": final_gain.reshape(1, D_MODEL),
        "w_in": jnp.pad(_chips_to_cols(full["w_in"]), ((0, 0), (0, PROJ_PAD - PROJ))),
        "w_out": full["w_out"].reshape(D_MODEL, D_MODEL),
        "s5_w_glu": full["s5_w_glu"].reshape(S5_WIDTH, S5_WIDTH),
        "mu_prev": jnp.pad(rk_shift_prev, ((0, 0), (0, RK_PAD - RK_IN))),
        "mu_next": jnp.pad(rk_shift_next, ((0, 0), (0, RK_PAD - RK_IN))),
        "lam_re": [s5_lambda_re[0, d].reshape(S5_CH, 1) for d in range(2)],
        "lam_im": [s5_lambda_im[0, d].reshape(S5_CH, 1) for d in range(2)],
        "log_step": [jnp.repeat(s5_log_step[0, d], S5_STATE).reshape(S5_CH, 1) for d in range(2)],
        "b_re": s5_b_re.reshape(S5_CH, S5_GROUP), "b_im": s5_b_im.reshape(S5_CH, S5_GROUP),
        "c_re": s5_c_re[0], "c_im": s5_c_im[0],
        "s5_d": s5_d, "s5_b_glu": s5_b_glu,
        "w0": list(rk_piece(0, 2, (2,))[:, None, :]), "a0": list(rk_piece(2, 4, (2,))[:, None, :]),
        "w_up": [jnp.concatenate([w_up[0], zeros]), jnp.concatenate([zeros, w_up[1]])],
        "a_up": [jnp.concatenate([a_up[0], zeros]), jnp.concatenate([zeros, a_up[1]])],
        "g_up": jnp.pad(rk_piece(260, 420, (GATE_LORA,)), ((0, GATE_PAD - GATE_LORA), (0, 0))),
        "k_k": rk_k_k, "k_a": rk_k_a, "r_k": rk_r_k.reshape(1, RK_WIDTH),
        "ln_gain": rk_ln_gain, "ln_bias": rk_ln_bias,
    }

    def chip_sums(tag, names, pieces):
        sums = []
        for n, piece, other in zip(names, pieces, _send_other_half("swap_halves_" + tag, pieces)):
            sums.append(_pair_sum("pair_" + n, piece, other, F32 if n == "rk" else BF16))
        return sums

    def ffn_reduce(g_w1, g_w2):
        return chip_sums("ffn", FFN_SHARDED, [g_w1, g_w2.reshape(N_CHIPS, -1, D_MODEL)])

    ffn_shards = [w[n][0].astype(BF16) for n in FFN_SHARDED]
    loss_rows, grad_x, g = _local_step(xt, target, mod, wt, ffn_shards=ffn_shards, ffn_reduce=ffn_reduce)
    loss = lax.psum(jnp.sum(loss_rows), ("x", "y", "c"))

    big_grads = {
        "w_in": _cols_to_chips(g["w_in"][:, :PROJ], D_MODEL),
        "w_out": g["w_out"].reshape(N_CHIPS, -1, D_MODEL),
        "s5_w_glu": g["s5_w_glu"].reshape(N_CHIPS, -1, S5_WIDTH),
    }
    rk_grads = jnp.concatenate([
        _cols_to_chips(jnp.concatenate(g["w0"]), 2), _cols_to_chips(jnp.concatenate(g["a0"]), 2),
        _cols_to_chips(jnp.concatenate([g["w_up"][0][:LORA], g["w_up"][1][LORA:]]), 2 * LORA),
        _cols_to_chips(jnp.concatenate([g["a_up"][0][:LORA], g["a_up"][1][LORA:]]), 2 * LORA),
        _cols_to_chips(g["g_up"][:GATE_LORA], GATE_LORA),
        jnp.zeros((N_CHIPS, RK_ROWS_PAD - RK_ROWS, 256), F32)], axis=1)
    local_small = {
        "ada_b": jnp.concatenate(g["mod"], axis=1),
        "norm1_gain": g["norm1_gain"], "norm2_gain": g["norm2_gain"], "final_gain": g["final_gain"],
        "s5_lambda_re": jnp.concatenate(g["lam_re"]), "s5_lambda_im": jnp.concatenate(g["lam_im"]),
        "s5_log_step": jnp.concatenate([q.reshape(S5_GROUPS, S5_STATE).sum(axis=1) for q in g["log_step"]]),
        "s5_b_re": g["b_re"], "s5_b_im": g["b_im"], "s5_c_re": g["c_re"], "s5_c_im": g["c_im"],
        "s5_d": g["s5_d"], "s5_b_glu": g["s5_b_glu"],
        "rk_shift_prev": g["mu_prev"][:, :RK_IN], "rk_shift_next": g["mu_next"][:, :RK_IN],
        "rk_k_k": g["k_k"], "rk_k_a": g["k_a"], "rk_r_k": g["r_k"],
        "rk_ln_gain": g["ln_gain"], "rk_ln_bias": g["ln_bias"],
    }
    late = [n for n in BIG_SHARDED if n not in FFN_SHARDED]
    late_sums = chip_sums("late", late + ["rk"], [big_grads[n] for n in late] + [rk_grads])
    arrived, (small_all,) = _run_plan("scatter_grads", _join_plans([
        _exchange_plan(late_sums, CHIP_PEERS, N_CHIPS, scatter=True),
        _exchange_plan([_pack_flat([local_small[n] for n in REPLICATED])], ALL_PEERS, 8)]))
    names = late + ["rk"] + FFN_SHARDED
    half_sums = [_sum_parts("sum_" + n, a) for n, a in zip(names, arrived + g["ffn_arrived"])]
    pairs = dict(zip(names, [p.reshape(1, 2 * p.shape[1], p.shape[2])
                             for p in _exchange("swap_sums", half_sums, CORE_PEER, 2)]))

    out = {}
    for n in BIG_SHARDED:
        res = _adamw("adamw_" + n, w[n][0], pairs[n], m[n][0], v[n][0])
        out[n] = [r[None] for r in res]
    rk_res = _adamw("adamw_rk", rk_rows(w), pairs["rk"], rk_rows(m), rk_rows(v))
    for q in range(4):
        pieces, pos = [], 0
        for n in RK_SHARDED:
            rows = w[n].size // 256
            pieces.append(rk_res[q][pos:pos + rows].reshape(w[n].shape))
            pos += rows
        for n, piece in zip(RK_SHARDED, pieces):
            out.setdefault(n, []).append(piece)

    small_res = _adamw("adamw_small", _pack_flat([w[n] for n in REPLICATED]), small_all,
                       _pack_flat([m[n] for n in REPLICATED]), _pack_flat([v[n] for n in REPLICATED]))
    for q in range(4):
        for n, piece in zip(REPLICATED, _unpack_flat(small_res[q], [w[n] for n in REPLICATED])):
            out.setdefault(n, []).append(piece)

    mod_rows = N_MOD * D_MODEL // PACK_COLS
    dmod_all = small_all[:, :mod_rows].reshape(8, N_MOD * D_MODEL)
    dmod = lax.dynamic_slice(dmod_all, (0, chip * n_mod_cols), (8, n_mod_cols))
    res = _ada_w_update(act.T, dmod, ada_w[0], m_ada_w[0], v_ada_w[0])
    out["ada_w"] = [r[None] for r in res]

    return (loss, grad_x[None], *[out[n][0] for n in WEIGHTS], *[out[n][1] for n in WEIGHTS],
            *[out[n][2] for n in WEIGHTS], *[out[n][3] for n in WEIGHTS])
```

```python
import functools
import math

import jax
import jax.numpy as jnp
from jax import lax
from jax.experimental import pallas as pl
from jax.experimental.pallas import tpu as pltpu

F32 = jnp.float32
BF16 = jnp.bfloat16

D_MODEL = 2048
S5_WIDTH = 1024
S5_GROUP = 16
S5_GROUPS = 64
S5_STATE = 64
S5_CH = S5_GROUPS * S5_STATE
S5_BLK = 256
RK_WIDTH = 1024
RK_HEAD = 64
RK_HEADS = 16
LORA = 64
GATE_LORA = 160
GATE_PAD = 256
RK_IN = 3488
RK_PAD = 3584
PROJ = 4512
PROJ_PAD = 4608
FFN = 8192
N_MOD = 6
NORM_EPS = 1e-6
GN_EPS = 64e-5
L2_EPS = 1e-12
RK_CHUNK = 64
RK_PASSES = {"solve": 3, "kt": 3, "s0": 1, "akk_v": 1, "ark_v": 1, "arb_u": 1, "state": 3}
LW_SCALE = math.exp(-0.5)
ADAM_LR, ADAM_B1, ADAM_B2, ADAM_EPS, ADAM_WD, ADAM_STEP = 0.001, 0.9, 0.999, 1e-08, 0.01, 10
VMEM_LIMIT = 56 * 1024 * 1024
HI = lax.Precision.HIGHEST


def _params(sem=None):
    return pltpu.CompilerParams(dimension_semantics=sem, vmem_limit_bytes=VMEM_LIMIT)


def _full(a):
    nd = a.ndim
    return pl.BlockSpec(a.shape, lambda *_: (0,) * nd)


@jax.custom_vjp
def _bdot(a, b):
    return jnp.dot(a.astype(BF16), b.astype(BF16), preferred_element_type=F32)


def _bdot_fwd(a, b):
    return _bdot(a, b), (a, b)


def _bdot_bwd(res, g):
    a, b = res
    gb = g.astype(BF16)
    da = lax.dot_general(gb, b.astype(BF16), (((1,), (1,)), ((), ())), preferred_element_type=F32)
    db = lax.dot_general(a.astype(BF16), gb, (((0,), (0,)), ((), ())), preferred_element_type=F32)
    return da, db


_bdot.defvjp(_bdot_fwd, _bdot_bwd)


@jax.custom_vjp
def _seg_dot(x, ind, ind_t):
    hi = x.astype(BF16)
    lo = (x - hi.astype(F32)).astype(BF16)
    both = jnp.dot(jnp.concatenate([hi, lo], axis=0), ind.astype(BF16), preferred_element_type=F32)
    return both[:x.shape[0]] + both[x.shape[0]:]


_seg_dot.defvjp(lambda x, ind, ind_t: (_seg_dot(x, ind, ind_t), (ind, ind_t)),
                lambda res, g: (_seg_dot(g, res[1], res[0]), jnp.zeros_like(res[0]), jnp.zeros_like(res[1])))


def _sigmoid(z):
    return 1.0 / (1.0 + jnp.exp(-z))


def _gelu(y):
    return 0.5 * y * (1.0 + jnp.tanh(0.7978845608028654 * (y + 0.044715 * (y * y * y))))


def _rms(x):
    return x * lax.rsqrt(jnp.mean(x * x, axis=-1, keepdims=True) + NORM_EPS)


def _tile(n, prefs):
    for t in prefs:
        if n % t == 0:
            return t
    return n


def _matmul(name, a, b, ta=False, tb=False, epilogue=None, extras=(), out_dtypes=(F32,), chips=None, hosted=None):
    m = a.shape[1] if ta else a.shape[0]
    k = a.shape[0] if ta else a.shape[1]
    if chips == "b":
        assert not tb and b.shape[1] == k
        n = N_CHIPS * b.shape[2]
    elif chips == "b_t":
        assert tb and N_CHIPS * b.shape[2] == k
        n = b.shape[1]
    else:
        n = b.shape[0] if tb else b.shape[1]
        assert k == (b.shape[1] if tb else b.shape[0]), (a.shape, b.shape, ta, tb)
    split = N_CHIPS if chips in ("b", "out") else 1
    tm = _tile(m, (1024, 512, 256, 128))
    tn = _tile(n // split, (1024, 768, 512, 256, 128))
    tk = k // N_CHIPS if chips == "b_t" else _tile(k, (2048, 1024, 512, 256, 128))
    nk = k // tk
    per = n // split // tn
    n_ex, n_out = len(extras), len(out_dtypes)
    dims = (((0 if ta else 1,), (1 if tb else 0,)), ((), ()))

    hosted = hosted or _NO_PLAN
    nh = len(hosted.arrays)
    grid = (m // tm, split, per, nk)

    def body(a_ref, b_ref, *rest):
        ex_refs, host_in = rest[:n_ex], rest[n_ex:n_ex + nh]
        out_refs, host_out = rest[n_ex + nh:n_ex + nh + n_out], rest[n_ex + nh + n_out:n_ex + 2 * nh + n_out]
        acc, sems = rest[n_ex + 2 * nh + n_out], rest[n_ex + 2 * nh + n_out + 1:]
        kk = pl.program_id(3)
        if nh:
            ids = [pl.program_id(d) for d in range(4)]
            first = functools.reduce(jnp.logical_and, [i == 0 for i in ids])
            last = functools.reduce(jnp.logical_and, [i == g - 1 for i, g in zip(ids, grid)])

            @pl.when(first)
            def _():
                hosted.start(host_in, host_out, sems)

        @pl.when(kk == 0)
        def _():
            acc[...] = jnp.zeros_like(acc)

        acc[...] += lax.dot_general(a_ref[...].astype(BF16), b_ref[...].astype(BF16), dims,
                                    preferred_element_type=F32)

        @pl.when(kk == nk - 1)
        def _():
            res = acc[...]
            outs = epilogue(res, *[e[...] for e in ex_refs]) if epilogue is not None else (res,)
            for o_ref, val in zip(out_refs, outs):
                o_ref[...] = val.astype(o_ref.dtype)

        if nh:
            @pl.when(last)
            def _():
                hosted.wait(host_in, host_out, sems)

    if ta:
        a_spec = pl.BlockSpec((tk, tm), lambda i, c, j, q: (q, i))
    else:
        a_spec = pl.BlockSpec((tm, tk), lambda i, c, j, q: (i, q))
    if chips == "b":
        b_spec = pl.BlockSpec((None, tk, tn), lambda i, c, j, q: (c, q, j))
    elif chips == "b_t":
        b_spec = pl.BlockSpec((None, tn, tk), lambda i, c, j, q: (q, j, 0))
    elif tb:
        b_spec = pl.BlockSpec((tn, tk), lambda i, c, j, q: (c * per + j, q))
    else:
        b_spec = pl.BlockSpec((tk, tn), lambda i, c, j, q: (q, c * per + j))
    mn_spec = pl.BlockSpec((tm, tn), lambda i, c, j, q: (i, c * per + j))
    if chips == "out":
        out_spec = pl.BlockSpec((None, tm, tn), lambda i, c, j, q: (c, i, j))
        out_shape = [jax.ShapeDtypeStruct((N_CHIPS, m, n // N_CHIPS), dt) for dt in out_dtypes]
    else:
        out_spec, out_shape = mn_spec, [jax.ShapeDtypeStruct((m, n), dt) for dt in out_dtypes]
    any_spec = pl.BlockSpec(memory_space=pl.ANY)
    order = ("arbitrary",) * 4 if nh else ("parallel", "parallel", "parallel", "arbitrary")
    outs = pl.pallas_call(
        body, name=name, grid=grid,
        in_specs=[a_spec, b_spec] + [mn_spec] * n_ex + [any_spec] * nh,
        out_specs=[out_spec] * n_out + [any_spec] * nh, out_shape=out_shape + hosted.out_shape,
        scratch_shapes=[pltpu.VMEM((tm, tn), F32)] + hosted.sems,
        compiler_params=_params(order),
    )(a, b, *extras, *hosted.arrays)
    res = outs[0] if n_out == 1 else outs[:n_out]
    return (res, hosted.finish(outs[n_out:])) if nh else res


def _row_spec(a, tm):
    return pl.BlockSpec((tm, a.shape[1]), lambda i: (i, 0))


def _rowwise(name, fn, rows, params, outs, tm):
    t = rows[0].shape[0]
    tm = min(tm, t)
    n_r, n_p = len(rows), len(params)

    def body(*refs):
        vals = [r[...] for r in refs[:n_r + n_p]]
        res = fn(*vals)
        for o_ref, val in zip(refs[n_r + n_p:], res):
            o_ref[...] = val.astype(o_ref.dtype)

    res = pl.pallas_call(
        body, name=name, grid=(t // tm,),
        in_specs=[_row_spec(r, tm) for r in rows] + [_full(p) for p in params],
        out_specs=[pl.BlockSpec((tm, n), lambda i: (i, 0)) for n, _ in outs],
        out_shape=[jax.ShapeDtypeStruct((t, n), dt) for n, dt in outs],
        compiler_params=_params(("parallel",)),
    )(*rows, *params)
    return res


def _rowwise_vjp(name, fn, rows, params, cts, row_grads, param_grads, tm, consts=(), addends=None,
                 emit=(), row_grad_dtypes=None):
    t = rows[0].shape[0]
    tm = min(tm, t)
    addends = addends or {}
    n_r, n_p, n_c = len(rows), len(params), len(consts)
    ct_flat = [c for group in cts for c in group]
    add_list = [addends[q] for q in sorted(addends)]
    n_ct, n_add = len(ct_flat), len(add_list)
    row_grad_dtypes = row_grad_dtypes or [F32] * len(row_grads)

    def body(*refs):
        pos = 0
        row_v = [r[...].astype(F32) for r in refs[pos:pos + n_r]]; pos += n_r
        par_v = [r[...].astype(F32) for r in refs[pos:pos + n_p]]; pos += n_p
        con_v = [r[...] for r in refs[pos:pos + n_c]]; pos += n_c
        ct_v = [r[...].astype(F32) for r in refs[pos:pos + n_ct]]; pos += n_ct
        add_v = [r[...] for r in refs[pos:pos + n_add]]; pos += n_add
        emit_refs = refs[pos:pos + len(emit)]; pos += len(emit)
        rg_refs = refs[pos:pos + len(row_grads)]; pos += len(row_grads)
        pg_refs = refs[pos:pos + len(param_grads)]

        def diff_fn(*dargs):
            rv, pv = list(row_v), list(par_v)
            for q, i in enumerate(row_grads):
                rv[i] = dargs[q]
            for q, j in enumerate(param_grads):
                pv[j] = dargs[len(row_grads) + q]
            return fn(*rv, *pv, *con_v)

        prim = [row_v[i] for i in row_grads] + [par_v[j] for j in param_grads]
        res, vjp = jax.vjp(diff_fn, *prim)
        ct_vals, q = [], 0
        for o, group in zip(res, cts):
            tot = jnp.zeros_like(o)
            for _ in group:
                tot = tot + ct_v[q]
                q += 1
            ct_vals.append(tot)
        grads = vjp(tuple(ct_vals))
        for e_ref, idx in zip(emit_refs, emit):
            e_ref[...] = res[idx].astype(e_ref.dtype)
        add_pos = {p: q for q, p in enumerate(sorted(addends))}
        for q, g_ref in enumerate(rg_refs):
            g = grads[q]
            if q in add_pos:
                g = g + add_v[add_pos[q]]
            g_ref[...] = g.astype(g_ref.dtype)

        @pl.when(pl.program_id(0) == 0)
        def _():
            for g_ref in pg_refs:
                g_ref[...] = jnp.zeros_like(g_ref)

        for q, g_ref in enumerate(pg_refs):
            g_ref[...] += grads[len(row_grads) + q]

    emit_shapes = []
    if emit:
        probe = jax.eval_shape(lambda *a: fn(*a), *[jax.ShapeDtypeStruct((tm, r.shape[1]), F32) for r in rows],
                               *[jax.ShapeDtypeStruct(p.shape, p.dtype) for p in params],
                               *[jax.ShapeDtypeStruct(c.shape, c.dtype) for c in consts])
        emit_shapes = [probe[idx].shape[1] for idx in emit]
    out_specs = ([pl.BlockSpec((tm, n), lambda i: (i, 0)) for n in emit_shapes]
                 + [_row_spec(rows[i], tm) for i in row_grads]
                 + [_full(params[j]) for j in param_grads])
    out_shape = ([jax.ShapeDtypeStruct((t, n), F32) for n in emit_shapes]
                 + [jax.ShapeDtypeStruct(rows[i].shape, dt) for i, dt in zip(row_grads, row_grad_dtypes)]
                 + [jax.ShapeDtypeStruct(params[j].shape, F32) for j in param_grads])
    return pl.pallas_call(
        body, name=name, grid=(t // tm,),
        in_specs=([_row_spec(r, tm) for r in rows] + [_full(p) for p in params] + [_full(c) for c in consts]
                  + [_row_spec(c, tm) for c in ct_flat] + [_row_spec(a, tm) for a in add_list]),
        out_specs=out_specs, out_shape=out_shape,
        compiler_params=_params(("arbitrary",)),
    )(*rows, *params, *consts, *ct_flat, *add_list)


def _norm_mod_fn(x, gain, scale, shift):
    return (_rms(x) * gain * (1.0 + scale) + shift,)


def _resid_norm_mod_fn(x, mixed, gate, gain, scale, shift):
    x1 = x + gate * mixed
    return x1, _rms(x1) * gain * (1.0 + scale) + shift


def _loss_fn(x1, ffn, target, gate, gain):
    y = _rms(x1 + gate * ffn) * gain
    err = y - target
    return (0.5 * jnp.mean(err * err, axis=-1, keepdims=True),)


def _s5_out_fn(ylin, u, d_skip, w_glu, b_glu):
    z = _gelu(ylin + d_skip * u)
    return (z * _sigmoid(_bdot(z, w_glu) + b_glu),)


def _rk_pre_fn(k, wdn, adn, gdn, w0_0, w0_1, wup_0, wup_1, a0_0, a0_1, aup_0, aup_1, g_up, k_k, k_a, seg, seg_t):
    kkr = k * k_k
    inv = 1.0 / jnp.sqrt(jnp.maximum(_seg_dot(kkr * kkr, seg, seg_t), L2_EPS * L2_EPS))
    kk = kkr * _seg_dot(inv, seg_t, seg)
    tw = jnp.tanh(wdn)
    lws, kds, acts = [], [], []
    for w0, wup, a0, aup in ((w0_0, wup_0, a0_0, aup_0), (w0_1, wup_1, a0_1, aup_1)):
        lws.append(-LW_SCALE * _sigmoid(w0 + _bdot(tw, wup)))
        act = _sigmoid(a0 + _bdot(adn, aup))
        acts.append(act)
        kds.append(k * (1.0 + (act - 1.0) * k_a))
    gate = _bdot(_sigmoid(gdn), g_up)
    return (kk, lws[0], lws[1], kds[0], kds[1], acts[0], acts[1], gate)


def _rk_post_fn(y0, y1, r, v, kd0, kd1, gate, ln_gain, ln_bias, r_k, seg, seg_t):
    y = y0 + y1
    mu = _seg_dot(_seg_dot(y, seg, seg_t) * (1.0 / RK_HEAD), seg_t, seg)
    yc = y - mu
    var = _seg_dot(yc * yc, seg, seg_t) * (1.0 / RK_HEAD)
    yn = yc * _seg_dot(lax.rsqrt(var + GN_EPS), seg_t, seg) * ln_gain + ln_bias
    bonus = _seg_dot(_seg_dot(r * (kd0 + kd1) * r_k, seg, seg_t), seg_t, seg)
    return ((yn + bonus * v) * gate,)


def _s5_prep_fn(lr0, li0, ls0, lr1, li1, ls1, b_re, b_im):
    outs = []
    for lam_re, lam_im, ls in ((lr0, li0, ls0), (lr1, li1, ls1)):
        step = jnp.exp(ls)
        mag = jnp.exp(lam_re * step)
        lbar_re = mag * jnp.cos(lam_im * step)
        lbar_im = mag * jnp.sin(lam_im * step)
        den = lam_re * lam_re + lam_im * lam_im
        nr = lbar_re - 1.0
        coef_re = (nr * lam_re + lbar_im * lam_im) / den
        coef_im = (lbar_im * lam_re - nr * lam_im) / den
        outs += [lbar_re, lbar_im, coef_re * b_re - coef_im * b_im, coef_re * b_im + coef_im * b_re]
    return tuple(outs)


def _shift_rows(x, down):
    t = x.shape[0]
    rows = lax.broadcasted_iota(jnp.int32, x.shape, 0)
    if down:
        return jnp.where(rows >= 1, pltpu.roll(x, 1, 0), 0.0)
    return jnp.where(rows < t - 1, pltpu.roll(x, t - 1, 0), 0.0)


def _token_shift(p, mu_prev, mu_next):
    t, n = p.shape

    def body(p_ref, mp_ref, mn_ref, o_ref):
        x = p_ref[...]
        o_ref[...] = x + mp_ref[...] * (_shift_rows(x, True) - x) + mn_ref[...] * (_shift_rows(x, False) - x)

    col = pl.BlockSpec((t, 128), lambda j: (0, j))
    par = pl.BlockSpec((1, 128), lambda j: (0, j))
    return pl.pallas_call(
        body, name="token_shift", grid=(n // 128,), in_specs=[col, par, par], out_specs=col,
        out_shape=jax.ShapeDtypeStruct((t, n), F32), compiler_params=_params(("parallel",)),
    )(p, mu_prev, mu_next)


def _token_shift_bwd(p, mu_prev, mu_next, dps):
    t, n = p.shape

    def body(p_ref, mp_ref, mn_ref, d_ref, dp_ref, dmp_ref, dmn_ref):
        x, d, mp, mn = p_ref[...], d_ref[...], mp_ref[...], mn_ref[...]
        dp_ref[...] = d * (1.0 - mp - mn) + _shift_rows(d * mp, False) + _shift_rows(d * mn, True)
        dmp_ref[...] = jnp.sum(d * (_shift_rows(x, True) - x), axis=0, keepdims=True)
        dmn_ref[...] = jnp.sum(d * (_shift_rows(x, False) - x), axis=0, keepdims=True)

    col = pl.BlockSpec((t, 128), lambda j: (0, j))
    par = pl.BlockSpec((1, 128), lambda j: (0, j))
    return pl.pallas_call(
        body, name="token_shift_bwd", grid=(n // 128,), in_specs=[col, par, par, col],
        out_specs=[col, par, par],
        out_shape=[jax.ShapeDtypeStruct((t, n), F32), jax.ShapeDtypeStruct((1, n), F32),
                   jax.ShapeDtypeStruct((1, n), F32)],
        compiler_params=_params(("parallel",)),
    )(p, mu_prev, mu_next, dps)


N_SEG = 32
S5_BLOCKS = 32
S5_PER_IN = 4


def _scan_in_place(sr_ref, si_ref, ar, ai, carry_ref, reverse):
    seg_len = sr_ref.shape[0] // N_SEG
    ng = N_SEG // 8

    def rows(i, grp):
        first = (seg_len - 1 - i if reverse else i) * N_SEG + 8 * grp
        return pl.ds(pl.multiple_of(first, 8), 8)

    zero = jnp.zeros((8, 128), F32)
    one = jnp.ones((8, 128), F32)

    def local(i, c):
        pr, pi = c[-2:]
        out = []
        for grp in range(ng):
            sr, si = c[2 * grp], c[2 * grp + 1]
            nr = ar * sr - ai * si + sr_ref[rows(i, grp), :]
            ni = ar * si + ai * sr + si_ref[rows(i, grp), :]
            sr_ref[rows(i, grp), :] = nr
            si_ref[rows(i, grp), :] = ni
            out += [nr, ni]
        return tuple(out) + (ar * pr - ai * pi, ar * pi + ai * pr)

    ends = lax.fori_loop(0, seg_len, local, (zero,) * (2 * ng) + (one, zero))
    qr, qi = ends[-2][0:1], ends[-1][0:1]
    order = list(range(N_SEG - 1, -1, -1)) if reverse else list(range(N_SEG))
    cr = jnp.zeros((1, 128), F32)
    ci = jnp.zeros((1, 128), F32)
    for j in order:
        carry_ref[j:j + 1, :] = cr
        carry_ref[N_SEG + j:N_SEG + j + 1, :] = ci
        grp, sub = divmod(j, 8)
        lr, li = ends[2 * grp][sub:sub + 1], ends[2 * grp + 1][sub:sub + 1]
        cr, ci = lr + qr * cr - qi * ci, li + qr * ci + qi * cr
    carries = [(carry_ref[8 * grp:8 * grp + 8, :], carry_ref[N_SEG + 8 * grp:N_SEG + 8 * grp + 8, :])
               for grp in range(ng)]

    def fix(i, c):
        pr, pi = c
        npr, npi = ar * pr - ai * pi, ar * pi + ai * pr
        for grp in range(ng):
            cr8, ci8 = carries[grp]
            sr_ref[rows(i, grp), :] = sr_ref[rows(i, grp), :] + npr * cr8 - npi * ci8
            si_ref[rows(i, grp), :] = si_ref[rows(i, grp), :] + npr * ci8 + npi * cr8
        return npr, npi

    lax.fori_loop(0, seg_len, fix, (one, zero))


def _interleave(x):
    t, c = x.shape
    return jnp.transpose(x.reshape(N_SEG, t // N_SEG, c), (1, 0, 2)).reshape(t, c)


def _deinterleave(x):
    t, c = x.shape
    return jnp.transpose(x.reshape(t // N_SEG, N_SEG, c), (1, 0, 2)).reshape(t, c)


def _step_neighbour(s, earlier):
    t = s.shape[0]
    rows = lax.broadcasted_iota(jnp.int32, s.shape, 0)
    if earlier:
        return jnp.where(rows >= N_SEG, pltpu.roll(s, N_SEG, 0),
                         jnp.where(rows >= 1, pltpu.roll(s, N_SEG + 1, 0), 0.0))
    return jnp.where(rows < t - N_SEG, pltpu.roll(s, t - N_SEG, 0),
                     jnp.where(rows < t - 1, pltpu.roll(s, t - N_SEG - 1, 0), 0.0))


def _dot_bf16(a, b, dims=(((1,), (0,)), ((), ()))):
    return lax.dot_general(a.astype(BF16), b.astype(BF16), dims, preferred_element_type=F32)


NT_DIMS = (((1,), (1,)), ((), ()))
TN_DIMS = (((0,), (0,)), ((), ()))


def _s5_specs(t):
    blk = pl.BlockSpec((None, t, 128), lambda i, q: (S5_PER_IN * i + q, 0, 0))
    mat = pl.BlockSpec((None, 128, 128), lambda i, q: (S5_PER_IN * i + q, 0, 0))
    vec = pl.BlockSpec((None, 1, 128), lambda i, q: (S5_PER_IN * i + q, 0, 0))
    chan = pl.BlockSpec((t, 128), lambda i, q: (0, i))
    return blk, mat, vec, chan


S5_GRID = (S5_BLOCKS // S5_PER_IN, S5_PER_IN)


def _s5_forward(name, u, b_re, b_im, l_re, l_im, reverse, other=None, c_re=None, c_im_neg=None):
    t = u.shape[0]
    project = other is not None
    blk, mat, vec, chan = _s5_specs(t)

    def body(*refs):
        u_ref, br_ref, bi_ref, lr_ref, li_ref = refs[:5]
        if project:
            or_ref, oi_ref, cr_ref, ci_ref, sr_ref, si_ref, y_ref, carry_ref = refs[5:]
        else:
            sr_ref, si_ref, carry_ref = refs[5:]
        uv = u_ref[...]
        sr_ref[...] = _dot_bf16(uv, br_ref[...])
        si_ref[...] = _dot_bf16(uv, bi_ref[...])
        ar = jnp.broadcast_to(lr_ref[...], (8, 128))
        ai = jnp.broadcast_to(li_ref[...], (8, 128))
        _scan_in_place(sr_ref, si_ref, ar, ai, carry_ref, reverse)
        if project:
            y = (_dot_bf16(sr_ref[...] + or_ref[...], cr_ref[...])
                 + _dot_bf16(si_ref[...] + oi_ref[...], ci_ref[...]))

            @pl.when(pl.program_id(1) == 0)
            def _():
                y_ref[...] = y

            @pl.when(pl.program_id(1) != 0)
            def _():
                y_ref[...] += y

    state = jax.ShapeDtypeStruct((S5_BLOCKS, t, 128), F32)
    ins = [u, b_re, b_im, l_re, l_im] + ([other[0], other[1], c_re, c_im_neg] if project else [])
    in_specs = [chan, mat, mat, vec, vec] + ([blk, blk, mat, mat] if project else [])
    return pl.pallas_call(
        body, name=name, grid=S5_GRID, in_specs=in_specs,
        out_specs=[blk, blk] + ([chan] if project else []),
        out_shape=[state, state] + ([jax.ShapeDtypeStruct((t, S5_WIDTH), F32)] if project else []),
        scratch_shapes=[pltpu.VMEM((2 * N_SEG, 128), F32)],
        compiler_params=_params(("arbitrary", "arbitrary")),
    )(*ins)


def _s5_backward(name, dy, u, du_in, states, other, b_re, b_im, c_re, c_im_neg, l_re, l_im, reverse):
    t = u.shape[0]
    with_c = other is not None
    blk, mat, vec, chan = _s5_specs(t)

    def body(*refs):
        dy_ref, u_ref, du_in_ref, sr_ref, si_ref = refs[:5]
        pos = 5
        if with_c:
            or_ref, oi_ref = refs[5:7]
            pos = 7
        br_ref, bi_ref, cr_ref, ci_ref, lr_ref, li_ref = refs[pos:pos + 6]
        outs = refs[pos + 6:]
        du_ref, dbr_ref, dbi_ref, dlr_ref, dli_ref = outs[:5]
        lam_r, lam_i, carry_ref = outs[-3:]
        dyv, uv = dy_ref[...], u_ref[...]
        lam_r[...] = _dot_bf16(dyv, cr_ref[...], NT_DIMS)
        lam_i[...] = _dot_bf16(dyv, ci_ref[...], NT_DIMS)
        ar = jnp.broadcast_to(lr_ref[...], (8, 128))
        ai = -jnp.broadcast_to(li_ref[...], (8, 128))
        _scan_in_place(lam_r, lam_i, ar, ai, carry_ref, not reverse)
        lr, li = lam_r[...], lam_i[...]
        pr, pi = _step_neighbour(sr_ref[...], not reverse), _step_neighbour(si_ref[...], not reverse)
        dlr_ref[...] = jnp.sum(lr * pr + li * pi, axis=0, keepdims=True)
        dli_ref[...] = jnp.sum(li * pr - lr * pi, axis=0, keepdims=True)
        dbr_ref[...] = _dot_bf16(uv, lr, TN_DIMS)
        dbi_ref[...] = _dot_bf16(uv, li, TN_DIMS)
        du = _dot_bf16(lr, br_ref[...], NT_DIMS) + _dot_bf16(li, bi_ref[...], NT_DIMS)

        @pl.when(pl.program_id(1) == 0)
        def _():
            du_ref[...] = du_in_ref[...] + du

        @pl.when(pl.program_id(1) != 0)
        def _():
            du_ref[...] += du

        if with_c:
            dcr_ref, dci_ref = outs[5:7]
            dcr_ref[...] = _dot_bf16(sr_ref[...] + or_ref[...], dyv, TN_DIMS)
            dci_ref[...] = _dot_bf16(si_ref[...] + oi_ref[...], dyv, TN_DIMS)

    mats = jax.ShapeDtypeStruct((S5_BLOCKS, 128, 128), F32)
    vecs = jax.ShapeDtypeStruct((S5_BLOCKS, 1, 128), F32)
    ins = [dy, u, du_in, states[0], states[1]] + ([other[0], other[1]] if with_c else [])
    ins += [b_re, b_im, c_re, c_im_neg, l_re, l_im]
    in_specs = [chan, chan, chan, blk, blk] + ([blk, blk] if with_c else []) + [mat] * 4 + [vec] * 2
    return pl.pallas_call(
        body, name=name, grid=S5_GRID, in_specs=in_specs,
        out_specs=[chan, mat, mat, vec, vec] + ([mat, mat] if with_c else []),
        out_shape=[jax.ShapeDtypeStruct((t, S5_WIDTH), F32), mats, mats, vecs, vecs] + ([mats, mats] if with_c else []),
        scratch_shapes=[pltpu.VMEM((t, 128), F32), pltpu.VMEM((t, 128), F32), pltpu.VMEM((2 * N_SEG, 128), F32)],
        compiler_params=_params(("arbitrary", "arbitrary")),
    )(*ins)


def _ein(passes, spec, a, b):
    if passes == 6:
        return jnp.einsum(spec, a, b, precision=HI, preferred_element_type=F32)
    a_hi, b_hi = a.astype(BF16), b.astype(BF16)
    if passes == 1:
        return jnp.einsum(spec, a_hi, b_hi, preferred_element_type=F32)
    a_lo = (a - a_hi.astype(F32)).astype(BF16)
    b_lo = (b - b_hi.astype(F32)).astype(BF16)
    cross = jnp.einsum(spec, a_hi, b_lo, preferred_element_type=F32)
    if spec.startswith('hik'):
        m = a.shape[1]
        stacked = jnp.einsum(spec, jnp.concatenate([a_hi, a_lo], axis=1), b_hi, preferred_element_type=F32)
        return stacked[:, :m] + stacked[:, m:] + cross
    return (jnp.einsum(spec, a_hi, b_hi, preferred_element_type=F32) + cross
            + jnp.einsum(spec, a_lo, b_hi, preferred_element_type=F32))


@jax.custom_vjp
def _tri_mm(tri, tri_t, z):
    return jnp.einsum('hik,hkj->hij', tri, z, precision=HI, preferred_element_type=F32)


def _tri_mm_bwd(res, g):
    tri, tri_t = res
    return jnp.zeros_like(tri), jnp.zeros_like(tri_t), _tri_mm(tri_t, tri, g)


_tri_mm.defvjp(lambda tri, tri_t, z: (_tri_mm(tri, tri_t, z), (tri, tri_t)), _tri_mm_bwd)


def _chunk_cumsum(lw, incl, incl_t):
    shape = (lw.shape[0],) + incl.shape
    return _tri_mm(jnp.broadcast_to(incl.astype(F32), shape), jnp.broadcast_to(incl_t.astype(F32), shape), lw)


@functools.partial(jax.custom_vjp, nondiff_argnums=(0,))
def _bmm(p, a, b):
    return _ein(p, 'hik,hkj->hij', a, b)


@functools.partial(jax.custom_vjp, nondiff_argnums=(0,))
def _bmm_nt(p, a, b):
    return _ein(p, 'hik,hjk->hij', a, b)


@functools.partial(jax.custom_vjp, nondiff_argnums=(0,))
def _bmm_tn(p, a, b):
    return _ein(p, 'hki,hkj->hij', a, b)


_bmm.defvjp(lambda p, a, b: (_bmm(p, a, b), (a, b)),
            lambda p, res, g: (_bmm_nt(p, g, res[1]), _bmm_tn(p, res[0], g)))
_bmm_nt.defvjp(lambda p, a, b: (_bmm_nt(p, a, b), (a, b)),
               lambda p, res, g: (_bmm(p, g, res[1]), _bmm_tn(p, g, res[0])))
_bmm_tn.defvjp(lambda p, a, b: (_bmm_tn(p, a, b), (a, b)),
               lambda p, res, g: (_bmm_nt(p, res[1], g), _bmm(p, res[0], g)))


@jax.custom_vjp
def _split_rows(x):
    c = x.shape[1] // 2
    return x[:, :c], x[:, c:]


_split_rows.defvjp(lambda x: (_split_rows(x), None), lambda _, g: (jnp.concatenate(g, axis=1),))


def _stack_rows(a, b):
    return jnp.concatenate([a, b], axis=1)


def _nilpotent_inverse(l_mat):
    c = l_mat.shape[1]
    ps = RK_PASSES["solve"]
    row = lax.broadcasted_iota(jnp.int32, (c, c), 0)
    col = lax.broadcasted_iota(jnp.int32, (c, c), 1)
    x = -l_mat
    inv = jnp.where(row == col, 1.0, 0.0) + x
    power = _bmm(ps, x, x)
    span = 2
    while 2 * span < c:
        step, power = _split_rows(_bmm(ps, _stack_rows(inv, power), power))
        inv = inv + step
        span *= 2
    return inv + _bmm(ps, inv, power)


@jax.custom_vjp
def _nilpotent_solve(l_mat, rhs):
    return _bmm(RK_PASSES["solve"], _nilpotent_inverse(l_mat), rhs)


def _nilpotent_solve_fwd(l_mat, rhs):
    inv = _nilpotent_inverse(l_mat)
    u = _bmm(RK_PASSES["solve"], inv, rhs)
    return u, (inv, u)


def _nilpotent_solve_bwd(res, g):
    inv, u = res
    d_rhs = _bmm_tn(RK_PASSES["solve"], inv, g)
    return -_bmm_nt(RK_PASSES["solve"], d_rhs, u), d_rhs


_nilpotent_solve.defvjp(_nilpotent_solve_fwd, _nilpotent_solve_bwd)


def _rk_chunk(s0, r, lw, k, v, kk, a, reverse):
    h, c, n = r.shape
    row = lax.broadcasted_iota(jnp.int32, (c, c), 0)
    col = lax.broadcasted_iota(jnp.int32, (c, c), 1)
    incl = (row <= col) if reverse else (row >= col)
    strict = (row < col) if reverse else (row > col)
    cum = _chunk_cumsum(lw, incl, (row >= col) if reverse else (row <= col))
    g_in = jnp.exp(cum)
    g_inv = jnp.exp(-cum)
    kap = kk * jnp.exp(cum - lw)
    beta = kk * a * g_inv
    kt = k * g_inv
    rt = r * g_in
    p, ps = RK_PASSES, RK_PASSES["solve"]
    both = _stack_rows(kap, rt)
    kap_beta, rt_beta = _split_rows(_bmm_nt(ps, both, beta))
    kap_kt, rt_kt = _split_rows(_bmm_nt(p["kt"], both, kt))
    kap_s0, rt_s0 = _split_rows(_bmm_nt(p["s0"], both, s0))
    l_mat = jnp.where(strict, kap_beta, 0.0)
    rhs = kap_s0 + _bmm(p["akk_v"], jnp.where(strict, kap_kt, 0.0), v)
    u = _nilpotent_solve(l_mat, rhs)
    y = (rt_s0 + _bmm(p["ark_v"], jnp.where(incl, rt_kt, 0.0), v)
         - _bmm(p["arb_u"], jnp.where(incl, rt_beta, 0.0), u))
    s1 = ((s0 + _bmm_tn(p["state"], _stack_rows(v, -u), _stack_rows(kt, beta)))
          * jnp.exp(jnp.sum(lw, axis=1, keepdims=True)))
    return y, s1


class _Plan:
    def __init__(self, arrays, out_shape, sems, start, wait, finish):
        self.arrays, self.out_shape, self.sems = list(arrays), list(out_shape), list(sems)
        self.start, self.wait, self.finish = start, wait, finish


_NO_PLAN = _Plan([], [], [], lambda *_: None, lambda *_: None, lambda outs: [])


def _join_plans(plans):
    def cut(seq, sizes):
        out, pos = [], 0
        for s in sizes:
            out.append(seq[pos:pos + s])
            pos += s
        return out

    n_arr, n_sem = [len(p.arrays) for p in plans], [len(p.sems) for p in plans]

    def run(which):
        def go(in_refs, out_refs, sems):
            for p, i, o, s in zip(plans, cut(in_refs, n_arr), cut(out_refs, n_arr), cut(sems, n_sem)):
                getattr(p, which)(i, o, s)
        return go

    return _Plan([a for p in plans for a in p.arrays], [s for p in plans for s in p.out_shape],
                 [s for p in plans for s in p.sems], run("start"), run("wait"),
                 lambda outs: [p.finish(o) for p, o in zip(plans, cut(outs, n_arr))])


def _split_heads(x):
    return jnp.stack([x[:, RK_HEAD * i:RK_HEAD * (i + 1)] for i in range(RK_HEADS)], axis=0)


def _store_heads(ref, x):
    for i in range(RK_HEADS):
        ref[:, RK_HEAD * i:RK_HEAD * (i + 1)] = x[i]


def _rk_core_fwd(name, r, lw, k, v, kk, a, reverse, chunk, hosted=None):
    t = r.shape[0]
    h, n = RK_HEADS, RK_HEAD
    nc = t // chunk

    def idx(i):
        return nc - 1 - i if reverse else i

    hosted = hosted or _NO_PLAN
    nh = len(hosted.arrays)

    def body(r_ref, lw_ref, k_ref, v_ref, kk_ref, a_ref, *rest):
        host_in, (y_ref, ck_ref), host_out = rest[:nh], rest[nh:nh + 2], rest[nh + 2:2 * nh + 2]
        s_ref, sems = rest[2 * nh + 2], rest[2 * nh + 3:]

        @pl.when(pl.program_id(0) == 0)
        def _():
            s_ref[...] = jnp.zeros_like(s_ref)
            hosted.start(host_in, host_out, sems)

        s0 = s_ref[...]
        ck_ref[0] = s0
        ops = [_split_heads(ref[...]) for ref in (r_ref, lw_ref, k_ref, v_ref, kk_ref, a_ref)]
        y, s1 = _rk_chunk(s0, *ops, reverse)
        _store_heads(y_ref, y)
        s_ref[...] = s1

        @pl.when(pl.program_id(0) == nc - 1)
        def _():
            hosted.wait(host_in, host_out, sems)

    blk = pl.BlockSpec((chunk, RK_WIDTH), lambda i: (idx(i), 0))
    any_spec = pl.BlockSpec(memory_space=pl.ANY)
    res = pl.pallas_call(
        body, name=name, grid=(nc,), in_specs=[blk] * 6 + [any_spec] * nh,
        out_specs=[blk, pl.BlockSpec((1, h, n, n), lambda i: (idx(i), 0, 0, 0))] + [any_spec] * nh,
        out_shape=[jax.ShapeDtypeStruct((t, RK_WIDTH), F32), jax.ShapeDtypeStruct((nc, h, n, n), F32)]
        + hosted.out_shape,
        scratch_shapes=[pltpu.VMEM((h, n, n), F32)] + hosted.sems,
        compiler_params=_params(("arbitrary",)),
    )(r, lw, k, v, kk, a, *hosted.arrays)
    return res[0], res[1], hosted.finish(res[2:])


def _rk_core_bwd(name, r, lw, k, v, kk, a, ck, dy, reverse, chunk, hosted=None):
    t = r.shape[0]
    h, n = RK_HEADS, RK_HEAD
    nc = t // chunk
    hosted = hosted or _NO_PLAN
    nh = len(hosted.arrays)

    def idx(i):
        return i if reverse else nc - 1 - i

    def body(r_ref, lw_ref, k_ref, v_ref, kk_ref, a_ref, ck_ref, dy_ref, *rest):
        host_in, out_refs, host_out = rest[:nh], rest[nh:nh + 6], rest[nh + 6:2 * nh + 6]
        ds_ref, sems = rest[2 * nh + 6], rest[2 * nh + 7:]

        @pl.when(pl.program_id(0) == 0)
        def _():
            ds_ref[...] = jnp.zeros_like(ds_ref)
            hosted.start(host_in, host_out, sems)

        fn = functools.partial(_rk_chunk, reverse=reverse)
        ops = [_split_heads(ref[...]) for ref in (r_ref, lw_ref, k_ref, v_ref, kk_ref, a_ref)]
        _, vjp = jax.vjp(fn, ck_ref[0], *ops)
        grads = vjp((_split_heads(dy_ref[...]), ds_ref[...]))
        ds_ref[...] = grads[0]
        for o_ref, g in zip(out_refs, grads[1:]):
            _store_heads(o_ref, g)

        @pl.when(pl.program_id(0) == nc - 1)
        def _():
            hosted.wait(host_in, host_out, sems)

    blk = pl.BlockSpec((chunk, RK_WIDTH), lambda i: (idx(i), 0))
    any_spec = pl.BlockSpec(memory_space=pl.ANY)
    res = pl.pallas_call(
        body, name=name, grid=(nc,),
        in_specs=[blk] * 6 + [pl.BlockSpec((1, h, n, n), lambda i: (idx(i), 0, 0, 0)), blk] + [any_spec] * nh,
        out_specs=[blk] * 6 + [any_spec] * nh,
        out_shape=[jax.ShapeDtypeStruct((t, RK_WIDTH), F32)] * 6 + hosted.out_shape,
        scratch_shapes=[pltpu.VMEM((h, n, n), F32)] + hosted.sems,
        compiler_params=_params(("arbitrary",)),
    )(r, lw, k, v, kk, a, ck, dy, *hosted.arrays)
    return res[:6], hosted.finish(res[6:])


def _s5_band_place():
    return jax.nn.one_hot(jnp.arange(S5_BLOCKS) % S5_PER_IN, S5_PER_IN, dtype=F32)


def _s5_in_blocks(bbar):
    b = jnp.transpose(bbar.reshape(S5_BLOCKS, 2, S5_STATE, S5_GROUP), (0, 1, 3, 2))
    band = jnp.einsum('jghp,gk->jghkp', b, jnp.eye(2, dtype=F32)).reshape(S5_BLOCKS, 32, 128)
    return jnp.einsum('jrc,jq->jqrc', band, _s5_band_place()).reshape(S5_BLOCKS, 128, 128)


def _s5_in_unblock(mats):
    band = jnp.einsum('jqrc,jq->jrc', mats.reshape(S5_BLOCKS, S5_PER_IN, 32, 128), _s5_band_place())
    diag = jnp.einsum('jghgp->jghp', band.reshape(S5_BLOCKS, 2, S5_GROUP, 2, S5_STATE))
    return jnp.transpose(diag, (0, 1, 3, 2)).reshape(S5_CH, S5_GROUP)


def _s5_out_blocks(c):
    ct = jnp.transpose(c.reshape(S5_BLOCKS, 2, S5_GROUP, S5_STATE), (0, 1, 3, 2))
    band = jnp.einsum('jgph,gk->jgpkh', ct, jnp.eye(2, dtype=F32)).reshape(S5_BLOCKS, 128, 32)
    return jnp.einsum('jrc,jq->jrqc', band, _s5_band_place()).reshape(S5_BLOCKS, 128, 128)


def _s5_out_unblock(mats):
    band = jnp.einsum('jrqc,jq->jrc', mats.reshape(S5_BLOCKS, 128, S5_PER_IN, 32), _s5_band_place())
    diag = jnp.einsum('jgpgh->jgph', band.reshape(S5_BLOCKS, 2, S5_STATE, 2, S5_GROUP))
    return jnp.transpose(diag, (0, 1, 3, 2)).reshape(S5_GROUPS, S5_GROUP, S5_STATE)


def _head_indicator():
    ch = lax.broadcasted_iota(jnp.int32, (RK_WIDTH, 128), 0) // RK_HEAD
    hd = lax.broadcasted_iota(jnp.int32, (RK_WIDTH, 128), 1)
    seg = (ch == hd).astype(F32)
    return seg, seg.T


def _add_epilogue(acc, e):
    return (acc + e,)


def _local_step(x, target, mod, wt, chunk=RK_CHUNK, ffn_shards=None):
    t = x.shape[0]
    wt = dict(wt)
    sh1, sc1, gt1, sh2, sc2, gt2 = mod
    seg, seg_t = _head_indicator()
    g = {}

    (h1,) = _rowwise("norm1", _norm_mod_fn, [x], [wt["norm1_gain"], sc1, sh1], [(D_MODEL, BF16)], 256)
    proj = _matmul("proj", h1, wt["w_in"])
    u, p = proj[:, :S5_WIDTH], proj[:, S5_WIDTH:]
    ps = _token_shift(p, wt["mu_prev"], wt["mu_next"])
    r, k, v = ps[:, :1024], ps[:, 1024:2048], ps[:, 2048:3072]
    wdn, adn, gdn = ps[:, 3072:3200], ps[:, 3200:3328], ps[:, 3328:RK_PAD]

    prep_rows = [wt["lam_re"][0], wt["lam_im"][0], wt["log_step"][0], wt["lam_re"][1], wt["lam_im"][1],
                 wt["log_step"][1], wt["b_re"], wt["b_im"]]
    col1, col16 = (1, F32), (S5_GROUP, F32)
    prep = _rowwise("s5_prep", _s5_prep_fn, prep_rows, [], [col1, col1, col16, col16] * 2, 512)
    lbar = [tuple(prep[4 * d + q].reshape(S5_BLOCKS, 1, 128) for q in range(2)) for d in range(2)]
    b_blk = [tuple(_s5_in_blocks(prep[4 * d + 2 + q]) for q in range(2)) for d in range(2)]
    c_blk = (_s5_out_blocks(wt["c_re"]), -_s5_out_blocks(wt["c_im"]))
    u_il = _interleave(u)
    state0 = _s5_forward("s5_fwd0", u_il, *b_blk[0], *lbar[0], reverse=False)
    s1_re, s1_im, ylin_il = _s5_forward("s5_fwd1", u_il, *b_blk[1], *lbar[1], reverse=True, other=state0,
                                        c_re=c_blk[0], c_im_neg=c_blk[1])
    ylin = _deinterleave(ylin_il)
    states = [tuple(state0), (s1_re, s1_im)]
    s5_par = [wt["s5_d"], wt["s5_w_glu"], wt["s5_b_glu"]]
    (o_s5,) = _rowwise("s5_out", _s5_out_fn, [ylin, u], s5_par, [(S5_WIDTH, BF16)], 256)

    pre_par = [wt["w0"][0], wt["w0"][1], wt["w_up"][0], wt["w_up"][1], wt["a0"][0], wt["a0"][1],
               wt["a_up"][0], wt["a_up"][1], wt["g_up"], wt["k_k"], wt["k_a"]]
    pre = _rowwise("rk_pre", _rk_pre_fn, [k, wdn, adn, gdn], pre_par + [seg, seg_t], [(RK_WIDTH, F32)] * 8, 256)
    kk, lw, kd, act, gate = pre[0], pre[1:3], pre[3:5], pre[5:7], pre[7]
    core_in, ys, cks = [], [], []
    for d in range(2):
        ops = (r, lw[d], kd[d], v, kk, act[d])
        plan = _gather_halves_plan([ffn_shards[d]]) if ffn_shards is not None else None
        y, ck, gathered = _rk_core_fwd(f"rk_core{d}", *ops, reverse=(d == 1), chunk=min(chunk, t), hosted=plan)
        if gathered:
            wt["ffn_w1" if d == 0 else "ffn_w2"] = gathered[0] if d == 0 else gathered[0].reshape(FFN, D_MODEL)
        core_in.append(ops)
        ys.append(y)
        cks.append(ck)
    post_rows = [ys[0], ys[1], r, v, kd[0], kd[1], gate]
    post_par = [wt["ln_gain"], wt["ln_bias"], wt["r_k"]]
    (o_rk,) = _rowwise("rk_post", _rk_post_fn, post_rows, post_par + [seg, seg_t], [(RK_WIDTH, BF16)], 256)

    o = jnp.concatenate([o_s5, o_rk], axis=1)
    mixed = _matmul("mix_out", o, wt["w_out"])
    n2_par = [gt1, wt["norm2_gain"], sc2, sh2]
    x1, h2 = _rowwise("norm2", _resid_norm_mod_fn, [x, mixed], n2_par, [(D_MODEL, F32), (D_MODEL, BF16)], 256)
    f1, hid = _matmul("ffn1", h2, wt["ffn_w1"], out_dtypes=(F32, BF16), chips="b",
                      epilogue=lambda acc: (acc, jnp.square(jnp.maximum(acc, 0.0))))
    ffn = _matmul("ffn2", hid, wt["ffn_w2"])

    ones = jnp.ones((t, 1), F32)
    loss_rows, dx1, dffn, g_gt2, g["final_gain"] = _rowwise_vjp(
        "loss", _loss_fn, [x1, ffn, target], [gt2, wt["final_gain"]], [[ones]], [0, 1], [0, 1], 256, emit=(0,),
        row_grad_dtypes=[F32, BF16])
    df1 = _matmul("ffn2_dx", dffn, wt["ffn_w2"], tb=True, extras=(f1,), out_dtypes=(BF16,),
                  epilogue=lambda acc, f: (acc * (2.0 * jnp.maximum(f, 0.0)),))
    g["ffn_w2"] = _matmul("ffn2_dw", hid, dffn, ta=True)
    g["ffn_w1"] = _matmul("ffn1_dw", h2, df1, ta=True, chips="out")
    if ffn_shards is None:
        dh2 = _matmul("ffn1_dx", df1, wt["ffn_w1"], tb=True, chips="b_t")
    else:
        ffn_pieces = [g.pop("ffn_w1"), g.pop("ffn_w2").reshape(N_CHIPS, -1, D_MODEL)]
        dh2, from_sibling = _matmul("ffn1_dx", df1, wt["ffn_w1"], tb=True, chips="b_t",
                                    hosted=_other_half_plan(ffn_pieces))
        ffn_sums = [_pair_sum("pair_" + n, piece, other, BF16)
                    for n, piece, other in zip(FFN_SHARDED, ffn_pieces, from_sibling)]
    dx_a, dmixed, g_gt1, g["norm2_gain"], g_sc2, g_sh2 = _rowwise_vjp(
        "norm2_bwd", _resid_norm_mod_fn, [x, mixed], n2_par, [[dx1], [dh2]], [0, 1], [0, 1, 2, 3], 256,
        row_grad_dtypes=[F32, BF16])
    do = _matmul("mix_out_dx", dmixed, wt["w_out"], tb=True)
    g["w_out"] = _matmul("mix_out_dw", o, dmixed, ta=True)
    do_s5, do_rk = do[:, :S5_WIDTH], do[:, S5_WIDTH:]

    dylin, du, g["s5_d"], g["s5_w_glu"], g["s5_b_glu"] = _rowwise_vjp(
        "s5_out_bwd", _s5_out_fn, [ylin, u], s5_par, [[do_s5]], [0, 1], [0, 1, 2], 256)
    prep_cts = []
    dylin_il, du_il = _interleave(dylin), _interleave(du)
    for d in range(2):
        res = _s5_backward(f"s5_bwd{d}", dylin_il, u_il, du_il, states[d], states[1] if d == 0 else None,
                           *b_blk[d], *c_blk, *lbar[d], reverse=(d == 1))
        du_il, db_re, db_im, dl_re, dl_im = res[:5]
        if d == 0:
            g["c_re"], g["c_im"] = _s5_out_unblock(res[5]), -_s5_out_unblock(res[6])
        prep_cts += [[dl_re.reshape(S5_CH, 1)], [dl_im.reshape(S5_CH, 1)], [_s5_in_unblock(db_re)],
                     [_s5_in_unblock(db_im)]]
    du = _deinterleave(du_il)
    pg = _rowwise_vjp("s5_prep_bwd", _s5_prep_fn, prep_rows, [], prep_cts, list(range(8)), [], 512)
    g["lam_re"], g["lam_im"], g["log_step"] = (pg[0], pg[3]), (pg[1], pg[4]), (pg[2], pg[5])
    g["b_re"], g["b_im"] = pg[6], pg[7]

    pb = _rowwise_vjp("rk_post_bwd", _rk_post_fn, post_rows, post_par, [[do_rk]], [0, 2, 3, 4, 5, 6], [0, 1, 2],
                      128, consts=[seg, seg_t])
    dy, dr_b, dv_b, dkd_b, dgate = pb[0], pb[1], pb[2], pb[3:5], pb[5]
    g["ln_gain"], g["ln_bias"], g["r_k"] = pb[6], pb[7], pb[8]
    cg = []
    for d in range(2):
        plan = None
        if d == 0 and ffn_shards is not None:
            plan = _exchange_plan(ffn_sums, CHIP_PEERS, N_CHIPS, scatter=True)
        grads, arrived = _rk_core_bwd(f"rk_core{d}_bwd", *core_in[d], cks[d], dy, reverse=(d == 1),
                                      chunk=min(chunk, t), hosted=plan)
        if arrived:
            g["ffn_arrived"] = arrived
        cg.append(grads)
    pre_cts = [[cg[0][4], cg[1][4]], [cg[0][1]], [cg[1][1]], [cg[0][2], dkd_b[0]], [cg[1][2], dkd_b[1]],
               [cg[0][5]], [cg[1][5]], [dgate]]
    qb = _rowwise_vjp("rk_pre_bwd", _rk_pre_fn, [k, wdn, adn, gdn], pre_par, pre_cts, [0, 1, 2, 3],
                      list(range(11)), 128, consts=[seg, seg_t])
    dk, dwdn, dadn, dgdn = qb[:4]
    g["w0"], g["w_up"], g["a0"], g["a_up"] = (qb[4], qb[5]), (qb[6], qb[7]), (qb[8], qb[9]), (qb[10], qb[11])
    g["g_up"], g["k_k"], g["k_a"] = qb[12], qb[13], qb[14]
    dr, dv = _rowwise("rk_sum", lambda a, b, c, e, f, h: (a + b + c, e + f + h),
                      [cg[0][0], cg[1][0], dr_b, cg[0][3], cg[1][3], dv_b], [], [(RK_WIDTH, F32)] * 2, 256)
    dps = jnp.concatenate([dr, dk, dv, dwdn, dadn, dgdn], axis=1)
    dp, g["mu_prev"], g["mu_next"] = _token_shift_bwd(p, wt["mu_prev"], wt["mu_next"], dps)

    dproj = jnp.concatenate([du, dp], axis=1).astype(BF16)
    dh1 = _matmul("proj_dx", dproj, wt["w_in"], tb=True)
    g["w_in"] = _matmul("proj_dw", h1, dproj, ta=True)
    grad_x, g["norm1_gain"], g_sc1, g_sh1 = _rowwise_vjp(
        "norm1_bwd", _norm_mod_fn, [x], [wt["norm1_gain"], sc1, sh1], [[dh1]], [0], [0, 1, 2], 256,
        addends={0: dx_a})
    g["mod"] = [g_sh1, g_sc1, g_gt1, g_sh2, g_sc2, g_gt2]
    return loss_rows, grad_x, g


CHIP_PEERS = ((1, 0, 0), (0, 1, 0), (1, 1, 0))
ALL_PEERS = ((0, 0, 1), (0, 1, 0), (0, 1, 1), (1, 0, 0), (1, 0, 1), (1, 1, 0), (1, 1, 1))
CORE_PEER = ((0, 0, 1),)


def _exchange(name, arrays, peers, n_slots, scatter=False):
    return _run_plan(name, _exchange_plan(arrays, peers, n_slots, scatter))


def _run_plan(name, plan):
    na = len(plan.arrays)

    def body(*refs):
        plan.start(refs[:na], refs[na:2 * na], refs[2 * na:])
        plan.wait(refs[:na], refs[na:2 * na], refs[2 * na:])

    any_spec = pl.BlockSpec(memory_space=pl.ANY)
    return plan.finish(pl.pallas_call(
        body, name=name, in_specs=[any_spec] * na, out_specs=[any_spec] * na, out_shape=plan.out_shape,
        scratch_shapes=plan.sems,
    )(*plan.arrays))


def _exchange_plan(arrays, peers, n_slots, scatter=False):
    na, nm = len(arrays), len(peers)

    def ident(px, py, pc):
        return {8: 4 * px + 2 * py + pc, 4: 2 * px + py, 2: pc}[n_slots]

    def copies(in_refs, out_refs, sems):
        send_sems, recv_sems = sems
        x, y, c = lax.axis_index("x"), lax.axis_index("y"), lax.axis_index("c")
        me = ident(x, y, c)
        made = []
        for i in range(na):
            for j, (fx, fy, fc) in enumerate(peers):
                px, py, pc = (1 - x if fx else x), (1 - y if fy else y), (1 - c if fc else c)
                src = in_refs[i].at[ident(px, py, pc)] if scatter else in_refs[i]
                made.append(pltpu.make_async_remote_copy(
                    src_ref=src, dst_ref=out_refs[i].at[me],
                    send_sem=send_sems.at[i * nm + j], recv_sem=recv_sems.at[i * nm + j],
                    device_id=(px, py, pc), device_id_type=pl.DeviceIdType.MESH))
        return made

    def start(in_refs, out_refs, sems):
        for copy in copies(in_refs, out_refs, sems):
            copy.start()

    def wait(in_refs, out_refs, sems):
        for copy in copies(in_refs, out_refs, sems):
            copy.wait()

    def finish(outs):
        me = ident(lax.axis_index("x"), lax.axis_index("y"), lax.axis_index("c"))
        return [lax.dynamic_update_slice_in_dim(
            o, lax.dynamic_index_in_dim(a, me, 0, keepdims=True) if scatter else a[None], me, axis=0)
            for a, o in zip(arrays, outs)]

    out_shape = [jax.ShapeDtypeStruct(((n_slots,) + a.shape[1:]) if scatter else ((n_slots,) + a.shape), a.dtype)
                 for a in arrays]
    sems = [pltpu.SemaphoreType.DMA((na * nm,)), pltpu.SemaphoreType.DMA((na * nm,))]
    return _Plan(arrays, out_shape, sems, start, wait, finish)


def _gather_halves(name, arrays):
    return _run_plan(name, _gather_halves_plan(arrays))


def _gather_halves_plan(arrays):
    na = len(arrays)
    chips = ((1, 0), (0, 1), (1, 1))

    def over_ici(in_refs, out_refs, sems):
        ici_send, ici_recv = sems[:2]
        x, y, c = lax.axis_index("x"), lax.axis_index("y"), lax.axis_index("c")
        made = []
        for i in range(na):
            half = arrays[i].shape[0] // 2
            mine = pl.ds(pl.multiple_of(c * half, 8), half)
            for j, (fx, fy) in enumerate(chips):
                px, py = (1 - x if fx else x), (1 - y if fy else y)
                k = len(chips) * i + j
                made.append([pltpu.make_async_remote_copy(
                    src_ref=in_refs[i].at[mine], dst_ref=out_refs[i].at[chip, mine],
                    send_sem=ici_send.at[k], recv_sem=ici_recv.at[k],
                    device_id=(px, py, c), device_id_type=pl.DeviceIdType.MESH)
                    for chip in (2 * x + y, 2 * px + py)])
        return made

    def start(in_refs, out_refs, sems):
        for outgoing, _ in over_ici(in_refs, out_refs, sems):
            outgoing.start()

    def wait(in_refs, out_refs, sems):
        d2d_send, d2d_recv = sems[2:]
        x, y, c = lax.axis_index("x"), lax.axis_index("y"), lax.axis_index("c")
        pending = []
        ici = over_ici(in_refs, out_refs, sems)
        for i in range(na):
            half = arrays[i].shape[0] // 2
            mine = pl.ds(pl.multiple_of(c * half, 8), half)
            theirs = pl.ds(pl.multiple_of((1 - c) * half, 8), half)
            for j, (fx, fy) in enumerate(chips):
                px, py = (1 - x if fx else x), (1 - y if fy else y)
                k = len(chips) * i + j
                outgoing, landing = ici[k]
                landing.wait_recv()
                landed = out_refs[i].at[2 * px + py, mine]
                passed = pltpu.make_async_remote_copy(
                    src_ref=landed, dst_ref=landed, send_sem=d2d_send.at[k], recv_sem=d2d_recv.at[k],
                    device_id=(x, y, 1 - c), device_id_type=pl.DeviceIdType.MESH)
                passed.start()
                from_sibling = out_refs[i].at[2 * px + py, theirs]
                pending += [outgoing.wait_send, passed.wait_send, pltpu.make_async_remote_copy(
                    src_ref=from_sibling, dst_ref=from_sibling, send_sem=d2d_send.at[k], recv_sem=d2d_recv.at[k],
                    device_id=(x, y, 1 - c), device_id_type=pl.DeviceIdType.MESH).wait_recv]
        for done in pending:
            done()

    def finish(outs):
        me = 2 * lax.axis_index("x") + lax.axis_index("y")
        return [lax.dynamic_update_slice_in_dim(o, a[None], me, axis=0) for a, o in zip(arrays, outs)]

    out_shape = [jax.ShapeDtypeStruct((N_CHIPS,) + a.shape, a.dtype) for a in arrays]
    return _Plan(arrays, out_shape, [pltpu.SemaphoreType.DMA((na * len(chips),))] * 4, start, wait, finish)


def _send_other_half(name, arrays):
    return _run_plan(name, _other_half_plan(arrays))


def _other_half_plan(arrays):
    na = len(arrays)

    def copies(in_refs, out_refs, sems):
        send_sems, recv_sems = sems
        x, y, c = lax.axis_index("x"), lax.axis_index("y"), lax.axis_index("c")
        made = []
        for i in range(na):
            half = arrays[i].shape[1] // 2
            theirs = pl.ds(pl.multiple_of((1 - c) * half, 8), half)
            made.append(pltpu.make_async_remote_copy(
                src_ref=in_refs[i].at[:, theirs], dst_ref=out_refs[i], send_sem=send_sems.at[i],
                recv_sem=recv_sems.at[i], device_id=(x, y, 1 - c), device_id_type=pl.DeviceIdType.MESH))
        return made

    def start(in_refs, out_refs, sems):
        for copy in copies(in_refs, out_refs, sems):
            copy.start()

    def wait(in_refs, out_refs, sems):
        for copy in copies(in_refs, out_refs, sems):
            copy.wait()

    out_shape = [jax.ShapeDtypeStruct((a.shape[0], a.shape[1] // 2, a.shape[2]), a.dtype) for a in arrays]
    sems = [pltpu.SemaphoreType.DMA((na,)), pltpu.SemaphoreType.DMA((na,))]
    return _Plan(arrays, out_shape, sems, start, wait, list)


def _adam_math(w, g, m, v):
    m = ADAM_B1 * m + (1.0 - ADAM_B1) * g
    v = ADAM_B2 * v + (1.0 - ADAM_B2) * jnp.square(g)
    m_hat = m / (1.0 - ADAM_B1 ** ADAM_STEP)
    v_hat = v / (1.0 - ADAM_B2 ** ADAM_STEP)
    delta = -ADAM_LR * (m_hat / (jnp.sqrt(v_hat) + ADAM_EPS) + ADAM_WD * w)
    return delta, m, v


WHOLE_BLOCK_BYTES = 2 * 1024 * 1024


def _row_tile(r, c):
    return r if 4 * r * c <= WHOLE_BLOCK_BYTES else _tile(r, (256, 128, 64, 32, 16, 8))


def _sum_parts(name, parts):
    n, r, c = parts.shape
    tr = _row_tile(r, c)

    def body(p_ref, o_ref):
        tot = p_ref[0].astype(F32)
        for i in range(1, n):
            tot = tot + p_ref[i].astype(F32)
        o_ref[...] = tot

    return pl.pallas_call(
        body, name=name, grid=(r // tr,), in_specs=[pl.BlockSpec((n, tr, c), lambda i: (0, i, 0))],
        out_specs=pl.BlockSpec((tr, c), lambda i: (i, 0)), out_shape=jax.ShapeDtypeStruct((r, c), F32),
        compiler_params=_params(("parallel",)),
    )(parts)


def _pair_sum(name, piece, other, dtype):
    n, r, c = piece.shape
    half = r // 2
    tr = _row_tile(half, c)

    def body(lo_ref, hi_ref, other_ref, o_ref):
        own = jnp.where(lax.axis_index("c") == 0, lo_ref[...], hi_ref[...])
        o_ref[...] = (own + other_ref[...]).astype(o_ref.dtype)

    blk = pl.BlockSpec((None, tr, c), lambda j, i: (j, i, 0))
    return pl.pallas_call(
        body, name=name, grid=(n, half // tr),
        in_specs=[pl.BlockSpec((None, None, tr, c), lambda j, i: (j, 0, i, 0)),
                  pl.BlockSpec((None, None, tr, c), lambda j, i: (j, 1, i, 0)), blk],
        out_specs=blk, out_shape=jax.ShapeDtypeStruct((n, half, c), dtype),
        compiler_params=_params(("parallel", "parallel")),
    )(piece.reshape(n, 2, half, c), piece.reshape(n, 2, half, c), other)


def _adamw(name, w, parts, m, v):
    n, r, c = parts.shape
    tr = _row_tile(r, c)

    def body(w_ref, p_ref, m_ref, v_ref, g_ref, d_ref, nm_ref, nv_ref):
        g = p_ref[0]
        for i in range(1, n):
            g = g + p_ref[i]
        delta, nm, nv = _adam_math(w_ref[...], g, m_ref[...], v_ref[...])
        g_ref[...], d_ref[...], nm_ref[...], nv_ref[...] = g, delta, nm, nv

    blk = pl.BlockSpec((tr, c), lambda i: (i, 0))
    return pl.pallas_call(
        body, name=name, grid=(r // tr,),
        in_specs=[blk, pl.BlockSpec((n, tr, c), lambda i: (0, i, 0)), blk, blk], out_specs=[blk] * 4,
        out_shape=[jax.ShapeDtypeStruct((r, c), F32)] * 4, compiler_params=_params(("parallel",)),
    )(w, parts, m, v)


def _ada_w_update(act_t, dmod, w, m, v):
    r, c = w.shape
    nb = act_t.shape[1]
    tr, tc = 256, 1024

    def body(a_ref, d_ref, w_ref, m_ref, v_ref, g_ref, dl_ref, nm_ref, nv_ref):
        a, dm = a_ref[...], d_ref[...]
        g = a[:, 0:1] * dm[0:1, :]
        for b in range(1, nb):
            g = g + a[:, b:b + 1] * dm[b:b + 1, :]
        delta, nm, nv = _adam_math(w_ref[...], g, m_ref[...], v_ref[...])
        g_ref[...], dl_ref[...], nm_ref[...], nv_ref[...] = g, delta, nm, nv

    blk = pl.BlockSpec((tr, tc), lambda i, j: (i, j))
    return pl.pallas_call(
        body, name="ada_w_update", grid=(r // tr, c // tc),
        in_specs=[pl.BlockSpec((tr, nb), lambda i, j: (i, 0)), pl.BlockSpec((nb, tc), lambda i, j: (0, j)),
                  blk, blk, blk],
        out_specs=[blk] * 4, out_shape=[jax.ShapeDtypeStruct((r, c), F32)] * 4,
        compiler_params=_params(("parallel", "parallel")),
    )(act_t, dmod, w, m, v)


WEIGHTS = ['ada_w', 'ada_b', 'norm1_gain', 'norm2_gain', 'final_gain', 'w_in', 'w_out', 's5_lambda_re',
           's5_lambda_im', 's5_log_step', 's5_b_re', 's5_b_im', 's5_c_re', 's5_c_im', 's5_d', 's5_w_glu',
           's5_b_glu', 'rk_shift_prev', 'rk_shift_next', 'rk_w0', 'rk_w_up', 'rk_a0', 'rk_a_up', 'rk_g_up',
           'rk_k_k', 'rk_k_a', 'rk_r_k', 'rk_ln_gain', 'rk_ln_bias', 'ffn_w1', 'ffn_w2']
BIG_SHARDED = ['w_in', 'w_out', 's5_w_glu', 'ffn_w1', 'ffn_w2']
FFN_SHARDED = ['ffn_w1', 'ffn_w2']
RK_SHARDED = ['rk_w0', 'rk_a0', 'rk_w_up', 'rk_a_up', 'rk_g_up']
REPLICATED = ['ada_b', 'norm1_gain', 'norm2_gain', 'final_gain', 's5_lambda_re', 's5_lambda_im', 's5_log_step',
              's5_b_re', 's5_b_im', 's5_c_re', 's5_c_im', 's5_d', 's5_b_glu', 'rk_shift_prev', 'rk_shift_next',
              'rk_k_k', 'rk_k_a', 'rk_r_k', 'rk_ln_gain', 'rk_ln_bias']
PACK_COLS = 1024
N_CHIPS = 4
RK_ROWS = 420
RK_ROWS_PAD = 432


def _pack_rows(arrays, cols):
    return jnp.concatenate([a.reshape(-1, cols) for a in arrays], axis=0)


def _pack_flat(arrays):
    flat = jnp.concatenate([a.reshape(-1) for a in arrays])
    rows = -(-flat.shape[0] // PACK_COLS)
    return jnp.pad(flat, (0, rows * PACK_COLS - flat.shape[0])).reshape(rows, PACK_COLS)


def _unpack_flat(packed, like):
    flat, out, pos = packed.reshape(-1), [], 0
    for a in like:
        out.append(flat[pos:pos + a.size].reshape(a.shape))
        pos += a.size
    return out


def _cols_to_chips(full, n_rows):
    return jnp.transpose(full.reshape(n_rows, N_CHIPS, -1), (1, 0, 2))


def _chips_to_cols(parts):
    return jnp.transpose(parts, (1, 0, 2)).reshape(parts.shape[1], -1)


def kernel(x, c, ada_w, ada_b, norm1_gain, norm2_gain, final_gain, w_in, w_out, s5_lambda_re, s5_lambda_im, s5_log_step, s5_b_re, s5_b_im, s5_c_re, s5_c_im, s5_d, s5_w_glu, s5_b_glu, rk_shift_prev, rk_shift_next, rk_w0, rk_w_up, rk_a0, rk_a_up, rk_g_up, rk_k_k, rk_k_a, rk_r_k, rk_ln_gain, rk_ln_bias, ffn_w1, ffn_w2, loss_target, m_ada_w, m_ada_b, m_norm1_gain, m_norm2_gain, m_final_gain, m_w_in, m_w_out, m_s5_lambda_re, m_s5_lambda_im, m_s5_log_step, m_s5_b_re, m_s5_b_im, m_s5_c_re, m_s5_c_im, m_s5_d, m_s5_w_glu, m_s5_b_glu, m_rk_shift_prev, m_rk_shift_next, m_rk_w0, m_rk_w_up, m_rk_a0, m_rk_a_up, m_rk_g_up, m_rk_k_k, m_rk_k_a, m_rk_r_k, m_rk_ln_gain, m_rk_ln_bias, m_ffn_w1, m_ffn_w2, v_ada_w, v_ada_b, v_norm1_gain, v_norm2_gain, v_final_gain, v_w_in, v_w_out, v_s5_lambda_re, v_s5_lambda_im, v_s5_log_step, v_s5_b_re, v_s5_b_im, v_s5_c_re, v_s5_c_im, v_s5_d, v_s5_w_glu, v_s5_b_glu, v_rk_shift_prev, v_rk_shift_next, v_rk_w0, v_rk_w_up, v_rk_a0, v_rk_a_up, v_rk_g_up, v_rk_k_k, v_rk_k_a, v_rk_r_k, v_rk_ln_gain, v_rk_ln_bias, v_ffn_w1, v_ffn_w2):
    given = dict(locals())
    w = {n: given[n] for n in WEIGHTS}
    m = {n: given["m_" + n] for n in WEIGHTS}
    v = {n: given["v_" + n] for n in WEIGHTS}
    mx, my, mc = lax.axis_index("x"), lax.axis_index("y"), lax.axis_index("c")
    chip = 2 * mx + my
    dev = 2 * chip + mc
    xt, target = x[0], loss_target[0]

    def rk_rows(d):
        rows = _pack_rows([d[n] for n in RK_SHARDED], 256)
        return jnp.pad(rows, ((0, RK_ROWS_PAD - rows.shape[0]), (0, 0)))

    (c_all,) = _exchange("gather_c", [c], ALL_PEERS, 8)
    early = [n for n in BIG_SHARDED if n not in FFN_SHARDED]
    gathered = _gather_halves("gather_w", [w[n][0].astype(BF16) for n in early] + [rk_rows(w)])
    full = dict(zip(early, gathered[:len(early)]))
    rk_full = gathered[len(early)]

    (act,) = _rowwise("ada_act", lambda q: (q * _sigmoid(q),), [c_all.reshape(8, D_MODEL)], [], [(D_MODEL, F32)], 8)
    n_mod_cols = N_MOD * D_MODEL // N_CHIPS
    bias = jnp.broadcast_to(lax.dynamic_slice(ada_b, (0, chip * n_mod_cols), (1, n_mod_cols)), (8, n_mod_cols))
    mod_shard = _matmul("ada_fwd", act, ada_w[0], epilogue=_add_epilogue, extras=(bias,))
    (mod_parts,) = _exchange("gather_mod", [mod_shard], CHIP_PEERS, N_CHIPS)
    mod_all = _chips_to_cols(mod_parts)
    mod_mine = lax.dynamic_slice(mod_all, (dev, 0), (1, N_MOD * D_MODEL))
    mod = [mod_mine[:, i * D_MODEL:(i + 1) * D_MODEL] for i in range(N_MOD)]

    def rk_piece(lo, hi, lead):
        return _chips_to_cols(rk_full[:, lo:hi]).reshape(lead + (RK_WIDTH,))

    zeros = jnp.zeros((LORA, RK_WIDTH), F32)
    w_up, a_up = rk_piece(4, 132, (2, LORA)), rk_piece(132, 260, (2, LORA))
    wt = {
        "norm1_gain": norm1_gain, "norm2_gain": norm2_gain, "final_gain": final_gain.reshape(1, D_MODEL),
        "w_in": jnp.pad(_chips_to_cols(full["w_in"]), ((0, 0), (0, PROJ_PAD - PROJ))),
        "w_out": full["w_out"].reshape(D_MODEL, D_MODEL),
        "s5_w_glu": full["s5_w_glu"].reshape(S5_WIDTH, S5_WIDTH),
        "mu_prev": jnp.pad(rk_shift_prev, ((0, 0), (0, RK_PAD - RK_IN))),
        "mu_next": jnp.pad(rk_shift_next, ((0, 0), (0, RK_PAD - RK_IN))),
        "lam_re": [s5_lambda_re[0, d].reshape(S5_CH, 1) for d in range(2)],
        "lam_im": [s5_lambda_im[0, d].reshape(S5_CH, 1) for d in range(2)],
        "log_step": [jnp.repeat(s5_log_step[0, d], S5_STATE).reshape(S5_CH, 1) for d in range(2)],
        "b_re": s5_b_re.reshape(S5_CH, S5_GROUP), "b_im": s5_b_im.reshape(S5_CH, S5_GROUP),
        "c_re": s5_c_re[0], "c_im": s5_c_im[0],
        "s5_d": s5_d, "s5_b_glu": s5_b_glu,
        "w0": list(rk_piece(0, 2, (2,))[:, None, :]), "a0": list(rk_piece(2, 4, (2,))[:, None, :]),
        "w_up": [jnp.concatenate([w_up[0], zeros]), jnp.concatenate([zeros, w_up[1]])],
        "a_up": [jnp.concatenate([a_up[0], zeros]), jnp.concatenate([zeros, a_up[1]])],
        "g_up": jnp.pad(rk_piece(260, 420, (GATE_LORA,)), ((0, GATE_PAD - GATE_LORA), (0, 0))),
        "k_k": rk_k_k, "k_a": rk_k_a, "r_k": rk_r_k.reshape(1, RK_WIDTH),
        "ln_gain": rk_ln_gain, "ln_bias": rk_ln_bias,
    }

    ffn_shards = [w[n][0].astype(BF16) for n in FFN_SHARDED]
    loss_rows, grad_x, g = _local_step(xt, target, mod, wt, ffn_shards=ffn_shards)
    loss = lax.psum(jnp.sum(loss_rows), ("x", "y", "c"))


    big_grads = {
        "w_in": _cols_to_chips(g["w_in"][:, :PROJ], D_MODEL),
        "w_out": g["w_out"].reshape(N_CHIPS, -1, D_MODEL),
        "s5_w_glu": g["s5_w_glu"].reshape(N_CHIPS, -1, S5_WIDTH),
    }
    rk_grads = jnp.concatenate([
        _cols_to_chips(jnp.concatenate(g["w0"]), 2), _cols_to_chips(jnp.concatenate(g["a0"]), 2),
        _cols_to_chips(jnp.concatenate([g["w_up"][0][:LORA], g["w_up"][1][LORA:]]), 2 * LORA),
        _cols_to_chips(jnp.concatenate([g["a_up"][0][:LORA], g["a_up"][1][LORA:]]), 2 * LORA),
        _cols_to_chips(g["g_up"][:GATE_LORA], GATE_LORA),
        jnp.zeros((N_CHIPS, RK_ROWS_PAD - RK_ROWS, 256), F32)], axis=1)
    local_small = {
        "ada_b": jnp.concatenate(g["mod"], axis=1),
        "norm1_gain": g["norm1_gain"], "norm2_gain": g["norm2_gain"], "final_gain": g["final_gain"],
        "s5_lambda_re": jnp.concatenate(g["lam_re"]), "s5_lambda_im": jnp.concatenate(g["lam_im"]),
        "s5_log_step": jnp.concatenate([q.reshape(S5_GROUPS, S5_STATE).sum(axis=1) for q in g["log_step"]]),
        "s5_b_re": g["b_re"], "s5_b_im": g["b_im"], "s5_c_re": g["c_re"], "s5_c_im": g["c_im"],
        "s5_d": g["s5_d"], "s5_b_glu": g["s5_b_glu"],
        "rk_shift_prev": g["mu_prev"][:, :RK_IN], "rk_shift_next": g["mu_next"][:, :RK_IN],
        "rk_k_k": g["k_k"], "rk_k_a": g["k_a"], "rk_r_k": g["r_k"],
        "rk_ln_gain": g["ln_gain"], "rk_ln_bias": g["ln_bias"],
    }
    late = [n for n in BIG_SHARDED if n not in FFN_SHARDED]
    late_pieces = [big_grads[n] for n in late] + [rk_grads]
    late_sums = [_pair_sum("pair_" + n, piece, other, F32 if n == "rk" else BF16) for n, piece, other in
                 zip(late + ["rk"], late_pieces, _send_other_half("swap_halves_late", late_pieces))]
    arrived, (small_all,) = _run_plan("scatter_grads", _join_plans([
        _exchange_plan(late_sums, CHIP_PEERS, N_CHIPS, scatter=True),
        _exchange_plan([_pack_flat([local_small[n] for n in REPLICATED])], ALL_PEERS, 8)]))
    names = late + ["rk"] + FFN_SHARDED
    half_sums = [_sum_parts("sum_" + n, a) for n, a in zip(names, arrived + g["ffn_arrived"])]
    pairs = dict(zip(names, [p.reshape(1, 2 * p.shape[1], p.shape[2])
                             for p in _exchange("swap_sums", half_sums, CORE_PEER, 2)]))

    out = {}
    for n in BIG_SHARDED:
        res = _adamw("adamw_" + n, w[n][0], pairs[n], m[n][0], v[n][0])
        out[n] = [r[None] for r in res]
    rk_res = _adamw("adamw_rk", rk_rows(w), pairs["rk"], rk_rows(m), rk_rows(v))
    for q in range(4):
        pieces, pos = [], 0
        for n in RK_SHARDED:
            rows = w[n].size // 256
            pieces.append(rk_res[q][pos:pos + rows].reshape(w[n].shape))
            pos += rows
        for n, piece in zip(RK_SHARDED, pieces):
            out.setdefault(n, []).append(piece)

    small_res = _adamw("adamw_small", _pack_flat([w[n] for n in REPLICATED]), small_all,
                       _pack_flat([m[n] for n in REPLICATED]), _pack_flat([v[n] for n in REPLICATED]))
    for q in range(4):
        for n, piece in zip(REPLICATED, _unpack_flat(small_res[q], [w[n] for n in REPLICATED])):
            out.setdefault(n, []).append(piece)

    mod_rows = N_MOD * D_MODEL // PACK_COLS
    dmod_all = small_all[:, :mod_rows].reshape(8, N_MOD * D_MODEL)
    dmod = lax.dynamic_slice(dmod_all, (0, chip * n_mod_cols), (8, n_mod_cols))
    res = _ada_w_update(act.T, dmod, ada_w[0], m_ada_w[0], v_ada_w[0])
    out["ada_w"] = [r[None] for r in res]

    return (loss, grad_x[None], *[out[n][0] for n in WEIGHTS], *[out[n][1] for n in WEIGHTS],
            *[out[n][2] for n in WEIGHTS], *[out[n][3] for n in WEIGHTS])
```

```python
import functools
import math

import jax
import jax.numpy as jnp
from jax import lax
from jax.experimental import pallas as pl
from jax.experimental.pallas import tpu as pltpu

F32 = jnp.float32
BF16 = jnp.bfloat16

D_MODEL = 2048
S5_WIDTH = 1024
S5_GROUP = 16
S5_GROUPS = 64
S5_STATE = 64
S5_CH = S5_GROUPS * S5_STATE
S5_BLK = 256
RK_WIDTH = 1024
RK_HEAD = 64
RK_HEADS = 16
LORA = 64
GATE_LORA = 160
GATE_PAD = 256
RK_IN = 3488
RK_PAD = 3584
PROJ = 4512
PROJ_PAD = 4608
FFN = 8192
N_MOD = 6
NORM_EPS = 1e-6
GN_EPS = 64e-5
L2_EPS = 1e-12
RK_CHUNK = 32
RK_PASSES = {"solve": 3, "kt": 3, "s0": 1, "akk_v": 1, "ark_v": 1, "arb_u": 1, "state": 3}
LW_SCALE = math.exp(-0.5)
ADAM_LR, ADAM_B1, ADAM_B2, ADAM_EPS, ADAM_WD, ADAM_STEP = 0.001, 0.9, 0.999, 1e-08, 0.01, 10
VMEM_LIMIT = 56 * 1024 * 1024
HI = lax.Precision.HIGHEST


def _params(sem=None):
    return pltpu.CompilerParams(dimension_semantics=sem, vmem_limit_bytes=VMEM_LIMIT)


def _full(a):
    nd = a.ndim
    return pl.BlockSpec(a.shape, lambda *_: (0,) * nd)


@jax.custom_vjp
def _bdot(a, b):
    return jnp.dot(a.astype(BF16), b.astype(BF16), preferred_element_type=F32)


def _bdot_fwd(a, b):
    return _bdot(a, b), (a, b)


def _bdot_bwd(res, g):
    a, b = res
    gb = g.astype(BF16)
    da = lax.dot_general(gb, b.astype(BF16), (((1,), (1,)), ((), ())), preferred_element_type=F32)
    db = lax.dot_general(a.astype(BF16), gb, (((0,), (0,)), ((), ())), preferred_element_type=F32)
    return da, db


_bdot.defvjp(_bdot_fwd, _bdot_bwd)


@jax.custom_vjp
def _seg_dot(x, ind, ind_t):
    hi = x.astype(BF16)
    lo = (x - hi.astype(F32)).astype(BF16)
    both = jnp.dot(jnp.concatenate([hi, lo], axis=0), ind.astype(BF16), preferred_element_type=F32)
    return both[:x.shape[0]] + both[x.shape[0]:]


_seg_dot.defvjp(lambda x, ind, ind_t: (_seg_dot(x, ind, ind_t), (ind, ind_t)),
                lambda res, g: (_seg_dot(g, res[1], res[0]), jnp.zeros_like(res[0]), jnp.zeros_like(res[1])))


def _sigmoid(z):
    return 1.0 / (1.0 + jnp.exp(-z))


def _gelu(y):
    return 0.5 * y * (1.0 + jnp.tanh(0.7978845608028654 * (y + 0.044715 * (y * y * y))))


def _rms(x):
    return x * lax.rsqrt(jnp.mean(x * x, axis=-1, keepdims=True) + NORM_EPS)


def _tile(n, prefs):
    for t in prefs:
        if n % t == 0:
            return t
    return n


def _matmul(name, a, b, ta=False, tb=False, epilogue=None, extras=(), out_dtypes=(F32,), chips=None, hosted=None):
    m = a.shape[1] if ta else a.shape[0]
    k = a.shape[0] if ta else a.shape[1]
    if chips == "b":
        assert not tb and b.shape[1] == k
        n = N_CHIPS * b.shape[2]
    elif chips == "b_t":
        assert tb and N_CHIPS * b.shape[2] == k
        n = b.shape[1]
    else:
        n = b.shape[0] if tb else b.shape[1]
        assert k == (b.shape[1] if tb else b.shape[0]), (a.shape, b.shape, ta, tb)
    split = N_CHIPS if chips in ("b", "out") else 1
    tm = _tile(m, (1024, 512, 256, 128))
    tn = _tile(n // split, (1024, 768, 512, 256, 128))
    tk = k // N_CHIPS if chips == "b_t" else _tile(k, (2048, 1024, 512, 256, 128))
    nk = k // tk
    per = n // split // tn
    n_ex, n_out = len(extras), len(out_dtypes)
    dims = (((0 if ta else 1,), (1 if tb else 0,)), ((), ()))

    hosted = hosted or _NO_PLAN
    nh = len(hosted.arrays)
    grid = (m // tm, split, per, nk)

    def body(a_ref, b_ref, *rest):
        ex_refs, host_in = rest[:n_ex], rest[n_ex:n_ex + nh]
        out_refs, host_out = rest[n_ex + nh:n_ex + nh + n_out], rest[n_ex + nh + n_out:n_ex + 2 * nh + n_out]
        acc, sems = rest[n_ex + 2 * nh + n_out], rest[n_ex + 2 * nh + n_out + 1:]
        kk = pl.program_id(3)
        if nh:
            ids = [pl.program_id(d) for d in range(4)]
            first = functools.reduce(jnp.logical_and, [i == 0 for i in ids])
            last = functools.reduce(jnp.logical_and, [i == g - 1 for i, g in zip(ids, grid)])

            @pl.when(first)
            def _():
                hosted.start(host_in, host_out, sems)

        @pl.when(kk == 0)
        def _():
            acc[...] = jnp.zeros_like(acc)

        acc[...] += lax.dot_general(a_ref[...].astype(BF16), b_ref[...].astype(BF16), dims,
                                    preferred_element_type=F32)

        @pl.when(kk == nk - 1)
        def _():
            res = acc[...]
            outs = epilogue(res, *[e[...] for e in ex_refs]) if epilogue is not None else (res,)
            for o_ref, val in zip(out_refs, outs):
                o_ref[...] = val.astype(o_ref.dtype)

        if nh:
            @pl.when(last)
            def _():
                hosted.wait(host_in, host_out, sems)

    if ta:
        a_spec = pl.BlockSpec((tk, tm), lambda i, c, j, q: (q, i))
    else:
        a_spec = pl.BlockSpec((tm, tk), lambda i, c, j, q: (i, q))
    if chips == "b":
        b_spec = pl.BlockSpec((None, tk, tn), lambda i, c, j, q: (c, q, j))
    elif chips == "b_t":
        b_spec = pl.BlockSpec((None, tn, tk), lambda i, c, j, q: (q, j, 0))
    elif tb:
        b_spec = pl.BlockSpec((tn, tk), lambda i, c, j, q: (c * per + j, q))
    else:
        b_spec = pl.BlockSpec((tk, tn), lambda i, c, j, q: (q, c * per + j))
    mn_spec = pl.BlockSpec((tm, tn), lambda i, c, j, q: (i, c * per + j))
    if chips == "out":
        out_spec = pl.BlockSpec((None, tm, tn), lambda i, c, j, q: (c, i, j))
        out_shape = [jax.ShapeDtypeStruct((N_CHIPS, m, n // N_CHIPS), dt) for dt in out_dtypes]
    else:
        out_spec, out_shape = mn_spec, [jax.ShapeDtypeStruct((m, n), dt) for dt in out_dtypes]
    any_spec = pl.BlockSpec(memory_space=pl.ANY)
    order = ("arbitrary",) * 4 if nh else ("parallel", "parallel", "parallel", "arbitrary")
    outs = pl.pallas_call(
        body, name=name, grid=grid,
        in_specs=[a_spec, b_spec] + [mn_spec] * n_ex + [any_spec] * nh,
        out_specs=[out_spec] * n_out + [any_spec] * nh, out_shape=out_shape + hosted.out_shape,
        scratch_shapes=[pltpu.VMEM((tm, tn), F32)] + hosted.sems,
        compiler_params=_params(order),
    )(a, b, *extras, *hosted.arrays)
    res = outs[0] if n_out == 1 else outs[:n_out]
    return (res, hosted.finish(outs[n_out:])) if nh else res


def _row_spec(a, tm):
    return pl.BlockSpec((tm, a.shape[1]), lambda i: (i, 0))


def _rowwise(name, fn, rows, params, outs, tm):
    t = rows[0].shape[0]
    tm = min(tm, t)
    n_r, n_p = len(rows), len(params)

    def body(*refs):
        vals = [r[...] for r in refs[:n_r + n_p]]
        res = fn(*vals)
        for o_ref, val in zip(refs[n_r + n_p:], res):
            o_ref[...] = val.astype(o_ref.dtype)

    res = pl.pallas_call(
        body, name=name, grid=(t // tm,),
        in_specs=[_row_spec(r, tm) for r in rows] + [_full(p) for p in params],
        out_specs=[pl.BlockSpec((tm, n), lambda i: (i, 0)) for n, _ in outs],
        out_shape=[jax.ShapeDtypeStruct((t, n), dt) for n, dt in outs],
        compiler_params=_params(("parallel",)),
    )(*rows, *params)
    return res


def _rowwise_vjp(name, fn, rows, params, cts, row_grads, param_grads, tm, consts=(), addends=None,
                 emit=(), row_grad_dtypes=None):
    t = rows[0].shape[0]
    tm = min(tm, t)
    addends = addends or {}
    n_r, n_p, n_c = len(rows), len(params), len(consts)
    ct_flat = [c for group in cts for c in group]
    add_list = [addends[q] for q in sorted(addends)]
    n_ct, n_add = len(ct_flat), len(add_list)
    row_grad_dtypes = row_grad_dtypes or [F32] * len(row_grads)

    def body(*refs):
        pos = 0
        row_v = [r[...].astype(F32) for r in refs[pos:pos + n_r]]; pos += n_r
        par_v = [r[...].astype(F32) for r in refs[pos:pos + n_p]]; pos += n_p
        con_v = [r[...] for r in refs[pos:pos + n_c]]; pos += n_c
        ct_v = [r[...].astype(F32) for r in refs[pos:pos + n_ct]]; pos += n_ct
        add_v = [r[...] for r in refs[pos:pos + n_add]]; pos += n_add
        emit_refs = refs[pos:pos + len(emit)]; pos += len(emit)
        rg_refs = refs[pos:pos + len(row_grads)]; pos += len(row_grads)
        pg_refs = refs[pos:pos + len(param_grads)]

        def diff_fn(*dargs):
            rv, pv = list(row_v), list(par_v)
            for q, i in enumerate(row_grads):
                rv[i] = dargs[q]
            for q, j in enumerate(param_grads):
                pv[j] = dargs[len(row_grads) + q]
            return fn(*rv, *pv, *con_v)

        prim = [row_v[i] for i in row_grads] + [par_v[j] for j in param_grads]
        res, vjp = jax.vjp(diff_fn, *prim)
        ct_vals, q = [], 0
        for o, group in zip(res, cts):
            tot = jnp.zeros_like(o)
            for _ in group:
                tot = tot + ct_v[q]
                q += 1
            ct_vals.append(tot)
        grads = vjp(tuple(ct_vals))
        for e_ref, idx in zip(emit_refs, emit):
            e_ref[...] = res[idx].astype(e_ref.dtype)
        add_pos = {p: q for q, p in enumerate(sorted(addends))}
        for q, g_ref in enumerate(rg_refs):
            g = grads[q]
            if q in add_pos:
                g = g + add_v[add_pos[q]]
            g_ref[...] = g.astype(g_ref.dtype)

        @pl.when(pl.program_id(0) == 0)
        def _():
            for g_ref in pg_refs:
                g_ref[...] = jnp.zeros_like(g_ref)

        for q, g_ref in enumerate(pg_refs):
            g_ref[...] += grads[len(row_grads) + q]

    emit_shapes = []
    if emit:
        probe = jax.eval_shape(lambda *a: fn(*a), *[jax.ShapeDtypeStruct((tm, r.shape[1]), F32) for r in rows],
                               *[jax.ShapeDtypeStruct(p.shape, p.dtype) for p in params],
                               *[jax.ShapeDtypeStruct(c.shape, c.dtype) for c in consts])
        emit_shapes = [probe[idx].shape[1] for idx in emit]
    out_specs = ([pl.BlockSpec((tm, n), lambda i: (i, 0)) for n in emit_shapes]
                 + [_row_spec(rows[i], tm) for i in row_grads]
                 + [_full(params[j]) for j in param_grads])
    out_shape = ([jax.ShapeDtypeStruct((t, n), F32) for n in emit_shapes]
                 + [jax.ShapeDtypeStruct(rows[i].shape, dt) for i, dt in zip(row_grads, row_grad_dtypes)]
                 + [jax.ShapeDtypeStruct(params[j].shape, F32) for j in param_grads])
    return pl.pallas_call(
        body, name=name, grid=(t // tm,),
        in_specs=([_row_spec(r, tm) for r in rows] + [_full(p) for p in params] + [_full(c) for c in consts]
                  + [_row_spec(c, tm) for c in ct_flat] + [_row_spec(a, tm) for a in add_list]),
        out_specs=out_specs, out_shape=out_shape,
        compiler_params=_params(("arbitrary",)),
    )(*rows, *params, *consts, *ct_flat, *add_list)


def _norm_mod_fn(x, gain, scale, shift):
    return (_rms(x) * gain * (1.0 + scale) + shift,)


def _resid_norm_mod_fn(x, mixed, gate, gain, scale, shift):
    x1 = x + gate * mixed
    return x1, _rms(x1) * gain * (1.0 + scale) + shift


def _loss_fn(x1, ffn, target, gate, gain):
    y = _rms(x1 + gate * ffn) * gain
    err = y - target
    return (0.5 * jnp.mean(err * err, axis=-1, keepdims=True),)


def _s5_out_fn(ylin, u, d_skip, w_glu, b_glu):
    z = _gelu(ylin + d_skip * u)
    return (z * _sigmoid(_bdot(z, w_glu) + b_glu),)


def _rk_pre_fn(k, wdn, adn, gdn, w0_0, w0_1, wup_0, wup_1, a0_0, a0_1, aup_0, aup_1, g_up, k_k, k_a, seg, seg_t):
    kkr = k * k_k
    inv = 1.0 / jnp.sqrt(jnp.maximum(_seg_dot(kkr * kkr, seg, seg_t), L2_EPS * L2_EPS))
    kk = kkr * _seg_dot(inv, seg_t, seg)
    tw = jnp.tanh(wdn)
    lws, kds, acts = [], [], []
    for w0, wup, a0, aup in ((w0_0, wup_0, a0_0, aup_0), (w0_1, wup_1, a0_1, aup_1)):
        lws.append(-LW_SCALE * _sigmoid(w0 + _bdot(tw, wup)))
        act = _sigmoid(a0 + _bdot(adn, aup))
        acts.append(act)
        kds.append(k * (1.0 + (act - 1.0) * k_a))
    gate = _bdot(_sigmoid(gdn), g_up)
    return (kk, lws[0], lws[1], kds[0], kds[1], acts[0], acts[1], gate)


def _rk_post_fn(y0, y1, r, v, kd0, kd1, gate, ln_gain, ln_bias, r_k, seg, seg_t):
    y = y0 + y1
    mu = _seg_dot(_seg_dot(y, seg, seg_t) * (1.0 / RK_HEAD), seg_t, seg)
    yc = y - mu
    var = _seg_dot(yc * yc, seg, seg_t) * (1.0 / RK_HEAD)
    yn = yc * _seg_dot(lax.rsqrt(var + GN_EPS), seg_t, seg) * ln_gain + ln_bias
    bonus = _seg_dot(_seg_dot(r * (kd0 + kd1) * r_k, seg, seg_t), seg_t, seg)
    return ((yn + bonus * v) * gate,)


def _s5_prep_fn(lr0, li0, ls0, lr1, li1, ls1, b_re, b_im):
    outs = []
    for lam_re, lam_im, ls in ((lr0, li0, ls0), (lr1, li1, ls1)):
        step = jnp.exp(ls)
        mag = jnp.exp(lam_re * step)
        lbar_re = mag * jnp.cos(lam_im * step)
        lbar_im = mag * jnp.sin(lam_im * step)
        den = lam_re * lam_re + lam_im * lam_im
        nr = lbar_re - 1.0
        coef_re = (nr * lam_re + lbar_im * lam_im) / den
        coef_im = (lbar_im * lam_re - nr * lam_im) / den
        outs += [lbar_re, lbar_im, coef_re * b_re - coef_im * b_im, coef_re * b_im + coef_im * b_re]
    return tuple(outs)


def _shift_rows(x, down):
    t = x.shape[0]
    rows = lax.broadcasted_iota(jnp.int32, x.shape, 0)
    if down:
        return jnp.where(rows >= 1, pltpu.roll(x, 1, 0), 0.0)
    return jnp.where(rows < t - 1, pltpu.roll(x, t - 1, 0), 0.0)


def _token_shift(p, mu_prev, mu_next):
    t, n = p.shape

    def body(p_ref, mp_ref, mn_ref, o_ref):
        x = p_ref[...]
        o_ref[...] = x + mp_ref[...] * (_shift_rows(x, True) - x) + mn_ref[...] * (_shift_rows(x, False) - x)

    col = pl.BlockSpec((t, 128), lambda j: (0, j))
    par = pl.BlockSpec((1, 128), lambda j: (0, j))
    return pl.pallas_call(
        body, name="token_shift", grid=(n // 128,), in_specs=[col, par, par], out_specs=col,
        out_shape=jax.ShapeDtypeStruct((t, n), F32), compiler_params=_params(("parallel",)),
    )(p, mu_prev, mu_next)


def _token_shift_bwd(p, mu_prev, mu_next, dps):
    t, n = p.shape

    def body(p_ref, mp_ref, mn_ref, d_ref, dp_ref, dmp_ref, dmn_ref):
        x, d, mp, mn = p_ref[...], d_ref[...], mp_ref[...], mn_ref[...]
        dp_ref[...] = d * (1.0 - mp - mn) + _shift_rows(d * mp, False) + _shift_rows(d * mn, True)
        dmp_ref[...] = jnp.sum(d * (_shift_rows(x, True) - x), axis=0, keepdims=True)
        dmn_ref[...] = jnp.sum(d * (_shift_rows(x, False) - x), axis=0, keepdims=True)

    col = pl.BlockSpec((t, 128), lambda j: (0, j))
    par = pl.BlockSpec((1, 128), lambda j: (0, j))
    return pl.pallas_call(
        body, name="token_shift_bwd", grid=(n // 128,), in_specs=[col, par, par, col],
        out_specs=[col, par, par],
        out_shape=[jax.ShapeDtypeStruct((t, n), F32), jax.ShapeDtypeStruct((1, n), F32),
                   jax.ShapeDtypeStruct((1, n), F32)],
        compiler_params=_params(("parallel",)),
    )(p, mu_prev, mu_next, dps)


N_SEG = 32
S5_BLOCKS = 32
S5_PER_IN = 4


def _scan_in_place(sr_ref, si_ref, ar, ai, carry_ref, reverse):
    seg_len = sr_ref.shape[0] // N_SEG
    ng = N_SEG // 8

    def rows(i, grp):
        first = (seg_len - 1 - i if reverse else i) * N_SEG + 8 * grp
        return pl.ds(pl.multiple_of(first, 8), 8)

    zero = jnp.zeros((8, 128), F32)
    one = jnp.ones((8, 128), F32)

    def local(i, c):
        pr, pi = c[-2:]
        out = []
        for grp in range(ng):
            sr, si = c[2 * grp], c[2 * grp + 1]
            nr = ar * sr - ai * si + sr_ref[rows(i, grp), :]
            ni = ar * si + ai * sr + si_ref[rows(i, grp), :]
            sr_ref[rows(i, grp), :] = nr
            si_ref[rows(i, grp), :] = ni
            out += [nr, ni]
        return tuple(out) + (ar * pr - ai * pi, ar * pi + ai * pr)

    ends = lax.fori_loop(0, seg_len, local, (zero,) * (2 * ng) + (one, zero))
    qr, qi = ends[-2][0:1], ends[-1][0:1]
    order = list(range(N_SEG - 1, -1, -1)) if reverse else list(range(N_SEG))
    cr = jnp.zeros((1, 128), F32)
    ci = jnp.zeros((1, 128), F32)
    for j in order:
        carry_ref[j:j + 1, :] = cr
        carry_ref[N_SEG + j:N_SEG + j + 1, :] = ci
        grp, sub = divmod(j, 8)
        lr, li = ends[2 * grp][sub:sub + 1], ends[2 * grp + 1][sub:sub + 1]
        cr, ci = lr + qr * cr - qi * ci, li + qr * ci + qi * cr
    carries = [(carry_ref[8 * grp:8 * grp + 8, :], carry_ref[N_SEG + 8 * grp:N_SEG + 8 * grp + 8, :])
               for grp in range(ng)]

    def fix(i, c):
        pr, pi = c
        npr, npi = ar * pr - ai * pi, ar * pi + ai * pr
        for grp in range(ng):
            cr8, ci8 = carries[grp]
            sr_ref[rows(i, grp), :] = sr_ref[rows(i, grp), :] + npr * cr8 - npi * ci8
            si_ref[rows(i, grp), :] = si_ref[rows(i, grp), :] + npr * ci8 + npi * cr8
        return npr, npi

    lax.fori_loop(0, seg_len, fix, (one, zero))


def _interleave(x):
    t, c = x.shape
    return jnp.transpose(x.reshape(N_SEG, t // N_SEG, c), (1, 0, 2)).reshape(t, c)


def _deinterleave(x):
    t, c = x.shape
    return jnp.transpose(x.reshape(t // N_SEG, N_SEG, c), (1, 0, 2)).reshape(t, c)


def _step_neighbour(s, earlier):
    t = s.shape[0]
    rows = lax.broadcasted_iota(jnp.int32, s.shape, 0)
    if earlier:
        return jnp.where(rows >= N_SEG, pltpu.roll(s, N_SEG, 0),
                         jnp.where(rows >= 1, pltpu.roll(s, N_SEG + 1, 0), 0.0))
    return jnp.where(rows < t - N_SEG, pltpu.roll(s, t - N_SEG, 0),
                     jnp.where(rows < t - 1, pltpu.roll(s, t - N_SEG - 1, 0), 0.0))


def _dot_bf16(a, b, dims=(((1,), (0,)), ((), ()))):
    return lax.dot_general(a.astype(BF16), b.astype(BF16), dims, preferred_element_type=F32)


NT_DIMS = (((1,), (1,)), ((), ()))
TN_DIMS = (((0,), (0,)), ((), ()))


def _s5_specs(t):
    blk = pl.BlockSpec((None, t, 128), lambda i, q: (S5_PER_IN * i + q, 0, 0))
    mat = pl.BlockSpec((None, 128, 128), lambda i, q: (S5_PER_IN * i + q, 0, 0))
    vec = pl.BlockSpec((None, 1, 128), lambda i, q: (S5_PER_IN * i + q, 0, 0))
    chan = pl.BlockSpec((t, 128), lambda i, q: (0, i))
    return blk, mat, vec, chan


S5_GRID = (S5_BLOCKS // S5_PER_IN, S5_PER_IN)


def _s5_forward(name, u, b_re, b_im, l_re, l_im, reverse, other=None, c_re=None, c_im_neg=None):
    t = u.shape[0]
    project = other is not None
    blk, mat, vec, chan = _s5_specs(t)

    def body(*refs):
        u_ref, br_ref, bi_ref, lr_ref, li_ref = refs[:5]
        if project:
            or_ref, oi_ref, cr_ref, ci_ref, sr_ref, si_ref, y_ref, carry_ref = refs[5:]
        else:
            sr_ref, si_ref, carry_ref = refs[5:]
        uv = u_ref[...]
        sr_ref[...] = _dot_bf16(uv, br_ref[...])
        si_ref[...] = _dot_bf16(uv, bi_ref[...])
        ar = jnp.broadcast_to(lr_ref[...], (8, 128))
        ai = jnp.broadcast_to(li_ref[...], (8, 128))
        _scan_in_place(sr_ref, si_ref, ar, ai, carry_ref, reverse)
        if project:
            y = (_dot_bf16(sr_ref[...] + or_ref[...], cr_ref[...])
                 + _dot_bf16(si_ref[...] + oi_ref[...], ci_ref[...]))

            @pl.when(pl.program_id(1) == 0)
            def _():
                y_ref[...] = y

            @pl.when(pl.program_id(1) != 0)
            def _():
                y_ref[...] += y

    state = jax.ShapeDtypeStruct((S5_BLOCKS, t, 128), F32)
    ins = [u, b_re, b_im, l_re, l_im] + ([other[0], other[1], c_re, c_im_neg] if project else [])
    in_specs = [chan, mat, mat, vec, vec] + ([blk, blk, mat, mat] if project else [])
    return pl.pallas_call(
        body, name=name, grid=S5_GRID, in_specs=in_specs,
        out_specs=[blk, blk] + ([chan] if project else []),
        out_shape=[state, state] + ([jax.ShapeDtypeStruct((t, S5_WIDTH), F32)] if project else []),
        scratch_shapes=[pltpu.VMEM((2 * N_SEG, 128), F32)],
        compiler_params=_params(("arbitrary", "arbitrary")),
    )(*ins)


def _s5_backward(name, dy, u, du_in, states, other, b_re, b_im, c_re, c_im_neg, l_re, l_im, reverse):
    t = u.shape[0]
    with_c = other is not None
    blk, mat, vec, chan = _s5_specs(t)

    def body(*refs):
        dy_ref, u_ref, du_in_ref, sr_ref, si_ref = refs[:5]
        pos = 5
        if with_c:
            or_ref, oi_ref = refs[5:7]
            pos = 7
        br_ref, bi_ref, cr_ref, ci_ref, lr_ref, li_ref = refs[pos:pos + 6]
        outs = refs[pos + 6:]
        du_ref, dbr_ref, dbi_ref, dlr_ref, dli_ref = outs[:5]
        lam_r, lam_i, carry_ref = outs[-3:]
        dyv, uv = dy_ref[...], u_ref[...]
        lam_r[...] = _dot_bf16(dyv, cr_ref[...], NT_DIMS)
        lam_i[...] = _dot_bf16(dyv, ci_ref[...], NT_DIMS)
        ar = jnp.broadcast_to(lr_ref[...], (8, 128))
        ai = -jnp.broadcast_to(li_ref[...], (8, 128))
        _scan_in_place(lam_r, lam_i, ar, ai, carry_ref, not reverse)
        lr, li = lam_r[...], lam_i[...]
        pr, pi = _step_neighbour(sr_ref[...], not reverse), _step_neighbour(si_ref[...], not reverse)
        dlr_ref[...] = jnp.sum(lr * pr + li * pi, axis=0, keepdims=True)
        dli_ref[...] = jnp.sum(li * pr - lr * pi, axis=0, keepdims=True)
        dbr_ref[...] = _dot_bf16(uv, lr, TN_DIMS)
        dbi_ref[...] = _dot_bf16(uv, li, TN_DIMS)
        du = _dot_bf16(lr, br_ref[...], NT_DIMS) + _dot_bf16(li, bi_ref[...], NT_DIMS)

        @pl.when(pl.program_id(1) == 0)
        def _():
            du_ref[...] = du_in_ref[...] + du

        @pl.when(pl.program_id(1) != 0)
        def _():
            du_ref[...] += du

        if with_c:
            dcr_ref, dci_ref = outs[5:7]
            dcr_ref[...] = _dot_bf16(sr_ref[...] + or_ref[...], dyv, TN_DIMS)
            dci_ref[...] = _dot_bf16(si_ref[...] + oi_ref[...], dyv, TN_DIMS)

    mats = jax.ShapeDtypeStruct((S5_BLOCKS, 128, 128), F32)
    vecs = jax.ShapeDtypeStruct((S5_BLOCKS, 1, 128), F32)
    ins = [dy, u, du_in, states[0], states[1]] + ([other[0], other[1]] if with_c else [])
    ins += [b_re, b_im, c_re, c_im_neg, l_re, l_im]
    in_specs = [chan, chan, chan, blk, blk] + ([blk, blk] if with_c else []) + [mat] * 4 + [vec] * 2
    return pl.pallas_call(
        body, name=name, grid=S5_GRID, in_specs=in_specs,
        out_specs=[chan, mat, mat, vec, vec] + ([mat, mat] if with_c else []),
        out_shape=[jax.ShapeDtypeStruct((t, S5_WIDTH), F32), mats, mats, vecs, vecs] + ([mats, mats] if with_c else []),
        scratch_shapes=[pltpu.VMEM((t, 128), F32), pltpu.VMEM((t, 128), F32), pltpu.VMEM((2 * N_SEG, 128), F32)],
        compiler_params=_params(("arbitrary", "arbitrary")),
    )(*ins)


def _ein(passes, spec, a, b):
    if passes == 6:
        return jnp.einsum(spec, a, b, precision=HI, preferred_element_type=F32)
    a_hi, b_hi = a.astype(BF16), b.astype(BF16)
    if passes == 1:
        return jnp.einsum(spec, a_hi, b_hi, preferred_element_type=F32)
    a_lo = (a - a_hi.astype(F32)).astype(BF16)
    b_lo = (b - b_hi.astype(F32)).astype(BF16)
    cross = jnp.einsum(spec, a_hi, b_lo, preferred_element_type=F32)
    if spec.startswith('hik'):
        m = a.shape[1]
        stacked = jnp.einsum(spec, jnp.concatenate([a_hi, a_lo], axis=1), b_hi, preferred_element_type=F32)
        return stacked[:, :m] + stacked[:, m:] + cross
    return (jnp.einsum(spec, a_hi, b_hi, preferred_element_type=F32) + cross
            + jnp.einsum(spec, a_lo, b_hi, preferred_element_type=F32))


@jax.custom_vjp
def _tri_mm(tri, tri_t, z):
    return jnp.einsum('hik,hkj->hij', tri, z, precision=HI, preferred_element_type=F32)


def _tri_mm_bwd(res, g):
    tri, tri_t = res
    return jnp.zeros_like(tri), jnp.zeros_like(tri_t), _tri_mm(tri_t, tri, g)


_tri_mm.defvjp(lambda tri, tri_t, z: (_tri_mm(tri, tri_t, z), (tri, tri_t)), _tri_mm_bwd)


def _chunk_cumsum(lw, incl, incl_t):
    shape = (lw.shape[0],) + incl.shape
    return _tri_mm(jnp.broadcast_to(incl.astype(F32), shape), jnp.broadcast_to(incl_t.astype(F32), shape), lw)


@functools.partial(jax.custom_vjp, nondiff_argnums=(0,))
def _bmm(p, a, b):
    return _ein(p, 'hik,hkj->hij', a, b)


@functools.partial(jax.custom_vjp, nondiff_argnums=(0,))
def _bmm_nt(p, a, b):
    return _ein(p, 'hik,hjk->hij', a, b)


@functools.partial(jax.custom_vjp, nondiff_argnums=(0,))
def _bmm_tn(p, a, b):
    return _ein(p, 'hki,hkj->hij', a, b)


_bmm.defvjp(lambda p, a, b: (_bmm(p, a, b), (a, b)),
            lambda p, res, g: (_bmm_nt(p, g, res[1]), _bmm_tn(p, res[0], g)))
_bmm_nt.defvjp(lambda p, a, b: (_bmm_nt(p, a, b), (a, b)),
               lambda p, res, g: (_bmm(p, g, res[1]), _bmm_tn(p, g, res[0])))
_bmm_tn.defvjp(lambda p, a, b: (_bmm_tn(p, a, b), (a, b)),
               lambda p, res, g: (_bmm_nt(p, res[1], g), _bmm(p, res[0], g)))


@jax.custom_vjp
def _split_rows(x):
    c = x.shape[1] // 2
    return x[:, :c], x[:, c:]


_split_rows.defvjp(lambda x: (_split_rows(x), None), lambda _, g: (jnp.concatenate(g, axis=1),))


def _stack_rows(a, b):
    return jnp.concatenate([a, b], axis=1)


def _nilpotent_inverse(l_mat):
    c = l_mat.shape[1]
    ps = RK_PASSES["solve"]
    row = lax.broadcasted_iota(jnp.int32, (c, c), 0)
    col = lax.broadcasted_iota(jnp.int32, (c, c), 1)
    x = -l_mat
    inv = jnp.where(row == col, 1.0, 0.0) + x
    power = _bmm(ps, x, x)
    span = 2
    while 2 * span < c:
        step, power = _split_rows(_bmm(ps, _stack_rows(inv, power), power))
        inv = inv + step
        span *= 2
    return inv + _bmm(ps, inv, power)


@jax.custom_vjp
def _nilpotent_solve(l_mat, rhs):
    return _bmm(RK_PASSES["solve"], _nilpotent_inverse(l_mat), rhs)


def _nilpotent_solve_fwd(l_mat, rhs):
    inv = _nilpotent_inverse(l_mat)
    u = _bmm(RK_PASSES["solve"], inv, rhs)
    return u, (inv, u)


def _nilpotent_solve_bwd(res, g):
    inv, u = res
    d_rhs = _bmm_tn(RK_PASSES["solve"], inv, g)
    return -_bmm_nt(RK_PASSES["solve"], d_rhs, u), d_rhs


_nilpotent_solve.defvjp(_nilpotent_solve_fwd, _nilpotent_solve_bwd)


def _rk_chunk(s0, r, lw, k, v, kk, a, reverse):
    h, c, n = r.shape
    row = lax.broadcasted_iota(jnp.int32, (c, c), 0)
    col = lax.broadcasted_iota(jnp.int32, (c, c), 1)
    incl = (row <= col) if reverse else (row >= col)
    strict = (row < col) if reverse else (row > col)
    cum = _chunk_cumsum(lw, incl, (row >= col) if reverse else (row <= col))
    g_in = jnp.exp(cum)
    g_inv = jnp.exp(-cum)
    kap = kk * jnp.exp(cum - lw)
    beta = kk * a * g_inv
    kt = k * g_inv
    rt = r * g_in
    p, ps = RK_PASSES, RK_PASSES["solve"]
    both = _stack_rows(kap, rt)
    kap_beta, rt_beta = _split_rows(_bmm_nt(ps, both, beta))
    kap_kt, rt_kt = _split_rows(_bmm_nt(p["kt"], both, kt))
    kap_s0, rt_s0 = _split_rows(_bmm_nt(p["s0"], both, s0))
    l_mat = jnp.where(strict, kap_beta, 0.0)
    rhs = kap_s0 + _bmm(p["akk_v"], jnp.where(strict, kap_kt, 0.0), v)
    u = _nilpotent_solve(l_mat, rhs)
    y = (rt_s0 + _bmm(p["ark_v"], jnp.where(incl, rt_kt, 0.0), v)
         - _bmm(p["arb_u"], jnp.where(incl, rt_beta, 0.0), u))
    s1 = ((s0 + _bmm_tn(p["state"], _stack_rows(v, -u), _stack_rows(kt, beta)))
          * jnp.exp(jnp.sum(lw, axis=1, keepdims=True)))
    return y, s1


class _Plan:
    def __init__(self, arrays, out_shape, sems, start, wait, finish):
        self.arrays, self.out_shape, self.sems = list(arrays), list(out_shape), list(sems)
        self.start, self.wait, self.finish = start, wait, finish


_NO_PLAN = _Plan([], [], [], lambda *_: None, lambda *_: None, lambda outs: [])


def _join_plans(plans):
    def cut(seq, sizes):
        out, pos = [], 0
        for s in sizes:
            out.append(seq[pos:pos + s])
            pos += s
        return out

    n_arr, n_sem = [len(p.arrays) for p in plans], [len(p.sems) for p in plans]

    def run(which):
        def go(in_refs, out_refs, sems):
            for p, i, o, s in zip(plans, cut(in_refs, n_arr), cut(out_refs, n_arr), cut(sems, n_sem)):
                getattr(p, which)(i, o, s)
        return go

    return _Plan([a for p in plans for a in p.arrays], [s for p in plans for s in p.out_shape],
                 [s for p in plans for s in p.sems], run("start"), run("wait"),
                 lambda outs: [p.finish(o) for p, o in zip(plans, cut(outs, n_arr))])


def _split_heads(x):
    return jnp.stack([x[:, RK_HEAD * i:RK_HEAD * (i + 1)] for i in range(RK_HEADS)], axis=0)


def _store_heads(ref, x):
    for i in range(RK_HEADS):
        ref[:, RK_HEAD * i:RK_HEAD * (i + 1)] = x[i]


def _rk_core_fwd(name, r, lw, k, v, kk, a, reverse, chunk, hosted=None):
    t = r.shape[0]
    h, n = RK_HEADS, RK_HEAD
    nc = t // chunk

    def idx(i):
        return nc - 1 - i if reverse else i

    hosted = hosted or _NO_PLAN
    nh = len(hosted.arrays)

    def body(r_ref, lw_ref, k_ref, v_ref, kk_ref, a_ref, *rest):
        host_in, (y_ref, ck_ref), host_out = rest[:nh], rest[nh:nh + 2], rest[nh + 2:2 * nh + 2]
        s_ref, sems = rest[2 * nh + 2], rest[2 * nh + 3:]

        @pl.when(pl.program_id(0) == 0)
        def _():
            s_ref[...] = jnp.zeros_like(s_ref)
            hosted.start(host_in, host_out, sems)

        s0 = s_ref[...]
        ck_ref[0] = s0
        ops = [_split_heads(ref[...]) for ref in (r_ref, lw_ref, k_ref, v_ref, kk_ref, a_ref)]
        y, s1 = _rk_chunk(s0, *ops, reverse)
        _store_heads(y_ref, y)
        s_ref[...] = s1

        @pl.when(pl.program_id(0) == nc - 1)
        def _():
            hosted.wait(host_in, host_out, sems)

    blk = pl.BlockSpec((chunk, RK_WIDTH), lambda i: (idx(i), 0))
    any_spec = pl.BlockSpec(memory_space=pl.ANY)
    res = pl.pallas_call(
        body, name=name, grid=(nc,), in_specs=[blk] * 6 + [any_spec] * nh,
        out_specs=[blk, pl.BlockSpec((1, h, n, n), lambda i: (idx(i), 0, 0, 0))] + [any_spec] * nh,
        out_shape=[jax.ShapeDtypeStruct((t, RK_WIDTH), F32), jax.ShapeDtypeStruct((nc, h, n, n), F32)]
        + hosted.out_shape,
        scratch_shapes=[pltpu.VMEM((h, n, n), F32)] + hosted.sems,
        compiler_params=_params(("arbitrary",)),
    )(r, lw, k, v, kk, a, *hosted.arrays)
    return res[0], res[1], hosted.finish(res[2:])


def _rk_core_bwd(name, r, lw, k, v, kk, a, ck, dy, reverse, chunk, hosted=None):
    t = r.shape[0]
    h, n = RK_HEADS, RK_HEAD
    nc = t // chunk
    hosted = hosted or _NO_PLAN
    nh = len(hosted.arrays)

    def idx(i):
        return i if reverse else nc - 1 - i

    def body(r_ref, lw_ref, k_ref, v_ref, kk_ref, a_ref, ck_ref, dy_ref, *rest):
        host_in, out_refs, host_out = rest[:nh], rest[nh:nh + 6], rest[nh + 6:2 * nh + 6]
        ds_ref, sems = rest[2 * nh + 6], rest[2 * nh + 7:]

        @pl.when(pl.program_id(0) == 0)
        def _():
            ds_ref[...] = jnp.zeros_like(ds_ref)
            hosted.start(host_in, host_out, sems)

        fn = functools.partial(_rk_chunk, reverse=reverse)
        ops = [_split_heads(ref[...]) for ref in (r_ref, lw_ref, k_ref, v_ref, kk_ref, a_ref)]
        _, vjp = jax.vjp(fn, ck_ref[0], *ops)
        grads = vjp((_split_heads(dy_ref[...]), ds_ref[...]))
        ds_ref[...] = grads[0]
        for o_ref, g in zip(out_refs, grads[1:]):
            _store_heads(o_ref, g)

        @pl.when(pl.program_id(0) == nc - 1)
        def _():
            hosted.wait(host_in, host_out, sems)

    blk = pl.BlockSpec((chunk, RK_WIDTH), lambda i: (idx(i), 0))
    any_spec = pl.BlockSpec(memory_space=pl.ANY)
    res = pl.pallas_call(
        body, name=name, grid=(nc,),
        in_specs=[blk] * 6 + [pl.BlockSpec((1, h, n, n), lambda i: (idx(i), 0, 0, 0)), blk] + [any_spec] * nh,
        out_specs=[blk] * 6 + [any_spec] * nh,
        out_shape=[jax.ShapeDtypeStruct((t, RK_WIDTH), F32)] * 6 + hosted.out_shape,
        scratch_shapes=[pltpu.VMEM((h, n, n), F32)] + hosted.sems,
        compiler_params=_params(("arbitrary",)),
    )(r, lw, k, v, kk, a, ck, dy, *hosted.arrays)
    return res[:6], hosted.finish(res[6:])


def _s5_band_place():
    return jax.nn.one_hot(jnp.arange(S5_BLOCKS) % S5_PER_IN, S5_PER_IN, dtype=F32)


def _s5_in_blocks(bbar):
    b = jnp.transpose(bbar.reshape(S5_BLOCKS, 2, S5_STATE, S5_GROUP), (0, 1, 3, 2))
    band = jnp.einsum('jghp,gk->jghkp', b, jnp.eye(2, dtype=F32)).reshape(S5_BLOCKS, 32, 128)
    return jnp.einsum('jrc,jq->jqrc', band, _s5_band_place()).reshape(S5_BLOCKS, 128, 128)


def _s5_in_unblock(mats):
    band = jnp.einsum('jqrc,jq->jrc', mats.reshape(S5_BLOCKS, S5_PER_IN, 32, 128), _s5_band_place())
    diag = jnp.einsum('jghgp->jghp', band.reshape(S5_BLOCKS, 2, S5_GROUP, 2, S5_STATE))
    return jnp.transpose(diag, (0, 1, 3, 2)).reshape(S5_CH, S5_GROUP)


def _s5_out_blocks(c):
    ct = jnp.transpose(c.reshape(S5_BLOCKS, 2, S5_GROUP, S5_STATE), (0, 1, 3, 2))
    band = jnp.einsum('jgph,gk->jgpkh', ct, jnp.eye(2, dtype=F32)).reshape(S5_BLOCKS, 128, 32)
    return jnp.einsum('jrc,jq->jrqc', band, _s5_band_place()).reshape(S5_BLOCKS, 128, 128)


def _s5_out_unblock(mats):
    band = jnp.einsum('jrqc,jq->jrc', mats.reshape(S5_BLOCKS, 128, S5_PER_IN, 32), _s5_band_place())
    diag = jnp.einsum('jgpgh->jgph', band.reshape(S5_BLOCKS, 2, S5_STATE, 2, S5_GROUP))
    return jnp.transpose(diag, (0, 1, 3, 2)).reshape(S5_GROUPS, S5_GROUP, S5_STATE)


def _head_indicator():
    ch = lax.broadcasted_iota(jnp.int32, (RK_WIDTH, 128), 0) // RK_HEAD
    hd = lax.broadcasted_iota(jnp.int32, (RK_WIDTH, 128), 1)
    seg = (ch == hd).astype(F32)
    return seg, seg.T


def _add_epilogue(acc, e):
    return (acc + e,)


def _local_step(x, target, mod, wt, chunk=RK_CHUNK, ffn_shards=None, mixer_shards=None):
    t = x.shape[0]
    wt = dict(wt)
    sh1, sc1, gt1, sh2, sc2, gt2 = mod
    seg, seg_t = _head_indicator()
    g = {}

    (h1,) = _rowwise("norm1", _norm_mod_fn, [x], [wt["norm1_gain"], sc1, sh1], [(D_MODEL, BF16)], 256)
    if mixer_shards is None:
        proj = _matmul("proj", h1, wt["w_in"])
    else:
        proj, gathered = _matmul("proj", h1, wt["w_in"], hosted=_gather_halves_plan(mixer_shards[0]))
        wt.update(mixer_shards[1](gathered))
    u, p = proj[:, :S5_WIDTH], proj[:, S5_WIDTH:]
    ps = _token_shift(p, wt["mu_prev"], wt["mu_next"])
    r, k, v = ps[:, :1024], ps[:, 1024:2048], ps[:, 2048:3072]
    wdn, adn, gdn = ps[:, 3072:3200], ps[:, 3200:3328], ps[:, 3328:RK_PAD]

    prep_rows = [wt["lam_re"][0], wt["lam_im"][0], wt["log_step"][0], wt["lam_re"][1], wt["lam_im"][1],
                 wt["log_step"][1], wt["b_re"], wt["b_im"]]
    col1, col16 = (1, F32), (S5_GROUP, F32)
    prep = _rowwise("s5_prep", _s5_prep_fn, prep_rows, [], [col1, col1, col16, col16] * 2, 512)
    lbar = [tuple(prep[4 * d + q].reshape(S5_BLOCKS, 1, 128) for q in range(2)) for d in range(2)]
    b_blk = [tuple(_s5_in_blocks(prep[4 * d + 2 + q]) for q in range(2)) for d in range(2)]
    c_blk = (_s5_out_blocks(wt["c_re"]), -_s5_out_blocks(wt["c_im"]))
    u_il = _interleave(u)
    state0 = _s5_forward("s5_fwd0", u_il, *b_blk[0], *lbar[0], reverse=False)
    s1_re, s1_im, ylin_il = _s5_forward("s5_fwd1", u_il, *b_blk[1], *lbar[1], reverse=True, other=state0,
                                        c_re=c_blk[0], c_im_neg=c_blk[1])
    ylin = _deinterleave(ylin_il)
    states = [tuple(state0), (s1_re, s1_im)]
    s5_par = [wt["s5_d"], wt["s5_w_glu"], wt["s5_b_glu"]]
    (o_s5,) = _rowwise("s5_out", _s5_out_fn, [ylin, u], s5_par, [(S5_WIDTH, BF16)], 256)

    pre_par = [wt["w0"][0], wt["w0"][1], wt["w_up"][0], wt["w_up"][1], wt["a0"][0], wt["a0"][1],
               wt["a_up"][0], wt["a_up"][1], wt["g_up"], wt["k_k"], wt["k_a"]]
    pre = _rowwise("rk_pre", _rk_pre_fn, [k, wdn, adn, gdn], pre_par + [seg, seg_t], [(RK_WIDTH, F32)] * 8, 256)
    kk, lw, kd, act, gate = pre[0], pre[1:3], pre[3:5], pre[5:7], pre[7]
    core_in, ys, cks = [], [], []
    for d in range(2):
        ops = (r, lw[d], kd[d], v, kk, act[d])
        plan = _gather_halves_plan([ffn_shards[d]]) if ffn_shards is not None else None
        y, ck, gathered = _rk_core_fwd(f"rk_core{d}", *ops, reverse=(d == 1), chunk=min(chunk, t), hosted=plan)
        if gathered:
            wt["ffn_w1" if d == 0 else "ffn_w2"] = gathered[0] if d == 0 else gathered[0].reshape(FFN, D_MODEL)
        core_in.append(ops)
        ys.append(y)
        cks.append(ck)
    post_rows = [ys[0], ys[1], r, v, kd[0], kd[1], gate]
    post_par = [wt["ln_gain"], wt["ln_bias"], wt["r_k"]]
    (o_rk,) = _rowwise("rk_post", _rk_post_fn, post_rows, post_par + [seg, seg_t], [(RK_WIDTH, BF16)], 256)

    o = jnp.concatenate([o_s5, o_rk], axis=1)
    mixed = _matmul("mix_out", o, wt["w_out"])
    n2_par = [gt1, wt["norm2_gain"], sc2, sh2]
    x1, h2 = _rowwise("norm2", _resid_norm_mod_fn, [x, mixed], n2_par, [(D_MODEL, F32), (D_MODEL, BF16)], 256)
    f1, hid = _matmul("ffn1", h2, wt["ffn_w1"], out_dtypes=(F32, BF16), chips="b",
                      epilogue=lambda acc: (acc, jnp.square(jnp.maximum(acc, 0.0))))
    ffn = _matmul("ffn2", hid, wt["ffn_w2"])

    ones = jnp.ones((t, 1), F32)
    loss_rows, dx1, dffn, g_gt2, g["final_gain"] = _rowwise_vjp(
        "loss", _loss_fn, [x1, ffn, target], [gt2, wt["final_gain"]], [[ones]], [0, 1], [0, 1], 256, emit=(0,),
        row_grad_dtypes=[F32, BF16])
    df1 = _matmul("ffn2_dx", dffn, wt["ffn_w2"], tb=True, extras=(f1,), out_dtypes=(BF16,),
                  epilogue=lambda acc, f: (acc * (2.0 * jnp.maximum(f, 0.0)),))
    g["ffn_w2"] = _matmul("ffn2_dw", hid, dffn, ta=True)
    g["ffn_w1"] = _matmul("ffn1_dw", h2, df1, ta=True, chips="out")
    if ffn_shards is None:
        dh2 = _matmul("ffn1_dx", df1, wt["ffn_w1"], tb=True, chips="b_t")
    else:
        ffn_pieces = [g.pop("ffn_w1"), g.pop("ffn_w2").reshape(N_CHIPS, -1, D_MODEL)]
        dh2, from_sibling = _matmul("ffn1_dx", df1, wt["ffn_w1"], tb=True, chips="b_t",
                                    hosted=_other_half_plan(ffn_pieces))
        ffn_sums = [_pair_sum("pair_" + n, piece, other, BF16)
                    for n, piece, other in zip(FFN_SHARDED, ffn_pieces, from_sibling)]
    dx_a, dmixed, g_gt1, g["norm2_gain"], g_sc2, g_sh2 = _rowwise_vjp(
        "norm2_bwd", _resid_norm_mod_fn, [x, mixed], n2_par, [[dx1], [dh2]], [0, 1], [0, 1, 2, 3], 256,
        row_grad_dtypes=[F32, BF16])
    do = _matmul("mix_out_dx", dmixed, wt["w_out"], tb=True)
    g["w_out"] = _matmul("mix_out_dw", o, dmixed, ta=True)
    do_s5, do_rk = do[:, :S5_WIDTH], do[:, S5_WIDTH:]

    dylin, du, g["s5_d"], g["s5_w_glu"], g["s5_b_glu"] = _rowwise_vjp(
        "s5_out_bwd", _s5_out_fn, [ylin, u], s5_par, [[do_s5]], [0, 1], [0, 1, 2], 256)
    prep_cts = []
    dylin_il, du_il = _interleave(dylin), _interleave(du)
    for d in range(2):
        res = _s5_backward(f"s5_bwd{d}", dylin_il, u_il, du_il, states[d], states[1] if d == 0 else None,
                           *b_blk[d], *c_blk, *lbar[d], reverse=(d == 1))
        du_il, db_re, db_im, dl_re, dl_im = res[:5]
        if d == 0:
            g["c_re"], g["c_im"] = _s5_out_unblock(res[5]), -_s5_out_unblock(res[6])
        prep_cts += [[dl_re.reshape(S5_CH, 1)], [dl_im.reshape(S5_CH, 1)], [_s5_in_unblock(db_re)],
                     [_s5_in_unblock(db_im)]]
    du = _deinterleave(du_il)
    pg = _rowwise_vjp("s5_prep_bwd", _s5_prep_fn, prep_rows, [], prep_cts, list(range(8)), [], 512)
    g["lam_re"], g["lam_im"], g["log_step"] = (pg[0], pg[3]), (pg[1], pg[4]), (pg[2], pg[5])
    g["b_re"], g["b_im"] = pg[6], pg[7]

    pb = _rowwise_vjp("rk_post_bwd", _rk_post_fn, post_rows, post_par, [[do_rk]], [0, 2, 3, 4, 5, 6], [0, 1, 2],
                      128, consts=[seg, seg_t])
    dy, dr_b, dv_b, dkd_b, dgate = pb[0], pb[1], pb[2], pb[3:5], pb[5]
    g["ln_gain"], g["ln_bias"], g["r_k"] = pb[6], pb[7], pb[8]
    cg = []
    for d in range(2):
        plan = None
        if d == 0 and ffn_shards is not None:
            plan = _exchange_plan(ffn_sums, CHIP_PEERS, N_CHIPS, scatter=True)
        grads, arrived = _rk_core_bwd(f"rk_core{d}_bwd", *core_in[d], cks[d], dy, reverse=(d == 1),
                                      chunk=min(chunk, t), hosted=plan)
        if arrived:
            g["ffn_arrived"] = arrived
        cg.append(grads)
    pre_cts = [[cg[0][4], cg[1][4]], [cg[0][1]], [cg[1][1]], [cg[0][2], dkd_b[0]], [cg[1][2], dkd_b[1]],
               [cg[0][5]], [cg[1][5]], [dgate]]
    qb = _rowwise_vjp("rk_pre_bwd", _rk_pre_fn, [k, wdn, adn, gdn], pre_par, pre_cts, [0, 1, 2, 3],
                      list(range(11)), 128, consts=[seg, seg_t])
    dk, dwdn, dadn, dgdn = qb[:4]
    g["w0"], g["w_up"], g["a0"], g["a_up"] = (qb[4], qb[5]), (qb[6], qb[7]), (qb[8], qb[9]), (qb[10], qb[11])
    g["g_up"], g["k_k"], g["k_a"] = qb[12], qb[13], qb[14]
    dr, dv = _rowwise("rk_sum", lambda a, b, c, e, f, h: (a + b + c, e + f + h),
                      [cg[0][0], cg[1][0], dr_b, cg[0][3], cg[1][3], dv_b], [], [(RK_WIDTH, F32)] * 2, 256)
    dps = jnp.concatenate([dr, dk, dv, dwdn, dadn, dgdn], axis=1)
    dp, g["mu_prev"], g["mu_next"] = _token_shift_bwd(p, wt["mu_prev"], wt["mu_next"], dps)

    dproj = jnp.concatenate([du, dp], axis=1).astype(BF16)
    dh1 = _matmul("proj_dx", dproj, wt["w_in"], tb=True)
    g["w_in"] = _matmul("proj_dw", h1, dproj, ta=True)
    grad_x, g["norm1_gain"], g_sc1, g_sh1 = _rowwise_vjp(
        "norm1_bwd", _norm_mod_fn, [x], [wt["norm1_gain"], sc1, sh1], [[dh1]], [0], [0, 1, 2], 256,
        addends={0: dx_a})
    g["mod"] = [g_sh1, g_sc1, g_gt1, g_sh2, g_sc2, g_gt2]
    return loss_rows, grad_x, g


CHIP_PEERS = ((1, 0, 0), (0, 1, 0), (1, 1, 0))
ALL_PEERS = ((0, 0, 1), (0, 1, 0), (0, 1, 1), (1, 0, 0), (1, 0, 1), (1, 1, 0), (1, 1, 1))
CORE_PEER = ((0, 0, 1),)


def _exchange(name, arrays, peers, n_slots, scatter=False):
    return _run_plan(name, _exchange_plan(arrays, peers, n_slots, scatter))


def _run_plan(name, plan):
    na = len(plan.arrays)

    def body(*refs):
        plan.start(refs[:na], refs[na:2 * na], refs[2 * na:])
        plan.wait(refs[:na], refs[na:2 * na], refs[2 * na:])

    any_spec = pl.BlockSpec(memory_space=pl.ANY)
    return plan.finish(pl.pallas_call(
        body, name=name, in_specs=[any_spec] * na, out_specs=[any_spec] * na, out_shape=plan.out_shape,
        scratch_shapes=plan.sems,
    )(*plan.arrays))


def _exchange_plan(arrays, peers, n_slots, scatter=False):
    na, nm = len(arrays), len(peers)

    def ident(px, py, pc):
        return {8: 4 * px + 2 * py + pc, 4: 2 * px + py, 2: pc}[n_slots]

    def copies(in_refs, out_refs, sems):
        send_sems, recv_sems = sems
        x, y, c = lax.axis_index("x"), lax.axis_index("y"), lax.axis_index("c")
        me = ident(x, y, c)
        made = []
        for i in range(na):
            for j, (fx, fy, fc) in enumerate(peers):
                px, py, pc = (1 - x if fx else x), (1 - y if fy else y), (1 - c if fc else c)
                src = in_refs[i].at[ident(px, py, pc)] if scatter else in_refs[i]
                made.append(pltpu.make_async_remote_copy(
                    src_ref=src, dst_ref=out_refs[i].at[me],
                    send_sem=send_sems.at[i * nm + j], recv_sem=recv_sems.at[i * nm + j],
                    device_id=(px, py, pc), device_id_type=pl.DeviceIdType.MESH))
        return made

    def start(in_refs, out_refs, sems):
        for copy in copies(in_refs, out_refs, sems):
            copy.start()

    def wait(in_refs, out_refs, sems):
        for copy in copies(in_refs, out_refs, sems):
            copy.wait()

    def finish(outs):
        me = ident(lax.axis_index("x"), lax.axis_index("y"), lax.axis_index("c"))
        return [lax.dynamic_update_slice_in_dim(
            o, lax.dynamic_index_in_dim(a, me, 0, keepdims=True) if scatter else a[None], me, axis=0)
            for a, o in zip(arrays, outs)]

    out_shape = [jax.ShapeDtypeStruct(((n_slots,) + a.shape[1:]) if scatter else ((n_slots,) + a.shape), a.dtype)
                 for a in arrays]
    sems = [pltpu.SemaphoreType.DMA((na * nm,)), pltpu.SemaphoreType.DMA((na * nm,))]
    return _Plan(arrays, out_shape, sems, start, wait, finish)


def _gather_halves(name, arrays):
    return _run_plan(name, _gather_halves_plan(arrays))


def _gather_halves_plan(arrays):
    na = len(arrays)
    chips = ((1, 0), (0, 1), (1, 1))

    def over_ici(in_refs, out_refs, sems):
        ici_send, ici_recv = sems[:2]
        x, y, c = lax.axis_index("x"), lax.axis_index("y"), lax.axis_index("c")
        made = []
        for i in range(na):
            half = arrays[i].shape[0] // 2
            mine = pl.ds(pl.multiple_of(c * half, 8), half)
            for j, (fx, fy) in enumerate(chips):
                px, py = (1 - x if fx else x), (1 - y if fy else y)
                k = len(chips) * i + j
                made.append([pltpu.make_async_remote_copy(
                    src_ref=in_refs[i].at[mine], dst_ref=out_refs[i].at[chip, mine],
                    send_sem=ici_send.at[k], recv_sem=ici_recv.at[k],
                    device_id=(px, py, c), device_id_type=pl.DeviceIdType.MESH)
                    for chip in (2 * x + y, 2 * px + py)])
        return made

    def start(in_refs, out_refs, sems):
        for outgoing, _ in over_ici(in_refs, out_refs, sems):
            outgoing.start()

    def wait(in_refs, out_refs, sems):
        d2d_send, d2d_recv = sems[2:]
        x, y, c = lax.axis_index("x"), lax.axis_index("y"), lax.axis_index("c")
        pending = []
        ici = over_ici(in_refs, out_refs, sems)
        for i in range(na):
            half = arrays[i].shape[0] // 2
            mine = pl.ds(pl.multiple_of(c * half, 8), half)
            theirs = pl.ds(pl.multiple_of((1 - c) * half, 8), half)
            for j, (fx, fy) in enumerate(chips):
                px, py = (1 - x if fx else x), (1 - y if fy else y)
                k = len(chips) * i + j
                outgoing, landing = ici[k]
                landing.wait_recv()
                landed = out_refs[i].at[2 * px + py, mine]
                passed = pltpu.make_async_remote_copy(
                    src_ref=landed, dst_ref=landed, send_sem=d2d_send.at[k], recv_sem=d2d_recv.at[k],
                    device_id=(x, y, 1 - c), device_id_type=pl.DeviceIdType.MESH)
                passed.start()
                from_sibling = out_refs[i].at[2 * px + py, theirs]
                pending += [outgoing.wait_send, passed.wait_send, pltpu.make_async_remote_copy(
                    src_ref=from_sibling, dst_ref=from_sibling, send_sem=d2d_send.at[k], recv_sem=d2d_recv.at[k],
                    device_id=(x, y, 1 - c), device_id_type=pl.DeviceIdType.MESH).wait_recv]
        for done in pending:
            done()

    def finish(outs):
        me = 2 * lax.axis_index("x") + lax.axis_index("y")
        return [lax.dynamic_update_slice_in_dim(o, a[None], me, axis=0) for a, o in zip(arrays, outs)]

    out_shape = [jax.ShapeDtypeStruct((N_CHIPS,) + a.shape, a.dtype) for a in arrays]
    return _Plan(arrays, out_shape, [pltpu.SemaphoreType.DMA((na * len(chips),))] * 4, start, wait, finish)


def _send_other_half(name, arrays):
    return _run_plan(name, _other_half_plan(arrays))


def _other_half_plan(arrays):
    na = len(arrays)

    def copies(in_refs, out_refs, sems):
        send_sems, recv_sems = sems
        x, y, c = lax.axis_index("x"), lax.axis_index("y"), lax.axis_index("c")
        made = []
        for i in range(na):
            half = arrays[i].shape[1] // 2
            theirs = pl.ds(pl.multiple_of((1 - c) * half, 8), half)
            made.append(pltpu.make_async_remote_copy(
                src_ref=in_refs[i].at[:, theirs], dst_ref=out_refs[i], send_sem=send_sems.at[i],
                recv_sem=recv_sems.at[i], device_id=(x, y, 1 - c), device_id_type=pl.DeviceIdType.MESH))
        return made

    def start(in_refs, out_refs, sems):
        for copy in copies(in_refs, out_refs, sems):
            copy.start()

    def wait(in_refs, out_refs, sems):
        for copy in copies(in_refs, out_refs, sems):
            copy.wait()

    out_shape = [jax.ShapeDtypeStruct((a.shape[0], a.shape[1] // 2, a.shape[2]), a.dtype) for a in arrays]
    sems = [pltpu.SemaphoreType.DMA((na,)), pltpu.SemaphoreType.DMA((na,))]
    return _Plan(arrays, out_shape, sems, start, wait, list)


def _adam_math(w, g, m, v):
    m = ADAM_B1 * m + (1.0 - ADAM_B1) * g
    v = ADAM_B2 * v + (1.0 - ADAM_B2) * jnp.square(g)
    m_hat = m / (1.0 - ADAM_B1 ** ADAM_STEP)
    v_hat = v / (1.0 - ADAM_B2 ** ADAM_STEP)
    delta = -ADAM_LR * (m_hat / (jnp.sqrt(v_hat) + ADAM_EPS) + ADAM_WD * w)
    return delta, m, v


WHOLE_BLOCK_BYTES = 2 * 1024 * 1024


def _row_tile(r, c):
    return r if 4 * r * c <= WHOLE_BLOCK_BYTES else _tile(r, (256, 128, 64, 32, 16, 8))


def _sum_parts(name, parts):
    n, r, c = parts.shape
    tr = _row_tile(r, c)

    def body(p_ref, o_ref):
        tot = p_ref[0].astype(F32)
        for i in range(1, n):
            tot = tot + p_ref[i].astype(F32)
        o_ref[...] = tot

    return pl.pallas_call(
        body, name=name, grid=(r // tr,), in_specs=[pl.BlockSpec((n, tr, c), lambda i: (0, i, 0))],
        out_specs=pl.BlockSpec((tr, c), lambda i: (i, 0)), out_shape=jax.ShapeDtypeStruct((r, c), F32),
        compiler_params=_params(("parallel",)),
    )(parts)


def _pair_sum(name, piece, other, dtype):
    n, r, c = piece.shape
    half = r // 2
    tr = _row_tile(half, c)

    def body(lo_ref, hi_ref, other_ref, o_ref):
        own = jnp.where(lax.axis_index("c") == 0, lo_ref[...], hi_ref[...])
        o_ref[...] = (own + other_ref[...]).astype(o_ref.dtype)

    blk = pl.BlockSpec((None, tr, c), lambda j, i: (j, i, 0))
    return pl.pallas_call(
        body, name=name, grid=(n, half // tr),
        in_specs=[pl.BlockSpec((None, None, tr, c), lambda j, i: (j, 0, i, 0)),
                  pl.BlockSpec((None, None, tr, c), lambda j, i: (j, 1, i, 0)), blk],
        out_specs=blk, out_shape=jax.ShapeDtypeStruct((n, half, c), dtype),
        compiler_params=_params(("parallel", "parallel")),
    )(piece.reshape(n, 2, half, c), piece.reshape(n, 2, half, c), other)


def _adamw(name, w, parts, m, v):
    n, r, c = parts.shape
    tr = _row_tile(r, c)

    def body(w_ref, p_ref, m_ref, v_ref, g_ref, d_ref, nm_ref, nv_ref):
        g = p_ref[0]
        for i in range(1, n):
            g = g + p_ref[i]
        delta, nm, nv = _adam_math(w_ref[...], g, m_ref[...], v_ref[...])
        g_ref[...], d_ref[...], nm_ref[...], nv_ref[...] = g, delta, nm, nv

    blk = pl.BlockSpec((tr, c), lambda i: (i, 0))
    return pl.pallas_call(
        body, name=name, grid=(r // tr,),
        in_specs=[blk, pl.BlockSpec((n, tr, c), lambda i: (0, i, 0)), blk, blk], out_specs=[blk] * 4,
        out_shape=[jax.ShapeDtypeStruct((r, c), F32)] * 4, compiler_params=_params(("parallel",)),
    )(w, parts, m, v)


def _ada_w_update(act_t, dmod, w, m, v, hosted):
    r, c = w.shape
    nb = act_t.shape[1]
    tr, tc = 256, 1024
    grid = (r // tr, c // tc)
    nh = len(hosted.arrays)

    def body(a_ref, d_ref, w_ref, m_ref, v_ref, *rest):
        host_in, (g_ref, dl_ref, nm_ref, nv_ref) = rest[:nh], rest[nh:nh + 4]
        host_out, sems = rest[nh + 4:2 * nh + 4], rest[2 * nh + 4:]
        i, j = pl.program_id(0), pl.program_id(1)

        @pl.when(jnp.logical_and(i == 0, j == 0))
        def _():
            hosted.start(host_in, host_out, sems)

        a, dm = a_ref[...], d_ref[...]
        g = a[:, 0:1] * dm[0:1, :]
        for b in range(1, nb):
            g = g + a[:, b:b + 1] * dm[b:b + 1, :]
        delta, nm, nv = _adam_math(w_ref[...], g, m_ref[...], v_ref[...])
        g_ref[...], dl_ref[...], nm_ref[...], nv_ref[...] = g, delta, nm, nv

        @pl.when(jnp.logical_and(i == grid[0] - 1, j == grid[1] - 1))
        def _():
            hosted.wait(host_in, host_out, sems)

    blk = pl.BlockSpec((tr, tc), lambda i, j: (i, j))
    any_spec = pl.BlockSpec(memory_space=pl.ANY)
    res = pl.pallas_call(
        body, name="ada_w_update", grid=grid,
        in_specs=[pl.BlockSpec((tr, nb), lambda i, j: (i, 0)), pl.BlockSpec((nb, tc), lambda i, j: (0, j)),
                  blk, blk, blk] + [any_spec] * nh,
        out_specs=[blk] * 4 + [any_spec] * nh,
        out_shape=[jax.ShapeDtypeStruct((r, c), F32)] * 4 + hosted.out_shape,
        scratch_shapes=hosted.sems,
        compiler_params=_params(("arbitrary", "arbitrary")),
    )(act_t, dmod, w, m, v, *hosted.arrays)
    return res[:4], hosted.finish(res[4:])


WEIGHTS = ['ada_w', 'ada_b', 'norm1_gain', 'norm2_gain', 'final_gain', 'w_in', 'w_out', 's5_lambda_re',
           's5_lambda_im', 's5_log_step', 's5_b_re', 's5_b_im', 's5_c_re', 's5_c_im', 's5_d', 's5_w_glu',
           's5_b_glu', 'rk_shift_prev', 'rk_shift_next', 'rk_w0', 'rk_w_up', 'rk_a0', 'rk_a_up', 'rk_g_up',
           'rk_k_k', 'rk_k_a', 'rk_r_k', 'rk_ln_gain', 'rk_ln_bias', 'ffn_w1', 'ffn_w2']
BIG_SHARDED = ['w_in', 'w_out', 's5_w_glu', 'ffn_w1', 'ffn_w2']
FFN_SHARDED = ['ffn_w1', 'ffn_w2']
RK_SHARDED = ['rk_w0', 'rk_a0', 'rk_w_up', 'rk_a_up', 'rk_g_up']
REPLICATED = ['ada_b', 'norm1_gain', 'norm2_gain', 'final_gain', 's5_lambda_re', 's5_lambda_im', 's5_log_step',
              's5_b_re', 's5_b_im', 's5_c_re', 's5_c_im', 's5_d', 's5_b_glu', 'rk_shift_prev', 'rk_shift_next',
              'rk_k_k', 'rk_k_a', 'rk_r_k', 'rk_ln_gain', 'rk_ln_bias']
PACK_COLS = 1024
N_CHIPS = 4
RK_ROWS = 420
RK_ROWS_PAD = 432


def _pack_rows(arrays, cols):
    return jnp.concatenate([a.reshape(-1, cols) for a in arrays], axis=0)


def _pack_flat(arrays):
    flat = jnp.concatenate([a.reshape(-1) for a in arrays])
    rows = -(-flat.shape[0] // PACK_COLS)
    return jnp.pad(flat, (0, rows * PACK_COLS - flat.shape[0])).reshape(rows, PACK_COLS)


def _unpack_flat(packed, like):
    flat, out, pos = packed.reshape(-1), [], 0
    for a in like:
        out.append(flat[pos:pos + a.size].reshape(a.shape))
        pos += a.size
    return out


def _cols_to_chips(full, n_rows):
    return jnp.transpose(full.reshape(n_rows, N_CHIPS, -1), (1, 0, 2))


def _chips_to_cols(parts):
    return jnp.transpose(parts, (1, 0, 2)).reshape(parts.shape[1], -1)


def kernel(x, c, ada_w, ada_b, norm1_gain, norm2_gain, final_gain, w_in, w_out, s5_lambda_re, s5_lambda_im, s5_log_step, s5_b_re, s5_b_im, s5_c_re, s5_c_im, s5_d, s5_w_glu, s5_b_glu, rk_shift_prev, rk_shift_next, rk_w0, rk_w_up, rk_a0, rk_a_up, rk_g_up, rk_k_k, rk_k_a, rk_r_k, rk_ln_gain, rk_ln_bias, ffn_w1, ffn_w2, loss_target, m_ada_w, m_ada_b, m_norm1_gain, m_norm2_gain, m_final_gain, m_w_in, m_w_out, m_s5_lambda_re, m_s5_lambda_im, m_s5_log_step, m_s5_b_re, m_s5_b_im, m_s5_c_re, m_s5_c_im, m_s5_d, m_s5_w_glu, m_s5_b_glu, m_rk_shift_prev, m_rk_shift_next, m_rk_w0, m_rk_w_up, m_rk_a0, m_rk_a_up, m_rk_g_up, m_rk_k_k, m_rk_k_a, m_rk_r_k, m_rk_ln_gain, m_rk_ln_bias, m_ffn_w1, m_ffn_w2, v_ada_w, v_ada_b, v_norm1_gain, v_norm2_gain, v_final_gain, v_w_in, v_w_out, v_s5_lambda_re, v_s5_lambda_im, v_s5_log_step, v_s5_b_re, v_s5_b_im, v_s5_c_re, v_s5_c_im, v_s5_d, v_s5_w_glu, v_s5_b_glu, v_rk_shift_prev, v_rk_shift_next, v_rk_w0, v_rk_w_up, v_rk_a0, v_rk_a_up, v_rk_g_up, v_rk_k_k, v_rk_k_a, v_rk_r_k, v_rk_ln_gain, v_rk_ln_bias, v_ffn_w1, v_ffn_w2):
    given = dict(locals())
    w = {n: given[n] for n in WEIGHTS}
    m = {n: given["m_" + n] for n in WEIGHTS}
    v = {n: given["v_" + n] for n in WEIGHTS}
    mx, my, mc = lax.axis_index("x"), lax.axis_index("y"), lax.axis_index("c")
    chip = 2 * mx + my
    dev = 2 * chip + mc
    xt, target = x[0], loss_target[0]

    def rk_rows(d):
        rows = _pack_rows([d[n] for n in RK_SHARDED], 256)
        return jnp.pad(rows, ((0, RK_ROWS_PAD - rows.shape[0]), (0, 0)))

    (c_all,), (w_in_parts,) = _run_plan("gather_first", _join_plans([
        _exchange_plan([c], ALL_PEERS, 8), _gather_halves_plan([w_in[0].astype(BF16)])]))

    (act,) = _rowwise("ada_act", lambda q: (q * _sigmoid(q),), [c_all.reshape(8, D_MODEL)], [], [(D_MODEL, F32)], 8)
    n_mod_cols = N_MOD * D_MODEL // N_CHIPS
    bias = jnp.broadcast_to(lax.dynamic_slice(ada_b, (0, chip * n_mod_cols), (1, n_mod_cols)), (8, n_mod_cols))
    mod_shard = _matmul("ada_fwd", act, ada_w[0], epilogue=_add_epilogue, extras=(bias,))
    (mod_parts,) = _exchange("gather_mod", [mod_shard], CHIP_PEERS, N_CHIPS)
    mod_all = _chips_to_cols(mod_parts)
    mod_mine = lax.dynamic_slice(mod_all, (dev, 0), (1, N_MOD * D_MODEL))
    mod = [mod_mine[:, i * D_MODEL:(i + 1) * D_MODEL] for i in range(N_MOD)]

    def mixer_weights(parts):
        w_out_parts, glu_parts, rk_full = parts

        def rk_piece(lo, hi, lead):
            return _chips_to_cols(rk_full[:, lo:hi]).reshape(lead + (RK_WIDTH,))

        zeros = jnp.zeros((LORA, RK_WIDTH), F32)
        w_up, a_up = rk_piece(4, 132, (2, LORA)), rk_piece(132, 260, (2, LORA))
        return {
            "w_out": w_out_parts.reshape(D_MODEL, D_MODEL), "s5_w_glu": glu_parts.reshape(S5_WIDTH, S5_WIDTH),
            "w0": list(rk_piece(0, 2, (2,))[:, None, :]), "a0": list(rk_piece(2, 4, (2,))[:, None, :]),
            "w_up": [jnp.concatenate([w_up[0], zeros]), jnp.concatenate([zeros, w_up[1]])],
            "a_up": [jnp.concatenate([a_up[0], zeros]), jnp.concatenate([zeros, a_up[1]])],
            "g_up": jnp.pad(rk_piece(260, 420, (GATE_LORA,)), ((0, GATE_PAD - GATE_LORA), (0, 0))),
        }

    wt = {
        "norm1_gain": norm1_gain, "norm2_gain": norm2_gain, "final_gain": final_gain.reshape(1, D_MODEL),
        "w_in": jnp.pad(_chips_to_cols(w_in_parts), ((0, 0), (0, PROJ_PAD - PROJ))),
        "mu_prev": jnp.pad(rk_shift_prev, ((0, 0), (0, RK_PAD - RK_IN))),
        "mu_next": jnp.pad(rk_shift_next, ((0, 0), (0, RK_PAD - RK_IN))),
        "lam_re": [s5_lambda_re[0, d].reshape(S5_CH, 1) for d in range(2)],
        "lam_im": [s5_lambda_im[0, d].reshape(S5_CH, 1) for d in range(2)],
        "log_step": [jnp.repeat(s5_log_step[0, d], S5_STATE).reshape(S5_CH, 1) for d in range(2)],
        "b_re": s5_b_re.reshape(S5_CH, S5_GROUP), "b_im": s5_b_im.reshape(S5_CH, S5_GROUP),
        "c_re": s5_c_re[0], "c_im": s5_c_im[0],
        "s5_d": s5_d, "s5_b_glu": s5_b_glu,
        "k_k": rk_k_k, "k_a": rk_k_a, "r_k": rk_r_k.reshape(1, RK_WIDTH),
        "ln_gain": rk_ln_gain, "ln_bias": rk_ln_bias,
    }

    ffn_shards = [w[n][0].astype(BF16) for n in FFN_SHARDED]
    mixer_shards = [w_out[0].astype(BF16), s5_w_glu[0].astype(BF16), rk_rows(w)]
    loss_rows, grad_x, g = _local_step(xt, target, mod, wt, ffn_shards=ffn_shards,
                                       mixer_shards=(mixer_shards, mixer_weights))
    loss = lax.psum(jnp.sum(loss_rows), ("x", "y", "c"))


    big_grads = {
        "w_in": _cols_to_chips(g["w_in"][:, :PROJ], D_MODEL),
        "w_out": g["w_out"].reshape(N_CHIPS, -1, D_MODEL),
        "s5_w_glu": g["s5_w_glu"].reshape(N_CHIPS, -1, S5_WIDTH),
    }
    rk_grads = jnp.concatenate([
        _cols_to_chips(jnp.concatenate(g["w0"]), 2), _cols_to_chips(jnp.concatenate(g["a0"]), 2),
        _cols_to_chips(jnp.concatenate([g["w_up"][0][:LORA], g["w_up"][1][LORA:]]), 2 * LORA),
        _cols_to_chips(jnp.concatenate([g["a_up"][0][:LORA], g["a_up"][1][LORA:]]), 2 * LORA),
        _cols_to_chips(g["g_up"][:GATE_LORA], GATE_LORA),
        jnp.zeros((N_CHIPS, RK_ROWS_PAD - RK_ROWS, 256), F32)], axis=1)
    local_small = {
        "ada_b": jnp.concatenate(g["mod"], axis=1),
        "norm1_gain": g["norm1_gain"], "norm2_gain": g["norm2_gain"], "final_gain": g["final_gain"],
        "s5_lambda_re": jnp.concatenate(g["lam_re"]), "s5_lambda_im": jnp.concatenate(g["lam_im"]),
        "s5_log_step": jnp.concatenate([q.reshape(S5_GROUPS, S5_STATE).sum(axis=1) for q in g["log_step"]]),
        "s5_b_re": g["b_re"], "s5_b_im": g["b_im"], "s5_c_re": g["c_re"], "s5_c_im": g["c_im"],
        "s5_d": g["s5_d"], "s5_b_glu": g["s5_b_glu"],
        "rk_shift_prev": g["mu_prev"][:, :RK_IN], "rk_shift_next": g["mu_next"][:, :RK_IN],
        "rk_k_k": g["k_k"], "rk_k_a": g["k_a"], "rk_r_k": g["r_k"],
        "rk_ln_gain": g["ln_gain"], "rk_ln_bias": g["ln_bias"],
    }
    late = [n for n in BIG_SHARDED if n not in FFN_SHARDED]
    late_pieces = [big_grads[n] for n in late] + [rk_grads]
    late_sums = [_pair_sum("pair_" + n, piece, other, F32 if n == "rk" else BF16) for n, piece, other in
                 zip(late + ["rk"], late_pieces, _send_other_half("swap_halves_late", late_pieces))]
    arrived, (small_all,) = _run_plan("scatter_grads", _join_plans([
        _exchange_plan(late_sums, CHIP_PEERS, N_CHIPS, scatter=True),
        _exchange_plan([_pack_flat([local_small[n] for n in REPLICATED])], ALL_PEERS, 8)]))
    names = late + ["rk"] + FFN_SHARDED
    half_sums = [_sum_parts("sum_" + n, a) for n, a in zip(names, arrived + g["ffn_arrived"])]

    mod_rows = N_MOD * D_MODEL // PACK_COLS
    dmod_all = small_all[:, :mod_rows].reshape(8, N_MOD * D_MODEL)
    dmod = lax.dynamic_slice(dmod_all, (0, chip * n_mod_cols), (8, n_mod_cols))
    ada_res, swapped = _ada_w_update(act.T, dmod, ada_w[0], m_ada_w[0], v_ada_w[0],
                                     hosted=_exchange_plan(half_sums, CORE_PEER, 2))
    pairs = dict(zip(names, [p.reshape(1, 2 * p.shape[1], p.shape[2]) for p in swapped]))

    out = {"ada_w": [r[None] for r in ada_res]}
    for n in BIG_SHARDED:
        res = _adamw("adamw_" + n, w[n][0], pairs[n], m[n][0], v[n][0])
        out[n] = [r[None] for r in res]
    rk_res = _adamw("adamw_rk", rk_rows(w), pairs["rk"], rk_rows(m), rk_rows(v))
    for q in range(4):
        pieces, pos = [], 0
        for n in RK_SHARDED:
            rows = w[n].size // 256
            pieces.append(rk_res[q][pos:pos + rows].reshape(w[n].shape))
            pos += rows
        for n, piece in zip(RK_SHARDED, pieces):
            out.setdefault(n, []).append(piece)

    small_res = _adamw("adamw_small", _pack_flat([w[n] for n in REPLICATED]), small_all,
                       _pack_flat([m[n] for n in REPLICATED]), _pack_flat([v[n] for n in REPLICATED]))
    for q in range(4):
        for n, piece in zip(REPLICATED, _unpack_flat(small_res[q], [w[n] for n in REPLICATED])):
            out.setdefault(n, []).append(piece)

    return (loss, grad_x[None], *[out[n][0] for n in WEIGHTS], *[out[n][1] for n in WEIGHTS],
            *[out[n][2] for n in WEIGHTS], *[out[n][3] for n in WEIGHTS])
```

```python
import functools
import math

import jax
import jax.numpy as jnp
from jax import lax
from jax.experimental import pallas as pl
from jax.experimental.pallas import tpu as pltpu

F32 = jnp.float32
BF16 = jnp.bfloat16

D_MODEL = 2048
S5_WIDTH = 1024
S5_GROUP = 16
S5_GROUPS = 64
S5_STATE = 64
S5_CH = S5_GROUPS * S5_STATE
S5_BLK = 256
RK_WIDTH = 1024
RK_HEAD = 64
RK_HEADS = 16
LORA = 64
GATE_LORA = 160
GATE_PAD = 256
RK_IN = 3488
RK_PAD = 3584
PROJ = 4512
PROJ_PAD = 4608
FFN = 8192
N_MOD = 6
NORM_EPS = 1e-6
GN_EPS = 64e-5
L2_EPS = 1e-12
RK_CHUNK = 64
RK_PASSES = {"solve": 3, "kt": 3, "s0": 1, "akk_v": 1, "ark_v": 1, "arb_u": 1, "state": 3}
LW_SCALE = math.exp(-0.5)
ADAM_LR, ADAM_B1, ADAM_B2, ADAM_EPS, ADAM_WD, ADAM_STEP = 0.001, 0.9, 0.999, 1e-08, 0.01, 10
VMEM_LIMIT = 56 * 1024 * 1024
HI = lax.Precision.HIGHEST


def _params(sem=None):
    return pltpu.CompilerParams(dimension_semantics=sem, vmem_limit_bytes=VMEM_LIMIT)


def _full(a):
    nd = a.ndim
    return pl.BlockSpec(a.shape, lambda *_: (0,) * nd)


@jax.custom_vjp
def _bdot(a, b):
    return jnp.dot(a.astype(BF16), b.astype(BF16), preferred_element_type=F32)


def _bdot_fwd(a, b):
    return _bdot(a, b), (a, b)


def _bdot_bwd(res, g):
    a, b = res
    gb = g.astype(BF16)
    da = lax.dot_general(gb, b.astype(BF16), (((1,), (1,)), ((), ())), preferred_element_type=F32)
    db = lax.dot_general(a.astype(BF16), gb, (((0,), (0,)), ((), ())), preferred_element_type=F32)
    return da, db


_bdot.defvjp(_bdot_fwd, _bdot_bwd)


@jax.custom_vjp
def _seg_dot(x, ind, ind_t):
    hi = x.astype(BF16)
    lo = (x - hi.astype(F32)).astype(BF16)
    both = jnp.dot(jnp.concatenate([hi, lo], axis=0), ind.astype(BF16), preferred_element_type=F32)
    return both[:x.shape[0]] + both[x.shape[0]:]


_seg_dot.defvjp(lambda x, ind, ind_t: (_seg_dot(x, ind, ind_t), (ind, ind_t)),
                lambda res, g: (_seg_dot(g, res[1], res[0]), jnp.zeros_like(res[0]), jnp.zeros_like(res[1])))


def _sigmoid(z):
    return 1.0 / (1.0 + jnp.exp(-z))


def _gelu(y):
    return 0.5 * y * (1.0 + jnp.tanh(0.7978845608028654 * (y + 0.044715 * (y * y * y))))


def _rms(x):
    return x * lax.rsqrt(jnp.mean(x * x, axis=-1, keepdims=True) + NORM_EPS)


def _tile(n, prefs):
    for t in prefs:
        if n % t == 0:
            return t
    return n


def _matmul(name, a, b, ta=False, tb=False, epilogue=None, extras=(), out_dtypes=(F32,), chips=None, hosted=None):
    m = a.shape[1] if ta else a.shape[0]
    k = a.shape[0] if ta else a.shape[1]
    if chips == "b":
        assert not tb and b.shape[1] == k
        n = N_CHIPS * b.shape[2]
    elif chips == "b_t":
        assert tb and N_CHIPS * b.shape[2] == k
        n = b.shape[1]
    else:
        n = b.shape[0] if tb else b.shape[1]
        assert k == (b.shape[1] if tb else b.shape[0]), (a.shape, b.shape, ta, tb)
    split = N_CHIPS if chips in ("b", "out") else 1
    tm = _tile(m, (1024, 512, 256, 128))
    tn = _tile(n // split, (1024, 768, 512, 256, 128))
    tk = k // N_CHIPS if chips == "b_t" else _tile(k, (2048, 1024, 512, 256, 128))
    nk = k // tk
    per = n // split // tn
    n_ex, n_out = len(extras), len(out_dtypes)
    dims = (((0 if ta else 1,), (1 if tb else 0,)), ((), ()))

    hosted = hosted or _NO_PLAN
    nh = len(hosted.arrays)
    grid = (m // tm, split, per, nk)

    def body(a_ref, b_ref, *rest):
        ex_refs, host_in = rest[:n_ex], rest[n_ex:n_ex + nh]
        out_refs, host_out = rest[n_ex + nh:n_ex + nh + n_out], rest[n_ex + nh + n_out:n_ex + 2 * nh + n_out]
        acc, sems = rest[n_ex + 2 * nh + n_out], rest[n_ex + 2 * nh + n_out + 1:]
        kk = pl.program_id(3)
        if nh:
            ids = [pl.program_id(d) for d in range(4)]
            first = functools.reduce(jnp.logical_and, [i == 0 for i in ids])
            last = functools.reduce(jnp.logical_and, [i == g - 1 for i, g in zip(ids, grid)])

            @pl.when(first)
            def _():
                hosted.start(host_in, host_out, sems)

        @pl.when(kk == 0)
        def _():
            acc[...] = jnp.zeros_like(acc)

        acc[...] += lax.dot_general(a_ref[...].astype(BF16), b_ref[...].astype(BF16), dims,
                                    preferred_element_type=F32)

        @pl.when(kk == nk - 1)
        def _():
            res = acc[...]
            outs = epilogue(res, *[e[...] for e in ex_refs]) if epilogue is not None else (res,)
            for o_ref, val in zip(out_refs, outs):
                o_ref[...] = val.astype(o_ref.dtype)

        if nh:
            @pl.when(last)
            def _():
                hosted.wait(host_in, host_out, sems)

    if ta:
        a_spec = pl.BlockSpec((tk, tm), lambda i, c, j, q: (q, i))
    else:
        a_spec = pl.BlockSpec((tm, tk), lambda i, c, j, q: (i, q))
    if chips == "b":
        b_spec = pl.BlockSpec((None, tk, tn), lambda i, c, j, q: (c, q, j))
    elif chips == "b_t":
        b_spec = pl.BlockSpec((None, tn, tk), lambda i, c, j, q: (q, j, 0))
    elif tb:
        b_spec = pl.BlockSpec((tn, tk), lambda i, c, j, q: (c * per + j, q))
    else:
        b_spec = pl.BlockSpec((tk, tn), lambda i, c, j, q: (q, c * per + j))
    mn_spec = pl.BlockSpec((tm, tn), lambda i, c, j, q: (i, c * per + j))
    if chips == "out":
        out_spec = pl.BlockSpec((None, tm, tn), lambda i, c, j, q: (c, i, j))
        out_shape = [jax.ShapeDtypeStruct((N_CHIPS, m, n // N_CHIPS), dt) for dt in out_dtypes]
    else:
        out_spec, out_shape = mn_spec, [jax.ShapeDtypeStruct((m, n), dt) for dt in out_dtypes]
    any_spec = pl.BlockSpec(memory_space=pl.ANY)
    order = ("arbitrary",) * 4 if nh else ("parallel", "parallel", "parallel", "arbitrary")
    outs = pl.pallas_call(
        body, name=name, grid=grid,
        in_specs=[a_spec, b_spec] + [mn_spec] * n_ex + [any_spec] * nh,
        out_specs=[out_spec] * n_out + [any_spec] * nh, out_shape=out_shape + hosted.out_shape,
        scratch_shapes=[pltpu.VMEM((tm, tn), F32)] + hosted.sems,
        compiler_params=_params(order),
    )(a, b, *extras, *hosted.arrays)
    res = outs[0] if n_out == 1 else outs[:n_out]
    return (res, hosted.finish(outs[n_out:])) if nh else res


def _row_spec(a, tm):
    return pl.BlockSpec((tm, a.shape[1]), lambda i: (i, 0))


def _rowwise(name, fn, rows, params, outs, tm):
    t = rows[0].shape[0]
    tm = min(tm, t)
    n_r, n_p = len(rows), len(params)

    def body(*refs):
        vals = [r[...] for r in refs[:n_r + n_p]]
        res = fn(*vals)
        for o_ref, val in zip(refs[n_r + n_p:], res):
            o_ref[...] = val.astype(o_ref.dtype)

    res = pl.pallas_call(
        body, name=name, grid=(t // tm,),
        in_specs=[_row_spec(r, tm) for r in rows] + [_full(p) for p in params],
        out_specs=[pl.BlockSpec((tm, n), lambda i: (i, 0)) for n, _ in outs],
        out_shape=[jax.ShapeDtypeStruct((t, n), dt) for n, dt in outs],
        compiler_params=_params(("parallel",)),
    )(*rows, *params)
    return res


def _rowwise_vjp(name, fn, rows, params, cts, row_grads, param_grads, tm, consts=(), addends=None,
                 emit=(), row_grad_dtypes=None):
    t = rows[0].shape[0]
    tm = min(tm, t)
    addends = addends or {}
    n_r, n_p, n_c = len(rows), len(params), len(consts)
    ct_flat = [c for group in cts for c in group]
    add_list = [addends[q] for q in sorted(addends)]
    n_ct, n_add = len(ct_flat), len(add_list)
    row_grad_dtypes = row_grad_dtypes or [F32] * len(row_grads)

    def body(*refs):
        pos = 0
        row_v = [r[...].astype(F32) for r in refs[pos:pos + n_r]]; pos += n_r
        par_v = [r[...].astype(F32) for r in refs[pos:pos + n_p]]; pos += n_p
        con_v = [r[...] for r in refs[pos:pos + n_c]]; pos += n_c
        ct_v = [r[...].astype(F32) for r in refs[pos:pos + n_ct]]; pos += n_ct
        add_v = [r[...] for r in refs[pos:pos + n_add]]; pos += n_add
        emit_refs = refs[pos:pos + len(emit)]; pos += len(emit)
        rg_refs = refs[pos:pos + len(row_grads)]; pos += len(row_grads)
        pg_refs = refs[pos:pos + len(param_grads)]

        def diff_fn(*dargs):
            rv, pv = list(row_v), list(par_v)
            for q, i in enumerate(row_grads):
                rv[i] = dargs[q]
            for q, j in enumerate(param_grads):
                pv[j] = dargs[len(row_grads) + q]
            return fn(*rv, *pv, *con_v)

        prim = [row_v[i] for i in row_grads] + [par_v[j] for j in param_grads]
        res, vjp = jax.vjp(diff_fn, *prim)
        ct_vals, q = [], 0
        for o, group in zip(res, cts):
            tot = jnp.zeros_like(o)
            for _ in group:
                tot = tot + ct_v[q]
                q += 1
            ct_vals.append(tot)
        grads = vjp(tuple(ct_vals))
        for e_ref, idx in zip(emit_refs, emit):
            e_ref[...] = res[idx].astype(e_ref.dtype)
        add_pos = {p: q for q, p in enumerate(sorted(addends))}
        for q, g_ref in enumerate(rg_refs):
            g = grads[q]
            if q in add_pos:
                g = g + add_v[add_pos[q]]
            g_ref[...] = g.astype(g_ref.dtype)

        @pl.when(pl.program_id(0) == 0)
        def _():
            for g_ref in pg_refs:
                g_ref[...] = jnp.zeros_like(g_ref)

        for q, g_ref in enumerate(pg_refs):
            g_ref[...] += grads[len(row_grads) + q]

    emit_shapes = []
    if emit:
        probe = jax.eval_shape(lambda *a: fn(*a), *[jax.ShapeDtypeStruct((tm, r.shape[1]), F32) for r in rows],
                               *[jax.ShapeDtypeStruct(p.shape, p.dtype) for p in params],
                               *[jax.ShapeDtypeStruct(c.shape, c.dtype) for c in consts])
        emit_shapes = [probe[idx].shape[1] for idx in emit]
    out_specs = ([pl.BlockSpec((tm, n), lambda i: (i, 0)) for n in emit_shapes]
                 + [_row_spec(rows[i], tm) for i in row_grads]
                 + [_full(params[j]) for j in param_grads])
    out_shape = ([jax.ShapeDtypeStruct((t, n), F32) for n in emit_shapes]
                 + [jax.ShapeDtypeStruct(rows[i].shape, dt) for i, dt in zip(row_grads, row_grad_dtypes)]
                 + [jax.ShapeDtypeStruct(params[j].shape, F32) for j in param_grads])
    return pl.pallas_call(
        body, name=name, grid=(t // tm,),
        in_specs=([_row_spec(r, tm) for r in rows] + [_full(p) for p in params] + [_full(c) for c in consts]
                  + [_row_spec(c, tm) for c in ct_flat] + [_row_spec(a, tm) for a in add_list]),
        out_specs=out_specs, out_shape=out_shape,
        compiler_params=_params(("arbitrary",)),
    )(*rows, *params, *consts, *ct_flat, *add_list)


def _norm_mod_fn(x, gain, scale, shift):
    return (_rms(x) * gain * (1.0 + scale) + shift,)


def _resid_norm_mod_fn(x, mixed, gate, gain, scale, shift):
    x1 = x + gate * mixed
    return x1, _rms(x1) * gain * (1.0 + scale) + shift


def _loss_fn(x1, ffn, target, gate, gain):
    y = _rms(x1 + gate * ffn) * gain
    err = y - target
    return (0.5 * jnp.mean(err * err, axis=-1, keepdims=True),)


def _s5_out_fn(ylin, u, d_skip, w_glu, b_glu):
    z = _gelu(ylin + d_skip * u)
    return (z * _sigmoid(_bdot(z, w_glu) + b_glu),)


def _rk_pre_fn(k, wdn, adn, gdn, w0_0, w0_1, wup_0, wup_1, a0_0, a0_1, aup_0, aup_1, g_up, k_k, k_a, seg, seg_t):
    kkr = k * k_k
    inv = 1.0 / jnp.sqrt(jnp.maximum(_seg_dot(kkr * kkr, seg, seg_t), L2_EPS * L2_EPS))
    kk = kkr * _seg_dot(inv, seg_t, seg)
    tw = jnp.tanh(wdn)
    lws, kds, acts = [], [], []
    for w0, wup, a0, aup in ((w0_0, wup_0, a0_0, aup_0), (w0_1, wup_1, a0_1, aup_1)):
        lws.append(-LW_SCALE * _sigmoid(w0 + _bdot(tw, wup)))
        act = _sigmoid(a0 + _bdot(adn, aup))
        acts.append(act)
        kds.append(k * (1.0 + (act - 1.0) * k_a))
    gate = _bdot(_sigmoid(gdn), g_up)
    return (kk, lws[0], lws[1], kds[0], kds[1], acts[0], acts[1], gate)


def _rk_post_fn(y0, y1, r, v, kd0, kd1, gate, ln_gain, ln_bias, r_k, seg, seg_t):
    y = y0 + y1
    mu = _seg_dot(_seg_dot(y, seg, seg_t) * (1.0 / RK_HEAD), seg_t, seg)
    yc = y - mu
    var = _seg_dot(yc * yc, seg, seg_t) * (1.0 / RK_HEAD)
    yn = yc * _seg_dot(lax.rsqrt(var + GN_EPS), seg_t, seg) * ln_gain + ln_bias
    bonus = _seg_dot(_seg_dot(r * (kd0 + kd1) * r_k, seg, seg_t), seg_t, seg)
    return ((yn + bonus * v) * gate,)


def _s5_prep_fn(lr0, li0, ls0, lr1, li1, ls1, b_re, b_im):
    outs = []
    for lam_re, lam_im, ls in ((lr0, li0, ls0), (lr1, li1, ls1)):
        step = jnp.exp(ls)
        mag = jnp.exp(lam_re * step)
        lbar_re = mag * jnp.cos(lam_im * step)
        lbar_im = mag * jnp.sin(lam_im * step)
        den = lam_re * lam_re + lam_im * lam_im
        nr = lbar_re - 1.0
        coef_re = (nr * lam_re + lbar_im * lam_im) / den
        coef_im = (lbar_im * lam_re - nr * lam_im) / den
        outs += [lbar_re, lbar_im, coef_re * b_re - coef_im * b_im, coef_re * b_im + coef_im * b_re]
    return tuple(outs)


def _shift_rows(x, down):
    t = x.shape[0]
    rows = lax.broadcasted_iota(jnp.int32, x.shape, 0)
    if down:
        return jnp.where(rows >= 1, pltpu.roll(x, 1, 0), 0.0)
    return jnp.where(rows < t - 1, pltpu.roll(x, t - 1, 0), 0.0)


def _token_shift(p, mu_prev, mu_next):
    t, n = p.shape

    def body(p_ref, mp_ref, mn_ref, o_ref):
        x = p_ref[...]
        o_ref[...] = x + mp_ref[...] * (_shift_rows(x, True) - x) + mn_ref[...] * (_shift_rows(x, False) - x)

    col = pl.BlockSpec((t, 128), lambda j: (0, j))
    par = pl.BlockSpec((1, 128), lambda j: (0, j))
    return pl.pallas_call(
        body, name="token_shift", grid=(n // 128,), in_specs=[col, par, par], out_specs=col,
        out_shape=jax.ShapeDtypeStruct((t, n), F32), compiler_params=_params(("parallel",)),
    )(p, mu_prev, mu_next)


def _token_shift_bwd(p, mu_prev, mu_next, dps):
    t, n = p.shape

    def body(p_ref, mp_ref, mn_ref, d_ref, dp_ref, dmp_ref, dmn_ref):
        x, d, mp, mn = p_ref[...], d_ref[...], mp_ref[...], mn_ref[...]
        dp_ref[...] = d * (1.0 - mp - mn) + _shift_rows(d * mp, False) + _shift_rows(d * mn, True)
        dmp_ref[...] = jnp.sum(d * (_shift_rows(x, True) - x), axis=0, keepdims=True)
        dmn_ref[...] = jnp.sum(d * (_shift_rows(x, False) - x), axis=0, keepdims=True)

    col = pl.BlockSpec((t, 128), lambda j: (0, j))
    par = pl.BlockSpec((1, 128), lambda j: (0, j))
    return pl.pallas_call(
        body, name="token_shift_bwd", grid=(n // 128,), in_specs=[col, par, par, col],
        out_specs=[col, par, par],
        out_shape=[jax.ShapeDtypeStruct((t, n), F32), jax.ShapeDtypeStruct((1, n), F32),
                   jax.ShapeDtypeStruct((1, n), F32)],
        compiler_params=_params(("parallel",)),
    )(p, mu_prev, mu_next, dps)


N_SEG = 32
S5_BLOCKS = 32
S5_PER_IN = 4


def _scan_in_place(sr_ref, si_ref, ar, ai, carry_ref, reverse):
    seg_len = sr_ref.shape[0] // N_SEG
    ng = N_SEG // 8

    def rows(i, grp):
        first = (seg_len - 1 - i if reverse else i) * N_SEG + 8 * grp
        return pl.ds(pl.multiple_of(first, 8), 8)

    zero = jnp.zeros((8, 128), F32)
    one = jnp.ones((8, 128), F32)

    def local(i, c):
        pr, pi = c[-2:]
        out = []
        for grp in range(ng):
            sr, si = c[2 * grp], c[2 * grp + 1]
            nr = ar * sr - ai * si + sr_ref[rows(i, grp), :]
            ni = ar * si + ai * sr + si_ref[rows(i, grp), :]
            sr_ref[rows(i, grp), :] = nr
            si_ref[rows(i, grp), :] = ni
            out += [nr, ni]
        return tuple(out) + (ar * pr - ai * pi, ar * pi + ai * pr)

    ends = lax.fori_loop(0, seg_len, local, (zero,) * (2 * ng) + (one, zero))
    qr, qi = ends[-2][0:1], ends[-1][0:1]
    order = list(range(N_SEG - 1, -1, -1)) if reverse else list(range(N_SEG))
    cr = jnp.zeros((1, 128), F32)
    ci = jnp.zeros((1, 128), F32)
    for j in order:
        carry_ref[j:j + 1, :] = cr
        carry_ref[N_SEG + j:N_SEG + j + 1, :] = ci
        grp, sub = divmod(j, 8)
        lr, li = ends[2 * grp][sub:sub + 1], ends[2 * grp + 1][sub:sub + 1]
        cr, ci = lr + qr * cr - qi * ci, li + qr * ci + qi * cr
    carries = [(carry_ref[8 * grp:8 * grp + 8, :], carry_ref[N_SEG + 8 * grp:N_SEG + 8 * grp + 8, :])
               for grp in range(ng)]

    def fix(i, c):
        pr, pi = c
        npr, npi = ar * pr - ai * pi, ar * pi + ai * pr
        for grp in range(ng):
            cr8, ci8 = carries[grp]
            sr_ref[rows(i, grp), :] = sr_ref[rows(i, grp), :] + npr * cr8 - npi * ci8
            si_ref[rows(i, grp), :] = si_ref[rows(i, grp), :] + npr * ci8 + npi * cr8
        return npr, npi

    lax.fori_loop(0, seg_len, fix, (one, zero))


def _interleave(x):
    t, c = x.shape
    return jnp.transpose(x.reshape(N_SEG, t // N_SEG, c), (1, 0, 2)).reshape(t, c)


def _deinterleave(x):
    t, c = x.shape
    return jnp.transpose(x.reshape(t // N_SEG, N_SEG, c), (1, 0, 2)).reshape(t, c)


def _step_neighbour(s, earlier):
    t = s.shape[0]
    rows = lax.broadcasted_iota(jnp.int32, s.shape, 0)
    if earlier:
        return jnp.where(rows >= N_SEG, pltpu.roll(s, N_SEG, 0),
                         jnp.where(rows >= 1, pltpu.roll(s, N_SEG + 1, 0), 0.0))
    return jnp.where(rows < t - N_SEG, pltpu.roll(s, t - N_SEG, 0),
                     jnp.where(rows < t - 1, pltpu.roll(s, t - N_SEG - 1, 0), 0.0))


def _dot_bf16(a, b, dims=(((1,), (0,)), ((), ()))):
    return lax.dot_general(a.astype(BF16), b.astype(BF16), dims, preferred_element_type=F32)


NT_DIMS = (((1,), (1,)), ((), ()))
TN_DIMS = (((0,), (0,)), ((), ()))


def _s5_specs(t):
    blk = pl.BlockSpec((None, t, 128), lambda i, q: (S5_PER_IN * i + q, 0, 0))
    mat = pl.BlockSpec((None, 128, 128), lambda i, q: (S5_PER_IN * i + q, 0, 0))
    vec = pl.BlockSpec((None, 1, 128), lambda i, q: (S5_PER_IN * i + q, 0, 0))
    chan = pl.BlockSpec((t, 128), lambda i, q: (0, i))
    return blk, mat, vec, chan


S5_GRID = (S5_BLOCKS // S5_PER_IN, S5_PER_IN)


def _s5_forward(name, u, b_re, b_im, l_re, l_im, reverse, other=None, c_re=None, c_im_neg=None):
    t = u.shape[0]
    project = other is not None
    blk, mat, vec, chan = _s5_specs(t)

    def body(*refs):
        u_ref, br_ref, bi_ref, lr_ref, li_ref = refs[:5]
        if project:
            or_ref, oi_ref, cr_ref, ci_ref, sr_ref, si_ref, y_ref, carry_ref = refs[5:]
        else:
            sr_ref, si_ref, carry_ref = refs[5:]
        uv = u_ref[...]
        sr_ref[...] = _dot_bf16(uv, br_ref[...])
        si_ref[...] = _dot_bf16(uv, bi_ref[...])
        ar = jnp.broadcast_to(lr_ref[...], (8, 128))
        ai = jnp.broadcast_to(li_ref[...], (8, 128))
        _scan_in_place(sr_ref, si_ref, ar, ai, carry_ref, reverse)
        if project:
            y = (_dot_bf16(sr_ref[...] + or_ref[...], cr_ref[...])
                 + _dot_bf16(si_ref[...] + oi_ref[...], ci_ref[...]))

            @pl.when(pl.program_id(1) == 0)
            def _():
                y_ref[...] = y

            @pl.when(pl.program_id(1) != 0)
            def _():
                y_ref[...] += y

    state = jax.ShapeDtypeStruct((S5_BLOCKS, t, 128), F32)
    ins = [u, b_re, b_im, l_re, l_im] + ([other[0], other[1], c_re, c_im_neg] if project else [])
    in_specs = [chan, mat, mat, vec, vec] + ([blk, blk, mat, mat] if project else [])
    return pl.pallas_call(
        body, name=name, grid=S5_GRID, in_specs=in_specs,
        out_specs=[blk, blk] + ([chan] if project else []),
        out_shape=[state, state] + ([jax.ShapeDtypeStruct((t, S5_WIDTH), F32)] if project else []),
        scratch_shapes=[pltpu.VMEM((2 * N_SEG, 128), F32)],
        compiler_params=_params(("arbitrary", "arbitrary")),
    )(*ins)


def _s5_backward(name, dy, u, du_in, states, other, b_re, b_im, c_re, c_im_neg, l_re, l_im, reverse):
    t = u.shape[0]
    with_c = other is not None
    blk, mat, vec, chan = _s5_specs(t)

    def body(*refs):
        dy_ref, u_ref, du_in_ref, sr_ref, si_ref = refs[:5]
        pos = 5
        if with_c:
            or_ref, oi_ref = refs[5:7]
            pos = 7
        br_ref, bi_ref, cr_ref, ci_ref, lr_ref, li_ref = refs[pos:pos + 6]
        outs = refs[pos + 6:]
        du_ref, dbr_ref, dbi_ref, dlr_ref, dli_ref = outs[:5]
        lam_r, lam_i, carry_ref = outs[-3:]
        dyv, uv = dy_ref[...], u_ref[...]
        lam_r[...] = _dot_bf16(dyv, cr_ref[...], NT_DIMS)
        lam_i[...] = _dot_bf16(dyv, ci_ref[...], NT_DIMS)
        ar = jnp.broadcast_to(lr_ref[...], (8, 128))
        ai = -jnp.broadcast_to(li_ref[...], (8, 128))
        _scan_in_place(lam_r, lam_i, ar, ai, carry_ref, not reverse)
        lr, li = lam_r[...], lam_i[...]
        pr, pi = _step_neighbour(sr_ref[...], not reverse), _step_neighbour(si_ref[...], not reverse)
        dlr_ref[...] = jnp.sum(lr * pr + li * pi, axis=0, keepdims=True)
        dli_ref[...] = jnp.sum(li * pr - lr * pi, axis=0, keepdims=True)
        dbr_ref[...] = _dot_bf16(uv, lr, TN_DIMS)
        dbi_ref[...] = _dot_bf16(uv, li, TN_DIMS)
        du = _dot_bf16(lr, br_ref[...], NT_DIMS) + _dot_bf16(li, bi_ref[...], NT_DIMS)

        @pl.when(pl.program_id(1) == 0)
        def _():
            du_ref[...] = du_in_ref[...] + du

        @pl.when(pl.program_id(1) != 0)
        def _():
            du_ref[...] += du

        if with_c:
            dcr_ref, dci_ref = outs[5:7]
            dcr_ref[...] = _dot_bf16(sr_ref[...] + or_ref[...], dyv, TN_DIMS)
            dci_ref[...] = _dot_bf16(si_ref[...] + oi_ref[...], dyv, TN_DIMS)

    mats = jax.ShapeDtypeStruct((S5_BLOCKS, 128, 128), F32)
    vecs = jax.ShapeDtypeStruct((S5_BLOCKS, 1, 128), F32)
    ins = [dy, u, du_in, states[0], states[1]] + ([other[0], other[1]] if with_c else [])
    ins += [b_re, b_im, c_re, c_im_neg, l_re, l_im]
    in_specs = [chan, chan, chan, blk, blk] + ([blk, blk] if with_c else []) + [mat] * 4 + [vec] * 2
    return pl.pallas_call(
        body, name=name, grid=S5_GRID, in_specs=in_specs,
        out_specs=[chan, mat, mat, vec, vec] + ([mat, mat] if with_c else []),
        out_shape=[jax.ShapeDtypeStruct((t, S5_WIDTH), F32), mats, mats, vecs, vecs] + ([mats, mats] if with_c else []),
        scratch_shapes=[pltpu.VMEM((t, 128), F32), pltpu.VMEM((t, 128), F32), pltpu.VMEM((2 * N_SEG, 128), F32)],
        compiler_params=_params(("arbitrary", "arbitrary")),
    )(*ins)


def _ein(passes, spec, a, b):
    if passes == 6:
        return jnp.einsum(spec, a, b, precision=HI, preferred_element_type=F32)
    a_hi, b_hi = a.astype(BF16), b.astype(BF16)
    if passes == 1:
        return jnp.einsum(spec, a_hi, b_hi, preferred_element_type=F32)
    a_lo = (a - a_hi.astype(F32)).astype(BF16)
    b_lo = (b - b_hi.astype(F32)).astype(BF16)
    cross = jnp.einsum(spec, a_hi, b_lo, preferred_element_type=F32)
    if spec.startswith('hik'):
        m = a.shape[1]
        stacked = jnp.einsum(spec, jnp.concatenate([a_hi, a_lo], axis=1), b_hi, preferred_element_type=F32)
        return stacked[:, :m] + stacked[:, m:] + cross
    return (jnp.einsum(spec, a_hi, b_hi, preferred_element_type=F32) + cross
            + jnp.einsum(spec, a_lo, b_hi, preferred_element_type=F32))


@jax.custom_vjp
def _tri_mm(tri, tri_t, z):
    return jnp.einsum('hik,hkj->hij', tri, z, precision=HI, preferred_element_type=F32)


def _tri_mm_bwd(res, g):
    tri, tri_t = res
    return jnp.zeros_like(tri), jnp.zeros_like(tri_t), _tri_mm(tri_t, tri, g)


_tri_mm.defvjp(lambda tri, tri_t, z: (_tri_mm(tri, tri_t, z), (tri, tri_t)), _tri_mm_bwd)


def _chunk_cumsum(lw, incl, incl_t):
    shape = (lw.shape[0],) + incl.shape
    return _tri_mm(jnp.broadcast_to(incl.astype(F32), shape), jnp.broadcast_to(incl_t.astype(F32), shape), lw)


@functools.partial(jax.custom_vjp, nondiff_argnums=(0,))
def _bmm(p, a, b):
    return _ein(p, 'hik,hkj->hij', a, b)


@functools.partial(jax.custom_vjp, nondiff_argnums=(0,))
def _bmm_nt(p, a, b):
    return _ein(p, 'hik,hjk->hij', a, b)


@functools.partial(jax.custom_vjp, nondiff_argnums=(0,))
def _bmm_tn(p, a, b):
    return _ein(p, 'hki,hkj->hij', a, b)


_bmm.defvjp(lambda p, a, b: (_bmm(p, a, b), (a, b)),
            lambda p, res, g: (_bmm_nt(p, g, res[1]), _bmm_tn(p, res[0], g)))
_bmm_nt.defvjp(lambda p, a, b: (_bmm_nt(p, a, b), (a, b)),
               lambda p, res, g: (_bmm(p, g, res[1]), _bmm_tn(p, g, res[0])))
_bmm_tn.defvjp(lambda p, a, b: (_bmm_tn(p, a, b), (a, b)),
               lambda p, res, g: (_bmm_nt(p, res[1], g), _bmm(p, res[0], g)))


@jax.custom_vjp
def _split_rows(x):
    c = x.shape[1] // 2
    return x[:, :c], x[:, c:]


_split_rows.defvjp(lambda x: (_split_rows(x), None), lambda _, g: (jnp.concatenate(g, axis=1),))


def _stack_rows(a, b):
    return jnp.concatenate([a, b], axis=1)


def _nilpotent_inverse(l_mat):
    c = l_mat.shape[1]
    ps = RK_PASSES["solve"]
    row = lax.broadcasted_iota(jnp.int32, (c, c), 0)
    col = lax.broadcasted_iota(jnp.int32, (c, c), 1)
    x = -l_mat
    inv = jnp.where(row == col, 1.0, 0.0) + x
    power = _bmm(ps, x, x)
    span = 2
    while 2 * span < c:
        step, power = _split_rows(_bmm(ps, _stack_rows(inv, power), power))
        inv = inv + step
        span *= 2
    return inv + _bmm(ps, inv, power)


@jax.custom_vjp
def _nilpotent_solve(l_mat, rhs):
    return _bmm(RK_PASSES["solve"], _nilpotent_inverse(l_mat), rhs)


def _nilpotent_solve_fwd(l_mat, rhs):
    inv = _nilpotent_inverse(l_mat)
    u = _bmm(RK_PASSES["solve"], inv, rhs)
    return u, (inv, u)


def _nilpotent_solve_bwd(res, g):
    inv, u = res
    d_rhs = _bmm_tn(RK_PASSES["solve"], inv, g)
    return -_bmm_nt(RK_PASSES["solve"], d_rhs, u), d_rhs


_nilpotent_solve.defvjp(_nilpotent_solve_fwd, _nilpotent_solve_bwd)


def _rk_chunk(s0, r, lw, k, v, kk, a, reverse):
    h, c, n = r.shape
    row = lax.broadcasted_iota(jnp.int32, (c, c), 0)
    col = lax.broadcasted_iota(jnp.int32, (c, c), 1)
    incl = (row <= col) if reverse else (row >= col)
    strict = (row < col) if reverse else (row > col)
    cum = _chunk_cumsum(lw, incl, (row >= col) if reverse else (row <= col))
    g_in = jnp.exp(cum)
    g_inv = jnp.exp(-cum)
    kap = kk * jnp.exp(cum - lw)
    beta = kk * a * g_inv
    kt = k * g_inv
    rt = r * g_in
    p, ps = RK_PASSES, RK_PASSES["solve"]
    both = _stack_rows(kap, rt)
    kap_beta, rt_beta = _split_rows(_bmm_nt(ps, both, beta))
    kap_kt, rt_kt = _split_rows(_bmm_nt(p["kt"], both, kt))
    kap_s0, rt_s0 = _split_rows(_bmm_nt(p["s0"], both, s0))
    l_mat = jnp.where(strict, kap_beta, 0.0)
    rhs = kap_s0 + _bmm(p["akk_v"], jnp.where(strict, kap_kt, 0.0), v)
    u = _nilpotent_solve(l_mat, rhs)
    y = (rt_s0 + _bmm(p["ark_v"], jnp.where(incl, rt_kt, 0.0), v)
         - _bmm(p["arb_u"], jnp.where(incl, rt_beta, 0.0), u))
    s1 = ((s0 + _bmm_tn(p["state"], _stack_rows(v, -u), _stack_rows(kt, beta)))
          * jnp.exp(jnp.sum(lw, axis=1, keepdims=True)))
    return y, s1


class _Plan:
    def __init__(self, arrays, out_shape, sems, start, wait, finish):
        self.arrays, self.out_shape, self.sems = list(arrays), list(out_shape), list(sems)
        self.start, self.wait, self.finish = start, wait, finish


_NO_PLAN = _Plan([], [], [], lambda *_: None, lambda *_: None, lambda outs: [])


def _join_plans(plans):
    def cut(seq, sizes):
        out, pos = [], 0
        for s in sizes:
            out.append(seq[pos:pos + s])
            pos += s
        return out

    n_arr, n_sem = [len(p.arrays) for p in plans], [len(p.sems) for p in plans]

    def run(which):
        def go(in_refs, out_refs, sems):
            for p, i, o, s in zip(plans, cut(in_refs, n_arr), cut(out_refs, n_arr), cut(sems, n_sem)):
                getattr(p, which)(i, o, s)
        return go

    return _Plan([a for p in plans for a in p.arrays], [s for p in plans for s in p.out_shape],
                 [s for p in plans for s in p.sems], run("start"), run("wait"),
                 lambda outs: [p.finish(o) for p, o in zip(plans, cut(outs, n_arr))])


def _split_heads(x):
    return jnp.stack([x[:, RK_HEAD * i:RK_HEAD * (i + 1)] for i in range(RK_HEADS)], axis=0)


def _store_heads(ref, x):
    for i in range(RK_HEADS):
        ref[:, RK_HEAD * i:RK_HEAD * (i + 1)] = x[i]


def _rk_core_fwd(name, r, lw, k, v, kk, a, reverse, chunk, hosted=None):
    t = r.shape[0]
    h, n = RK_HEADS, RK_HEAD
    nc = t // chunk

    def idx(i):
        return nc - 1 - i if reverse else i

    hosted = hosted or _NO_PLAN
    nh = len(hosted.arrays)

    def body(r_ref, lw_ref, k_ref, v_ref, kk_ref, a_ref, *rest):
        host_in, (y_ref, ck_ref), host_out = rest[:nh], rest[nh:nh + 2], rest[nh + 2:2 * nh + 2]
        s_ref, sems = rest[2 * nh + 2], rest[2 * nh + 3:]

        @pl.when(pl.program_id(0) == 0)
        def _():
            s_ref[...] = jnp.zeros_like(s_ref)
            hosted.start(host_in, host_out, sems)

        s0 = s_ref[...]
        ck_ref[0] = s0
        ops = [_split_heads(ref[...]) for ref in (r_ref, lw_ref, k_ref, v_ref, kk_ref, a_ref)]
        y, s1 = _rk_chunk(s0, *ops, reverse)
        _store_heads(y_ref, y)
        s_ref[...] = s1

        @pl.when(pl.program_id(0) == nc - 1)
        def _():
            hosted.wait(host_in, host_out, sems)

    blk = pl.BlockSpec((chunk, RK_WIDTH), lambda i: (idx(i), 0))
    any_spec = pl.BlockSpec(memory_space=pl.ANY)
    res = pl.pallas_call(
        body, name=name, grid=(nc,), in_specs=[blk] * 6 + [any_spec] * nh,
        out_specs=[blk, pl.BlockSpec((1, h, n, n), lambda i: (idx(i), 0, 0, 0))] + [any_spec] * nh,
        out_shape=[jax.ShapeDtypeStruct((t, RK_WIDTH), F32), jax.ShapeDtypeStruct((nc, h, n, n), F32)]
        + hosted.out_shape,
        scratch_shapes=[pltpu.VMEM((h, n, n), F32)] + hosted.sems,
        compiler_params=_params(("arbitrary",)),
    )(r, lw, k, v, kk, a, *hosted.arrays)
    return res[0], res[1], hosted.finish(res[2:])


def _rk_core_bwd(name, r, lw, k, v, kk, a, ck, dy, reverse, chunk, hosted=None):
    t = r.shape[0]
    h, n = RK_HEADS, RK_HEAD
    nc = t // chunk
    hosted = hosted or _NO_PLAN
    nh = len(hosted.arrays)

    def idx(i):
        return i if reverse else nc - 1 - i

    def body(r_ref, lw_ref, k_ref, v_ref, kk_ref, a_ref, ck_ref, dy_ref, *rest):
        host_in, out_refs, host_out = rest[:nh], rest[nh:nh + 6], rest[nh + 6:2 * nh + 6]
        ds_ref, sems = rest[2 * nh + 6], rest[2 * nh + 7:]

        @pl.when(pl.program_id(0) == 0)
        def _():
            ds_ref[...] = jnp.zeros_like(ds_ref)
            hosted.start(host_in, host_out, sems)

        fn = functools.partial(_rk_chunk, reverse=reverse)
        ops = [_split_heads(ref[...]) for ref in (r_ref, lw_ref, k_ref, v_ref, kk_ref, a_ref)]
        _, vjp = jax.vjp(fn, ck_ref[0], *ops)
        grads = vjp((_split_heads(dy_ref[...]), ds_ref[...]))
        ds_ref[...] = grads[0]
        for o_ref, g in zip(out_refs, grads[1:]):
            _store_heads(o_ref, g)

        @pl.when(pl.program_id(0) == nc - 1)
        def _():
            hosted.wait(host_in, host_out, sems)

    blk = pl.BlockSpec((chunk, RK_WIDTH), lambda i: (idx(i), 0))
    any_spec = pl.BlockSpec(memory_space=pl.ANY)
    res = pl.pallas_call(
        body, name=name, grid=(nc,),
        in_specs=[blk] * 6 + [pl.BlockSpec((1, h, n, n), lambda i: (idx(i), 0, 0, 0)), blk] + [any_spec] * nh,
        out_specs=[blk] * 6 + [any_spec] * nh,
        out_shape=[jax.ShapeDtypeStruct((t, RK_WIDTH), F32)] * 6 + hosted.out_shape,
        scratch_shapes=[pltpu.VMEM((h, n, n), F32)] + hosted.sems,
        compiler_params=_params(("arbitrary",)),
    )(r, lw, k, v, kk, a, ck, dy, *hosted.arrays)
    return res[:6], hosted.finish(res[6:])


def _s5_band_place():
    return jax.nn.one_hot(jnp.arange(S5_BLOCKS) % S5_PER_IN, S5_PER_IN, dtype=F32)


def _s5_in_blocks(bbar):
    b = jnp.transpose(bbar.reshape(S5_BLOCKS, 2, S5_STATE, S5_GROUP), (0, 1, 3, 2))
    band = jnp.einsum('jghp,gk->jghkp', b, jnp.eye(2, dtype=F32)).reshape(S5_BLOCKS, 32, 128)
    return jnp.einsum('jrc,jq->jqrc', band, _s5_band_place()).reshape(S5_BLOCKS, 128, 128)


def _s5_in_unblock(mats):
    band = jnp.einsum('jqrc,jq->jrc', mats.reshape(S5_BLOCKS, S5_PER_IN, 32, 128), _s5_band_place())
    diag = jnp.einsum('jghgp->jghp', band.reshape(S5_BLOCKS, 2, S5_GROUP, 2, S5_STATE))
    return jnp.transpose(diag, (0, 1, 3, 2)).reshape(S5_CH, S5_GROUP)


def _s5_out_blocks(c):
    ct = jnp.transpose(c.reshape(S5_BLOCKS, 2, S5_GROUP, S5_STATE), (0, 1, 3, 2))
    band = jnp.einsum('jgph,gk->jgpkh', ct, jnp.eye(2, dtype=F32)).reshape(S5_BLOCKS, 128, 32)
    return jnp.einsum('jrc,jq->jrqc', band, _s5_band_place()).reshape(S5_BLOCKS, 128, 128)


def _s5_out_unblock(mats):
    band = jnp.einsum('jrqc,jq->jrc', mats.reshape(S5_BLOCKS, 128, S5_PER_IN, 32), _s5_band_place())
    diag = jnp.einsum('jgpgh->jgph', band.reshape(S5_BLOCKS, 2, S5_STATE, 2, S5_GROUP))
    return jnp.transpose(diag, (0, 1, 3, 2)).reshape(S5_GROUPS, S5_GROUP, S5_STATE)


def _head_indicator():
    ch = lax.broadcasted_iota(jnp.int32, (RK_WIDTH, 128), 0) // RK_HEAD
    hd = lax.broadcasted_iota(jnp.int32, (RK_WIDTH, 128), 1)
    seg = (ch == hd).astype(F32)
    return seg, seg.T


def _add_epilogue(acc, e):
    return (acc + e,)


def _local_step(x, target, mod, wt, chunk=RK_CHUNK, ffn_shards=None, mixer_shards=None):
    t = x.shape[0]
    wt = dict(wt)
    sh1, sc1, gt1, sh2, sc2, gt2 = mod
    seg, seg_t = _head_indicator()
    g = {}

    (h1,) = _rowwise("norm1", _norm_mod_fn, [x], [wt["norm1_gain"], sc1, sh1], [(D_MODEL, BF16)], 256)
    if mixer_shards is None:
        proj = _matmul("proj", h1, wt["w_in"])
    else:
        proj, gathered = _matmul("proj", h1, wt["w_in"], hosted=_gather_halves_plan(mixer_shards[0]))
        wt.update(mixer_shards[1](gathered))
    u, p = proj[:, :S5_WIDTH], proj[:, S5_WIDTH:]
    ps = _token_shift(p, wt["mu_prev"], wt["mu_next"])
    r, k, v = ps[:, :1024], ps[:, 1024:2048], ps[:, 2048:3072]
    wdn, adn, gdn = ps[:, 3072:3200], ps[:, 3200:3328], ps[:, 3328:RK_PAD]

    prep_rows = [wt["lam_re"][0], wt["lam_im"][0], wt["log_step"][0], wt["lam_re"][1], wt["lam_im"][1],
                 wt["log_step"][1], wt["b_re"], wt["b_im"]]
    col1, col16 = (1, F32), (S5_GROUP, F32)
    prep = _rowwise("s5_prep", _s5_prep_fn, prep_rows, [], [col1, col1, col16, col16] * 2, 512)
    lbar = [tuple(prep[4 * d + q].reshape(S5_BLOCKS, 1, 128) for q in range(2)) for d in range(2)]
    b_blk = [tuple(_s5_in_blocks(prep[4 * d + 2 + q]) for q in range(2)) for d in range(2)]
    c_blk = (_s5_out_blocks(wt["c_re"]), -_s5_out_blocks(wt["c_im"]))
    u_il = _interleave(u)
    state0 = _s5_forward("s5_fwd0", u_il, *b_blk[0], *lbar[0], reverse=False)
    s1_re, s1_im, ylin_il = _s5_forward("s5_fwd1", u_il, *b_blk[1], *lbar[1], reverse=True, other=state0,
                                        c_re=c_blk[0], c_im_neg=c_blk[1])
    ylin = _deinterleave(ylin_il)
    states = [tuple(state0), (s1_re, s1_im)]
    s5_par = [wt["s5_d"], wt["s5_w_glu"], wt["s5_b_glu"]]
    (o_s5,) = _rowwise("s5_out", _s5_out_fn, [ylin, u], s5_par, [(S5_WIDTH, BF16)], 256)

    pre_par = [wt["w0"][0], wt["w0"][1], wt["w_up"][0], wt["w_up"][1], wt["a0"][0], wt["a0"][1],
               wt["a_up"][0], wt["a_up"][1], wt["g_up"], wt["k_k"], wt["k_a"]]
    pre = _rowwise("rk_pre", _rk_pre_fn, [k, wdn, adn, gdn], pre_par + [seg, seg_t], [(RK_WIDTH, F32)] * 8, 256)
    kk, lw, kd, act, gate = pre[0], pre[1:3], pre[3:5], pre[5:7], pre[7]
    core_in, ys, cks = [], [], []
    for d in range(2):
        ops = (r, lw[d], kd[d], v, kk, act[d])
        plan = _gather_halves_plan([ffn_shards[d]]) if ffn_shards is not None else None
        y, ck, gathered = _rk_core_fwd(f"rk_core{d}", *ops, reverse=(d == 1), chunk=min(chunk, t), hosted=plan)
        if gathered:
            wt["ffn_w1" if d == 0 else "ffn_w2"] = gathered[0] if d == 0 else gathered[0].reshape(FFN, D_MODEL)
        core_in.append(ops)
        ys.append(y)
        cks.append(ck)
    post_rows = [ys[0], ys[1], r, v, kd[0], kd[1], gate]
    post_par = [wt["ln_gain"], wt["ln_bias"], wt["r_k"]]
    (o_rk,) = _rowwise("rk_post", _rk_post_fn, post_rows, post_par + [seg, seg_t], [(RK_WIDTH, BF16)], 256)

    o = jnp.concatenate([o_s5, o_rk], axis=1)
    mixed = _matmul("mix_out", o, wt["w_out"])
    n2_par = [gt1, wt["norm2_gain"], sc2, sh2]
    x1, h2 = _rowwise("norm2", _resid_norm_mod_fn, [x, mixed], n2_par, [(D_MODEL, F32), (D_MODEL, BF16)], 256)
    f1, hid = _matmul("ffn1", h2, wt["ffn_w1"], out_dtypes=(F32, BF16), chips="b",
                      epilogue=lambda acc: (acc, jnp.square(jnp.maximum(acc, 0.0))))
    ffn = _matmul("ffn2", hid, wt["ffn_w2"])

    ones = jnp.ones((t, 1), F32)
    loss_rows, dx1, dffn, g_gt2, g["final_gain"] = _rowwise_vjp(
        "loss", _loss_fn, [x1, ffn, target], [gt2, wt["final_gain"]], [[ones]], [0, 1], [0, 1], 256, emit=(0,),
        row_grad_dtypes=[F32, BF16])
    df1 = _matmul("ffn2_dx", dffn, wt["ffn_w2"], tb=True, extras=(f1,), out_dtypes=(BF16,),
                  epilogue=lambda acc, f: (acc * (2.0 * jnp.maximum(f, 0.0)),))
    g["ffn_w2"] = _matmul("ffn2_dw", hid, dffn, ta=True)
    g["ffn_w1"] = _matmul("ffn1_dw", h2, df1, ta=True, chips="out")
    if ffn_shards is None:
        dh2 = _matmul("ffn1_dx", df1, wt["ffn_w1"], tb=True, chips="b_t")
    else:
        ffn_pieces = [g.pop("ffn_w1"), g.pop("ffn_w2").reshape(N_CHIPS, -1, D_MODEL)]
        dh2, from_sibling = _matmul("ffn1_dx", df1, wt["ffn_w1"], tb=True, chips="b_t",
                                    hosted=_other_half_plan(ffn_pieces))
        ffn_sums = [_pair_sum("pair_" + n, piece, other, BF16)
                    for n, piece, other in zip(FFN_SHARDED, ffn_pieces, from_sibling)]
    dx_a, dmixed, g_gt1, g["norm2_gain"], g_sc2, g_sh2 = _rowwise_vjp(
        "norm2_bwd", _resid_norm_mod_fn, [x, mixed], n2_par, [[dx1], [dh2]], [0, 1], [0, 1, 2, 3], 256,
        row_grad_dtypes=[F32, BF16])
    do = _matmul("mix_out_dx", dmixed, wt["w_out"], tb=True)
    g["w_out"] = _matmul("mix_out_dw", o, dmixed, ta=True)
    do_s5, do_rk = do[:, :S5_WIDTH], do[:, S5_WIDTH:]

    dylin, du, g["s5_d"], g["s5_w_glu"], g["s5_b_glu"] = _rowwise_vjp(
        "s5_out_bwd", _s5_out_fn, [ylin, u], s5_par, [[do_s5]], [0, 1], [0, 1, 2], 256)
    prep_cts = []
    dylin_il, du_il = _interleave(dylin), _interleave(du)
    for d in range(2):
        res = _s5_backward(f"s5_bwd{d}", dylin_il, u_il, du_il, states[d], states[1] if d == 0 else None,
                           *b_blk[d], *c_blk, *lbar[d], reverse=(d == 1))
        du_il, db_re, db_im, dl_re, dl_im = res[:5]
        if d == 0:
            g["c_re"], g["c_im"] = _s5_out_unblock(res[5]), -_s5_out_unblock(res[6])
        prep_cts += [[dl_re.reshape(S5_CH, 1)], [dl_im.reshape(S5_CH, 1)], [_s5_in_unblock(db_re)],
                     [_s5_in_unblock(db_im)]]
    du = _deinterleave(du_il)
    pg = _rowwise_vjp("s5_prep_bwd", _s5_prep_fn, prep_rows, [], prep_cts, list(range(8)), [], 512)
    g["lam_re"], g["lam_im"], g["log_step"] = (pg[0], pg[3]), (pg[1], pg[4]), (pg[2], pg[5])
    g["b_re"], g["b_im"] = pg[6], pg[7]

    pb = _rowwise_vjp("rk_post_bwd", _rk_post_fn, post_rows, post_par, [[do_rk]], [0, 2, 3, 4, 5, 6], [0, 1, 2],
                      128, consts=[seg, seg_t])
    dy, dr_b, dv_b, dkd_b, dgate = pb[0], pb[1], pb[2], pb[3:5], pb[5]
    g["ln_gain"], g["ln_bias"], g["r_k"] = pb[6], pb[7], pb[8]
    cg = []
    for d in range(2):
        plan = None
        if d == 0 and ffn_shards is not None:
            plan = _exchange_plan(ffn_sums, CHIP_PEERS, N_CHIPS, scatter=True)
        grads, arrived = _rk_core_bwd(f"rk_core{d}_bwd", *core_in[d], cks[d], dy, reverse=(d == 1),
                                      chunk=min(chunk, t), hosted=plan)
        if arrived:
            g["ffn_arrived"] = arrived
        cg.append(grads)
    pre_cts = [[cg[0][4], cg[1][4]], [cg[0][1]], [cg[1][1]], [cg[0][2], dkd_b[0]], [cg[1][2], dkd_b[1]],
               [cg[0][5]], [cg[1][5]], [dgate]]
    qb = _rowwise_vjp("rk_pre_bwd", _rk_pre_fn, [k, wdn, adn, gdn], pre_par, pre_cts, [0, 1, 2, 3],
                      list(range(11)), 128, consts=[seg, seg_t])
    dk, dwdn, dadn, dgdn = qb[:4]
    g["w0"], g["w_up"], g["a0"], g["a_up"] = (qb[4], qb[5]), (qb[6], qb[7]), (qb[8], qb[9]), (qb[10], qb[11])
    g["g_up"], g["k_k"], g["k_a"] = qb[12], qb[13], qb[14]
    dr, dv = _rowwise("rk_sum", lambda a, b, c, e, f, h: (a + b + c, e + f + h),
                      [cg[0][0], cg[1][0], dr_b, cg[0][3], cg[1][3], dv_b], [], [(RK_WIDTH, F32)] * 2, 256)
    dps = jnp.concatenate([dr, dk, dv, dwdn, dadn, dgdn], axis=1)
    dp, g["mu_prev"], g["mu_next"] = _token_shift_bwd(p, wt["mu_prev"], wt["mu_next"], dps)

    dproj = jnp.concatenate([du, dp], axis=1).astype(BF16)
    dh1 = _matmul("proj_dx", dproj, wt["w_in"], tb=True)
    g["w_in"] = _matmul("proj_dw", h1, dproj, ta=True)
    grad_x, g["norm1_gain"], g_sc1, g_sh1 = _rowwise_vjp(
        "norm1_bwd", _norm_mod_fn, [x], [wt["norm1_gain"], sc1, sh1], [[dh1]], [0], [0, 1, 2], 256,
        addends={0: dx_a})
    g["mod"] = [g_sh1, g_sc1, g_gt1, g_sh2, g_sc2, g_gt2]
    return loss_rows, grad_x, g


CHIP_PEERS = ((1, 0, 0), (0, 1, 0), (1, 1, 0))
ALL_PEERS = ((0, 0, 1), (0, 1, 0), (0, 1, 1), (1, 0, 0), (1, 0, 1), (1, 1, 0), (1, 1, 1))
CORE_PEER = ((0, 0, 1),)


def _exchange(name, arrays, peers, n_slots, scatter=False):
    return _run_plan(name, _exchange_plan(arrays, peers, n_slots, scatter))


def _run_plan(name, plan):
    na = len(plan.arrays)

    def body(*refs):
        plan.start(refs[:na], refs[na:2 * na], refs[2 * na:])
        plan.wait(refs[:na], refs[na:2 * na], refs[2 * na:])

    any_spec = pl.BlockSpec(memory_space=pl.ANY)
    return plan.finish(pl.pallas_call(
        body, name=name, in_specs=[any_spec] * na, out_specs=[any_spec] * na, out_shape=plan.out_shape,
        scratch_shapes=plan.sems,
    )(*plan.arrays))


def _exchange_plan(arrays, peers, n_slots, scatter=False):
    na, nm = len(arrays), len(peers)

    def ident(px, py, pc):
        return {8: 4 * px + 2 * py + pc, 4: 2 * px + py, 2: pc}[n_slots]

    def copies(in_refs, out_refs, sems):
        send_sems, recv_sems = sems
        x, y, c = lax.axis_index("x"), lax.axis_index("y"), lax.axis_index("c")
        me = ident(x, y, c)
        made = []
        for i in range(na):
            for j, (fx, fy, fc) in enumerate(peers):
                px, py, pc = (1 - x if fx else x), (1 - y if fy else y), (1 - c if fc else c)
                src = in_refs[i].at[ident(px, py, pc)] if scatter else in_refs[i]
                made.append(pltpu.make_async_remote_copy(
                    src_ref=src, dst_ref=out_refs[i].at[me],
                    send_sem=send_sems.at[i * nm + j], recv_sem=recv_sems.at[i * nm + j],
                    device_id=(px, py, pc), device_id_type=pl.DeviceIdType.MESH))
        return made

    def start(in_refs, out_refs, sems):
        for copy in copies(in_refs, out_refs, sems):
            copy.start()

    def wait(in_refs, out_refs, sems):
        for copy in copies(in_refs, out_refs, sems):
            copy.wait()

    def finish(outs):
        me = ident(lax.axis_index("x"), lax.axis_index("y"), lax.axis_index("c"))
        return [lax.dynamic_update_slice_in_dim(
            o, lax.dynamic_index_in_dim(a, me, 0, keepdims=True) if scatter else a[None], me, axis=0)
            for a, o in zip(arrays, outs)]

    out_shape = [jax.ShapeDtypeStruct(((n_slots,) + a.shape[1:]) if scatter else ((n_slots,) + a.shape), a.dtype)
                 for a in arrays]
    sems = [pltpu.SemaphoreType.DMA((na * nm,)), pltpu.SemaphoreType.DMA((na * nm,))]
    return _Plan(arrays, out_shape, sems, start, wait, finish)


def _gather_halves(name, arrays):
    return _run_plan(name, _gather_halves_plan(arrays))


def _gather_halves_plan(arrays):
    na = len(arrays)
    chips = ((1, 0), (0, 1), (1, 1))

    def over_ici(in_refs, out_refs, sems):
        ici_send, ici_recv = sems[:2]
        x, y, c = lax.axis_index("x"), lax.axis_index("y"), lax.axis_index("c")
        made = []
        for i in range(na):
            half = arrays[i].shape[0] // 2
            mine = pl.ds(pl.multiple_of(c * half, 8), half)
            for j, (fx, fy) in enumerate(chips):
                px, py = (1 - x if fx else x), (1 - y if fy else y)
                k = len(chips) * i + j
                made.append([pltpu.make_async_remote_copy(
                    src_ref=in_refs[i].at[mine], dst_ref=out_refs[i].at[chip, mine],
                    send_sem=ici_send.at[k], recv_sem=ici_recv.at[k],
                    device_id=(px, py, c), device_id_type=pl.DeviceIdType.MESH)
                    for chip in (2 * x + y, 2 * px + py)])
        return made

    def start(in_refs, out_refs, sems):
        for outgoing, _ in over_ici(in_refs, out_refs, sems):
            outgoing.start()

    def wait(in_refs, out_refs, sems):
        d2d_send, d2d_recv = sems[2:]
        x, y, c = lax.axis_index("x"), lax.axis_index("y"), lax.axis_index("c")
        pending = []
        ici = over_ici(in_refs, out_refs, sems)
        for i in range(na):
            half = arrays[i].shape[0] // 2
            mine = pl.ds(pl.multiple_of(c * half, 8), half)
            theirs = pl.ds(pl.multiple_of((1 - c) * half, 8), half)
            for j, (fx, fy) in enumerate(chips):
                px, py = (1 - x if fx else x), (1 - y if fy else y)
                k = len(chips) * i + j
                outgoing, landing = ici[k]
                landing.wait_recv()
                landed = out_refs[i].at[2 * px + py, mine]
                passed = pltpu.make_async_remote_copy(
                    src_ref=landed, dst_ref=landed, send_sem=d2d_send.at[k], recv_sem=d2d_recv.at[k],
                    device_id=(x, y, 1 - c), device_id_type=pl.DeviceIdType.MESH)
                passed.start()
                from_sibling = out_refs[i].at[2 * px + py, theirs]
                pending += [outgoing.wait_send, passed.wait_send, pltpu.make_async_remote_copy(
                    src_ref=from_sibling, dst_ref=from_sibling, send_sem=d2d_send.at[k], recv_sem=d2d_recv.at[k],
                    device_id=(x, y, 1 - c), device_id_type=pl.DeviceIdType.MESH).wait_recv]
        for done in pending:
            done()

    def finish(outs):
        me = 2 * lax.axis_index("x") + lax.axis_index("y")
        return [lax.dynamic_update_slice_in_dim(o, a[None], me, axis=0) for a, o in zip(arrays, outs)]

    out_shape = [jax.ShapeDtypeStruct((N_CHIPS,) + a.shape, a.dtype) for a in arrays]
    return _Plan(arrays, out_shape, [pltpu.SemaphoreType.DMA((na * len(chips),))] * 4, start, wait, finish)


def _send_other_half(name, arrays):
    return _run_plan(name, _other_half_plan(arrays))


def _other_half_plan(arrays):
    na = len(arrays)

    def copies(in_refs, out_refs, sems):
        send_sems, recv_sems = sems
        x, y, c = lax.axis_index("x"), lax.axis_index("y"), lax.axis_index("c")
        made = []
        for i in range(na):
            half = arrays[i].shape[1] // 2
            theirs = pl.ds(pl.multiple_of((1 - c) * half, 8), half)
            made.append(pltpu.make_async_remote_copy(
                src_ref=in_refs[i].at[:, theirs], dst_ref=out_refs[i], send_sem=send_sems.at[i],
                recv_sem=recv_sems.at[i], device_id=(x, y, 1 - c), device_id_type=pl.DeviceIdType.MESH))
        return made

    def start(in_refs, out_refs, sems):
        for copy in copies(in_refs, out_refs, sems):
            copy.start()

    def wait(in_refs, out_refs, sems):
        for copy in copies(in_refs, out_refs, sems):
            copy.wait()

    out_shape = [jax.ShapeDtypeStruct((a.shape[0], a.shape[1] // 2, a.shape[2]), a.dtype) for a in arrays]
    sems = [pltpu.SemaphoreType.DMA((na,)), pltpu.SemaphoreType.DMA((na,))]
    return _Plan(arrays, out_shape, sems, start, wait, list)


def _adam_math(w, g, m, v):
    m = ADAM_B1 * m + (1.0 - ADAM_B1) * g
    v = ADAM_B2 * v + (1.0 - ADAM_B2) * jnp.square(g)
    m_hat = m / (1.0 - ADAM_B1 ** ADAM_STEP)
    v_hat = v / (1.0 - ADAM_B2 ** ADAM_STEP)
    delta = -ADAM_LR * (m_hat / (jnp.sqrt(v_hat) + ADAM_EPS) + ADAM_WD * w)
    return delta, m, v


WHOLE_BLOCK_BYTES = 2 * 1024 * 1024


def _row_tile(r, c):
    return r if 4 * r * c <= WHOLE_BLOCK_BYTES else _tile(r, (256, 128, 64, 32, 16, 8))


def _sum_parts(name, parts):
    n, r, c = parts.shape
    tr = _row_tile(r, c)

    def body(p_ref, o_ref):
        tot = p_ref[0].astype(F32)
        for i in range(1, n):
            tot = tot + p_ref[i].astype(F32)
        o_ref[...] = tot

    return pl.pallas_call(
        body, name=name, grid=(r // tr,), in_specs=[pl.BlockSpec((n, tr, c), lambda i: (0, i, 0))],
        out_specs=pl.BlockSpec((tr, c), lambda i: (i, 0)), out_shape=jax.ShapeDtypeStruct((r, c), F32),
        compiler_params=_params(("parallel",)),
    )(parts)


def _pair_sum(name, piece, other, dtype):
    n, r, c = piece.shape
    half = r // 2
    tr = _row_tile(half, c)

    def body(lo_ref, hi_ref, other_ref, o_ref):
        own = jnp.where(lax.axis_index("c") == 0, lo_ref[...], hi_ref[...])
        o_ref[...] = (own + other_ref[...]).astype(o_ref.dtype)

    blk = pl.BlockSpec((None, tr, c), lambda j, i: (j, i, 0))
    return pl.pallas_call(
        body, name=name, grid=(n, half // tr),
        in_specs=[pl.BlockSpec((None, None, tr, c), lambda j, i: (j, 0, i, 0)),
                  pl.BlockSpec((None, None, tr, c), lambda j, i: (j, 1, i, 0)), blk],
        out_specs=blk, out_shape=jax.ShapeDtypeStruct((n, half, c), dtype),
        compiler_params=_params(("parallel", "parallel")),
    )(piece.reshape(n, 2, half, c), piece.reshape(n, 2, half, c), other)


def _adamw(name, w, parts, m, v, hosted=None):
    n, r, c = parts.shape
    tr = _row_tile(r, c)
    steps = r // tr
    plan = hosted or _NO_PLAN
    nh = len(plan.arrays)

    def body(w_ref, p_ref, m_ref, v_ref, *rest):
        host_in, (g_ref, d_ref, nm_ref, nv_ref) = rest[:nh], rest[nh:nh + 4]
        host_out, sems = rest[nh + 4:2 * nh + 4], rest[2 * nh + 4:]

        @pl.when(pl.program_id(0) == 0)
        def _():
            plan.start(host_in, host_out, sems)

        g = p_ref[0]
        for i in range(1, n):
            g = g + p_ref[i]
        delta, nm, nv = _adam_math(w_ref[...], g, m_ref[...], v_ref[...])
        g_ref[...], d_ref[...], nm_ref[...], nv_ref[...] = g, delta, nm, nv

        @pl.when(pl.program_id(0) == steps - 1)
        def _():
            plan.wait(host_in, host_out, sems)

    blk = pl.BlockSpec((tr, c), lambda i: (i, 0))
    any_spec = pl.BlockSpec(memory_space=pl.ANY)
    res = pl.pallas_call(
        body, name=name, grid=(steps,),
        in_specs=[blk, pl.BlockSpec((n, tr, c), lambda i: (0, i, 0)), blk, blk] + [any_spec] * nh,
        out_specs=[blk] * 4 + [any_spec] * nh,
        out_shape=[jax.ShapeDtypeStruct((r, c), F32)] * 4 + plan.out_shape, scratch_shapes=plan.sems,
        compiler_params=_params(("arbitrary",) if nh else ("parallel",)),
    )(w, parts, m, v, *plan.arrays)
    return (res[:4], plan.finish(res[4:])) if nh else res


def _ada_w_update(act_t, dmod, w, m, v, hosted):
    r, c = w.shape
    nb = act_t.shape[1]
    tr, tc = 256, 1024
    grid = (r // tr, c // tc)
    nh = len(hosted.arrays)

    def body(a_ref, d_ref, w_ref, m_ref, v_ref, *rest):
        host_in, (g_ref, dl_ref, nm_ref, nv_ref) = rest[:nh], rest[nh:nh + 4]
        host_out, sems = rest[nh + 4:2 * nh + 4], rest[2 * nh + 4:]
        i, j = pl.program_id(0), pl.program_id(1)

        @pl.when(jnp.logical_and(i == 0, j == 0))
        def _():
            hosted.start(host_in, host_out, sems)

        a, dm = a_ref[...], d_ref[...]
        g = a[:, 0:1] * dm[0:1, :]
        for b in range(1, nb):
            g = g + a[:, b:b + 1] * dm[b:b + 1, :]
        delta, nm, nv = _adam_math(w_ref[...], g, m_ref[...], v_ref[...])
        g_ref[...], dl_ref[...], nm_ref[...], nv_ref[...] = g, delta, nm, nv

        @pl.when(jnp.logical_and(i == grid[0] - 1, j == grid[1] - 1))
        def _():
            hosted.wait(host_in, host_out, sems)

    blk = pl.BlockSpec((tr, tc), lambda i, j: (i, j))
    any_spec = pl.BlockSpec(memory_space=pl.ANY)
    res = pl.pallas_call(
        body, name="ada_w_update", grid=grid,
        in_specs=[pl.BlockSpec((tr, nb), lambda i, j: (i, 0)), pl.BlockSpec((nb, tc), lambda i, j: (0, j)),
                  blk, blk, blk] + [any_spec] * nh,
        out_specs=[blk] * 4 + [any_spec] * nh,
        out_shape=[jax.ShapeDtypeStruct((r, c), F32)] * 4 + hosted.out_shape,
        scratch_shapes=hosted.sems,
        compiler_params=_params(("arbitrary", "arbitrary")),
    )(act_t, dmod, w, m, v, *hosted.arrays)
    return res[:4], hosted.finish(res[4:])


WEIGHTS = ['ada_w', 'ada_b', 'norm1_gain', 'norm2_gain', 'final_gain', 'w_in', 'w_out', 's5_lambda_re',
           's5_lambda_im', 's5_log_step', 's5_b_re', 's5_b_im', 's5_c_re', 's5_c_im', 's5_d', 's5_w_glu',
           's5_b_glu', 'rk_shift_prev', 'rk_shift_next', 'rk_w0', 'rk_w_up', 'rk_a0', 'rk_a_up', 'rk_g_up',
           'rk_k_k', 'rk_k_a', 'rk_r_k', 'rk_ln_gain', 'rk_ln_bias', 'ffn_w1', 'ffn_w2']
BIG_SHARDED = ['w_in', 'w_out', 's5_w_glu', 'ffn_w1', 'ffn_w2']
FFN_SHARDED = ['ffn_w1', 'ffn_w2']
RK_SHARDED = ['rk_w0', 'rk_a0', 'rk_w_up', 'rk_a_up', 'rk_g_up']
REPLICATED = ['ada_b', 'norm1_gain', 'norm2_gain', 'final_gain', 's5_lambda_re', 's5_lambda_im', 's5_log_step',
              's5_b_re', 's5_b_im', 's5_c_re', 's5_c_im', 's5_d', 's5_b_glu', 'rk_shift_prev', 'rk_shift_next',
              'rk_k_k', 'rk_k_a', 'rk_r_k', 'rk_ln_gain', 'rk_ln_bias']
PACK_COLS = 1024
N_CHIPS = 4
RK_ROWS = 420
RK_ROWS_PAD = 432


def _pack_rows(arrays, cols):
    return jnp.concatenate([a.reshape(-1, cols) for a in arrays], axis=0)


def _pack_flat(arrays):
    flat = jnp.concatenate([a.reshape(-1) for a in arrays])
    rows = -(-flat.shape[0] // PACK_COLS)
    return jnp.pad(flat, (0, rows * PACK_COLS - flat.shape[0])).reshape(rows, PACK_COLS)


def _unpack_flat(packed, like):
    flat, out, pos = packed.reshape(-1), [], 0
    for a in like:
        out.append(flat[pos:pos + a.size].reshape(a.shape))
        pos += a.size
    return out


def _cols_to_chips(full, n_rows):
    return jnp.transpose(full.reshape(n_rows, N_CHIPS, -1), (1, 0, 2))


def _chips_to_cols(parts):
    return jnp.transpose(parts, (1, 0, 2)).reshape(parts.shape[1], -1)


def kernel(x, c, ada_w, ada_b, norm1_gain, norm2_gain, final_gain, w_in, w_out, s5_lambda_re, s5_lambda_im, s5_log_step, s5_b_re, s5_b_im, s5_c_re, s5_c_im, s5_d, s5_w_glu, s5_b_glu, rk_shift_prev, rk_shift_next, rk_w0, rk_w_up, rk_a0, rk_a_up, rk_g_up, rk_k_k, rk_k_a, rk_r_k, rk_ln_gain, rk_ln_bias, ffn_w1, ffn_w2, loss_target, m_ada_w, m_ada_b, m_norm1_gain, m_norm2_gain, m_final_gain, m_w_in, m_w_out, m_s5_lambda_re, m_s5_lambda_im, m_s5_log_step, m_s5_b_re, m_s5_b_im, m_s5_c_re, m_s5_c_im, m_s5_d, m_s5_w_glu, m_s5_b_glu, m_rk_shift_prev, m_rk_shift_next, m_rk_w0, m_rk_w_up, m_rk_a0, m_rk_a_up, m_rk_g_up, m_rk_k_k, m_rk_k_a, m_rk_r_k, m_rk_ln_gain, m_rk_ln_bias, m_ffn_w1, m_ffn_w2, v_ada_w, v_ada_b, v_norm1_gain, v_norm2_gain, v_final_gain, v_w_in, v_w_out, v_s5_lambda_re, v_s5_lambda_im, v_s5_log_step, v_s5_b_re, v_s5_b_im, v_s5_c_re, v_s5_c_im, v_s5_d, v_s5_w_glu, v_s5_b_glu, v_rk_shift_prev, v_rk_shift_next, v_rk_w0, v_rk_w_up, v_rk_a0, v_rk_a_up, v_rk_g_up, v_rk_k_k, v_rk_k_a, v_rk_r_k, v_rk_ln_gain, v_rk_ln_bias, v_ffn_w1, v_ffn_w2):
    given = dict(locals())
    w = {n: given[n] for n in WEIGHTS}
    m = {n: given["m_" + n] for n in WEIGHTS}
    v = {n: given["v_" + n] for n in WEIGHTS}
    mx, my, mc = lax.axis_index("x"), lax.axis_index("y"), lax.axis_index("c")
    chip = 2 * mx + my
    dev = 2 * chip + mc
    xt, target = x[0], loss_target[0]

    def rk_rows(d):
        rows = _pack_rows([d[n] for n in RK_SHARDED], 256)
        return jnp.pad(rows, ((0, RK_ROWS_PAD - rows.shape[0]), (0, 0)))

    (c_all,), (w_in_parts,) = _run_plan("gather_first", _join_plans([
        _exchange_plan([c], ALL_PEERS, 8), _gather_halves_plan([w_in[0].astype(BF16)])]))

    (act,) = _rowwise("ada_act", lambda q: (q * _sigmoid(q),), [c_all.reshape(8, D_MODEL)], [], [(D_MODEL, F32)], 8)
    n_mod_cols = N_MOD * D_MODEL // N_CHIPS
    bias = jnp.broadcast_to(lax.dynamic_slice(ada_b, (0, chip * n_mod_cols), (1, n_mod_cols)), (8, n_mod_cols))
    mod_shard = _matmul("ada_fwd", act, ada_w[0], epilogue=_add_epilogue, extras=(bias,))
    (mod_parts,) = _exchange("gather_mod", [mod_shard], CHIP_PEERS, N_CHIPS)
    mod_all = _chips_to_cols(mod_parts)
    mod_mine = lax.dynamic_slice(mod_all, (dev, 0), (1, N_MOD * D_MODEL))
    mod = [mod_mine[:, i * D_MODEL:(i + 1) * D_MODEL] for i in range(N_MOD)]

    def mixer_weights(parts):
        w_out_parts, glu_parts, rk_full = parts

        def rk_piece(lo, hi, lead):
            return _chips_to_cols(rk_full[:, lo:hi]).reshape(lead + (RK_WIDTH,))

        zeros = jnp.zeros((LORA, RK_WIDTH), F32)
        w_up, a_up = rk_piece(4, 132, (2, LORA)), rk_piece(132, 260, (2, LORA))
        return {
            "w_out": w_out_parts.reshape(D_MODEL, D_MODEL), "s5_w_glu": glu_parts.reshape(S5_WIDTH, S5_WIDTH),
            "w0": list(rk_piece(0, 2, (2,))[:, None, :]), "a0": list(rk_piece(2, 4, (2,))[:, None, :]),
            "w_up": [jnp.concatenate([w_up[0], zeros]), jnp.concatenate([zeros, w_up[1]])],
            "a_up": [jnp.concatenate([a_up[0], zeros]), jnp.concatenate([zeros, a_up[1]])],
            "g_up": jnp.pad(rk_piece(260, 420, (GATE_LORA,)), ((0, GATE_PAD - GATE_LORA), (0, 0))),
        }

    wt = {
        "norm1_gain": norm1_gain, "norm2_gain": norm2_gain, "final_gain": final_gain.reshape(1, D_MODEL),
        "w_in": jnp.pad(_chips_to_cols(w_in_parts), ((0, 0), (0, PROJ_PAD - PROJ))),
        "mu_prev": jnp.pad(rk_shift_prev, ((0, 0), (0, RK_PAD - RK_IN))),
        "mu_next": jnp.pad(rk_shift_next, ((0, 0), (0, RK_PAD - RK_IN))),
        "lam_re": [s5_lambda_re[0, d].reshape(S5_CH, 1) for d in range(2)],
        "lam_im": [s5_lambda_im[0, d].reshape(S5_CH, 1) for d in range(2)],
        "log_step": [jnp.repeat(s5_log_step[0, d], S5_STATE).reshape(S5_CH, 1) for d in range(2)],
        "b_re": s5_b_re.reshape(S5_CH, S5_GROUP), "b_im": s5_b_im.reshape(S5_CH, S5_GROUP),
        "c_re": s5_c_re[0], "c_im": s5_c_im[0],
        "s5_d": s5_d, "s5_b_glu": s5_b_glu,
        "k_k": rk_k_k, "k_a": rk_k_a, "r_k": rk_r_k.reshape(1, RK_WIDTH),
        "ln_gain": rk_ln_gain, "ln_bias": rk_ln_bias,
    }

    ffn_shards = [w[n][0].astype(BF16) for n in FFN_SHARDED]
    mixer_shards = [w_out[0].astype(BF16), s5_w_glu[0].astype(BF16), rk_rows(w)]
    loss_rows, grad_x, g = _local_step(xt, target, mod, wt, ffn_shards=ffn_shards,
                                       mixer_shards=(mixer_shards, mixer_weights))
    loss = lax.psum(jnp.sum(loss_rows), ("x", "y", "c"))


    big_grads = {
        "w_in": _cols_to_chips(g["w_in"][:, :PROJ], D_MODEL),
        "w_out": g["w_out"].reshape(N_CHIPS, -1, D_MODEL),
        "s5_w_glu": g["s5_w_glu"].reshape(N_CHIPS, -1, S5_WIDTH),
    }
    rk_grads = jnp.concatenate([
        _cols_to_chips(jnp.concatenate(g["w0"]), 2), _cols_to_chips(jnp.concatenate(g["a0"]), 2),
        _cols_to_chips(jnp.concatenate([g["w_up"][0][:LORA], g["w_up"][1][LORA:]]), 2 * LORA),
        _cols_to_chips(jnp.concatenate([g["a_up"][0][:LORA], g["a_up"][1][LORA:]]), 2 * LORA),
        _cols_to_chips(g["g_up"][:GATE_LORA], GATE_LORA),
        jnp.zeros((N_CHIPS, RK_ROWS_PAD - RK_ROWS, 256), F32)], axis=1)
    local_small = {
        "ada_b": jnp.concatenate(g["mod"], axis=1),
        "norm1_gain": g["norm1_gain"], "norm2_gain": g["norm2_gain"], "final_gain": g["final_gain"],
        "s5_lambda_re": jnp.concatenate(g["lam_re"]), "s5_lambda_im": jnp.concatenate(g["lam_im"]),
        "s5_log_step": jnp.concatenate([q.reshape(S5_GROUPS, S5_STATE).sum(axis=1) for q in g["log_step"]]),
        "s5_b_re": g["b_re"], "s5_b_im": g["b_im"], "s5_c_re": g["c_re"], "s5_c_im": g["c_im"],
        "s5_d": g["s5_d"], "s5_b_glu": g["s5_b_glu"],
        "rk_shift_prev": g["mu_prev"][:, :RK_IN], "rk_shift_next": g["mu_next"][:, :RK_IN],
        "rk_k_k": g["k_k"], "rk_k_a": g["k_a"], "rk_r_k": g["r_k"],
        "rk_ln_gain": g["ln_gain"], "rk_ln_bias": g["ln_bias"],
    }
    late = [n for n in BIG_SHARDED if n not in FFN_SHARDED]
    late_pieces = [big_grads[n] for n in late] + [rk_grads]
    late_names = late + ["rk"]

    def whole(halves):
        return halves.reshape(1, 2 * halves.shape[1], halves.shape[2])

    ffn_halves = [_sum_parts("sum_" + n, a) for n, a in zip(FFN_SHARDED, g["ffn_arrived"])]
    from_sibling, ffn_pairs, (small_all,) = _run_plan("swap_late", _join_plans([
        _other_half_plan(late_pieces), _exchange_plan(ffn_halves, CORE_PEER, 2),
        _exchange_plan([_pack_flat([local_small[n] for n in REPLICATED])], ALL_PEERS, 8)]))
    late_sums = [_pair_sum("pair_" + n, piece, other, F32 if n == "rk" else BF16)
                 for n, piece, other in zip(late_names, late_pieces, from_sibling)]
    pairs = dict(zip(FFN_SHARDED, [whole(p) for p in ffn_pairs]))

    mod_rows = N_MOD * D_MODEL // PACK_COLS
    dmod_all = small_all[:, :mod_rows].reshape(8, N_MOD * D_MODEL)
    dmod = lax.dynamic_slice(dmod_all, (0, chip * n_mod_cols), (8, n_mod_cols))
    ada_res, arrived = _ada_w_update(act.T, dmod, ada_w[0], m_ada_w[0], v_ada_w[0],
                                     hosted=_exchange_plan(late_sums, CHIP_PEERS, N_CHIPS, scatter=True))
    late_halves = [_sum_parts("sum_" + n, a) for n, a in zip(late_names, arrived)]

    out = {"ada_w": [r[None] for r in ada_res]}
    first = FFN_SHARDED[0]
    res, swapped = _adamw("adamw_" + first, w[first][0], pairs[first], m[first][0], v[first][0],
                          hosted=_exchange_plan(late_halves, CORE_PEER, 2))
    out[first] = [r[None] for r in res]
    pairs.update(zip(late_names, [whole(p) for p in swapped]))
    for n in [FFN_SHARDED[1]] + late:
        out[n] = [r[None] for r in _adamw("adamw_" + n, w[n][0], pairs[n], m[n][0], v[n][0])]
    rk_res = _adamw("adamw_rk", rk_rows(w), pairs["rk"], rk_rows(m), rk_rows(v))
    for q in range(4):
        pieces, pos = [], 0
        for n in RK_SHARDED:
            rows = w[n].size // 256
            pieces.append(rk_res[q][pos:pos + rows].reshape(w[n].shape))
            pos += rows
        for n, piece in zip(RK_SHARDED, pieces):
            out.setdefault(n, []).append(piece)

    small_res = _adamw("adamw_small", _pack_flat([w[n] for n in REPLICATED]), small_all,
                       _pack_flat([m[n] for n in REPLICATED]), _pack_flat([v[n] for n in REPLICATED]))
    for q in range(4):
        for n, piece in zip(REPLICATED, _unpack_flat(small_res[q], [w[n] for n in REPLICATED])):
            out.setdefault(n, []).append(piece)

    return (loss, grad_x[None], *[out[n][0] for n in WEIGHTS], *[out[n][1] for n in WEIGHTS],
            *[out[n][2] for n in WEIGHTS], *[out[n][3] for n in WEIGHTS])
```

```python
import functools
import math

import jax
import jax.numpy as jnp
from jax import lax
from jax.experimental import pallas as pl
from jax.experimental.pallas import tpu as pltpu

F32 = jnp.float32
BF16 = jnp.bfloat16

D_MODEL = 2048
S5_WIDTH = 1024
S5_GROUP = 16
S5_GROUPS = 64
S5_STATE = 64
S5_CH = S5_GROUPS * S5_STATE
S5_BLK = 256
RK_WIDTH = 1024
RK_HEAD = 64
RK_HEADS = 16
LORA = 64
GATE_LORA = 160
GATE_PAD = 256
RK_IN = 3488
RK_PAD = 3584
PROJ = 4512
PROJ_PAD = 4608
FFN = 8192
N_MOD = 6
NORM_EPS = 1e-6
GN_EPS = 64e-5
L2_EPS = 1e-12
RK_CHUNK = 64
RK_PASSES = {"solve": 3, "kt": 3, "s0": 1, "akk_v": 1, "ark_v": 1, "arb_u": 1, "state": 3}
LW_SCALE = math.exp(-0.5)
ADAM_LR, ADAM_B1, ADAM_B2, ADAM_EPS, ADAM_WD, ADAM_STEP = 0.001, 0.9, 0.999, 1e-08, 0.01, 10
VMEM_LIMIT = 56 * 1024 * 1024
HI = lax.Precision.HIGHEST


def _params(sem=None):
    return pltpu.CompilerParams(dimension_semantics=sem, vmem_limit_bytes=VMEM_LIMIT)


def _full(a):
    nd = a.ndim
    return pl.BlockSpec(a.shape, lambda *_: (0,) * nd)


@jax.custom_vjp
def _bdot(a, b):
    return jnp.dot(a.astype(BF16), b.astype(BF16), preferred_element_type=F32)


def _bdot_fwd(a, b):
    return _bdot(a, b), (a, b)


def _bdot_bwd(res, g):
    a, b = res
    gb = g.astype(BF16)
    da = lax.dot_general(gb, b.astype(BF16), (((1,), (1,)), ((), ())), preferred_element_type=F32)
    db = lax.dot_general(a.astype(BF16), gb, (((0,), (0,)), ((), ())), preferred_element_type=F32)
    return da, db


_bdot.defvjp(_bdot_fwd, _bdot_bwd)


@jax.custom_vjp
def _seg_dot(x, ind, ind_t):
    hi = x.astype(BF16)
    lo = (x - hi.astype(F32)).astype(BF16)
    both = jnp.dot(jnp.concatenate([hi, lo], axis=0), ind.astype(BF16), preferred_element_type=F32)
    return both[:x.shape[0]] + both[x.shape[0]:]


_seg_dot.defvjp(lambda x, ind, ind_t: (_seg_dot(x, ind, ind_t), (ind, ind_t)),
                lambda res, g: (_seg_dot(g, res[1], res[0]), jnp.zeros_like(res[0]), jnp.zeros_like(res[1])))


def _sigmoid(z):
    return 1.0 / (1.0 + jnp.exp(-z))


def _gelu(y):
    return 0.5 * y * (1.0 + jnp.tanh(0.7978845608028654 * (y + 0.044715 * (y * y * y))))


def _rms(x):
    return x * lax.rsqrt(jnp.mean(x * x, axis=-1, keepdims=True) + NORM_EPS)


def _tile(n, prefs):
    for t in prefs:
        if n % t == 0:
            return t
    return n


def _matmul(name, a, b, ta=False, tb=False, epilogue=None, extras=(), out_dtypes=(F32,), chips=None, hosted=None):
    m = a.shape[1] if ta else a.shape[0]
    k = a.shape[0] if ta else a.shape[1]
    if chips == "b":
        assert not tb and b.shape[1] == k
        n = N_CHIPS * b.shape[2]
    elif chips == "b_t":
        assert tb and N_CHIPS * b.shape[2] == k
        n = b.shape[1]
    else:
        n = b.shape[0] if tb else b.shape[1]
        assert k == (b.shape[1] if tb else b.shape[0]), (a.shape, b.shape, ta, tb)
    split = N_CHIPS if chips in ("b", "out") else 1
    tm = _tile(m, (1024, 512, 256, 128))
    tn = _tile(n // split, (1024, 768, 512, 256, 128))
    tk = k // N_CHIPS if chips == "b_t" else _tile(k, (2048, 1024, 512, 256, 128))
    nk = k // tk
    per = n // split // tn
    n_ex, n_out = len(extras), len(out_dtypes)
    dims = (((0 if ta else 1,), (1 if tb else 0,)), ((), ()))

    hosted = hosted or _NO_PLAN
    nh = len(hosted.arrays)
    grid = (m // tm, split, per, nk)

    def body(a_ref, b_ref, *rest):
        ex_refs, host_in = rest[:n_ex], rest[n_ex:n_ex + nh]
        out_refs, host_out = rest[n_ex + nh:n_ex + nh + n_out], rest[n_ex + nh + n_out:n_ex + 2 * nh + n_out]
        acc, sems = rest[n_ex + 2 * nh + n_out], rest[n_ex + 2 * nh + n_out + 1:]
        kk = pl.program_id(3)
        if nh:
            ids = [pl.program_id(d) for d in range(4)]
            first = functools.reduce(jnp.logical_and, [i == 0 for i in ids])
            last = functools.reduce(jnp.logical_and, [i == g - 1 for i, g in zip(ids, grid)])

            @pl.when(first)
            def _():
                hosted.start(host_in, host_out, sems)

        @pl.when(kk == 0)
        def _():
            acc[...] = jnp.zeros_like(acc)

        acc[...] += lax.dot_general(a_ref[...].astype(BF16), b_ref[...].astype(BF16), dims,
                                    preferred_element_type=F32)

        @pl.when(kk == nk - 1)
        def _():
            res = acc[...]
            outs = epilogue(res, *[e[...] for e in ex_refs]) if epilogue is not None else (res,)
            for o_ref, val in zip(out_refs, outs):
                o_ref[...] = val.astype(o_ref.dtype)

        if nh:
            @pl.when(last)
            def _():
                hosted.wait(host_in, host_out, sems)

    if ta:
        a_spec = pl.BlockSpec((tk, tm), lambda i, c, j, q: (q, i))
    else:
        a_spec = pl.BlockSpec((tm, tk), lambda i, c, j, q: (i, q))
    if chips == "b":
        b_spec = pl.BlockSpec((None, tk, tn), lambda i, c, j, q: (c, q, j))
    elif chips == "b_t":
        b_spec = pl.BlockSpec((None, tn, tk), lambda i, c, j, q: (q, j, 0))
    elif tb:
        b_spec = pl.BlockSpec((tn, tk), lambda i, c, j, q: (c * per + j, q))
    else:
        b_spec = pl.BlockSpec((tk, tn), lambda i, c, j, q: (q, c * per + j))
    mn_spec = pl.BlockSpec((tm, tn), lambda i, c, j, q: (i, c * per + j))
    if chips == "out":
        out_spec = pl.BlockSpec((None, tm, tn), lambda i, c, j, q: (c, i, j))
        out_shape = [jax.ShapeDtypeStruct((N_CHIPS, m, n // N_CHIPS), dt) for dt in out_dtypes]
    else:
        out_spec, out_shape = mn_spec, [jax.ShapeDtypeStruct((m, n), dt) for dt in out_dtypes]
    any_spec = pl.BlockSpec(memory_space=pl.ANY)
    order = ("arbitrary",) * 4 if nh else ("parallel", "parallel", "parallel", "arbitrary")
    outs = pl.pallas_call(
        body, name=name, grid=grid,
        in_specs=[a_spec, b_spec] + [mn_spec] * n_ex + [any_spec] * nh,
        out_specs=[out_spec] * n_out + [any_spec] * nh, out_shape=out_shape + hosted.out_shape,
        scratch_shapes=[pltpu.VMEM((tm, tn), F32)] + hosted.sems,
        compiler_params=_params(order),
    )(a, b, *extras, *hosted.arrays)
    res = outs[0] if n_out == 1 else outs[:n_out]
    return (res, hosted.finish(outs[n_out:])) if nh else res


def _row_spec(a, tm):
    return pl.BlockSpec((tm, a.shape[1]), lambda i: (i, 0))


def _rowwise(name, fn, rows, params, outs, tm):
    t = rows[0].shape[0]
    tm = min(tm, t)
    n_r, n_p = len(rows), len(params)

    def body(*refs):
        vals = [r[...] for r in refs[:n_r + n_p]]
        res = fn(*vals)
        for o_ref, val in zip(refs[n_r + n_p:], res):
            o_ref[...] = val.astype(o_ref.dtype)

    res = pl.pallas_call(
        body, name=name, grid=(t // tm,),
        in_specs=[_row_spec(r, tm) for r in rows] + [_full(p) for p in params],
        out_specs=[pl.BlockSpec((tm, n), lambda i: (i, 0)) for n, _ in outs],
        out_shape=[jax.ShapeDtypeStruct((t, n), dt) for n, dt in outs],
        compiler_params=_params(("parallel",)),
    )(*rows, *params)
    return res


def _rowwise_vjp(name, fn, rows, params, cts, row_grads, param_grads, tm, consts=(), addends=None,
                 emit=(), row_grad_dtypes=None):
    t = rows[0].shape[0]
    tm = min(tm, t)
    addends = addends or {}
    n_r, n_p, n_c = len(rows), len(params), len(consts)
    ct_flat = [c for group in cts for c in group]
    add_list = [addends[q] for q in sorted(addends)]
    n_ct, n_add = len(ct_flat), len(add_list)
    row_grad_dtypes = row_grad_dtypes or [F32] * len(row_grads)

    def body(*refs):
        pos = 0
        row_v = [r[...].astype(F32) for r in refs[pos:pos + n_r]]; pos += n_r
        par_v = [r[...].astype(F32) for r in refs[pos:pos + n_p]]; pos += n_p
        con_v = [r[...] for r in refs[pos:pos + n_c]]; pos += n_c
        ct_v = [r[...].astype(F32) for r in refs[pos:pos + n_ct]]; pos += n_ct
        add_v = [r[...] for r in refs[pos:pos + n_add]]; pos += n_add
        emit_refs = refs[pos:pos + len(emit)]; pos += len(emit)
        rg_refs = refs[pos:pos + len(row_grads)]; pos += len(row_grads)
        pg_refs = refs[pos:pos + len(param_grads)]

        def diff_fn(*dargs):
            rv, pv = list(row_v), list(par_v)
            for q, i in enumerate(row_grads):
                rv[i] = dargs[q]
            for q, j in enumerate(param_grads):
                pv[j] = dargs[len(row_grads) + q]
            return fn(*rv, *pv, *con_v)

        prim = [row_v[i] for i in row_grads] + [par_v[j] for j in param_grads]
        res, vjp = jax.vjp(diff_fn, *prim)
        ct_vals, q = [], 0
        for o, group in zip(res, cts):
            tot = jnp.zeros_like(o)
            for _ in group:
                tot = tot + ct_v[q]
                q += 1
            ct_vals.append(tot)
        grads = vjp(tuple(ct_vals))
        for e_ref, idx in zip(emit_refs, emit):
            e_ref[...] = res[idx].astype(e_ref.dtype)
        add_pos = {p: q for q, p in enumerate(sorted(addends))}
        for q, g_ref in enumerate(rg_refs):
            g = grads[q]
            if q in add_pos:
                g = g + add_v[add_pos[q]]
            g_ref[...] = g.astype(g_ref.dtype)

        @pl.when(pl.program_id(0) == 0)
        def _():
            for g_ref in pg_refs:
                g_ref[...] = jnp.zeros_like(g_ref)

        for q, g_ref in enumerate(pg_refs):
            g_ref[...] += grads[len(row_grads) + q]

    emit_shapes = []
    if emit:
        probe = jax.eval_shape(lambda *a: fn(*a), *[jax.ShapeDtypeStruct((tm, r.shape[1]), F32) for r in rows],
                               *[jax.ShapeDtypeStruct(p.shape, p.dtype) for p in params],
                               *[jax.ShapeDtypeStruct(c.shape, c.dtype) for c in consts])
        emit_shapes = [probe[idx].shape[1] for idx in emit]
    out_specs = ([pl.BlockSpec((tm, n), lambda i: (i, 0)) for n in emit_shapes]
                 + [_row_spec(rows[i], tm) for i in row_grads]
                 + [_full(params[j]) for j in param_grads])
    out_shape = ([jax.ShapeDtypeStruct((t, n), F32) for n in emit_shapes]
                 + [jax.ShapeDtypeStruct(rows[i].shape, dt) for i, dt in zip(row_grads, row_grad_dtypes)]
                 + [jax.ShapeDtypeStruct(params[j].shape, F32) for j in param_grads])
    return pl.pallas_call(
        body, name=name, grid=(t // tm,),
        in_specs=([_row_spec(r, tm) for r in rows] + [_full(p) for p in params] + [_full(c) for c in consts]
                  + [_row_spec(c, tm) for c in ct_flat] + [_row_spec(a, tm) for a in add_list]),
        out_specs=out_specs, out_shape=out_shape,
        compiler_params=_params(("arbitrary",)),
    )(*rows, *params, *consts, *ct_flat, *add_list)


def _norm_mod_fn(x, gain, scale, shift):
    return (_rms(x) * gain * (1.0 + scale) + shift,)


def _resid_norm_mod_fn(x, mixed, gate, gain, scale, shift):
    x1 = x + gate * mixed
    return x1, _rms(x1) * gain * (1.0 + scale) + shift


def _loss_fn(x1, ffn, target, gate, gain):
    y = _rms(x1 + gate * ffn) * gain
    err = y - target
    return (0.5 * jnp.mean(err * err, axis=-1, keepdims=True),)


def _s5_out_fn(ylin, u, d_skip, w_glu, b_glu):
    z = _gelu(ylin + d_skip * u)
    return (z * _sigmoid(_bdot(z, w_glu) + b_glu),)


def _rk_pre_fn(k, wdn, adn, gdn, w0_0, w0_1, wup_0, wup_1, a0_0, a0_1, aup_0, aup_1, g_up, k_k, k_a, seg, seg_t):
    kkr = k * k_k
    inv = 1.0 / jnp.sqrt(jnp.maximum(_seg_dot(kkr * kkr, seg, seg_t), L2_EPS * L2_EPS))
    kk = kkr * _seg_dot(inv, seg_t, seg)
    tw = jnp.tanh(wdn)
    lws, kds, acts = [], [], []
    for w0, wup, a0, aup in ((w0_0, wup_0, a0_0, aup_0), (w0_1, wup_1, a0_1, aup_1)):
        lws.append(-LW_SCALE * _sigmoid(w0 + _bdot(tw, wup)))
        act = _sigmoid(a0 + _bdot(adn, aup))
        acts.append(act)
        kds.append(k * (1.0 + (act - 1.0) * k_a))
    gate = _bdot(_sigmoid(gdn), g_up)
    return (kk, lws[0], lws[1], kds[0], kds[1], acts[0], acts[1], gate)


def _rk_post_fn(y0, y1, r, v, kd0, kd1, gate, ln_gain, ln_bias, r_k, seg, seg_t):
    y = y0 + y1
    mu = _seg_dot(_seg_dot(y, seg, seg_t) * (1.0 / RK_HEAD), seg_t, seg)
    yc = y - mu
    var = _seg_dot(yc * yc, seg, seg_t) * (1.0 / RK_HEAD)
    yn = yc * _seg_dot(lax.rsqrt(var + GN_EPS), seg_t, seg) * ln_gain + ln_bias
    bonus = _seg_dot(_seg_dot(r * (kd0 + kd1) * r_k, seg, seg_t), seg_t, seg)
    return ((yn + bonus * v) * gate,)


def _s5_prep_fn(lr0, li0, ls0, lr1, li1, ls1, b_re, b_im):
    outs = []
    for lam_re, lam_im, ls in ((lr0, li0, ls0), (lr1, li1, ls1)):
        step = jnp.exp(ls)
        mag = jnp.exp(lam_re * step)
        lbar_re = mag * jnp.cos(lam_im * step)
        lbar_im = mag * jnp.sin(lam_im * step)
        den = lam_re * lam_re + lam_im * lam_im
        nr = lbar_re - 1.0
        coef_re = (nr * lam_re + lbar_im * lam_im) / den
        coef_im = (lbar_im * lam_re - nr * lam_im) / den
        outs += [lbar_re, lbar_im, coef_re * b_re - coef_im * b_im, coef_re * b_im + coef_im * b_re]
    return tuple(outs)


def _shift_rows(x, down):
    t = x.shape[0]
    rows = lax.broadcasted_iota(jnp.int32, x.shape, 0)
    if down:
        return jnp.where(rows >= 1, pltpu.roll(x, 1, 0), 0.0)
    return jnp.where(rows < t - 1, pltpu.roll(x, t - 1, 0), 0.0)


def _token_shift(src, mu_prev, mu_next, first):
    t, n = src.shape[0], mu_prev.shape[1]

    def body(p_ref, mp_ref, mn_ref, o_ref):
        x = p_ref[...]
        o_ref[...] = x + mp_ref[...] * (_shift_rows(x, True) - x) + mn_ref[...] * (_shift_rows(x, False) - x)

    col = pl.BlockSpec((t, 128), lambda j: (0, j))
    par = pl.BlockSpec((1, 128), lambda j: (0, j))
    return pl.pallas_call(
        body, name="token_shift", grid=(n // 128,),
        in_specs=[pl.BlockSpec((t, 128), lambda j: (0, j + first)), par, par], out_specs=col,
        out_shape=jax.ShapeDtypeStruct((t, n), F32), compiler_params=_params(("parallel",)),
    )(src, mu_prev, mu_next)


def _token_shift_bwd(src, mu_prev, mu_next, dps, first):
    t, n = dps.shape

    def body(p_ref, mp_ref, mn_ref, d_ref, dp_ref, dmp_ref, dmn_ref):
        x, d, mp, mn = p_ref[...], d_ref[...], mp_ref[...], mn_ref[...]
        dp_ref[...] = d * (1.0 - mp - mn) + _shift_rows(d * mp, False) + _shift_rows(d * mn, True)
        dmp_ref[...] = jnp.sum(d * (_shift_rows(x, True) - x), axis=0, keepdims=True)
        dmn_ref[...] = jnp.sum(d * (_shift_rows(x, False) - x), axis=0, keepdims=True)

    col = pl.BlockSpec((t, 128), lambda j: (0, j))
    par = pl.BlockSpec((1, 128), lambda j: (0, j))
    return pl.pallas_call(
        body, name="token_shift_bwd", grid=(n // 128,),
        in_specs=[pl.BlockSpec((t, 128), lambda j: (0, j + first)), par, par, col],
        out_specs=[col, par, par],
        out_shape=[jax.ShapeDtypeStruct((t, n), F32), jax.ShapeDtypeStruct((1, n), F32),
                   jax.ShapeDtypeStruct((1, n), F32)],
        compiler_params=_params(("parallel",)),
    )(src, mu_prev, mu_next, dps)


N_SEG = 32
S5_BLOCKS = 32
S5_PER_IN = 4


def _scan_in_place(sr_ref, si_ref, ar, ai, carry_ref, reverse):
    seg_len = sr_ref.shape[0] // N_SEG
    ng = N_SEG // 8

    def rows(i, grp):
        first = (seg_len - 1 - i if reverse else i) * N_SEG + 8 * grp
        return pl.ds(pl.multiple_of(first, 8), 8)

    zero = jnp.zeros((8, 128), F32)
    one = jnp.ones((8, 128), F32)

    def local(i, c):
        pr, pi = c[-2:]
        out = []
        for grp in range(ng):
            sr, si = c[2 * grp], c[2 * grp + 1]
            nr = ar * sr - ai * si + sr_ref[rows(i, grp), :]
            ni = ar * si + ai * sr + si_ref[rows(i, grp), :]
            sr_ref[rows(i, grp), :] = nr
            si_ref[rows(i, grp), :] = ni
            out += [nr, ni]
        return tuple(out) + (ar * pr - ai * pi, ar * pi + ai * pr)

    ends = lax.fori_loop(0, seg_len, local, (zero,) * (2 * ng) + (one, zero))
    qr, qi = ends[-2][0:1], ends[-1][0:1]
    order = list(range(N_SEG - 1, -1, -1)) if reverse else list(range(N_SEG))
    cr = jnp.zeros((1, 128), F32)
    ci = jnp.zeros((1, 128), F32)
    for j in order:
        carry_ref[j:j + 1, :] = cr
        carry_ref[N_SEG + j:N_SEG + j + 1, :] = ci
        grp, sub = divmod(j, 8)
        lr, li = ends[2 * grp][sub:sub + 1], ends[2 * grp + 1][sub:sub + 1]
        cr, ci = lr + qr * cr - qi * ci, li + qr * ci + qi * cr
    carries = [(carry_ref[8 * grp:8 * grp + 8, :], carry_ref[N_SEG + 8 * grp:N_SEG + 8 * grp + 8, :])
               for grp in range(ng)]

    def fix(i, c):
        pr, pi = c
        npr, npi = ar * pr - ai * pi, ar * pi + ai * pr
        for grp in range(ng):
            cr8, ci8 = carries[grp]
            sr_ref[rows(i, grp), :] = sr_ref[rows(i, grp), :] + npr * cr8 - npi * ci8
            si_ref[rows(i, grp), :] = si_ref[rows(i, grp), :] + npr * ci8 + npi * cr8
        return npr, npi

    lax.fori_loop(0, seg_len, fix, (one, zero))


def _interleave(x):
    t, c = x.shape
    return jnp.transpose(x.reshape(N_SEG, t // N_SEG, c), (1, 0, 2)).reshape(t, c)


def _deinterleave(x):
    t, c = x.shape
    return jnp.transpose(x.reshape(t // N_SEG, N_SEG, c), (1, 0, 2)).reshape(t, c)


def _lag_sums(lr_ref, li_ref, sr_ref, si_ref, earlier):
    t = lr_ref.shape[0]
    body, edge = pl.ds(N_SEG, t - N_SEG), pl.ds(0, N_SEG)
    far = pl.ds(t - N_SEG, N_SEG)
    rows = lax.broadcasted_iota(jnp.int32, (N_SEG, 128), 0)
    if earlier:
        lam_main, s_main, lam_edge = body, pl.ds(0, t - N_SEG), edge
        wrap = lambda ref: jnp.where(rows >= 1, pltpu.roll(ref[far, :], 1, 0), 0.0)
    else:
        lam_main, s_main, lam_edge = pl.ds(0, t - N_SEG), body, far
        wrap = lambda ref: jnp.where(rows < N_SEG - 1, pltpu.roll(ref[edge, :], N_SEG - 1, 0), 0.0)
    lr, li, sr, si = lr_ref[lam_main, :], li_ref[lam_main, :], sr_ref[s_main, :], si_ref[s_main, :]
    er, ei, pr, pi = lr_ref[lam_edge, :], li_ref[lam_edge, :], wrap(sr_ref), wrap(si_ref)
    re = jnp.sum(lr * sr + li * si, axis=0, keepdims=True) + jnp.sum(er * pr + ei * pi, axis=0, keepdims=True)
    im = jnp.sum(li * sr - lr * si, axis=0, keepdims=True) + jnp.sum(ei * pr - er * pi, axis=0, keepdims=True)
    return re, im


def _dot_bf16(a, b, dims=(((1,), (0,)), ((), ()))):
    return lax.dot_general(a.astype(BF16), b.astype(BF16), dims, preferred_element_type=F32)


NT_DIMS = (((1,), (1,)), ((), ()))
TN_DIMS = (((0,), (0,)), ((), ()))


def _s5_specs(t):
    blk = pl.BlockSpec((None, t, 128), lambda i, q: (S5_PER_IN * i + q, 0, 0))
    mat = pl.BlockSpec((None, 128, 128), lambda i, q: (S5_PER_IN * i + q, 0, 0))
    vec = pl.BlockSpec((None, 1, 128), lambda i, q: (S5_PER_IN * i + q, 0, 0))
    chan = pl.BlockSpec((t, 128), lambda i, q: (0, i))
    return blk, mat, vec, chan


S5_GRID = (S5_BLOCKS // S5_PER_IN, S5_PER_IN)


def _s5_forward(name, u, b_re, b_im, l_re, l_im, reverse, other=None, c_re=None, c_im_neg=None):
    t = u.shape[0]
    project = other is not None
    blk, mat, vec, chan = _s5_specs(t)

    def body(*refs):
        u_ref, br_ref, bi_ref, lr_ref, li_ref = refs[:5]
        if project:
            or_ref, oi_ref, cr_ref, ci_ref, sr_ref, si_ref, y_ref, carry_ref = refs[5:]
        else:
            sr_ref, si_ref, carry_ref = refs[5:]
        uv = u_ref[...]
        sr_ref[...] = _dot_bf16(uv, br_ref[...])
        si_ref[...] = _dot_bf16(uv, bi_ref[...])
        ar = jnp.broadcast_to(lr_ref[...], (8, 128))
        ai = jnp.broadcast_to(li_ref[...], (8, 128))
        _scan_in_place(sr_ref, si_ref, ar, ai, carry_ref, reverse)
        if project:
            y = (_dot_bf16(sr_ref[...] + or_ref[...], cr_ref[...])
                 + _dot_bf16(si_ref[...] + oi_ref[...], ci_ref[...]))

            @pl.when(pl.program_id(1) == 0)
            def _():
                y_ref[...] = y

            @pl.when(pl.program_id(1) != 0)
            def _():
                y_ref[...] += y

    state = jax.ShapeDtypeStruct((S5_BLOCKS, t, 128), F32)
    ins = [u, b_re, b_im, l_re, l_im] + ([other[0], other[1], c_re, c_im_neg] if project else [])
    in_specs = [chan, mat, mat, vec, vec] + ([blk, blk, mat, mat] if project else [])
    return pl.pallas_call(
        body, name=name, grid=S5_GRID, in_specs=in_specs,
        out_specs=[blk, blk] + ([chan] if project else []),
        out_shape=[state, state] + ([jax.ShapeDtypeStruct((t, S5_WIDTH), F32)] if project else []),
        scratch_shapes=[pltpu.VMEM((2 * N_SEG, 128), F32)],
        compiler_params=_params(("arbitrary", "arbitrary")),
    )(*ins)


def _s5_backward(name, dy, u, du_in, states, other, b_re, b_im, c_re, c_im_neg, l_re, l_im, reverse):
    t = u.shape[0]
    with_c = other is not None
    blk, mat, vec, chan = _s5_specs(t)

    def body(*refs):
        dy_ref, u_ref, du_in_ref, sr_ref, si_ref = refs[:5]
        pos = 5
        if with_c:
            or_ref, oi_ref = refs[5:7]
            pos = 7
        br_ref, bi_ref, cr_ref, ci_ref, lr_ref, li_ref = refs[pos:pos + 6]
        outs = refs[pos + 6:]
        du_ref, dbr_ref, dbi_ref, dlr_ref, dli_ref = outs[:5]
        lam_r, lam_i, carry_ref = outs[-3:]
        dyv, uv = dy_ref[...], u_ref[...]
        lam_r[...] = _dot_bf16(dyv, cr_ref[...], NT_DIMS)
        lam_i[...] = _dot_bf16(dyv, ci_ref[...], NT_DIMS)
        ar = jnp.broadcast_to(lr_ref[...], (8, 128))
        ai = -jnp.broadcast_to(li_ref[...], (8, 128))
        _scan_in_place(lam_r, lam_i, ar, ai, carry_ref, not reverse)
        lr, li = lam_r[...], lam_i[...]
        dlr_ref[...], dli_ref[...] = _lag_sums(lam_r, lam_i, sr_ref, si_ref, not reverse)
        dbr_ref[...] = _dot_bf16(uv, lr, TN_DIMS)
        dbi_ref[...] = _dot_bf16(uv, li, TN_DIMS)
        du = _dot_bf16(lr, br_ref[...], NT_DIMS) + _dot_bf16(li, bi_ref[...], NT_DIMS)

        @pl.when(pl.program_id(1) == 0)
        def _():
            du_ref[...] = du_in_ref[...] + du

        @pl.when(pl.program_id(1) != 0)
        def _():
            du_ref[...] += du

        if with_c:
            dcr_ref, dci_ref = outs[5:7]
            dcr_ref[...] = _dot_bf16(sr_ref[...] + or_ref[...], dyv, TN_DIMS)
            dci_ref[...] = _dot_bf16(si_ref[...] + oi_ref[...], dyv, TN_DIMS)

    mats = jax.ShapeDtypeStruct((S5_BLOCKS, 128, 128), F32)
    vecs = jax.ShapeDtypeStruct((S5_BLOCKS, 1, 128), F32)
    ins = [dy, u, du_in, states[0], states[1]] + ([other[0], other[1]] if with_c else [])
    ins += [b_re, b_im, c_re, c_im_neg, l_re, l_im]
    in_specs = [chan, chan, chan, blk, blk] + ([blk, blk] if with_c else []) + [mat] * 4 + [vec] * 2
    return pl.pallas_call(
        body, name=name, grid=S5_GRID, in_specs=in_specs,
        out_specs=[chan, mat, mat, vec, vec] + ([mat, mat] if with_c else []),
        out_shape=[jax.ShapeDtypeStruct((t, S5_WIDTH), F32), mats, mats, vecs, vecs] + ([mats, mats] if with_c else []),
        scratch_shapes=[pltpu.VMEM((t, 128), F32), pltpu.VMEM((t, 128), F32), pltpu.VMEM((2 * N_SEG, 128), F32)],
        compiler_params=_params(("arbitrary", "arbitrary")),
    )(*ins)


def _ein(passes, spec, a, b):
    if passes == 6:
        return jnp.einsum(spec, a, b, precision=HI, preferred_element_type=F32)
    a_hi, b_hi = a.astype(BF16), b.astype(BF16)
    if passes == 1:
        return jnp.einsum(spec, a_hi, b_hi, preferred_element_type=F32)
    a_lo = (a - a_hi.astype(F32)).astype(BF16)
    b_lo = (b - b_hi.astype(F32)).astype(BF16)
    cross = jnp.einsum(spec, a_hi, b_lo, preferred_element_type=F32)
    if spec.startswith('hik'):
        m = a.shape[1]
        stacked = jnp.einsum(spec, jnp.concatenate([a_hi, a_lo], axis=1), b_hi, preferred_element_type=F32)
        return stacked[:, :m] + stacked[:, m:] + cross
    return (jnp.einsum(spec, a_hi, b_hi, preferred_element_type=F32) + cross
            + jnp.einsum(spec, a_lo, b_hi, preferred_element_type=F32))


@jax.custom_vjp
def _tri_mm(tri, tri_t, z):
    n = z.shape[2]
    hi = z.astype(BF16)
    rest = z - hi.astype(F32)
    mid = rest.astype(BF16)
    lo = (rest - mid.astype(F32)).astype(BF16)
    out = jnp.einsum('hik,hkj->hij', tri.astype(BF16), jnp.concatenate([hi, mid, lo], axis=2),
                     preferred_element_type=F32)
    return out[:, :, :n] + out[:, :, n:2 * n] + out[:, :, 2 * n:]


def _tri_mm_bwd(res, g):
    tri, tri_t = res
    return jnp.zeros_like(tri), jnp.zeros_like(tri_t), _tri_mm(tri_t, tri, g)


_tri_mm.defvjp(lambda tri, tri_t, z: (_tri_mm(tri, tri_t, z), (tri, tri_t)), _tri_mm_bwd)


def _chunk_cumsum(lw, incl, incl_t):
    shape = (lw.shape[0],) + incl.shape
    return _tri_mm(jnp.broadcast_to(incl.astype(F32), shape), jnp.broadcast_to(incl_t.astype(F32), shape), lw)


@functools.partial(jax.custom_vjp, nondiff_argnums=(0,))
def _bmm(p, a, b):
    return _ein(p, 'hik,hkj->hij', a, b)


@functools.partial(jax.custom_vjp, nondiff_argnums=(0,))
def _bmm_nt(p, a, b):
    return _ein(p, 'hik,hjk->hij', a, b)


@functools.partial(jax.custom_vjp, nondiff_argnums=(0,))
def _bmm_tn(p, a, b):
    return _ein(p, 'hki,hkj->hij', a, b)


_bmm.defvjp(lambda p, a, b: (_bmm(p, a, b), (a, b)),
            lambda p, res, g: (_bmm_nt(p, g, res[1]), _bmm_tn(p, res[0], g)))
_bmm_nt.defvjp(lambda p, a, b: (_bmm_nt(p, a, b), (a, b)),
               lambda p, res, g: (_bmm(p, g, res[1]), _bmm_tn(p, g, res[0])))
_bmm_tn.defvjp(lambda p, a, b: (_bmm_tn(p, a, b), (a, b)),
               lambda p, res, g: (_bmm_nt(p, res[1], g), _bmm(p, res[0], g)))


@jax.custom_vjp
def _split_rows(x):
    c = x.shape[1] // 2
    return x[:, :c], x[:, c:]


_split_rows.defvjp(lambda x: (_split_rows(x), None), lambda _, g: (jnp.concatenate(g, axis=1),))


def _stack_rows(a, b):
    return jnp.concatenate([a, b], axis=1)


def _nilpotent_inverse(l_mat):
    c = l_mat.shape[1]
    ps = RK_PASSES["solve"]
    row = lax.broadcasted_iota(jnp.int32, (c, c), 0)
    col = lax.broadcasted_iota(jnp.int32, (c, c), 1)
    x = -l_mat
    inv = jnp.where(row == col, 1.0, 0.0) + x
    power = _bmm(ps, x, x)
    span = 2
    while 2 * span < c:
        step, power = _split_rows(_bmm(ps, _stack_rows(inv, power), power))
        inv = inv + step
        span *= 2
    return inv + _bmm(ps, inv, power)


@jax.custom_vjp
def _nilpotent_solve(l_mat, rhs):
    return _bmm(RK_PASSES["solve"], _nilpotent_inverse(l_mat), rhs)


def _nilpotent_solve_fwd(l_mat, rhs):
    inv = _nilpotent_inverse(l_mat)
    u = _bmm(RK_PASSES["solve"], inv, rhs)
    return u, (inv, u)


def _nilpotent_solve_bwd(res, g):
    inv, u = res
    d_rhs = _bmm_tn(RK_PASSES["solve"], inv, g)
    return -_bmm_nt(RK_PASSES["solve"], d_rhs, u), d_rhs


_nilpotent_solve.defvjp(_nilpotent_solve_fwd, _nilpotent_solve_bwd)


def _rk_chunk(s0, r, lw, k, v, kk, a, reverse):
    h, c, n = r.shape
    row = lax.broadcasted_iota(jnp.int32, (c, c), 0)
    col = lax.broadcasted_iota(jnp.int32, (c, c), 1)
    incl = (row <= col) if reverse else (row >= col)
    strict = (row < col) if reverse else (row > col)
    cum = _chunk_cumsum(lw, incl, (row >= col) if reverse else (row <= col))
    g_in = jnp.exp(cum)
    g_inv = jnp.exp(-cum)
    kap = kk * jnp.exp(cum - lw)
    beta = kk * a * g_inv
    kt = k * g_inv
    rt = r * g_in
    p, ps = RK_PASSES, RK_PASSES["solve"]
    both = _stack_rows(kap, rt)
    kap_beta, rt_beta = _split_rows(_bmm_nt(ps, both, beta))
    kap_kt, rt_kt = _split_rows(_bmm_nt(p["kt"], both, kt))
    kap_s0, rt_s0 = _split_rows(_bmm_nt(p["s0"], both, s0))
    l_mat = jnp.where(strict, kap_beta, 0.0)
    rhs = kap_s0 + _bmm(p["akk_v"], jnp.where(strict, kap_kt, 0.0), v)
    u = _nilpotent_solve(l_mat, rhs)
    y = (rt_s0 + _bmm(p["ark_v"], jnp.where(incl, rt_kt, 0.0), v)
         - _bmm(p["arb_u"], jnp.where(incl, rt_beta, 0.0), u))
    s1 = ((s0 + _bmm_tn(p["state"], _stack_rows(v, -u), _stack_rows(kt, beta)))
          * jnp.exp(jnp.sum(lw, axis=1, keepdims=True)))
    return y, s1


class _Plan:
    def __init__(self, arrays, out_shape, sems, start, wait, finish):
        self.arrays, self.out_shape, self.sems = list(arrays), list(out_shape), list(sems)
        self.start, self.wait, self.finish = start, wait, finish


_NO_PLAN = _Plan([], [], [], lambda *_: None, lambda *_: None, lambda outs: [])


def _join_plans(plans):
    def cut(seq, sizes):
        out, pos = [], 0
        for s in sizes:
            out.append(seq[pos:pos + s])
            pos += s
        return out

    n_arr, n_sem = [len(p.arrays) for p in plans], [len(p.sems) for p in plans]

    def run(which):
        def go(in_refs, out_refs, sems):
            for p, i, o, s in zip(plans, cut(in_refs, n_arr), cut(out_refs, n_arr), cut(sems, n_sem)):
                getattr(p, which)(i, o, s)
        return go

    return _Plan([a for p in plans for a in p.arrays], [s for p in plans for s in p.out_shape],
                 [s for p in plans for s in p.sems], run("start"), run("wait"),
                 lambda outs: [p.finish(o) for p, o in zip(plans, cut(outs, n_arr))])


def _split_heads(x):
    return jnp.stack([x[:, RK_HEAD * i:RK_HEAD * (i + 1)] for i in range(RK_HEADS)], axis=0)


def _store_heads(ref, x):
    for i in range(RK_HEADS):
        ref[:, RK_HEAD * i:RK_HEAD * (i + 1)] = x[i]


def _rk_core_fwd(name, r, lw, k, v, kk, a, reverse, chunk, hosted=None):
    t = r.shape[0]
    h, n = RK_HEADS, RK_HEAD
    nc = t // chunk

    def idx(i):
        return nc - 1 - i if reverse else i

    hosted = hosted or _NO_PLAN
    nh = len(hosted.arrays)

    def body(r_ref, lw_ref, k_ref, v_ref, kk_ref, a_ref, *rest):
        host_in, (y_ref, ck_ref), host_out = rest[:nh], rest[nh:nh + 2], rest[nh + 2:2 * nh + 2]
        s_ref, sems = rest[2 * nh + 2], rest[2 * nh + 3:]

        @pl.when(pl.program_id(0) == 0)
        def _():
            s_ref[...] = jnp.zeros_like(s_ref)
            hosted.start(host_in, host_out, sems)

        s0 = s_ref[...]
        ck_ref[0] = s0
        ops = [_split_heads(ref[...]) for ref in (r_ref, lw_ref, k_ref, v_ref, kk_ref, a_ref)]
        y, s1 = _rk_chunk(s0, *ops, reverse)
        _store_heads(y_ref, y)
        s_ref[...] = s1

        @pl.when(pl.program_id(0) == nc - 1)
        def _():
            hosted.wait(host_in, host_out, sems)

    blk = pl.BlockSpec((chunk, RK_WIDTH), lambda i: (idx(i), 0))
    any_spec = pl.BlockSpec(memory_space=pl.ANY)
    res = pl.pallas_call(
        body, name=name, grid=(nc,), in_specs=[blk] * 6 + [any_spec] * nh,
        out_specs=[blk, pl.BlockSpec((1, h, n, n), lambda i: (idx(i), 0, 0, 0))] + [any_spec] * nh,
        out_shape=[jax.ShapeDtypeStruct((t, RK_WIDTH), F32), jax.ShapeDtypeStruct((nc, h, n, n), F32)]
        + hosted.out_shape,
        scratch_shapes=[pltpu.VMEM((h, n, n), F32)] + hosted.sems,
        compiler_params=_params(("arbitrary",)),
    )(r, lw, k, v, kk, a, *hosted.arrays)
    return res[0], res[1], hosted.finish(res[2:])


def _rk_core_bwd(name, r, lw, k, v, kk, a, ck, dy, reverse, chunk, hosted=None):
    t = r.shape[0]
    h, n = RK_HEADS, RK_HEAD
    nc = t // chunk
    hosted = hosted or _NO_PLAN
    nh = len(hosted.arrays)

    def idx(i):
        return i if reverse else nc - 1 - i

    def body(r_ref, lw_ref, k_ref, v_ref, kk_ref, a_ref, ck_ref, dy_ref, *rest):
        host_in, out_refs, host_out = rest[:nh], rest[nh:nh + 6], rest[nh + 6:2 * nh + 6]
        ds_ref, sems = rest[2 * nh + 6], rest[2 * nh + 7:]

        @pl.when(pl.program_id(0) == 0)
        def _():
            ds_ref[...] = jnp.zeros_like(ds_ref)
            hosted.start(host_in, host_out, sems)

        fn = functools.partial(_rk_chunk, reverse=reverse)
        ops = [_split_heads(ref[...]) for ref in (r_ref, lw_ref, k_ref, v_ref, kk_ref, a_ref)]
        _, vjp = jax.vjp(fn, ck_ref[0], *ops)
        grads = vjp((_split_heads(dy_ref[...]), ds_ref[...]))
        ds_ref[...] = grads[0]
        for o_ref, g in zip(out_refs, grads[1:]):
            _store_heads(o_ref, g)

        @pl.when(pl.program_id(0) == nc - 1)
        def _():
            hosted.wait(host_in, host_out, sems)

    blk = pl.BlockSpec((chunk, RK_WIDTH), lambda i: (idx(i), 0))
    any_spec = pl.BlockSpec(memory_space=pl.ANY)
    res = pl.pallas_call(
        body, name=name, grid=(nc,),
        in_specs=[blk] * 6 + [pl.BlockSpec((1, h, n, n), lambda i: (idx(i), 0, 0, 0)), blk] + [any_spec] * nh,
        out_specs=[blk] * 6 + [any_spec] * nh,
        out_shape=[jax.ShapeDtypeStruct((t, RK_WIDTH), F32)] * 6 + hosted.out_shape,
        scratch_shapes=[pltpu.VMEM((h, n, n), F32)] + hosted.sems,
        compiler_params=_params(("arbitrary",)),
    )(r, lw, k, v, kk, a, ck, dy, *hosted.arrays)
    return res[:6], hosted.finish(res[6:])


def _s5_band_place():
    return jax.nn.one_hot(jnp.arange(S5_BLOCKS) % S5_PER_IN, S5_PER_IN, dtype=F32)


def _s5_in_blocks(bbar):
    b = jnp.transpose(bbar.reshape(S5_BLOCKS, 2, S5_STATE, S5_GROUP), (0, 1, 3, 2))
    band = jnp.einsum('jghp,gk->jghkp', b, jnp.eye(2, dtype=F32)).reshape(S5_BLOCKS, 32, 128)
    return jnp.einsum('jrc,jq->jqrc', band, _s5_band_place()).reshape(S5_BLOCKS, 128, 128)


def _s5_in_unblock(mats):
    band = jnp.einsum('jqrc,jq->jrc', mats.reshape(S5_BLOCKS, S5_PER_IN, 32, 128), _s5_band_place())
    diag = jnp.einsum('jghgp->jghp', band.reshape(S5_BLOCKS, 2, S5_GROUP, 2, S5_STATE))
    return jnp.transpose(diag, (0, 1, 3, 2)).reshape(S5_CH, S5_GROUP)


def _s5_out_blocks(c):
    ct = jnp.transpose(c.reshape(S5_BLOCKS, 2, S5_GROUP, S5_STATE), (0, 1, 3, 2))
    band = jnp.einsum('jgph,gk->jgpkh', ct, jnp.eye(2, dtype=F32)).reshape(S5_BLOCKS, 128, 32)
    return jnp.einsum('jrc,jq->jrqc', band, _s5_band_place()).reshape(S5_BLOCKS, 128, 128)


def _s5_out_unblock(mats):
    band = jnp.einsum('jrqc,jq->jrc', mats.reshape(S5_BLOCKS, 128, S5_PER_IN, 32), _s5_band_place())
    diag = jnp.einsum('jgpgh->jgph', band.reshape(S5_BLOCKS, 2, S5_STATE, 2, S5_GROUP))
    return jnp.transpose(diag, (0, 1, 3, 2)).reshape(S5_GROUPS, S5_GROUP, S5_STATE)


def _head_indicator():
    ch = lax.broadcasted_iota(jnp.int32, (RK_WIDTH, 128), 0) // RK_HEAD
    hd = lax.broadcasted_iota(jnp.int32, (RK_WIDTH, 128), 1)
    seg = (ch == hd).astype(F32)
    return seg, seg.T


def _add_epilogue(acc, e):
    return (acc + e,)


def _local_step(x, target, mod, wt, chunk=RK_CHUNK, ffn_shards=None, mixer_shards=None):
    t = x.shape[0]
    wt = dict(wt)
    sh1, sc1, gt1, sh2, sc2, gt2 = mod
    seg, seg_t = _head_indicator()
    g = {}

    (h1,) = _rowwise("norm1", _norm_mod_fn, [x], [wt["norm1_gain"], sc1, sh1], [(D_MODEL, BF16)], 256)
    if mixer_shards is None:
        proj = _matmul("proj", h1, wt["w_in"])
    else:
        proj, gathered = _matmul("proj", h1, wt["w_in"], hosted=_gather_halves_plan(mixer_shards[0]))
        wt.update(mixer_shards[1](gathered))
    u = proj[:, :S5_WIDTH]
    ps = _token_shift(proj, wt["mu_prev"], wt["mu_next"], first=S5_WIDTH // 128)
    r, k, v = ps[:, :1024], ps[:, 1024:2048], ps[:, 2048:3072]
    wdn, adn, gdn = ps[:, 3072:3200], ps[:, 3200:3328], ps[:, 3328:RK_PAD]

    prep_rows = [wt["lam_re"][0], wt["lam_im"][0], wt["log_step"][0], wt["lam_re"][1], wt["lam_im"][1],
                 wt["log_step"][1], wt["b_re"], wt["b_im"]]
    col1, col16 = (1, F32), (S5_GROUP, F32)
    prep = _rowwise("s5_prep", _s5_prep_fn, prep_rows, [], [col1, col1, col16, col16] * 2, 512)
    lbar = [tuple(prep[4 * d + q].reshape(S5_BLOCKS, 1, 128) for q in range(2)) for d in range(2)]
    b_blk = [tuple(_s5_in_blocks(prep[4 * d + 2 + q]) for q in range(2)) for d in range(2)]
    c_blk = (_s5_out_blocks(wt["c_re"]), -_s5_out_blocks(wt["c_im"]))
    u_il = _interleave(u)
    state0 = _s5_forward("s5_fwd0", u_il, *b_blk[0], *lbar[0], reverse=False)
    s1_re, s1_im, ylin_il = _s5_forward("s5_fwd1", u_il, *b_blk[1], *lbar[1], reverse=True, other=state0,
                                        c_re=c_blk[0], c_im_neg=c_blk[1])
    ylin = _deinterleave(ylin_il)
    states = [tuple(state0), (s1_re, s1_im)]
    s5_par = [wt["s5_d"], wt["s5_w_glu"], wt["s5_b_glu"]]
    (o_s5,) = _rowwise("s5_out", _s5_out_fn, [ylin, u], s5_par, [(S5_WIDTH, BF16)], 256)

    pre_par = [wt["w0"][0], wt["w0"][1], wt["w_up"][0], wt["w_up"][1], wt["a0"][0], wt["a0"][1],
               wt["a_up"][0], wt["a_up"][1], wt["g_up"], wt["k_k"], wt["k_a"]]
    pre = _rowwise("rk_pre", _rk_pre_fn, [k, wdn, adn, gdn], pre_par + [seg, seg_t], [(RK_WIDTH, F32)] * 8, 256)
    kk, lw, kd, act, gate = pre[0], pre[1:3], pre[3:5], pre[5:7], pre[7]
    core_in, ys, cks = [], [], []
    for d in range(2):
        ops = (r, lw[d], kd[d], v, kk, act[d])
        plan = _gather_halves_plan([ffn_shards[d]]) if ffn_shards is not None else None
        y, ck, gathered = _rk_core_fwd(f"rk_core{d}", *ops, reverse=(d == 1), chunk=min(chunk, t), hosted=plan)
        if gathered:
            wt["ffn_w1" if d == 0 else "ffn_w2"] = gathered[0] if d == 0 else gathered[0].reshape(FFN, D_MODEL)
        core_in.append(ops)
        ys.append(y)
        cks.append(ck)
    post_rows = [ys[0], ys[1], r, v, kd[0], kd[1], gate]
    post_par = [wt["ln_gain"], wt["ln_bias"], wt["r_k"]]
    (o_rk,) = _rowwise("rk_post", _rk_post_fn, post_rows, post_par + [seg, seg_t], [(RK_WIDTH, BF16)], 256)

    o = jnp.concatenate([o_s5, o_rk], axis=1)
    mixed = _matmul("mix_out", o, wt["w_out"])
    n2_par = [gt1, wt["norm2_gain"], sc2, sh2]
    x1, h2 = _rowwise("norm2", _resid_norm_mod_fn, [x, mixed], n2_par, [(D_MODEL, F32), (D_MODEL, BF16)], 256)
    f1, hid = _matmul("ffn1", h2, wt["ffn_w1"], out_dtypes=(F32, BF16), chips="b",
                      epilogue=lambda acc: (acc, jnp.square(jnp.maximum(acc, 0.0))))
    ffn = _matmul("ffn2", hid, wt["ffn_w2"])

    ones = jnp.ones((t, 1), F32)
    loss_rows, dx1, dffn, g_gt2, g["final_gain"] = _rowwise_vjp(
        "loss", _loss_fn, [x1, ffn, target], [gt2, wt["final_gain"]], [[ones]], [0, 1], [0, 1], 256, emit=(0,),
        row_grad_dtypes=[F32, BF16])
    df1 = _matmul("ffn2_dx", dffn, wt["ffn_w2"], tb=True, extras=(f1,), out_dtypes=(BF16,),
                  epilogue=lambda acc, f: (acc * (2.0 * jnp.maximum(f, 0.0)),))
    g["ffn_w2"] = _matmul("ffn2_dw", hid, dffn, ta=True)
    g["ffn_w1"] = _matmul("ffn1_dw", h2, df1, ta=True, chips="out")
    if ffn_shards is None:
        dh2 = _matmul("ffn1_dx", df1, wt["ffn_w1"], tb=True, chips="b_t")
    else:
        ffn_pieces = [g.pop("ffn_w1"), g.pop("ffn_w2").reshape(N_CHIPS, -1, D_MODEL)]
        dh2, from_sibling = _matmul("ffn1_dx", df1, wt["ffn_w1"], tb=True, chips="b_t",
                                    hosted=_other_half_plan(ffn_pieces))
        ffn_sums = [_pair_sum("pair_" + n, piece, other, BF16)
                    for n, piece, other in zip(FFN_SHARDED, ffn_pieces, from_sibling)]
    dx_a, dmixed, g_gt1, g["norm2_gain"], g_sc2, g_sh2 = _rowwise_vjp(
        "norm2_bwd", _resid_norm_mod_fn, [x, mixed], n2_par, [[dx1], [dh2]], [0, 1], [0, 1, 2, 3], 256,
        row_grad_dtypes=[F32, BF16])
    do = _matmul("mix_out_dx", dmixed, wt["w_out"], tb=True)
    g["w_out"] = _matmul("mix_out_dw", o, dmixed, ta=True)
    do_s5, do_rk = do[:, :S5_WIDTH], do[:, S5_WIDTH:]

    dylin, du, g["s5_d"], g["s5_w_glu"], g["s5_b_glu"] = _rowwise_vjp(
        "s5_out_bwd", _s5_out_fn, [ylin, u], s5_par, [[do_s5]], [0, 1], [0, 1, 2], 256)
    prep_cts = []
    dylin_il, du_il = _interleave(dylin), _interleave(du)
    for d in range(2):
        res = _s5_backward(f"s5_bwd{d}", dylin_il, u_il, du_il, states[d], states[1] if d == 0 else None,
                           *b_blk[d], *c_blk, *lbar[d], reverse=(d == 1))
        du_il, db_re, db_im, dl_re, dl_im = res[:5]
        if d == 0:
            g["c_re"], g["c_im"] = _s5_out_unblock(res[5]), -_s5_out_unblock(res[6])
        prep_cts += [[dl_re.reshape(S5_CH, 1)], [dl_im.reshape(S5_CH, 1)], [_s5_in_unblock(db_re)],
                     [_s5_in_unblock(db_im)]]
    du = _deinterleave(du_il)
    pg = _rowwise_vjp("s5_prep_bwd", _s5_prep_fn, prep_rows, [], prep_cts, list(range(8)), [], 512)
    g["lam_re"], g["lam_im"], g["log_step"] = (pg[0], pg[3]), (pg[1], pg[4]), (pg[2], pg[5])
    g["b_re"], g["b_im"] = pg[6], pg[7]

    pb = _rowwise_vjp("rk_post_bwd", _rk_post_fn, post_rows, post_par, [[do_rk]], [0, 2, 3, 4, 5, 6], [0, 1, 2],
                      128, consts=[seg, seg_t])
    dy, dr_b, dv_b, dkd_b, dgate = pb[0], pb[1], pb[2], pb[3:5], pb[5]
    g["ln_gain"], g["ln_bias"], g["r_k"] = pb[6], pb[7], pb[8]
    cg = []
    for d in range(2):
        plan = None
        if d == 0 and ffn_shards is not None:
            plan = _exchange_plan(ffn_sums, CHIP_PEERS, N_CHIPS, scatter=True)
        grads, arrived = _rk_core_bwd(f"rk_core{d}_bwd", *core_in[d], cks[d], dy, reverse=(d == 1),
                                      chunk=min(chunk, t), hosted=plan)
        if arrived:
            g["ffn_arrived"] = arrived
        cg.append(grads)
    pre_cts = [[cg[0][4], cg[1][4]], [cg[0][1]], [cg[1][1]], [cg[0][2], dkd_b[0]], [cg[1][2], dkd_b[1]],
               [cg[0][5]], [cg[1][5]], [dgate]]
    qb = _rowwise_vjp("rk_pre_bwd", _rk_pre_fn, [k, wdn, adn, gdn], pre_par, pre_cts, [0, 1, 2, 3],
                      list(range(11)), 128, consts=[seg, seg_t])
    dk, dwdn, dadn, dgdn = qb[:4]
    g["w0"], g["w_up"], g["a0"], g["a_up"] = (qb[4], qb[5]), (qb[6], qb[7]), (qb[8], qb[9]), (qb[10], qb[11])
    g["g_up"], g["k_k"], g["k_a"] = qb[12], qb[13], qb[14]
    dr, dv = _rowwise("rk_sum", lambda a, b, c, e, f, h: (a + b + c, e + f + h),
                      [cg[0][0], cg[1][0], dr_b, cg[0][3], cg[1][3], dv_b], [], [(RK_WIDTH, F32)] * 2, 256)
    dps = jnp.concatenate([dr, dk, dv, dwdn, dadn, dgdn], axis=1)
    dp, g["mu_prev"], g["mu_next"] = _token_shift_bwd(proj, wt["mu_prev"], wt["mu_next"], dps,
                                                      first=S5_WIDTH // 128)

    dproj = jnp.concatenate([du, dp], axis=1).astype(BF16)
    dh1 = _matmul("proj_dx", dproj, wt["w_in"], tb=True)
    g["w_in"] = _matmul("proj_dw", h1, dproj, ta=True)
    grad_x, g["norm1_gain"], g_sc1, g_sh1 = _rowwise_vjp(
        "norm1_bwd", _norm_mod_fn, [x], [wt["norm1_gain"], sc1, sh1], [[dh1]], [0], [0, 1, 2], 256,
        addends={0: dx_a})
    g["mod"] = [g_sh1, g_sc1, g_gt1, g_sh2, g_sc2, g_gt2]
    return loss_rows, grad_x, g


CHIP_PEERS = ((1, 0, 0), (0, 1, 0), (1, 1, 0))
ALL_PEERS = ((0, 0, 1), (0, 1, 0), (0, 1, 1), (1, 0, 0), (1, 0, 1), (1, 1, 0), (1, 1, 1))
CORE_PEER = ((0, 0, 1),)


def _exchange(name, arrays, peers, n_slots, scatter=False):
    return _run_plan(name, _exchange_plan(arrays, peers, n_slots, scatter))


def _run_plan(name, plan):
    na = len(plan.arrays)

    def body(*refs):
        plan.start(refs[:na], refs[na:2 * na], refs[2 * na:])
        plan.wait(refs[:na], refs[na:2 * na], refs[2 * na:])

    any_spec = pl.BlockSpec(memory_space=pl.ANY)
    return plan.finish(pl.pallas_call(
        body, name=name, in_specs=[any_spec] * na, out_specs=[any_spec] * na, out_shape=plan.out_shape,
        scratch_shapes=plan.sems,
    )(*plan.arrays))


def _exchange_plan(arrays, peers, n_slots, scatter=False):
    na, nm = len(arrays), len(peers)

    def ident(px, py, pc):
        return {8: 4 * px + 2 * py + pc, 4: 2 * px + py, 2: pc}[n_slots]

    def copies(in_refs, out_refs, sems):
        send_sems, recv_sems = sems
        x, y, c = lax.axis_index("x"), lax.axis_index("y"), lax.axis_index("c")
        me = ident(x, y, c)
        made = []
        for i in range(na):
            for j, (fx, fy, fc) in enumerate(peers):
                px, py, pc = (1 - x if fx else x), (1 - y if fy else y), (1 - c if fc else c)
                src = in_refs[i].at[ident(px, py, pc)] if scatter else in_refs[i]
                made.append(pltpu.make_async_remote_copy(
                    src_ref=src, dst_ref=out_refs[i].at[me],
                    send_sem=send_sems.at[i * nm + j], recv_sem=recv_sems.at[i * nm + j],
                    device_id=(px, py, pc), device_id_type=pl.DeviceIdType.MESH))
        return made

    def start(in_refs, out_refs, sems):
        for copy in copies(in_refs, out_refs, sems):
            copy.start()

    def wait(in_refs, out_refs, sems):
        for copy in copies(in_refs, out_refs, sems):
            copy.wait()

    def finish(outs):
        me = ident(lax.axis_index("x"), lax.axis_index("y"), lax.axis_index("c"))
        return [lax.dynamic_update_slice_in_dim(
            o, lax.dynamic_index_in_dim(a, me, 0, keepdims=True) if scatter else a[None], me, axis=0)
            for a, o in zip(arrays, outs)]

    out_shape = [jax.ShapeDtypeStruct(((n_slots,) + a.shape[1:]) if scatter else ((n_slots,) + a.shape), a.dtype)
                 for a in arrays]
    sems = [pltpu.SemaphoreType.DMA((na * nm,)), pltpu.SemaphoreType.DMA((na * nm,))]
    return _Plan(arrays, out_shape, sems, start, wait, finish)


def _gather_halves(name, arrays):
    return _run_plan(name, _gather_halves_plan(arrays))


def _gather_halves_plan(arrays):
    na = len(arrays)
    chips = ((1, 0), (0, 1), (1, 1))

    def over_ici(in_refs, out_refs, sems):
        ici_send, ici_recv = sems[:2]
        x, y, c = lax.axis_index("x"), lax.axis_index("y"), lax.axis_index("c")
        made = []
        for i in range(na):
            half = arrays[i].shape[0] // 2
            mine = pl.ds(pl.multiple_of(c * half, 8), half)
            for j, (fx, fy) in enumerate(chips):
                px, py = (1 - x if fx else x), (1 - y if fy else y)
                k = len(chips) * i + j
                made.append([pltpu.make_async_remote_copy(
                    src_ref=in_refs[i].at[mine], dst_ref=out_refs[i].at[chip, mine],
                    send_sem=ici_send.at[k], recv_sem=ici_recv.at[k],
                    device_id=(px, py, c), device_id_type=pl.DeviceIdType.MESH)
                    for chip in (2 * x + y, 2 * px + py)])
        return made

    def start(in_refs, out_refs, sems):
        for outgoing, _ in over_ici(in_refs, out_refs, sems):
            outgoing.start()

    def wait(in_refs, out_refs, sems):
        d2d_send, d2d_recv = sems[2:]
        x, y, c = lax.axis_index("x"), lax.axis_index("y"), lax.axis_index("c")
        pending = []
        ici = over_ici(in_refs, out_refs, sems)
        for i in range(na):
            half = arrays[i].shape[0] // 2
            mine = pl.ds(pl.multiple_of(c * half, 8), half)
            theirs = pl.ds(pl.multiple_of((1 - c) * half, 8), half)
            for j, (fx, fy) in enumerate(chips):
                px, py = (1 - x if fx else x), (1 - y if fy else y)
                k = len(chips) * i + j
                outgoing, landing = ici[k]
                landing.wait_recv()
                landed = out_refs[i].at[2 * px + py, mine]
                passed = pltpu.make_async_remote_copy(
                    src_ref=landed, dst_ref=landed, send_sem=d2d_send.at[k], recv_sem=d2d_recv.at[k],
                    device_id=(x, y, 1 - c), device_id_type=pl.DeviceIdType.MESH)
                passed.start()
                from_sibling = out_refs[i].at[2 * px + py, theirs]
                pending += [outgoing.wait_send, passed.wait_send, pltpu.make_async_remote_copy(
                    src_ref=from_sibling, dst_ref=from_sibling, send_sem=d2d_send.at[k], recv_sem=d2d_recv.at[k],
                    device_id=(x, y, 1 - c), device_id_type=pl.DeviceIdType.MESH).wait_recv]
        for done in pending:
            done()

    def finish(outs):
        me = 2 * lax.axis_index("x") + lax.axis_index("y")
        return [lax.dynamic_update_slice_in_dim(o, a[None], me, axis=0) for a, o in zip(arrays, outs)]

    out_shape = [jax.ShapeDtypeStruct((N_CHIPS,) + a.shape, a.dtype) for a in arrays]
    return _Plan(arrays, out_shape, [pltpu.SemaphoreType.DMA((na * len(chips),))] * 4, start, wait, finish)


def _send_other_half(name, arrays):
    return _run_plan(name, _other_half_plan(arrays))


def _other_half_plan(arrays):
    na = len(arrays)

    def copies(in_refs, out_refs, sems):
        send_sems, recv_sems = sems
        x, y, c = lax.axis_index("x"), lax.axis_index("y"), lax.axis_index("c")
        made = []
        for i in range(na):
            half = arrays[i].shape[1] // 2
            theirs = pl.ds(pl.multiple_of((1 - c) * half, 8), half)
            made.append(pltpu.make_async_remote_copy(
                src_ref=in_refs[i].at[:, theirs], dst_ref=out_refs[i], send_sem=send_sems.at[i],
                recv_sem=recv_sems.at[i], device_id=(x, y, 1 - c), device_id_type=pl.DeviceIdType.MESH))
        return made

    def start(in_refs, out_refs, sems):
        for copy in copies(in_refs, out_refs, sems):
            copy.start()

    def wait(in_refs, out_refs, sems):
        for copy in copies(in_refs, out_refs, sems):
            copy.wait()

    out_shape = [jax.ShapeDtypeStruct((a.shape[0], a.shape[1] // 2, a.shape[2]), a.dtype) for a in arrays]
    sems = [pltpu.SemaphoreType.DMA((na,)), pltpu.SemaphoreType.DMA((na,))]
    return _Plan(arrays, out_shape, sems, start, wait, list)


def _adam_math(w, g, m, v):
    m = ADAM_B1 * m + (1.0 - ADAM_B1) * g
    v = ADAM_B2 * v + (1.0 - ADAM_B2) * jnp.square(g)
    m_hat = m / (1.0 - ADAM_B1 ** ADAM_STEP)
    v_hat = v / (1.0 - ADAM_B2 ** ADAM_STEP)
    delta = -ADAM_LR * (m_hat / (jnp.sqrt(v_hat) + ADAM_EPS) + ADAM_WD * w)
    return delta, m, v


WHOLE_BLOCK_BYTES = 2 * 1024 * 1024


def _row_tile(r, c):
    return r if 4 * r * c <= WHOLE_BLOCK_BYTES else _tile(r, (256, 128, 64, 32, 16, 8))


def _sum_parts(name, parts):
    n, r, c = parts.shape
    tr = _row_tile(r, c)

    def body(p_ref, o_ref):
        tot = p_ref[0].astype(F32)
        for i in range(1, n):
            tot = tot + p_ref[i].astype(F32)
        o_ref[...] = tot

    return pl.pallas_call(
        body, name=name, grid=(r // tr,), in_specs=[pl.BlockSpec((n, tr, c), lambda i: (0, i, 0))],
        out_specs=pl.BlockSpec((tr, c), lambda i: (i, 0)), out_shape=jax.ShapeDtypeStruct((r, c), F32),
        compiler_params=_params(("parallel",)),
    )(parts)


def _pair_sum(name, piece, other, dtype):
    n, r, c = piece.shape
    half = r // 2
    tr = _row_tile(half, c)

    def body(lo_ref, hi_ref, other_ref, o_ref):
        own = jnp.where(lax.axis_index("c") == 0, lo_ref[...], hi_ref[...])
        o_ref[...] = (own + other_ref[...]).astype(o_ref.dtype)

    blk = pl.BlockSpec((None, tr, c), lambda j, i: (j, i, 0))
    return pl.pallas_call(
        body, name=name, grid=(n, half // tr),
        in_specs=[pl.BlockSpec((None, None, tr, c), lambda j, i: (j, 0, i, 0)),
                  pl.BlockSpec((None, None, tr, c), lambda j, i: (j, 1, i, 0)), blk],
        out_specs=blk, out_shape=jax.ShapeDtypeStruct((n, half, c), dtype),
        compiler_params=_params(("parallel", "parallel")),
    )(piece.reshape(n, 2, half, c), piece.reshape(n, 2, half, c), other)


def _adamw(name, w, parts, m, v, hosted=None):
    n, r, c = parts.shape
    tr = _row_tile(r, c)
    steps = r // tr
    plan = hosted or _NO_PLAN
    nh = len(plan.arrays)

    def body(w_ref, p_ref, m_ref, v_ref, *rest):
        host_in, (g_ref, d_ref, nm_ref, nv_ref) = rest[:nh], rest[nh:nh + 4]
        host_out, sems = rest[nh + 4:2 * nh + 4], rest[2 * nh + 4:]

        @pl.when(pl.program_id(0) == 0)
        def _():
            plan.start(host_in, host_out, sems)

        g = p_ref[0]
        for i in range(1, n):
            g = g + p_ref[i]
        delta, nm, nv = _adam_math(w_ref[...], g, m_ref[...], v_ref[...])
        g_ref[...], d_ref[...], nm_ref[...], nv_ref[...] = g, delta, nm, nv

        @pl.when(pl.program_id(0) == steps - 1)
        def _():
            plan.wait(host_in, host_out, sems)

    blk = pl.BlockSpec((tr, c), lambda i: (i, 0))
    any_spec = pl.BlockSpec(memory_space=pl.ANY)
    res = pl.pallas_call(
        body, name=name, grid=(steps,),
        in_specs=[blk, pl.BlockSpec((n, tr, c), lambda i: (0, i, 0)), blk, blk] + [any_spec] * nh,
        out_specs=[blk] * 4 + [any_spec] * nh,
        out_shape=[jax.ShapeDtypeStruct((r, c), F32)] * 4 + plan.out_shape, scratch_shapes=plan.sems,
        compiler_params=_params(("arbitrary",) if nh else ("parallel",)),
    )(w, parts, m, v, *plan.arrays)
    return (res[:4], plan.finish(res[4:])) if nh else res


def _ada_w_update(act_t, dmod, w, m, v, hosted):
    r, c = w.shape
    nb = act_t.shape[1]
    tr, tc = 256, 1024
    grid = (r // tr, c // tc)
    nh = len(hosted.arrays)

    def body(a_ref, d_ref, w_ref, m_ref, v_ref, *rest):
        host_in, (g_ref, dl_ref, nm_ref, nv_ref) = rest[:nh], rest[nh:nh + 4]
        host_out, sems = rest[nh + 4:2 * nh + 4], rest[2 * nh + 4:]
        i, j = pl.program_id(0), pl.program_id(1)

        @pl.when(jnp.logical_and(i == 0, j == 0))
        def _():
            hosted.start(host_in, host_out, sems)

        a, dm = a_ref[...], d_ref[...]
        g = a[:, 0:1] * dm[0:1, :]
        for b in range(1, nb):
            g = g + a[:, b:b + 1] * dm[b:b + 1, :]
        delta, nm, nv = _adam_math(w_ref[...], g, m_ref[...], v_ref[...])
        g_ref[...], dl_ref[...], nm_ref[...], nv_ref[...] = g, delta, nm, nv

        @pl.when(jnp.logical_and(i == grid[0] - 1, j == grid[1] - 1))
        def _():
            hosted.wait(host_in, host_out, sems)

    blk = pl.BlockSpec((tr, tc), lambda i, j: (i, j))
    any_spec = pl.BlockSpec(memory_space=pl.ANY)
    res = pl.pallas_call(
        body, name="ada_w_update", grid=grid,
        in_specs=[pl.BlockSpec((tr, nb), lambda i, j: (i, 0)), pl.BlockSpec((nb, tc), lambda i, j: (0, j)),
                  blk, blk, blk] + [any_spec] * nh,
        out_specs=[blk] * 4 + [any_spec] * nh,
        out_shape=[jax.ShapeDtypeStruct((r, c), F32)] * 4 + hosted.out_shape,
        scratch_shapes=hosted.sems,
        compiler_params=_params(("arbitrary", "arbitrary")),
    )(act_t, dmod, w, m, v, *hosted.arrays)
    return res[:4], hosted.finish(res[4:])


WEIGHTS = ['ada_w', 'ada_b', 'norm1_gain', 'norm2_gain', 'final_gain', 'w_in', 'w_out', 's5_lambda_re',
           's5_lambda_im', 's5_log_step', 's5_b_re', 's5_b_im', 's5_c_re', 's5_c_im', 's5_d', 's5_w_glu',
           's5_b_glu', 'rk_shift_prev', 'rk_shift_next', 'rk_w0', 'rk_w_up', 'rk_a0', 'rk_a_up', 'rk_g_up',
           'rk_k_k', 'rk_k_a', 'rk_r_k', 'rk_ln_gain', 'rk_ln_bias', 'ffn_w1', 'ffn_w2']
BIG_SHARDED = ['w_in', 'w_out', 's5_w_glu', 'ffn_w1', 'ffn_w2']
FFN_SHARDED = ['ffn_w1', 'ffn_w2']
RK_SHARDED = ['rk_w0', 'rk_a0', 'rk_w_up', 'rk_a_up', 'rk_g_up']
REPLICATED = ['ada_b', 'norm1_gain', 'norm2_gain', 'final_gain', 's5_lambda_re', 's5_lambda_im', 's5_log_step',
              's5_b_re', 's5_b_im', 's5_c_re', 's5_c_im', 's5_d', 's5_b_glu', 'rk_shift_prev', 'rk_shift_next',
              'rk_k_k', 'rk_k_a', 'rk_r_k', 'rk_ln_gain', 'rk_ln_bias']
PACK_COLS = 1024
N_CHIPS = 4
RK_ROWS = 420
RK_ROWS_PAD = 432


def _pack_rows(arrays, cols):
    return jnp.concatenate([a.reshape(-1, cols) for a in arrays], axis=0)


def _pack_flat(arrays):
    flat = jnp.concatenate([a.reshape(-1) for a in arrays])
    rows = -(-flat.shape[0] // PACK_COLS)
    return jnp.pad(flat, (0, rows * PACK_COLS - flat.shape[0])).reshape(rows, PACK_COLS)


def _unpack_flat(packed, like):
    flat, out, pos = packed.reshape(-1), [], 0
    for a in like:
        out.append(flat[pos:pos + a.size].reshape(a.shape))
        pos += a.size
    return out


def _cols_to_chips(full, n_rows):
    return jnp.transpose(full.reshape(n_rows, N_CHIPS, -1), (1, 0, 2))


def _chips_to_cols(parts):
    return jnp.transpose(parts, (1, 0, 2)).reshape(parts.shape[1], -1)


def kernel(x, c, ada_w, ada_b, norm1_gain, norm2_gain, final_gain, w_in, w_out, s5_lambda_re, s5_lambda_im, s5_log_step, s5_b_re, s5_b_im, s5_c_re, s5_c_im, s5_d, s5_w_glu, s5_b_glu, rk_shift_prev, rk_shift_next, rk_w0, rk_w_up, rk_a0, rk_a_up, rk_g_up, rk_k_k, rk_k_a, rk_r_k, rk_ln_gain, rk_ln_bias, ffn_w1, ffn_w2, loss_target, m_ada_w, m_ada_b, m_norm1_gain, m_norm2_gain, m_final_gain, m_w_in, m_w_out, m_s5_lambda_re, m_s5_lambda_im, m_s5_log_step, m_s5_b_re, m_s5_b_im, m_s5_c_re, m_s5_c_im, m_s5_d, m_s5_w_glu, m_s5_b_glu, m_rk_shift_prev, m_rk_shift_next, m_rk_w0, m_rk_w_up, m_rk_a0, m_rk_a_up, m_rk_g_up, m_rk_k_k, m_rk_k_a, m_rk_r_k, m_rk_ln_gain, m_rk_ln_bias, m_ffn_w1, m_ffn_w2, v_ada_w, v_ada_b, v_norm1_gain, v_norm2_gain, v_final_gain, v_w_in, v_w_out, v_s5_lambda_re, v_s5_lambda_im, v_s5_log_step, v_s5_b_re, v_s5_b_im, v_s5_c_re, v_s5_c_im, v_s5_d, v_s5_w_glu, v_s5_b_glu, v_rk_shift_prev, v_rk_shift_next, v_rk_w0, v_rk_w_up, v_rk_a0, v_rk_a_up, v_rk_g_up, v_rk_k_k, v_rk_k_a, v_rk_r_k, v_rk_ln_gain, v_rk_ln_bias, v_ffn_w1, v_ffn_w2):
    given = dict(locals())
    w = {n: given[n] for n in WEIGHTS}
    m = {n: given["m_" + n] for n in WEIGHTS}
    v = {n: given["v_" + n] for n in WEIGHTS}
    mx, my, mc = lax.axis_index("x"), lax.axis_index("y"), lax.axis_index("c")
    chip = 2 * mx + my
    dev = 2 * chip + mc
    xt, target = x[0], loss_target[0]

    def rk_rows(d):
        rows = _pack_rows([d[n] for n in RK_SHARDED], 256)
        return jnp.pad(rows, ((0, RK_ROWS_PAD - rows.shape[0]), (0, 0)))

    (c_all,), (w_in_parts,) = _run_plan("gather_first", _join_plans([
        _exchange_plan([c], ALL_PEERS, 8), _gather_halves_plan([w_in[0].astype(BF16)])]))

    (act,) = _rowwise("ada_act", lambda q: (q * _sigmoid(q),), [c_all.reshape(8, D_MODEL)], [], [(D_MODEL, F32)], 8)
    n_mod_cols = N_MOD * D_MODEL // N_CHIPS
    bias = jnp.broadcast_to(lax.dynamic_slice(ada_b, (0, chip * n_mod_cols), (1, n_mod_cols)), (8, n_mod_cols))
    mod_shard = _matmul("ada_fwd", act, ada_w[0], epilogue=_add_epilogue, extras=(bias,))
    (mod_parts,) = _exchange("gather_mod", [mod_shard], CHIP_PEERS, N_CHIPS)
    mod_all = _chips_to_cols(mod_parts)
    mod_mine = lax.dynamic_slice(mod_all, (dev, 0), (1, N_MOD * D_MODEL))
    mod = [mod_mine[:, i * D_MODEL:(i + 1) * D_MODEL] for i in range(N_MOD)]

    def mixer_weights(parts):
        w_out_parts, glu_parts, rk_full = parts

        def rk_piece(lo, hi, lead):
            return _chips_to_cols(rk_full[:, lo:hi]).reshape(lead + (RK_WIDTH,))

        zeros = jnp.zeros((LORA, RK_WIDTH), F32)
        w_up, a_up = rk_piece(4, 132, (2, LORA)), rk_piece(132, 260, (2, LORA))
        return {
            "w_out": w_out_parts.reshape(D_MODEL, D_MODEL), "s5_w_glu": glu_parts.reshape(S5_WIDTH, S5_WIDTH),
            "w0": list(rk_piece(0, 2, (2,))[:, None, :]), "a0": list(rk_piece(2, 4, (2,))[:, None, :]),
            "w_up": [jnp.concatenate([w_up[0], zeros]), jnp.concatenate([zeros, w_up[1]])],
            "a_up": [jnp.concatenate([a_up[0], zeros]), jnp.concatenate([zeros, a_up[1]])],
            "g_up": jnp.pad(rk_piece(260, 420, (GATE_LORA,)), ((0, GATE_PAD - GATE_LORA), (0, 0))),
        }

    wt = {
        "norm1_gain": norm1_gain, "norm2_gain": norm2_gain, "final_gain": final_gain.reshape(1, D_MODEL),
        "w_in": jnp.pad(_chips_to_cols(w_in_parts), ((0, 0), (0, PROJ_PAD - PROJ))),
        "mu_prev": jnp.pad(rk_shift_prev, ((0, 0), (0, RK_PAD - RK_IN))),
        "mu_next": jnp.pad(rk_shift_next, ((0, 0), (0, RK_PAD - RK_IN))),
        "lam_re": [s5_lambda_re[0, d].reshape(S5_CH, 1) for d in range(2)],
        "lam_im": [s5_lambda_im[0, d].reshape(S5_CH, 1) for d in range(2)],
        "log_step": [jnp.repeat(s5_log_step[0, d], S5_STATE).reshape(S5_CH, 1) for d in range(2)],
        "b_re": s5_b_re.reshape(S5_CH, S5_GROUP), "b_im": s5_b_im.reshape(S5_CH, S5_GROUP),
        "c_re": s5_c_re[0], "c_im": s5_c_im[0],
        "s5_d": s5_d, "s5_b_glu": s5_b_glu,
        "k_k": rk_k_k, "k_a": rk_k_a, "r_k": rk_r_k.reshape(1, RK_WIDTH),
        "ln_gain": rk_ln_gain, "ln_bias": rk_ln_bias,
    }

    ffn_shards = [w[n][0].astype(BF16) for n in FFN_SHARDED]
    mixer_shards = [w_out[0].astype(BF16), s5_w_glu[0].astype(BF16), rk_rows(w)]
    loss_rows, grad_x, g = _local_step(xt, target, mod, wt, ffn_shards=ffn_shards,
                                       mixer_shards=(mixer_shards, mixer_weights))
    loss = lax.psum(jnp.sum(loss_rows), ("x", "y", "c"))


    big_grads = {
        "w_in": _cols_to_chips(g["w_in"][:, :PROJ], D_MODEL),
        "w_out": g["w_out"].reshape(N_CHIPS, -1, D_MODEL),
        "s5_w_glu": g["s5_w_glu"].reshape(N_CHIPS, -1, S5_WIDTH),
    }
    rk_grads = jnp.concatenate([
        _cols_to_chips(jnp.concatenate(g["w0"]), 2), _cols_to_chips(jnp.concatenate(g["a0"]), 2),
        _cols_to_chips(jnp.concatenate([g["w_up"][0][:LORA], g["w_up"][1][LORA:]]), 2 * LORA),
        _cols_to_chips(jnp.concatenate([g["a_up"][0][:LORA], g["a_up"][1][LORA:]]), 2 * LORA),
        _cols_to_chips(g["g_up"][:GATE_LORA], GATE_LORA),
        jnp.zeros((N_CHIPS, RK_ROWS_PAD - RK_ROWS, 256), F32)], axis=1)
    local_small = {
        "ada_b": jnp.concatenate(g["mod"], axis=1),
        "norm1_gain": g["norm1_gain"], "norm2_gain": g["norm2_gain"], "final_gain": g["final_gain"],
        "s5_lambda_re": jnp.concatenate(g["lam_re"]), "s5_lambda_im": jnp.concatenate(g["lam_im"]),
        "s5_log_step": jnp.concatenate([q.reshape(S5_GROUPS, S5_STATE).sum(axis=1) for q in g["log_step"]]),
        "s5_b_re": g["b_re"], "s5_b_im": g["b_im"], "s5_c_re": g["c_re"], "s5_c_im": g["c_im"],
        "s5_d": g["s5_d"], "s5_b_glu": g["s5_b_glu"],
        "rk_shift_prev": g["mu_prev"][:, :RK_IN], "rk_shift_next": g["mu_next"][:, :RK_IN],
        "rk_k_k": g["k_k"], "rk_k_a": g["k_a"], "rk_r_k": g["r_k"],
        "rk_ln_gain": g["ln_gain"], "rk_ln_bias": g["ln_bias"],
    }
    late = [n for n in BIG_SHARDED if n not in FFN_SHARDED]
    late_pieces = [big_grads[n] for n in late] + [rk_grads]
    late_names = late + ["rk"]

    def whole(halves):
        return halves.reshape(1, 2 * halves.shape[1], halves.shape[2])

    ffn_halves = [_sum_parts("sum_" + n, a) for n, a in zip(FFN_SHARDED, g["ffn_arrived"])]
    from_sibling, ffn_pairs, (small_all,) = _run_plan("swap_late", _join_plans([
        _other_half_plan(late_pieces), _exchange_plan(ffn_halves, CORE_PEER, 2),
        _exchange_plan([_pack_flat([local_small[n] for n in REPLICATED])], ALL_PEERS, 8)]))
    late_sums = [_pair_sum("pair_" + n, piece, other, F32 if n == "rk" else BF16)
                 for n, piece, other in zip(late_names, late_pieces, from_sibling)]
    pairs = dict(zip(FFN_SHARDED, [whole(p) for p in ffn_pairs]))

    mod_rows = N_MOD * D_MODEL // PACK_COLS
    dmod_all = small_all[:, :mod_rows].reshape(8, N_MOD * D_MODEL)
    dmod = lax.dynamic_slice(dmod_all, (0, chip * n_mod_cols), (8, n_mod_cols))
    ada_res, arrived = _ada_w_update(act.T, dmod, ada_w[0], m_ada_w[0], v_ada_w[0],
                                     hosted=_exchange_plan(late_sums, CHIP_PEERS, N_CHIPS, scatter=True))
    late_halves = [_sum_parts("sum_" + n, a) for n, a in zip(late_names, arrived)]

    out = {"ada_w": [r[None] for r in ada_res]}
    first = FFN_SHARDED[0]
    res, swapped = _adamw("adamw_" + first, w[first][0], pairs[first], m[first][0], v[first][0],
                          hosted=_exchange_plan(late_halves, CORE_PEER, 2))
    out[first] = [r[None] for r in res]
    pairs.update(zip(late_names, [whole(p) for p in swapped]))
    for n in [FFN_SHARDED[1]] + late:
        out[n] = [r[None] for r in _adamw("adamw_" + n, w[n][0], pairs[n], m[n][0], v[n][0])]
    rk_res = _adamw("adamw_rk", rk_rows(w), pairs["rk"], rk_rows(m), rk_rows(v))
    for q in range(4):
        pieces, pos = [], 0
        for n in RK_SHARDED:
            rows = w[n].size // 256
            pieces.append(rk_res[q][pos:pos + rows].reshape(w[n].shape))
            pos += rows
        for n, piece in zip(RK_SHARDED, pieces):
            out.setdefault(n, []).append(piece)

    small_res = _adamw("adamw_small", _pack_flat([w[n] for n in REPLICATED]), small_all,
                       _pack_flat([m[n] for n in REPLICATED]), _pack_flat([v[n] for n in REPLICATED]))
    for q in range(4):
        for n, piece in zip(REPLICATED, _unpack_flat(small_res[q], [w[n] for n in REPLICATED])):
            out.setdefault(n, []).append(piece)

    return (loss, grad_x[None], *[out[n][0] for n in WEIGHTS], *[out[n][1] for n in WEIGHTS],
            *[out[n][2] for n in WEIGHTS], *[out[n][3] for n in WEIGHTS])
```

```python
import functools
import math

import jax
import jax.numpy as jnp
from jax import lax
from jax.experimental import pallas as pl
from jax.experimental.pallas import tpu as pltpu

F32 = jnp.float32
BF16 = jnp.bfloat16

D_MODEL = 2048
S5_WIDTH = 1024
S5_GROUP = 16
S5_GROUPS = 64
S5_STATE = 64
S5_CH = S5_GROUPS * S5_STATE
S5_BLK = 256
RK_WIDTH = 1024
RK_HEAD = 64
RK_HEADS = 16
LORA = 64
GATE_LORA = 160
GATE_PAD = 256
RK_IN = 3488
RK_PAD = 3584
PROJ = 4512
PROJ_PAD = 4608
FFN = 8192
N_MOD = 6
NORM_EPS = 1e-6
GN_EPS = 64e-5
L2_EPS = 1e-12
RK_CHUNK = 64
RK_PASSES = {"solve": 3, "kt": 3, "s0": 1, "akk_v": 1, "ark_v": 1, "arb_u": 1, "state": 3}
LW_SCALE = math.exp(-0.5)
ADAM_LR, ADAM_B1, ADAM_B2, ADAM_EPS, ADAM_WD, ADAM_STEP = 0.001, 0.9, 0.999, 1e-08, 0.01, 10
VMEM_LIMIT = 56 * 1024 * 1024
HI = lax.Precision.HIGHEST


def _params(sem=None):
    return pltpu.CompilerParams(dimension_semantics=sem, vmem_limit_bytes=VMEM_LIMIT)


def _full(a):
    nd = a.ndim
    return pl.BlockSpec(a.shape, lambda *_: (0,) * nd)


@jax.custom_vjp
def _bdot(a, b):
    return jnp.dot(a.astype(BF16), b.astype(BF16), preferred_element_type=F32)


def _bdot_fwd(a, b):
    return _bdot(a, b), (a, b)


def _bdot_bwd(res, g):
    a, b = res
    gb = g.astype(BF16)
    da = lax.dot_general(gb, b.astype(BF16), (((1,), (1,)), ((), ())), preferred_element_type=F32)
    db = lax.dot_general(a.astype(BF16), gb, (((0,), (0,)), ((), ())), preferred_element_type=F32)
    return da, db


_bdot.defvjp(_bdot_fwd, _bdot_bwd)


@jax.custom_vjp
def _seg_dot(x, ind, ind_t):
    hi = x.astype(BF16)
    lo = (x - hi.astype(F32)).astype(BF16)
    both = jnp.dot(jnp.concatenate([hi, lo], axis=0), ind.astype(BF16), preferred_element_type=F32)
    return both[:x.shape[0]] + both[x.shape[0]:]


_seg_dot.defvjp(lambda x, ind, ind_t: (_seg_dot(x, ind, ind_t), (ind, ind_t)),
                lambda res, g: (_seg_dot(g, res[1], res[0]), jnp.zeros_like(res[0]), jnp.zeros_like(res[1])))


def _sigmoid(z):
    return 1.0 / (1.0 + jnp.exp(-z))


def _gelu(y):
    return 0.5 * y * (1.0 + jnp.tanh(0.7978845608028654 * (y + 0.044715 * (y * y * y))))


def _rms(x):
    return x * lax.rsqrt(jnp.mean(x * x, axis=-1, keepdims=True) + NORM_EPS)


def _tile(n, prefs):
    for t in prefs:
        if n % t == 0:
            return t
    return n


def _matmul(name, a, b, ta=False, tb=False, epilogue=None, extras=(), out_dtypes=(F32,), chips=None, hosted=None):
    m = a.shape[1] if ta else a.shape[0]
    k = a.shape[0] if ta else a.shape[1]
    if chips == "b":
        assert not tb and b.shape[1] == k
        n = N_CHIPS * b.shape[2]
    elif chips == "b_t":
        assert tb and N_CHIPS * b.shape[2] == k
        n = b.shape[1]
    else:
        n = b.shape[0] if tb else b.shape[1]
        assert k == (b.shape[1] if tb else b.shape[0]), (a.shape, b.shape, ta, tb)
    split = N_CHIPS if chips in ("b", "out") else 1
    tm = _tile(m, (1024, 512, 256, 128))
    tn = _tile(n // split, (1024, 768, 512, 256, 128))
    tk = k // N_CHIPS if chips == "b_t" else _tile(k, (2048, 1024, 512, 256, 128))
    nk = k // tk
    per = n // split // tn
    n_ex, n_out = len(extras), len(out_dtypes)
    dims = (((0 if ta else 1,), (1 if tb else 0,)), ((), ()))

    hosted = hosted or _NO_PLAN
    nh = len(hosted.arrays)
    grid = (m // tm, split, per, nk)

    def body(a_ref, b_ref, *rest):
        ex_refs, host_in = rest[:n_ex], rest[n_ex:n_ex + nh]
        out_refs, host_out = rest[n_ex + nh:n_ex + nh + n_out], rest[n_ex + nh + n_out:n_ex + 2 * nh + n_out]
        acc, sems = rest[n_ex + 2 * nh + n_out], rest[n_ex + 2 * nh + n_out + 1:]
        kk = pl.program_id(3)
        if nh:
            ids = [pl.program_id(d) for d in range(4)]
            first = functools.reduce(jnp.logical_and, [i == 0 for i in ids])
            last = functools.reduce(jnp.logical_and, [i == g - 1 for i, g in zip(ids, grid)])

            @pl.when(first)
            def _():
                hosted.start(host_in, host_out, sems)

        @pl.when(kk == 0)
        def _():
            acc[...] = jnp.zeros_like(acc)

        acc[...] += lax.dot_general(a_ref[...].astype(BF16), b_ref[...].astype(BF16), dims,
                                    preferred_element_type=F32)

        @pl.when(kk == nk - 1)
        def _():
            res = acc[...]
            outs = epilogue(res, *[e[...] for e in ex_refs]) if epilogue is not None else (res,)
            for o_ref, val in zip(out_refs, outs):
                o_ref[...] = val.astype(o_ref.dtype)

        if nh:
            @pl.when(last)
            def _():
                hosted.wait(host_in, host_out, sems)

    if ta:
        a_spec = pl.BlockSpec((tk, tm), lambda i, c, j, q: (q, i))
    else:
        a_spec = pl.BlockSpec((tm, tk), lambda i, c, j, q: (i, q))
    if chips == "b":
        b_spec = pl.BlockSpec((None, tk, tn), lambda i, c, j, q: (c, q, j))
    elif chips == "b_t":
        b_spec = pl.BlockSpec((None, tn, tk), lambda i, c, j, q: (q, j, 0))
    elif tb:
        b_spec = pl.BlockSpec((tn, tk), lambda i, c, j, q: (c * per + j, q))
    else:
        b_spec = pl.BlockSpec((tk, tn), lambda i, c, j, q: (q, c * per + j))
    mn_spec = pl.BlockSpec((tm, tn), lambda i, c, j, q: (i, c * per + j))
    if chips == "out":
        out_spec = pl.BlockSpec((None, tm, tn), lambda i, c, j, q: (c, i, j))
        out_shape = [jax.ShapeDtypeStruct((N_CHIPS, m, n // N_CHIPS), dt) for dt in out_dtypes]
    else:
        out_spec, out_shape = mn_spec, [jax.ShapeDtypeStruct((m, n), dt) for dt in out_dtypes]
    any_spec = pl.BlockSpec(memory_space=pl.ANY)
    order = ("arbitrary",) * 4 if nh else ("parallel", "parallel", "parallel", "arbitrary")
    outs = pl.pallas_call(
        body, name=name, grid=grid,
        in_specs=[a_spec, b_spec] + [mn_spec] * n_ex + [any_spec] * nh,
        out_specs=[out_spec] * n_out + [any_spec] * nh, out_shape=out_shape + hosted.out_shape,
        scratch_shapes=[pltpu.VMEM((tm, tn), F32)] + hosted.sems,
        compiler_params=_params(order),
    )(a, b, *extras, *hosted.arrays)
    res = outs[0] if n_out == 1 else outs[:n_out]
    return (res, hosted.finish(outs[n_out:])) if nh else res


def _row_spec(a, tm):
    return pl.BlockSpec((tm, a.shape[1]), lambda i: (i, 0))


def _rowwise(name, fn, rows, params, outs, tm):
    t = rows[0].shape[0]
    tm = min(tm, t)
    n_r, n_p = len(rows), len(params)

    def body(*refs):
        vals = [r[...] for r in refs[:n_r + n_p]]
        res = fn(*vals)
        for o_ref, val in zip(refs[n_r + n_p:], res):
            o_ref[...] = val.astype(o_ref.dtype)

    res = pl.pallas_call(
        body, name=name, grid=(t // tm,),
        in_specs=[_row_spec(r, tm) for r in rows] + [_full(p) for p in params],
        out_specs=[pl.BlockSpec((tm, n), lambda i: (i, 0)) for n, _ in outs],
        out_shape=[jax.ShapeDtypeStruct((t, n), dt) for n, dt in outs],
        compiler_params=_params(("parallel",)),
    )(*rows, *params)
    return res


def _rowwise_vjp(name, fn, rows, params, cts, row_grads, param_grads, tm, consts=(), addends=None,
                 emit=(), row_grad_dtypes=None):
    t = rows[0].shape[0]
    tm = min(tm, t)
    addends = addends or {}
    n_r, n_p, n_c = len(rows), len(params), len(consts)
    ct_flat = [c for group in cts for c in group]
    add_list = [addends[q] for q in sorted(addends)]
    n_ct, n_add = len(ct_flat), len(add_list)
    row_grad_dtypes = row_grad_dtypes or [F32] * len(row_grads)

    def body(*refs):
        pos = 0
        row_v = [r[...].astype(F32) for r in refs[pos:pos + n_r]]; pos += n_r
        par_v = [r[...].astype(F32) for r in refs[pos:pos + n_p]]; pos += n_p
        con_v = [r[...] for r in refs[pos:pos + n_c]]; pos += n_c
        ct_v = [r[...].astype(F32) for r in refs[pos:pos + n_ct]]; pos += n_ct
        add_v = [r[...] for r in refs[pos:pos + n_add]]; pos += n_add
        emit_refs = refs[pos:pos + len(emit)]; pos += len(emit)
        rg_refs = refs[pos:pos + len(row_grads)]; pos += len(row_grads)
        pg_refs = refs[pos:pos + len(param_grads)]

        def diff_fn(*dargs):
            rv, pv = list(row_v), list(par_v)
            for q, i in enumerate(row_grads):
                rv[i] = dargs[q]
            for q, j in enumerate(param_grads):
                pv[j] = dargs[len(row_grads) + q]
            return fn(*rv, *pv, *con_v)

        prim = [row_v[i] for i in row_grads] + [par_v[j] for j in param_grads]
        res, vjp = jax.vjp(diff_fn, *prim)
        ct_vals, q = [], 0
        for o, group in zip(res, cts):
            tot = jnp.zeros_like(o)
            for _ in group:
                tot = tot + ct_v[q]
                q += 1
            ct_vals.append(tot)
        grads = vjp(tuple(ct_vals))
        for e_ref, idx in zip(emit_refs, emit):
            e_ref[...] = res[idx].astype(e_ref.dtype)
        add_pos = {p: q for q, p in enumerate(sorted(addends))}
        for q, g_ref in enumerate(rg_refs):
            g = grads[q]
            if q in add_pos:
                g = g + add_v[add_pos[q]]
            g_ref[...] = g.astype(g_ref.dtype)

        @pl.when(pl.program_id(0) == 0)
        def _():
            for g_ref in pg_refs:
                g_ref[...] = jnp.zeros_like(g_ref)

        for q, g_ref in enumerate(pg_refs):
            g_ref[...] += grads[len(row_grads) + q]

    emit_shapes = []
    if emit:
        probe = jax.eval_shape(lambda *a: fn(*a), *[jax.ShapeDtypeStruct((tm, r.shape[1]), F32) for r in rows],
                               *[jax.ShapeDtypeStruct(p.shape, p.dtype) for p in params],
                               *[jax.ShapeDtypeStruct(c.shape, c.dtype) for c in consts])
        emit_shapes = [probe[idx].shape[1] for idx in emit]
    out_specs = ([pl.BlockSpec((tm, n), lambda i: (i, 0)) for n in emit_shapes]
                 + [_row_spec(rows[i], tm) for i in row_grads]
                 + [_full(params[j]) for j in param_grads])
    out_shape = ([jax.ShapeDtypeStruct((t, n), F32) for n in emit_shapes]
                 + [jax.ShapeDtypeStruct(rows[i].shape, dt) for i, dt in zip(row_grads, row_grad_dtypes)]
                 + [jax.ShapeDtypeStruct(params[j].shape, F32) for j in param_grads])
    return pl.pallas_call(
        body, name=name, grid=(t // tm,),
        in_specs=([_row_spec(r, tm) for r in rows] + [_full(p) for p in params] + [_full(c) for c in consts]
                  + [_row_spec(c, tm) for c in ct_flat] + [_row_spec(a, tm) for a in add_list]),
        out_specs=out_specs, out_shape=out_shape,
        compiler_params=_params(("arbitrary",)),
    )(*rows, *params, *consts, *ct_flat, *add_list)


def _norm_mod_fn(x, gain, scale, shift):
    return (_rms(x) * gain * (1.0 + scale) + shift,)


def _resid_norm_mod_fn(x, mixed, gate, gain, scale, shift):
    x1 = x + gate * mixed
    return x1, _rms(x1) * gain * (1.0 + scale) + shift


def _loss_fn(x1, ffn, target, gate, gain):
    y = _rms(x1 + gate * ffn) * gain
    err = y - target
    return (0.5 * jnp.mean(err * err, axis=-1, keepdims=True),)


def _s5_out_fn(ylin, u, d_skip, w_glu, b_glu):
    z = _gelu(ylin + d_skip * u)
    return (z * _sigmoid(_bdot(z, w_glu) + b_glu),)


def _rk_pre_fn(k, wdn, adn, gdn, w0_0, w0_1, wup_0, wup_1, a0_0, a0_1, aup_0, aup_1, g_up, k_k, k_a, seg, seg_t):
    kkr = k * k_k
    inv = 1.0 / jnp.sqrt(jnp.maximum(_seg_dot(kkr * kkr, seg, seg_t), L2_EPS * L2_EPS))
    kk = kkr * _seg_dot(inv, seg_t, seg)
    tw = jnp.tanh(wdn)
    lws, kds, acts = [], [], []
    for w0, wup, a0, aup in ((w0_0, wup_0, a0_0, aup_0), (w0_1, wup_1, a0_1, aup_1)):
        lws.append(-LW_SCALE * _sigmoid(w0 + _bdot(tw, wup)))
        act = _sigmoid(a0 + _bdot(adn, aup))
        acts.append(act)
        kds.append(k * (1.0 + (act - 1.0) * k_a))
    gate = _bdot(_sigmoid(gdn), g_up)
    return (kk, lws[0], lws[1], kds[0], kds[1], acts[0], acts[1], gate)


def _rk_post_fn(y0, y1, r, v, kd0, kd1, gate, ln_gain, ln_bias, r_k, seg, seg_t):
    y = y0 + y1
    mu = _seg_dot(_seg_dot(y, seg, seg_t) * (1.0 / RK_HEAD), seg_t, seg)
    yc = y - mu
    var = _seg_dot(yc * yc, seg, seg_t) * (1.0 / RK_HEAD)
    yn = yc * _seg_dot(lax.rsqrt(var + GN_EPS), seg_t, seg) * ln_gain + ln_bias
    bonus = _seg_dot(_seg_dot(r * (kd0 + kd1) * r_k, seg, seg_t), seg_t, seg)
    return ((yn + bonus * v) * gate,)


def _s5_prep_fn(lr0, li0, ls0, lr1, li1, ls1, b_re, b_im):
    outs = []
    for lam_re, lam_im, ls in ((lr0, li0, ls0), (lr1, li1, ls1)):
        step = jnp.exp(ls)
        mag = jnp.exp(lam_re * step)
        lbar_re = mag * jnp.cos(lam_im * step)
        lbar_im = mag * jnp.sin(lam_im * step)
        den = lam_re * lam_re + lam_im * lam_im
        nr = lbar_re - 1.0
        coef_re = (nr * lam_re + lbar_im * lam_im) / den
        coef_im = (lbar_im * lam_re - nr * lam_im) / den
        outs += [lbar_re, lbar_im, coef_re * b_re - coef_im * b_im, coef_re * b_im + coef_im * b_re]
    return tuple(outs)


def _shift_rows(x, down):
    t = x.shape[0]
    rows = lax.broadcasted_iota(jnp.int32, x.shape, 0)
    if down:
        return jnp.where(rows >= 1, pltpu.roll(x, 1, 0), 0.0)
    return jnp.where(rows < t - 1, pltpu.roll(x, t - 1, 0), 0.0)


def _token_shift(src, mu_prev, mu_next, first):
    t, n = src.shape[0], mu_prev.shape[1]

    def body(p_ref, mp_ref, mn_ref, o_ref):
        x = p_ref[...]
        o_ref[...] = x + mp_ref[...] * (_shift_rows(x, True) - x) + mn_ref[...] * (_shift_rows(x, False) - x)

    col = pl.BlockSpec((t, 128), lambda j: (0, j))
    par = pl.BlockSpec((1, 128), lambda j: (0, j))
    return pl.pallas_call(
        body, name="token_shift", grid=(n // 128,),
        in_specs=[pl.BlockSpec((t, 128), lambda j: (0, j + first)), par, par], out_specs=col,
        out_shape=jax.ShapeDtypeStruct((t, n), F32), compiler_params=_params(("parallel",)),
    )(src, mu_prev, mu_next)


def _token_shift_bwd(src, mu_prev, mu_next, dps, first):
    t, n = dps.shape

    def body(p_ref, mp_ref, mn_ref, d_ref, dp_ref, dmp_ref, dmn_ref):
        x, d, mp, mn = p_ref[...], d_ref[...], mp_ref[...], mn_ref[...]
        dp_ref[...] = d * (1.0 - mp - mn) + _shift_rows(d * mp, False) + _shift_rows(d * mn, True)
        dmp_ref[...] = jnp.sum(d * (_shift_rows(x, True) - x), axis=0, keepdims=True)
        dmn_ref[...] = jnp.sum(d * (_shift_rows(x, False) - x), axis=0, keepdims=True)

    col = pl.BlockSpec((t, 128), lambda j: (0, j))
    par = pl.BlockSpec((1, 128), lambda j: (0, j))
    return pl.pallas_call(
        body, name="token_shift_bwd", grid=(n // 128,),
        in_specs=[pl.BlockSpec((t, 128), lambda j: (0, j + first)), par, par, col],
        out_specs=[col, par, par],
        out_shape=[jax.ShapeDtypeStruct((t, n), F32), jax.ShapeDtypeStruct((1, n), F32),
                   jax.ShapeDtypeStruct((1, n), F32)],
        compiler_params=_params(("parallel",)),
    )(src, mu_prev, mu_next, dps)


N_SEG = 32
S5_BLOCKS = 32
S5_PER_IN = 4


def _scan_in_place(sr_ref, si_ref, ar, ai, carry_ref, reverse):
    seg_len = sr_ref.shape[0] // N_SEG
    ng = N_SEG // 8

    def rows(i, grp):
        first = (seg_len - 1 - i if reverse else i) * N_SEG + 8 * grp
        return pl.ds(pl.multiple_of(first, 8), 8)

    zero = jnp.zeros((8, 128), F32)
    one = jnp.ones((8, 128), F32)

    def local(i, c):
        pr, pi = c[-2:]
        out = []
        for grp in range(ng):
            sr, si = c[2 * grp], c[2 * grp + 1]
            nr = ar * sr - ai * si + sr_ref[rows(i, grp), :]
            ni = ar * si + ai * sr + si_ref[rows(i, grp), :]
            sr_ref[rows(i, grp), :] = nr
            si_ref[rows(i, grp), :] = ni
            out += [nr, ni]
        return tuple(out) + (ar * pr - ai * pi, ar * pi + ai * pr)

    ends = lax.fori_loop(0, seg_len, local, (zero,) * (2 * ng) + (one, zero))
    qr, qi = ends[-2][0:1], ends[-1][0:1]
    order = list(range(N_SEG - 1, -1, -1)) if reverse else list(range(N_SEG))
    cr = jnp.zeros((1, 128), F32)
    ci = jnp.zeros((1, 128), F32)
    for j in order:
        carry_ref[j:j + 1, :] = cr
        carry_ref[N_SEG + j:N_SEG + j + 1, :] = ci
        grp, sub = divmod(j, 8)
        lr, li = ends[2 * grp][sub:sub + 1], ends[2 * grp + 1][sub:sub + 1]
        cr, ci = lr + qr * cr - qi * ci, li + qr * ci + qi * cr
    carries = [(carry_ref[8 * grp:8 * grp + 8, :], carry_ref[N_SEG + 8 * grp:N_SEG + 8 * grp + 8, :])
               for grp in range(ng)]

    def fix(i, c):
        pr, pi = c
        npr, npi = ar * pr - ai * pi, ar * pi + ai * pr
        for grp in range(ng):
            cr8, ci8 = carries[grp]
            sr_ref[rows(i, grp), :] = sr_ref[rows(i, grp), :] + npr * cr8 - npi * ci8
            si_ref[rows(i, grp), :] = si_ref[rows(i, grp), :] + npr * ci8 + npi * cr8
        return npr, npi

    lax.fori_loop(0, seg_len, fix, (one, zero))


def _interleave(x):
    t, c = x.shape
    return jnp.transpose(x.reshape(N_SEG, t // N_SEG, c), (1, 0, 2)).reshape(t, c)


def _deinterleave(x):
    t, c = x.shape
    return jnp.transpose(x.reshape(t // N_SEG, N_SEG, c), (1, 0, 2)).reshape(t, c)


def _lag_sums(lr_ref, li_ref, sr_ref, si_ref, earlier):
    t = lr_ref.shape[0]
    body, edge = pl.ds(N_SEG, t - N_SEG), pl.ds(0, N_SEG)
    far = pl.ds(t - N_SEG, N_SEG)
    rows = lax.broadcasted_iota(jnp.int32, (N_SEG, 128), 0)
    if earlier:
        lam_main, s_main, lam_edge = body, pl.ds(0, t - N_SEG), edge
        wrap = lambda ref: jnp.where(rows >= 1, pltpu.roll(ref[far, :], 1, 0), 0.0)
    else:
        lam_main, s_main, lam_edge = pl.ds(0, t - N_SEG), body, far
        wrap = lambda ref: jnp.where(rows < N_SEG - 1, pltpu.roll(ref[edge, :], N_SEG - 1, 0), 0.0)
    lr, li, sr, si = lr_ref[lam_main, :], li_ref[lam_main, :], sr_ref[s_main, :], si_ref[s_main, :]
    er, ei, pr, pi = lr_ref[lam_edge, :], li_ref[lam_edge, :], wrap(sr_ref), wrap(si_ref)
    re = jnp.sum(lr * sr + li * si, axis=0, keepdims=True) + jnp.sum(er * pr + ei * pi, axis=0, keepdims=True)
    im = jnp.sum(li * sr - lr * si, axis=0, keepdims=True) + jnp.sum(ei * pr - er * pi, axis=0, keepdims=True)
    return re, im


def _dot_bf16(a, b, dims=(((1,), (0,)), ((), ()))):
    return lax.dot_general(a.astype(BF16), b.astype(BF16), dims, preferred_element_type=F32)


NT_DIMS = (((1,), (1,)), ((), ()))
TN_DIMS = (((0,), (0,)), ((), ()))


def _s5_specs(t):
    blk = pl.BlockSpec((None, t, 128), lambda i, q: (S5_PER_IN * i + q, 0, 0))
    mat = pl.BlockSpec((None, 128, 128), lambda i, q: (S5_PER_IN * i + q, 0, 0))
    vec = pl.BlockSpec((None, 1, 128), lambda i, q: (S5_PER_IN * i + q, 0, 0))
    chan = pl.BlockSpec((t, 128), lambda i, q: (0, i))
    return blk, mat, vec, chan


S5_GRID = (S5_BLOCKS // S5_PER_IN, S5_PER_IN)


def _s5_forward(name, u, b_re, b_im, l_re, l_im, reverse, other=None, c_re=None, c_im_neg=None):
    t = u.shape[0]
    project = other is not None
    blk, mat, vec, chan = _s5_specs(t)

    def body(*refs):
        u_ref, br_ref, bi_ref, lr_ref, li_ref = refs[:5]
        if project:
            or_ref, oi_ref, cr_ref, ci_ref, sr_ref, si_ref, y_ref, carry_ref = refs[5:]
        else:
            sr_ref, si_ref, carry_ref = refs[5:]
        uv = u_ref[...]
        sr_ref[...] = _dot_bf16(uv, br_ref[...])
        si_ref[...] = _dot_bf16(uv, bi_ref[...])
        ar = jnp.broadcast_to(lr_ref[...], (8, 128))
        ai = jnp.broadcast_to(li_ref[...], (8, 128))
        _scan_in_place(sr_ref, si_ref, ar, ai, carry_ref, reverse)
        if project:
            y = (_dot_bf16(sr_ref[...] + or_ref[...], cr_ref[...])
                 + _dot_bf16(si_ref[...] + oi_ref[...], ci_ref[...]))

            @pl.when(pl.program_id(1) == 0)
            def _():
                y_ref[...] = y

            @pl.when(pl.program_id(1) != 0)
            def _():
                y_ref[...] += y

    state = jax.ShapeDtypeStruct((S5_BLOCKS, t, 128), F32)
    ins = [u, b_re, b_im, l_re, l_im] + ([other[0], other[1], c_re, c_im_neg] if project else [])
    in_specs = [chan, mat, mat, vec, vec] + ([blk, blk, mat, mat] if project else [])
    return pl.pallas_call(
        body, name=name, grid=S5_GRID, in_specs=in_specs,
        out_specs=[blk, blk] + ([chan] if project else []),
        out_shape=[state, state] + ([jax.ShapeDtypeStruct((t, S5_WIDTH), F32)] if project else []),
        scratch_shapes=[pltpu.VMEM((2 * N_SEG, 128), F32)],
        compiler_params=_params(("arbitrary", "arbitrary")),
    )(*ins)


def _s5_backward(name, dy, u, du_in, states, other, b_re, b_im, c_re, c_im_neg, l_re, l_im, reverse):
    t = u.shape[0]
    with_c = other is not None
    blk, mat, vec, chan = _s5_specs(t)

    def body(*refs):
        dy_ref, u_ref, du_in_ref, sr_ref, si_ref = refs[:5]
        pos = 5
        if with_c:
            or_ref, oi_ref = refs[5:7]
            pos = 7
        br_ref, bi_ref, cr_ref, ci_ref, lr_ref, li_ref = refs[pos:pos + 6]
        outs = refs[pos + 6:]
        du_ref, dbr_ref, dbi_ref, dlr_ref, dli_ref = outs[:5]
        lam_r, lam_i, carry_ref = outs[-3:]
        dyv, uv = dy_ref[...], u_ref[...]
        lam_r[...] = _dot_bf16(dyv, cr_ref[...], NT_DIMS)
        lam_i[...] = _dot_bf16(dyv, ci_ref[...], NT_DIMS)
        ar = jnp.broadcast_to(lr_ref[...], (8, 128))
        ai = -jnp.broadcast_to(li_ref[...], (8, 128))
        _scan_in_place(lam_r, lam_i, ar, ai, carry_ref, not reverse)
        lr, li = lam_r[...], lam_i[...]
        dlr_ref[...], dli_ref[...] = _lag_sums(lam_r, lam_i, sr_ref, si_ref, not reverse)
        dbr_ref[...] = _dot_bf16(uv, lr, TN_DIMS)
        dbi_ref[...] = _dot_bf16(uv, li, TN_DIMS)
        du = _dot_bf16(lr, br_ref[...], NT_DIMS) + _dot_bf16(li, bi_ref[...], NT_DIMS)

        @pl.when(pl.program_id(1) == 0)
        def _():
            du_ref[...] = du_in_ref[...] + du

        @pl.when(pl.program_id(1) != 0)
        def _():
            du_ref[...] += du

        if with_c:
            dcr_ref, dci_ref = outs[5:7]
            dcr_ref[...] = _dot_bf16(sr_ref[...] + or_ref[...], dyv, TN_DIMS)
            dci_ref[...] = _dot_bf16(si_ref[...] + oi_ref[...], dyv, TN_DIMS)

    mats = jax.ShapeDtypeStruct((S5_BLOCKS, 128, 128), F32)
    vecs = jax.ShapeDtypeStruct((S5_BLOCKS, 1, 128), F32)
    ins = [dy, u, du_in, states[0], states[1]] + ([other[0], other[1]] if with_c else [])
    ins += [b_re, b_im, c_re, c_im_neg, l_re, l_im]
    in_specs = [chan, chan, chan, blk, blk] + ([blk, blk] if with_c else []) + [mat] * 4 + [vec] * 2
    return pl.pallas_call(
        body, name=name, grid=S5_GRID, in_specs=in_specs,
        out_specs=[chan, mat, mat, vec, vec] + ([mat, mat] if with_c else []),
        out_shape=[jax.ShapeDtypeStruct((t, S5_WIDTH), F32), mats, mats, vecs, vecs] + ([mats, mats] if with_c else []),
        scratch_shapes=[pltpu.VMEM((t, 128), F32), pltpu.VMEM((t, 128), F32), pltpu.VMEM((2 * N_SEG, 128), F32)],
        compiler_params=_params(("arbitrary", "arbitrary")),
    )(*ins)


def _ein(passes, spec, a, b):
    if passes == 6:
        return jnp.einsum(spec, a, b, precision=HI, preferred_element_type=F32)
    a_hi, b_hi = a.astype(BF16), b.astype(BF16)
    if passes == 1:
        return jnp.einsum(spec, a_hi, b_hi, preferred_element_type=F32)
    a_lo = (a - a_hi.astype(F32)).astype(BF16)
    b_lo = (b - b_hi.astype(F32)).astype(BF16)
    cross = jnp.einsum(spec, a_hi, b_lo, preferred_element_type=F32)
    if spec.startswith('hik'):
        m = a.shape[1]
        stacked = jnp.einsum(spec, jnp.concatenate([a_hi, a_lo], axis=1), b_hi, preferred_element_type=F32)
        return stacked[:, :m] + stacked[:, m:] + cross
    return (jnp.einsum(spec, a_hi, b_hi, preferred_element_type=F32) + cross
            + jnp.einsum(spec, a_lo, b_hi, preferred_element_type=F32))


@jax.custom_vjp
def _tri_mm(tri, tri_t, z):
    n = z.shape[2]
    hi = z.astype(BF16)
    rest = z - hi.astype(F32)
    mid = rest.astype(BF16)
    lo = (rest - mid.astype(F32)).astype(BF16)
    out = jnp.einsum('hik,hkj->hij', tri.astype(BF16), jnp.concatenate([hi, mid, lo], axis=2),
                     preferred_element_type=F32)
    return out[:, :, :n] + out[:, :, n:2 * n] + out[:, :, 2 * n:]


def _tri_mm_bwd(res, g):
    tri, tri_t = res
    return jnp.zeros_like(tri), jnp.zeros_like(tri_t), _tri_mm(tri_t, tri, g)


_tri_mm.defvjp(lambda tri, tri_t, z: (_tri_mm(tri, tri_t, z), (tri, tri_t)), _tri_mm_bwd)


def _chunk_cumsum(lw, incl, incl_t):
    shape = (lw.shape[0],) + incl.shape
    return _tri_mm(jnp.broadcast_to(incl.astype(F32), shape), jnp.broadcast_to(incl_t.astype(F32), shape), lw)


@functools.partial(jax.custom_vjp, nondiff_argnums=(0,))
def _bmm(p, a, b):
    return _ein(p, 'hik,hkj->hij', a, b)


@functools.partial(jax.custom_vjp, nondiff_argnums=(0,))
def _bmm_nt(p, a, b):
    return _ein(p, 'hik,hjk->hij', a, b)


@functools.partial(jax.custom_vjp, nondiff_argnums=(0,))
def _bmm_tn(p, a, b):
    return _ein(p, 'hki,hkj->hij', a, b)


_bmm.defvjp(lambda p, a, b: (_bmm(p, a, b), (a, b)),
            lambda p, res, g: (_bmm_nt(p, g, res[1]), _bmm_tn(p, res[0], g)))
_bmm_nt.defvjp(lambda p, a, b: (_bmm_nt(p, a, b), (a, b)),
               lambda p, res, g: (_bmm(p, g, res[1]), _bmm_tn(p, g, res[0])))
_bmm_tn.defvjp(lambda p, a, b: (_bmm_tn(p, a, b), (a, b)),
               lambda p, res, g: (_bmm_nt(p, res[1], g), _bmm(p, res[0], g)))


@jax.custom_vjp
def _split_rows(x):
    c = x.shape[1] // 2
    return x[:, :c], x[:, c:]


_split_rows.defvjp(lambda x: (_split_rows(x), None), lambda _, g: (jnp.concatenate(g, axis=1),))


def _stack_rows(a, b):
    return jnp.concatenate([a, b], axis=1)


def _nilpotent_inverse(l_mat):
    c = l_mat.shape[1]
    ps = RK_PASSES["solve"]
    row = lax.broadcasted_iota(jnp.int32, (c, c), 0)
    col = lax.broadcasted_iota(jnp.int32, (c, c), 1)
    x = -l_mat
    inv = jnp.where(row == col, 1.0, 0.0) + x
    power = _bmm(ps, x, x)
    span = 2
    while 2 * span < c:
        step, power = _split_rows(_bmm(ps, _stack_rows(inv, power), power))
        inv = inv + step
        span *= 2
    return inv + _bmm(ps, inv, power)


@jax.custom_vjp
def _nilpotent_solve(l_mat, rhs):
    return _bmm(RK_PASSES["solve"], _nilpotent_inverse(l_mat), rhs)


def _nilpotent_solve_fwd(l_mat, rhs):
    inv = _nilpotent_inverse(l_mat)
    u = _bmm(RK_PASSES["solve"], inv, rhs)
    return u, (inv, u)


def _nilpotent_solve_bwd(res, g):
    inv, u = res
    d_rhs = _bmm_tn(RK_PASSES["solve"], inv, g)
    return -_bmm_nt(RK_PASSES["solve"], d_rhs, u), d_rhs


_nilpotent_solve.defvjp(_nilpotent_solve_fwd, _nilpotent_solve_bwd)


def _rk_chunk(s0, r, lw, k, v, kk, a, reverse):
    h, c, n = r.shape
    row = lax.broadcasted_iota(jnp.int32, (c, c), 0)
    col = lax.broadcasted_iota(jnp.int32, (c, c), 1)
    incl = (row <= col) if reverse else (row >= col)
    strict = (row < col) if reverse else (row > col)
    cum = _chunk_cumsum(lw, incl, (row >= col) if reverse else (row <= col))
    g_in = jnp.exp(cum)
    g_inv = jnp.exp(-cum)
    kap = kk * jnp.exp(cum - lw)
    beta = kk * a * g_inv
    kt = k * g_inv
    rt = r * g_in
    p, ps = RK_PASSES, RK_PASSES["solve"]
    both = _stack_rows(kap, rt)
    kap_beta, rt_beta = _split_rows(_bmm_nt(ps, both, beta))
    kap_kt, rt_kt = _split_rows(_bmm_nt(p["kt"], both, kt))
    kap_s0, rt_s0 = _split_rows(_bmm_nt(p["s0"], both, s0))
    l_mat = jnp.where(strict, kap_beta, 0.0)
    rhs = kap_s0 + _bmm(p["akk_v"], jnp.where(strict, kap_kt, 0.0), v)
    u = _nilpotent_solve(l_mat, rhs)
    y = (rt_s0 + _bmm(p["ark_v"], jnp.where(incl, rt_kt, 0.0), v)
         - _bmm(p["arb_u"], jnp.where(incl, rt_beta, 0.0), u))
    s1 = ((s0 + _bmm_tn(p["state"], _stack_rows(v, -u), _stack_rows(kt, beta)))
          * jnp.exp(jnp.sum(lw, axis=1, keepdims=True)))
    return y, s1


class _Plan:
    def __init__(self, arrays, out_shape, sems, start, wait, finish):
        self.arrays, self.out_shape, self.sems = list(arrays), list(out_shape), list(sems)
        self.start, self.wait, self.finish = start, wait, finish


_NO_PLAN = _Plan([], [], [], lambda *_: None, lambda *_: None, lambda outs: [])


def _join_plans(plans):
    def cut(seq, sizes):
        out, pos = [], 0
        for s in sizes:
            out.append(seq[pos:pos + s])
            pos += s
        return out

    n_arr, n_sem = [len(p.arrays) for p in plans], [len(p.sems) for p in plans]

    def run(which):
        def go(in_refs, out_refs, sems):
            for p, i, o, s in zip(plans, cut(in_refs, n_arr), cut(out_refs, n_arr), cut(sems, n_sem)):
                getattr(p, which)(i, o, s)
        return go

    return _Plan([a for p in plans for a in p.arrays], [s for p in plans for s in p.out_shape],
                 [s for p in plans for s in p.sems], run("start"), run("wait"),
                 lambda outs: [p.finish(o) for p, o in zip(plans, cut(outs, n_arr))])


def _split_heads(x):
    return jnp.stack([x[:, RK_HEAD * i:RK_HEAD * (i + 1)] for i in range(RK_HEADS)], axis=0)


def _store_heads(ref, x):
    for i in range(RK_HEADS):
        ref[:, RK_HEAD * i:RK_HEAD * (i + 1)] = x[i]


def _rk_core_fwd(name, r, lw, k, v, kk, a, reverse, chunk, hosted=None):
    t = r.shape[0]
    h, n = RK_HEADS, RK_HEAD
    nc = t // chunk

    def idx(i):
        return nc - 1 - i if reverse else i

    hosted = hosted or _NO_PLAN
    nh = len(hosted.arrays)

    def body(r_ref, lw_ref, k_ref, v_ref, kk_ref, a_ref, *rest):
        host_in, (y_ref, ck_ref), host_out = rest[:nh], rest[nh:nh + 2], rest[nh + 2:2 * nh + 2]
        s_ref, sems = rest[2 * nh + 2], rest[2 * nh + 3:]

        @pl.when(pl.program_id(0) == 0)
        def _():
            s_ref[...] = jnp.zeros_like(s_ref)
            hosted.start(host_in, host_out, sems)

        s0 = s_ref[...]
        ck_ref[0] = s0
        ops = [_split_heads(ref[...]) for ref in (r_ref, lw_ref, k_ref, v_ref, kk_ref, a_ref)]
        y, s1 = _rk_chunk(s0, *ops, reverse)
        _store_heads(y_ref, y)
        s_ref[...] = s1

        @pl.when(pl.program_id(0) == nc - 1)
        def _():
            hosted.wait(host_in, host_out, sems)

    blk = pl.BlockSpec((chunk, RK_WIDTH), lambda i: (idx(i), 0))
    any_spec = pl.BlockSpec(memory_space=pl.ANY)
    res = pl.pallas_call(
        body, name=name, grid=(nc,), in_specs=[blk] * 6 + [any_spec] * nh,
        out_specs=[blk, pl.BlockSpec((1, h, n, n), lambda i: (idx(i), 0, 0, 0))] + [any_spec] * nh,
        out_shape=[jax.ShapeDtypeStruct((t, RK_WIDTH), F32), jax.ShapeDtypeStruct((nc, h, n, n), F32)]
        + hosted.out_shape,
        scratch_shapes=[pltpu.VMEM((h, n, n), F32)] + hosted.sems,
        compiler_params=_params(("arbitrary",)),
    )(r, lw, k, v, kk, a, *hosted.arrays)
    return res[0], res[1], hosted.finish(res[2:])


def _rk_core_bwd(name, r, lw, k, v, kk, a, ck, dy, reverse, chunk, hosted=None):
    t = r.shape[0]
    h, n = RK_HEADS, RK_HEAD
    nc = t // chunk
    hosted = hosted or _NO_PLAN
    nh = len(hosted.arrays)

    def idx(i):
        return i if reverse else nc - 1 - i

    def body(r_ref, lw_ref, k_ref, v_ref, kk_ref, a_ref, ck_ref, dy_ref, *rest):
        host_in, out_refs, host_out = rest[:nh], rest[nh:nh + 6], rest[nh + 6:2 * nh + 6]
        ds_ref, sems = rest[2 * nh + 6], rest[2 * nh + 7:]

        @pl.when(pl.program_id(0) == 0)
        def _():
            ds_ref[...] = jnp.zeros_like(ds_ref)
            hosted.start(host_in, host_out, sems)

        fn = functools.partial(_rk_chunk, reverse=reverse)
        ops = [_split_heads(ref[...]) for ref in (r_ref, lw_ref, k_ref, v_ref, kk_ref, a_ref)]
        _, vjp = jax.vjp(fn, ck_ref[0], *ops)
        grads = vjp((_split_heads(dy_ref[...]), ds_ref[...]))
        ds_ref[...] = grads[0]
        for o_ref, g in zip(out_refs, grads[1:]):
            _store_heads(o_ref, g)

        @pl.when(pl.program_id(0) == nc - 1)
        def _():
            hosted.wait(host_in, host_out, sems)

    blk = pl.BlockSpec((chunk, RK_WIDTH), lambda i: (idx(i), 0))
    any_spec = pl.BlockSpec(memory_space=pl.ANY)
    res = pl.pallas_call(
        body, name=name, grid=(nc,),
        in_specs=[blk] * 6 + [pl.BlockSpec((1, h, n, n), lambda i: (idx(i), 0, 0, 0)), blk] + [any_spec] * nh,
        out_specs=[blk] * 6 + [any_spec] * nh,
        out_shape=[jax.ShapeDtypeStruct((t, RK_WIDTH), F32)] * 6 + hosted.out_shape,
        scratch_shapes=[pltpu.VMEM((h, n, n), F32)] + hosted.sems,
        compiler_params=_params(("arbitrary",)),
    )(r, lw, k, v, kk, a, ck, dy, *hosted.arrays)
    return res[:6], hosted.finish(res[6:])


def _s5_band_place():
    return jax.nn.one_hot(jnp.arange(S5_BLOCKS) % S5_PER_IN, S5_PER_IN, dtype=F32)


def _s5_in_blocks(bbar):
    b = jnp.transpose(bbar.reshape(S5_BLOCKS, 2, S5_STATE, S5_GROUP), (0, 1, 3, 2))
    band = jnp.einsum('jghp,gk->jghkp', b, jnp.eye(2, dtype=F32)).reshape(S5_BLOCKS, 32, 128)
    return jnp.einsum('jrc,jq->jqrc', band, _s5_band_place()).reshape(S5_BLOCKS, 128, 128)


def _s5_in_unblock(mats):
    band = jnp.einsum('jqrc,jq->jrc', mats.reshape(S5_BLOCKS, S5_PER_IN, 32, 128), _s5_band_place())
    diag = jnp.einsum('jghgp->jghp', band.reshape(S5_BLOCKS, 2, S5_GROUP, 2, S5_STATE))
    return jnp.transpose(diag, (0, 1, 3, 2)).reshape(S5_CH, S5_GROUP)


def _s5_out_blocks(c):
    ct = jnp.transpose(c.reshape(S5_BLOCKS, 2, S5_GROUP, S5_STATE), (0, 1, 3, 2))
    band = jnp.einsum('jgph,gk->jgpkh', ct, jnp.eye(2, dtype=F32)).reshape(S5_BLOCKS, 128, 32)
    return jnp.einsum('jrc,jq->jrqc', band, _s5_band_place()).reshape(S5_BLOCKS, 128, 128)


def _s5_out_unblock(mats):
    band = jnp.einsum('jrqc,jq->jrc', mats.reshape(S5_BLOCKS, 128, S5_PER_IN, 32), _s5_band_place())
    diag = jnp.einsum('jgpgh->jgph', band.reshape(S5_BLOCKS, 2, S5_STATE, 2, S5_GROUP))
    return jnp.transpose(diag, (0, 1, 3, 2)).reshape(S5_GROUPS, S5_GROUP, S5_STATE)


def _head_indicator():
    ch = lax.broadcasted_iota(jnp.int32, (RK_WIDTH, 128), 0) // RK_HEAD
    hd = lax.broadcasted_iota(jnp.int32, (RK_WIDTH, 128), 1)
    seg = (ch == hd).astype(F32)
    return seg, seg.T


def _add_epilogue(acc, e):
    return (acc + e,)


def _local_step(x, target, mod, wt, chunk=RK_CHUNK, ffn_shards=None, mixer_shards=None):
    t = x.shape[0]
    wt = dict(wt)
    sh1, sc1, gt1, sh2, sc2, gt2 = mod
    seg, seg_t = _head_indicator()
    g = {}

    (h1,) = _rowwise("norm1", _norm_mod_fn, [x], [wt["norm1_gain"], sc1, sh1], [(D_MODEL, BF16)], 256)
    if mixer_shards is None:
        proj = _matmul("proj", h1, wt["w_in"])
    else:
        proj, gathered = _matmul("proj", h1, wt["w_in"], hosted=_gather_halves_plan(mixer_shards[0]))
        wt.update(mixer_shards[1](gathered))
    u = proj[:, :S5_WIDTH]
    ps = _token_shift(proj, wt["mu_prev"], wt["mu_next"], first=S5_WIDTH // 128)
    r, k, v = ps[:, :1024], ps[:, 1024:2048], ps[:, 2048:3072]
    wdn, adn, gdn = ps[:, 3072:3200], ps[:, 3200:3328], ps[:, 3328:RK_PAD]

    prep_rows = [wt["lam_re"][0], wt["lam_im"][0], wt["log_step"][0], wt["lam_re"][1], wt["lam_im"][1],
                 wt["log_step"][1], wt["b_re"], wt["b_im"]]
    col1, col16 = (1, F32), (S5_GROUP, F32)
    prep = _rowwise("s5_prep", _s5_prep_fn, prep_rows, [], [col1, col1, col16, col16] * 2, 512)
    lbar = [tuple(prep[4 * d + q].reshape(S5_BLOCKS, 1, 128) for q in range(2)) for d in range(2)]
    b_blk = [tuple(_s5_in_blocks(prep[4 * d + 2 + q]) for q in range(2)) for d in range(2)]
    c_blk = (_s5_out_blocks(wt["c_re"]), -_s5_out_blocks(wt["c_im"]))
    u_il = _interleave(u)
    state0 = _s5_forward("s5_fwd0", u_il, *b_blk[0], *lbar[0], reverse=False)
    s1_re, s1_im, ylin_il = _s5_forward("s5_fwd1", u_il, *b_blk[1], *lbar[1], reverse=True, other=state0,
                                        c_re=c_blk[0], c_im_neg=c_blk[1])
    ylin = _deinterleave(ylin_il)
    states = [tuple(state0), (s1_re, s1_im)]
    s5_par = [wt["s5_d"], wt["s5_w_glu"], wt["s5_b_glu"]]
    (o_s5,) = _rowwise("s5_out", _s5_out_fn, [ylin, u], s5_par, [(S5_WIDTH, BF16)], 256)

    pre_par = [wt["w0"][0], wt["w0"][1], wt["w_up"][0], wt["w_up"][1], wt["a0"][0], wt["a0"][1],
               wt["a_up"][0], wt["a_up"][1], wt["g_up"], wt["k_k"], wt["k_a"]]
    pre = _rowwise("rk_pre", _rk_pre_fn, [k, wdn, adn, gdn], pre_par + [seg, seg_t], [(RK_WIDTH, F32)] * 8, 256)
    kk, lw, kd, act, gate = pre[0], pre[1:3], pre[3:5], pre[5:7], pre[7]
    core_in, ys, cks = [], [], []
    for d in range(2):
        ops = (r, lw[d], kd[d], v, kk, act[d])
        plan = _gather_halves_plan([ffn_shards[d]]) if ffn_shards is not None else None
        y, ck, gathered = _rk_core_fwd(f"rk_core{d}", *ops, reverse=(d == 1), chunk=min(chunk, t), hosted=plan)
        if gathered:
            wt["ffn_w1" if d == 0 else "ffn_w2"] = gathered[0] if d == 0 else gathered[0].reshape(FFN, D_MODEL)
        core_in.append(ops)
        ys.append(y)
        cks.append(ck)
    post_rows = [ys[0], ys[1], r, v, kd[0], kd[1], gate]
    post_par = [wt["ln_gain"], wt["ln_bias"], wt["r_k"]]
    (o_rk,) = _rowwise("rk_post", _rk_post_fn, post_rows, post_par + [seg, seg_t], [(RK_WIDTH, BF16)], 256)

    o = jnp.concatenate([o_s5, o_rk], axis=1)
    mixed = _matmul("mix_out", o, wt["w_out"])
    n2_par = [gt1, wt["norm2_gain"], sc2, sh2]
    x1, h2 = _rowwise("norm2", _resid_norm_mod_fn, [x, mixed], n2_par, [(D_MODEL, F32), (D_MODEL, BF16)], 256)
    f1, hid = _matmul("ffn1", h2, wt["ffn_w1"], out_dtypes=(F32, BF16), chips="b",
                      epilogue=lambda acc: (acc, jnp.square(jnp.maximum(acc, 0.0))))
    ffn = _matmul("ffn2", hid, wt["ffn_w2"])

    ones = jnp.ones((t, 1), F32)
    loss_rows, dx1, dffn, g_gt2, g["final_gain"] = _rowwise_vjp(
        "loss", _loss_fn, [x1, ffn, target], [gt2, wt["final_gain"]], [[ones]], [0, 1], [0, 1], 256, emit=(0,),
        row_grad_dtypes=[F32, BF16])
    df1 = _matmul("ffn2_dx", dffn, wt["ffn_w2"], tb=True, extras=(f1,), out_dtypes=(BF16,),
                  epilogue=lambda acc, f: (acc * (2.0 * jnp.maximum(f, 0.0)),))
    g["ffn_w2"] = _matmul("ffn2_dw", hid, dffn, ta=True)
    if ffn_shards is None:
        g["ffn_w1"] = _matmul("ffn1_dw", h2, df1, ta=True, chips="out")
        dh2 = _matmul("ffn1_dx", df1, wt["ffn_w1"], tb=True, chips="b_t")
    else:
        piece2 = g.pop("ffn_w2").reshape(N_CHIPS, -1, D_MODEL)
        piece1, (other2,) = _matmul("ffn1_dw", h2, df1, ta=True, chips="out", hosted=_other_half_plan([piece2]))
        dh2, (other1,) = _matmul("ffn1_dx", df1, wt["ffn_w1"], tb=True, chips="b_t",
                                 hosted=_other_half_plan([piece1]))
        ffn_sums = [_pair_sum("pair_" + n, piece, other, BF16)
                    for n, piece, other in zip(FFN_SHARDED, (piece1, piece2), (other1, other2))]
    dx_a, dmixed, g_gt1, g["norm2_gain"], g_sc2, g_sh2 = _rowwise_vjp(
        "norm2_bwd", _resid_norm_mod_fn, [x, mixed], n2_par, [[dx1], [dh2]], [0, 1], [0, 1, 2, 3], 256,
        row_grad_dtypes=[F32, BF16])
    do = _matmul("mix_out_dx", dmixed, wt["w_out"], tb=True)
    g["w_out"] = _matmul("mix_out_dw", o, dmixed, ta=True)
    do_s5, do_rk = do[:, :S5_WIDTH], do[:, S5_WIDTH:]

    dylin, du, g["s5_d"], g["s5_w_glu"], g["s5_b_glu"] = _rowwise_vjp(
        "s5_out_bwd", _s5_out_fn, [ylin, u], s5_par, [[do_s5]], [0, 1], [0, 1, 2], 256)
    prep_cts = []
    dylin_il, du_il = _interleave(dylin), _interleave(du)
    for d in range(2):
        res = _s5_backward(f"s5_bwd{d}", dylin_il, u_il, du_il, states[d], states[1] if d == 0 else None,
                           *b_blk[d], *c_blk, *lbar[d], reverse=(d == 1))
        du_il, db_re, db_im, dl_re, dl_im = res[:5]
        if d == 0:
            g["c_re"], g["c_im"] = _s5_out_unblock(res[5]), -_s5_out_unblock(res[6])
        prep_cts += [[dl_re.reshape(S5_CH, 1)], [dl_im.reshape(S5_CH, 1)], [_s5_in_unblock(db_re)],
                     [_s5_in_unblock(db_im)]]
    du = _deinterleave(du_il)
    pg = _rowwise_vjp("s5_prep_bwd", _s5_prep_fn, prep_rows, [], prep_cts, list(range(8)), [], 512)
    g["lam_re"], g["lam_im"], g["log_step"] = (pg[0], pg[3]), (pg[1], pg[4]), (pg[2], pg[5])
    g["b_re"], g["b_im"] = pg[6], pg[7]

    pb = _rowwise_vjp("rk_post_bwd", _rk_post_fn, post_rows, post_par, [[do_rk]], [0, 2, 3, 4, 5, 6], [0, 1, 2],
                      128, consts=[seg, seg_t])
    dy, dr_b, dv_b, dkd_b, dgate = pb[0], pb[1], pb[2], pb[3:5], pb[5]
    g["ln_gain"], g["ln_bias"], g["r_k"] = pb[6], pb[7], pb[8]
    cg = []
    for d in range(2):
        plan = None
        if ffn_shards is not None:
            plan = _exchange_plan([ffn_sums[d]], CHIP_PEERS, N_CHIPS, scatter=True)
        grads, arrived = _rk_core_bwd(f"rk_core{d}_bwd", *core_in[d], cks[d], dy, reverse=(d == 1),
                                      chunk=min(chunk, t), hosted=plan)
        g.setdefault("ffn_arrived", []).extend(arrived)
        cg.append(grads)
    pre_cts = [[cg[0][4], cg[1][4]], [cg[0][1]], [cg[1][1]], [cg[0][2], dkd_b[0]], [cg[1][2], dkd_b[1]],
               [cg[0][5]], [cg[1][5]], [dgate]]
    qb = _rowwise_vjp("rk_pre_bwd", _rk_pre_fn, [k, wdn, adn, gdn], pre_par, pre_cts, [0, 1, 2, 3],
                      list(range(11)), 128, consts=[seg, seg_t])
    dk, dwdn, dadn, dgdn = qb[:4]
    g["w0"], g["w_up"], g["a0"], g["a_up"] = (qb[4], qb[5]), (qb[6], qb[7]), (qb[8], qb[9]), (qb[10], qb[11])
    g["g_up"], g["k_k"], g["k_a"] = qb[12], qb[13], qb[14]
    dr, dv = _rowwise("rk_sum", lambda a, b, c, e, f, h: (a + b + c, e + f + h),
                      [cg[0][0], cg[1][0], dr_b, cg[0][3], cg[1][3], dv_b], [], [(RK_WIDTH, F32)] * 2, 256)
    dps = jnp.concatenate([dr, dk, dv, dwdn, dadn, dgdn], axis=1)
    dp, g["mu_prev"], g["mu_next"] = _token_shift_bwd(proj, wt["mu_prev"], wt["mu_next"], dps,
                                                      first=S5_WIDTH // 128)

    dproj = jnp.concatenate([du, dp], axis=1).astype(BF16)
    dh1 = _matmul("proj_dx", dproj, wt["w_in"], tb=True)
    g["w_in"] = _matmul("proj_dw", h1, dproj, ta=True)
    grad_x, g["norm1_gain"], g_sc1, g_sh1 = _rowwise_vjp(
        "norm1_bwd", _norm_mod_fn, [x], [wt["norm1_gain"], sc1, sh1], [[dh1]], [0], [0, 1, 2], 256,
        addends={0: dx_a})
    g["mod"] = [g_sh1, g_sc1, g_gt1, g_sh2, g_sc2, g_gt2]
    return loss_rows, grad_x, g


CHIP_PEERS = ((1, 0, 0), (0, 1, 0), (1, 1, 0))
ALL_PEERS = ((0, 0, 1), (0, 1, 0), (0, 1, 1), (1, 0, 0), (1, 0, 1), (1, 1, 0), (1, 1, 1))
CORE_PEER = ((0, 0, 1),)


def _exchange(name, arrays, peers, n_slots, scatter=False):
    return _run_plan(name, _exchange_plan(arrays, peers, n_slots, scatter))


def _run_plan(name, plan):
    na = len(plan.arrays)

    def body(*refs):
        plan.start(refs[:na], refs[na:2 * na], refs[2 * na:])
        plan.wait(refs[:na], refs[na:2 * na], refs[2 * na:])

    any_spec = pl.BlockSpec(memory_space=pl.ANY)
    return plan.finish(pl.pallas_call(
        body, name=name, in_specs=[any_spec] * na, out_specs=[any_spec] * na, out_shape=plan.out_shape,
        scratch_shapes=plan.sems,
    )(*plan.arrays))


def _exchange_plan(arrays, peers, n_slots, scatter=False):
    na, nm = len(arrays), len(peers)

    def ident(px, py, pc):
        return {8: 4 * px + 2 * py + pc, 4: 2 * px + py, 2: pc}[n_slots]

    def copies(in_refs, out_refs, sems):
        send_sems, recv_sems = sems
        x, y, c = lax.axis_index("x"), lax.axis_index("y"), lax.axis_index("c")
        me = ident(x, y, c)
        made = []
        for i in range(na):
            for j, (fx, fy, fc) in enumerate(peers):
                px, py, pc = (1 - x if fx else x), (1 - y if fy else y), (1 - c if fc else c)
                src = in_refs[i].at[ident(px, py, pc)] if scatter else in_refs[i]
                made.append(pltpu.make_async_remote_copy(
                    src_ref=src, dst_ref=out_refs[i].at[me],
                    send_sem=send_sems.at[i * nm + j], recv_sem=recv_sems.at[i * nm + j],
                    device_id=(px, py, pc), device_id_type=pl.DeviceIdType.MESH))
        return made

    def start(in_refs, out_refs, sems):
        for copy in copies(in_refs, out_refs, sems):
            copy.start()

    def wait(in_refs, out_refs, sems):
        for copy in copies(in_refs, out_refs, sems):
            copy.wait()

    def finish(outs):
        me = ident(lax.axis_index("x"), lax.axis_index("y"), lax.axis_index("c"))
        return [lax.dynamic_update_slice_in_dim(
            o, lax.dynamic_index_in_dim(a, me, 0, keepdims=True) if scatter else a[None], me, axis=0)
            for a, o in zip(arrays, outs)]

    out_shape = [jax.ShapeDtypeStruct(((n_slots,) + a.shape[1:]) if scatter else ((n_slots,) + a.shape), a.dtype)
                 for a in arrays]
    sems = [pltpu.SemaphoreType.DMA((na * nm,)), pltpu.SemaphoreType.DMA((na * nm,))]
    return _Plan(arrays, out_shape, sems, start, wait, finish)


def _gather_halves(name, arrays):
    return _run_plan(name, _gather_halves_plan(arrays))


def _gather_halves_plan(arrays):
    na = len(arrays)
    chips = ((1, 0), (0, 1), (1, 1))

    def over_ici(in_refs, out_refs, sems):
        ici_send, ici_recv = sems[:2]
        x, y, c = lax.axis_index("x"), lax.axis_index("y"), lax.axis_index("c")
        made = []
        for i in range(na):
            half = arrays[i].shape[0] // 2
            mine = pl.ds(pl.multiple_of(c * half, 8), half)
            for j, (fx, fy) in enumerate(chips):
                px, py = (1 - x if fx else x), (1 - y if fy else y)
                k = len(chips) * i + j
                made.append([pltpu.make_async_remote_copy(
                    src_ref=in_refs[i].at[mine], dst_ref=out_refs[i].at[chip, mine],
                    send_sem=ici_send.at[k], recv_sem=ici_recv.at[k],
                    device_id=(px, py, c), device_id_type=pl.DeviceIdType.MESH)
                    for chip in (2 * x + y, 2 * px + py)])
        return made

    def start(in_refs, out_refs, sems):
        for outgoing, _ in over_ici(in_refs, out_refs, sems):
            outgoing.start()

    def wait(in_refs, out_refs, sems):
        d2d_send, d2d_recv = sems[2:]
        x, y, c = lax.axis_index("x"), lax.axis_index("y"), lax.axis_index("c")
        pending = []
        ici = over_ici(in_refs, out_refs, sems)
        for i in range(na):
            half = arrays[i].shape[0] // 2
            mine = pl.ds(pl.multiple_of(c * half, 8), half)
            theirs = pl.ds(pl.multiple_of((1 - c) * half, 8), half)
            for j, (fx, fy) in enumerate(chips):
                px, py = (1 - x if fx else x), (1 - y if fy else y)
                k = len(chips) * i + j
                outgoing, landing = ici[k]
                landing.wait_recv()
                landed = out_refs[i].at[2 * px + py, mine]
                passed = pltpu.make_async_remote_copy(
                    src_ref=landed, dst_ref=landed, send_sem=d2d_send.at[k], recv_sem=d2d_recv.at[k],
                    device_id=(x, y, 1 - c), device_id_type=pl.DeviceIdType.MESH)
                passed.start()
                from_sibling = out_refs[i].at[2 * px + py, theirs]
                pending += [outgoing.wait_send, passed.wait_send, pltpu.make_async_remote_copy(
                    src_ref=from_sibling, dst_ref=from_sibling, send_sem=d2d_send.at[k], recv_sem=d2d_recv.at[k],
                    device_id=(x, y, 1 - c), device_id_type=pl.DeviceIdType.MESH).wait_recv]
        for done in pending:
            done()

    def finish(outs):
        me = 2 * lax.axis_index("x") + lax.axis_index("y")
        return [lax.dynamic_update_slice_in_dim(o, a[None], me, axis=0) for a, o in zip(arrays, outs)]

    out_shape = [jax.ShapeDtypeStruct((N_CHIPS,) + a.shape, a.dtype) for a in arrays]
    return _Plan(arrays, out_shape, [pltpu.SemaphoreType.DMA((na * len(chips),))] * 4, start, wait, finish)


def _send_other_half(name, arrays):
    return _run_plan(name, _other_half_plan(arrays))


def _other_half_plan(arrays):
    na = len(arrays)

    def copies(in_refs, out_refs, sems):
        send_sems, recv_sems = sems
        x, y, c = lax.axis_index("x"), lax.axis_index("y"), lax.axis_index("c")
        made = []
        for i in range(na):
            half = arrays[i].shape[1] // 2
            theirs = pl.ds(pl.multiple_of((1 - c) * half, 8), half)
            made.append(pltpu.make_async_remote_copy(
                src_ref=in_refs[i].at[:, theirs], dst_ref=out_refs[i], send_sem=send_sems.at[i],
                recv_sem=recv_sems.at[i], device_id=(x, y, 1 - c), device_id_type=pl.DeviceIdType.MESH))
        return made

    def start(in_refs, out_refs, sems):
        for copy in copies(in_refs, out_refs, sems):
            copy.start()

    def wait(in_refs, out_refs, sems):
        for copy in copies(in_refs, out_refs, sems):
            copy.wait()

    out_shape = [jax.ShapeDtypeStruct((a.shape[0], a.shape[1] // 2, a.shape[2]), a.dtype) for a in arrays]
    sems = [pltpu.SemaphoreType.DMA((na,)), pltpu.SemaphoreType.DMA((na,))]
    return _Plan(arrays, out_shape, sems, start, wait, list)


def _adam_math(w, g, m, v):
    m = ADAM_B1 * m + (1.0 - ADAM_B1) * g
    v = ADAM_B2 * v + (1.0 - ADAM_B2) * jnp.square(g)
    m_hat = m / (1.0 - ADAM_B1 ** ADAM_STEP)
    v_hat = v / (1.0 - ADAM_B2 ** ADAM_STEP)
    delta = -ADAM_LR * (m_hat / (jnp.sqrt(v_hat) + ADAM_EPS) + ADAM_WD * w)
    return delta, m, v


WHOLE_BLOCK_BYTES = 2 * 1024 * 1024


def _row_tile(r, c):
    return r if 4 * r * c <= WHOLE_BLOCK_BYTES else _tile(r, (256, 128, 64, 32, 16, 8))


def _sum_parts(name, parts):
    n, r, c = parts.shape
    tr = _row_tile(r, c)

    def body(p_ref, o_ref):
        tot = p_ref[0].astype(F32)
        for i in range(1, n):
            tot = tot + p_ref[i].astype(F32)
        o_ref[...] = tot

    return pl.pallas_call(
        body, name=name, grid=(r // tr,), in_specs=[pl.BlockSpec((n, tr, c), lambda i: (0, i, 0))],
        out_specs=pl.BlockSpec((tr, c), lambda i: (i, 0)), out_shape=jax.ShapeDtypeStruct((r, c), F32),
        compiler_params=_params(("parallel",)),
    )(parts)


def _pair_sum(name, piece, other, dtype):
    n, r, c = piece.shape
    half = r // 2
    tr = _row_tile(half, c)

    def body(lo_ref, hi_ref, other_ref, o_ref):
        own = jnp.where(lax.axis_index("c") == 0, lo_ref[...], hi_ref[...])
        o_ref[...] = (own + other_ref[...]).astype(o_ref.dtype)

    blk = pl.BlockSpec((None, tr, c), lambda j, i: (j, i, 0))
    return pl.pallas_call(
        body, name=name, grid=(n, half // tr),
        in_specs=[pl.BlockSpec((None, None, tr, c), lambda j, i: (j, 0, i, 0)),
                  pl.BlockSpec((None, None, tr, c), lambda j, i: (j, 1, i, 0)), blk],
        out_specs=blk, out_shape=jax.ShapeDtypeStruct((n, half, c), dtype),
        compiler_params=_params(("parallel", "parallel")),
    )(piece.reshape(n, 2, half, c), piece.reshape(n, 2, half, c), other)


def _adamw(name, w, parts, m, v, hosted=None):
    n, r, c = parts.shape
    tr = _row_tile(r, c)
    steps = r // tr
    plan = hosted or _NO_PLAN
    nh = len(plan.arrays)

    def body(w_ref, p_ref, m_ref, v_ref, *rest):
        host_in, (g_ref, d_ref, nm_ref, nv_ref) = rest[:nh], rest[nh:nh + 4]
        host_out, sems = rest[nh + 4:2 * nh + 4], rest[2 * nh + 4:]

        @pl.when(pl.program_id(0) == 0)
        def _():
            plan.start(host_in, host_out, sems)

        g = p_ref[0].astype(F32)
        for i in range(1, n):
            g = g + p_ref[i].astype(F32)
        delta, nm, nv = _adam_math(w_ref[...], g, m_ref[...], v_ref[...])
        g_ref[...], d_ref[...], nm_ref[...], nv_ref[...] = g, delta, nm, nv

        @pl.when(pl.program_id(0) == steps - 1)
        def _():
            plan.wait(host_in, host_out, sems)

    blk = pl.BlockSpec((tr, c), lambda i: (i, 0))
    any_spec = pl.BlockSpec(memory_space=pl.ANY)
    res = pl.pallas_call(
        body, name=name, grid=(steps,),
        in_specs=[blk, pl.BlockSpec((n, tr, c), lambda i: (0, i, 0)), blk, blk] + [any_spec] * nh,
        out_specs=[blk] * 4 + [any_spec] * nh,
        out_shape=[jax.ShapeDtypeStruct((r, c), F32)] * 4 + plan.out_shape, scratch_shapes=plan.sems,
        compiler_params=_params(("arbitrary",) if nh else ("parallel",)),
    )(w, parts, m, v, *plan.arrays)
    return (res[:4], plan.finish(res[4:])) if nh else res


def _ada_w_update(act_t, dmod, w, m, v, hosted):
    r, c = w.shape
    nb = act_t.shape[1]
    tr, tc = 256, 1024
    grid = (r // tr, c // tc)
    nh = len(hosted.arrays)

    def body(a_ref, d_ref, w_ref, m_ref, v_ref, *rest):
        host_in, (g_ref, dl_ref, nm_ref, nv_ref) = rest[:nh], rest[nh:nh + 4]
        host_out, sems = rest[nh + 4:2 * nh + 4], rest[2 * nh + 4:]
        i, j = pl.program_id(0), pl.program_id(1)

        @pl.when(jnp.logical_and(i == 0, j == 0))
        def _():
            hosted.start(host_in, host_out, sems)

        a, dm = a_ref[...], d_ref[...]
        g = a[:, 0:1] * dm[0:1, :]
        for b in range(1, nb):
            g = g + a[:, b:b + 1] * dm[b:b + 1, :]
        delta, nm, nv = _adam_math(w_ref[...], g, m_ref[...], v_ref[...])
        g_ref[...], dl_ref[...], nm_ref[...], nv_ref[...] = g, delta, nm, nv

        @pl.when(jnp.logical_and(i == grid[0] - 1, j == grid[1] - 1))
        def _():
            hosted.wait(host_in, host_out, sems)

    blk = pl.BlockSpec((tr, tc), lambda i, j: (i, j))
    any_spec = pl.BlockSpec(memory_space=pl.ANY)
    res = pl.pallas_call(
        body, name="ada_w_update", grid=grid,
        in_specs=[pl.BlockSpec((tr, nb), lambda i, j: (i, 0)), pl.BlockSpec((nb, tc), lambda i, j: (0, j)),
                  blk, blk, blk] + [any_spec] * nh,
        out_specs=[blk] * 4 + [any_spec] * nh,
        out_shape=[jax.ShapeDtypeStruct((r, c), F32)] * 4 + hosted.out_shape,
        scratch_shapes=hosted.sems,
        compiler_params=_params(("arbitrary", "arbitrary")),
    )(act_t, dmod, w, m, v, *hosted.arrays)
    return res[:4], hosted.finish(res[4:])


WEIGHTS = ['ada_w', 'ada_b', 'norm1_gain', 'norm2_gain', 'final_gain', 'w_in', 'w_out', 's5_lambda_re',
           's5_lambda_im', 's5_log_step', 's5_b_re', 's5_b_im', 's5_c_re', 's5_c_im', 's5_d', 's5_w_glu',
           's5_b_glu', 'rk_shift_prev', 'rk_shift_next', 'rk_w0', 'rk_w_up', 'rk_a0', 'rk_a_up', 'rk_g_up',
           'rk_k_k', 'rk_k_a', 'rk_r_k', 'rk_ln_gain', 'rk_ln_bias', 'ffn_w1', 'ffn_w2']
BIG_SHARDED = ['w_in', 'w_out', 's5_w_glu', 'ffn_w1', 'ffn_w2']
FFN_SHARDED = ['ffn_w1', 'ffn_w2']
RK_SHARDED = ['rk_w0', 'rk_a0', 'rk_w_up', 'rk_a_up', 'rk_g_up']
REPLICATED = ['ada_b', 'norm1_gain', 'norm2_gain', 'final_gain', 's5_lambda_re', 's5_lambda_im', 's5_log_step',
              's5_b_re', 's5_b_im', 's5_c_re', 's5_c_im', 's5_d', 's5_b_glu', 'rk_shift_prev', 'rk_shift_next',
              'rk_k_k', 'rk_k_a', 'rk_r_k', 'rk_ln_gain', 'rk_ln_bias']
PACK_COLS = 1024
N_CHIPS = 4
RK_ROWS = 420
RK_ROWS_PAD = 432


def _pack_rows(arrays, cols):
    return jnp.concatenate([a.reshape(-1, cols) for a in arrays], axis=0)


def _pack_flat(arrays):
    flat = jnp.concatenate([a.reshape(-1) for a in arrays])
    rows = -(-flat.shape[0] // PACK_COLS)
    return jnp.pad(flat, (0, rows * PACK_COLS - flat.shape[0])).reshape(rows, PACK_COLS)


def _unpack_flat(packed, like):
    flat, out, pos = packed.reshape(-1), [], 0
    for a in like:
        out.append(flat[pos:pos + a.size].reshape(a.shape))
        pos += a.size
    return out


def _cols_to_chips(full, n_rows):
    return jnp.transpose(full.reshape(n_rows, N_CHIPS, -1), (1, 0, 2))


def _chips_to_cols(parts):
    return jnp.transpose(parts, (1, 0, 2)).reshape(parts.shape[1], -1)


def kernel(x, c, ada_w, ada_b, norm1_gain, norm2_gain, final_gain, w_in, w_out, s5_lambda_re, s5_lambda_im, s5_log_step, s5_b_re, s5_b_im, s5_c_re, s5_c_im, s5_d, s5_w_glu, s5_b_glu, rk_shift_prev, rk_shift_next, rk_w0, rk_w_up, rk_a0, rk_a_up, rk_g_up, rk_k_k, rk_k_a, rk_r_k, rk_ln_gain, rk_ln_bias, ffn_w1, ffn_w2, loss_target, m_ada_w, m_ada_b, m_norm1_gain, m_norm2_gain, m_final_gain, m_w_in, m_w_out, m_s5_lambda_re, m_s5_lambda_im, m_s5_log_step, m_s5_b_re, m_s5_b_im, m_s5_c_re, m_s5_c_im, m_s5_d, m_s5_w_glu, m_s5_b_glu, m_rk_shift_prev, m_rk_shift_next, m_rk_w0, m_rk_w_up, m_rk_a0, m_rk_a_up, m_rk_g_up, m_rk_k_k, m_rk_k_a, m_rk_r_k, m_rk_ln_gain, m_rk_ln_bias, m_ffn_w1, m_ffn_w2, v_ada_w, v_ada_b, v_norm1_gain, v_norm2_gain, v_final_gain, v_w_in, v_w_out, v_s5_lambda_re, v_s5_lambda_im, v_s5_log_step, v_s5_b_re, v_s5_b_im, v_s5_c_re, v_s5_c_im, v_s5_d, v_s5_w_glu, v_s5_b_glu, v_rk_shift_prev, v_rk_shift_next, v_rk_w0, v_rk_w_up, v_rk_a0, v_rk_a_up, v_rk_g_up, v_rk_k_k, v_rk_k_a, v_rk_r_k, v_rk_ln_gain, v_rk_ln_bias, v_ffn_w1, v_ffn_w2):
    given = dict(locals())
    w = {n: given[n] for n in WEIGHTS}
    m = {n: given["m_" + n] for n in WEIGHTS}
    v = {n: given["v_" + n] for n in WEIGHTS}
    mx, my, mc = lax.axis_index("x"), lax.axis_index("y"), lax.axis_index("c")
    chip = 2 * mx + my
    dev = 2 * chip + mc
    xt, target = x[0], loss_target[0]

    def rk_rows(d):
        rows = _pack_rows([d[n] for n in RK_SHARDED], 256)
        return jnp.pad(rows, ((0, RK_ROWS_PAD - rows.shape[0]), (0, 0)))

    (c_all,), (w_in_parts,) = _run_plan("gather_first", _join_plans([
        _exchange_plan([c], ALL_PEERS, 8), _gather_halves_plan([w_in[0].astype(BF16)])]))

    (act,) = _rowwise("ada_act", lambda q: (q * _sigmoid(q),), [c_all.reshape(8, D_MODEL)], [], [(D_MODEL, F32)], 8)
    n_mod_cols = N_MOD * D_MODEL // N_CHIPS
    bias = jnp.broadcast_to(lax.dynamic_slice(ada_b, (0, chip * n_mod_cols), (1, n_mod_cols)), (8, n_mod_cols))
    mod_shard = _matmul("ada_fwd", act, ada_w[0], epilogue=_add_epilogue, extras=(bias,))
    (mod_parts,) = _exchange("gather_mod", [mod_shard], CHIP_PEERS, N_CHIPS)
    mod_all = _chips_to_cols(mod_parts)
    mod_mine = lax.dynamic_slice(mod_all, (dev, 0), (1, N_MOD * D_MODEL))
    mod = [mod_mine[:, i * D_MODEL:(i + 1) * D_MODEL] for i in range(N_MOD)]

    def mixer_weights(parts):
        w_out_parts, glu_parts, rk_full = parts

        def rk_piece(lo, hi, lead):
            return _chips_to_cols(rk_full[:, lo:hi]).reshape(lead + (RK_WIDTH,))

        zeros = jnp.zeros((LORA, RK_WIDTH), F32)
        w_up, a_up = rk_piece(4, 132, (2, LORA)), rk_piece(132, 260, (2, LORA))
        return {
            "w_out": w_out_parts.reshape(D_MODEL, D_MODEL), "s5_w_glu": glu_parts.reshape(S5_WIDTH, S5_WIDTH),
            "w0": list(rk_piece(0, 2, (2,))[:, None, :]), "a0": list(rk_piece(2, 4, (2,))[:, None, :]),
            "w_up": [jnp.concatenate([w_up[0], zeros]), jnp.concatenate([zeros, w_up[1]])],
            "a_up": [jnp.concatenate([a_up[0], zeros]), jnp.concatenate([zeros, a_up[1]])],
            "g_up": jnp.pad(rk_piece(260, 420, (GATE_LORA,)), ((0, GATE_PAD - GATE_LORA), (0, 0))),
        }

    wt = {
        "norm1_gain": norm1_gain, "norm2_gain": norm2_gain, "final_gain": final_gain.reshape(1, D_MODEL),
        "w_in": jnp.pad(_chips_to_cols(w_in_parts), ((0, 0), (0, PROJ_PAD - PROJ))),
        "mu_prev": jnp.pad(rk_shift_prev, ((0, 0), (0, RK_PAD - RK_IN))),
        "mu_next": jnp.pad(rk_shift_next, ((0, 0), (0, RK_PAD - RK_IN))),
        "lam_re": [s5_lambda_re[0, d].reshape(S5_CH, 1) for d in range(2)],
        "lam_im": [s5_lambda_im[0, d].reshape(S5_CH, 1) for d in range(2)],
        "log_step": [jnp.repeat(s5_log_step[0, d], S5_STATE).reshape(S5_CH, 1) for d in range(2)],
        "b_re": s5_b_re.reshape(S5_CH, S5_GROUP), "b_im": s5_b_im.reshape(S5_CH, S5_GROUP),
        "c_re": s5_c_re[0], "c_im": s5_c_im[0],
        "s5_d": s5_d, "s5_b_glu": s5_b_glu,
        "k_k": rk_k_k, "k_a": rk_k_a, "r_k": rk_r_k.reshape(1, RK_WIDTH),
        "ln_gain": rk_ln_gain, "ln_bias": rk_ln_bias,
    }

    ffn_shards = [w[n][0].astype(BF16) for n in FFN_SHARDED]
    mixer_shards = [w_out[0].astype(BF16), s5_w_glu[0].astype(BF16), rk_rows(w)]
    loss_rows, grad_x, g = _local_step(xt, target, mod, wt, ffn_shards=ffn_shards,
                                       mixer_shards=(mixer_shards, mixer_weights))
    loss = lax.psum(jnp.sum(loss_rows), ("x", "y", "c"))


    big_grads = {
        "w_in": _cols_to_chips(g["w_in"][:, :PROJ], D_MODEL),
        "w_out": g["w_out"].reshape(N_CHIPS, -1, D_MODEL),
        "s5_w_glu": g["s5_w_glu"].reshape(N_CHIPS, -1, S5_WIDTH),
    }
    rk_grads = jnp.concatenate([
        _cols_to_chips(jnp.concatenate(g["w0"]), 2), _cols_to_chips(jnp.concatenate(g["a0"]), 2),
        _cols_to_chips(jnp.concatenate([g["w_up"][0][:LORA], g["w_up"][1][LORA:]]), 2 * LORA),
        _cols_to_chips(jnp.concatenate([g["a_up"][0][:LORA], g["a_up"][1][LORA:]]), 2 * LORA),
        _cols_to_chips(g["g_up"][:GATE_LORA], GATE_LORA),
        jnp.zeros((N_CHIPS, RK_ROWS_PAD - RK_ROWS, 256), F32)], axis=1)
    local_small = {
        "ada_b": jnp.concatenate(g["mod"], axis=1),
        "norm1_gain": g["norm1_gain"], "norm2_gain": g["norm2_gain"], "final_gain": g["final_gain"],
        "s5_lambda_re": jnp.concatenate(g["lam_re"]), "s5_lambda_im": jnp.concatenate(g["lam_im"]),
        "s5_log_step": jnp.concatenate([q.reshape(S5_GROUPS, S5_STATE).sum(axis=1) for q in g["log_step"]]),
        "s5_b_re": g["b_re"], "s5_b_im": g["b_im"], "s5_c_re": g["c_re"], "s5_c_im": g["c_im"],
        "s5_d": g["s5_d"], "s5_b_glu": g["s5_b_glu"],
        "rk_shift_prev": g["mu_prev"][:, :RK_IN], "rk_shift_next": g["mu_next"][:, :RK_IN],
        "rk_k_k": g["k_k"], "rk_k_a": g["k_a"], "rk_r_k": g["r_k"],
        "rk_ln_gain": g["ln_gain"], "rk_ln_bias": g["ln_bias"],
    }
    late = [n for n in BIG_SHARDED if n not in FFN_SHARDED]
    late_pieces = [big_grads[n] for n in late] + [rk_grads]
    late_names = late + ["rk"]

    def whole(halves):
        return halves.reshape(1, 2 * halves.shape[1], halves.shape[2])

    ffn_halves = [_sum_parts("sum_" + n, a) for n, a in zip(FFN_SHARDED, g["ffn_arrived"])]
    from_sibling, ffn_pairs, (small_all,) = _run_plan("swap_late", _join_plans([
        _other_half_plan(late_pieces), _exchange_plan(ffn_halves, CORE_PEER, 2),
        _exchange_plan([_pack_flat([local_small[n] for n in REPLICATED]).astype(BF16)], ALL_PEERS, 8)]))
    late_sums = [_pair_sum("pair_" + n, piece, other, F32 if n == "rk" else BF16)
                 for n, piece, other in zip(late_names, late_pieces, from_sibling)]
    pairs = dict(zip(FFN_SHARDED, [whole(p) for p in ffn_pairs]))

    mod_rows = N_MOD * D_MODEL // PACK_COLS
    dmod_all = small_all[:, :mod_rows].reshape(8, N_MOD * D_MODEL).astype(F32)
    dmod = lax.dynamic_slice(dmod_all, (0, chip * n_mod_cols), (8, n_mod_cols))
    ada_res, arrived = _ada_w_update(act.T, dmod, ada_w[0], m_ada_w[0], v_ada_w[0],
                                     hosted=_exchange_plan(late_sums, CHIP_PEERS, N_CHIPS, scatter=True))
    late_halves = [_sum_parts("sum_" + n, a) for n, a in zip(late_names, arrived)]

    out = {"ada_w": [r[None] for r in ada_res]}
    first = FFN_SHARDED[0]
    res, swapped = _adamw("adamw_" + first, w[first][0], pairs[first], m[first][0], v[first][0],
                          hosted=_exchange_plan(late_halves, CORE_PEER, 2))
    out[first] = [r[None] for r in res]
    pairs.update(zip(late_names, [whole(p) for p in swapped]))
    for n in [FFN_SHARDED[1]] + late:
        out[n] = [r[None] for r in _adamw("adamw_" + n, w[n][0], pairs[n], m[n][0], v[n][0])]
    rk_res = _adamw("adamw_rk", rk_rows(w), pairs["rk"], rk_rows(m), rk_rows(v))
    for q in range(4):
        pieces, pos = [], 0
        for n in RK_SHARDED:
            rows = w[n].size // 256
            pieces.append(rk_res[q][pos:pos + rows].reshape(w[n].shape))
            pos += rows
        for n, piece in zip(RK_SHARDED, pieces):
            out.setdefault(n, []).append(piece)

    small_res = _adamw("adamw_small", _pack_flat([w[n] for n in REPLICATED]), small_all,
                       _pack_flat([m[n] for n in REPLICATED]), _pack_flat([v[n] for n in REPLICATED]))
    for q in range(4):
        for n, piece in zip(REPLICATED, _unpack_flat(small_res[q], [w[n] for n in REPLICATED])):
            out.setdefault(n, []).append(piece)

    return (loss, grad_x[None], *[out[n][0] for n in WEIGHTS], *[out[n][1] for n in WEIGHTS],
            *[out[n][2] for n in WEIGHTS], *[out[n][3] for n in WEIGHTS])
```

```python
import functools
import math

import jax
import jax.numpy as jnp
from jax import lax
from jax.experimental import pallas as pl
from jax.experimental.pallas import tpu as pltpu

F32 = jnp.float32
BF16 = jnp.bfloat16

D_MODEL = 2048
S5_WIDTH = 1024
S5_GROUP = 16
S5_GROUPS = 64
S5_STATE = 64
S5_CH = S5_GROUPS * S5_STATE
S5_BLK = 256
RK_WIDTH = 1024
RK_HEAD = 64
RK_HEADS = 16
LORA = 64
GATE_LORA = 160
GATE_PAD = 256
RK_IN = 3488
RK_PAD = 3584
PROJ = 4512
PROJ_PAD = 4608
FFN = 8192
N_MOD = 6
NORM_EPS = 1e-6
GN_EPS = 64e-5
L2_EPS = 1e-12
RK_CHUNK = 64
RK_PASSES = {"solve": 3, "kt": 3, "s0": 1, "akk_v": 1, "ark_v": 1, "arb_u": 1, "state": 3}
LW_SCALE = math.exp(-0.5)
ADAM_LR, ADAM_B1, ADAM_B2, ADAM_EPS, ADAM_WD, ADAM_STEP = 0.001, 0.9, 0.999, 1e-08, 0.01, 10
VMEM_LIMIT = 56 * 1024 * 1024
HI = lax.Precision.HIGHEST


def _params(sem=None):
    return pltpu.CompilerParams(dimension_semantics=sem, vmem_limit_bytes=VMEM_LIMIT)


def _full(a):
    nd = a.ndim
    return pl.BlockSpec(a.shape, lambda *_: (0,) * nd)


@jax.custom_vjp
def _bdot(a, b):
    return jnp.dot(a.astype(BF16), b.astype(BF16), preferred_element_type=F32)


def _bdot_fwd(a, b):
    return _bdot(a, b), (a, b)


def _bdot_bwd(res, g):
    a, b = res
    gb = g.astype(BF16)
    da = lax.dot_general(gb, b.astype(BF16), (((1,), (1,)), ((), ())), preferred_element_type=F32)
    db = lax.dot_general(a.astype(BF16), gb, (((0,), (0,)), ((), ())), preferred_element_type=F32)
    return da, db


_bdot.defvjp(_bdot_fwd, _bdot_bwd)


@jax.custom_vjp
def _seg_dot(x, ind, ind_t):
    hi = x.astype(BF16)
    lo = (x - hi.astype(F32)).astype(BF16)
    both = jnp.dot(jnp.concatenate([hi, lo], axis=0), ind.astype(BF16), preferred_element_type=F32)
    return both[:x.shape[0]] + both[x.shape[0]:]


_seg_dot.defvjp(lambda x, ind, ind_t: (_seg_dot(x, ind, ind_t), (ind, ind_t)),
                lambda res, g: (_seg_dot(g, res[1], res[0]), jnp.zeros_like(res[0]), jnp.zeros_like(res[1])))


def _sigmoid(z):
    return 1.0 / (1.0 + jnp.exp(-z))


def _gelu(y):
    return 0.5 * y * (1.0 + jnp.tanh(0.7978845608028654 * (y + 0.044715 * (y * y * y))))


def _rms(x):
    return x * lax.rsqrt(jnp.mean(x * x, axis=-1, keepdims=True) + NORM_EPS)


def _tile(n, prefs):
    for t in prefs:
        if n % t == 0:
            return t
    return n


def _matmul(name, a, b, ta=False, tb=False, epilogue=None, extras=(), out_dtypes=(F32,), chips=None, hosted=None):
    m = a.shape[1] if ta else a.shape[0]
    k = a.shape[0] if ta else a.shape[1]
    if chips == "b":
        assert not tb and b.shape[1] == k
        n = N_CHIPS * b.shape[2]
    elif chips == "b_t":
        assert tb and N_CHIPS * b.shape[2] == k
        n = b.shape[1]
    else:
        n = b.shape[0] if tb else b.shape[1]
        assert k == (b.shape[1] if tb else b.shape[0]), (a.shape, b.shape, ta, tb)
    split = N_CHIPS if chips in ("b", "out") else 1
    tm = _tile(m, (1024, 512, 256, 128))
    tn = _tile(n // split, (1024, 768, 512, 256, 128))
    tk = k // N_CHIPS if chips == "b_t" else _tile(k, (2048, 1024, 512, 256, 128))
    nk = k // tk
    per = n // split // tn
    n_ex, n_out = len(extras), len(out_dtypes)
    dims = (((0 if ta else 1,), (1 if tb else 0,)), ((), ()))

    hosted = hosted or _NO_PLAN
    nh = len(hosted.arrays)
    grid = (m // tm, split, per, nk)

    def body(a_ref, b_ref, *rest):
        ex_refs, host_in = rest[:n_ex], rest[n_ex:n_ex + nh]
        out_refs, host_out = rest[n_ex + nh:n_ex + nh + n_out], rest[n_ex + nh + n_out:n_ex + 2 * nh + n_out]
        acc, sems = rest[n_ex + 2 * nh + n_out], rest[n_ex + 2 * nh + n_out + 1:]
        kk = pl.program_id(3)
        if nh:
            ids = [pl.program_id(d) for d in range(4)]
            first = functools.reduce(jnp.logical_and, [i == 0 for i in ids])
            last = functools.reduce(jnp.logical_and, [i == g - 1 for i, g in zip(ids, grid)])

            @pl.when(first)
            def _():
                hosted.start(host_in, host_out, sems)

        @pl.when(kk == 0)
        def _():
            acc[...] = jnp.zeros_like(acc)

        acc[...] += lax.dot_general(a_ref[...].astype(BF16), b_ref[...].astype(BF16), dims,
                                    preferred_element_type=F32)

        @pl.when(kk == nk - 1)
        def _():
            res = acc[...]
            outs = epilogue(res, *[e[...] for e in ex_refs]) if epilogue is not None else (res,)
            for o_ref, val in zip(out_refs, outs):
                o_ref[...] = val.astype(o_ref.dtype)

        if nh:
            @pl.when(last)
            def _():
                hosted.wait(host_in, host_out, sems)

    if ta:
        a_spec = pl.BlockSpec((tk, tm), lambda i, c, j, q: (q, i))
    else:
        a_spec = pl.BlockSpec((tm, tk), lambda i, c, j, q: (i, q))
    if chips == "b":
        b_spec = pl.BlockSpec((None, tk, tn), lambda i, c, j, q: (c, q, j))
    elif chips == "b_t":
        b_spec = pl.BlockSpec((None, tn, tk), lambda i, c, j, q: (q, j, 0))
    elif tb:
        b_spec = pl.BlockSpec((tn, tk), lambda i, c, j, q: (c * per + j, q))
    else:
        b_spec = pl.BlockSpec((tk, tn), lambda i, c, j, q: (q, c * per + j))
    mn_spec = pl.BlockSpec((tm, tn), lambda i, c, j, q: (i, c * per + j))
    if chips == "out":
        out_spec = pl.BlockSpec((None, tm, tn), lambda i, c, j, q: (c, i, j))
        out_shape = [jax.ShapeDtypeStruct((N_CHIPS, m, n // N_CHIPS), dt) for dt in out_dtypes]
    else:
        out_spec, out_shape = mn_spec, [jax.ShapeDtypeStruct((m, n), dt) for dt in out_dtypes]
    any_spec = pl.BlockSpec(memory_space=pl.ANY)
    order = ("arbitrary",) * 4 if nh else ("parallel", "parallel", "parallel", "arbitrary")
    outs = pl.pallas_call(
        body, name=name, grid=grid,
        in_specs=[a_spec, b_spec] + [mn_spec] * n_ex + [any_spec] * nh,
        out_specs=[out_spec] * n_out + [any_spec] * nh, out_shape=out_shape + hosted.out_shape,
        scratch_shapes=[pltpu.VMEM((tm, tn), F32)] + hosted.sems,
        compiler_params=_params(order),
    )(a, b, *extras, *hosted.arrays)
    res = outs[0] if n_out == 1 else outs[:n_out]
    return (res, hosted.finish(outs[n_out:])) if nh else res


def _row_spec(a, tm):
    return pl.BlockSpec((tm, a.shape[1]), lambda i: (i, 0))


def _rowwise(name, fn, rows, params, outs, tm):
    t = rows[0].shape[0]
    tm = min(tm, t)
    n_r, n_p = len(rows), len(params)

    def body(*refs):
        vals = [r[...] for r in refs[:n_r + n_p]]
        res = fn(*vals)
        for o_ref, val in zip(refs[n_r + n_p:], res):
            o_ref[...] = val.astype(o_ref.dtype)

    res = pl.pallas_call(
        body, name=name, grid=(t // tm,),
        in_specs=[_row_spec(r, tm) for r in rows] + [_full(p) for p in params],
        out_specs=[pl.BlockSpec((tm, n), lambda i: (i, 0)) for n, _ in outs],
        out_shape=[jax.ShapeDtypeStruct((t, n), dt) for n, dt in outs],
        compiler_params=_params(("parallel",)),
    )(*rows, *params)
    return res


def _rowwise_vjp(name, fn, rows, params, cts, row_grads, param_grads, tm, consts=(), addends=None,
                 emit=(), row_grad_dtypes=None):
    t = rows[0].shape[0]
    tm = min(tm, t)
    addends = addends or {}
    n_r, n_p, n_c = len(rows), len(params), len(consts)
    ct_flat = [c for group in cts for c in group]
    add_list = [addends[q] for q in sorted(addends)]
    n_ct, n_add = len(ct_flat), len(add_list)
    row_grad_dtypes = row_grad_dtypes or [F32] * len(row_grads)

    def body(*refs):
        pos = 0
        row_v = [r[...].astype(F32) for r in refs[pos:pos + n_r]]; pos += n_r
        par_v = [r[...].astype(F32) for r in refs[pos:pos + n_p]]; pos += n_p
        con_v = [r[...] for r in refs[pos:pos + n_c]]; pos += n_c
        ct_v = [r[...].astype(F32) for r in refs[pos:pos + n_ct]]; pos += n_ct
        add_v = [r[...] for r in refs[pos:pos + n_add]]; pos += n_add
        emit_refs = refs[pos:pos + len(emit)]; pos += len(emit)
        rg_refs = refs[pos:pos + len(row_grads)]; pos += len(row_grads)
        pg_refs = refs[pos:pos + len(param_grads)]

        def diff_fn(*dargs):
            rv, pv = list(row_v), list(par_v)
            for q, i in enumerate(row_grads):
                rv[i] = dargs[q]
            for q, j in enumerate(param_grads):
                pv[j] = dargs[len(row_grads) + q]
            return fn(*rv, *pv, *con_v)

        prim = [row_v[i] for i in row_grads] + [par_v[j] for j in param_grads]
        res, vjp = jax.vjp(diff_fn, *prim)
        ct_vals, q = [], 0
        for o, group in zip(res, cts):
            tot = jnp.zeros_like(o)
            for _ in group:
                tot = tot + ct_v[q]
                q += 1
            ct_vals.append(tot)
        grads = vjp(tuple(ct_vals))
        for e_ref, idx in zip(emit_refs, emit):
            e_ref[...] = res[idx].astype(e_ref.dtype)
        add_pos = {p: q for q, p in enumerate(sorted(addends))}
        for q, g_ref in enumerate(rg_refs):
            g = grads[q]
            if q in add_pos:
                g = g + add_v[add_pos[q]]
            g_ref[...] = g.astype(g_ref.dtype)

        @pl.when(pl.program_id(0) == 0)
        def _():
            for g_ref in pg_refs:
                g_ref[...] = jnp.zeros_like(g_ref)

        for q, g_ref in enumerate(pg_refs):
            g_ref[...] += grads[len(row_grads) + q]

    emit_shapes = []
    if emit:
        probe = jax.eval_shape(lambda *a: fn(*a), *[jax.ShapeDtypeStruct((tm, r.shape[1]), F32) for r in rows],
                               *[jax.ShapeDtypeStruct(p.shape, p.dtype) for p in params],
                               *[jax.ShapeDtypeStruct(c.shape, c.dtype) for c in consts])
        emit_shapes = [probe[idx].shape[1] for idx in emit]
    out_specs = ([pl.BlockSpec((tm, n), lambda i: (i, 0)) for n in emit_shapes]
                 + [_row_spec(rows[i], tm) for i in row_grads]
                 + [_full(params[j]) for j in param_grads])
    out_shape = ([jax.ShapeDtypeStruct((t, n), F32) for n in emit_shapes]
                 + [jax.ShapeDtypeStruct(rows[i].shape, dt) for i, dt in zip(row_grads, row_grad_dtypes)]
                 + [jax.ShapeDtypeStruct(params[j].shape, F32) for j in param_grads])
    return pl.pallas_call(
        body, name=name, grid=(t // tm,),
        in_specs=([_row_spec(r, tm) for r in rows] + [_full(p) for p in params] + [_full(c) for c in consts]
                  + [_row_spec(c, tm) for c in ct_flat] + [_row_spec(a, tm) for a in add_list]),
        out_specs=out_specs, out_shape=out_shape,
        compiler_params=_params(("arbitrary",)),
    )(*rows, *params, *consts, *ct_flat, *add_list)


def _norm_mod_fn(x, gain, scale, shift):
    return (_rms(x) * gain * (1.0 + scale) + shift,)


def _resid_norm_mod_fn(x, mixed, gate, gain, scale, shift):
    x1 = x + gate * mixed
    return x1, _rms(x1) * gain * (1.0 + scale) + shift


def _loss_fn(x1, ffn, target, gate, gain):
    y = _rms(x1 + gate * ffn) * gain
    err = y - target
    return (0.5 * jnp.mean(err * err, axis=-1, keepdims=True),)


def _s5_out_fn(ylin, u, d_skip, w_glu, b_glu):
    z = _gelu(ylin + d_skip * u)
    return (z * _sigmoid(_bdot(z, w_glu) + b_glu),)


def _rk_pre_fn(k, wdn, adn, gdn, w0_0, w0_1, wup_0, wup_1, a0_0, a0_1, aup_0, aup_1, g_up, k_k, k_a, seg, seg_t):
    kkr = k * k_k
    inv = 1.0 / jnp.sqrt(jnp.maximum(_seg_dot(kkr * kkr, seg, seg_t), L2_EPS * L2_EPS))
    kk = kkr * _seg_dot(inv, seg_t, seg)
    tw = jnp.tanh(wdn)
    lws, kds, acts = [], [], []
    for w0, wup, a0, aup in ((w0_0, wup_0, a0_0, aup_0), (w0_1, wup_1, a0_1, aup_1)):
        lws.append(-LW_SCALE * _sigmoid(w0 + _bdot(tw, wup)))
        act = _sigmoid(a0 + _bdot(adn, aup))
        acts.append(act)
        kds.append(k * (1.0 + (act - 1.0) * k_a))
    gate = _bdot(_sigmoid(gdn), g_up)
    return (kk, lws[0], lws[1], kds[0], kds[1], acts[0], acts[1], gate)


def _rk_post_fn(y0, y1, r, v, kd0, kd1, gate, ln_gain, ln_bias, r_k, seg, seg_t):
    y = y0 + y1
    mu = _seg_dot(_seg_dot(y, seg, seg_t) * (1.0 / RK_HEAD), seg_t, seg)
    yc = y - mu
    var = _seg_dot(yc * yc, seg, seg_t) * (1.0 / RK_HEAD)
    yn = yc * _seg_dot(lax.rsqrt(var + GN_EPS), seg_t, seg) * ln_gain + ln_bias
    bonus = _seg_dot(_seg_dot(r * (kd0 + kd1) * r_k, seg, seg_t), seg_t, seg)
    return ((yn + bonus * v) * gate,)


def _s5_prep_fn(lr0, li0, ls0, lr1, li1, ls1, b_re, b_im):
    outs = []
    for lam_re, lam_im, ls in ((lr0, li0, ls0), (lr1, li1, ls1)):
        step = jnp.exp(ls)
        mag = jnp.exp(lam_re * step)
        lbar_re = mag * jnp.cos(lam_im * step)
        lbar_im = mag * jnp.sin(lam_im * step)
        den = lam_re * lam_re + lam_im * lam_im
        nr = lbar_re - 1.0
        coef_re = (nr * lam_re + lbar_im * lam_im) / den
        coef_im = (lbar_im * lam_re - nr * lam_im) / den
        outs += [lbar_re, lbar_im, coef_re * b_re - coef_im * b_im, coef_re * b_im + coef_im * b_re]
    return tuple(outs)


def _shift_rows(x, down):
    t = x.shape[0]
    rows = lax.broadcasted_iota(jnp.int32, x.shape, 0)
    if down:
        return jnp.where(rows >= 1, pltpu.roll(x, 1, 0), 0.0)
    return jnp.where(rows < t - 1, pltpu.roll(x, t - 1, 0), 0.0)


def _token_shift(src, mu_prev, mu_next, first):
    t, n = src.shape[0], mu_prev.shape[1]

    def body(p_ref, mp_ref, mn_ref, o_ref):
        x = p_ref[...]
        o_ref[...] = x + mp_ref[...] * (_shift_rows(x, True) - x) + mn_ref[...] * (_shift_rows(x, False) - x)

    col = pl.BlockSpec((t, 128), lambda j: (0, j))
    par = pl.BlockSpec((1, 128), lambda j: (0, j))
    return pl.pallas_call(
        body, name="token_shift", grid=(n // 128,),
        in_specs=[pl.BlockSpec((t, 128), lambda j: (0, j + first)), par, par], out_specs=col,
        out_shape=jax.ShapeDtypeStruct((t, n), F32), compiler_params=_params(("parallel",)),
    )(src, mu_prev, mu_next)


def _token_shift_bwd(src, mu_prev, mu_next, dps, first):
    t, n = dps.shape

    def body(p_ref, mp_ref, mn_ref, d_ref, dp_ref, dmp_ref, dmn_ref):
        x, d, mp, mn = p_ref[...], d_ref[...], mp_ref[...], mn_ref[...]
        dp_ref[...] = d * (1.0 - mp - mn) + _shift_rows(d * mp, False) + _shift_rows(d * mn, True)
        dmp_ref[...] = jnp.sum(d * (_shift_rows(x, True) - x), axis=0, keepdims=True)
        dmn_ref[...] = jnp.sum(d * (_shift_rows(x, False) - x), axis=0, keepdims=True)

    col = pl.BlockSpec((t, 128), lambda j: (0, j))
    par = pl.BlockSpec((1, 128), lambda j: (0, j))
    return pl.pallas_call(
        body, name="token_shift_bwd", grid=(n // 128,),
        in_specs=[pl.BlockSpec((t, 128), lambda j: (0, j + first)), par, par, col],
        out_specs=[col, par, par],
        out_shape=[jax.ShapeDtypeStruct((t, n), F32), jax.ShapeDtypeStruct((1, n), F32),
                   jax.ShapeDtypeStruct((1, n), F32)],
        compiler_params=_params(("parallel",)),
    )(src, mu_prev, mu_next, dps)


N_SEG = 32
S5_BLOCKS = 32
S5_PER_IN = 4


def _scan_in_place(sr_ref, si_ref, ar, ai, carry_ref, reverse):
    seg_len = sr_ref.shape[0] // N_SEG
    ng = N_SEG // 8

    def rows(i, grp):
        first = (seg_len - 1 - i if reverse else i) * N_SEG + 8 * grp
        return pl.ds(pl.multiple_of(first, 8), 8)

    zero = jnp.zeros((8, 128), F32)
    one = jnp.ones((8, 128), F32)

    def local(i, c):
        pr, pi = c[-2:]
        out = []
        for grp in range(ng):
            sr, si = c[2 * grp], c[2 * grp + 1]
            nr = ar * sr - ai * si + sr_ref[rows(i, grp), :]
            ni = ar * si + ai * sr + si_ref[rows(i, grp), :]
            sr_ref[rows(i, grp), :] = nr
            si_ref[rows(i, grp), :] = ni
            out += [nr, ni]
        return tuple(out) + (ar * pr - ai * pi, ar * pi + ai * pr)

    ends = lax.fori_loop(0, seg_len, local, (zero,) * (2 * ng) + (one, zero))
    qr, qi = ends[-2][0:1], ends[-1][0:1]
    order = list(range(N_SEG - 1, -1, -1)) if reverse else list(range(N_SEG))
    cr = jnp.zeros((1, 128), F32)
    ci = jnp.zeros((1, 128), F32)
    for j in order:
        carry_ref[j:j + 1, :] = cr
        carry_ref[N_SEG + j:N_SEG + j + 1, :] = ci
        grp, sub = divmod(j, 8)
        lr, li = ends[2 * grp][sub:sub + 1], ends[2 * grp + 1][sub:sub + 1]
        cr, ci = lr + qr * cr - qi * ci, li + qr * ci + qi * cr
    carries = [(carry_ref[8 * grp:8 * grp + 8, :], carry_ref[N_SEG + 8 * grp:N_SEG + 8 * grp + 8, :])
               for grp in range(ng)]

    def fix(i, c):
        pr, pi = c
        npr, npi = ar * pr - ai * pi, ar * pi + ai * pr
        for grp in range(ng):
            cr8, ci8 = carries[grp]
            sr_ref[rows(i, grp), :] = sr_ref[rows(i, grp), :] + npr * cr8 - npi * ci8
            si_ref[rows(i, grp), :] = si_ref[rows(i, grp), :] + npr * ci8 + npi * cr8
        return npr, npi

    lax.fori_loop(0, seg_len, fix, (one, zero))


def _interleave(x):
    t, c = x.shape
    return jnp.transpose(x.reshape(N_SEG, t // N_SEG, c), (1, 0, 2)).reshape(t, c)


def _deinterleave(x):
    t, c = x.shape
    return jnp.transpose(x.reshape(t // N_SEG, N_SEG, c), (1, 0, 2)).reshape(t, c)


def _lag_sums(lr_ref, li_ref, sr_ref, si_ref, earlier):
    t = lr_ref.shape[0]
    body, edge = pl.ds(N_SEG, t - N_SEG), pl.ds(0, N_SEG)
    far = pl.ds(t - N_SEG, N_SEG)
    rows = lax.broadcasted_iota(jnp.int32, (N_SEG, 128), 0)
    if earlier:
        lam_main, s_main, lam_edge = body, pl.ds(0, t - N_SEG), edge
        wrap = lambda ref: jnp.where(rows >= 1, pltpu.roll(ref[far, :], 1, 0), 0.0)
    else:
        lam_main, s_main, lam_edge = pl.ds(0, t - N_SEG), body, far
        wrap = lambda ref: jnp.where(rows < N_SEG - 1, pltpu.roll(ref[edge, :], N_SEG - 1, 0), 0.0)
    lr, li, sr, si = lr_ref[lam_main, :], li_ref[lam_main, :], sr_ref[s_main, :], si_ref[s_main, :]
    er, ei, pr, pi = lr_ref[lam_edge, :], li_ref[lam_edge, :], wrap(sr_ref), wrap(si_ref)
    re = jnp.sum(lr * sr + li * si, axis=0, keepdims=True) + jnp.sum(er * pr + ei * pi, axis=0, keepdims=True)
    im = jnp.sum(li * sr - lr * si, axis=0, keepdims=True) + jnp.sum(ei * pr - er * pi, axis=0, keepdims=True)
    return re, im


def _dot_bf16(a, b, dims=(((1,), (0,)), ((), ()))):
    return lax.dot_general(a.astype(BF16), b.astype(BF16), dims, preferred_element_type=F32)


NT_DIMS = (((1,), (1,)), ((), ()))
TN_DIMS = (((0,), (0,)), ((), ()))


def _s5_specs(t):
    blk = pl.BlockSpec((None, t, 128), lambda i, q: (S5_PER_IN * i + q, 0, 0))
    mat = pl.BlockSpec((None, 128, 128), lambda i, q: (S5_PER_IN * i + q, 0, 0))
    vec = pl.BlockSpec((None, 1, 128), lambda i, q: (S5_PER_IN * i + q, 0, 0))
    chan = pl.BlockSpec((t, 128), lambda i, q: (0, i))
    return blk, mat, vec, chan


S5_GRID = (S5_BLOCKS // S5_PER_IN, S5_PER_IN)


def _s5_forward(name, u, b_re, b_im, l_re, l_im, reverse, other=None, c_re=None, c_im_neg=None):
    t = u.shape[0]
    project = other is not None
    blk, mat, vec, chan = _s5_specs(t)

    def body(*refs):
        u_ref, br_ref, bi_ref, lr_ref, li_ref = refs[:5]
        if project:
            or_ref, oi_ref, cr_ref, ci_ref, sr_ref, si_ref, y_ref, carry_ref = refs[5:]
        else:
            sr_ref, si_ref, carry_ref = refs[5:]
        uv = u_ref[...]
        sr_ref[...] = _dot_bf16(uv, br_ref[...])
        si_ref[...] = _dot_bf16(uv, bi_ref[...])
        ar = jnp.broadcast_to(lr_ref[...], (8, 128))
        ai = jnp.broadcast_to(li_ref[...], (8, 128))
        _scan_in_place(sr_ref, si_ref, ar, ai, carry_ref, reverse)
        if project:
            y = (_dot_bf16(sr_ref[...] + or_ref[...], cr_ref[...])
                 + _dot_bf16(si_ref[...] + oi_ref[...], ci_ref[...]))

            @pl.when(pl.program_id(1) == 0)
            def _():
                y_ref[...] = y

            @pl.when(pl.program_id(1) != 0)
            def _():
                y_ref[...] += y

    state = jax.ShapeDtypeStruct((S5_BLOCKS, t, 128), F32)
    ins = [u, b_re, b_im, l_re, l_im] + ([other[0], other[1], c_re, c_im_neg] if project else [])
    in_specs = [chan, mat, mat, vec, vec] + ([blk, blk, mat, mat] if project else [])
    return pl.pallas_call(
        body, name=name, grid=S5_GRID, in_specs=in_specs,
        out_specs=[blk, blk] + ([chan] if project else []),
        out_shape=[state, state] + ([jax.ShapeDtypeStruct((t, S5_WIDTH), F32)] if project else []),
        scratch_shapes=[pltpu.VMEM((2 * N_SEG, 128), F32)],
        compiler_params=_params(("arbitrary", "arbitrary")),
    )(*ins)


def _s5_backward(name, dy, u, du_in, states, other, b_re, b_im, c_re, c_im_neg, l_re, l_im, reverse):
    t = u.shape[0]
    with_c = other is not None
    blk, mat, vec, chan = _s5_specs(t)

    def body(*refs):
        dy_ref, u_ref, du_in_ref, sr_ref, si_ref = refs[:5]
        pos = 5
        if with_c:
            or_ref, oi_ref = refs[5:7]
            pos = 7
        br_ref, bi_ref, cr_ref, ci_ref, lr_ref, li_ref = refs[pos:pos + 6]
        outs = refs[pos + 6:]
        du_ref, dbr_ref, dbi_ref, dlr_ref, dli_ref = outs[:5]
        lam_r, lam_i, carry_ref = outs[-3:]
        dyv, uv = dy_ref[...], u_ref[...]
        lam_r[...] = _dot_bf16(dyv, cr_ref[...], NT_DIMS)
        lam_i[...] = _dot_bf16(dyv, ci_ref[...], NT_DIMS)
        ar = jnp.broadcast_to(lr_ref[...], (8, 128))
        ai = -jnp.broadcast_to(li_ref[...], (8, 128))
        _scan_in_place(lam_r, lam_i, ar, ai, carry_ref, not reverse)
        lr, li = lam_r[...], lam_i[...]
        dlr_ref[...], dli_ref[...] = _lag_sums(lam_r, lam_i, sr_ref, si_ref, not reverse)
        dbr_ref[...] = _dot_bf16(uv, lr, TN_DIMS)
        dbi_ref[...] = _dot_bf16(uv, li, TN_DIMS)
        du = _dot_bf16(lr, br_ref[...], NT_DIMS) + _dot_bf16(li, bi_ref[...], NT_DIMS)

        @pl.when(pl.program_id(1) == 0)
        def _():
            du_ref[...] = du_in_ref[...] + du

        @pl.when(pl.program_id(1) != 0)
        def _():
            du_ref[...] += du

        if with_c:
            dcr_ref, dci_ref = outs[5:7]
            dcr_ref[...] = _dot_bf16(sr_ref[...] + or_ref[...], dyv, TN_DIMS)
            dci_ref[...] = _dot_bf16(si_ref[...] + oi_ref[...], dyv, TN_DIMS)

    mats = jax.ShapeDtypeStruct((S5_BLOCKS, 128, 128), F32)
    vecs = jax.ShapeDtypeStruct((S5_BLOCKS, 1, 128), F32)
    ins = [dy, u, du_in, states[0], states[1]] + ([other[0], other[1]] if with_c else [])
    ins += [b_re, b_im, c_re, c_im_neg, l_re, l_im]
    in_specs = [chan, chan, chan, blk, blk] + ([blk, blk] if with_c else []) + [mat] * 4 + [vec] * 2
    return pl.pallas_call(
        body, name=name, grid=S5_GRID, in_specs=in_specs,
        out_specs=[chan, mat, mat, vec, vec] + ([mat, mat] if with_c else []),
        out_shape=[jax.ShapeDtypeStruct((t, S5_WIDTH), F32), mats, mats, vecs, vecs] + ([mats, mats] if with_c else []),
        scratch_shapes=[pltpu.VMEM((t, 128), F32), pltpu.VMEM((t, 128), F32), pltpu.VMEM((2 * N_SEG, 128), F32)],
        compiler_params=_params(("arbitrary", "arbitrary")),
    )(*ins)


def _ein(passes, spec, a, b):
    if passes == 6:
        return jnp.einsum(spec, a, b, precision=HI, preferred_element_type=F32)
    a_hi, b_hi = a.astype(BF16), b.astype(BF16)
    if passes == 1:
        return jnp.einsum(spec, a_hi, b_hi, preferred_element_type=F32)
    a_lo = (a - a_hi.astype(F32)).astype(BF16)
    b_lo = (b - b_hi.astype(F32)).astype(BF16)
    cross = jnp.einsum(spec, a_hi, b_lo, preferred_element_type=F32)
    if spec.startswith('hik'):
        m = a.shape[1]
        stacked = jnp.einsum(spec, jnp.concatenate([a_hi, a_lo], axis=1), b_hi, preferred_element_type=F32)
        return stacked[:, :m] + stacked[:, m:] + cross
    return (jnp.einsum(spec, a_hi, b_hi, preferred_element_type=F32) + cross
            + jnp.einsum(spec, a_lo, b_hi, preferred_element_type=F32))


@jax.custom_vjp
def _tri_mm(tri, tri_t, z):
    n = z.shape[2]
    hi = z.astype(BF16)
    rest = z - hi.astype(F32)
    mid = rest.astype(BF16)
    lo = (rest - mid.astype(F32)).astype(BF16)
    out = jnp.einsum('hik,hkj->hij', tri.astype(BF16), jnp.concatenate([hi, mid, lo], axis=2),
                     preferred_element_type=F32)
    return out[:, :, :n] + out[:, :, n:2 * n] + out[:, :, 2 * n:]


def _tri_mm_bwd(res, g):
    tri, tri_t = res
    return jnp.zeros_like(tri), jnp.zeros_like(tri_t), _tri_mm(tri_t, tri, g)


_tri_mm.defvjp(lambda tri, tri_t, z: (_tri_mm(tri, tri_t, z), (tri, tri_t)), _tri_mm_bwd)


def _chunk_cumsum(lw, incl, incl_t):
    shape = (lw.shape[0],) + incl.shape
    return _tri_mm(jnp.broadcast_to(incl.astype(F32), shape), jnp.broadcast_to(incl_t.astype(F32), shape), lw)


@functools.partial(jax.custom_vjp, nondiff_argnums=(0,))
def _bmm(p, a, b):
    return _ein(p, 'hik,hkj->hij', a, b)


@functools.partial(jax.custom_vjp, nondiff_argnums=(0,))
def _bmm_nt(p, a, b):
    return _ein(p, 'hik,hjk->hij', a, b)


@functools.partial(jax.custom_vjp, nondiff_argnums=(0,))
def _bmm_tn(p, a, b):
    return _ein(p, 'hki,hkj->hij', a, b)


_bmm.defvjp(lambda p, a, b: (_bmm(p, a, b), (a, b)),
            lambda p, res, g: (_bmm_nt(p, g, res[1]), _bmm_tn(p, res[0], g)))
_bmm_nt.defvjp(lambda p, a, b: (_bmm_nt(p, a, b), (a, b)),
               lambda p, res, g: (_bmm(p, g, res[1]), _bmm_tn(p, g, res[0])))
_bmm_tn.defvjp(lambda p, a, b: (_bmm_tn(p, a, b), (a, b)),
               lambda p, res, g: (_bmm_nt(p, res[1], g), _bmm(p, res[0], g)))


@jax.custom_vjp
def _split_rows(x):
    c = x.shape[1] // 2
    return x[:, :c], x[:, c:]


_split_rows.defvjp(lambda x: (_split_rows(x), None), lambda _, g: (jnp.concatenate(g, axis=1),))


def _stack_rows(a, b):
    return jnp.concatenate([a, b], axis=1)


def _nilpotent_inverse(l_mat):
    c = l_mat.shape[1]
    ps = RK_PASSES["solve"]
    row = lax.broadcasted_iota(jnp.int32, (c, c), 0)
    col = lax.broadcasted_iota(jnp.int32, (c, c), 1)
    x = -l_mat
    inv = jnp.where(row == col, 1.0, 0.0) + x
    power = _bmm(ps, x, x)
    span = 2
    while 2 * span < c:
        step, power = _split_rows(_bmm(ps, _stack_rows(inv, power), power))
        inv = inv + step
        span *= 2
    return inv + _bmm(ps, inv, power)


@jax.custom_vjp
def _solve_with(inv, l_mat, rhs):
    return _bmm(RK_PASSES["solve"], inv, rhs)


def _solve_with_fwd(inv, l_mat, rhs):
    u = _bmm(RK_PASSES["solve"], inv, rhs)
    return u, (inv, u)


def _solve_with_bwd(res, g):
    inv, u = res
    d_rhs = _bmm_tn(RK_PASSES["solve"], inv, g)
    return jnp.zeros_like(inv), -_bmm_nt(RK_PASSES["solve"], d_rhs, u), d_rhs


_solve_with.defvjp(_solve_with_fwd, _solve_with_bwd)


def _rk_chunk(s0, r, lw, k, v, kk, a, reverse, inv=None):
    h, c, n = r.shape
    row = lax.broadcasted_iota(jnp.int32, (c, c), 0)
    col = lax.broadcasted_iota(jnp.int32, (c, c), 1)
    incl = (row <= col) if reverse else (row >= col)
    strict = (row < col) if reverse else (row > col)
    cum = _chunk_cumsum(lw, incl, (row >= col) if reverse else (row <= col))
    g_in = jnp.exp(cum)
    g_inv = jnp.exp(-cum)
    kap = kk * jnp.exp(cum - lw)
    beta = kk * a * g_inv
    kt = k * g_inv
    rt = r * g_in
    p, ps = RK_PASSES, RK_PASSES["solve"]
    both = _stack_rows(kap, rt)
    kap_beta, rt_beta = _split_rows(_bmm_nt(ps, both, beta))
    kap_kt, rt_kt = _split_rows(_bmm_nt(p["kt"], both, kt))
    kap_s0, rt_s0 = _split_rows(_bmm_nt(p["s0"], both, s0))
    l_mat = jnp.where(strict, kap_beta, 0.0)
    rhs = kap_s0 + _bmm(p["akk_v"], jnp.where(strict, kap_kt, 0.0), v)
    if inv is None:
        inv = lax.stop_gradient(_nilpotent_inverse(l_mat))
    u = _solve_with(inv, l_mat, rhs)
    y = (rt_s0 + _bmm(p["ark_v"], jnp.where(incl, rt_kt, 0.0), v)
         - _bmm(p["arb_u"], jnp.where(incl, rt_beta, 0.0), u))
    s1 = ((s0 + _bmm_tn(p["state"], _stack_rows(v, -u), _stack_rows(kt, beta)))
          * jnp.exp(jnp.sum(lw, axis=1, keepdims=True)))
    return y, s1, inv


class _Plan:
    def __init__(self, arrays, out_shape, sems, start, wait, finish):
        self.arrays, self.out_shape, self.sems = list(arrays), list(out_shape), list(sems)
        self.start, self.wait, self.finish = start, wait, finish


_NO_PLAN = _Plan([], [], [], lambda *_: None, lambda *_: None, lambda outs: [])


def _join_plans(plans):
    def cut(seq, sizes):
        out, pos = [], 0
        for s in sizes:
            out.append(seq[pos:pos + s])
            pos += s
        return out

    n_arr, n_sem = [len(p.arrays) for p in plans], [len(p.sems) for p in plans]

    def run(which):
        def go(in_refs, out_refs, sems):
            for p, i, o, s in zip(plans, cut(in_refs, n_arr), cut(out_refs, n_arr), cut(sems, n_sem)):
                getattr(p, which)(i, o, s)
        return go

    return _Plan([a for p in plans for a in p.arrays], [s for p in plans for s in p.out_shape],
                 [s for p in plans for s in p.sems], run("start"), run("wait"),
                 lambda outs: [p.finish(o) for p, o in zip(plans, cut(outs, n_arr))])


def _split_heads(x):
    return jnp.stack([x[:, RK_HEAD * i:RK_HEAD * (i + 1)] for i in range(RK_HEADS)], axis=0)


def _store_heads(ref, x):
    for i in range(RK_HEADS):
        ref[:, RK_HEAD * i:RK_HEAD * (i + 1)] = x[i]


def _rk_core_fwd(name, r, lw, k, v, kk, a, reverse, chunk, hosted=None):
    t = r.shape[0]
    h, n = RK_HEADS, RK_HEAD
    nc = t // chunk

    def idx(i):
        return nc - 1 - i if reverse else i

    hosted = hosted or _NO_PLAN
    nh = len(hosted.arrays)

    def body(r_ref, lw_ref, k_ref, v_ref, kk_ref, a_ref, *rest):
        host_in, (y_ref, ck_ref, inv_ref), host_out = rest[:nh], rest[nh:nh + 3], rest[nh + 3:2 * nh + 3]
        s_ref, sems = rest[2 * nh + 3], rest[2 * nh + 4:]

        @pl.when(pl.program_id(0) == 0)
        def _():
            s_ref[...] = jnp.zeros_like(s_ref)
            hosted.start(host_in, host_out, sems)

        s0 = s_ref[...]
        ck_ref[0] = s0
        ops = [_split_heads(ref[...]) for ref in (r_ref, lw_ref, k_ref, v_ref, kk_ref, a_ref)]
        y, s1, inv = _rk_chunk(s0, *ops, reverse)
        _store_heads(y_ref, y)
        inv_ref[0] = inv
        s_ref[...] = s1

        @pl.when(pl.program_id(0) == nc - 1)
        def _():
            hosted.wait(host_in, host_out, sems)

    blk = pl.BlockSpec((chunk, RK_WIDTH), lambda i: (idx(i), 0))
    any_spec = pl.BlockSpec(memory_space=pl.ANY)
    res = pl.pallas_call(
        body, name=name, grid=(nc,), in_specs=[blk] * 6 + [any_spec] * nh,
        out_specs=[blk, pl.BlockSpec((1, h, n, n), lambda i: (idx(i), 0, 0, 0)),
                   pl.BlockSpec((1, h, chunk, chunk), lambda i: (idx(i), 0, 0, 0))] + [any_spec] * nh,
        out_shape=[jax.ShapeDtypeStruct((t, RK_WIDTH), F32), jax.ShapeDtypeStruct((nc, h, n, n), F32),
                   jax.ShapeDtypeStruct((nc, h, chunk, chunk), F32)] + hosted.out_shape,
        scratch_shapes=[pltpu.VMEM((h, n, n), F32)] + hosted.sems,
        compiler_params=_params(("arbitrary",)),
    )(r, lw, k, v, kk, a, *hosted.arrays)
    return res[0], (res[1], res[2]), hosted.finish(res[3:])


def _rk_core_bwd(name, r, lw, k, v, kk, a, ck, dy, reverse, chunk, hosted=None):
    t = r.shape[0]
    h, n = RK_HEADS, RK_HEAD
    nc = t // chunk
    hosted = hosted or _NO_PLAN
    nh = len(hosted.arrays)

    def idx(i):
        return i if reverse else nc - 1 - i

    def body(r_ref, lw_ref, k_ref, v_ref, kk_ref, a_ref, ck_ref, inv_ref, dy_ref, *rest):
        host_in, out_refs, host_out = rest[:nh], rest[nh:nh + 6], rest[nh + 6:2 * nh + 6]
        ds_ref, sems = rest[2 * nh + 6], rest[2 * nh + 7:]

        @pl.when(pl.program_id(0) == 0)
        def _():
            ds_ref[...] = jnp.zeros_like(ds_ref)
            hosted.start(host_in, host_out, sems)

        inv = inv_ref[0]

        def fn(*operands):
            return _rk_chunk(*operands, reverse=reverse, inv=inv)[:2]

        ops = [_split_heads(ref[...]) for ref in (r_ref, lw_ref, k_ref, v_ref, kk_ref, a_ref)]
        _, vjp = jax.vjp(fn, ck_ref[0], *ops)
        grads = vjp((_split_heads(dy_ref[...]), ds_ref[...]))
        ds_ref[...] = grads[0]
        for o_ref, g in zip(out_refs, grads[1:]):
            _store_heads(o_ref, g)

        @pl.when(pl.program_id(0) == nc - 1)
        def _():
            hosted.wait(host_in, host_out, sems)

    blk = pl.BlockSpec((chunk, RK_WIDTH), lambda i: (idx(i), 0))
    any_spec = pl.BlockSpec(memory_space=pl.ANY)
    res = pl.pallas_call(
        body, name=name, grid=(nc,),
        in_specs=[blk] * 6 + [pl.BlockSpec((1, h, n, n), lambda i: (idx(i), 0, 0, 0)),
                              pl.BlockSpec((1, h, chunk, chunk), lambda i: (idx(i), 0, 0, 0)), blk]
        + [any_spec] * nh,
        out_specs=[blk] * 6 + [any_spec] * nh,
        out_shape=[jax.ShapeDtypeStruct((t, RK_WIDTH), F32)] * 6 + hosted.out_shape,
        scratch_shapes=[pltpu.VMEM((h, n, n), F32)] + hosted.sems,
        compiler_params=_params(("arbitrary",)),
    )(r, lw, k, v, kk, a, ck[0], ck[1], dy, *hosted.arrays)
    return res[:6], hosted.finish(res[6:])


def _s5_band_place():
    return jax.nn.one_hot(jnp.arange(S5_BLOCKS) % S5_PER_IN, S5_PER_IN, dtype=F32)


def _s5_in_blocks(bbar):
    b = jnp.transpose(bbar.reshape(S5_BLOCKS, 2, S5_STATE, S5_GROUP), (0, 1, 3, 2))
    band = jnp.einsum('jghp,gk->jghkp', b, jnp.eye(2, dtype=F32)).reshape(S5_BLOCKS, 32, 128)
    return jnp.einsum('jrc,jq->jqrc', band, _s5_band_place()).reshape(S5_BLOCKS, 128, 128)


def _s5_in_unblock(mats):
    band = jnp.einsum('jqrc,jq->jrc', mats.reshape(S5_BLOCKS, S5_PER_IN, 32, 128), _s5_band_place())
    diag = jnp.einsum('jghgp->jghp', band.reshape(S5_BLOCKS, 2, S5_GROUP, 2, S5_STATE))
    return jnp.transpose(diag, (0, 1, 3, 2)).reshape(S5_CH, S5_GROUP)


def _s5_out_blocks(c):
    ct = jnp.transpose(c.reshape(S5_BLOCKS, 2, S5_GROUP, S5_STATE), (0, 1, 3, 2))
    band = jnp.einsum('jgph,gk->jgpkh', ct, jnp.eye(2, dtype=F32)).reshape(S5_BLOCKS, 128, 32)
    return jnp.einsum('jrc,jq->jrqc', band, _s5_band_place()).reshape(S5_BLOCKS, 128, 128)


def _s5_out_unblock(mats):
    band = jnp.einsum('jrqc,jq->jrc', mats.reshape(S5_BLOCKS, 128, S5_PER_IN, 32), _s5_band_place())
    diag = jnp.einsum('jgpgh->jgph', band.reshape(S5_BLOCKS, 2, S5_STATE, 2, S5_GROUP))
    return jnp.transpose(diag, (0, 1, 3, 2)).reshape(S5_GROUPS, S5_GROUP, S5_STATE)


def _head_indicator():
    ch = lax.broadcasted_iota(jnp.int32, (RK_WIDTH, 128), 0) // RK_HEAD
    hd = lax.broadcasted_iota(jnp.int32, (RK_WIDTH, 128), 1)
    seg = (ch == hd).astype(F32)
    return seg, seg.T


def _add_epilogue(acc, e):
    return (acc + e,)


def _local_step(x, target, mod, wt, chunk=RK_CHUNK, ffn_shards=None, mixer_shards=None):
    t = x.shape[0]
    wt = dict(wt)
    sh1, sc1, gt1, sh2, sc2, gt2 = mod
    seg, seg_t = _head_indicator()
    g = {}

    (h1,) = _rowwise("norm1", _norm_mod_fn, [x], [wt["norm1_gain"], sc1, sh1], [(D_MODEL, BF16)], 256)
    if mixer_shards is None:
        proj = _matmul("proj", h1, wt["w_in"])
    else:
        proj, gathered = _matmul("proj", h1, wt["w_in"], hosted=_gather_halves_plan(mixer_shards[0]))
        wt.update(mixer_shards[1](gathered))
    u = proj[:, :S5_WIDTH]
    ps = _token_shift(proj, wt["mu_prev"], wt["mu_next"], first=S5_WIDTH // 128)
    r, k, v = ps[:, :1024], ps[:, 1024:2048], ps[:, 2048:3072]
    wdn, adn, gdn = ps[:, 3072:3200], ps[:, 3200:3328], ps[:, 3328:RK_PAD]

    prep_rows = [wt["lam_re"][0], wt["lam_im"][0], wt["log_step"][0], wt["lam_re"][1], wt["lam_im"][1],
                 wt["log_step"][1], wt["b_re"], wt["b_im"]]
    col1, col16 = (1, F32), (S5_GROUP, F32)
    prep = _rowwise("s5_prep", _s5_prep_fn, prep_rows, [], [col1, col1, col16, col16] * 2, 512)
    lbar = [tuple(prep[4 * d + q].reshape(S5_BLOCKS, 1, 128) for q in range(2)) for d in range(2)]
    b_blk = [tuple(_s5_in_blocks(prep[4 * d + 2 + q]) for q in range(2)) for d in range(2)]
    c_blk = (_s5_out_blocks(wt["c_re"]), -_s5_out_blocks(wt["c_im"]))
    u_il = _interleave(u)
    state0 = _s5_forward("s5_fwd0", u_il, *b_blk[0], *lbar[0], reverse=False)
    s1_re, s1_im, ylin_il = _s5_forward("s5_fwd1", u_il, *b_blk[1], *lbar[1], reverse=True, other=state0,
                                        c_re=c_blk[0], c_im_neg=c_blk[1])
    ylin = _deinterleave(ylin_il)
    states = [tuple(state0), (s1_re, s1_im)]
    s5_par = [wt["s5_d"], wt["s5_w_glu"], wt["s5_b_glu"]]
    (o_s5,) = _rowwise("s5_out", _s5_out_fn, [ylin, u], s5_par, [(S5_WIDTH, BF16)], 256)

    pre_par = [wt["w0"][0], wt["w0"][1], wt["w_up"][0], wt["w_up"][1], wt["a0"][0], wt["a0"][1],
               wt["a_up"][0], wt["a_up"][1], wt["g_up"], wt["k_k"], wt["k_a"]]
    pre = _rowwise("rk_pre", _rk_pre_fn, [k, wdn, adn, gdn], pre_par + [seg, seg_t], [(RK_WIDTH, F32)] * 8, 256)
    kk, lw, kd, act, gate = pre[0], pre[1:3], pre[3:5], pre[5:7], pre[7]
    core_in, ys, cks = [], [], []
    for d in range(2):
        ops = (r, lw[d], kd[d], v, kk, act[d])
        plan = _gather_halves_plan([ffn_shards[d]]) if ffn_shards is not None else None
        y, ck, gathered = _rk_core_fwd(f"rk_core{d}", *ops, reverse=(d == 1), chunk=min(chunk, t), hosted=plan)
        if gathered:
            wt["ffn_w1" if d == 0 else "ffn_w2"] = gathered[0] if d == 0 else gathered[0].reshape(FFN, D_MODEL)
        core_in.append(ops)
        ys.append(y)
        cks.append(ck)
    post_rows = [ys[0], ys[1], r, v, kd[0], kd[1], gate]
    post_par = [wt["ln_gain"], wt["ln_bias"], wt["r_k"]]
    (o_rk,) = _rowwise("rk_post", _rk_post_fn, post_rows, post_par + [seg, seg_t], [(RK_WIDTH, BF16)], 256)

    o = jnp.concatenate([o_s5, o_rk], axis=1)
    mixed = _matmul("mix_out", o, wt["w_out"])
    n2_par = [gt1, wt["norm2_gain"], sc2, sh2]
    x1, h2 = _rowwise("norm2", _resid_norm_mod_fn, [x, mixed], n2_par, [(D_MODEL, F32), (D_MODEL, BF16)], 256)
    f1, hid = _matmul("ffn1", h2, wt["ffn_w1"], out_dtypes=(F32, BF16), chips="b",
                      epilogue=lambda acc: (acc, jnp.square(jnp.maximum(acc, 0.0))))
    ffn = _matmul("ffn2", hid, wt["ffn_w2"])

    ones = jnp.ones((t, 1), F32)
    loss_rows, dx1, dffn, g_gt2, g["final_gain"] = _rowwise_vjp(
        "loss", _loss_fn, [x1, ffn, target], [gt2, wt["final_gain"]], [[ones]], [0, 1], [0, 1], 256, emit=(0,),
        row_grad_dtypes=[F32, BF16])
    df1 = _matmul("ffn2_dx", dffn, wt["ffn_w2"], tb=True, extras=(f1,), out_dtypes=(BF16,),
                  epilogue=lambda acc, f: (acc * (2.0 * jnp.maximum(f, 0.0)),))
    g["ffn_w2"] = _matmul("ffn2_dw", hid, dffn, ta=True)
    if ffn_shards is None:
        g["ffn_w1"] = _matmul("ffn1_dw", h2, df1, ta=True, chips="out")
        dh2 = _matmul("ffn1_dx", df1, wt["ffn_w1"], tb=True, chips="b_t")
    else:
        piece2 = g.pop("ffn_w2").reshape(N_CHIPS, -1, D_MODEL)
        piece1, (other2,) = _matmul("ffn1_dw", h2, df1, ta=True, chips="out", hosted=_other_half_plan([piece2]))
        dh2, (other1,) = _matmul("ffn1_dx", df1, wt["ffn_w1"], tb=True, chips="b_t",
                                 hosted=_other_half_plan([piece1]))
        ffn_sums = [_pair_sum("pair_" + n, piece, other, BF16)
                    for n, piece, other in zip(FFN_SHARDED, (piece1, piece2), (other1, other2))]
    dx_a, dmixed, g_gt1, g["norm2_gain"], g_sc2, g_sh2 = _rowwise_vjp(
        "norm2_bwd", _resid_norm_mod_fn, [x, mixed], n2_par, [[dx1], [dh2]], [0, 1], [0, 1, 2, 3], 256,
        row_grad_dtypes=[F32, BF16])
    do = _matmul("mix_out_dx", dmixed, wt["w_out"], tb=True)
    g["w_out"] = _matmul("mix_out_dw", o, dmixed, ta=True)
    do_s5, do_rk = do[:, :S5_WIDTH], do[:, S5_WIDTH:]

    dylin, du, g["s5_d"], g["s5_w_glu"], g["s5_b_glu"] = _rowwise_vjp(
        "s5_out_bwd", _s5_out_fn, [ylin, u], s5_par, [[do_s5]], [0, 1], [0, 1, 2], 256)
    prep_cts = []
    dylin_il, du_il = _interleave(dylin), _interleave(du)
    for d in range(2):
        res = _s5_backward(f"s5_bwd{d}", dylin_il, u_il, du_il, states[d], states[1] if d == 0 else None,
                           *b_blk[d], *c_blk, *lbar[d], reverse=(d == 1))
        du_il, db_re, db_im, dl_re, dl_im = res[:5]
        if d == 0:
            g["c_re"], g["c_im"] = _s5_out_unblock(res[5]), -_s5_out_unblock(res[6])
        prep_cts += [[dl_re.reshape(S5_CH, 1)], [dl_im.reshape(S5_CH, 1)], [_s5_in_unblock(db_re)],
                     [_s5_in_unblock(db_im)]]
    du = _deinterleave(du_il)
    pg = _rowwise_vjp("s5_prep_bwd", _s5_prep_fn, prep_rows, [], prep_cts, list(range(8)), [], 512)
    g["lam_re"], g["lam_im"], g["log_step"] = (pg[0], pg[3]), (pg[1], pg[4]), (pg[2], pg[5])
    g["b_re"], g["b_im"] = pg[6], pg[7]

    pb = _rowwise_vjp("rk_post_bwd", _rk_post_fn, post_rows, post_par, [[do_rk]], [0, 2, 3, 4, 5, 6], [0, 1, 2],
                      128, consts=[seg, seg_t])
    dy, dr_b, dv_b, dkd_b, dgate = pb[0], pb[1], pb[2], pb[3:5], pb[5]
    g["ln_gain"], g["ln_bias"], g["r_k"] = pb[6], pb[7], pb[8]
    cg = []
    for d in range(2):
        plan = None
        if ffn_shards is not None:
            plan = _exchange_plan([ffn_sums[d]], CHIP_PEERS, N_CHIPS, scatter=True)
        grads, arrived = _rk_core_bwd(f"rk_core{d}_bwd", *core_in[d], cks[d], dy, reverse=(d == 1),
                                      chunk=min(chunk, t), hosted=plan)
        g.setdefault("ffn_arrived", []).extend(arrived)
        cg.append(grads)
    pre_cts = [[cg[0][4], cg[1][4]], [cg[0][1]], [cg[1][1]], [cg[0][2], dkd_b[0]], [cg[1][2], dkd_b[1]],
               [cg[0][5]], [cg[1][5]], [dgate]]
    qb = _rowwise_vjp("rk_pre_bwd", _rk_pre_fn, [k, wdn, adn, gdn], pre_par, pre_cts, [0, 1, 2, 3],
                      list(range(11)), 128, consts=[seg, seg_t])
    dk, dwdn, dadn, dgdn = qb[:4]
    g["w0"], g["w_up"], g["a0"], g["a_up"] = (qb[4], qb[5]), (qb[6], qb[7]), (qb[8], qb[9]), (qb[10], qb[11])
    g["g_up"], g["k_k"], g["k_a"] = qb[12], qb[13], qb[14]
    dr, dv = _rowwise("rk_sum", lambda a, b, c, e, f, h: (a + b + c, e + f + h),
                      [cg[0][0], cg[1][0], dr_b, cg[0][3], cg[1][3], dv_b], [], [(RK_WIDTH, F32)] * 2, 256)
    dps = jnp.concatenate([dr, dk, dv, dwdn, dadn, dgdn], axis=1)
    dp, g["mu_prev"], g["mu_next"] = _token_shift_bwd(proj, wt["mu_prev"], wt["mu_next"], dps,
                                                      first=S5_WIDTH // 128)

    dproj = jnp.concatenate([du, dp], axis=1).astype(BF16)
    dh1 = _matmul("proj_dx", dproj, wt["w_in"], tb=True)
    g["w_in"] = _matmul("proj_dw", h1, dproj, ta=True)
    grad_x, g["norm1_gain"], g_sc1, g_sh1 = _rowwise_vjp(
        "norm1_bwd", _norm_mod_fn, [x], [wt["norm1_gain"], sc1, sh1], [[dh1]], [0], [0, 1, 2], 256,
        addends={0: dx_a})
    g["mod"] = [g_sh1, g_sc1, g_gt1, g_sh2, g_sc2, g_gt2]
    return loss_rows, grad_x, g


CHIP_PEERS = ((1, 0, 0), (0, 1, 0), (1, 1, 0))
ALL_PEERS = ((0, 0, 1), (0, 1, 0), (0, 1, 1), (1, 0, 0), (1, 0, 1), (1, 1, 0), (1, 1, 1))
CORE_PEER = ((0, 0, 1),)


def _exchange(name, arrays, peers, n_slots, scatter=False):
    return _run_plan(name, _exchange_plan(arrays, peers, n_slots, scatter))


def _run_plan(name, plan):
    na = len(plan.arrays)

    def body(*refs):
        plan.start(refs[:na], refs[na:2 * na], refs[2 * na:])
        plan.wait(refs[:na], refs[na:2 * na], refs[2 * na:])

    any_spec = pl.BlockSpec(memory_space=pl.ANY)
    return plan.finish(pl.pallas_call(
        body, name=name, in_specs=[any_spec] * na, out_specs=[any_spec] * na, out_shape=plan.out_shape,
        scratch_shapes=plan.sems,
    )(*plan.arrays))


def _exchange_plan(arrays, peers, n_slots, scatter=False):
    na, nm = len(arrays), len(peers)

    def ident(px, py, pc):
        return {8: 4 * px + 2 * py + pc, 4: 2 * px + py, 2: pc}[n_slots]

    def copies(in_refs, out_refs, sems):
        send_sems, recv_sems = sems
        x, y, c = lax.axis_index("x"), lax.axis_index("y"), lax.axis_index("c")
        me = ident(x, y, c)
        made = []
        for i in range(na):
            for j, (fx, fy, fc) in enumerate(peers):
                px, py, pc = (1 - x if fx else x), (1 - y if fy else y), (1 - c if fc else c)
                src = in_refs[i].at[ident(px, py, pc)] if scatter else in_refs[i]
                made.append(pltpu.make_async_remote_copy(
                    src_ref=src, dst_ref=out_refs[i].at[me],
                    send_sem=send_sems.at[i * nm + j], recv_sem=recv_sems.at[i * nm + j],
                    device_id=(px, py, pc), device_id_type=pl.DeviceIdType.MESH))
        return made

    def start(in_refs, out_refs, sems):
        for copy in copies(in_refs, out_refs, sems):
            copy.start()

    def wait(in_refs, out_refs, sems):
        for copy in copies(in_refs, out_refs, sems):
            copy.wait()

    def finish(outs):
        me = ident(lax.axis_index("x"), lax.axis_index("y"), lax.axis_index("c"))
        return [lax.dynamic_update_slice_in_dim(
            o, lax.dynamic_index_in_dim(a, me, 0, keepdims=True) if scatter else a[None], me, axis=0)
            for a, o in zip(arrays, outs)]

    out_shape = [jax.ShapeDtypeStruct(((n_slots,) + a.shape[1:]) if scatter else ((n_slots,) + a.shape), a.dtype)
                 for a in arrays]
    sems = [pltpu.SemaphoreType.DMA((na * nm,)), pltpu.SemaphoreType.DMA((na * nm,))]
    return _Plan(arrays, out_shape, sems, start, wait, finish)


def _gather_halves(name, arrays):
    return _run_plan(name, _gather_halves_plan(arrays))


def _gather_halves_plan(arrays):
    na = len(arrays)
    chips = ((1, 0), (0, 1), (1, 1))

    def over_ici(in_refs, out_refs, sems):
        ici_send, ici_recv = sems[:2]
        x, y, c = lax.axis_index("x"), lax.axis_index("y"), lax.axis_index("c")
        made = []
        for i in range(na):
            half = arrays[i].shape[0] // 2
            mine = pl.ds(pl.multiple_of(c * half, 8), half)
            for j, (fx, fy) in enumerate(chips):
                px, py = (1 - x if fx else x), (1 - y if fy else y)
                k = len(chips) * i + j
                made.append([pltpu.make_async_remote_copy(
                    src_ref=in_refs[i].at[mine], dst_ref=out_refs[i].at[chip, mine],
                    send_sem=ici_send.at[k], recv_sem=ici_recv.at[k],
                    device_id=(px, py, c), device_id_type=pl.DeviceIdType.MESH)
                    for chip in (2 * x + y, 2 * px + py)])
        return made

    def start(in_refs, out_refs, sems):
        for outgoing, _ in over_ici(in_refs, out_refs, sems):
            outgoing.start()

    def wait(in_refs, out_refs, sems):
        d2d_send, d2d_recv = sems[2:]
        x, y, c = lax.axis_index("x"), lax.axis_index("y"), lax.axis_index("c")
        pending = []
        ici = over_ici(in_refs, out_refs, sems)
        for i in range(na):
            half = arrays[i].shape[0] // 2
            mine = pl.ds(pl.multiple_of(c * half, 8), half)
            theirs = pl.ds(pl.multiple_of((1 - c) * half, 8), half)
            for j, (fx, fy) in enumerate(chips):
                px, py = (1 - x if fx else x), (1 - y if fy else y)
                k = len(chips) * i + j
                outgoing, landing = ici[k]
                landing.wait_recv()
                landed = out_refs[i].at[2 * px + py, mine]
                passed = pltpu.make_async_remote_copy(
                    src_ref=landed, dst_ref=landed, send_sem=d2d_send.at[k], recv_sem=d2d_recv.at[k],
                    device_id=(x, y, 1 - c), device_id_type=pl.DeviceIdType.MESH)
                passed.start()
                from_sibling = out_refs[i].at[2 * px + py, theirs]
                pending += [outgoing.wait_send, passed.wait_send, pltpu.make_async_remote_copy(
                    src_ref=from_sibling, dst_ref=from_sibling, send_sem=d2d_send.at[k], recv_sem=d2d_recv.at[k],
                    device_id=(x, y, 1 - c), device_id_type=pl.DeviceIdType.MESH).wait_recv]
        for done in pending:
            done()

    def finish(outs):
        me = 2 * lax.axis_index("x") + lax.axis_index("y")
        return [lax.dynamic_update_slice_in_dim(o, a[None], me, axis=0) for a, o in zip(arrays, outs)]

    out_shape = [jax.ShapeDtypeStruct((N_CHIPS,) + a.shape, a.dtype) for a in arrays]
    return _Plan(arrays, out_shape, [pltpu.SemaphoreType.DMA((na * len(chips),))] * 4, start, wait, finish)


def _send_other_half(name, arrays):
    return _run_plan(name, _other_half_plan(arrays))


def _other_half_plan(arrays):
    na = len(arrays)

    def copies(in_refs, out_refs, sems):
        send_sems, recv_sems = sems
        x, y, c = lax.axis_index("x"), lax.axis_index("y"), lax.axis_index("c")
        made = []
        for i in range(na):
            half = arrays[i].shape[1] // 2
            theirs = pl.ds(pl.multiple_of((1 - c) * half, 8), half)
            made.append(pltpu.make_async_remote_copy(
                src_ref=in_refs[i].at[:, theirs], dst_ref=out_refs[i], send_sem=send_sems.at[i],
                recv_sem=recv_sems.at[i], device_id=(x, y, 1 - c), device_id_type=pl.DeviceIdType.MESH))
        return made

    def start(in_refs, out_refs, sems):
        for copy in copies(in_refs, out_refs, sems):
            copy.start()

    def wait(in_refs, out_refs, sems):
        for copy in copies(in_refs, out_refs, sems):
            copy.wait()

    out_shape = [jax.ShapeDtypeStruct((a.shape[0], a.shape[1] // 2, a.shape[2]), a.dtype) for a in arrays]
    sems = [pltpu.SemaphoreType.DMA((na,)), pltpu.SemaphoreType.DMA((na,))]
    return _Plan(arrays, out_shape, sems, start, wait, list)


def _adam_math(w, g, m, v):
    m = ADAM_B1 * m + (1.0 - ADAM_B1) * g
    v = ADAM_B2 * v + (1.0 - ADAM_B2) * jnp.square(g)
    m_hat = m / (1.0 - ADAM_B1 ** ADAM_STEP)
    v_hat = v / (1.0 - ADAM_B2 ** ADAM_STEP)
    delta = -ADAM_LR * (m_hat / (jnp.sqrt(v_hat) + ADAM_EPS) + ADAM_WD * w)
    return delta, m, v


WHOLE_BLOCK_BYTES = 2 * 1024 * 1024


def _row_tile(r, c):
    return r if 4 * r * c <= WHOLE_BLOCK_BYTES else _tile(r, (256, 128, 64, 32, 16, 8))


def _sum_parts(name, parts):
    n, r, c = parts.shape
    tr = _row_tile(r, c)

    def body(p_ref, o_ref):
        tot = p_ref[0].astype(F32)
        for i in range(1, n):
            tot = tot + p_ref[i].astype(F32)
        o_ref[...] = tot

    return pl.pallas_call(
        body, name=name, grid=(r // tr,), in_specs=[pl.BlockSpec((n, tr, c), lambda i: (0, i, 0))],
        out_specs=pl.BlockSpec((tr, c), lambda i: (i, 0)), out_shape=jax.ShapeDtypeStruct((r, c), F32),
        compiler_params=_params(("parallel",)),
    )(parts)


def _pair_sum(name, piece, other, dtype):
    n, r, c = piece.shape
    half = r // 2
    tr = _row_tile(half, c)

    def body(lo_ref, hi_ref, other_ref, o_ref):
        own = jnp.where(lax.axis_index("c") == 0, lo_ref[...], hi_ref[...])
        o_ref[...] = (own + other_ref[...]).astype(o_ref.dtype)

    blk = pl.BlockSpec((None, tr, c), lambda j, i: (j, i, 0))
    return pl.pallas_call(
        body, name=name, grid=(n, half // tr),
        in_specs=[pl.BlockSpec((None, None, tr, c), lambda j, i: (j, 0, i, 0)),
                  pl.BlockSpec((None, None, tr, c), lambda j, i: (j, 1, i, 0)), blk],
        out_specs=blk, out_shape=jax.ShapeDtypeStruct((n, half, c), dtype),
        compiler_params=_params(("parallel", "parallel")),
    )(piece.reshape(n, 2, half, c), piece.reshape(n, 2, half, c), other)


def _adamw(name, w, parts, m, v, hosted=None):
    n, r, c = parts.shape
    tr = _row_tile(r, c)
    steps = r // tr
    plan = hosted or _NO_PLAN
    nh = len(plan.arrays)

    def body(w_ref, p_ref, m_ref, v_ref, *rest):
        host_in, (g_ref, d_ref, nm_ref, nv_ref) = rest[:nh], rest[nh:nh + 4]
        host_out, sems = rest[nh + 4:2 * nh + 4], rest[2 * nh + 4:]

        @pl.when(pl.program_id(0) == 0)
        def _():
            plan.start(host_in, host_out, sems)

        g = p_ref[0].astype(F32)
        for i in range(1, n):
            g = g + p_ref[i].astype(F32)
        delta, nm, nv = _adam_math(w_ref[...], g, m_ref[...], v_ref[...])
        g_ref[...], d_ref[...], nm_ref[...], nv_ref[...] = g, delta, nm, nv

        @pl.when(pl.program_id(0) == steps - 1)
        def _():
            plan.wait(host_in, host_out, sems)

    blk = pl.BlockSpec((tr, c), lambda i: (i, 0))
    any_spec = pl.BlockSpec(memory_space=pl.ANY)
    res = pl.pallas_call(
        body, name=name, grid=(steps,),
        in_specs=[blk, pl.BlockSpec((n, tr, c), lambda i: (0, i, 0)), blk, blk] + [any_spec] * nh,
        out_specs=[blk] * 4 + [any_spec] * nh,
        out_shape=[jax.ShapeDtypeStruct((r, c), F32)] * 4 + plan.out_shape, scratch_shapes=plan.sems,
        compiler_params=_params(("arbitrary",) if nh else ("parallel",)),
    )(w, parts, m, v, *plan.arrays)
    return (res[:4], plan.finish(res[4:])) if nh else res


def _ada_w_update(act_t, dmod, w, m, v, hosted):
    r, c = w.shape
    nb = act_t.shape[1]
    tr, tc = 256, 1024
    grid = (r // tr, c // tc)
    nh = len(hosted.arrays)

    def body(a_ref, d_ref, w_ref, m_ref, v_ref, *rest):
        host_in, (g_ref, dl_ref, nm_ref, nv_ref) = rest[:nh], rest[nh:nh + 4]
        host_out, sems = rest[nh + 4:2 * nh + 4], rest[2 * nh + 4:]
        i, j = pl.program_id(0), pl.program_id(1)

        @pl.when(jnp.logical_and(i == 0, j == 0))
        def _():
            hosted.start(host_in, host_out, sems)

        a, dm = a_ref[...], d_ref[...]
        g = a[:, 0:1] * dm[0:1, :]
        for b in range(1, nb):
            g = g + a[:, b:b + 1] * dm[b:b + 1, :]
        delta, nm, nv = _adam_math(w_ref[...], g, m_ref[...], v_ref[...])
        g_ref[...], dl_ref[...], nm_ref[...], nv_ref[...] = g, delta, nm, nv

        @pl.when(jnp.logical_and(i == grid[0] - 1, j == grid[1] - 1))
        def _():
            hosted.wait(host_in, host_out, sems)

    blk = pl.BlockSpec((tr, tc), lambda i, j: (i, j))
    any_spec = pl.BlockSpec(memory_space=pl.ANY)
    res = pl.pallas_call(
        body, name="ada_w_update", grid=grid,
        in_specs=[pl.BlockSpec((tr, nb), lambda i, j: (i, 0)), pl.BlockSpec((nb, tc), lambda i, j: (0, j)),
                  blk, blk, blk] + [any_spec] * nh,
        out_specs=[blk] * 4 + [any_spec] * nh,
        out_shape=[jax.ShapeDtypeStruct((r, c), F32)] * 4 + hosted.out_shape,
        scratch_shapes=hosted.sems,
        compiler_params=_params(("arbitrary", "arbitrary")),
    )(act_t, dmod, w, m, v, *hosted.arrays)
    return res[:4], hosted.finish(res[4:])


WEIGHTS = ['ada_w', 'ada_b', 'norm1_gain', 'norm2_gain', 'final_gain', 'w_in', 'w_out', 's5_lambda_re',
           's5_lambda_im', 's5_log_step', 's5_b_re', 's5_b_im', 's5_c_re', 's5_c_im', 's5_d', 's5_w_glu',
           's5_b_glu', 'rk_shift_prev', 'rk_shift_next', 'rk_w0', 'rk_w_up', 'rk_a0', 'rk_a_up', 'rk_g_up',
           'rk_k_k', 'rk_k_a', 'rk_r_k', 'rk_ln_gain', 'rk_ln_bias', 'ffn_w1', 'ffn_w2']
BIG_SHARDED = ['w_in', 'w_out', 's5_w_glu', 'ffn_w1', 'ffn_w2']
FFN_SHARDED = ['ffn_w1', 'ffn_w2']
RK_SHARDED = ['rk_w0', 'rk_a0', 'rk_w_up', 'rk_a_up', 'rk_g_up']
REPLICATED = ['ada_b', 'norm1_gain', 'norm2_gain', 'final_gain', 's5_lambda_re', 's5_lambda_im', 's5_log_step',
              's5_b_re', 's5_b_im', 's5_c_re', 's5_c_im', 's5_d', 's5_b_glu', 'rk_shift_prev', 'rk_shift_next',
              'rk_k_k', 'rk_k_a', 'rk_r_k', 'rk_ln_gain', 'rk_ln_bias']
PACK_COLS = 1024
N_CHIPS = 4
RK_ROWS = 420
RK_ROWS_PAD = 432


def _pack_rows(arrays, cols):
    return jnp.concatenate([a.reshape(-1, cols) for a in arrays], axis=0)


def _pack_flat(arrays):
    flat = jnp.concatenate([a.reshape(-1) for a in arrays])
    rows = -(-flat.shape[0] // PACK_COLS)
    return jnp.pad(flat, (0, rows * PACK_COLS - flat.shape[0])).reshape(rows, PACK_COLS)


def _unpack_flat(packed, like):
    flat, out, pos = packed.reshape(-1), [], 0
    for a in like:
        out.append(flat[pos:pos + a.size].reshape(a.shape))
        pos += a.size
    return out


def _cols_to_chips(full, n_rows):
    return jnp.transpose(full.reshape(n_rows, N_CHIPS, -1), (1, 0, 2))


def _chips_to_cols(parts):
    return jnp.transpose(parts, (1, 0, 2)).reshape(parts.shape[1], -1)


def kernel(x, c, ada_w, ada_b, norm1_gain, norm2_gain, final_gain, w_in, w_out, s5_lambda_re, s5_lambda_im, s5_log_step, s5_b_re, s5_b_im, s5_c_re, s5_c_im, s5_d, s5_w_glu, s5_b_glu, rk_shift_prev, rk_shift_next, rk_w0, rk_w_up, rk_a0, rk_a_up, rk_g_up, rk_k_k, rk_k_a, rk_r_k, rk_ln_gain, rk_ln_bias, ffn_w1, ffn_w2, loss_target, m_ada_w, m_ada_b, m_norm1_gain, m_norm2_gain, m_final_gain, m_w_in, m_w_out, m_s5_lambda_re, m_s5_lambda_im, m_s5_log_step, m_s5_b_re, m_s5_b_im, m_s5_c_re, m_s5_c_im, m_s5_d, m_s5_w_glu, m_s5_b_glu, m_rk_shift_prev, m_rk_shift_next, m_rk_w0, m_rk_w_up, m_rk_a0, m_rk_a_up, m_rk_g_up, m_rk_k_k, m_rk_k_a, m_rk_r_k, m_rk_ln_gain, m_rk_ln_bias, m_ffn_w1, m_ffn_w2, v_ada_w, v_ada_b, v_norm1_gain, v_norm2_gain, v_final_gain, v_w_in, v_w_out, v_s5_lambda_re, v_s5_lambda_im, v_s5_log_step, v_s5_b_re, v_s5_b_im, v_s5_c_re, v_s5_c_im, v_s5_d, v_s5_w_glu, v_s5_b_glu, v_rk_shift_prev, v_rk_shift_next, v_rk_w0, v_rk_w_up, v_rk_a0, v_rk_a_up, v_rk_g_up, v_rk_k_k, v_rk_k_a, v_rk_r_k, v_rk_ln_gain, v_rk_ln_bias, v_ffn_w1, v_ffn_w2):
    given = dict(locals())
    w = {n: given[n] for n in WEIGHTS}
    m = {n: given["m_" + n] for n in WEIGHTS}
    v = {n: given["v_" + n] for n in WEIGHTS}
    mx, my, mc = lax.axis_index("x"), lax.axis_index("y"), lax.axis_index("c")
    chip = 2 * mx + my
    dev = 2 * chip + mc
    xt, target = x[0], loss_target[0]

    def rk_rows(d):
        rows = _pack_rows([d[n] for n in RK_SHARDED], 256)
        return jnp.pad(rows, ((0, RK_ROWS_PAD - rows.shape[0]), (0, 0)))

    (c_all,), (w_in_parts,) = _run_plan("gather_first", _join_plans([
        _exchange_plan([c], ALL_PEERS, 8), _gather_halves_plan([w_in[0].astype(BF16)])]))

    (act,) = _rowwise("ada_act", lambda q: (q * _sigmoid(q),), [c_all.reshape(8, D_MODEL)], [], [(D_MODEL, F32)], 8)
    n_mod_cols = N_MOD * D_MODEL // N_CHIPS
    bias = jnp.broadcast_to(lax.dynamic_slice(ada_b, (0, chip * n_mod_cols), (1, n_mod_cols)), (8, n_mod_cols))
    mod_shard = _matmul("ada_fwd", act, ada_w[0], epilogue=_add_epilogue, extras=(bias,))
    (mod_parts,) = _exchange("gather_mod", [mod_shard], CHIP_PEERS, N_CHIPS)
    mod_all = _chips_to_cols(mod_parts)
    mod_mine = lax.dynamic_slice(mod_all, (dev, 0), (1, N_MOD * D_MODEL))
    mod = [mod_mine[:, i * D_MODEL:(i + 1) * D_MODEL] for i in range(N_MOD)]

    def mixer_weights(parts):
        w_out_parts, glu_parts, rk_full = parts

        def rk_piece(lo, hi, lead):
            return _chips_to_cols(rk_full[:, lo:hi]).reshape(lead + (RK_WIDTH,))

        zeros = jnp.zeros((LORA, RK_WIDTH), F32)
        w_up, a_up = rk_piece(4, 132, (2, LORA)), rk_piece(132, 260, (2, LORA))
        return {
            "w_out": w_out_parts.reshape(D_MODEL, D_MODEL), "s5_w_glu": glu_parts.reshape(S5_WIDTH, S5_WIDTH),
            "w0": list(rk_piece(0, 2, (2,))[:, None, :]), "a0": list(rk_piece(2, 4, (2,))[:, None, :]),
            "w_up": [jnp.concatenate([w_up[0], zeros]), jnp.concatenate([zeros, w_up[1]])],
            "a_up": [jnp.concatenate([a_up[0], zeros]), jnp.concatenate([zeros, a_up[1]])],
            "g_up": jnp.pad(rk_piece(260, 420, (GATE_LORA,)), ((0, GATE_PAD - GATE_LORA), (0, 0))),
        }

    wt = {
        "norm1_gain": norm1_gain, "norm2_gain": norm2_gain, "final_gain": final_gain.reshape(1, D_MODEL),
        "w_in": jnp.pad(_chips_to_cols(w_in_parts), ((0, 0), (0, PROJ_PAD - PROJ))),
        "mu_prev": jnp.pad(rk_shift_prev, ((0, 0), (0, RK_PAD - RK_IN))),
        "mu_next": jnp.pad(rk_shift_next, ((0, 0), (0, RK_PAD - RK_IN))),
        "lam_re": [s5_lambda_re[0, d].reshape(S5_CH, 1) for d in range(2)],
        "lam_im": [s5_lambda_im[0, d].reshape(S5_CH, 1) for d in range(2)],
        "log_step": [jnp.repeat(s5_log_step[0, d], S5_STATE).reshape(S5_CH, 1) for d in range(2)],
        "b_re": s5_b_re.reshape(S5_CH, S5_GROUP), "b_im": s5_b_im.reshape(S5_CH, S5_GROUP),
        "c_re": s5_c_re[0], "c_im": s5_c_im[0],
        "s5_d": s5_d, "s5_b_glu": s5_b_glu,
        "k_k": rk_k_k, "k_a": rk_k_a, "r_k": rk_r_k.reshape(1, RK_WIDTH),
        "ln_gain": rk_ln_gain, "ln_bias": rk_ln_bias,
    }

    ffn_shards = [w[n][0].astype(BF16) for n in FFN_SHARDED]
    mixer_shards = [w_out[0].astype(BF16), s5_w_glu[0].astype(BF16), rk_rows(w)]
    loss_rows, grad_x, g = _local_step(xt, target, mod, wt, ffn_shards=ffn_shards,
                                       mixer_shards=(mixer_shards, mixer_weights))
    loss = lax.psum(jnp.sum(loss_rows), ("x", "y", "c"))


    big_grads = {
        "w_in": _cols_to_chips(g["w_in"][:, :PROJ], D_MODEL),
        "w_out": g["w_out"].reshape(N_CHIPS, -1, D_MODEL),
        "s5_w_glu": g["s5_w_glu"].reshape(N_CHIPS, -1, S5_WIDTH),
    }
    rk_grads = jnp.concatenate([
        _cols_to_chips(jnp.concatenate(g["w0"]), 2), _cols_to_chips(jnp.concatenate(g["a0"]), 2),
        _cols_to_chips(jnp.concatenate([g["w_up"][0][:LORA], g["w_up"][1][LORA:]]), 2 * LORA),
        _cols_to_chips(jnp.concatenate([g["a_up"][0][:LORA], g["a_up"][1][LORA:]]), 2 * LORA),
        _cols_to_chips(g["g_up"][:GATE_LORA], GATE_LORA),
        jnp.zeros((N_CHIPS, RK_ROWS_PAD - RK_ROWS, 256), F32)], axis=1)
    local_small = {
        "ada_b": jnp.concatenate(g["mod"], axis=1),
        "norm1_gain": g["norm1_gain"], "norm2_gain": g["norm2_gain"], "final_gain": g["final_gain"],
        "s5_lambda_re": jnp.concatenate(g["lam_re"]), "s5_lambda_im": jnp.concatenate(g["lam_im"]),
        "s5_log_step": jnp.concatenate([q.reshape(S5_GROUPS, S5_STATE).sum(axis=1) for q in g["log_step"]]),
        "s5_b_re": g["b_re"], "s5_b_im": g["b_im"], "s5_c_re": g["c_re"], "s5_c_im": g["c_im"],
        "s5_d": g["s5_d"], "s5_b_glu": g["s5_b_glu"],
        "rk_shift_prev": g["mu_prev"][:, :RK_IN], "rk_shift_next": g["mu_next"][:, :RK_IN],
        "rk_k_k": g["k_k"], "rk_k_a": g["k_a"], "rk_r_k": g["r_k"],
        "rk_ln_gain": g["ln_gain"], "rk_ln_bias": g["ln_bias"],
    }
    late = [n for n in BIG_SHARDED if n not in FFN_SHARDED]
    late_pieces = [big_grads[n] for n in late] + [rk_grads]
    late_names = late + ["rk"]

    def whole(halves):
        return halves.reshape(1, 2 * halves.shape[1], halves.shape[2])

    ffn_halves = [_sum_parts("sum_" + n, a) for n, a in zip(FFN_SHARDED, g["ffn_arrived"])]
    from_sibling, ffn_pairs, (small_all,) = _run_plan("swap_late", _join_plans([
        _other_half_plan(late_pieces), _exchange_plan(ffn_halves, CORE_PEER, 2),
        _exchange_plan([_pack_flat([local_small[n] for n in REPLICATED]).astype(BF16)], ALL_PEERS, 8)]))
    late_sums = [_pair_sum("pair_" + n, piece, other, F32 if n == "rk" else BF16)
                 for n, piece, other in zip(late_names, late_pieces, from_sibling)]
    pairs = dict(zip(FFN_SHARDED, [whole(p) for p in ffn_pairs]))

    mod_rows = N_MOD * D_MODEL // PACK_COLS
    dmod_all = small_all[:, :mod_rows].reshape(8, N_MOD * D_MODEL).astype(F32)
    dmod = lax.dynamic_slice(dmod_all, (0, chip * n_mod_cols), (8, n_mod_cols))
    ada_res, arrived = _ada_w_update(act.T, dmod, ada_w[0], m_ada_w[0], v_ada_w[0],
                                     hosted=_exchange_plan(late_sums, CHIP_PEERS, N_CHIPS, scatter=True))
    late_halves = [_sum_parts("sum_" + n, a) for n, a in zip(late_names, arrived)]

    out = {"ada_w": [r[None] for r in ada_res]}
    first = FFN_SHARDED[0]
    res, swapped = _adamw("adamw_" + first, w[first][0], pairs[first], m[first][0], v[first][0],
                          hosted=_exchange_plan(late_halves, CORE_PEER, 2))
    out[first] = [r[None] for r in res]
    pairs.update(zip(late_names, [whole(p) for p in swapped]))
    for n in [FFN_SHARDED[1]] + late:
        out[n] = [r[None] for r in _adamw("adamw_" + n, w[n][0], pairs[n], m[n][0], v[n][0])]
    rk_res = _adamw("adamw_rk", rk_rows(w), pairs["rk"], rk_rows(m), rk_rows(v))
    for q in range(4):
        pieces, pos = [], 0
        for n in RK_SHARDED:
            rows = w[n].size // 256
            pieces.append(rk_res[q][pos:pos + rows].reshape(w[n].shape))
            pos += rows
        for n, piece in zip(RK_SHARDED, pieces):
            out.setdefault(n, []).append(piece)

    small_res = _adamw("adamw_small", _pack_flat([w[n] for n in REPLICATED]), small_all,
                       _pack_flat([m[n] for n in REPLICATED]), _pack_flat([v[n] for n in REPLICATED]))
    for q in range(4):
        for n, piece in zip(REPLICATED, _unpack_flat(small_res[q], [w[n] for n in REPLICATED])):
            out.setdefault(n, []).append(piece)

    return (loss, grad_x[None], *[out[n][0] for n in WEIGHTS], *[out[n][1] for n in WEIGHTS],
            *[out[n][2] for n in WEIGHTS], *[out[n][3] for n in WEIGHTS])
```

```python
import functools
import math

import jax
import jax.numpy as jnp
from jax import lax
from jax.experimental import pallas as pl
from jax.experimental.pallas import tpu as pltpu

F32 = jnp.float32
BF16 = jnp.bfloat16

D_MODEL = 2048
S5_WIDTH = 1024
S5_GROUP = 16
S5_GROUPS = 64
S5_STATE = 64
S5_CH = S5_GROUPS * S5_STATE
S5_BLK = 256
RK_WIDTH = 1024
RK_HEAD = 64
RK_HEADS = 16
LORA = 64
GATE_LORA = 160
GATE_PAD = 256
RK_IN = 3488
RK_PAD = 3584
PROJ = 4512
PROJ_PAD = 4608
FFN = 8192
N_MOD = 6
NORM_EPS = 1e-6
GN_EPS = 64e-5
L2_EPS = 1e-12
RK_CHUNK = 64
RK_PASSES = {"solve": 3, "kt": 3, "s0": 1, "akk_v": 1, "ark_v": 1, "arb_u": 1, "state": 3}
LW_SCALE = math.exp(-0.5)
ADAM_LR, ADAM_B1, ADAM_B2, ADAM_EPS, ADAM_WD, ADAM_STEP = 0.001, 0.9, 0.999, 1e-08, 0.01, 10
VMEM_LIMIT = 56 * 1024 * 1024
HI = lax.Precision.HIGHEST


def _params(sem=None):
    return pltpu.CompilerParams(dimension_semantics=sem, vmem_limit_bytes=VMEM_LIMIT)


def _full(a):
    nd = a.ndim
    return pl.BlockSpec(a.shape, lambda *_: (0,) * nd)


@jax.custom_vjp
def _bdot(a, b):
    return jnp.dot(a.astype(BF16), b.astype(BF16), preferred_element_type=F32)


def _bdot_fwd(a, b):
    return _bdot(a, b), (a, b)


def _bdot_bwd(res, g):
    a, b = res
    gb = g.astype(BF16)
    da = lax.dot_general(gb, b.astype(BF16), (((1,), (1,)), ((), ())), preferred_element_type=F32)
    db = lax.dot_general(a.astype(BF16), gb, (((0,), (0,)), ((), ())), preferred_element_type=F32)
    return da, db


_bdot.defvjp(_bdot_fwd, _bdot_bwd)


@jax.custom_vjp
def _seg_dot(x, ind, ind_t):
    hi = x.astype(BF16)
    lo = (x - hi.astype(F32)).astype(BF16)
    both = jnp.dot(jnp.concatenate([hi, lo], axis=0), ind.astype(BF16), preferred_element_type=F32)
    return both[:x.shape[0]] + both[x.shape[0]:]


_seg_dot.defvjp(lambda x, ind, ind_t: (_seg_dot(x, ind, ind_t), (ind, ind_t)),
                lambda res, g: (_seg_dot(g, res[1], res[0]), jnp.zeros_like(res[0]), jnp.zeros_like(res[1])))


def _sigmoid(z):
    return 1.0 / (1.0 + jnp.exp(-z))


def _gelu(y):
    return 0.5 * y * (1.0 + jnp.tanh(0.7978845608028654 * (y + 0.044715 * (y * y * y))))


def _rms(x):
    return x * lax.rsqrt(jnp.mean(x * x, axis=-1, keepdims=True) + NORM_EPS)


def _tile(n, prefs):
    for t in prefs:
        if n % t == 0:
            return t
    return n


def _matmul(name, a, b, ta=False, tb=False, epilogue=None, extras=(), out_dtypes=(F32,), chips=None, hosted=None):
    m = a.shape[1] if ta else a.shape[0]
    k = a.shape[0] if ta else a.shape[1]
    if chips == "b":
        assert not tb and b.shape[1] == k
        n = N_CHIPS * b.shape[2]
    elif chips == "b_t":
        assert tb and N_CHIPS * b.shape[2] == k
        n = b.shape[1]
    else:
        n = b.shape[0] if tb else b.shape[1]
        assert k == (b.shape[1] if tb else b.shape[0]), (a.shape, b.shape, ta, tb)
    split = N_CHIPS if chips in ("b", "out") else 1
    tm = _tile(m, (1024, 512, 256, 128))
    tn = _tile(n // split, (1024, 768, 512, 256, 128))
    tk = k // N_CHIPS if chips == "b_t" else _tile(k, (2048, 1024, 512, 256, 128))
    nk = k // tk
    per = n // split // tn
    n_ex, n_out = len(extras), len(out_dtypes)
    dims = (((0 if ta else 1,), (1 if tb else 0,)), ((), ()))

    hosted = hosted or _NO_PLAN
    nh = len(hosted.arrays)
    grid = (m // tm, split, per, nk)

    def body(a_ref, b_ref, *rest):
        ex_refs, host_in = rest[:n_ex], rest[n_ex:n_ex + nh]
        out_refs, host_out = rest[n_ex + nh:n_ex + nh + n_out], rest[n_ex + nh + n_out:n_ex + 2 * nh + n_out]
        acc, sems = rest[n_ex + 2 * nh + n_out], rest[n_ex + 2 * nh + n_out + 1:]
        kk = pl.program_id(3)
        if nh:
            ids = [pl.program_id(d) for d in range(4)]
            first = functools.reduce(jnp.logical_and, [i == 0 for i in ids])
            last = functools.reduce(jnp.logical_and, [i == g - 1 for i, g in zip(ids, grid)])

            @pl.when(first)
            def _():
                hosted.start(host_in, host_out, sems)

        @pl.when(kk == 0)
        def _():
            acc[...] = jnp.zeros_like(acc)

        acc[...] += lax.dot_general(a_ref[...].astype(BF16), b_ref[...].astype(BF16), dims,
                                    preferred_element_type=F32)

        @pl.when(kk == nk - 1)
        def _():
            res = acc[...]
            outs = epilogue(res, *[e[...] for e in ex_refs]) if epilogue is not None else (res,)
            for o_ref, val in zip(out_refs, outs):
                o_ref[...] = val.astype(o_ref.dtype)

        if nh:
            @pl.when(last)
            def _():
                hosted.wait(host_in, host_out, sems)

    if ta:
        a_spec = pl.BlockSpec((tk, tm), lambda i, c, j, q: (q, i))
    else:
        a_spec = pl.BlockSpec((tm, tk), lambda i, c, j, q: (i, q))
    if chips == "b":
        b_spec = pl.BlockSpec((None, tk, tn), lambda i, c, j, q: (c, q, j))
    elif chips == "b_t":
        b_spec = pl.BlockSpec((None, tn, tk), lambda i, c, j, q: (q, j, 0))
    elif tb:
        b_spec = pl.BlockSpec((tn, tk), lambda i, c, j, q: (c * per + j, q))
    else:
        b_spec = pl.BlockSpec((tk, tn), lambda i, c, j, q: (q, c * per + j))
    mn_spec = pl.BlockSpec((tm, tn), lambda i, c, j, q: (i, c * per + j))
    if chips == "out":
        out_spec = pl.BlockSpec((None, tm, tn), lambda i, c, j, q: (c, i, j))
        out_shape = [jax.ShapeDtypeStruct((N_CHIPS, m, n // N_CHIPS), dt) for dt in out_dtypes]
    else:
        out_spec, out_shape = mn_spec, [jax.ShapeDtypeStruct((m, n), dt) for dt in out_dtypes]
    any_spec = pl.BlockSpec(memory_space=pl.ANY)
    order = ("arbitrary",) * 4 if nh else ("parallel", "parallel", "parallel", "arbitrary")
    outs = pl.pallas_call(
        body, name=name, grid=grid,
        in_specs=[a_spec, b_spec] + [mn_spec] * n_ex + [any_spec] * nh,
        out_specs=[out_spec] * n_out + [any_spec] * nh, out_shape=out_shape + hosted.out_shape,
        scratch_shapes=[pltpu.VMEM((tm, tn), F32)] + hosted.sems,
        compiler_params=_params(order),
    )(a, b, *extras, *hosted.arrays)
    res = outs[0] if n_out == 1 else outs[:n_out]
    return (res, hosted.finish(outs[n_out:])) if nh else res


def _row_spec(a, tm):
    return pl.BlockSpec((tm, a.shape[1]), lambda i: (i, 0))


def _rowwise(name, fn, rows, params, outs, tm):
    t = rows[0].shape[0]
    tm = min(tm, t)
    n_r, n_p = len(rows), len(params)

    def body(*refs):
        vals = [r[...] for r in refs[:n_r + n_p]]
        res = fn(*vals)
        for o_ref, val in zip(refs[n_r + n_p:], res):
            o_ref[...] = val.astype(o_ref.dtype)

    res = pl.pallas_call(
        body, name=name, grid=(t // tm,),
        in_specs=[_row_spec(r, tm) for r in rows] + [_full(p) for p in params],
        out_specs=[pl.BlockSpec((tm, n), lambda i: (i, 0)) for n, _ in outs],
        out_shape=[jax.ShapeDtypeStruct((t, n), dt) for n, dt in outs],
        compiler_params=_params(("parallel",)),
    )(*rows, *params)
    return res


def _rowwise_vjp(name, fn, rows, params, cts, row_grads, param_grads, tm, consts=(), addends=None,
                 emit=(), row_grad_dtypes=None):
    t = rows[0].shape[0]
    tm = min(tm, t)
    addends = addends or {}
    n_r, n_p, n_c = len(rows), len(params), len(consts)
    ct_flat = [c for group in cts for c in group]
    add_list = [addends[q] for q in sorted(addends)]
    n_ct, n_add = len(ct_flat), len(add_list)
    row_grad_dtypes = row_grad_dtypes or [F32] * len(row_grads)

    def body(*refs):
        pos = 0
        row_v = [r[...].astype(F32) for r in refs[pos:pos + n_r]]; pos += n_r
        par_v = [r[...].astype(F32) for r in refs[pos:pos + n_p]]; pos += n_p
        con_v = [r[...] for r in refs[pos:pos + n_c]]; pos += n_c
        ct_v = [r[...].astype(F32) for r in refs[pos:pos + n_ct]]; pos += n_ct
        add_v = [r[...] for r in refs[pos:pos + n_add]]; pos += n_add
        emit_refs = refs[pos:pos + len(emit)]; pos += len(emit)
        rg_refs = refs[pos:pos + len(row_grads)]; pos += len(row_grads)
        pg_refs = refs[pos:pos + len(param_grads)]

        def diff_fn(*dargs):
            rv, pv = list(row_v), list(par_v)
            for q, i in enumerate(row_grads):
                rv[i] = dargs[q]
            for q, j in enumerate(param_grads):
                pv[j] = dargs[len(row_grads) + q]
            return fn(*rv, *pv, *con_v)

        prim = [row_v[i] for i in row_grads] + [par_v[j] for j in param_grads]
        res, vjp = jax.vjp(diff_fn, *prim)
        ct_vals, q = [], 0
        for o, group in zip(res, cts):
            tot = jnp.zeros_like(o)
            for _ in group:
                tot = tot + ct_v[q]
                q += 1
            ct_vals.append(tot)
        grads = vjp(tuple(ct_vals))
        for e_ref, idx in zip(emit_refs, emit):
            e_ref[...] = res[idx].astype(e_ref.dtype)
        add_pos = {p: q for q, p in enumerate(sorted(addends))}
        for q, g_ref in enumerate(rg_refs):
            g = grads[q]
            if q in add_pos:
                g = g + add_v[add_pos[q]]
            g_ref[...] = g.astype(g_ref.dtype)

        @pl.when(pl.program_id(0) == 0)
        def _():
            for g_ref in pg_refs:
                g_ref[...] = jnp.zeros_like(g_ref)

        for q, g_ref in enumerate(pg_refs):
            g_ref[...] += grads[len(row_grads) + q]

    emit_shapes = []
    if emit:
        probe = jax.eval_shape(lambda *a: fn(*a), *[jax.ShapeDtypeStruct((tm, r.shape[1]), F32) for r in rows],
                               *[jax.ShapeDtypeStruct(p.shape, p.dtype) for p in params],
                               *[jax.ShapeDtypeStruct(c.shape, c.dtype) for c in consts])
        emit_shapes = [probe[idx].shape[1] for idx in emit]
    out_specs = ([pl.BlockSpec((tm, n), lambda i: (i, 0)) for n in emit_shapes]
                 + [_row_spec(rows[i], tm) for i in row_grads]
                 + [_full(params[j]) for j in param_grads])
    out_shape = ([jax.ShapeDtypeStruct((t, n), F32) for n in emit_shapes]
                 + [jax.ShapeDtypeStruct(rows[i].shape, dt) for i, dt in zip(row_grads, row_grad_dtypes)]
                 + [jax.ShapeDtypeStruct(params[j].shape, F32) for j in param_grads])
    return pl.pallas_call(
        body, name=name, grid=(t // tm,),
        in_specs=([_row_spec(r, tm) for r in rows] + [_full(p) for p in params] + [_full(c) for c in consts]
                  + [_row_spec(c, tm) for c in ct_flat] + [_row_spec(a, tm) for a in add_list]),
        out_specs=out_specs, out_shape=out_shape,
        compiler_params=_params(("arbitrary",)),
    )(*rows, *params, *consts, *ct_flat, *add_list)


def _norm_mod_fn(x, gain, scale, shift):
    return (_rms(x) * gain * (1.0 + scale) + shift,)


def _resid_norm_mod_fn(x, mixed, gate, gain, scale, shift):
    x1 = x + gate * mixed
    return x1, _rms(x1) * gain * (1.0 + scale) + shift


def _loss_fn(x1, ffn, target, gate, gain):
    y = _rms(x1 + gate * ffn) * gain
    err = y - target
    return (0.5 * jnp.mean(err * err, axis=-1, keepdims=True),)


def _s5_out_fn(ylin, u, d_skip, w_glu, b_glu):
    z = _gelu(ylin + d_skip * u)
    return (z * _sigmoid(_bdot(z, w_glu) + b_glu),)


def _rk_pre_fn(k, wdn, adn, gdn, w0_0, w0_1, wup_0, wup_1, a0_0, a0_1, aup_0, aup_1, g_up, k_k, k_a, seg, seg_t):
    kkr = k * k_k
    inv = 1.0 / jnp.sqrt(jnp.maximum(_seg_dot(kkr * kkr, seg, seg_t), L2_EPS * L2_EPS))
    kk = kkr * _seg_dot(inv, seg_t, seg)
    tw = jnp.tanh(wdn)
    lws, kds, acts = [], [], []
    for w0, wup, a0, aup in ((w0_0, wup_0, a0_0, aup_0), (w0_1, wup_1, a0_1, aup_1)):
        lws.append(-LW_SCALE * _sigmoid(w0 + _bdot(tw, wup)))
        act = _sigmoid(a0 + _bdot(adn, aup))
        acts.append(act)
        kds.append(k * (1.0 + (act - 1.0) * k_a))
    gate = _bdot(_sigmoid(gdn), g_up)
    return (kk, lws[0], lws[1], kds[0], kds[1], acts[0], acts[1], gate)


def _rk_post_fn(y0, y1, r, v, kd0, kd1, gate, ln_gain, ln_bias, r_k, seg, seg_t):
    y = y0 + y1
    mu = _seg_dot(_seg_dot(y, seg, seg_t) * (1.0 / RK_HEAD), seg_t, seg)
    yc = y - mu
    var = _seg_dot(yc * yc, seg, seg_t) * (1.0 / RK_HEAD)
    yn = yc * _seg_dot(lax.rsqrt(var + GN_EPS), seg_t, seg) * ln_gain + ln_bias
    bonus = _seg_dot(_seg_dot(r * (kd0 + kd1) * r_k, seg, seg_t), seg_t, seg)
    return ((yn + bonus * v) * gate,)


def _s5_prep_fn(lr0, li0, ls0, lr1, li1, ls1, b_re, b_im):
    outs = []
    for lam_re, lam_im, ls in ((lr0, li0, ls0), (lr1, li1, ls1)):
        step = jnp.exp(ls)
        mag = jnp.exp(lam_re * step)
        lbar_re = mag * jnp.cos(lam_im * step)
        lbar_im = mag * jnp.sin(lam_im * step)
        den = lam_re * lam_re + lam_im * lam_im
        nr = lbar_re - 1.0
        coef_re = (nr * lam_re + lbar_im * lam_im) / den
        coef_im = (lbar_im * lam_re - nr * lam_im) / den
        outs += [lbar_re, lbar_im, coef_re * b_re - coef_im * b_im, coef_re * b_im + coef_im * b_re]
    return tuple(outs)


def _shift_rows(x, down):
    t = x.shape[0]
    rows = lax.broadcasted_iota(jnp.int32, x.shape, 0)
    if down:
        return jnp.where(rows >= 1, pltpu.roll(x, 1, 0), 0.0)
    return jnp.where(rows < t - 1, pltpu.roll(x, t - 1, 0), 0.0)


def _token_shift(src, mu_prev, mu_next, first):
    t, n = src.shape[0], mu_prev.shape[1]

    def body(p_ref, mp_ref, mn_ref, o_ref):
        x = p_ref[...]
        o_ref[...] = x + mp_ref[...] * (_shift_rows(x, True) - x) + mn_ref[...] * (_shift_rows(x, False) - x)

    col = pl.BlockSpec((t, 128), lambda j: (0, j))
    par = pl.BlockSpec((1, 128), lambda j: (0, j))
    return pl.pallas_call(
        body, name="token_shift", grid=(n // 128,),
        in_specs=[pl.BlockSpec((t, 128), lambda j: (0, j + first)), par, par], out_specs=col,
        out_shape=jax.ShapeDtypeStruct((t, n), F32), compiler_params=_params(("parallel",)),
    )(src, mu_prev, mu_next)


def _token_shift_bwd(src, mu_prev, mu_next, dps, first):
    t, n = dps.shape

    def body(p_ref, mp_ref, mn_ref, d_ref, dp_ref, dmp_ref, dmn_ref):
        x, d, mp, mn = p_ref[...], d_ref[...], mp_ref[...], mn_ref[...]
        dp_ref[...] = d * (1.0 - mp - mn) + _shift_rows(d * mp, False) + _shift_rows(d * mn, True)
        dmp_ref[...] = jnp.sum(d * (_shift_rows(x, True) - x), axis=0, keepdims=True)
        dmn_ref[...] = jnp.sum(d * (_shift_rows(x, False) - x), axis=0, keepdims=True)

    col = pl.BlockSpec((t, 128), lambda j: (0, j))
    par = pl.BlockSpec((1, 128), lambda j: (0, j))
    return pl.pallas_call(
        body, name="token_shift_bwd", grid=(n // 128,),
        in_specs=[pl.BlockSpec((t, 128), lambda j: (0, j + first)), par, par, col],
        out_specs=[col, par, par],
        out_shape=[jax.ShapeDtypeStruct((t, n), F32), jax.ShapeDtypeStruct((1, n), F32),
                   jax.ShapeDtypeStruct((1, n), F32)],
        compiler_params=_params(("parallel",)),
    )(src, mu_prev, mu_next, dps)


N_SEG = 32
S5_LANES = 256
S5_BLOCKS = S5_CH // S5_LANES
S5_BLOCK_GROUPS = S5_LANES // S5_STATE
S5_BAND = S5_BLOCK_GROUPS * S5_GROUP
S5_PER_IN = 128 // S5_BAND


def _scan_in_place(sr_ref, si_ref, ar, ai, carry_ref, reverse):
    seg_len = sr_ref.shape[0] // N_SEG
    ng = N_SEG // 8

    def rows(i, grp):
        first = (seg_len - 1 - i if reverse else i) * N_SEG + 8 * grp
        return pl.ds(pl.multiple_of(first, 8), 8)

    zero = jnp.zeros((8, S5_LANES), F32)
    one = jnp.ones((8, S5_LANES), F32)

    def local(i, c):
        pr, pi = c[-2:]
        out = []
        for grp in range(ng):
            sr, si = c[2 * grp], c[2 * grp + 1]
            nr = ar * sr - ai * si + sr_ref[rows(i, grp), :]
            ni = ar * si + ai * sr + si_ref[rows(i, grp), :]
            sr_ref[rows(i, grp), :] = nr
            si_ref[rows(i, grp), :] = ni
            out += [nr, ni]
        return tuple(out) + (ar * pr - ai * pi, ar * pi + ai * pr)

    ends = lax.fori_loop(0, seg_len, local, (zero,) * (2 * ng) + (one, zero))
    qr, qi = ends[-2][0:1], ends[-1][0:1]
    order = list(range(N_SEG - 1, -1, -1)) if reverse else list(range(N_SEG))
    cr = jnp.zeros((1, S5_LANES), F32)
    ci = jnp.zeros((1, S5_LANES), F32)
    for j in order:
        carry_ref[j:j + 1, :] = cr
        carry_ref[N_SEG + j:N_SEG + j + 1, :] = ci
        grp, sub = divmod(j, 8)
        lr, li = ends[2 * grp][sub:sub + 1], ends[2 * grp + 1][sub:sub + 1]
        cr, ci = lr + qr * cr - qi * ci, li + qr * ci + qi * cr
    carries = [(carry_ref[8 * grp:8 * grp + 8, :], carry_ref[N_SEG + 8 * grp:N_SEG + 8 * grp + 8, :])
               for grp in range(ng)]

    def fix(i, c):
        pr, pi = c
        npr, npi = ar * pr - ai * pi, ar * pi + ai * pr
        for grp in range(ng):
            cr8, ci8 = carries[grp]
            sr_ref[rows(i, grp), :] = sr_ref[rows(i, grp), :] + npr * cr8 - npi * ci8
            si_ref[rows(i, grp), :] = si_ref[rows(i, grp), :] + npr * ci8 + npi * cr8
        return npr, npi

    lax.fori_loop(0, seg_len, fix, (one, zero))


def _interleave(x):
    t, c = x.shape
    return jnp.transpose(x.reshape(N_SEG, t // N_SEG, c), (1, 0, 2)).reshape(t, c)


def _deinterleave(x):
    t, c = x.shape
    return jnp.transpose(x.reshape(t // N_SEG, N_SEG, c), (1, 0, 2)).reshape(t, c)


def _lag_sums(lr_ref, li_ref, sr_ref, si_ref, earlier):
    t = lr_ref.shape[0]
    body, edge = pl.ds(N_SEG, t - N_SEG), pl.ds(0, N_SEG)
    far = pl.ds(t - N_SEG, N_SEG)
    rows = lax.broadcasted_iota(jnp.int32, (N_SEG, S5_LANES), 0)
    if earlier:
        lam_main, s_main, lam_edge = body, pl.ds(0, t - N_SEG), edge
        wrap = lambda ref: jnp.where(rows >= 1, pltpu.roll(ref[far, :], 1, 0), 0.0)
    else:
        lam_main, s_main, lam_edge = pl.ds(0, t - N_SEG), body, far
        wrap = lambda ref: jnp.where(rows < N_SEG - 1, pltpu.roll(ref[edge, :], N_SEG - 1, 0), 0.0)
    lr, li, sr, si = lr_ref[lam_main, :], li_ref[lam_main, :], sr_ref[s_main, :], si_ref[s_main, :]
    er, ei, pr, pi = lr_ref[lam_edge, :], li_ref[lam_edge, :], wrap(sr_ref), wrap(si_ref)
    re = jnp.sum(lr * sr + li * si, axis=0, keepdims=True) + jnp.sum(er * pr + ei * pi, axis=0, keepdims=True)
    im = jnp.sum(li * sr - lr * si, axis=0, keepdims=True) + jnp.sum(ei * pr - er * pi, axis=0, keepdims=True)
    return re, im


def _dot_bf16(a, b, dims=(((1,), (0,)), ((), ()))):
    return lax.dot_general(a.astype(BF16), b.astype(BF16), dims, preferred_element_type=F32)


NT_DIMS = (((1,), (1,)), ((), ()))
TN_DIMS = (((0,), (0,)), ((), ()))


def _s5_specs(t):
    def at(i, q):
        return (S5_PER_IN * i + q, 0, 0)

    blk = pl.BlockSpec((None, t, S5_LANES), at)
    b_mat = pl.BlockSpec((None, 128, S5_LANES), at)
    c_mat = pl.BlockSpec((None, S5_LANES, 128), at)
    vec = pl.BlockSpec((None, 1, S5_LANES), at)
    chan = pl.BlockSpec((t, 128), lambda i, q: (0, i))
    return blk, b_mat, c_mat, vec, chan


S5_GRID = (S5_BLOCKS // S5_PER_IN, S5_PER_IN)


def _s5_forward(name, u, b_re, b_im, l_re, l_im, reverse, other=None, c_re=None, c_im_neg=None):
    t = u.shape[0]
    project = other is not None
    blk, b_mat, c_mat, vec, chan = _s5_specs(t)

    def body(*refs):
        u_ref, br_ref, bi_ref, lr_ref, li_ref = refs[:5]
        if project:
            or_ref, oi_ref, cr_ref, ci_ref, sr_ref, si_ref, y_ref, carry_ref = refs[5:]
        else:
            sr_ref, si_ref, carry_ref = refs[5:]
        uv = u_ref[...]
        sr_ref[...] = _dot_bf16(uv, br_ref[...])
        si_ref[...] = _dot_bf16(uv, bi_ref[...])
        ar = jnp.broadcast_to(lr_ref[...], (8, S5_LANES))
        ai = jnp.broadcast_to(li_ref[...], (8, S5_LANES))
        _scan_in_place(sr_ref, si_ref, ar, ai, carry_ref, reverse)
        if project:
            y = (_dot_bf16(sr_ref[...] + or_ref[...], cr_ref[...])
                 + _dot_bf16(si_ref[...] + oi_ref[...], ci_ref[...]))

            @pl.when(pl.program_id(1) == 0)
            def _():
                y_ref[...] = y

            @pl.when(pl.program_id(1) != 0)
            def _():
                y_ref[...] += y

    state = jax.ShapeDtypeStruct((S5_BLOCKS, t, S5_LANES), F32)
    ins = [u, b_re, b_im, l_re, l_im] + ([other[0], other[1], c_re, c_im_neg] if project else [])
    in_specs = [chan, b_mat, b_mat, vec, vec] + ([blk, blk, c_mat, c_mat] if project else [])
    return pl.pallas_call(
        body, name=name, grid=S5_GRID, in_specs=in_specs,
        out_specs=[blk, blk] + ([chan] if project else []),
        out_shape=[state, state] + ([jax.ShapeDtypeStruct((t, S5_WIDTH), F32)] if project else []),
        scratch_shapes=[pltpu.VMEM((2 * N_SEG, S5_LANES), F32)],
        compiler_params=_params(("arbitrary", "arbitrary")),
    )(*ins)


def _s5_backward(name, dy, u, du_in, states, other, b_re, b_im, c_re, c_im_neg, l_re, l_im, reverse):
    t = u.shape[0]
    with_c = other is not None
    blk, b_mat, c_mat, vec, chan = _s5_specs(t)

    def body(*refs):
        dy_ref, u_ref, du_in_ref, sr_ref, si_ref = refs[:5]
        pos = 5
        if with_c:
            or_ref, oi_ref = refs[5:7]
            pos = 7
        br_ref, bi_ref, cr_ref, ci_ref, lr_ref, li_ref = refs[pos:pos + 6]
        outs = refs[pos + 6:]
        du_ref, dbr_ref, dbi_ref, dlr_ref, dli_ref = outs[:5]
        lam_r, lam_i, carry_ref = outs[-3:]
        dyv, uv = dy_ref[...], u_ref[...]
        lam_r[...] = _dot_bf16(dyv, cr_ref[...], NT_DIMS)
        lam_i[...] = _dot_bf16(dyv, ci_ref[...], NT_DIMS)
        ar = jnp.broadcast_to(lr_ref[...], (8, S5_LANES))
        ai = -jnp.broadcast_to(li_ref[...], (8, S5_LANES))
        _scan_in_place(lam_r, lam_i, ar, ai, carry_ref, not reverse)
        lr, li = lam_r[...], lam_i[...]
        dlr_ref[...], dli_ref[...] = _lag_sums(lam_r, lam_i, sr_ref, si_ref, not reverse)
        dbr_ref[...] = _dot_bf16(uv, lr, TN_DIMS)
        dbi_ref[...] = _dot_bf16(uv, li, TN_DIMS)
        du = _dot_bf16(lr, br_ref[...], NT_DIMS) + _dot_bf16(li, bi_ref[...], NT_DIMS)

        @pl.when(pl.program_id(1) == 0)
        def _():
            du_ref[...] = du_in_ref[...] + du

        @pl.when(pl.program_id(1) != 0)
        def _():
            du_ref[...] += du

        if with_c:
            dcr_ref, dci_ref = outs[5:7]
            dcr_ref[...] = _dot_bf16(sr_ref[...] + or_ref[...], dyv, TN_DIMS)
            dci_ref[...] = _dot_bf16(si_ref[...] + oi_ref[...], dyv, TN_DIMS)

    b_mats = jax.ShapeDtypeStruct((S5_BLOCKS, 128, S5_LANES), F32)
    c_mats = jax.ShapeDtypeStruct((S5_BLOCKS, S5_LANES, 128), F32)
    vecs = jax.ShapeDtypeStruct((S5_BLOCKS, 1, S5_LANES), F32)
    ins = [dy, u, du_in, states[0], states[1]] + ([other[0], other[1]] if with_c else [])
    ins += [b_re, b_im, c_re, c_im_neg, l_re, l_im]
    in_specs = ([chan, chan, chan, blk, blk] + ([blk, blk] if with_c else [])
                + [b_mat, b_mat, c_mat, c_mat, vec, vec])
    return pl.pallas_call(
        body, name=name, grid=S5_GRID, in_specs=in_specs,
        out_specs=[chan, b_mat, b_mat, vec, vec] + ([c_mat, c_mat] if with_c else []),
        out_shape=[jax.ShapeDtypeStruct((t, S5_WIDTH), F32), b_mats, b_mats, vecs, vecs]
        + ([c_mats, c_mats] if with_c else []),
        scratch_shapes=[pltpu.VMEM((t, S5_LANES), F32), pltpu.VMEM((t, S5_LANES), F32),
                        pltpu.VMEM((2 * N_SEG, S5_LANES), F32)],
        compiler_params=_params(("arbitrary", "arbitrary")),
    )(*ins)


def _ein(passes, spec, a, b):
    if passes == 6:
        return jnp.einsum(spec, a, b, precision=HI, preferred_element_type=F32)
    a_hi, b_hi = a.astype(BF16), b.astype(BF16)
    if passes == 1:
        return jnp.einsum(spec, a_hi, b_hi, preferred_element_type=F32)
    a_lo = (a - a_hi.astype(F32)).astype(BF16)
    b_lo = (b - b_hi.astype(F32)).astype(BF16)
    cross = jnp.einsum(spec, a_hi, b_lo, preferred_element_type=F32)
    if spec.startswith('hik'):
        m = a.shape[1]
        stacked = jnp.einsum(spec, jnp.concatenate([a_hi, a_lo], axis=1), b_hi, preferred_element_type=F32)
        return stacked[:, :m] + stacked[:, m:] + cross
    return (jnp.einsum(spec, a_hi, b_hi, preferred_element_type=F32) + cross
            + jnp.einsum(spec, a_lo, b_hi, preferred_element_type=F32))


@jax.custom_vjp
def _tri_mm(tri, tri_t, z):
    n = z.shape[2]
    hi = z.astype(BF16)
    rest = z - hi.astype(F32)
    mid = rest.astype(BF16)
    lo = (rest - mid.astype(F32)).astype(BF16)
    out = jnp.einsum('hik,hkj->hij', tri.astype(BF16), jnp.concatenate([hi, mid, lo], axis=2),
                     preferred_element_type=F32)
    return out[:, :, :n] + out[:, :, n:2 * n] + out[:, :, 2 * n:]


def _tri_mm_bwd(res, g):
    tri, tri_t = res
    return jnp.zeros_like(tri), jnp.zeros_like(tri_t), _tri_mm(tri_t, tri, g)


_tri_mm.defvjp(lambda tri, tri_t, z: (_tri_mm(tri, tri_t, z), (tri, tri_t)), _tri_mm_bwd)


def _chunk_cumsum(lw, incl, incl_t):
    shape = (lw.shape[0],) + incl.shape
    return _tri_mm(jnp.broadcast_to(incl.astype(F32), shape), jnp.broadcast_to(incl_t.astype(F32), shape), lw)


@functools.partial(jax.custom_vjp, nondiff_argnums=(0,))
def _bmm(p, a, b):
    return _ein(p, 'hik,hkj->hij', a, b)


@functools.partial(jax.custom_vjp, nondiff_argnums=(0,))
def _bmm_nt(p, a, b):
    return _ein(p, 'hik,hjk->hij', a, b)


@functools.partial(jax.custom_vjp, nondiff_argnums=(0,))
def _bmm_tn(p, a, b):
    return _ein(p, 'hki,hkj->hij', a, b)


_bmm.defvjp(lambda p, a, b: (_bmm(p, a, b), (a, b)),
            lambda p, res, g: (_bmm_nt(p, g, res[1]), _bmm_tn(p, res[0], g)))
_bmm_nt.defvjp(lambda p, a, b: (_bmm_nt(p, a, b), (a, b)),
               lambda p, res, g: (_bmm(p, g, res[1]), _bmm_tn(p, g, res[0])))
_bmm_tn.defvjp(lambda p, a, b: (_bmm_tn(p, a, b), (a, b)),
               lambda p, res, g: (_bmm_nt(p, res[1], g), _bmm(p, res[0], g)))


@jax.custom_vjp
def _split_rows(x):
    c = x.shape[1] // 2
    return x[:, :c], x[:, c:]


_split_rows.defvjp(lambda x: (_split_rows(x), None), lambda _, g: (jnp.concatenate(g, axis=1),))


def _stack_rows(a, b):
    return jnp.concatenate([a, b], axis=1)


def _nilpotent_inverse(l_mat):
    c = l_mat.shape[1]
    ps = RK_PASSES["solve"]
    row = lax.broadcasted_iota(jnp.int32, (c, c), 0)
    col = lax.broadcasted_iota(jnp.int32, (c, c), 1)
    x = -l_mat
    inv = jnp.where(row == col, 1.0, 0.0) + x
    power = _bmm(ps, x, x)
    span = 2
    while 2 * span < c:
        step, power = _split_rows(_bmm(ps, _stack_rows(inv, power), power))
        inv = inv + step
        span *= 2
    return inv + _bmm(ps, inv, power)


@jax.custom_vjp
def _solve_with(inv, l_mat, rhs):
    return _bmm(RK_PASSES["solve"], inv, rhs)


def _solve_with_fwd(inv, l_mat, rhs):
    u = _bmm(RK_PASSES["solve"], inv, rhs)
    return u, (inv, u)


def _solve_with_bwd(res, g):
    inv, u = res
    d_rhs = _bmm_tn(RK_PASSES["solve"], inv, g)
    return jnp.zeros_like(inv), -_bmm_nt(RK_PASSES["solve"], d_rhs, u), d_rhs


_solve_with.defvjp(_solve_with_fwd, _solve_with_bwd)


def _rk_chunk(s0, r, lw, k, v, kk, a, reverse, inv=None):
    h, c, n = r.shape
    row = lax.broadcasted_iota(jnp.int32, (c, c), 0)
    col = lax.broadcasted_iota(jnp.int32, (c, c), 1)
    incl = (row <= col) if reverse else (row >= col)
    strict = (row < col) if reverse else (row > col)
    cum = _chunk_cumsum(lw, incl, (row >= col) if reverse else (row <= col))
    g_in = jnp.exp(cum)
    g_inv = jnp.exp(-cum)
    kap = kk * jnp.exp(cum - lw)
    beta = kk * a * g_inv
    kt = k * g_inv
    rt = r * g_in
    p, ps = RK_PASSES, RK_PASSES["solve"]
    both = _stack_rows(kap, rt)
    kap_beta, rt_beta = _split_rows(_bmm_nt(ps, both, beta))
    kap_kt, rt_kt = _split_rows(_bmm_nt(p["kt"], both, kt))
    kap_s0, rt_s0 = _split_rows(_bmm_nt(p["s0"], both, s0))
    l_mat = jnp.where(strict, kap_beta, 0.0)
    rhs = kap_s0 + _bmm(p["akk_v"], jnp.where(strict, kap_kt, 0.0), v)
    if inv is None:
        inv = lax.stop_gradient(_nilpotent_inverse(l_mat))
    u = _solve_with(inv, l_mat, rhs)
    y = (rt_s0 + _bmm(p["ark_v"], jnp.where(incl, rt_kt, 0.0), v)
         - _bmm(p["arb_u"], jnp.where(incl, rt_beta, 0.0), u))
    s1 = ((s0 + _bmm_tn(p["state"], _stack_rows(v, -u), _stack_rows(kt, beta)))
          * jnp.exp(jnp.sum(lw, axis=1, keepdims=True)))
    return y, s1, inv


class _Plan:
    def __init__(self, arrays, out_shape, sems, start, wait, finish):
        self.arrays, self.out_shape, self.sems = list(arrays), list(out_shape), list(sems)
        self.start, self.wait, self.finish = start, wait, finish


_NO_PLAN = _Plan([], [], [], lambda *_: None, lambda *_: None, lambda outs: [])


def _join_plans(plans):
    def cut(seq, sizes):
        out, pos = [], 0
        for s in sizes:
            out.append(seq[pos:pos + s])
            pos += s
        return out

    n_arr, n_sem = [len(p.arrays) for p in plans], [len(p.sems) for p in plans]

    def run(which):
        def go(in_refs, out_refs, sems):
            for p, i, o, s in zip(plans, cut(in_refs, n_arr), cut(out_refs, n_arr), cut(sems, n_sem)):
                getattr(p, which)(i, o, s)
        return go

    return _Plan([a for p in plans for a in p.arrays], [s for p in plans for s in p.out_shape],
                 [s for p in plans for s in p.sems], run("start"), run("wait"),
                 lambda outs: [p.finish(o) for p, o in zip(plans, cut(outs, n_arr))])


def _split_heads(x):
    return jnp.stack([x[:, RK_HEAD * i:RK_HEAD * (i + 1)] for i in range(RK_HEADS)], axis=0)


def _store_heads(ref, x):
    for i in range(RK_HEADS):
        ref[:, RK_HEAD * i:RK_HEAD * (i + 1)] = x[i]


def _rk_core_fwd(name, r, lw, k, v, kk, a, reverse, chunk, hosted=None):
    t = r.shape[0]
    h, n = RK_HEADS, RK_HEAD
    nc = t // chunk

    def idx(i):
        return nc - 1 - i if reverse else i

    hosted = hosted or _NO_PLAN
    nh = len(hosted.arrays)

    def body(r_ref, lw_ref, k_ref, v_ref, kk_ref, a_ref, *rest):
        host_in, (y_ref, ck_ref, inv_ref), host_out = rest[:nh], rest[nh:nh + 3], rest[nh + 3:2 * nh + 3]
        s_ref, sems = rest[2 * nh + 3], rest[2 * nh + 4:]

        @pl.when(pl.program_id(0) == 0)
        def _():
            s_ref[...] = jnp.zeros_like(s_ref)
            hosted.start(host_in, host_out, sems)

        s0 = s_ref[...]
        ck_ref[0] = s0
        ops = [_split_heads(ref[...]) for ref in (r_ref, lw_ref, k_ref, v_ref, kk_ref, a_ref)]
        y, s1, inv = _rk_chunk(s0, *ops, reverse)
        _store_heads(y_ref, y)
        inv_ref[0] = inv
        s_ref[...] = s1

        @pl.when(pl.program_id(0) == nc - 1)
        def _():
            hosted.wait(host_in, host_out, sems)

    blk = pl.BlockSpec((chunk, RK_WIDTH), lambda i: (idx(i), 0))
    any_spec = pl.BlockSpec(memory_space=pl.ANY)
    res = pl.pallas_call(
        body, name=name, grid=(nc,), in_specs=[blk] * 6 + [any_spec] * nh,
        out_specs=[blk, pl.BlockSpec((1, h, n, n), lambda i: (idx(i), 0, 0, 0)),
                   pl.BlockSpec((1, h, chunk, chunk), lambda i: (idx(i), 0, 0, 0))] + [any_spec] * nh,
        out_shape=[jax.ShapeDtypeStruct((t, RK_WIDTH), F32), jax.ShapeDtypeStruct((nc, h, n, n), F32),
                   jax.ShapeDtypeStruct((nc, h, chunk, chunk), F32)] + hosted.out_shape,
        scratch_shapes=[pltpu.VMEM((h, n, n), F32)] + hosted.sems,
        compiler_params=_params(("arbitrary",)),
    )(r, lw, k, v, kk, a, *hosted.arrays)
    return res[0], (res[1], res[2]), hosted.finish(res[3:])


def _rk_core_bwd(name, r, lw, k, v, kk, a, ck, dy, reverse, chunk, hosted=None):
    t = r.shape[0]
    h, n = RK_HEADS, RK_HEAD
    nc = t // chunk
    hosted = hosted or _NO_PLAN
    nh = len(hosted.arrays)

    def idx(i):
        return i if reverse else nc - 1 - i

    def body(r_ref, lw_ref, k_ref, v_ref, kk_ref, a_ref, ck_ref, inv_ref, dy_ref, *rest):
        host_in, out_refs, host_out = rest[:nh], rest[nh:nh + 6], rest[nh + 6:2 * nh + 6]
        ds_ref, sems = rest[2 * nh + 6], rest[2 * nh + 7:]

        @pl.when(pl.program_id(0) == 0)
        def _():
            ds_ref[...] = jnp.zeros_like(ds_ref)
            hosted.start(host_in, host_out, sems)

        inv = inv_ref[0]

        def fn(*operands):
            return _rk_chunk(*operands, reverse=reverse, inv=inv)[:2]

        ops = [_split_heads(ref[...]) for ref in (r_ref, lw_ref, k_ref, v_ref, kk_ref, a_ref)]
        _, vjp = jax.vjp(fn, ck_ref[0], *ops)
        grads = vjp((_split_heads(dy_ref[...]), ds_ref[...]))
        ds_ref[...] = grads[0]
        for o_ref, g in zip(out_refs, grads[1:]):
            _store_heads(o_ref, g)

        @pl.when(pl.program_id(0) == nc - 1)
        def _():
            hosted.wait(host_in, host_out, sems)

    blk = pl.BlockSpec((chunk, RK_WIDTH), lambda i: (idx(i), 0))
    any_spec = pl.BlockSpec(memory_space=pl.ANY)
    res = pl.pallas_call(
        body, name=name, grid=(nc,),
        in_specs=[blk] * 6 + [pl.BlockSpec((1, h, n, n), lambda i: (idx(i), 0, 0, 0)),
                              pl.BlockSpec((1, h, chunk, chunk), lambda i: (idx(i), 0, 0, 0)), blk]
        + [any_spec] * nh,
        out_specs=[blk] * 6 + [any_spec] * nh,
        out_shape=[jax.ShapeDtypeStruct((t, RK_WIDTH), F32)] * 6 + hosted.out_shape,
        scratch_shapes=[pltpu.VMEM((h, n, n), F32)] + hosted.sems,
        compiler_params=_params(("arbitrary",)),
    )(r, lw, k, v, kk, a, ck[0], ck[1], dy, *hosted.arrays)
    return res[:6], hosted.finish(res[6:])


def _s5_band_place():
    return jax.nn.one_hot(jnp.arange(S5_BLOCKS) % S5_PER_IN, S5_PER_IN, dtype=F32)


def _s5_in_blocks(bbar):
    gb = S5_BLOCK_GROUPS
    b = jnp.transpose(bbar.reshape(S5_BLOCKS, gb, S5_STATE, S5_GROUP), (0, 1, 3, 2))
    band = jnp.einsum('jghp,gk->jghkp', b, jnp.eye(gb, dtype=F32)).reshape(S5_BLOCKS, S5_BAND, S5_LANES)
    return jnp.einsum('jrc,jq->jqrc', band, _s5_band_place()).reshape(S5_BLOCKS, 128, S5_LANES)


def _s5_in_unblock(mats):
    gb = S5_BLOCK_GROUPS
    band = jnp.einsum('jqrc,jq->jrc', mats.reshape(S5_BLOCKS, S5_PER_IN, S5_BAND, S5_LANES), _s5_band_place())
    diag = jnp.einsum('jghgp->jghp', band.reshape(S5_BLOCKS, gb, S5_GROUP, gb, S5_STATE))
    return jnp.transpose(diag, (0, 1, 3, 2)).reshape(S5_CH, S5_GROUP)


def _s5_out_blocks(c):
    gb = S5_BLOCK_GROUPS
    ct = jnp.transpose(c.reshape(S5_BLOCKS, gb, S5_GROUP, S5_STATE), (0, 1, 3, 2))
    band = jnp.einsum('jgph,gk->jgpkh', ct, jnp.eye(gb, dtype=F32)).reshape(S5_BLOCKS, S5_LANES, S5_BAND)
    return jnp.einsum('jrc,jq->jrqc', band, _s5_band_place()).reshape(S5_BLOCKS, S5_LANES, 128)


def _s5_out_unblock(mats):
    gb = S5_BLOCK_GROUPS
    band = jnp.einsum('jrqc,jq->jrc', mats.reshape(S5_BLOCKS, S5_LANES, S5_PER_IN, S5_BAND), _s5_band_place())
    diag = jnp.einsum('jgpgh->jgph', band.reshape(S5_BLOCKS, gb, S5_STATE, gb, S5_GROUP))
    return jnp.transpose(diag, (0, 1, 3, 2)).reshape(S5_GROUPS, S5_GROUP, S5_STATE)


def _head_indicator():
    ch = lax.broadcasted_iota(jnp.int32, (RK_WIDTH, 128), 0) // RK_HEAD
    hd = lax.broadcasted_iota(jnp.int32, (RK_WIDTH, 128), 1)
    seg = (ch == hd).astype(F32)
    return seg, seg.T


def _add_epilogue(acc, e):
    return (acc + e,)


def _local_step(x, target, mod, wt, chunk=RK_CHUNK, ffn_shards=None, mixer_shards=None):
    t = x.shape[0]
    wt = dict(wt)
    sh1, sc1, gt1, sh2, sc2, gt2 = mod
    seg, seg_t = _head_indicator()
    g = {}

    (h1,) = _rowwise("norm1", _norm_mod_fn, [x], [wt["norm1_gain"], sc1, sh1], [(D_MODEL, BF16)], 256)
    if mixer_shards is None:
        proj = _matmul("proj", h1, wt["w_in"])
    else:
        proj, gathered = _matmul("proj", h1, wt["w_in"], hosted=_gather_halves_plan(mixer_shards[0]))
        wt.update(mixer_shards[1](gathered))
    u = proj[:, :S5_WIDTH]
    ps = _token_shift(proj, wt["mu_prev"], wt["mu_next"], first=S5_WIDTH // 128)
    r, k, v = ps[:, :1024], ps[:, 1024:2048], ps[:, 2048:3072]
    wdn, adn, gdn = ps[:, 3072:3200], ps[:, 3200:3328], ps[:, 3328:RK_PAD]

    prep_rows = [wt["lam_re"][0], wt["lam_im"][0], wt["log_step"][0], wt["lam_re"][1], wt["lam_im"][1],
                 wt["log_step"][1], wt["b_re"], wt["b_im"]]
    col1, col16 = (1, F32), (S5_GROUP, F32)
    prep = _rowwise("s5_prep", _s5_prep_fn, prep_rows, [], [col1, col1, col16, col16] * 2, 512)
    lbar = [tuple(prep[4 * d + q].reshape(S5_BLOCKS, 1, S5_LANES) for q in range(2)) for d in range(2)]
    b_blk = [tuple(_s5_in_blocks(prep[4 * d + 2 + q]) for q in range(2)) for d in range(2)]
    c_blk = (_s5_out_blocks(wt["c_re"]), -_s5_out_blocks(wt["c_im"]))
    u_il = _interleave(u)
    state0 = _s5_forward("s5_fwd0", u_il, *b_blk[0], *lbar[0], reverse=False)
    s1_re, s1_im, ylin_il = _s5_forward("s5_fwd1", u_il, *b_blk[1], *lbar[1], reverse=True, other=state0,
                                        c_re=c_blk[0], c_im_neg=c_blk[1])
    ylin = _deinterleave(ylin_il)
    states = [tuple(state0), (s1_re, s1_im)]
    s5_par = [wt["s5_d"], wt["s5_w_glu"], wt["s5_b_glu"]]
    (o_s5,) = _rowwise("s5_out", _s5_out_fn, [ylin, u], s5_par, [(S5_WIDTH, BF16)], 256)

    pre_par = [wt["w0"][0], wt["w0"][1], wt["w_up"][0], wt["w_up"][1], wt["a0"][0], wt["a0"][1],
               wt["a_up"][0], wt["a_up"][1], wt["g_up"], wt["k_k"], wt["k_a"]]
    pre = _rowwise("rk_pre", _rk_pre_fn, [k, wdn, adn, gdn], pre_par + [seg, seg_t], [(RK_WIDTH, F32)] * 8, 256)
    kk, lw, kd, act, gate = pre[0], pre[1:3], pre[3:5], pre[5:7], pre[7]
    core_in, ys, cks = [], [], []
    for d in range(2):
        ops = (r, lw[d], kd[d], v, kk, act[d])
        plan = _gather_halves_plan([ffn_shards[d]]) if ffn_shards is not None else None
        y, ck, gathered = _rk_core_fwd(f"rk_core{d}", *ops, reverse=(d == 1), chunk=min(chunk, t), hosted=plan)
        if gathered:
            wt["ffn_w1" if d == 0 else "ffn_w2"] = gathered[0] if d == 0 else gathered[0].reshape(FFN, D_MODEL)
        core_in.append(ops)
        ys.append(y)
        cks.append(ck)
    post_rows = [ys[0], ys[1], r, v, kd[0], kd[1], gate]
    post_par = [wt["ln_gain"], wt["ln_bias"], wt["r_k"]]
    (o_rk,) = _rowwise("rk_post", _rk_post_fn, post_rows, post_par + [seg, seg_t], [(RK_WIDTH, BF16)], 256)

    o = jnp.concatenate([o_s5, o_rk], axis=1)
    mixed = _matmul("mix_out", o, wt["w_out"])
    n2_par = [gt1, wt["norm2_gain"], sc2, sh2]
    x1, h2 = _rowwise("norm2", _resid_norm_mod_fn, [x, mixed], n2_par, [(D_MODEL, F32), (D_MODEL, BF16)], 256)
    f1, hid = _matmul("ffn1", h2, wt["ffn_w1"], out_dtypes=(F32, BF16), chips="b",
                      epilogue=lambda acc: (acc, jnp.square(jnp.maximum(acc, 0.0))))
    ffn = _matmul("ffn2", hid, wt["ffn_w2"])

    ones = jnp.ones((t, 1), F32)
    loss_rows, dx1, dffn, g_gt2, g["final_gain"] = _rowwise_vjp(
        "loss", _loss_fn, [x1, ffn, target], [gt2, wt["final_gain"]], [[ones]], [0, 1], [0, 1], 256, emit=(0,),
        row_grad_dtypes=[F32, BF16])
    df1 = _matmul("ffn2_dx", dffn, wt["ffn_w2"], tb=True, extras=(f1,), out_dtypes=(BF16,),
                  epilogue=lambda acc, f: (acc * (2.0 * jnp.maximum(f, 0.0)),))
    g["ffn_w2"] = _matmul("ffn2_dw", hid, dffn, ta=True)
    if ffn_shards is None:
        g["ffn_w1"] = _matmul("ffn1_dw", h2, df1, ta=True, chips="out")
        dh2 = _matmul("ffn1_dx", df1, wt["ffn_w1"], tb=True, chips="b_t")
    else:
        piece2 = g.pop("ffn_w2").reshape(N_CHIPS, -1, D_MODEL)
        piece1, (other2,) = _matmul("ffn1_dw", h2, df1, ta=True, chips="out", hosted=_other_half_plan([piece2]))
        dh2, (other1,) = _matmul("ffn1_dx", df1, wt["ffn_w1"], tb=True, chips="b_t",
                                 hosted=_other_half_plan([piece1]))
        ffn_sums = [_pair_sum("pair_" + n, piece, other, BF16)
                    for n, piece, other in zip(FFN_SHARDED, (piece1, piece2), (other1, other2))]
    dx_a, dmixed, g_gt1, g["norm2_gain"], g_sc2, g_sh2 = _rowwise_vjp(
        "norm2_bwd", _resid_norm_mod_fn, [x, mixed], n2_par, [[dx1], [dh2]], [0, 1], [0, 1, 2, 3], 256,
        row_grad_dtypes=[F32, BF16])
    do = _matmul("mix_out_dx", dmixed, wt["w_out"], tb=True)
    g["w_out"] = _matmul("mix_out_dw", o, dmixed, ta=True)
    do_s5, do_rk = do[:, :S5_WIDTH], do[:, S5_WIDTH:]

    dylin, du, g["s5_d"], g["s5_w_glu"], g["s5_b_glu"] = _rowwise_vjp(
        "s5_out_bwd", _s5_out_fn, [ylin, u], s5_par, [[do_s5]], [0, 1], [0, 1, 2], 256)
    prep_cts = []
    dylin_il, du_il = _interleave(dylin), _interleave(du)
    for d in range(2):
        res = _s5_backward(f"s5_bwd{d}", dylin_il, u_il, du_il, states[d], states[1] if d == 0 else None,
                           *b_blk[d], *c_blk, *lbar[d], reverse=(d == 1))
        du_il, db_re, db_im, dl_re, dl_im = res[:5]
        if d == 0:
            g["c_re"], g["c_im"] = _s5_out_unblock(res[5]), -_s5_out_unblock(res[6])
        prep_cts += [[dl_re.reshape(S5_CH, 1)], [dl_im.reshape(S5_CH, 1)], [_s5_in_unblock(db_re)],
                     [_s5_in_unblock(db_im)]]
    du = _deinterleave(du_il)
    pg = _rowwise_vjp("s5_prep_bwd", _s5_prep_fn, prep_rows, [], prep_cts, list(range(8)), [], 512)
    g["lam_re"], g["lam_im"], g["log_step"] = (pg[0], pg[3]), (pg[1], pg[4]), (pg[2], pg[5])
    g["b_re"], g["b_im"] = pg[6], pg[7]

    pb = _rowwise_vjp("rk_post_bwd", _rk_post_fn, post_rows, post_par, [[do_rk]], [0, 2, 3, 4, 5, 6], [0, 1, 2],
                      128, consts=[seg, seg_t])
    dy, dr_b, dv_b, dkd_b, dgate = pb[0], pb[1], pb[2], pb[3:5], pb[5]
    g["ln_gain"], g["ln_bias"], g["r_k"] = pb[6], pb[7], pb[8]
    cg = []
    for d in range(2):
        plan = None
        if ffn_shards is not None:
            plan = _exchange_plan([ffn_sums[d]], CHIP_PEERS, N_CHIPS, scatter=True)
        grads, arrived = _rk_core_bwd(f"rk_core{d}_bwd", *core_in[d], cks[d], dy, reverse=(d == 1),
                                      chunk=min(chunk, t), hosted=plan)
        g.setdefault("ffn_arrived", []).extend(arrived)
        cg.append(grads)
    pre_cts = [[cg[0][4], cg[1][4]], [cg[0][1]], [cg[1][1]], [cg[0][2], dkd_b[0]], [cg[1][2], dkd_b[1]],
               [cg[0][5]], [cg[1][5]], [dgate]]
    qb = _rowwise_vjp("rk_pre_bwd", _rk_pre_fn, [k, wdn, adn, gdn], pre_par, pre_cts, [0, 1, 2, 3],
                      list(range(11)), 128, consts=[seg, seg_t])
    dk, dwdn, dadn, dgdn = qb[:4]
    g["w0"], g["w_up"], g["a0"], g["a_up"] = (qb[4], qb[5]), (qb[6], qb[7]), (qb[8], qb[9]), (qb[10], qb[11])
    g["g_up"], g["k_k"], g["k_a"] = qb[12], qb[13], qb[14]
    dr, dv = _rowwise("rk_sum", lambda a, b, c, e, f, h: (a + b + c, e + f + h),
                      [cg[0][0], cg[1][0], dr_b, cg[0][3], cg[1][3], dv_b], [], [(RK_WIDTH, F32)] * 2, 256)
    dps = jnp.concatenate([dr, dk, dv, dwdn, dadn, dgdn], axis=1)
    dp, g["mu_prev"], g["mu_next"] = _token_shift_bwd(proj, wt["mu_prev"], wt["mu_next"], dps,
                                                      first=S5_WIDTH // 128)

    dproj = jnp.concatenate([du, dp], axis=1).astype(BF16)
    dh1 = _matmul("proj_dx", dproj, wt["w_in"], tb=True)
    g["w_in"] = _matmul("proj_dw", h1, dproj, ta=True)
    grad_x, g["norm1_gain"], g_sc1, g_sh1 = _rowwise_vjp(
        "norm1_bwd", _norm_mod_fn, [x], [wt["norm1_gain"], sc1, sh1], [[dh1]], [0], [0, 1, 2], 256,
        addends={0: dx_a})
    g["mod"] = [g_sh1, g_sc1, g_gt1, g_sh2, g_sc2, g_gt2]
    return loss_rows, grad_x, g


CHIP_PEERS = ((1, 0, 0), (0, 1, 0), (1, 1, 0))
ALL_PEERS = ((0, 0, 1), (0, 1, 0), (0, 1, 1), (1, 0, 0), (1, 0, 1), (1, 1, 0), (1, 1, 1))
CORE_PEER = ((0, 0, 1),)


def _exchange(name, arrays, peers, n_slots, scatter=False):
    return _run_plan(name, _exchange_plan(arrays, peers, n_slots, scatter))


def _run_plan(name, plan):
    na = len(plan.arrays)

    def body(*refs):
        plan.start(refs[:na], refs[na:2 * na], refs[2 * na:])
        plan.wait(refs[:na], refs[na:2 * na], refs[2 * na:])

    any_spec = pl.BlockSpec(memory_space=pl.ANY)
    return plan.finish(pl.pallas_call(
        body, name=name, in_specs=[any_spec] * na, out_specs=[any_spec] * na, out_shape=plan.out_shape,
        scratch_shapes=plan.sems,
    )(*plan.arrays))


def _exchange_plan(arrays, peers, n_slots, scatter=False):
    na, nm = len(arrays), len(peers)

    def ident(px, py, pc):
        return {8: 4 * px + 2 * py + pc, 4: 2 * px + py, 2: pc}[n_slots]

    def copies(in_refs, out_refs, sems):
        send_sems, recv_sems = sems
        x, y, c = lax.axis_index("x"), lax.axis_index("y"), lax.axis_index("c")
        me = ident(x, y, c)
        made = []
        for i in range(na):
            for j, (fx, fy, fc) in enumerate(peers):
                px, py, pc = (1 - x if fx else x), (1 - y if fy else y), (1 - c if fc else c)
                src = in_refs[i].at[ident(px, py, pc)] if scatter else in_refs[i]
                made.append(pltpu.make_async_remote_copy(
                    src_ref=src, dst_ref=out_refs[i].at[me],
                    send_sem=send_sems.at[i * nm + j], recv_sem=recv_sems.at[i * nm + j],
                    device_id=(px, py, pc), device_id_type=pl.DeviceIdType.MESH))
        return made

    def start(in_refs, out_refs, sems):
        for copy in copies(in_refs, out_refs, sems):
            copy.start()

    def wait(in_refs, out_refs, sems):
        for copy in copies(in_refs, out_refs, sems):
            copy.wait()

    def finish(outs):
        me = ident(lax.axis_index("x"), lax.axis_index("y"), lax.axis_index("c"))
        return [lax.dynamic_update_slice_in_dim(
            o, lax.dynamic_index_in_dim(a, me, 0, keepdims=True) if scatter else a[None], me, axis=0)
            for a, o in zip(arrays, outs)]

    out_shape = [jax.ShapeDtypeStruct(((n_slots,) + a.shape[1:]) if scatter else ((n_slots,) + a.shape), a.dtype)
                 for a in arrays]
    sems = [pltpu.SemaphoreType.DMA((na * nm,)), pltpu.SemaphoreType.DMA((na * nm,))]
    return _Plan(arrays, out_shape, sems, start, wait, finish)


def _gather_halves(name, arrays):
    return _run_plan(name, _gather_halves_plan(arrays))


def _gather_halves_plan(arrays):
    na = len(arrays)
    chips = ((1, 0), (0, 1), (1, 1))

    def over_ici(in_refs, out_refs, sems):
        ici_send, ici_recv = sems[:2]
        x, y, c = lax.axis_index("x"), lax.axis_index("y"), lax.axis_index("c")
        made = []
        for i in range(na):
            half = arrays[i].shape[0] // 2
            mine = pl.ds(pl.multiple_of(c * half, 8), half)
            for j, (fx, fy) in enumerate(chips):
                px, py = (1 - x if fx else x), (1 - y if fy else y)
                k = len(chips) * i + j
                made.append([pltpu.make_async_remote_copy(
                    src_ref=in_refs[i].at[mine], dst_ref=out_refs[i].at[chip, mine],
                    send_sem=ici_send.at[k], recv_sem=ici_recv.at[k],
                    device_id=(px, py, c), device_id_type=pl.DeviceIdType.MESH)
                    for chip in (2 * x + y, 2 * px + py)])
        return made

    def start(in_refs, out_refs, sems):
        for outgoing, _ in over_ici(in_refs, out_refs, sems):
            outgoing.start()

    def wait(in_refs, out_refs, sems):
        d2d_send, d2d_recv = sems[2:]
        x, y, c = lax.axis_index("x"), lax.axis_index("y"), lax.axis_index("c")
        pending = []
        ici = over_ici(in_refs, out_refs, sems)
        for i in range(na):
            half = arrays[i].shape[0] // 2
            mine = pl.ds(pl.multiple_of(c * half, 8), half)
            theirs = pl.ds(pl.multiple_of((1 - c) * half, 8), half)
            for j, (fx, fy) in enumerate(chips):
                px, py = (1 - x if fx else x), (1 - y if fy else y)
                k = len(chips) * i + j
                outgoing, landing = ici[k]
                landing.wait_recv()
                landed = out_refs[i].at[2 * px + py, mine]
                passed = pltpu.make_async_remote_copy(
                    src_ref=landed, dst_ref=landed, send_sem=d2d_send.at[k], recv_sem=d2d_recv.at[k],
                    device_id=(x, y, 1 - c), device_id_type=pl.DeviceIdType.MESH)
                passed.start()
                from_sibling = out_refs[i].at[2 * px + py, theirs]
                pending += [outgoing.wait_send, passed.wait_send, pltpu.make_async_remote_copy(
                    src_ref=from_sibling, dst_ref=from_sibling, send_sem=d2d_send.at[k], recv_sem=d2d_recv.at[k],
                    device_id=(x, y, 1 - c), device_id_type=pl.DeviceIdType.MESH).wait_recv]
        for done in pending:
            done()

    def finish(outs):
        me = 2 * lax.axis_index("x") + lax.axis_index("y")
        return [lax.dynamic_update_slice_in_dim(o, a[None], me, axis=0) for a, o in zip(arrays, outs)]

    out_shape = [jax.ShapeDtypeStruct((N_CHIPS,) + a.shape, a.dtype) for a in arrays]
    return _Plan(arrays, out_shape, [pltpu.SemaphoreType.DMA((na * len(chips),))] * 4, start, wait, finish)


def _send_other_half(name, arrays):
    return _run_plan(name, _other_half_plan(arrays))


def _other_half_plan(arrays):
    na = len(arrays)

    def copies(in_refs, out_refs, sems):
        send_sems, recv_sems = sems
        x, y, c = lax.axis_index("x"), lax.axis_index("y"), lax.axis_index("c")
        made = []
        for i in range(na):
            half = arrays[i].shape[1] // 2
            theirs = pl.ds(pl.multiple_of((1 - c) * half, 8), half)
            made.append(pltpu.make_async_remote_copy(
                src_ref=in_refs[i].at[:, theirs], dst_ref=out_refs[i], send_sem=send_sems.at[i],
                recv_sem=recv_sems.at[i], device_id=(x, y, 1 - c), device_id_type=pl.DeviceIdType.MESH))
        return made

    def start(in_refs, out_refs, sems):
        for copy in copies(in_refs, out_refs, sems):
            copy.start()

    def wait(in_refs, out_refs, sems):
        for copy in copies(in_refs, out_refs, sems):
            copy.wait()

    out_shape = [jax.ShapeDtypeStruct((a.shape[0], a.shape[1] // 2, a.shape[2]), a.dtype) for a in arrays]
    sems = [pltpu.SemaphoreType.DMA((na,)), pltpu.SemaphoreType.DMA((na,))]
    return _Plan(arrays, out_shape, sems, start, wait, list)


def _adam_math(w, g, m, v):
    m = ADAM_B1 * m + (1.0 - ADAM_B1) * g
    v = ADAM_B2 * v + (1.0 - ADAM_B2) * jnp.square(g)
    m_hat = m / (1.0 - ADAM_B1 ** ADAM_STEP)
    v_hat = v / (1.0 - ADAM_B2 ** ADAM_STEP)
    delta = -ADAM_LR * (m_hat / (jnp.sqrt(v_hat) + ADAM_EPS) + ADAM_WD * w)
    return delta, m, v


WHOLE_BLOCK_BYTES = 2 * 1024 * 1024


def _row_tile(r, c):
    return r if 4 * r * c <= WHOLE_BLOCK_BYTES else _tile(r, (256, 128, 64, 32, 16, 8))


def _sum_parts(name, parts):
    n, r, c = parts.shape
    tr = _row_tile(r, c)

    def body(p_ref, o_ref):
        tot = p_ref[0].astype(F32)
        for i in range(1, n):
            tot = tot + p_ref[i].astype(F32)
        o_ref[...] = tot

    return pl.pallas_call(
        body, name=name, grid=(r // tr,), in_specs=[pl.BlockSpec((n, tr, c), lambda i: (0, i, 0))],
        out_specs=pl.BlockSpec((tr, c), lambda i: (i, 0)), out_shape=jax.ShapeDtypeStruct((r, c), F32),
        compiler_params=_params(("parallel",)),
    )(parts)


def _pair_sum(name, piece, other, dtype):
    n, r, c = piece.shape
    half = r // 2
    tr = _row_tile(half, c)

    def body(lo_ref, hi_ref, other_ref, o_ref):
        own = jnp.where(lax.axis_index("c") == 0, lo_ref[...], hi_ref[...])
        o_ref[...] = (own + other_ref[...]).astype(o_ref.dtype)

    blk = pl.BlockSpec((None, tr, c), lambda j, i: (j, i, 0))
    return pl.pallas_call(
        body, name=name, grid=(n, half // tr),
        in_specs=[pl.BlockSpec((None, None, tr, c), lambda j, i: (j, 0, i, 0)),
                  pl.BlockSpec((None, None, tr, c), lambda j, i: (j, 1, i, 0)), blk],
        out_specs=blk, out_shape=jax.ShapeDtypeStruct((n, half, c), dtype),
        compiler_params=_params(("parallel", "parallel")),
    )(piece.reshape(n, 2, half, c), piece.reshape(n, 2, half, c), other)


def _adamw(name, w, parts, m, v, hosted=None):
    n, r, c = parts.shape
    tr = _row_tile(r, c)
    steps = r // tr
    plan = hosted or _NO_PLAN
    nh = len(plan.arrays)

    def body(w_ref, p_ref, m_ref, v_ref, *rest):
        host_in, (g_ref, d_ref, nm_ref, nv_ref) = rest[:nh], rest[nh:nh + 4]
        host_out, sems = rest[nh + 4:2 * nh + 4], rest[2 * nh + 4:]

        @pl.when(pl.program_id(0) == 0)
        def _():
            plan.start(host_in, host_out, sems)

        g = p_ref[0].astype(F32)
        for i in range(1, n):
            g = g + p_ref[i].astype(F32)
        delta, nm, nv = _adam_math(w_ref[...], g, m_ref[...], v_ref[...])
        g_ref[...], d_ref[...], nm_ref[...], nv_ref[...] = g, delta, nm, nv

        @pl.when(pl.program_id(0) == steps - 1)
        def _():
            plan.wait(host_in, host_out, sems)

    blk = pl.BlockSpec((tr, c), lambda i: (i, 0))
    any_spec = pl.BlockSpec(memory_space=pl.ANY)
    res = pl.pallas_call(
        body, name=name, grid=(steps,),
        in_specs=[blk, pl.BlockSpec((n, tr, c), lambda i: (0, i, 0)), blk, blk] + [any_spec] * nh,
        out_specs=[blk] * 4 + [any_spec] * nh,
        out_shape=[jax.ShapeDtypeStruct((r, c), F32)] * 4 + plan.out_shape, scratch_shapes=plan.sems,
        compiler_params=_params(("arbitrary",) if nh else ("parallel",)),
    )(w, parts, m, v, *plan.arrays)
    return (res[:4], plan.finish(res[4:])) if nh else res


def _ada_w_update(act_t, dmod, w, m, v, hosted):
    r, c = w.shape
    nb = act_t.shape[1]
    tr, tc = 256, 1024
    grid = (r // tr, c // tc)
    nh = len(hosted.arrays)

    def body(a_ref, d_ref, w_ref, m_ref, v_ref, *rest):
        host_in, (g_ref, dl_ref, nm_ref, nv_ref) = rest[:nh], rest[nh:nh + 4]
        host_out, sems = rest[nh + 4:2 * nh + 4], rest[2 * nh + 4:]
        i, j = pl.program_id(0), pl.program_id(1)

        @pl.when(jnp.logical_and(i == 0, j == 0))
        def _():
            hosted.start(host_in, host_out, sems)

        a, dm = a_ref[...], d_ref[...]
        g = a[:, 0:1] * dm[0:1, :]
        for b in range(1, nb):
            g = g + a[:, b:b + 1] * dm[b:b + 1, :]
        delta, nm, nv = _adam_math(w_ref[...], g, m_ref[...], v_ref[...])
        g_ref[...], dl_ref[...], nm_ref[...], nv_ref[...] = g, delta, nm, nv

        @pl.when(jnp.logical_and(i == grid[0] - 1, j == grid[1] - 1))
        def _():
            hosted.wait(host_in, host_out, sems)

    blk = pl.BlockSpec((tr, tc), lambda i, j: (i, j))
    any_spec = pl.BlockSpec(memory_space=pl.ANY)
    res = pl.pallas_call(
        body, name="ada_w_update", grid=grid,
        in_specs=[pl.BlockSpec((tr, nb), lambda i, j: (i, 0)), pl.BlockSpec((nb, tc), lambda i, j: (0, j)),
                  blk, blk, blk] + [any_spec] * nh,
        out_specs=[blk] * 4 + [any_spec] * nh,
        out_shape=[jax.ShapeDtypeStruct((r, c), F32)] * 4 + hosted.out_shape,
        scratch_shapes=hosted.sems,
        compiler_params=_params(("arbitrary", "arbitrary")),
    )(act_t, dmod, w, m, v, *hosted.arrays)
    return res[:4], hosted.finish(res[4:])


WEIGHTS = ['ada_w', 'ada_b', 'norm1_gain', 'norm2_gain', 'final_gain', 'w_in', 'w_out', 's5_lambda_re',
           's5_lambda_im', 's5_log_step', 's5_b_re', 's5_b_im', 's5_c_re', 's5_c_im', 's5_d', 's5_w_glu',
           's5_b_glu', 'rk_shift_prev', 'rk_shift_next', 'rk_w0', 'rk_w_up', 'rk_a0', 'rk_a_up', 'rk_g_up',
           'rk_k_k', 'rk_k_a', 'rk_r_k', 'rk_ln_gain', 'rk_ln_bias', 'ffn_w1', 'ffn_w2']
BIG_SHARDED = ['w_in', 'w_out', 's5_w_glu', 'ffn_w1', 'ffn_w2']
FFN_SHARDED = ['ffn_w1', 'ffn_w2']
RK_SHARDED = ['rk_w0', 'rk_a0', 'rk_w_up', 'rk_a_up', 'rk_g_up']
REPLICATED = ['ada_b', 'norm1_gain', 'norm2_gain', 'final_gain', 's5_lambda_re', 's5_lambda_im', 's5_log_step',
              's5_b_re', 's5_b_im', 's5_c_re', 's5_c_im', 's5_d', 's5_b_glu', 'rk_shift_prev', 'rk_shift_next',
              'rk_k_k', 'rk_k_a', 'rk_r_k', 'rk_ln_gain', 'rk_ln_bias']
PACK_COLS = 1024
N_CHIPS = 4
RK_ROWS = 420
RK_ROWS_PAD = 432


def _pack_rows(arrays, cols):
    return jnp.concatenate([a.reshape(-1, cols) for a in arrays], axis=0)


def _pack_flat(arrays):
    flat = jnp.concatenate([a.reshape(-1) for a in arrays])
    rows = -(-flat.shape[0] // PACK_COLS)
    return jnp.pad(flat, (0, rows * PACK_COLS - flat.shape[0])).reshape(rows, PACK_COLS)


def _unpack_flat(packed, like):
    flat, out, pos = packed.reshape(-1), [], 0
    for a in like:
        out.append(flat[pos:pos + a.size].reshape(a.shape))
        pos += a.size
    return out


def _cols_to_chips(full, n_rows):
    return jnp.transpose(full.reshape(n_rows, N_CHIPS, -1), (1, 0, 2))


def _chips_to_cols(parts):
    return jnp.transpose(parts, (1, 0, 2)).reshape(parts.shape[1], -1)


def kernel(x, c, ada_w, ada_b, norm1_gain, norm2_gain, final_gain, w_in, w_out, s5_lambda_re, s5_lambda_im, s5_log_step, s5_b_re, s5_b_im, s5_c_re, s5_c_im, s5_d, s5_w_glu, s5_b_glu, rk_shift_prev, rk_shift_next, rk_w0, rk_w_up, rk_a0, rk_a_up, rk_g_up, rk_k_k, rk_k_a, rk_r_k, rk_ln_gain, rk_ln_bias, ffn_w1, ffn_w2, loss_target, m_ada_w, m_ada_b, m_norm1_gain, m_norm2_gain, m_final_gain, m_w_in, m_w_out, m_s5_lambda_re, m_s5_lambda_im, m_s5_log_step, m_s5_b_re, m_s5_b_im, m_s5_c_re, m_s5_c_im, m_s5_d, m_s5_w_glu, m_s5_b_glu, m_rk_shift_prev, m_rk_shift_next, m_rk_w0, m_rk_w_up, m_rk_a0, m_rk_a_up, m_rk_g_up, m_rk_k_k, m_rk_k_a, m_rk_r_k, m_rk_ln_gain, m_rk_ln_bias, m_ffn_w1, m_ffn_w2, v_ada_w, v_ada_b, v_norm1_gain, v_norm2_gain, v_final_gain, v_w_in, v_w_out, v_s5_lambda_re, v_s5_lambda_im, v_s5_log_step, v_s5_b_re, v_s5_b_im, v_s5_c_re, v_s5_c_im, v_s5_d, v_s5_w_glu, v_s5_b_glu, v_rk_shift_prev, v_rk_shift_next, v_rk_w0, v_rk_w_up, v_rk_a0, v_rk_a_up, v_rk_g_up, v_rk_k_k, v_rk_k_a, v_rk_r_k, v_rk_ln_gain, v_rk_ln_bias, v_ffn_w1, v_ffn_w2):
    given = dict(locals())
    w = {n: given[n] for n in WEIGHTS}
    m = {n: given["m_" + n] for n in WEIGHTS}
    v = {n: given["v_" + n] for n in WEIGHTS}
    mx, my, mc = lax.axis_index("x"), lax.axis_index("y"), lax.axis_index("c")
    chip = 2 * mx + my
    dev = 2 * chip + mc
    xt, target = x[0], loss_target[0]

    def rk_rows(d):
        rows = _pack_rows([d[n] for n in RK_SHARDED], 256)
        return jnp.pad(rows, ((0, RK_ROWS_PAD - rows.shape[0]), (0, 0)))

    (c_all,), (w_in_parts,) = _run_plan("gather_first", _join_plans([
        _exchange_plan([c], ALL_PEERS, 8), _gather_halves_plan([w_in[0].astype(BF16)])]))

    (act,) = _rowwise("ada_act", lambda q: (q * _sigmoid(q),), [c_all.reshape(8, D_MODEL)], [], [(D_MODEL, F32)], 8)
    n_mod_cols = N_MOD * D_MODEL // N_CHIPS
    bias = jnp.broadcast_to(lax.dynamic_slice(ada_b, (0, chip * n_mod_cols), (1, n_mod_cols)), (8, n_mod_cols))
    mod_shard = _matmul("ada_fwd", act, ada_w[0], epilogue=_add_epilogue, extras=(bias,))
    (mod_parts,) = _exchange("gather_mod", [mod_shard], CHIP_PEERS, N_CHIPS)
    mod_all = _chips_to_cols(mod_parts)
    mod_mine = lax.dynamic_slice(mod_all, (dev, 0), (1, N_MOD * D_MODEL))
    mod = [mod_mine[:, i * D_MODEL:(i + 1) * D_MODEL] for i in range(N_MOD)]

    def mixer_weights(parts):
        w_out_parts, glu_parts, rk_full = parts

        def rk_piece(lo, hi, lead):
            return _chips_to_cols(rk_full[:, lo:hi]).reshape(lead + (RK_WIDTH,))

        zeros = jnp.zeros((LORA, RK_WIDTH), F32)
        w_up, a_up = rk_piece(4, 132, (2, LORA)), rk_piece(132, 260, (2, LORA))
        return {
            "w_out": w_out_parts.reshape(D_MODEL, D_MODEL), "s5_w_glu": glu_parts.reshape(S5_WIDTH, S5_WIDTH),
            "w0": list(rk_piece(0, 2, (2,))[:, None, :]), "a0": list(rk_piece(2, 4, (2,))[:, None, :]),
            "w_up": [jnp.concatenate([w_up[0], zeros]), jnp.concatenate([zeros, w_up[1]])],
            "a_up": [jnp.concatenate([a_up[0], zeros]), jnp.concatenate([zeros, a_up[1]])],
            "g_up": jnp.pad(rk_piece(260, 420, (GATE_LORA,)), ((0, GATE_PAD - GATE_LORA), (0, 0))),
        }

    wt = {
        "norm1_gain": norm1_gain, "norm2_gain": norm2_gain, "final_gain": final_gain.reshape(1, D_MODEL),
        "w_in": jnp.pad(_chips_to_cols(w_in_parts), ((0, 0), (0, PROJ_PAD - PROJ))),
        "mu_prev": jnp.pad(rk_shift_prev, ((0, 0), (0, RK_PAD - RK_IN))),
        "mu_next": jnp.pad(rk_shift_next, ((0, 0), (0, RK_PAD - RK_IN))),
        "lam_re": [s5_lambda_re[0, d].reshape(S5_CH, 1) for d in range(2)],
        "lam_im": [s5_lambda_im[0, d].reshape(S5_CH, 1) for d in range(2)],
        "log_step": [jnp.repeat(s5_log_step[0, d], S5_STATE).reshape(S5_CH, 1) for d in range(2)],
        "b_re": s5_b_re.reshape(S5_CH, S5_GROUP), "b_im": s5_b_im.reshape(S5_CH, S5_GROUP),
        "c_re": s5_c_re[0], "c_im": s5_c_im[0],
        "s5_d": s5_d, "s5_b_glu": s5_b_glu,
        "k_k": rk_k_k, "k_a": rk_k_a, "r_k": rk_r_k.reshape(1, RK_WIDTH),
        "ln_gain": rk_ln_gain, "ln_bias": rk_ln_bias,
    }

    ffn_shards = [w[n][0].astype(BF16) for n in FFN_SHARDED]
    mixer_shards = [w_out[0].astype(BF16), s5_w_glu[0].astype(BF16), rk_rows(w)]
    loss_rows, grad_x, g = _local_step(xt, target, mod, wt, ffn_shards=ffn_shards,
                                       mixer_shards=(mixer_shards, mixer_weights))
    loss = lax.psum(jnp.sum(loss_rows), ("x", "y", "c"))


    big_grads = {
        "w_in": _cols_to_chips(g["w_in"][:, :PROJ], D_MODEL),
        "w_out": g["w_out"].reshape(N_CHIPS, -1, D_MODEL),
        "s5_w_glu": g["s5_w_glu"].reshape(N_CHIPS, -1, S5_WIDTH),
    }
    rk_grads = jnp.concatenate([
        _cols_to_chips(jnp.concatenate(g["w0"]), 2), _cols_to_chips(jnp.concatenate(g["a0"]), 2),
        _cols_to_chips(jnp.concatenate([g["w_up"][0][:LORA], g["w_up"][1][LORA:]]), 2 * LORA),
        _cols_to_chips(jnp.concatenate([g["a_up"][0][:LORA], g["a_up"][1][LORA:]]), 2 * LORA),
        _cols_to_chips(g["g_up"][:GATE_LORA], GATE_LORA),
        jnp.zeros((N_CHIPS, RK_ROWS_PAD - RK_ROWS, 256), F32)], axis=1)
    local_small = {
        "ada_b": jnp.concatenate(g["mod"], axis=1),
        "norm1_gain": g["norm1_gain"], "norm2_gain": g["norm2_gain"], "final_gain": g["final_gain"],
        "s5_lambda_re": jnp.concatenate(g["lam_re"]), "s5_lambda_im": jnp.concatenate(g["lam_im"]),
        "s5_log_step": jnp.concatenate([q.reshape(S5_GROUPS, S5_STATE).sum(axis=1) for q in g["log_step"]]),
        "s5_b_re": g["b_re"], "s5_b_im": g["b_im"], "s5_c_re": g["c_re"], "s5_c_im": g["c_im"],
        "s5_d": g["s5_d"], "s5_b_glu": g["s5_b_glu"],
        "rk_shift_prev": g["mu_prev"][:, :RK_IN], "rk_shift_next": g["mu_next"][:, :RK_IN],
        "rk_k_k": g["k_k"], "rk_k_a": g["k_a"], "rk_r_k": g["r_k"],
        "rk_ln_gain": g["ln_gain"], "rk_ln_bias": g["ln_bias"],
    }
    late = [n for n in BIG_SHARDED if n not in FFN_SHARDED]
    late_pieces = [big_grads[n] for n in late] + [rk_grads]
    late_names = late + ["rk"]

    def whole(halves):
        return halves.reshape(1, 2 * halves.shape[1], halves.shape[2])

    ffn_halves = [_sum_parts("sum_" + n, a) for n, a in zip(FFN_SHARDED, g["ffn_arrived"])]
    from_sibling, ffn_pairs, (small_all,) = _run_plan("swap_late", _join_plans([
        _other_half_plan(late_pieces), _exchange_plan(ffn_halves, CORE_PEER, 2),
        _exchange_plan([_pack_flat([local_small[n] for n in REPLICATED]).astype(BF16)], ALL_PEERS, 8)]))
    late_sums = [_pair_sum("pair_" + n, piece, other, F32 if n == "rk" else BF16)
                 for n, piece, other in zip(late_names, late_pieces, from_sibling)]
    pairs = dict(zip(FFN_SHARDED, [whole(p) for p in ffn_pairs]))

    mod_rows = N_MOD * D_MODEL // PACK_COLS
    dmod_all = small_all[:, :mod_rows].reshape(8, N_MOD * D_MODEL).astype(F32)
    dmod = lax.dynamic_slice(dmod_all, (0, chip * n_mod_cols), (8, n_mod_cols))
    ada_res, arrived = _ada_w_update(act.T, dmod, ada_w[0], m_ada_w[0], v_ada_w[0],
                                     hosted=_exchange_plan(late_sums, CHIP_PEERS, N_CHIPS, scatter=True))
    late_halves = [_sum_parts("sum_" + n, a) for n, a in zip(late_names, arrived)]

    out = {"ada_w": [r[None] for r in ada_res]}
    first = FFN_SHARDED[0]
    res, swapped = _adamw("adamw_" + first, w[first][0], pairs[first], m[first][0], v[first][0],
                          hosted=_exchange_plan(late_halves, CORE_PEER, 2))
    out[first] = [r[None] for r in res]
    pairs.update(zip(late_names, [whole(p) for p in swapped]))
    for n in [FFN_SHARDED[1]] + late:
        out[n] = [r[None] for r in _adamw("adamw_" + n, w[n][0], pairs[n], m[n][0], v[n][0])]
    rk_res = _adamw("adamw_rk", rk_rows(w), pairs["rk"], rk_rows(m), rk_rows(v))
    for q in range(4):
        pieces, pos = [], 0
        for n in RK_SHARDED:
            rows = w[n].size // 256
            pieces.append(rk_res[q][pos:pos + rows].reshape(w[n].shape))
            pos += rows
        for n, piece in zip(RK_SHARDED, pieces):
            out.setdefault(n, []).append(piece)

    small_res = _adamw("adamw_small", _pack_flat([w[n] for n in REPLICATED]), small_all,
                       _pack_flat([m[n] for n in REPLICATED]), _pack_flat([v[n] for n in REPLICATED]))
    for q in range(4):
        for n, piece in zip(REPLICATED, _unpack_flat(small_res[q], [w[n] for n in REPLICATED])):
            out.setdefault(n, []).append(piece)

    return (loss, grad_x[None], *[out[n][0] for n in WEIGHTS], *[out[n][1] for n in WEIGHTS],
            *[out[n][2] for n in WEIGHTS], *[out[n][3] for n in WEIGHTS])
```

```python
import functools
import math

import jax
import jax.numpy as jnp
from jax import lax
from jax.experimental import pallas as pl
from jax.experimental.pallas import tpu as pltpu

F32 = jnp.float32
BF16 = jnp.bfloat16

D_MODEL = 2048
S5_WIDTH = 1024
S5_GROUP = 16
S5_GROUPS = 64
S5_STATE = 64
S5_CH = S5_GROUPS * S5_STATE
S5_BLK = 256
RK_WIDTH = 1024
RK_HEAD = 64
RK_HEADS = 16
LORA = 64
GATE_LORA = 160
GATE_PAD = 256
RK_IN = 3488
RK_PAD = 3584
PROJ = 4512
PROJ_PAD = 4608
FFN = 8192
N_MOD = 6
NORM_EPS = 1e-6
GN_EPS = 64e-5
L2_EPS = 1e-12
RK_CHUNK = 64
RK_PASSES = {"solve": 3, "kt": 3, "s0": 1, "akk_v": 1, "ark_v": 1, "arb_u": 1, "state": 1}
LW_SCALE = math.exp(-0.5)
ADAM_LR, ADAM_B1, ADAM_B2, ADAM_EPS, ADAM_WD, ADAM_STEP = 0.001, 0.9, 0.999, 1e-08, 0.01, 10
VMEM_LIMIT = 56 * 1024 * 1024
HI = lax.Precision.HIGHEST


def _params(sem=None):
    return pltpu.CompilerParams(dimension_semantics=sem, vmem_limit_bytes=VMEM_LIMIT)


def _full(a):
    nd = a.ndim
    return pl.BlockSpec(a.shape, lambda *_: (0,) * nd)


@jax.custom_vjp
def _bdot(a, b):
    return jnp.dot(a.astype(BF16), b.astype(BF16), preferred_element_type=F32)


def _bdot_fwd(a, b):
    return _bdot(a, b), (a, b)


def _bdot_bwd(res, g):
    a, b = res
    gb = g.astype(BF16)
    da = lax.dot_general(gb, b.astype(BF16), (((1,), (1,)), ((), ())), preferred_element_type=F32)
    db = lax.dot_general(a.astype(BF16), gb, (((0,), (0,)), ((), ())), preferred_element_type=F32)
    return da, db


_bdot.defvjp(_bdot_fwd, _bdot_bwd)


@jax.custom_vjp
def _seg_dot(x, ind, ind_t):
    hi = x.astype(BF16)
    lo = (x - hi.astype(F32)).astype(BF16)
    both = jnp.dot(jnp.concatenate([hi, lo], axis=0), ind.astype(BF16), preferred_element_type=F32)
    return both[:x.shape[0]] + both[x.shape[0]:]


_seg_dot.defvjp(lambda x, ind, ind_t: (_seg_dot(x, ind, ind_t), (ind, ind_t)),
                lambda res, g: (_seg_dot(g, res[1], res[0]), jnp.zeros_like(res[0]), jnp.zeros_like(res[1])))


def _sigmoid(z):
    return 1.0 / (1.0 + jnp.exp(-z))


def _gelu(y):
    return 0.5 * y * (1.0 + jnp.tanh(0.7978845608028654 * (y + 0.044715 * (y * y * y))))


def _rms(x):
    return x * lax.rsqrt(jnp.mean(x * x, axis=-1, keepdims=True) + NORM_EPS)


def _tile(n, prefs):
    for t in prefs:
        if n % t == 0:
            return t
    return n


def _matmul(name, a, b, ta=False, tb=False, epilogue=None, extras=(), out_dtypes=(F32,), chips=None, hosted=None):
    m = a.shape[1] if ta else a.shape[0]
    k = a.shape[0] if ta else a.shape[1]
    if chips == "b":
        assert not tb and b.shape[1] == k
        n = N_CHIPS * b.shape[2]
    elif chips == "b_t":
        assert tb and N_CHIPS * b.shape[2] == k
        n = b.shape[1]
    else:
        n = b.shape[0] if tb else b.shape[1]
        assert k == (b.shape[1] if tb else b.shape[0]), (a.shape, b.shape, ta, tb)
    split = N_CHIPS if chips in ("b", "out") else 1
    tm = _tile(m, (1024, 512, 256, 128))
    tn = _tile(n // split, (1024, 768, 512, 256, 128))
    tk = k // N_CHIPS if chips == "b_t" else _tile(k, (2048, 1024, 512, 256, 128))
    nk = k // tk
    per = n // split // tn
    n_ex, n_out = len(extras), len(out_dtypes)
    dims = (((0 if ta else 1,), (1 if tb else 0,)), ((), ()))

    hosted = hosted or _NO_PLAN
    nh = len(hosted.arrays)
    grid = (m // tm, split, per, nk)

    def body(a_ref, b_ref, *rest):
        ex_refs, host_in = rest[:n_ex], rest[n_ex:n_ex + nh]
        out_refs, host_out = rest[n_ex + nh:n_ex + nh + n_out], rest[n_ex + nh + n_out:n_ex + 2 * nh + n_out]
        acc, sems = rest[n_ex + 2 * nh + n_out], rest[n_ex + 2 * nh + n_out + 1:]
        kk = pl.program_id(3)
        if nh:
            ids = [pl.program_id(d) for d in range(4)]
            first = functools.reduce(jnp.logical_and, [i == 0 for i in ids])
            last = functools.reduce(jnp.logical_and, [i == g - 1 for i, g in zip(ids, grid)])

            @pl.when(first)
            def _():
                hosted.start(host_in, host_out, sems)

        @pl.when(kk == 0)
        def _():
            acc[...] = jnp.zeros_like(acc)

        acc[...] += lax.dot_general(a_ref[...].astype(BF16), b_ref[...].astype(BF16), dims,
                                    preferred_element_type=F32)

        @pl.when(kk == nk - 1)
        def _():
            res = acc[...]
            outs = epilogue(res, *[e[...] for e in ex_refs]) if epilogue is not None else (res,)
            for o_ref, val in zip(out_refs, outs):
                o_ref[...] = val.astype(o_ref.dtype)

        if nh:
            @pl.when(last)
            def _():
                hosted.wait(host_in, host_out, sems)

    if ta:
        a_spec = pl.BlockSpec((tk, tm), lambda i, c, j, q: (q, i))
    else:
        a_spec = pl.BlockSpec((tm, tk), lambda i, c, j, q: (i, q))
    if chips == "b":
        b_spec = pl.BlockSpec((None, tk, tn), lambda i, c, j, q: (c, q, j))
    elif chips == "b_t":
        b_spec = pl.BlockSpec((None, tn, tk), lambda i, c, j, q: (q, j, 0))
    elif tb:
        b_spec = pl.BlockSpec((tn, tk), lambda i, c, j, q: (c * per + j, q))
    else:
        b_spec = pl.BlockSpec((tk, tn), lambda i, c, j, q: (q, c * per + j))
    mn_spec = pl.BlockSpec((tm, tn), lambda i, c, j, q: (i, c * per + j))
    if chips == "out":
        out_spec = pl.BlockSpec((None, tm, tn), lambda i, c, j, q: (c, i, j))
        out_shape = [jax.ShapeDtypeStruct((N_CHIPS, m, n // N_CHIPS), dt) for dt in out_dtypes]
    else:
        out_spec, out_shape = mn_spec, [jax.ShapeDtypeStruct((m, n), dt) for dt in out_dtypes]
    any_spec = pl.BlockSpec(memory_space=pl.ANY)
    order = ("arbitrary",) * 4 if nh else ("parallel", "parallel", "parallel", "arbitrary")
    outs = pl.pallas_call(
        body, name=name, grid=grid,
        in_specs=[a_spec, b_spec] + [mn_spec] * n_ex + [any_spec] * nh,
        out_specs=[out_spec] * n_out + [any_spec] * nh, out_shape=out_shape + hosted.out_shape,
        scratch_shapes=[pltpu.VMEM((tm, tn), F32)] + hosted.sems,
        compiler_params=_params(order),
    )(a, b, *extras, *hosted.arrays)
    res = outs[0] if n_out == 1 else outs[:n_out]
    return (res, hosted.finish(outs[n_out:])) if nh else res


def _row_spec(a, tm):
    return pl.BlockSpec((tm, a.shape[1]), lambda i: (i, 0))


def _rowwise(name, fn, rows, params, outs, tm):
    t = rows[0].shape[0]
    tm = min(tm, t)
    n_r, n_p = len(rows), len(params)

    def body(*refs):
        vals = [r[...] for r in refs[:n_r + n_p]]
        res = fn(*vals)
        for o_ref, val in zip(refs[n_r + n_p:], res):
            o_ref[...] = val.astype(o_ref.dtype)

    res = pl.pallas_call(
        body, name=name, grid=(t // tm,),
        in_specs=[_row_spec(r, tm) for r in rows] + [_full(p) for p in params],
        out_specs=[pl.BlockSpec((tm, n), lambda i: (i, 0)) for n, _ in outs],
        out_shape=[jax.ShapeDtypeStruct((t, n), dt) for n, dt in outs],
        compiler_params=_params(("parallel",)),
    )(*rows, *params)
    return res


def _rowwise_vjp(name, fn, rows, params, cts, row_grads, param_grads, tm, consts=(), addends=None,
                 emit=(), row_grad_dtypes=None):
    t = rows[0].shape[0]
    tm = min(tm, t)
    addends = addends or {}
    n_r, n_p, n_c = len(rows), len(params), len(consts)
    ct_flat = [c for group in cts for c in group]
    add_list = [addends[q] for q in sorted(addends)]
    n_ct, n_add = len(ct_flat), len(add_list)
    row_grad_dtypes = row_grad_dtypes or [F32] * len(row_grads)

    def body(*refs):
        pos = 0
        row_v = [r[...].astype(F32) for r in refs[pos:pos + n_r]]; pos += n_r
        par_v = [r[...].astype(F32) for r in refs[pos:pos + n_p]]; pos += n_p
        con_v = [r[...] for r in refs[pos:pos + n_c]]; pos += n_c
        ct_v = [r[...].astype(F32) for r in refs[pos:pos + n_ct]]; pos += n_ct
        add_v = [r[...] for r in refs[pos:pos + n_add]]; pos += n_add
        emit_refs = refs[pos:pos + len(emit)]; pos += len(emit)
        rg_refs = refs[pos:pos + len(row_grads)]; pos += len(row_grads)
        pg_refs = refs[pos:pos + len(param_grads)]

        def diff_fn(*dargs):
            rv, pv = list(row_v), list(par_v)
            for q, i in enumerate(row_grads):
                rv[i] = dargs[q]
            for q, j in enumerate(param_grads):
                pv[j] = dargs[len(row_grads) + q]
            return fn(*rv, *pv, *con_v)

        prim = [row_v[i] for i in row_grads] + [par_v[j] for j in param_grads]
        res, vjp = jax.vjp(diff_fn, *prim)
        ct_vals, q = [], 0
        for o, group in zip(res, cts):
            tot = jnp.zeros_like(o)
            for _ in group:
                tot = tot + ct_v[q]
                q += 1
            ct_vals.append(tot)
        grads = vjp(tuple(ct_vals))
        for e_ref, idx in zip(emit_refs, emit):
            e_ref[...] = res[idx].astype(e_ref.dtype)
        add_pos = {p: q for q, p in enumerate(sorted(addends))}
        for q, g_ref in enumerate(rg_refs):
            g = grads[q]
            if q in add_pos:
                g = g + add_v[add_pos[q]]
            g_ref[...] = g.astype(g_ref.dtype)

        @pl.when(pl.program_id(0) == 0)
        def _():
            for g_ref in pg_refs:
                g_ref[...] = jnp.zeros_like(g_ref)

        for q, g_ref in enumerate(pg_refs):
            g_ref[...] += grads[len(row_grads) + q]

    emit_shapes = []
    if emit:
        probe = jax.eval_shape(lambda *a: fn(*a), *[jax.ShapeDtypeStruct((tm, r.shape[1]), F32) for r in rows],
                               *[jax.ShapeDtypeStruct(p.shape, p.dtype) for p in params],
                               *[jax.ShapeDtypeStruct(c.shape, c.dtype) for c in consts])
        emit_shapes = [probe[idx].shape[1] for idx in emit]
    out_specs = ([pl.BlockSpec((tm, n), lambda i: (i, 0)) for n in emit_shapes]
                 + [_row_spec(rows[i], tm) for i in row_grads]
                 + [_full(params[j]) for j in param_grads])
    out_shape = ([jax.ShapeDtypeStruct((t, n), F32) for n in emit_shapes]
                 + [jax.ShapeDtypeStruct(rows[i].shape, dt) for i, dt in zip(row_grads, row_grad_dtypes)]
                 + [jax.ShapeDtypeStruct(params[j].shape, F32) for j in param_grads])
    return pl.pallas_call(
        body, name=name, grid=(t // tm,),
        in_specs=([_row_spec(r, tm) for r in rows] + [_full(p) for p in params] + [_full(c) for c in consts]
                  + [_row_spec(c, tm) for c in ct_flat] + [_row_spec(a, tm) for a in add_list]),
        out_specs=out_specs, out_shape=out_shape,
        compiler_params=_params(("arbitrary",)),
    )(*rows, *params, *consts, *ct_flat, *add_list)


def _norm_mod_fn(x, gain, scale, shift):
    return (_rms(x) * gain * (1.0 + scale) + shift,)


def _resid_norm_mod_fn(x, mixed, gate, gain, scale, shift):
    x1 = x + gate * mixed
    return x1, _rms(x1) * gain * (1.0 + scale) + shift


def _loss_fn(x1, ffn, target, gate, gain):
    y = _rms(x1 + gate * ffn) * gain
    err = y - target
    return (0.5 * jnp.mean(err * err, axis=-1, keepdims=True),)


def _s5_out_fn(ylin, u, d_skip, w_glu, b_glu):
    z = _gelu(ylin + d_skip * u)
    return (z * _sigmoid(_bdot(z, w_glu) + b_glu),)


def _rk_pre_fn(k, wdn, adn, gdn, w0_0, w0_1, wup_0, wup_1, a0_0, a0_1, aup_0, aup_1, g_up, k_k, k_a, seg, seg_t):
    kkr = k * k_k
    inv = 1.0 / jnp.sqrt(jnp.maximum(_seg_dot(kkr * kkr, seg, seg_t), L2_EPS * L2_EPS))
    kk = kkr * _seg_dot(inv, seg_t, seg)
    tw = jnp.tanh(wdn)
    lws, kds, acts = [], [], []
    for w0, wup, a0, aup in ((w0_0, wup_0, a0_0, aup_0), (w0_1, wup_1, a0_1, aup_1)):
        lws.append(-LW_SCALE * _sigmoid(w0 + _bdot(tw, wup)))
        act = _sigmoid(a0 + _bdot(adn, aup))
        acts.append(act)
        kds.append(k * (1.0 + (act - 1.0) * k_a))
    gate = _bdot(_sigmoid(gdn), g_up)
    return (kk, lws[0], lws[1], kds[0], kds[1], acts[0], acts[1], gate)


def _rk_post_fn(y0, y1, r, v, kd0, kd1, gate, ln_gain, ln_bias, r_k, seg, seg_t):
    y = y0 + y1
    mu = _seg_dot(_seg_dot(y, seg, seg_t) * (1.0 / RK_HEAD), seg_t, seg)
    yc = y - mu
    var = _seg_dot(yc * yc, seg, seg_t) * (1.0 / RK_HEAD)
    yn = yc * _seg_dot(lax.rsqrt(var + GN_EPS), seg_t, seg) * ln_gain + ln_bias
    bonus = _seg_dot(_seg_dot(r * (kd0 + kd1) * r_k, seg, seg_t), seg_t, seg)
    return ((yn + bonus * v) * gate,)


def _s5_prep_fn(lr0, li0, ls0, lr1, li1, ls1, b_re, b_im):
    outs = []
    for lam_re, lam_im, ls in ((lr0, li0, ls0), (lr1, li1, ls1)):
        step = jnp.exp(ls)
        mag = jnp.exp(lam_re * step)
        lbar_re = mag * jnp.cos(lam_im * step)
        lbar_im = mag * jnp.sin(lam_im * step)
        den = lam_re * lam_re + lam_im * lam_im
        nr = lbar_re - 1.0
        coef_re = (nr * lam_re + lbar_im * lam_im) / den
        coef_im = (lbar_im * lam_re - nr * lam_im) / den
        outs += [lbar_re, lbar_im, coef_re * b_re - coef_im * b_im, coef_re * b_im + coef_im * b_re]
    return tuple(outs)


def _shift_rows(x, down):
    t = x.shape[0]
    rows = lax.broadcasted_iota(jnp.int32, x.shape, 0)
    if down:
        return jnp.where(rows >= 1, pltpu.roll(x, 1, 0), 0.0)
    return jnp.where(rows < t - 1, pltpu.roll(x, t - 1, 0), 0.0)


def _token_shift(src, mu_prev, mu_next, first):
    t, n = src.shape[0], mu_prev.shape[1]

    def body(p_ref, mp_ref, mn_ref, o_ref):
        x = p_ref[...]
        o_ref[...] = x + mp_ref[...] * (_shift_rows(x, True) - x) + mn_ref[...] * (_shift_rows(x, False) - x)

    col = pl.BlockSpec((t, 128), lambda j: (0, j))
    par = pl.BlockSpec((1, 128), lambda j: (0, j))
    return pl.pallas_call(
        body, name="token_shift", grid=(n // 128,),
        in_specs=[pl.BlockSpec((t, 128), lambda j: (0, j + first)), par, par], out_specs=col,
        out_shape=jax.ShapeDtypeStruct((t, n), F32), compiler_params=_params(("parallel",)),
    )(src, mu_prev, mu_next)


def _token_shift_bwd(src, mu_prev, mu_next, dps, first):
    t, n = dps.shape

    def body(p_ref, mp_ref, mn_ref, d_ref, dp_ref, dmp_ref, dmn_ref):
        x, d, mp, mn = p_ref[...], d_ref[...], mp_ref[...], mn_ref[...]
        dp_ref[...] = d * (1.0 - mp - mn) + _shift_rows(d * mp, False) + _shift_rows(d * mn, True)
        dmp_ref[...] = jnp.sum(d * (_shift_rows(x, True) - x), axis=0, keepdims=True)
        dmn_ref[...] = jnp.sum(d * (_shift_rows(x, False) - x), axis=0, keepdims=True)

    col = pl.BlockSpec((t, 128), lambda j: (0, j))
    par = pl.BlockSpec((1, 128), lambda j: (0, j))
    return pl.pallas_call(
        body, name="token_shift_bwd", grid=(n // 128,),
        in_specs=[pl.BlockSpec((t, 128), lambda j: (0, j + first)), par, par, col],
        out_specs=[col, par, par],
        out_shape=[jax.ShapeDtypeStruct((t, n), F32), jax.ShapeDtypeStruct((1, n), F32),
                   jax.ShapeDtypeStruct((1, n), F32)],
        compiler_params=_params(("parallel",)),
    )(src, mu_prev, mu_next, dps)


N_SEG = 32
S5_BLOCKS = 32
S5_PER_IN = 4


def _scan_in_place(sr_ref, si_ref, ar, ai, carry_ref, reverse):
    seg_len = sr_ref.shape[0] // N_SEG
    ng = N_SEG // 8

    def rows(i, grp):
        first = (seg_len - 1 - i if reverse else i) * N_SEG + 8 * grp
        return pl.ds(pl.multiple_of(first, 8), 8)

    zero = jnp.zeros((8, 128), F32)
    one = jnp.ones((8, 128), F32)

    def local(i, c):
        pr, pi = c[-2:]
        out = []
        for grp in range(ng):
            sr, si = c[2 * grp], c[2 * grp + 1]
            nr = ar * sr - ai * si + sr_ref[rows(i, grp), :]
            ni = ar * si + ai * sr + si_ref[rows(i, grp), :]
            sr_ref[rows(i, grp), :] = nr
            si_ref[rows(i, grp), :] = ni
            out += [nr, ni]
        return tuple(out) + (ar * pr - ai * pi, ar * pi + ai * pr)

    ends = lax.fori_loop(0, seg_len, local, (zero,) * (2 * ng) + (one, zero))
    qr, qi = ends[-2][0:1], ends[-1][0:1]
    order = list(range(N_SEG - 1, -1, -1)) if reverse else list(range(N_SEG))
    cr = jnp.zeros((1, 128), F32)
    ci = jnp.zeros((1, 128), F32)
    for j in order:
        carry_ref[j:j + 1, :] = cr
        carry_ref[N_SEG + j:N_SEG + j + 1, :] = ci
        grp, sub = divmod(j, 8)
        lr, li = ends[2 * grp][sub:sub + 1], ends[2 * grp + 1][sub:sub + 1]
        cr, ci = lr + qr * cr - qi * ci, li + qr * ci + qi * cr
    carries = [(carry_ref[8 * grp:8 * grp + 8, :], carry_ref[N_SEG + 8 * grp:N_SEG + 8 * grp + 8, :])
               for grp in range(ng)]

    def fix(i, c):
        pr, pi = c
        npr, npi = ar * pr - ai * pi, ar * pi + ai * pr
        for grp in range(ng):
            cr8, ci8 = carries[grp]
            sr_ref[rows(i, grp), :] = sr_ref[rows(i, grp), :] + npr * cr8 - npi * ci8
            si_ref[rows(i, grp), :] = si_ref[rows(i, grp), :] + npr * ci8 + npi * cr8
        return npr, npi

    lax.fori_loop(0, seg_len, fix, (one, zero))


def _interleave(x):
    t, c = x.shape
    return jnp.transpose(x.reshape(N_SEG, t // N_SEG, c), (1, 0, 2)).reshape(t, c)


def _deinterleave(x):
    t, c = x.shape
    return jnp.transpose(x.reshape(t // N_SEG, N_SEG, c), (1, 0, 2)).reshape(t, c)


def _lag_sums(lr_ref, li_ref, sr_ref, si_ref, earlier):
    t = lr_ref.shape[0]
    body, edge = pl.ds(N_SEG, t - N_SEG), pl.ds(0, N_SEG)
    far = pl.ds(t - N_SEG, N_SEG)
    rows = lax.broadcasted_iota(jnp.int32, (N_SEG, 128), 0)
    if earlier:
        lam_main, s_main, lam_edge = body, pl.ds(0, t - N_SEG), edge
        wrap = lambda ref: jnp.where(rows >= 1, pltpu.roll(ref[far, :], 1, 0), 0.0)
    else:
        lam_main, s_main, lam_edge = pl.ds(0, t - N_SEG), body, far
        wrap = lambda ref: jnp.where(rows < N_SEG - 1, pltpu.roll(ref[edge, :], N_SEG - 1, 0), 0.0)
    lr, li, sr, si = lr_ref[lam_main, :], li_ref[lam_main, :], sr_ref[s_main, :], si_ref[s_main, :]
    er, ei, pr, pi = lr_ref[lam_edge, :], li_ref[lam_edge, :], wrap(sr_ref), wrap(si_ref)
    re = jnp.sum(lr * sr + li * si, axis=0, keepdims=True) + jnp.sum(er * pr + ei * pi, axis=0, keepdims=True)
    im = jnp.sum(li * sr - lr * si, axis=0, keepdims=True) + jnp.sum(ei * pr - er * pi, axis=0, keepdims=True)
    return re, im


def _dot_bf16(a, b, dims=(((1,), (0,)), ((), ()))):
    return lax.dot_general(a.astype(BF16), b.astype(BF16), dims, preferred_element_type=F32)


NT_DIMS = (((1,), (1,)), ((), ()))
TN_DIMS = (((0,), (0,)), ((), ()))


def _s5_specs(t):
    blk = pl.BlockSpec((None, t, 128), lambda i, q: (S5_PER_IN * i + q, 0, 0))
    mat = pl.BlockSpec((None, 128, 128), lambda i, q: (S5_PER_IN * i + q, 0, 0))
    vec = pl.BlockSpec((None, 1, 128), lambda i, q: (S5_PER_IN * i + q, 0, 0))
    chan = pl.BlockSpec((t, 128), lambda i, q: (0, i))
    return blk, mat, vec, chan


S5_GRID = (S5_BLOCKS // S5_PER_IN, S5_PER_IN)


def _s5_forward(name, u, b_re, b_im, l_re, l_im, reverse, other=None, c_re=None, c_im_neg=None):
    t = u.shape[0]
    project = other is not None
    blk, mat, vec, chan = _s5_specs(t)

    def body(*refs):
        u_ref, br_ref, bi_ref, lr_ref, li_ref = refs[:5]
        if project:
            or_ref, oi_ref, cr_ref, ci_ref, sr_ref, si_ref, y_ref, carry_ref = refs[5:]
        else:
            sr_ref, si_ref, carry_ref = refs[5:]
        uv = u_ref[...]
        sr_ref[...] = _dot_bf16(uv, br_ref[...])
        si_ref[...] = _dot_bf16(uv, bi_ref[...])
        ar = jnp.broadcast_to(lr_ref[...], (8, 128))
        ai = jnp.broadcast_to(li_ref[...], (8, 128))
        _scan_in_place(sr_ref, si_ref, ar, ai, carry_ref, reverse)
        if project:
            y = (_dot_bf16(sr_ref[...] + or_ref[...], cr_ref[...])
                 + _dot_bf16(si_ref[...] + oi_ref[...], ci_ref[...]))

            @pl.when(pl.program_id(1) == 0)
            def _():
                y_ref[...] = y

            @pl.when(pl.program_id(1) != 0)
            def _():
                y_ref[...] += y

    state = jax.ShapeDtypeStruct((S5_BLOCKS, t, 128), F32)
    ins = [u, b_re, b_im, l_re, l_im] + ([other[0], other[1], c_re, c_im_neg] if project else [])
    in_specs = [chan, mat, mat, vec, vec] + ([blk, blk, mat, mat] if project else [])
    return pl.pallas_call(
        body, name=name, grid=S5_GRID, in_specs=in_specs,
        out_specs=[blk, blk] + ([chan] if project else []),
        out_shape=[state, state] + ([jax.ShapeDtypeStruct((t, S5_WIDTH), F32)] if project else []),
        scratch_shapes=[pltpu.VMEM((2 * N_SEG, 128), F32)],
        compiler_params=_params(("arbitrary", "arbitrary")),
    )(*ins)


def _s5_backward(name, dy, u, du_in, states, other, b_re, b_im, c_re, c_im_neg, l_re, l_im, reverse):
    t = u.shape[0]
    with_c = other is not None
    blk, mat, vec, chan = _s5_specs(t)

    def body(*refs):
        dy_ref, u_ref, du_in_ref, sr_ref, si_ref = refs[:5]
        pos = 5
        if with_c:
            or_ref, oi_ref = refs[5:7]
            pos = 7
        br_ref, bi_ref, cr_ref, ci_ref, lr_ref, li_ref = refs[pos:pos + 6]
        outs = refs[pos + 6:]
        du_ref, dbr_ref, dbi_ref, dlr_ref, dli_ref = outs[:5]
        lam_r, lam_i, carry_ref = outs[-3:]
        dyv, uv = dy_ref[...], u_ref[...]
        lam_r[...] = _dot_bf16(dyv, cr_ref[...], NT_DIMS)
        lam_i[...] = _dot_bf16(dyv, ci_ref[...], NT_DIMS)
        ar = jnp.broadcast_to(lr_ref[...], (8, 128))
        ai = -jnp.broadcast_to(li_ref[...], (8, 128))
        _scan_in_place(lam_r, lam_i, ar, ai, carry_ref, not reverse)
        lr, li = lam_r[...], lam_i[...]
        dlr_ref[...], dli_ref[...] = _lag_sums(lam_r, lam_i, sr_ref, si_ref, not reverse)
        dbr_ref[...] = _dot_bf16(uv, lr, TN_DIMS)
        dbi_ref[...] = _dot_bf16(uv, li, TN_DIMS)
        du = _dot_bf16(lr, br_ref[...], NT_DIMS) + _dot_bf16(li, bi_ref[...], NT_DIMS)

        @pl.when(pl.program_id(1) == 0)
        def _():
            du_ref[...] = du_in_ref[...] + du

        @pl.when(pl.program_id(1) != 0)
        def _():
            du_ref[...] += du

        if with_c:
            dcr_ref, dci_ref = outs[5:7]
            dcr_ref[...] = _dot_bf16(sr_ref[...] + or_ref[...], dyv, TN_DIMS)
            dci_ref[...] = _dot_bf16(si_ref[...] + oi_ref[...], dyv, TN_DIMS)

    mats = jax.ShapeDtypeStruct((S5_BLOCKS, 128, 128), F32)
    vecs = jax.ShapeDtypeStruct((S5_BLOCKS, 1, 128), F32)
    ins = [dy, u, du_in, states[0], states[1]] + ([other[0], other[1]] if with_c else [])
    ins += [b_re, b_im, c_re, c_im_neg, l_re, l_im]
    in_specs = [chan, chan, chan, blk, blk] + ([blk, blk] if with_c else []) + [mat] * 4 + [vec] * 2
    return pl.pallas_call(
        body, name=name, grid=S5_GRID, in_specs=in_specs,
        out_specs=[chan, mat, mat, vec, vec] + ([mat, mat] if with_c else []),
        out_shape=[jax.ShapeDtypeStruct((t, S5_WIDTH), F32), mats, mats, vecs, vecs] + ([mats, mats] if with_c else []),
        scratch_shapes=[pltpu.VMEM((t, 128), F32), pltpu.VMEM((t, 128), F32), pltpu.VMEM((2 * N_SEG, 128), F32)],
        compiler_params=_params(("arbitrary", "arbitrary")),
    )(*ins)


def _ein(passes, spec, a, b):
    if passes == 6:
        return jnp.einsum(spec, a, b, precision=HI, preferred_element_type=F32)
    a_hi, b_hi = a.astype(BF16), b.astype(BF16)
    if passes == 1:
        return jnp.einsum(spec, a_hi, b_hi, preferred_element_type=F32)
    a_lo = (a - a_hi.astype(F32)).astype(BF16)
    b_lo = (b - b_hi.astype(F32)).astype(BF16)
    cross = jnp.einsum(spec, a_hi, b_lo, preferred_element_type=F32)
    if spec.startswith('hik'):
        m = a.shape[1]
        stacked = jnp.einsum(spec, jnp.concatenate([a_hi, a_lo], axis=1), b_hi, preferred_element_type=F32)
        return stacked[:, :m] + stacked[:, m:] + cross
    return (jnp.einsum(spec, a_hi, b_hi, preferred_element_type=F32) + cross
            + jnp.einsum(spec, a_lo, b_hi, preferred_element_type=F32))


@jax.custom_vjp
def _tri_mm(tri, tri_t, z):
    n = z.shape[2]
    hi = z.astype(BF16)
    rest = z - hi.astype(F32)
    mid = rest.astype(BF16)
    lo = (rest - mid.astype(F32)).astype(BF16)
    out = jnp.einsum('hik,hkj->hij', tri.astype(BF16), jnp.concatenate([hi, mid, lo], axis=2),
                     preferred_element_type=F32)
    return out[:, :, :n] + out[:, :, n:2 * n] + out[:, :, 2 * n:]


def _tri_mm_bwd(res, g):
    tri, tri_t = res
    return jnp.zeros_like(tri), jnp.zeros_like(tri_t), _tri_mm(tri_t, tri, g)


_tri_mm.defvjp(lambda tri, tri_t, z: (_tri_mm(tri, tri_t, z), (tri, tri_t)), _tri_mm_bwd)


def _chunk_cumsum(lw, incl, incl_t):
    shape = (lw.shape[0],) + incl.shape
    return _tri_mm(jnp.broadcast_to(incl.astype(F32), shape), jnp.broadcast_to(incl_t.astype(F32), shape), lw)


@functools.partial(jax.custom_vjp, nondiff_argnums=(0,))
def _bmm(p, a, b):
    return _ein(p, 'hik,hkj->hij', a, b)


@functools.partial(jax.custom_vjp, nondiff_argnums=(0,))
def _bmm_nt(p, a, b):
    return _ein(p, 'hik,hjk->hij', a, b)


@functools.partial(jax.custom_vjp, nondiff_argnums=(0,))
def _bmm_tn(p, a, b):
    return _ein(p, 'hki,hkj->hij', a, b)


_bmm.defvjp(lambda p, a, b: (_bmm(p, a, b), (a, b)),
            lambda p, res, g: (_bmm_nt(p, g, res[1]), _bmm_tn(p, res[0], g)))
_bmm_nt.defvjp(lambda p, a, b: (_bmm_nt(p, a, b), (a, b)),
               lambda p, res, g: (_bmm(p, g, res[1]), _bmm_tn(p, g, res[0])))
_bmm_tn.defvjp(lambda p, a, b: (_bmm_tn(p, a, b), (a, b)),
               lambda p, res, g: (_bmm_nt(p, res[1], g), _bmm(p, res[0], g)))


@jax.custom_vjp
def _split_rows(x):
    c = x.shape[1] // 2
    return x[:, :c], x[:, c:]


_split_rows.defvjp(lambda x: (_split_rows(x), None), lambda _, g: (jnp.concatenate(g, axis=1),))


def _stack_rows(a, b):
    return jnp.concatenate([a, b], axis=1)


def _nilpotent_inverse(l_mat):
    c = l_mat.shape[1]
    ps = RK_PASSES["solve"]
    row = lax.broadcasted_iota(jnp.int32, (c, c), 0)
    col = lax.broadcasted_iota(jnp.int32, (c, c), 1)
    x = -l_mat
    inv = jnp.where(row == col, 1.0, 0.0) + x
    power = _bmm(ps, x, x)
    span = 2
    while 2 * span < c:
        step, power = _split_rows(_bmm(ps, _stack_rows(inv, power), power))
        inv = inv + step
        span *= 2
    return inv + _bmm(ps, inv, power)


@jax.custom_vjp
def _solve_with(inv, l_mat, rhs):
    return _bmm(RK_PASSES["solve"], inv, rhs)


def _solve_with_fwd(inv, l_mat, rhs):
    u = _bmm(RK_PASSES["solve"], inv, rhs)
    return u, (inv, u)


def _solve_with_bwd(res, g):
    inv, u = res
    d_rhs = _bmm_tn(RK_PASSES["solve"], inv, g)
    return jnp.zeros_like(inv), -_bmm_nt(RK_PASSES["solve"], d_rhs, u), d_rhs


_solve_with.defvjp(_solve_with_fwd, _solve_with_bwd)


def _rk_chunk(s0, r, lw, k, v, kk, a, reverse, inv=None):
    h, c, n = r.shape
    row = lax.broadcasted_iota(jnp.int32, (c, c), 0)
    col = lax.broadcasted_iota(jnp.int32, (c, c), 1)
    incl = (row <= col) if reverse else (row >= col)
    strict = (row < col) if reverse else (row > col)
    cum = _chunk_cumsum(lw, incl, (row >= col) if reverse else (row <= col))
    g_in = jnp.exp(cum)
    g_inv = jnp.exp(-cum)
    kap = kk * jnp.exp(cum - lw)
    beta = kk * a * g_inv
    kt = k * g_inv
    rt = r * g_in
    p, ps = RK_PASSES, RK_PASSES["solve"]
    both = _stack_rows(kap, rt)
    kap_beta, rt_beta = _split_rows(_bmm_nt(ps, both, beta))
    kap_kt, rt_kt = _split_rows(_bmm_nt(p["kt"], both, kt))
    kap_s0, rt_s0 = _split_rows(_bmm_nt(p["s0"], both, s0))
    l_mat = jnp.where(strict, kap_beta, 0.0)
    rhs = kap_s0 + _bmm(p["akk_v"], jnp.where(strict, kap_kt, 0.0), v)
    if inv is None:
        inv = lax.stop_gradient(_nilpotent_inverse(l_mat))
    u = _solve_with(inv, l_mat, rhs)
    y = (rt_s0 + _bmm(p["ark_v"], jnp.where(incl, rt_kt, 0.0), v)
         - _bmm(p["arb_u"], jnp.where(incl, rt_beta, 0.0), u))
    s1 = ((s0 + _bmm_tn(p["state"], _stack_rows(v, -u), _stack_rows(kt, beta)))
          * jnp.exp(jnp.sum(lw, axis=1, keepdims=True)))
    return y, s1, inv


class _Plan:
    def __init__(self, arrays, out_shape, sems, start, wait, finish):
        self.arrays, self.out_shape, self.sems = list(arrays), list(out_shape), list(sems)
        self.start, self.wait, self.finish = start, wait, finish


_NO_PLAN = _Plan([], [], [], lambda *_: None, lambda *_: None, lambda outs: [])


def _join_plans(plans):
    def cut(seq, sizes):
        out, pos = [], 0
        for s in sizes:
            out.append(seq[pos:pos + s])
            pos += s
        return out

    n_arr, n_sem = [len(p.arrays) for p in plans], [len(p.sems) for p in plans]

    def run(which):
        def go(in_refs, out_refs, sems):
            for p, i, o, s in zip(plans, cut(in_refs, n_arr), cut(out_refs, n_arr), cut(sems, n_sem)):
                getattr(p, which)(i, o, s)
        return go

    return _Plan([a for p in plans for a in p.arrays], [s for p in plans for s in p.out_shape],
                 [s for p in plans for s in p.sems], run("start"), run("wait"),
                 lambda outs: [p.finish(o) for p, o in zip(plans, cut(outs, n_arr))])


def _split_heads(x):
    return jnp.stack([x[:, RK_HEAD * i:RK_HEAD * (i + 1)] for i in range(RK_HEADS)], axis=0)


def _store_heads(ref, x):
    for i in range(RK_HEADS):
        ref[:, RK_HEAD * i:RK_HEAD * (i + 1)] = x[i]


def _rk_core_fwd(name, r, lw, k, v, kk, a, reverse, chunk, hosted=None):
    t = r.shape[0]
    h, n = RK_HEADS, RK_HEAD
    nc = t // chunk

    def idx(i):
        return nc - 1 - i if reverse else i

    hosted = hosted or _NO_PLAN
    nh = len(hosted.arrays)

    def body(r_ref, lw_ref, k_ref, v_ref, kk_ref, a_ref, *rest):
        host_in, (y_ref, ck_ref, inv_ref), host_out = rest[:nh], rest[nh:nh + 3], rest[nh + 3:2 * nh + 3]
        s_ref, sems = rest[2 * nh + 3], rest[2 * nh + 4:]

        @pl.when(pl.program_id(0) == 0)
        def _():
            s_ref[...] = jnp.zeros_like(s_ref)
            hosted.start(host_in, host_out, sems)

        s0 = s_ref[...]
        ck_ref[0] = s0
        ops = [_split_heads(ref[...]) for ref in (r_ref, lw_ref, k_ref, v_ref, kk_ref, a_ref)]
        y, s1, inv = _rk_chunk(s0, *ops, reverse)
        _store_heads(y_ref, y)
        inv_ref[0] = inv
        s_ref[...] = s1

        @pl.when(pl.program_id(0) == nc - 1)
        def _():
            hosted.wait(host_in, host_out, sems)

    blk = pl.BlockSpec((chunk, RK_WIDTH), lambda i: (idx(i), 0))
    any_spec = pl.BlockSpec(memory_space=pl.ANY)
    res = pl.pallas_call(
        body, name=name, grid=(nc,), in_specs=[blk] * 6 + [any_spec] * nh,
        out_specs=[blk, pl.BlockSpec((1, h, n, n), lambda i: (idx(i), 0, 0, 0)),
                   pl.BlockSpec((1, h, chunk, chunk), lambda i: (idx(i), 0, 0, 0))] + [any_spec] * nh,
        out_shape=[jax.ShapeDtypeStruct((t, RK_WIDTH), F32), jax.ShapeDtypeStruct((nc, h, n, n), F32),
                   jax.ShapeDtypeStruct((nc, h, chunk, chunk), F32)] + hosted.out_shape,
        scratch_shapes=[pltpu.VMEM((h, n, n), F32)] + hosted.sems,
        compiler_params=_params(("arbitrary",)),
    )(r, lw, k, v, kk, a, *hosted.arrays)
    return res[0], (res[1], res[2]), hosted.finish(res[3:])


def _rk_core_bwd(name, r, lw, k, v, kk, a, ck, dy, reverse, chunk, hosted=None):
    t = r.shape[0]
    h, n = RK_HEADS, RK_HEAD
    nc = t // chunk
    hosted = hosted or _NO_PLAN
    nh = len(hosted.arrays)

    def idx(i):
        return i if reverse else nc - 1 - i

    def body(r_ref, lw_ref, k_ref, v_ref, kk_ref, a_ref, ck_ref, inv_ref, dy_ref, *rest):
        host_in, out_refs, host_out = rest[:nh], rest[nh:nh + 6], rest[nh + 6:2 * nh + 6]
        ds_ref, sems = rest[2 * nh + 6], rest[2 * nh + 7:]

        @pl.when(pl.program_id(0) == 0)
        def _():
            ds_ref[...] = jnp.zeros_like(ds_ref)
            hosted.start(host_in, host_out, sems)

        inv = inv_ref[0]

        def fn(*operands):
            return _rk_chunk(*operands, reverse=reverse, inv=inv)[:2]

        ops = [_split_heads(ref[...]) for ref in (r_ref, lw_ref, k_ref, v_ref, kk_ref, a_ref)]
        _, vjp = jax.vjp(fn, ck_ref[0], *ops)
        grads = vjp((_split_heads(dy_ref[...]), ds_ref[...]))
        ds_ref[...] = grads[0]
        for o_ref, g in zip(out_refs, grads[1:]):
            _store_heads(o_ref, g)

        @pl.when(pl.program_id(0) == nc - 1)
        def _():
            hosted.wait(host_in, host_out, sems)

    blk = pl.BlockSpec((chunk, RK_WIDTH), lambda i: (idx(i), 0))
    any_spec = pl.BlockSpec(memory_space=pl.ANY)
    res = pl.pallas_call(
        body, name=name, grid=(nc,),
        in_specs=[blk] * 6 + [pl.BlockSpec((1, h, n, n), lambda i: (idx(i), 0, 0, 0)),
                              pl.BlockSpec((1, h, chunk, chunk), lambda i: (idx(i), 0, 0, 0)), blk]
        + [any_spec] * nh,
        out_specs=[blk] * 6 + [any_spec] * nh,
        out_shape=[jax.ShapeDtypeStruct((t, RK_WIDTH), F32)] * 6 + hosted.out_shape,
        scratch_shapes=[pltpu.VMEM((h, n, n), F32)] + hosted.sems,
        compiler_params=_params(("arbitrary",)),
    )(r, lw, k, v, kk, a, ck[0], ck[1], dy, *hosted.arrays)
    return res[:6], hosted.finish(res[6:])


def _s5_band_place():
    return jax.nn.one_hot(jnp.arange(S5_BLOCKS) % S5_PER_IN, S5_PER_IN, dtype=F32)


def _s5_in_blocks(bbar):
    b = jnp.transpose(bbar.reshape(S5_BLOCKS, 2, S5_STATE, S5_GROUP), (0, 1, 3, 2))
    band = jnp.einsum('jghp,gk->jghkp', b, jnp.eye(2, dtype=F32)).reshape(S5_BLOCKS, 32, 128)
    return jnp.einsum('jrc,jq->jqrc', band, _s5_band_place()).reshape(S5_BLOCKS, 128, 128)


def _s5_in_unblock(mats):
    band = jnp.einsum('jqrc,jq->jrc', mats.reshape(S5_BLOCKS, S5_PER_IN, 32, 128), _s5_band_place())
    diag = jnp.einsum('jghgp->jghp', band.reshape(S5_BLOCKS, 2, S5_GROUP, 2, S5_STATE))
    return jnp.transpose(diag, (0, 1, 3, 2)).reshape(S5_CH, S5_GROUP)


def _s5_out_blocks(c):
    ct = jnp.transpose(c.reshape(S5_BLOCKS, 2, S5_GROUP, S5_STATE), (0, 1, 3, 2))
    band = jnp.einsum('jgph,gk->jgpkh', ct, jnp.eye(2, dtype=F32)).reshape(S5_BLOCKS, 128, 32)
    return jnp.einsum('jrc,jq->jrqc', band, _s5_band_place()).reshape(S5_BLOCKS, 128, 128)


def _s5_out_unblock(mats):
    band = jnp.einsum('jrqc,jq->jrc', mats.reshape(S5_BLOCKS, 128, S5_PER_IN, 32), _s5_band_place())
    diag = jnp.einsum('jgpgh->jgph', band.reshape(S5_BLOCKS, 2, S5_STATE, 2, S5_GROUP))
    return jnp.transpose(diag, (0, 1, 3, 2)).reshape(S5_GROUPS, S5_GROUP, S5_STATE)


def _head_indicator():
    ch = lax.broadcasted_iota(jnp.int32, (RK_WIDTH, 128), 0) // RK_HEAD
    hd = lax.broadcasted_iota(jnp.int32, (RK_WIDTH, 128), 1)
    seg = (ch == hd).astype(F32)
    return seg, seg.T


def _add_epilogue(acc, e):
    return (acc + e,)


def _local_step(x, target, mod, wt, chunk=RK_CHUNK, ffn_shards=None, mixer_shards=None):
    t = x.shape[0]
    wt = dict(wt)
    sh1, sc1, gt1, sh2, sc2, gt2 = mod
    seg, seg_t = _head_indicator()
    g = {}

    (h1,) = _rowwise("norm1", _norm_mod_fn, [x], [wt["norm1_gain"], sc1, sh1], [(D_MODEL, BF16)], 256)
    if mixer_shards is None:
        proj = _matmul("proj", h1, wt["w_in"])
    else:
        proj, gathered = _matmul("proj", h1, wt["w_in"], hosted=_gather_halves_plan(mixer_shards[0]))
        wt.update(mixer_shards[1](gathered))
    u = proj[:, :S5_WIDTH]
    ps = _token_shift(proj, wt["mu_prev"], wt["mu_next"], first=S5_WIDTH // 128)
    r, k, v = ps[:, :1024], ps[:, 1024:2048], ps[:, 2048:3072]
    wdn, adn, gdn = ps[:, 3072:3200], ps[:, 3200:3328], ps[:, 3328:RK_PAD]

    prep_rows = [wt["lam_re"][0], wt["lam_im"][0], wt["log_step"][0], wt["lam_re"][1], wt["lam_im"][1],
                 wt["log_step"][1], wt["b_re"], wt["b_im"]]
    col1, col16 = (1, F32), (S5_GROUP, F32)
    prep = _rowwise("s5_prep", _s5_prep_fn, prep_rows, [], [col1, col1, col16, col16] * 2, 512)
    lbar = [tuple(prep[4 * d + q].reshape(S5_BLOCKS, 1, 128) for q in range(2)) for d in range(2)]
    b_blk = [tuple(_s5_in_blocks(prep[4 * d + 2 + q]) for q in range(2)) for d in range(2)]
    c_blk = (_s5_out_blocks(wt["c_re"]), -_s5_out_blocks(wt["c_im"]))
    u_il = _interleave(u)
    state0 = _s5_forward("s5_fwd0", u_il, *b_blk[0], *lbar[0], reverse=False)
    s1_re, s1_im, ylin_il = _s5_forward("s5_fwd1", u_il, *b_blk[1], *lbar[1], reverse=True, other=state0,
                                        c_re=c_blk[0], c_im_neg=c_blk[1])
    ylin = _deinterleave(ylin_il)
    states = [tuple(state0), (s1_re, s1_im)]
    s5_par = [wt["s5_d"], wt["s5_w_glu"], wt["s5_b_glu"]]
    (o_s5,) = _rowwise("s5_out", _s5_out_fn, [ylin, u], s5_par, [(S5_WIDTH, BF16)], 256)

    pre_par = [wt["w0"][0], wt["w0"][1], wt["w_up"][0], wt["w_up"][1], wt["a0"][0], wt["a0"][1],
               wt["a_up"][0], wt["a_up"][1], wt["g_up"], wt["k_k"], wt["k_a"]]
    pre = _rowwise("rk_pre", _rk_pre_fn, [k, wdn, adn, gdn], pre_par + [seg, seg_t], [(RK_WIDTH, F32)] * 8, 256)
    kk, lw, kd, act, gate = pre[0], pre[1:3], pre[3:5], pre[5:7], pre[7]
    core_in, ys, cks = [], [], []
    for d in range(2):
        ops = (r, lw[d], kd[d], v, kk, act[d])
        plan = _gather_halves_plan([ffn_shards[d]]) if ffn_shards is not None else None
        y, ck, gathered = _rk_core_fwd(f"rk_core{d}", *ops, reverse=(d == 1), chunk=min(chunk, t), hosted=plan)
        if gathered:
            wt["ffn_w1" if d == 0 else "ffn_w2"] = gathered[0] if d == 0 else gathered[0].reshape(FFN, D_MODEL)
        core_in.append(ops)
        ys.append(y)
        cks.append(ck)
    post_rows = [ys[0], ys[1], r, v, kd[0], kd[1], gate]
    post_par = [wt["ln_gain"], wt["ln_bias"], wt["r_k"]]
    (o_rk,) = _rowwise("rk_post", _rk_post_fn, post_rows, post_par + [seg, seg_t], [(RK_WIDTH, BF16)], 256)

    o = jnp.concatenate([o_s5, o_rk], axis=1)
    mixed = _matmul("mix_out", o, wt["w_out"])
    n2_par = [gt1, wt["norm2_gain"], sc2, sh2]
    x1, h2 = _rowwise("norm2", _resid_norm_mod_fn, [x, mixed], n2_par, [(D_MODEL, F32), (D_MODEL, BF16)], 256)
    f1, hid = _matmul("ffn1", h2, wt["ffn_w1"], out_dtypes=(F32, BF16), chips="b",
                      epilogue=lambda acc: (acc, jnp.square(jnp.maximum(acc, 0.0))))
    ffn = _matmul("ffn2", hid, wt["ffn_w2"])

    ones = jnp.ones((t, 1), F32)
    loss_rows, dx1, dffn, g_gt2, g["final_gain"] = _rowwise_vjp(
        "loss", _loss_fn, [x1, ffn, target], [gt2, wt["final_gain"]], [[ones]], [0, 1], [0, 1], 256, emit=(0,),
        row_grad_dtypes=[F32, BF16])
    df1 = _matmul("ffn2_dx", dffn, wt["ffn_w2"], tb=True, extras=(f1,), out_dtypes=(BF16,),
                  epilogue=lambda acc, f: (acc * (2.0 * jnp.maximum(f, 0.0)),))
    g["ffn_w2"] = _matmul("ffn2_dw", hid, dffn, ta=True)
    if ffn_shards is None:
        g["ffn_w1"] = _matmul("ffn1_dw", h2, df1, ta=True, chips="out")
        dh2 = _matmul("ffn1_dx", df1, wt["ffn_w1"], tb=True, chips="b_t")
    else:
        piece2 = g.pop("ffn_w2").reshape(N_CHIPS, -1, D_MODEL)
        piece1, (other2,) = _matmul("ffn1_dw", h2, df1, ta=True, chips="out", hosted=_other_half_plan([piece2]))
        dh2, (other1,) = _matmul("ffn1_dx", df1, wt["ffn_w1"], tb=True, chips="b_t",
                                 hosted=_other_half_plan([piece1]))
        ffn_sums = [_pair_sum("pair_" + n, piece, other, BF16)
                    for n, piece, other in zip(FFN_SHARDED, (piece1, piece2), (other1, other2))]
    dx_a, dmixed, g_gt1, g["norm2_gain"], g_sc2, g_sh2 = _rowwise_vjp(
        "norm2_bwd", _resid_norm_mod_fn, [x, mixed], n2_par, [[dx1], [dh2]], [0, 1], [0, 1, 2, 3], 256,
        row_grad_dtypes=[F32, BF16])
    do = _matmul("mix_out_dx", dmixed, wt["w_out"], tb=True)
    g["w_out"] = _matmul("mix_out_dw", o, dmixed, ta=True)
    do_s5, do_rk = do[:, :S5_WIDTH], do[:, S5_WIDTH:]

    dylin, du, g["s5_d"], g["s5_w_glu"], g["s5_b_glu"] = _rowwise_vjp(
        "s5_out_bwd", _s5_out_fn, [ylin, u], s5_par, [[do_s5]], [0, 1], [0, 1, 2], 256)
    prep_cts = []
    dylin_il, du_il = _interleave(dylin), _interleave(du)
    for d in range(2):
        res = _s5_backward(f"s5_bwd{d}", dylin_il, u_il, du_il, states[d], states[1] if d == 0 else None,
                           *b_blk[d], *c_blk, *lbar[d], reverse=(d == 1))
        du_il, db_re, db_im, dl_re, dl_im = res[:5]
        if d == 0:
            g["c_re"], g["c_im"] = _s5_out_unblock(res[5]), -_s5_out_unblock(res[6])
        prep_cts += [[dl_re.reshape(S5_CH, 1)], [dl_im.reshape(S5_CH, 1)], [_s5_in_unblock(db_re)],
                     [_s5_in_unblock(db_im)]]
    du = _deinterleave(du_il)
    pg = _rowwise_vjp("s5_prep_bwd", _s5_prep_fn, prep_rows, [], prep_cts, list(range(8)), [], 512)
    g["lam_re"], g["lam_im"], g["log_step"] = (pg[0], pg[3]), (pg[1], pg[4]), (pg[2], pg[5])
    g["b_re"], g["b_im"] = pg[6], pg[7]

    pb = _rowwise_vjp("rk_post_bwd", _rk_post_fn, post_rows, post_par, [[do_rk]], [0, 2, 3, 4, 5, 6], [0, 1, 2],
                      128, consts=[seg, seg_t])
    dy, dr_b, dv_b, dkd_b, dgate = pb[0], pb[1], pb[2], pb[3:5], pb[5]
    g["ln_gain"], g["ln_bias"], g["r_k"] = pb[6], pb[7], pb[8]
    cg = []
    for d in range(2):
        plan = None
        if ffn_shards is not None:
            plan = _exchange_plan([ffn_sums[d]], CHIP_PEERS, N_CHIPS, scatter=True)
        grads, arrived = _rk_core_bwd(f"rk_core{d}_bwd", *core_in[d], cks[d], dy, reverse=(d == 1),
                                      chunk=min(chunk, t), hosted=plan)
        g.setdefault("ffn_arrived", []).extend(arrived)
        cg.append(grads)
    pre_cts = [[cg[0][4], cg[1][4]], [cg[0][1]], [cg[1][1]], [cg[0][2], dkd_b[0]], [cg[1][2], dkd_b[1]],
               [cg[0][5]], [cg[1][5]], [dgate]]
    qb = _rowwise_vjp("rk_pre_bwd", _rk_pre_fn, [k, wdn, adn, gdn], pre_par, pre_cts, [0, 1, 2, 3],
                      list(range(11)), 128, consts=[seg, seg_t])
    dk, dwdn, dadn, dgdn = qb[:4]
    g["w0"], g["w_up"], g["a0"], g["a_up"] = (qb[4], qb[5]), (qb[6], qb[7]), (qb[8], qb[9]), (qb[10], qb[11])
    g["g_up"], g["k_k"], g["k_a"] = qb[12], qb[13], qb[14]
    dr, dv = _rowwise("rk_sum", lambda a, b, c, e, f, h: (a + b + c, e + f + h),
                      [cg[0][0], cg[1][0], dr_b, cg[0][3], cg[1][3], dv_b], [], [(RK_WIDTH, F32)] * 2, 256)
    dps = jnp.concatenate([dr, dk, dv, dwdn, dadn, dgdn], axis=1)
    dp, g["mu_prev"], g["mu_next"] = _token_shift_bwd(proj, wt["mu_prev"], wt["mu_next"], dps,
                                                      first=S5_WIDTH // 128)

    dproj = jnp.concatenate([du, dp], axis=1).astype(BF16)
    dh1 = _matmul("proj_dx", dproj, wt["w_in"], tb=True)
    g["w_in"] = _matmul("proj_dw", h1, dproj, ta=True)
    grad_x, g["norm1_gain"], g_sc1, g_sh1 = _rowwise_vjp(
        "norm1_bwd", _norm_mod_fn, [x], [wt["norm1_gain"], sc1, sh1], [[dh1]], [0], [0, 1, 2], 256,
        addends={0: dx_a})
    g["mod"] = [g_sh1, g_sc1, g_gt1, g_sh2, g_sc2, g_gt2]
    return loss_rows, grad_x, g


CHIP_PEERS = ((1, 0, 0), (0, 1, 0), (1, 1, 0))
ALL_PEERS = ((0, 0, 1), (0, 1, 0), (0, 1, 1), (1, 0, 0), (1, 0, 1), (1, 1, 0), (1, 1, 1))
CORE_PEER = ((0, 0, 1),)


def _exchange(name, arrays, peers, n_slots, scatter=False):
    return _run_plan(name, _exchange_plan(arrays, peers, n_slots, scatter))


def _run_plan(name, plan):
    na = len(plan.arrays)

    def body(*refs):
        plan.start(refs[:na], refs[na:2 * na], refs[2 * na:])
        plan.wait(refs[:na], refs[na:2 * na], refs[2 * na:])

    any_spec = pl.BlockSpec(memory_space=pl.ANY)
    return plan.finish(pl.pallas_call(
        body, name=name, in_specs=[any_spec] * na, out_specs=[any_spec] * na, out_shape=plan.out_shape,
        scratch_shapes=plan.sems,
    )(*plan.arrays))


def _exchange_plan(arrays, peers, n_slots, scatter=False):
    na, nm = len(arrays), len(peers)

    def ident(px, py, pc):
        return {8: 4 * px + 2 * py + pc, 4: 2 * px + py, 2: pc}[n_slots]

    def copies(in_refs, out_refs, sems):
        send_sems, recv_sems = sems
        x, y, c = lax.axis_index("x"), lax.axis_index("y"), lax.axis_index("c")
        me = ident(x, y, c)
        made = []
        for i in range(na):
            for j, (fx, fy, fc) in enumerate(peers):
                px, py, pc = (1 - x if fx else x), (1 - y if fy else y), (1 - c if fc else c)
                src = in_refs[i].at[ident(px, py, pc)] if scatter else in_refs[i]
                made.append(pltpu.make_async_remote_copy(
                    src_ref=src, dst_ref=out_refs[i].at[me],
                    send_sem=send_sems.at[i * nm + j], recv_sem=recv_sems.at[i * nm + j],
                    device_id=(px, py, pc), device_id_type=pl.DeviceIdType.MESH))
        return made

    def start(in_refs, out_refs, sems):
        for copy in copies(in_refs, out_refs, sems):
            copy.start()

    def wait(in_refs, out_refs, sems):
        for copy in copies(in_refs, out_refs, sems):
            copy.wait()

    def finish(outs):
        me = ident(lax.axis_index("x"), lax.axis_index("y"), lax.axis_index("c"))
        return [lax.dynamic_update_slice_in_dim(
            o, lax.dynamic_index_in_dim(a, me, 0, keepdims=True) if scatter else a[None], me, axis=0)
            for a, o in zip(arrays, outs)]

    out_shape = [jax.ShapeDtypeStruct(((n_slots,) + a.shape[1:]) if scatter else ((n_slots,) + a.shape), a.dtype)
                 for a in arrays]
    sems = [pltpu.SemaphoreType.DMA((na * nm,)), pltpu.SemaphoreType.DMA((na * nm,))]
    return _Plan(arrays, out_shape, sems, start, wait, finish)


def _gather_halves(name, arrays):
    return _run_plan(name, _gather_halves_plan(arrays))


def _gather_halves_plan(arrays):
    na = len(arrays)
    chips = ((1, 0), (0, 1), (1, 1))

    def over_ici(in_refs, out_refs, sems):
        ici_send, ici_recv = sems[:2]
        x, y, c = lax.axis_index("x"), lax.axis_index("y"), lax.axis_index("c")
        made = []
        for i in range(na):
            half = arrays[i].shape[0] // 2
            mine = pl.ds(pl.multiple_of(c * half, 8), half)
            for j, (fx, fy) in enumerate(chips):
                px, py = (1 - x if fx else x), (1 - y if fy else y)
                k = len(chips) * i + j
                made.append([pltpu.make_async_remote_copy(
                    src_ref=in_refs[i].at[mine], dst_ref=out_refs[i].at[chip, mine],
                    send_sem=ici_send.at[k], recv_sem=ici_recv.at[k],
                    device_id=(px, py, c), device_id_type=pl.DeviceIdType.MESH)
                    for chip in (2 * x + y, 2 * px + py)])
        return made

    def start(in_refs, out_refs, sems):
        for outgoing, _ in over_ici(in_refs, out_refs, sems):
            outgoing.start()

    def wait(in_refs, out_refs, sems):
        d2d_send, d2d_recv = sems[2:]
        x, y, c = lax.axis_index("x"), lax.axis_index("y"), lax.axis_index("c")
        pending = []
        ici = over_ici(in_refs, out_refs, sems)
        for i in range(na):
            half = arrays[i].shape[0] // 2
            mine = pl.ds(pl.multiple_of(c * half, 8), half)
            theirs = pl.ds(pl.multiple_of((1 - c) * half, 8), half)
            for j, (fx, fy) in enumerate(chips):
                px, py = (1 - x if fx else x), (1 - y if fy else y)
                k = len(chips) * i + j
                outgoing, landing = ici[k]
                landing.wait_recv()
                landed = out_refs[i].at[2 * px + py, mine]
                passed = pltpu.make_async_remote_copy(
                    src_ref=landed, dst_ref=landed, send_sem=d2d_send.at[k], recv_sem=d2d_recv.at[k],
                    device_id=(x, y, 1 - c), device_id_type=pl.DeviceIdType.MESH)
                passed.start()
                from_sibling = out_refs[i].at[2 * px + py, theirs]
                pending += [outgoing.wait_send, passed.wait_send, pltpu.make_async_remote_copy(
                    src_ref=from_sibling, dst_ref=from_sibling, send_sem=d2d_send.at[k], recv_sem=d2d_recv.at[k],
                    device_id=(x, y, 1 - c), device_id_type=pl.DeviceIdType.MESH).wait_recv]
        for done in pending:
            done()

    def finish(outs):
        me = 2 * lax.axis_index("x") + lax.axis_index("y")
        return [lax.dynamic_update_slice_in_dim(o, a[None], me, axis=0) for a, o in zip(arrays, outs)]

    out_shape = [jax.ShapeDtypeStruct((N_CHIPS,) + a.shape, a.dtype) for a in arrays]
    return _Plan(arrays, out_shape, [pltpu.SemaphoreType.DMA((na * len(chips),))] * 4, start, wait, finish)


def _send_other_half(name, arrays):
    return _run_plan(name, _other_half_plan(arrays))


def _other_half_plan(arrays):
    na = len(arrays)

    def copies(in_refs, out_refs, sems):
        send_sems, recv_sems = sems
        x, y, c = lax.axis_index("x"), lax.axis_index("y"), lax.axis_index("c")
        made = []
        for i in range(na):
            half = arrays[i].shape[1] // 2
            theirs = pl.ds(pl.multiple_of((1 - c) * half, 8), half)
            made.append(pltpu.make_async_remote_copy(
                src_ref=in_refs[i].at[:, theirs], dst_ref=out_refs[i], send_sem=send_sems.at[i],
                recv_sem=recv_sems.at[i], device_id=(x, y, 1 - c), device_id_type=pl.DeviceIdType.MESH))
        return made

    def start(in_refs, out_refs, sems):
        for copy in copies(in_refs, out_refs, sems):
            copy.start()

    def wait(in_refs, out_refs, sems):
        for copy in copies(in_refs, out_refs, sems):
            copy.wait()

    out_shape = [jax.ShapeDtypeStruct((a.shape[0], a.shape[1] // 2, a.shape[2]), a.dtype) for a in arrays]
    sems = [pltpu.SemaphoreType.DMA((na,)), pltpu.SemaphoreType.DMA((na,))]
    return _Plan(arrays, out_shape, sems, start, wait, list)


def _adam_math(w, g, m, v):
    m = ADAM_B1 * m + (1.0 - ADAM_B1) * g
    v = ADAM_B2 * v + (1.0 - ADAM_B2) * jnp.square(g)
    m_hat = m / (1.0 - ADAM_B1 ** ADAM_STEP)
    v_hat = v / (1.0 - ADAM_B2 ** ADAM_STEP)
    delta = -ADAM_LR * (m_hat / (jnp.sqrt(v_hat) + ADAM_EPS) + ADAM_WD * w)
    return delta, m, v


WHOLE_BLOCK_BYTES = 2 * 1024 * 1024


def _row_tile(r, c):
    return r if 4 * r * c <= WHOLE_BLOCK_BYTES else _tile(r, (256, 128, 64, 32, 16, 8))


def _sum_parts(name, parts):
    n, r, c = parts.shape
    tr = _row_tile(r, c)

    def body(p_ref, o_ref):
        tot = p_ref[0].astype(F32)
        for i in range(1, n):
            tot = tot + p_ref[i].astype(F32)
        o_ref[...] = tot

    return pl.pallas_call(
        body, name=name, grid=(r // tr,), in_specs=[pl.BlockSpec((n, tr, c), lambda i: (0, i, 0))],
        out_specs=pl.BlockSpec((tr, c), lambda i: (i, 0)), out_shape=jax.ShapeDtypeStruct((r, c), F32),
        compiler_params=_params(("parallel",)),
    )(parts)


def _pair_sum(name, piece, other, dtype):
    n, r, c = piece.shape
    half = r // 2
    tr = _row_tile(half, c)

    def body(lo_ref, hi_ref, other_ref, o_ref):
        own = jnp.where(lax.axis_index("c") == 0, lo_ref[...], hi_ref[...])
        o_ref[...] = (own + other_ref[...]).astype(o_ref.dtype)

    blk = pl.BlockSpec((None, tr, c), lambda j, i: (j, i, 0))
    return pl.pallas_call(
        body, name=name, grid=(n, half // tr),
        in_specs=[pl.BlockSpec((None, None, tr, c), lambda j, i: (j, 0, i, 0)),
                  pl.BlockSpec((None, None, tr, c), lambda j, i: (j, 1, i, 0)), blk],
        out_specs=blk, out_shape=jax.ShapeDtypeStruct((n, half, c), dtype),
        compiler_params=_params(("parallel", "parallel")),
    )(piece.reshape(n, 2, half, c), piece.reshape(n, 2, half, c), other)


def _adamw(name, w, parts, m, v, hosted=None):
    n, r, c = parts.shape
    tr = _row_tile(r, c)
    steps = r // tr
    plan = hosted or _NO_PLAN
    nh = len(plan.arrays)

    def body(w_ref, p_ref, m_ref, v_ref, *rest):
        host_in, (g_ref, d_ref, nm_ref, nv_ref) = rest[:nh], rest[nh:nh + 4]
        host_out, sems = rest[nh + 4:2 * nh + 4], rest[2 * nh + 4:]

        @pl.when(pl.program_id(0) == 0)
        def _():
            plan.start(host_in, host_out, sems)

        g = p_ref[0].astype(F32)
        for i in range(1, n):
            g = g + p_ref[i].astype(F32)
        delta, nm, nv = _adam_math(w_ref[...], g, m_ref[...], v_ref[...])
        g_ref[...], d_ref[...], nm_ref[...], nv_ref[...] = g, delta, nm, nv

        @pl.when(pl.program_id(0) == steps - 1)
        def _():
            plan.wait(host_in, host_out, sems)

    blk = pl.BlockSpec((tr, c), lambda i: (i, 0))
    any_spec = pl.BlockSpec(memory_space=pl.ANY)
    res = pl.pallas_call(
        body, name=name, grid=(steps,),
        in_specs=[blk, pl.BlockSpec((n, tr, c), lambda i: (0, i, 0)), blk, blk] + [any_spec] * nh,
        out_specs=[blk] * 4 + [any_spec] * nh,
        out_shape=[jax.ShapeDtypeStruct((r, c), F32)] * 4 + plan.out_shape, scratch_shapes=plan.sems,
        compiler_params=_params(("arbitrary",) if nh else ("parallel",)),
    )(w, parts, m, v, *plan.arrays)
    return (res[:4], plan.finish(res[4:])) if nh else res


def _ada_w_update(act_t, dmod, w, m, v, hosted):
    r, c = w.shape
    nb = act_t.shape[1]
    tr, tc = 256, 1024
    grid = (r // tr, c // tc)
    nh = len(hosted.arrays)

    def body(a_ref, d_ref, w_ref, m_ref, v_ref, *rest):
        host_in, (g_ref, dl_ref, nm_ref, nv_ref) = rest[:nh], rest[nh:nh + 4]
        host_out, sems = rest[nh + 4:2 * nh + 4], rest[2 * nh + 4:]
        i, j = pl.program_id(0), pl.program_id(1)

        @pl.when(jnp.logical_and(i == 0, j == 0))
        def _():
            hosted.start(host_in, host_out, sems)

        a, dm = a_ref[...], d_ref[...]
        g = a[:, 0:1] * dm[0:1, :]
        for b in range(1, nb):
            g = g + a[:, b:b + 1] * dm[b:b + 1, :]
        delta, nm, nv = _adam_math(w_ref[...], g, m_ref[...], v_ref[...])
        g_ref[...], dl_ref[...], nm_ref[...], nv_ref[...] = g, delta, nm, nv

        @pl.when(jnp.logical_and(i == grid[0] - 1, j == grid[1] - 1))
        def _():
            hosted.wait(host_in, host_out, sems)

    blk = pl.BlockSpec((tr, tc), lambda i, j: (i, j))
    any_spec = pl.BlockSpec(memory_space=pl.ANY)
    res = pl.pallas_call(
        body, name="ada_w_update", grid=grid,
        in_specs=[pl.BlockSpec((tr, nb), lambda i, j: (i, 0)), pl.BlockSpec((nb, tc), lambda i, j: (0, j)),
                  blk, blk, blk] + [any_spec] * nh,
        out_specs=[blk] * 4 + [any_spec] * nh,
        out_shape=[jax.ShapeDtypeStruct((r, c), F32)] * 4 + hosted.out_shape,
        scratch_shapes=hosted.sems,
        compiler_params=_params(("arbitrary", "arbitrary")),
    )(act_t, dmod, w, m, v, *hosted.arrays)
    return res[:4], hosted.finish(res[4:])


WEIGHTS = ['ada_w', 'ada_b', 'norm1_gain', 'norm2_gain', 'final_gain', 'w_in', 'w_out', 's5_lambda_re',
           's5_lambda_im', 's5_log_step', 's5_b_re', 's5_b_im', 's5_c_re', 's5_c_im', 's5_d', 's5_w_glu',
           's5_b_glu', 'rk_shift_prev', 'rk_shift_next', 'rk_w0', 'rk_w_up', 'rk_a0', 'rk_a_up', 'rk_g_up',
           'rk_k_k', 'rk_k_a', 'rk_r_k', 'rk_ln_gain', 'rk_ln_bias', 'ffn_w1', 'ffn_w2']
BIG_SHARDED = ['w_in', 'w_out', 's5_w_glu', 'ffn_w1', 'ffn_w2']
FFN_SHARDED = ['ffn_w1', 'ffn_w2']
RK_SHARDED = ['rk_w0', 'rk_a0', 'rk_w_up', 'rk_a_up', 'rk_g_up']
REPLICATED = ['ada_b', 'norm1_gain', 'norm2_gain', 'final_gain', 's5_lambda_re', 's5_lambda_im', 's5_log_step',
              's5_b_re', 's5_b_im', 's5_c_re', 's5_c_im', 's5_d', 's5_b_glu', 'rk_shift_prev', 'rk_shift_next',
              'rk_k_k', 'rk_k_a', 'rk_r_k', 'rk_ln_gain', 'rk_ln_bias']
PACK_COLS = 1024
N_CHIPS = 4
RK_ROWS = 420
RK_ROWS_PAD = 432


def _pack_rows(arrays, cols):
    return jnp.concatenate([a.reshape(-1, cols) for a in arrays], axis=0)


def _pack_flat(arrays):
    flat = jnp.concatenate([a.reshape(-1) for a in arrays])
    rows = -(-flat.shape[0] // PACK_COLS)
    return jnp.pad(flat, (0, rows * PACK_COLS - flat.shape[0])).reshape(rows, PACK_COLS)


def _unpack_flat(packed, like):
    flat, out, pos = packed.reshape(-1), [], 0
    for a in like:
        out.append(flat[pos:pos + a.size].reshape(a.shape))
        pos += a.size
    return out


def _cols_to_chips(full, n_rows):
    return jnp.transpose(full.reshape(n_rows, N_CHIPS, -1), (1, 0, 2))


def _chips_to_cols(parts):
    return jnp.transpose(parts, (1, 0, 2)).reshape(parts.shape[1], -1)


def kernel(x, c, ada_w, ada_b, norm1_gain, norm2_gain, final_gain, w_in, w_out, s5_lambda_re, s5_lambda_im, s5_log_step, s5_b_re, s5_b_im, s5_c_re, s5_c_im, s5_d, s5_w_glu, s5_b_glu, rk_shift_prev, rk_shift_next, rk_w0, rk_w_up, rk_a0, rk_a_up, rk_g_up, rk_k_k, rk_k_a, rk_r_k, rk_ln_gain, rk_ln_bias, ffn_w1, ffn_w2, loss_target, m_ada_w, m_ada_b, m_norm1_gain, m_norm2_gain, m_final_gain, m_w_in, m_w_out, m_s5_lambda_re, m_s5_lambda_im, m_s5_log_step, m_s5_b_re, m_s5_b_im, m_s5_c_re, m_s5_c_im, m_s5_d, m_s5_w_glu, m_s5_b_glu, m_rk_shift_prev, m_rk_shift_next, m_rk_w0, m_rk_w_up, m_rk_a0, m_rk_a_up, m_rk_g_up, m_rk_k_k, m_rk_k_a, m_rk_r_k, m_rk_ln_gain, m_rk_ln_bias, m_ffn_w1, m_ffn_w2, v_ada_w, v_ada_b, v_norm1_gain, v_norm2_gain, v_final_gain, v_w_in, v_w_out, v_s5_lambda_re, v_s5_lambda_im, v_s5_log_step, v_s5_b_re, v_s5_b_im, v_s5_c_re, v_s5_c_im, v_s5_d, v_s5_w_glu, v_s5_b_glu, v_rk_shift_prev, v_rk_shift_next, v_rk_w0, v_rk_w_up, v_rk_a0, v_rk_a_up, v_rk_g_up, v_rk_k_k, v_rk_k_a, v_rk_r_k, v_rk_ln_gain, v_rk_ln_bias, v_ffn_w1, v_ffn_w2):
    given = dict(locals())
    w = {n: given[n] for n in WEIGHTS}
    m = {n: given["m_" + n] for n in WEIGHTS}
    v = {n: given["v_" + n] for n in WEIGHTS}
    mx, my, mc = lax.axis_index("x"), lax.axis_index("y"), lax.axis_index("c")
    chip = 2 * mx + my
    dev = 2 * chip + mc
    xt, target = x[0], loss_target[0]

    def rk_rows(d):
        rows = _pack_rows([d[n] for n in RK_SHARDED], 256)
        return jnp.pad(rows, ((0, RK_ROWS_PAD - rows.shape[0]), (0, 0)))

    (c_all,), (w_in_parts,) = _run_plan("gather_first", _join_plans([
        _exchange_plan([c], ALL_PEERS, 8), _gather_halves_plan([w_in[0].astype(BF16)])]))

    (act,) = _rowwise("ada_act", lambda q: (q * _sigmoid(q),), [c_all.reshape(8, D_MODEL)], [], [(D_MODEL, F32)], 8)
    n_mod_cols = N_MOD * D_MODEL // N_CHIPS
    bias = jnp.broadcast_to(lax.dynamic_slice(ada_b, (0, chip * n_mod_cols), (1, n_mod_cols)), (8, n_mod_cols))
    mod_shard = _matmul("ada_fwd", act, ada_w[0], epilogue=_add_epilogue, extras=(bias,))
    (mod_parts,) = _exchange("gather_mod", [mod_shard], CHIP_PEERS, N_CHIPS)
    mod_all = _chips_to_cols(mod_parts)
    mod_mine = lax.dynamic_slice(mod_all, (dev, 0), (1, N_MOD * D_MODEL))
    mod = [mod_mine[:, i * D_MODEL:(i + 1) * D_MODEL] for i in range(N_MOD)]

    def mixer_weights(parts):
        w_out_parts, glu_parts, rk_full = parts

        def rk_piece(lo, hi, lead):
            return _chips_to_cols(rk_full[:, lo:hi]).reshape(lead + (RK_WIDTH,))

        zeros = jnp.zeros((LORA, RK_WIDTH), F32)
        w_up, a_up = rk_piece(4, 132, (2, LORA)), rk_piece(132, 260, (2, LORA))
        return {
            "w_out": w_out_parts.reshape(D_MODEL, D_MODEL), "s5_w_glu": glu_parts.reshape(S5_WIDTH, S5_WIDTH),
            "w0": list(rk_piece(0, 2, (2,))[:, None, :]), "a0": list(rk_piece(2, 4, (2,))[:, None, :]),
            "w_up": [jnp.concatenate([w_up[0], zeros]), jnp.concatenate([zeros, w_up[1]])],
            "a_up": [jnp.concatenate([a_up[0], zeros]), jnp.concatenate([zeros, a_up[1]])],
            "g_up": jnp.pad(rk_piece(260, 420, (GATE_LORA,)), ((0, GATE_PAD - GATE_LORA), (0, 0))),
        }

    wt = {
        "norm1_gain": norm1_gain, "norm2_gain": norm2_gain, "final_gain": final_gain.reshape(1, D_MODEL),
        "w_in": jnp.pad(_chips_to_cols(w_in_parts), ((0, 0), (0, PROJ_PAD - PROJ))),
        "mu_prev": jnp.pad(rk_shift_prev, ((0, 0), (0, RK_PAD - RK_IN))),
        "mu_next": jnp.pad(rk_shift_next, ((0, 0), (0, RK_PAD - RK_IN))),
        "lam_re": [s5_lambda_re[0, d].reshape(S5_CH, 1) for d in range(2)],
        "lam_im": [s5_lambda_im[0, d].reshape(S5_CH, 1) for d in range(2)],
        "log_step": [jnp.repeat(s5_log_step[0, d], S5_STATE).reshape(S5_CH, 1) for d in range(2)],
        "b_re": s5_b_re.reshape(S5_CH, S5_GROUP), "b_im": s5_b_im.reshape(S5_CH, S5_GROUP),
        "c_re": s5_c_re[0], "c_im": s5_c_im[0],
        "s5_d": s5_d, "s5_b_glu": s5_b_glu,
        "k_k": rk_k_k, "k_a": rk_k_a, "r_k": rk_r_k.reshape(1, RK_WIDTH),
        "ln_gain": rk_ln_gain, "ln_bias": rk_ln_bias,
    }

    ffn_shards = [w[n][0].astype(BF16) for n in FFN_SHARDED]
    mixer_shards = [w_out[0].astype(BF16), s5_w_glu[0].astype(BF16), rk_rows(w)]
    loss_rows, grad_x, g = _local_step(xt, target, mod, wt, ffn_shards=ffn_shards,
                                       mixer_shards=(mixer_shards, mixer_weights))
    loss = lax.psum(jnp.sum(loss_rows), ("x", "y", "c"))


    big_grads = {
        "w_in": _cols_to_chips(g["w_in"][:, :PROJ], D_MODEL),
        "w_out": g["w_out"].reshape(N_CHIPS, -1, D_MODEL),
        "s5_w_glu": g["s5_w_glu"].reshape(N_CHIPS, -1, S5_WIDTH),
    }
    rk_grads = jnp.concatenate([
        _cols_to_chips(jnp.concatenate(g["w0"]), 2), _cols_to_chips(jnp.concatenate(g["a0"]), 2),
        _cols_to_chips(jnp.concatenate([g["w_up"][0][:LORA], g["w_up"][1][LORA:]]), 2 * LORA),
        _cols_to_chips(jnp.concatenate([g["a_up"][0][:LORA], g["a_up"][1][LORA:]]), 2 * LORA),
        _cols_to_chips(g["g_up"][:GATE_LORA], GATE_LORA),
        jnp.zeros((N_CHIPS, RK_ROWS_PAD - RK_ROWS, 256), F32)], axis=1)
    local_small = {
        "ada_b": jnp.concatenate(g["mod"], axis=1),
        "norm1_gain": g["norm1_gain"], "norm2_gain": g["norm2_gain"], "final_gain": g["final_gain"],
        "s5_lambda_re": jnp.concatenate(g["lam_re"]), "s5_lambda_im": jnp.concatenate(g["lam_im"]),
        "s5_log_step": jnp.concatenate([q.reshape(S5_GROUPS, S5_STATE).sum(axis=1) for q in g["log_step"]]),
        "s5_b_re": g["b_re"], "s5_b_im": g["b_im"], "s5_c_re": g["c_re"], "s5_c_im": g["c_im"],
        "s5_d": g["s5_d"], "s5_b_glu": g["s5_b_glu"],
        "rk_shift_prev": g["mu_prev"][:, :RK_IN], "rk_shift_next": g["mu_next"][:, :RK_IN],
        "rk_k_k": g["k_k"], "rk_k_a": g["k_a"], "rk_r_k": g["r_k"],
        "rk_ln_gain": g["ln_gain"], "rk_ln_bias": g["ln_bias"],
    }
    late = [n for n in BIG_SHARDED if n not in FFN_SHARDED]
    late_pieces = [big_grads[n] for n in late] + [rk_grads]
    late_names = late + ["rk"]

    def whole(halves):
        return halves.reshape(1, 2 * halves.shape[1], halves.shape[2])

    ffn_halves = [_sum_parts("sum_" + n, a) for n, a in zip(FFN_SHARDED, g["ffn_arrived"])]
    from_sibling, ffn_pairs, (small_all,) = _run_plan("swap_late", _join_plans([
        _other_half_plan(late_pieces), _exchange_plan(ffn_halves, CORE_PEER, 2),
        _exchange_plan([_pack_flat([local_small[n] for n in REPLICATED]).astype(BF16)], ALL_PEERS, 8)]))
    late_sums = [_pair_sum("pair_" + n, piece, other, F32 if n == "rk" else BF16)
                 for n, piece, other in zip(late_names, late_pieces, from_sibling)]
    pairs = dict(zip(FFN_SHARDED, [whole(p) for p in ffn_pairs]))

    mod_rows = N_MOD * D_MODEL // PACK_COLS
    dmod_all = small_all[:, :mod_rows].reshape(8, N_MOD * D_MODEL).astype(F32)
    dmod = lax.dynamic_slice(dmod_all, (0, chip * n_mod_cols), (8, n_mod_cols))
    ada_res, arrived = _ada_w_update(act.T, dmod, ada_w[0], m_ada_w[0], v_ada_w[0],
                                     hosted=_exchange_plan(late_sums, CHIP_PEERS, N_CHIPS, scatter=True))
    late_halves = [_sum_parts("sum_" + n, a) for n, a in zip(late_names, arrived)]

    out = {"ada_w": [r[None] for r in ada_res]}
    first = FFN_SHARDED[0]
    res, swapped = _adamw("adamw_" + first, w[first][0], pairs[first], m[first][0], v[first][0],
                          hosted=_exchange_plan(late_halves, CORE_PEER, 2))
    out[first] = [r[None] for r in res]
    pairs.update(zip(late_names, [whole(p) for p in swapped]))
    for n in [FFN_SHARDED[1]] + late:
        out[n] = [r[None] for r in _adamw("adamw_" + n, w[n][0], pairs[n], m[n][0], v[n][0])]
    rk_res = _adamw("adamw_rk", rk_rows(w), pairs["rk"], rk_rows(m), rk_rows(v))
    for q in range(4):
        pieces, pos = [], 0
        for n in RK_SHARDED:
            rows = w[n].size // 256
            pieces.append(rk_res[q][pos:pos + rows].reshape(w[n].shape))
            pos += rows
        for n, piece in zip(RK_SHARDED, pieces):
            out.setdefault(n, []).append(piece)

    small_res = _adamw("adamw_small", _pack_flat([w[n] for n in REPLICATED]), small_all,
                       _pack_flat([m[n] for n in REPLICATED]), _pack_flat([v[n] for n in REPLICATED]))
    for q in range(4):
        for n, piece in zip(REPLICATED, _unpack_flat(small_res[q], [w[n] for n in REPLICATED])):
            out.setdefault(n, []).append(piece)

    return (loss, grad_x[None], *[out[n][0] for n in WEIGHTS], *[out[n][1] for n in WEIGHTS],
            *[out[n][2] for n in WEIGHTS], *[out[n][3] for n in WEIGHTS])
```

```python
import functools
import math

import jax
import jax.numpy as jnp
from jax import lax
from jax.experimental import pallas as pl
from jax.experimental.pallas import tpu as pltpu

F32 = jnp.float32
BF16 = jnp.bfloat16

D_MODEL = 2048
S5_WIDTH = 1024
S5_GROUP = 16
S5_GROUPS = 64
S5_STATE = 64
S5_CH = S5_GROUPS * S5_STATE
S5_BLK = 256
RK_WIDTH = 1024
RK_HEAD = 64
RK_HEADS = 16
LORA = 64
GATE_LORA = 160
GATE_PAD = 256
RK_IN = 3488
RK_PAD = 3584
PROJ = 4512
PROJ_PAD = 4608
FFN = 8192
N_MOD = 6
NORM_EPS = 1e-6
GN_EPS = 64e-5
L2_EPS = 1e-12
RK_CHUNK = 64
RK_PASSES = {"solve": 3, "kt": 1, "s0": 1, "akk_v": 1, "ark_v": 1, "arb_u": 1, "state": 1}
LW_SCALE = math.exp(-0.5)
ADAM_LR, ADAM_B1, ADAM_B2, ADAM_EPS, ADAM_WD, ADAM_STEP = 0.001, 0.9, 0.999, 1e-08, 0.01, 10
VMEM_LIMIT = 56 * 1024 * 1024
HI = lax.Precision.HIGHEST


def _params(sem=None):
    return pltpu.CompilerParams(dimension_semantics=sem, vmem_limit_bytes=VMEM_LIMIT)


def _full(a):
    nd = a.ndim
    return pl.BlockSpec(a.shape, lambda *_: (0,) * nd)


@jax.custom_vjp
def _bdot(a, b):
    return jnp.dot(a.astype(BF16), b.astype(BF16), preferred_element_type=F32)


def _bdot_fwd(a, b):
    return _bdot(a, b), (a, b)


def _bdot_bwd(res, g):
    a, b = res
    gb = g.astype(BF16)
    da = lax.dot_general(gb, b.astype(BF16), (((1,), (1,)), ((), ())), preferred_element_type=F32)
    db = lax.dot_general(a.astype(BF16), gb, (((0,), (0,)), ((), ())), preferred_element_type=F32)
    return da, db


_bdot.defvjp(_bdot_fwd, _bdot_bwd)


@jax.custom_vjp
def _seg_dot(x, ind, ind_t):
    hi = x.astype(BF16)
    lo = (x - hi.astype(F32)).astype(BF16)
    both = jnp.dot(jnp.concatenate([hi, lo], axis=0), ind.astype(BF16), preferred_element_type=F32)
    return both[:x.shape[0]] + both[x.shape[0]:]


_seg_dot.defvjp(lambda x, ind, ind_t: (_seg_dot(x, ind, ind_t), (ind, ind_t)),
                lambda res, g: (_seg_dot(g, res[1], res[0]), jnp.zeros_like(res[0]), jnp.zeros_like(res[1])))


def _sigmoid(z):
    return 1.0 / (1.0 + jnp.exp(-z))


def _gelu(y):
    return 0.5 * y * (1.0 + jnp.tanh(0.7978845608028654 * (y + 0.044715 * (y * y * y))))


def _rms(x):
    return x * lax.rsqrt(jnp.mean(x * x, axis=-1, keepdims=True) + NORM_EPS)


def _tile(n, prefs):
    for t in prefs:
        if n % t == 0:
            return t
    return n


def _matmul(name, a, b, ta=False, tb=False, epilogue=None, extras=(), out_dtypes=(F32,), chips=None, hosted=None):
    m = a.shape[1] if ta else a.shape[0]
    k = a.shape[0] if ta else a.shape[1]
    if chips == "b":
        assert not tb and b.shape[1] == k
        n = N_CHIPS * b.shape[2]
    elif chips == "b_t":
        assert tb and N_CHIPS * b.shape[2] == k
        n = b.shape[1]
    else:
        n = b.shape[0] if tb else b.shape[1]
        assert k == (b.shape[1] if tb else b.shape[0]), (a.shape, b.shape, ta, tb)
    split = N_CHIPS if chips in ("b", "out") else 1
    tm = _tile(m, (1024, 512, 256, 128))
    tn = _tile(n // split, (1024, 768, 512, 256, 128))
    tk = k // N_CHIPS if chips == "b_t" else _tile(k, (2048, 1024, 512, 256, 128))
    nk = k // tk
    per = n // split // tn
    n_ex, n_out = len(extras), len(out_dtypes)
    dims = (((0 if ta else 1,), (1 if tb else 0,)), ((), ()))

    hosted = hosted or _NO_PLAN
    nh = len(hosted.arrays)
    grid = (m // tm, split, per, nk)

    def body(a_ref, b_ref, *rest):
        ex_refs, host_in = rest[:n_ex], rest[n_ex:n_ex + nh]
        out_refs, host_out = rest[n_ex + nh:n_ex + nh + n_out], rest[n_ex + nh + n_out:n_ex + 2 * nh + n_out]
        acc, sems = rest[n_ex + 2 * nh + n_out], rest[n_ex + 2 * nh + n_out + 1:]
        kk = pl.program_id(3)
        if nh:
            ids = [pl.program_id(d) for d in range(4)]
            first = functools.reduce(jnp.logical_and, [i == 0 for i in ids])
            last = functools.reduce(jnp.logical_and, [i == g - 1 for i, g in zip(ids, grid)])

            @pl.when(first)
            def _():
                hosted.start(host_in, host_out, sems)

        @pl.when(kk == 0)
        def _():
            acc[...] = jnp.zeros_like(acc)

        acc[...] += lax.dot_general(a_ref[...].astype(BF16), b_ref[...].astype(BF16), dims,
                                    preferred_element_type=F32)

        @pl.when(kk == nk - 1)
        def _():
            res = acc[...]
            outs = epilogue(res, *[e[...] for e in ex_refs]) if epilogue is not None else (res,)
            for o_ref, val in zip(out_refs, outs):
                o_ref[...] = val.astype(o_ref.dtype)

        if nh:
            @pl.when(last)
            def _():
                hosted.wait(host_in, host_out, sems)

    if ta:
        a_spec = pl.BlockSpec((tk, tm), lambda i, c, j, q: (q, i))
    else:
        a_spec = pl.BlockSpec((tm, tk), lambda i, c, j, q: (i, q))
    if chips == "b":
        b_spec = pl.BlockSpec((None, tk, tn), lambda i, c, j, q: (c, q, j))
    elif chips == "b_t":
        b_spec = pl.BlockSpec((None, tn, tk), lambda i, c, j, q: (q, j, 0))
    elif tb:
        b_spec = pl.BlockSpec((tn, tk), lambda i, c, j, q: (c * per + j, q))
    else:
        b_spec = pl.BlockSpec((tk, tn), lambda i, c, j, q: (q, c * per + j))
    mn_spec = pl.BlockSpec((tm, tn), lambda i, c, j, q: (i, c * per + j))
    if chips == "out":
        out_spec = pl.BlockSpec((None, tm, tn), lambda i, c, j, q: (c, i, j))
        out_shape = [jax.ShapeDtypeStruct((N_CHIPS, m, n // N_CHIPS), dt) for dt in out_dtypes]
    else:
        out_spec, out_shape = mn_spec, [jax.ShapeDtypeStruct((m, n), dt) for dt in out_dtypes]
    any_spec = pl.BlockSpec(memory_space=pl.ANY)
    order = ("arbitrary",) * 4 if nh else ("parallel", "parallel", "parallel", "arbitrary")
    outs = pl.pallas_call(
        body, name=name, grid=grid,
        in_specs=[a_spec, b_spec] + [mn_spec] * n_ex + [any_spec] * nh,
        out_specs=[out_spec] * n_out + [any_spec] * nh, out_shape=out_shape + hosted.out_shape,
        scratch_shapes=[pltpu.VMEM((tm, tn), F32)] + hosted.sems,
        compiler_params=_params(order),
    )(a, b, *extras, *hosted.arrays)
    res = outs[0] if n_out == 1 else outs[:n_out]
    return (res, hosted.finish(outs[n_out:])) if nh else res


def _row_spec(a, tm):
    return pl.BlockSpec((tm, a.shape[1]), lambda i: (i, 0))


def _rowwise(name, fn, rows, params, outs, tm):
    t = rows[0].shape[0]
    tm = min(tm, t)
    n_r, n_p = len(rows), len(params)

    def body(*refs):
        vals = [r[...] for r in refs[:n_r + n_p]]
        res = fn(*vals)
        for o_ref, val in zip(refs[n_r + n_p:], res):
            o_ref[...] = val.astype(o_ref.dtype)

    res = pl.pallas_call(
        body, name=name, grid=(t // tm,),
        in_specs=[_row_spec(r, tm) for r in rows] + [_full(p) for p in params],
        out_specs=[pl.BlockSpec((tm, n), lambda i: (i, 0)) for n, _ in outs],
        out_shape=[jax.ShapeDtypeStruct((t, n), dt) for n, dt in outs],
        compiler_params=_params(("parallel",)),
    )(*rows, *params)
    return res


def _rowwise_vjp(name, fn, rows, params, cts, row_grads, param_grads, tm, consts=(), addends=None,
                 emit=(), row_grad_dtypes=None):
    t = rows[0].shape[0]
    tm = min(tm, t)
    addends = addends or {}
    n_r, n_p, n_c = len(rows), len(params), len(consts)
    ct_flat = [c for group in cts for c in group]
    add_list = [addends[q] for q in sorted(addends)]
    n_ct, n_add = len(ct_flat), len(add_list)
    row_grad_dtypes = row_grad_dtypes or [F32] * len(row_grads)

    def body(*refs):
        pos = 0
        row_v = [r[...].astype(F32) for r in refs[pos:pos + n_r]]; pos += n_r
        par_v = [r[...].astype(F32) for r in refs[pos:pos + n_p]]; pos += n_p
        con_v = [r[...] for r in refs[pos:pos + n_c]]; pos += n_c
        ct_v = [r[...].astype(F32) for r in refs[pos:pos + n_ct]]; pos += n_ct
        add_v = [r[...] for r in refs[pos:pos + n_add]]; pos += n_add
        emit_refs = refs[pos:pos + len(emit)]; pos += len(emit)
        rg_refs = refs[pos:pos + len(row_grads)]; pos += len(row_grads)
        pg_refs = refs[pos:pos + len(param_grads)]

        def diff_fn(*dargs):
            rv, pv = list(row_v), list(par_v)
            for q, i in enumerate(row_grads):
                rv[i] = dargs[q]
            for q, j in enumerate(param_grads):
                pv[j] = dargs[len(row_grads) + q]
            return fn(*rv, *pv, *con_v)

        prim = [row_v[i] for i in row_grads] + [par_v[j] for j in param_grads]
        res, vjp = jax.vjp(diff_fn, *prim)
        ct_vals, q = [], 0
        for o, group in zip(res, cts):
            tot = jnp.zeros_like(o)
            for _ in group:
                tot = tot + ct_v[q]
                q += 1
            ct_vals.append(tot)
        grads = vjp(tuple(ct_vals))
        for e_ref, idx in zip(emit_refs, emit):
            e_ref[...] = res[idx].astype(e_ref.dtype)
        add_pos = {p: q for q, p in enumerate(sorted(addends))}
        for q, g_ref in enumerate(rg_refs):
            g = grads[q]
            if q in add_pos:
                g = g + add_v[add_pos[q]]
            g_ref[...] = g.astype(g_ref.dtype)

        @pl.when(pl.program_id(0) == 0)
        def _():
            for g_ref in pg_refs:
                g_ref[...] = jnp.zeros_like(g_ref)

        for q, g_ref in enumerate(pg_refs):
            g_ref[...] += grads[len(row_grads) + q]

    emit_shapes = []
    if emit:
        probe = jax.eval_shape(lambda *a: fn(*a), *[jax.ShapeDtypeStruct((tm, r.shape[1]), F32) for r in rows],
                               *[jax.ShapeDtypeStruct(p.shape, p.dtype) for p in params],
                               *[jax.ShapeDtypeStruct(c.shape, c.dtype) for c in consts])
        emit_shapes = [probe[idx].shape[1] for idx in emit]
    out_specs = ([pl.BlockSpec((tm, n), lambda i: (i, 0)) for n in emit_shapes]
                 + [_row_spec(rows[i], tm) for i in row_grads]
                 + [_full(params[j]) for j in param_grads])
    out_shape = ([jax.ShapeDtypeStruct((t, n), F32) for n in emit_shapes]
                 + [jax.ShapeDtypeStruct(rows[i].shape, dt) for i, dt in zip(row_grads, row_grad_dtypes)]
                 + [jax.ShapeDtypeStruct(params[j].shape, F32) for j in param_grads])
    return pl.pallas_call(
        body, name=name, grid=(t // tm,),
        in_specs=([_row_spec(r, tm) for r in rows] + [_full(p) for p in params] + [_full(c) for c in consts]
                  + [_row_spec(c, tm) for c in ct_flat] + [_row_spec(a, tm) for a in add_list]),
        out_specs=out_specs, out_shape=out_shape,
        compiler_params=_params(("arbitrary",)),
    )(*rows, *params, *consts, *ct_flat, *add_list)


def _norm_mod_fn(x, gain, scale, shift):
    return (_rms(x) * gain * (1.0 + scale) + shift,)


def _resid_norm_mod_fn(x, mixed, gate, gain, scale, shift):
    x1 = x + gate * mixed
    return x1, _rms(x1) * gain * (1.0 + scale) + shift


def _loss_fn(x1, ffn, target, gate, gain):
    y = _rms(x1 + gate * ffn) * gain
    err = y - target
    return (0.5 * jnp.mean(err * err, axis=-1, keepdims=True),)


def _s5_out_fn(ylin, u, d_skip, w_glu, b_glu):
    z = _gelu(ylin + d_skip * u)
    return (z * _sigmoid(_bdot(z, w_glu) + b_glu),)


def _rk_pre_fn(k, wdn, adn, gdn, w0_0, w0_1, wup_0, wup_1, a0_0, a0_1, aup_0, aup_1, g_up, k_k, k_a, seg, seg_t):
    kkr = k * k_k
    inv = 1.0 / jnp.sqrt(jnp.maximum(_seg_dot(kkr * kkr, seg, seg_t), L2_EPS * L2_EPS))
    kk = kkr * _seg_dot(inv, seg_t, seg)
    tw = jnp.tanh(wdn)
    lws, kds, acts = [], [], []
    for w0, wup, a0, aup in ((w0_0, wup_0, a0_0, aup_0), (w0_1, wup_1, a0_1, aup_1)):
        lws.append(-LW_SCALE * _sigmoid(w0 + _bdot(tw, wup)))
        act = _sigmoid(a0 + _bdot(adn, aup))
        acts.append(act)
        kds.append(k * (1.0 + (act - 1.0) * k_a))
    gate = _bdot(_sigmoid(gdn), g_up)
    return (kk, lws[0], lws[1], kds[0], kds[1], acts[0], acts[1], gate)


def _rk_post_fn(y0, y1, r, v, kd0, kd1, gate, ln_gain, ln_bias, r_k, seg, seg_t):
    y = y0 + y1
    mu = _seg_dot(_seg_dot(y, seg, seg_t) * (1.0 / RK_HEAD), seg_t, seg)
    yc = y - mu
    var = _seg_dot(yc * yc, seg, seg_t) * (1.0 / RK_HEAD)
    yn = yc * _seg_dot(lax.rsqrt(var + GN_EPS), seg_t, seg) * ln_gain + ln_bias
    bonus = _seg_dot(_seg_dot(r * (kd0 + kd1) * r_k, seg, seg_t), seg_t, seg)
    return ((yn + bonus * v) * gate,)


def _s5_prep_fn(lr0, li0, ls0, lr1, li1, ls1, b_re, b_im):
    outs = []
    for lam_re, lam_im, ls in ((lr0, li0, ls0), (lr1, li1, ls1)):
        step = jnp.exp(ls)
        mag = jnp.exp(lam_re * step)
        lbar_re = mag * jnp.cos(lam_im * step)
        lbar_im = mag * jnp.sin(lam_im * step)
        den = lam_re * lam_re + lam_im * lam_im
        nr = lbar_re - 1.0
        coef_re = (nr * lam_re + lbar_im * lam_im) / den
        coef_im = (lbar_im * lam_re - nr * lam_im) / den
        outs += [lbar_re, lbar_im, coef_re * b_re - coef_im * b_im, coef_re * b_im + coef_im * b_re]
    return tuple(outs)


def _shift_rows(x, down):
    t = x.shape[0]
    rows = lax.broadcasted_iota(jnp.int32, x.shape, 0)
    if down:
        return jnp.where(rows >= 1, pltpu.roll(x, 1, 0), 0.0)
    return jnp.where(rows < t - 1, pltpu.roll(x, t - 1, 0), 0.0)


def _token_shift(src, mu_prev, mu_next, first):
    t, n = src.shape[0], mu_prev.shape[1]

    def body(p_ref, mp_ref, mn_ref, o_ref):
        x = p_ref[...]
        o_ref[...] = x + mp_ref[...] * (_shift_rows(x, True) - x) + mn_ref[...] * (_shift_rows(x, False) - x)

    col = pl.BlockSpec((t, 128), lambda j: (0, j))
    par = pl.BlockSpec((1, 128), lambda j: (0, j))
    return pl.pallas_call(
        body, name="token_shift", grid=(n // 128,),
        in_specs=[pl.BlockSpec((t, 128), lambda j: (0, j + first)), par, par], out_specs=col,
        out_shape=jax.ShapeDtypeStruct((t, n), F32), compiler_params=_params(("parallel",)),
    )(src, mu_prev, mu_next)


def _token_shift_bwd(src, mu_prev, mu_next, dps, first):
    t, n = dps.shape

    def body(p_ref, mp_ref, mn_ref, d_ref, dp_ref, dmp_ref, dmn_ref):
        x, d, mp, mn = p_ref[...], d_ref[...], mp_ref[...], mn_ref[...]
        dp_ref[...] = d * (1.0 - mp - mn) + _shift_rows(d * mp, False) + _shift_rows(d * mn, True)
        dmp_ref[...] = jnp.sum(d * (_shift_rows(x, True) - x), axis=0, keepdims=True)
        dmn_ref[...] = jnp.sum(d * (_shift_rows(x, False) - x), axis=0, keepdims=True)

    col = pl.BlockSpec((t, 128), lambda j: (0, j))
    par = pl.BlockSpec((1, 128), lambda j: (0, j))
    return pl.pallas_call(
        body, name="token_shift_bwd", grid=(n // 128,),
        in_specs=[pl.BlockSpec((t, 128), lambda j: (0, j + first)), par, par, col],
        out_specs=[col, par, par],
        out_shape=[jax.ShapeDtypeStruct((t, n), F32), jax.ShapeDtypeStruct((1, n), F32),
                   jax.ShapeDtypeStruct((1, n), F32)],
        compiler_params=_params(("parallel",)),
    )(src, mu_prev, mu_next, dps)


N_SEG = 32
S5_BLOCKS = 32
S5_PER_IN = 4


def _scan_in_place(sr_ref, si_ref, ar, ai, carry_ref, reverse):
    seg_len = sr_ref.shape[0] // N_SEG
    ng = N_SEG // 8

    def rows(i, grp):
        first = (seg_len - 1 - i if reverse else i) * N_SEG + 8 * grp
        return pl.ds(pl.multiple_of(first, 8), 8)

    zero = jnp.zeros((8, 128), F32)
    one = jnp.ones((8, 128), F32)

    def local(i, c):
        pr, pi = c[-2:]
        out = []
        for grp in range(ng):
            sr, si = c[2 * grp], c[2 * grp + 1]
            nr = ar * sr - ai * si + sr_ref[rows(i, grp), :]
            ni = ar * si + ai * sr + si_ref[rows(i, grp), :]
            sr_ref[rows(i, grp), :] = nr
            si_ref[rows(i, grp), :] = ni
            out += [nr, ni]
        return tuple(out) + (ar * pr - ai * pi, ar * pi + ai * pr)

    ends = lax.fori_loop(0, seg_len, local, (zero,) * (2 * ng) + (one, zero))
    qr, qi = ends[-2][0:1], ends[-1][0:1]
    order = list(range(N_SEG - 1, -1, -1)) if reverse else list(range(N_SEG))
    cr = jnp.zeros((1, 128), F32)
    ci = jnp.zeros((1, 128), F32)
    for j in order:
        carry_ref[j:j + 1, :] = cr
        carry_ref[N_SEG + j:N_SEG + j + 1, :] = ci
        grp, sub = divmod(j, 8)
        lr, li = ends[2 * grp][sub:sub + 1], ends[2 * grp + 1][sub:sub + 1]
        cr, ci = lr + qr * cr - qi * ci, li + qr * ci + qi * cr
    carries = [(carry_ref[8 * grp:8 * grp + 8, :], carry_ref[N_SEG + 8 * grp:N_SEG + 8 * grp + 8, :])
               for grp in range(ng)]

    def fix(i, c):
        pr, pi = c
        npr, npi = ar * pr - ai * pi, ar * pi + ai * pr
        for grp in range(ng):
            cr8, ci8 = carries[grp]
            sr_ref[rows(i, grp), :] = sr_ref[rows(i, grp), :] + npr * cr8 - npi * ci8
            si_ref[rows(i, grp), :] = si_ref[rows(i, grp), :] + npr * ci8 + npi * cr8
        return npr, npi

    lax.fori_loop(0, seg_len, fix, (one, zero))


def _interleave(x):
    t, c = x.shape
    return jnp.transpose(x.reshape(N_SEG, t // N_SEG, c), (1, 0, 2)).reshape(t, c)


def _deinterleave(x):
    t, c = x.shape
    return jnp.transpose(x.reshape(t // N_SEG, N_SEG, c), (1, 0, 2)).reshape(t, c)


def _lag_sums(lr_ref, li_ref, sr_ref, si_ref, earlier):
    t = lr_ref.shape[0]
    body, edge = pl.ds(N_SEG, t - N_SEG), pl.ds(0, N_SEG)
    far = pl.ds(t - N_SEG, N_SEG)
    rows = lax.broadcasted_iota(jnp.int32, (N_SEG, 128), 0)
    if earlier:
        lam_main, s_main, lam_edge = body, pl.ds(0, t - N_SEG), edge
        wrap = lambda ref: jnp.where(rows >= 1, pltpu.roll(ref[far, :], 1, 0), 0.0)
    else:
        lam_main, s_main, lam_edge = pl.ds(0, t - N_SEG), body, far
        wrap = lambda ref: jnp.where(rows < N_SEG - 1, pltpu.roll(ref[edge, :], N_SEG - 1, 0), 0.0)
    lr, li, sr, si = lr_ref[lam_main, :], li_ref[lam_main, :], sr_ref[s_main, :], si_ref[s_main, :]
    er, ei, pr, pi = lr_ref[lam_edge, :], li_ref[lam_edge, :], wrap(sr_ref), wrap(si_ref)
    re = jnp.sum(lr * sr + li * si, axis=0, keepdims=True) + jnp.sum(er * pr + ei * pi, axis=0, keepdims=True)
    im = jnp.sum(li * sr - lr * si, axis=0, keepdims=True) + jnp.sum(ei * pr - er * pi, axis=0, keepdims=True)
    return re, im


def _dot_bf16(a, b, dims=(((1,), (0,)), ((), ()))):
    return lax.dot_general(a.astype(BF16), b.astype(BF16), dims, preferred_element_type=F32)


NT_DIMS = (((1,), (1,)), ((), ()))
TN_DIMS = (((0,), (0,)), ((), ()))


def _s5_specs(t):
    blk = pl.BlockSpec((None, t, 128), lambda i, q: (S5_PER_IN * i + q, 0, 0))
    mat = pl.BlockSpec((None, 128, 128), lambda i, q: (S5_PER_IN * i + q, 0, 0))
    vec = pl.BlockSpec((None, 1, 128), lambda i, q: (S5_PER_IN * i + q, 0, 0))
    chan = pl.BlockSpec((t, 128), lambda i, q: (0, i))
    return blk, mat, vec, chan


S5_GRID = (S5_BLOCKS // S5_PER_IN, S5_PER_IN)


def _s5_forward(name, u, b_re, b_im, l_re, l_im, reverse, other=None, c_re=None, c_im_neg=None):
    t = u.shape[0]
    project = other is not None
    blk, mat, vec, chan = _s5_specs(t)

    def body(*refs):
        u_ref, br_ref, bi_ref, lr_ref, li_ref = refs[:5]
        if project:
            or_ref, oi_ref, cr_ref, ci_ref, sr_ref, si_ref, y_ref, carry_ref = refs[5:]
        else:
            sr_ref, si_ref, carry_ref = refs[5:]
        uv = u_ref[...]
        sr_ref[...] = _dot_bf16(uv, br_ref[...])
        si_ref[...] = _dot_bf16(uv, bi_ref[...])
        ar = jnp.broadcast_to(lr_ref[...], (8, 128))
        ai = jnp.broadcast_to(li_ref[...], (8, 128))
        _scan_in_place(sr_ref, si_ref, ar, ai, carry_ref, reverse)
        if project:
            y = (_dot_bf16(sr_ref[...] + or_ref[...], cr_ref[...])
                 + _dot_bf16(si_ref[...] + oi_ref[...], ci_ref[...]))

            @pl.when(pl.program_id(1) == 0)
            def _():
                y_ref[...] = y

            @pl.when(pl.program_id(1) != 0)
            def _():
                y_ref[...] += y

    state = jax.ShapeDtypeStruct((S5_BLOCKS, t, 128), F32)
    ins = [u, b_re, b_im, l_re, l_im] + ([other[0], other[1], c_re, c_im_neg] if project else [])
    in_specs = [chan, mat, mat, vec, vec] + ([blk, blk, mat, mat] if project else [])
    return pl.pallas_call(
        body, name=name, grid=S5_GRID, in_specs=in_specs,
        out_specs=[blk, blk] + ([chan] if project else []),
        out_shape=[state, state] + ([jax.ShapeDtypeStruct((t, S5_WIDTH), F32)] if project else []),
        scratch_shapes=[pltpu.VMEM((2 * N_SEG, 128), F32)],
        compiler_params=_params(("arbitrary", "arbitrary")),
    )(*ins)


def _s5_backward(name, dy, u, du_in, states, other, b_re, b_im, c_re, c_im_neg, l_re, l_im, reverse):
    t = u.shape[0]
    with_c = other is not None
    blk, mat, vec, chan = _s5_specs(t)

    def body(*refs):
        dy_ref, u_ref, du_in_ref, sr_ref, si_ref = refs[:5]
        pos = 5
        if with_c:
            or_ref, oi_ref = refs[5:7]
            pos = 7
        br_ref, bi_ref, cr_ref, ci_ref, lr_ref, li_ref = refs[pos:pos + 6]
        outs = refs[pos + 6:]
        du_ref, dbr_ref, dbi_ref, dlr_ref, dli_ref = outs[:5]
        lam_r, lam_i, carry_ref = outs[-3:]
        dyv, uv = dy_ref[...], u_ref[...]
        lam_r[...] = _dot_bf16(dyv, cr_ref[...], NT_DIMS)
        lam_i[...] = _dot_bf16(dyv, ci_ref[...], NT_DIMS)
        ar = jnp.broadcast_to(lr_ref[...], (8, 128))
        ai = -jnp.broadcast_to(li_ref[...], (8, 128))
        _scan_in_place(lam_r, lam_i, ar, ai, carry_ref, not reverse)
        lr, li = lam_r[...], lam_i[...]
        dlr_ref[...], dli_ref[...] = _lag_sums(lam_r, lam_i, sr_ref, si_ref, not reverse)
        dbr_ref[...] = _dot_bf16(uv, lr, TN_DIMS)
        dbi_ref[...] = _dot_bf16(uv, li, TN_DIMS)
        du = _dot_bf16(lr, br_ref[...], NT_DIMS) + _dot_bf16(li, bi_ref[...], NT_DIMS)

        @pl.when(pl.program_id(1) == 0)
        def _():
            du_ref[...] = du_in_ref[...] + du

        @pl.when(pl.program_id(1) != 0)
        def _():
            du_ref[...] += du

        if with_c:
            dcr_ref, dci_ref = outs[5:7]
            dcr_ref[...] = _dot_bf16(sr_ref[...] + or_ref[...], dyv, TN_DIMS)
            dci_ref[...] = _dot_bf16(si_ref[...] + oi_ref[...], dyv, TN_DIMS)

    mats = jax.ShapeDtypeStruct((S5_BLOCKS, 128, 128), F32)
    vecs = jax.ShapeDtypeStruct((S5_BLOCKS, 1, 128), F32)
    ins = [dy, u, du_in, states[0], states[1]] + ([other[0], other[1]] if with_c else [])
    ins += [b_re, b_im, c_re, c_im_neg, l_re, l_im]
    in_specs = [chan, chan, chan, blk, blk] + ([blk, blk] if with_c else []) + [mat] * 4 + [vec] * 2
    return pl.pallas_call(
        body, name=name, grid=S5_GRID, in_specs=in_specs,
        out_specs=[chan, mat, mat, vec, vec] + ([mat, mat] if with_c else []),
        out_shape=[jax.ShapeDtypeStruct((t, S5_WIDTH), F32), mats, mats, vecs, vecs] + ([mats, mats] if with_c else []),
        scratch_shapes=[pltpu.VMEM((t, 128), F32), pltpu.VMEM((t, 128), F32), pltpu.VMEM((2 * N_SEG, 128), F32)],
        compiler_params=_params(("arbitrary", "arbitrary")),
    )(*ins)


def _ein(passes, spec, a, b):
    if passes == 6:
        return jnp.einsum(spec, a, b, precision=HI, preferred_element_type=F32)
    a_hi, b_hi = a.astype(BF16), b.astype(BF16)
    if passes == 1:
        return jnp.einsum(spec, a_hi, b_hi, preferred_element_type=F32)
    a_lo = (a - a_hi.astype(F32)).astype(BF16)
    b_lo = (b - b_hi.astype(F32)).astype(BF16)
    cross = jnp.einsum(spec, a_hi, b_lo, preferred_element_type=F32)
    if spec.startswith('hik'):
        m = a.shape[1]
        stacked = jnp.einsum(spec, jnp.concatenate([a_hi, a_lo], axis=1), b_hi, preferred_element_type=F32)
        return stacked[:, :m] + stacked[:, m:] + cross
    return (jnp.einsum(spec, a_hi, b_hi, preferred_element_type=F32) + cross
            + jnp.einsum(spec, a_lo, b_hi, preferred_element_type=F32))


@jax.custom_vjp
def _tri_mm(tri, tri_t, z):
    n = z.shape[2]
    hi = z.astype(BF16)
    rest = z - hi.astype(F32)
    mid = rest.astype(BF16)
    lo = (rest - mid.astype(F32)).astype(BF16)
    out = jnp.einsum('hik,hkj->hij', tri.astype(BF16), jnp.concatenate([hi, mid, lo], axis=2),
                     preferred_element_type=F32)
    return out[:, :, :n] + out[:, :, n:2 * n] + out[:, :, 2 * n:]


def _tri_mm_bwd(res, g):
    tri, tri_t = res
    return jnp.zeros_like(tri), jnp.zeros_like(tri_t), _tri_mm(tri_t, tri, g)


_tri_mm.defvjp(lambda tri, tri_t, z: (_tri_mm(tri, tri_t, z), (tri, tri_t)), _tri_mm_bwd)


def _chunk_cumsum(lw, incl, incl_t):
    shape = (lw.shape[0],) + incl.shape
    return _tri_mm(jnp.broadcast_to(incl.astype(F32), shape), jnp.broadcast_to(incl_t.astype(F32), shape), lw)


@functools.partial(jax.custom_vjp, nondiff_argnums=(0,))
def _bmm(p, a, b):
    return _ein(p, 'hik,hkj->hij', a, b)


@functools.partial(jax.custom_vjp, nondiff_argnums=(0,))
def _bmm_nt(p, a, b):
    return _ein(p, 'hik,hjk->hij', a, b)


@functools.partial(jax.custom_vjp, nondiff_argnums=(0,))
def _bmm_tn(p, a, b):
    return _ein(p, 'hki,hkj->hij', a, b)


_bmm.defvjp(lambda p, a, b: (_bmm(p, a, b), (a, b)),
            lambda p, res, g: (_bmm_nt(p, g, res[1]), _bmm_tn(p, res[0], g)))
_bmm_nt.defvjp(lambda p, a, b: (_bmm_nt(p, a, b), (a, b)),
               lambda p, res, g: (_bmm(p, g, res[1]), _bmm_tn(p, g, res[0])))
_bmm_tn.defvjp(lambda p, a, b: (_bmm_tn(p, a, b), (a, b)),
               lambda p, res, g: (_bmm_nt(p, res[1], g), _bmm(p, res[0], g)))


@jax.custom_vjp
def _split_rows(x):
    c = x.shape[1] // 2
    return x[:, :c], x[:, c:]


_split_rows.defvjp(lambda x: (_split_rows(x), None), lambda _, g: (jnp.concatenate(g, axis=1),))


def _stack_rows(a, b):
    return jnp.concatenate([a, b], axis=1)


def _nilpotent_inverse(l_mat):
    c = l_mat.shape[1]
    ps = RK_PASSES["solve"]
    row = lax.broadcasted_iota(jnp.int32, (c, c), 0)
    col = lax.broadcasted_iota(jnp.int32, (c, c), 1)
    x = -l_mat
    inv = jnp.where(row == col, 1.0, 0.0) + x
    power = _bmm(ps, x, x)
    span = 2
    while 2 * span < c:
        step, power = _split_rows(_bmm(ps, _stack_rows(inv, power), power))
        inv = inv + step
        span *= 2
    return inv + _bmm(ps, inv, power)


@jax.custom_vjp
def _solve_with(inv, l_mat, rhs):
    return _bmm(RK_PASSES["solve"], inv, rhs)


def _solve_with_fwd(inv, l_mat, rhs):
    u = _bmm(RK_PASSES["solve"], inv, rhs)
    return u, (inv, u)


def _solve_with_bwd(res, g):
    inv, u = res
    d_rhs = _bmm_tn(RK_PASSES["solve"], inv, g)
    return jnp.zeros_like(inv), -_bmm_nt(RK_PASSES["solve"], d_rhs, u), d_rhs


_solve_with.defvjp(_solve_with_fwd, _solve_with_bwd)


def _rk_chunk(s0, r, lw, k, v, kk, a, reverse, inv=None):
    h, c, n = r.shape
    row = lax.broadcasted_iota(jnp.int32, (c, c), 0)
    col = lax.broadcasted_iota(jnp.int32, (c, c), 1)
    incl = (row <= col) if reverse else (row >= col)
    strict = (row < col) if reverse else (row > col)
    cum = _chunk_cumsum(lw, incl, (row >= col) if reverse else (row <= col))
    g_in = jnp.exp(cum)
    g_inv = jnp.exp(-cum)
    kap = kk * jnp.exp(cum - lw)
    beta = kk * a * g_inv
    kt = k * g_inv
    rt = r * g_in
    p, ps = RK_PASSES, RK_PASSES["solve"]
    both = _stack_rows(kap, rt)
    kap_beta, rt_beta = _split_rows(_bmm_nt(ps, both, beta))
    kap_kt, rt_kt = _split_rows(_bmm_nt(p["kt"], both, kt))
    kap_s0, rt_s0 = _split_rows(_bmm_nt(p["s0"], both, s0))
    l_mat = jnp.where(strict, kap_beta, 0.0)
    rhs = kap_s0 + _bmm(p["akk_v"], jnp.where(strict, kap_kt, 0.0), v)
    if inv is None:
        inv = lax.stop_gradient(_nilpotent_inverse(l_mat))
    u = _solve_with(inv, l_mat, rhs)
    y = (rt_s0 + _bmm(p["ark_v"], jnp.where(incl, rt_kt, 0.0), v)
         - _bmm(p["arb_u"], jnp.where(incl, rt_beta, 0.0), u))
    s1 = ((s0 + _bmm_tn(p["state"], _stack_rows(v, -u), _stack_rows(kt, beta)))
          * jnp.exp(jnp.sum(lw, axis=1, keepdims=True)))
    return y, s1, inv


class _Plan:
    def __init__(self, arrays, out_shape, sems, start, wait, finish):
        self.arrays, self.out_shape, self.sems = list(arrays), list(out_shape), list(sems)
        self.start, self.wait, self.finish = start, wait, finish


_NO_PLAN = _Plan([], [], [], lambda *_: None, lambda *_: None, lambda outs: [])


def _join_plans(plans):
    def cut(seq, sizes):
        out, pos = [], 0
        for s in sizes:
            out.append(seq[pos:pos + s])
            pos += s
        return out

    n_arr, n_sem = [len(p.arrays) for p in plans], [len(p.sems) for p in plans]

    def run(which):
        def go(in_refs, out_refs, sems):
            for p, i, o, s in zip(plans, cut(in_refs, n_arr), cut(out_refs, n_arr), cut(sems, n_sem)):
                getattr(p, which)(i, o, s)
        return go

    return _Plan([a for p in plans for a in p.arrays], [s for p in plans for s in p.out_shape],
                 [s for p in plans for s in p.sems], run("start"), run("wait"),
                 lambda outs: [p.finish(o) for p, o in zip(plans, cut(outs, n_arr))])


def _split_heads(x):
    return jnp.stack([x[:, RK_HEAD * i:RK_HEAD * (i + 1)] for i in range(RK_HEADS)], axis=0)


def _store_heads(ref, x):
    for i in range(RK_HEADS):
        ref[:, RK_HEAD * i:RK_HEAD * (i + 1)] = x[i]


def _rk_core_fwd(name, r, lw, k, v, kk, a, reverse, chunk, hosted=None):
    t = r.shape[0]
    h, n = RK_HEADS, RK_HEAD
    nc = t // chunk

    def idx(i):
        return nc - 1 - i if reverse else i

    hosted = hosted or _NO_PLAN
    nh = len(hosted.arrays)

    def body(r_ref, lw_ref, k_ref, v_ref, kk_ref, a_ref, *rest):
        host_in, (y_ref, ck_ref, inv_ref), host_out = rest[:nh], rest[nh:nh + 3], rest[nh + 3:2 * nh + 3]
        s_ref, sems = rest[2 * nh + 3], rest[2 * nh + 4:]

        @pl.when(pl.program_id(0) == 0)
        def _():
            s_ref[...] = jnp.zeros_like(s_ref)
            hosted.start(host_in, host_out, sems)

        s0 = s_ref[...]
        ck_ref[0] = s0
        ops = [_split_heads(ref[...]) for ref in (r_ref, lw_ref, k_ref, v_ref, kk_ref, a_ref)]
        y, s1, inv = _rk_chunk(s0, *ops, reverse)
        _store_heads(y_ref, y)
        inv_ref[0] = inv
        s_ref[...] = s1

        @pl.when(pl.program_id(0) == nc - 1)
        def _():
            hosted.wait(host_in, host_out, sems)

    blk = pl.BlockSpec((chunk, RK_WIDTH), lambda i: (idx(i), 0))
    any_spec = pl.BlockSpec(memory_space=pl.ANY)
    res = pl.pallas_call(
        body, name=name, grid=(nc,), in_specs=[blk] * 6 + [any_spec] * nh,
        out_specs=[blk, pl.BlockSpec((1, h, n, n), lambda i: (idx(i), 0, 0, 0)),
                   pl.BlockSpec((1, h, chunk, chunk), lambda i: (idx(i), 0, 0, 0))] + [any_spec] * nh,
        out_shape=[jax.ShapeDtypeStruct((t, RK_WIDTH), F32), jax.ShapeDtypeStruct((nc, h, n, n), F32),
                   jax.ShapeDtypeStruct((nc, h, chunk, chunk), F32)] + hosted.out_shape,
        scratch_shapes=[pltpu.VMEM((h, n, n), F32)] + hosted.sems,
        compiler_params=_params(("arbitrary",)),
    )(r, lw, k, v, kk, a, *hosted.arrays)
    return res[0], (res[1], res[2]), hosted.finish(res[3:])


def _rk_core_bwd(name, r, lw, k, v, kk, a, ck, dy, reverse, chunk, hosted=None):
    t = r.shape[0]
    h, n = RK_HEADS, RK_HEAD
    nc = t // chunk
    hosted = hosted or _NO_PLAN
    nh = len(hosted.arrays)

    def idx(i):
        return i if reverse else nc - 1 - i

    def body(r_ref, lw_ref, k_ref, v_ref, kk_ref, a_ref, ck_ref, inv_ref, dy_ref, *rest):
        host_in, out_refs, host_out = rest[:nh], rest[nh:nh + 6], rest[nh + 6:2 * nh + 6]
        ds_ref, sems = rest[2 * nh + 6], rest[2 * nh + 7:]

        @pl.when(pl.program_id(0) == 0)
        def _():
            ds_ref[...] = jnp.zeros_like(ds_ref)
            hosted.start(host_in, host_out, sems)

        inv = inv_ref[0]

        def fn(*operands):
            return _rk_chunk(*operands, reverse=reverse, inv=inv)[:2]

        ops = [_split_heads(ref[...]) for ref in (r_ref, lw_ref, k_ref, v_ref, kk_ref, a_ref)]
        _, vjp = jax.vjp(fn, ck_ref[0], *ops)
        grads = vjp((_split_heads(dy_ref[...]), ds_ref[...]))
        ds_ref[...] = grads[0]
        for o_ref, g in zip(out_refs, grads[1:]):
            _store_heads(o_ref, g)

        @pl.when(pl.program_id(0) == nc - 1)
        def _():
            hosted.wait(host_in, host_out, sems)

    blk = pl.BlockSpec((chunk, RK_WIDTH), lambda i: (idx(i), 0))
    any_spec = pl.BlockSpec(memory_space=pl.ANY)
    res = pl.pallas_call(
        body, name=name, grid=(nc,),
        in_specs=[blk] * 6 + [pl.BlockSpec((1, h, n, n), lambda i: (idx(i), 0, 0, 0)),
                              pl.BlockSpec((1, h, chunk, chunk), lambda i: (idx(i), 0, 0, 0)), blk]
        + [any_spec] * nh,
        out_specs=[blk] * 6 + [any_spec] * nh,
        out_shape=[jax.ShapeDtypeStruct((t, RK_WIDTH), F32)] * 6 + hosted.out_shape,
        scratch_shapes=[pltpu.VMEM((h, n, n), F32)] + hosted.sems,
        compiler_params=_params(("arbitrary",)),
    )(r, lw, k, v, kk, a, ck[0], ck[1], dy, *hosted.arrays)
    return res[:6], hosted.finish(res[6:])


def _s5_band_place():
    return jax.nn.one_hot(jnp.arange(S5_BLOCKS) % S5_PER_IN, S5_PER_IN, dtype=F32)


def _s5_in_blocks(bbar):
    b = jnp.transpose(bbar.reshape(S5_BLOCKS, 2, S5_STATE, S5_GROUP), (0, 1, 3, 2))
    band = jnp.einsum('jghp,gk->jghkp', b, jnp.eye(2, dtype=F32)).reshape(S5_BLOCKS, 32, 128)
    return jnp.einsum('jrc,jq->jqrc', band, _s5_band_place()).reshape(S5_BLOCKS, 128, 128)


def _s5_in_unblock(mats):
    band = jnp.einsum('jqrc,jq->jrc', mats.reshape(S5_BLOCKS, S5_PER_IN, 32, 128), _s5_band_place())
    diag = jnp.einsum('jghgp->jghp', band.reshape(S5_BLOCKS, 2, S5_GROUP, 2, S5_STATE))
    return jnp.transpose(diag, (0, 1, 3, 2)).reshape(S5_CH, S5_GROUP)


def _s5_out_blocks(c):
    ct = jnp.transpose(c.reshape(S5_BLOCKS, 2, S5_GROUP, S5_STATE), (0, 1, 3, 2))
    band = jnp.einsum('jgph,gk->jgpkh', ct, jnp.eye(2, dtype=F32)).reshape(S5_BLOCKS, 128, 32)
    return jnp.einsum('jrc,jq->jrqc', band, _s5_band_place()).reshape(S5_BLOCKS, 128, 128)


def _s5_out_unblock(mats):
    band = jnp.einsum('jrqc,jq->jrc', mats.reshape(S5_BLOCKS, 128, S5_PER_IN, 32), _s5_band_place())
    diag = jnp.einsum('jgpgh->jgph', band.reshape(S5_BLOCKS, 2, S5_STATE, 2, S5_GROUP))
    return jnp.transpose(diag, (0, 1, 3, 2)).reshape(S5_GROUPS, S5_GROUP, S5_STATE)


def _head_indicator():
    ch = lax.broadcasted_iota(jnp.int32, (RK_WIDTH, 128), 0) // RK_HEAD
    hd = lax.broadcasted_iota(jnp.int32, (RK_WIDTH, 128), 1)
    seg = (ch == hd).astype(F32)
    return seg, seg.T


def _add_epilogue(acc, e):
    return (acc + e,)


def _local_step(x, target, mod, wt, chunk=RK_CHUNK, ffn_shards=None, mixer_shards=None):
    t = x.shape[0]
    wt = dict(wt)
    sh1, sc1, gt1, sh2, sc2, gt2 = mod
    seg, seg_t = _head_indicator()
    g = {}

    (h1,) = _rowwise("norm1", _norm_mod_fn, [x], [wt["norm1_gain"], sc1, sh1], [(D_MODEL, BF16)], 256)
    if mixer_shards is None:
        proj = _matmul("proj", h1, wt["w_in"])
    else:
        proj, gathered = _matmul("proj", h1, wt["w_in"], hosted=_gather_halves_plan(mixer_shards[0]))
        wt.update(mixer_shards[1](gathered))
    u = proj[:, :S5_WIDTH]
    ps = _token_shift(proj, wt["mu_prev"], wt["mu_next"], first=S5_WIDTH // 128)
    r, k, v = ps[:, :1024], ps[:, 1024:2048], ps[:, 2048:3072]
    wdn, adn, gdn = ps[:, 3072:3200], ps[:, 3200:3328], ps[:, 3328:RK_PAD]

    prep_rows = [wt["lam_re"][0], wt["lam_im"][0], wt["log_step"][0], wt["lam_re"][1], wt["lam_im"][1],
                 wt["log_step"][1], wt["b_re"], wt["b_im"]]
    col1, col16 = (1, F32), (S5_GROUP, F32)
    prep = _rowwise("s5_prep", _s5_prep_fn, prep_rows, [], [col1, col1, col16, col16] * 2, 512)
    lbar = [tuple(prep[4 * d + q].reshape(S5_BLOCKS, 1, 128) for q in range(2)) for d in range(2)]
    b_blk = [tuple(_s5_in_blocks(prep[4 * d + 2 + q]) for q in range(2)) for d in range(2)]
    c_blk = (_s5_out_blocks(wt["c_re"]), -_s5_out_blocks(wt["c_im"]))
    u_il = _interleave(u)
    state0 = _s5_forward("s5_fwd0", u_il, *b_blk[0], *lbar[0], reverse=False)
    s1_re, s1_im, ylin_il = _s5_forward("s5_fwd1", u_il, *b_blk[1], *lbar[1], reverse=True, other=state0,
                                        c_re=c_blk[0], c_im_neg=c_blk[1])
    ylin = _deinterleave(ylin_il)
    states = [tuple(state0), (s1_re, s1_im)]
    s5_par = [wt["s5_d"], wt["s5_w_glu"], wt["s5_b_glu"]]
    (o_s5,) = _rowwise("s5_out", _s5_out_fn, [ylin, u], s5_par, [(S5_WIDTH, BF16)], 256)

    pre_par = [wt["w0"][0], wt["w0"][1], wt["w_up"][0], wt["w_up"][1], wt["a0"][0], wt["a0"][1],
               wt["a_up"][0], wt["a_up"][1], wt["g_up"], wt["k_k"], wt["k_a"]]
    pre = _rowwise("rk_pre", _rk_pre_fn, [k, wdn, adn, gdn], pre_par + [seg, seg_t], [(RK_WIDTH, F32)] * 8, 256)
    kk, lw, kd, act, gate = pre[0], pre[1:3], pre[3:5], pre[5:7], pre[7]
    core_in, ys, cks = [], [], []
    for d in range(2):
        ops = (r, lw[d], kd[d], v, kk, act[d])
        plan = _gather_halves_plan([ffn_shards[d]]) if ffn_shards is not None else None
        y, ck, gathered = _rk_core_fwd(f"rk_core{d}", *ops, reverse=(d == 1), chunk=min(chunk, t), hosted=plan)
        if gathered:
            wt["ffn_w1" if d == 0 else "ffn_w2"] = gathered[0] if d == 0 else gathered[0].reshape(FFN, D_MODEL)
        core_in.append(ops)
        ys.append(y)
        cks.append(ck)
    post_rows = [ys[0], ys[1], r, v, kd[0], kd[1], gate]
    post_par = [wt["ln_gain"], wt["ln_bias"], wt["r_k"]]
    (o_rk,) = _rowwise("rk_post", _rk_post_fn, post_rows, post_par + [seg, seg_t], [(RK_WIDTH, BF16)], 256)

    o = jnp.concatenate([o_s5, o_rk], axis=1)
    mixed = _matmul("mix_out", o, wt["w_out"])
    n2_par = [gt1, wt["norm2_gain"], sc2, sh2]
    x1, h2 = _rowwise("norm2", _resid_norm_mod_fn, [x, mixed], n2_par, [(D_MODEL, F32), (D_MODEL, BF16)], 256)
    f1, hid = _matmul("ffn1", h2, wt["ffn_w1"], out_dtypes=(F32, BF16), chips="b",
                      epilogue=lambda acc: (acc, jnp.square(jnp.maximum(acc, 0.0))))
    ffn = _matmul("ffn2", hid, wt["ffn_w2"])

    ones = jnp.ones((t, 1), F32)
    loss_rows, dx1, dffn, g_gt2, g["final_gain"] = _rowwise_vjp(
        "loss", _loss_fn, [x1, ffn, target], [gt2, wt["final_gain"]], [[ones]], [0, 1], [0, 1], 256, emit=(0,),
        row_grad_dtypes=[F32, BF16])
    df1 = _matmul("ffn2_dx", dffn, wt["ffn_w2"], tb=True, extras=(f1,), out_dtypes=(BF16,),
                  epilogue=lambda acc, f: (acc * (2.0 * jnp.maximum(f, 0.0)),))
    g["ffn_w2"] = _matmul("ffn2_dw", hid, dffn, ta=True)
    if ffn_shards is None:
        g["ffn_w1"] = _matmul("ffn1_dw", h2, df1, ta=True, chips="out")
        dh2 = _matmul("ffn1_dx", df1, wt["ffn_w1"], tb=True, chips="b_t")
    else:
        piece2 = g.pop("ffn_w2").reshape(N_CHIPS, -1, D_MODEL)
        piece1, (other2,) = _matmul("ffn1_dw", h2, df1, ta=True, chips="out", hosted=_other_half_plan([piece2]))
        dh2, (other1,) = _matmul("ffn1_dx", df1, wt["ffn_w1"], tb=True, chips="b_t",
                                 hosted=_other_half_plan([piece1]))
        ffn_sums = [_pair_sum("pair_" + n, piece, other, BF16)
                    for n, piece, other in zip(FFN_SHARDED, (piece1, piece2), (other1, other2))]
    dx_a, dmixed, g_gt1, g["norm2_gain"], g_sc2, g_sh2 = _rowwise_vjp(
        "norm2_bwd", _resid_norm_mod_fn, [x, mixed], n2_par, [[dx1], [dh2]], [0, 1], [0, 1, 2, 3], 256,
        row_grad_dtypes=[F32, BF16])
    do = _matmul("mix_out_dx", dmixed, wt["w_out"], tb=True)
    g["w_out"] = _matmul("mix_out_dw", o, dmixed, ta=True)
    do_s5, do_rk = do[:, :S5_WIDTH], do[:, S5_WIDTH:]

    dylin, du, g["s5_d"], g["s5_w_glu"], g["s5_b_glu"] = _rowwise_vjp(
        "s5_out_bwd", _s5_out_fn, [ylin, u], s5_par, [[do_s5]], [0, 1], [0, 1, 2], 256)
    prep_cts = []
    dylin_il, du_il = _interleave(dylin), _interleave(du)
    for d in range(2):
        res = _s5_backward(f"s5_bwd{d}", dylin_il, u_il, du_il, states[d], states[1] if d == 0 else None,
                           *b_blk[d], *c_blk, *lbar[d], reverse=(d == 1))
        du_il, db_re, db_im, dl_re, dl_im = res[:5]
        if d == 0:
            g["c_re"], g["c_im"] = _s5_out_unblock(res[5]), -_s5_out_unblock(res[6])
        prep_cts += [[dl_re.reshape(S5_CH, 1)], [dl_im.reshape(S5_CH, 1)], [_s5_in_unblock(db_re)],
                     [_s5_in_unblock(db_im)]]
    du = _deinterleave(du_il)
    pg = _rowwise_vjp("s5_prep_bwd", _s5_prep_fn, prep_rows, [], prep_cts, list(range(8)), [], 512)
    g["lam_re"], g["lam_im"], g["log_step"] = (pg[0], pg[3]), (pg[1], pg[4]), (pg[2], pg[5])
    g["b_re"], g["b_im"] = pg[6], pg[7]

    pb = _rowwise_vjp("rk_post_bwd", _rk_post_fn, post_rows, post_par, [[do_rk]], [0, 2, 3, 4, 5, 6], [0, 1, 2],
                      128, consts=[seg, seg_t])
    dy, dr_b, dv_b, dkd_b, dgate = pb[0], pb[1], pb[2], pb[3:5], pb[5]
    g["ln_gain"], g["ln_bias"], g["r_k"] = pb[6], pb[7], pb[8]
    cg = []
    for d in range(2):
        plan = None
        if ffn_shards is not None:
            plan = _exchange_plan([ffn_sums[d]], CHIP_PEERS, N_CHIPS, scatter=True)
        grads, arrived = _rk_core_bwd(f"rk_core{d}_bwd", *core_in[d], cks[d], dy, reverse=(d == 1),
                                      chunk=min(chunk, t), hosted=plan)
        g.setdefault("ffn_arrived", []).extend(arrived)
        cg.append(grads)
    pre_cts = [[cg[0][4], cg[1][4]], [cg[0][1]], [cg[1][1]], [cg[0][2], dkd_b[0]], [cg[1][2], dkd_b[1]],
               [cg[0][5]], [cg[1][5]], [dgate]]
    qb = _rowwise_vjp("rk_pre_bwd", _rk_pre_fn, [k, wdn, adn, gdn], pre_par, pre_cts, [0, 1, 2, 3],
                      list(range(11)), 128, consts=[seg, seg_t])
    dk, dwdn, dadn, dgdn = qb[:4]
    g["w0"], g["w_up"], g["a0"], g["a_up"] = (qb[4], qb[5]), (qb[6], qb[7]), (qb[8], qb[9]), (qb[10], qb[11])
    g["g_up"], g["k_k"], g["k_a"] = qb[12], qb[13], qb[14]
    dr, dv = _rowwise("rk_sum", lambda a, b, c, e, f, h: (a + b + c, e + f + h),
                      [cg[0][0], cg[1][0], dr_b, cg[0][3], cg[1][3], dv_b], [], [(RK_WIDTH, F32)] * 2, 256)
    dps = jnp.concatenate([dr, dk, dv, dwdn, dadn, dgdn], axis=1)
    dp, g["mu_prev"], g["mu_next"] = _token_shift_bwd(proj, wt["mu_prev"], wt["mu_next"], dps,
                                                      first=S5_WIDTH // 128)

    dproj = jnp.concatenate([du, dp], axis=1).astype(BF16)
    dh1 = _matmul("proj_dx", dproj, wt["w_in"], tb=True)
    g["w_in"] = _matmul("proj_dw", h1, dproj, ta=True)
    grad_x, g["norm1_gain"], g_sc1, g_sh1 = _rowwise_vjp(
        "norm1_bwd", _norm_mod_fn, [x], [wt["norm1_gain"], sc1, sh1], [[dh1]], [0], [0, 1, 2], 256,
        addends={0: dx_a})
    g["mod"] = [g_sh1, g_sc1, g_gt1, g_sh2, g_sc2, g_gt2]
    return loss_rows, grad_x, g


CHIP_PEERS = ((1, 0, 0), (0, 1, 0), (1, 1, 0))
ALL_PEERS = ((0, 0, 1), (0, 1, 0), (0, 1, 1), (1, 0, 0), (1, 0, 1), (1, 1, 0), (1, 1, 1))
CORE_PEER = ((0, 0, 1),)


def _exchange(name, arrays, peers, n_slots, scatter=False):
    return _run_plan(name, _exchange_plan(arrays, peers, n_slots, scatter))


def _run_plan(name, plan):
    na = len(plan.arrays)

    def body(*refs):
        plan.start(refs[:na], refs[na:2 * na], refs[2 * na:])
        plan.wait(refs[:na], refs[na:2 * na], refs[2 * na:])

    any_spec = pl.BlockSpec(memory_space=pl.ANY)
    return plan.finish(pl.pallas_call(
        body, name=name, in_specs=[any_spec] * na, out_specs=[any_spec] * na, out_shape=plan.out_shape,
        scratch_shapes=plan.sems,
    )(*plan.arrays))


def _exchange_plan(arrays, peers, n_slots, scatter=False):
    na, nm = len(arrays), len(peers)

    def ident(px, py, pc):
        return {8: 4 * px + 2 * py + pc, 4: 2 * px + py, 2: pc}[n_slots]

    def copies(in_refs, out_refs, sems):
        send_sems, recv_sems = sems
        x, y, c = lax.axis_index("x"), lax.axis_index("y"), lax.axis_index("c")
        me = ident(x, y, c)
        made = []
        for i in range(na):
            for j, (fx, fy, fc) in enumerate(peers):
                px, py, pc = (1 - x if fx else x), (1 - y if fy else y), (1 - c if fc else c)
                src = in_refs[i].at[ident(px, py, pc)] if scatter else in_refs[i]
                made.append(pltpu.make_async_remote_copy(
                    src_ref=src, dst_ref=out_refs[i].at[me],
                    send_sem=send_sems.at[i * nm + j], recv_sem=recv_sems.at[i * nm + j],
                    device_id=(px, py, pc), device_id_type=pl.DeviceIdType.MESH))
        return made

    def start(in_refs, out_refs, sems):
        for copy in copies(in_refs, out_refs, sems):
            copy.start()

    def wait(in_refs, out_refs, sems):
        for copy in copies(in_refs, out_refs, sems):
            copy.wait()

    def finish(outs):
        me = ident(lax.axis_index("x"), lax.axis_index("y"), lax.axis_index("c"))
        return [lax.dynamic_update_slice_in_dim(
            o, lax.dynamic_index_in_dim(a, me, 0, keepdims=True) if scatter else a[None], me, axis=0)
            for a, o in zip(arrays, outs)]

    out_shape = [jax.ShapeDtypeStruct(((n_slots,) + a.shape[1:]) if scatter else ((n_slots,) + a.shape), a.dtype)
                 for a in arrays]
    sems = [pltpu.SemaphoreType.DMA((na * nm,)), pltpu.SemaphoreType.DMA((na * nm,))]
    return _Plan(arrays, out_shape, sems, start, wait, finish)


def _gather_halves(name, arrays):
    return _run_plan(name, _gather_halves_plan(arrays))


def _gather_halves_plan(arrays):
    na = len(arrays)
    chips = ((1, 0), (0, 1), (1, 1))

    def over_ici(in_refs, out_refs, sems):
        ici_send, ici_recv = sems[:2]
        x, y, c = lax.axis_index("x"), lax.axis_index("y"), lax.axis_index("c")
        made = []
        for i in range(na):
            half = arrays[i].shape[0] // 2
            mine = pl.ds(pl.multiple_of(c * half, 8), half)
            for j, (fx, fy) in enumerate(chips):
                px, py = (1 - x if fx else x), (1 - y if fy else y)
                k = len(chips) * i + j
                made.append([pltpu.make_async_remote_copy(
                    src_ref=in_refs[i].at[mine], dst_ref=out_refs[i].at[chip, mine],
                    send_sem=ici_send.at[k], recv_sem=ici_recv.at[k],
                    device_id=(px, py, c), device_id_type=pl.DeviceIdType.MESH)
                    for chip in (2 * x + y, 2 * px + py)])
        return made

    def start(in_refs, out_refs, sems):
        for outgoing, _ in over_ici(in_refs, out_refs, sems):
            outgoing.start()

    def wait(in_refs, out_refs, sems):
        d2d_send, d2d_recv = sems[2:]
        x, y, c = lax.axis_index("x"), lax.axis_index("y"), lax.axis_index("c")
        pending = []
        ici = over_ici(in_refs, out_refs, sems)
        for i in range(na):
            half = arrays[i].shape[0] // 2
            mine = pl.ds(pl.multiple_of(c * half, 8), half)
            theirs = pl.ds(pl.multiple_of((1 - c) * half, 8), half)
            for j, (fx, fy) in enumerate(chips):
                px, py = (1 - x if fx else x), (1 - y if fy else y)
                k = len(chips) * i + j
                outgoing, landing = ici[k]
                landing.wait_recv()
                landed = out_refs[i].at[2 * px + py, mine]
                passed = pltpu.make_async_remote_copy(
                    src_ref=landed, dst_ref=landed, send_sem=d2d_send.at[k], recv_sem=d2d_recv.at[k],
                    device_id=(x, y, 1 - c), device_id_type=pl.DeviceIdType.MESH)
                passed.start()
                from_sibling = out_refs[i].at[2 * px + py, theirs]
                pending += [outgoing.wait_send, passed.wait_send, pltpu.make_async_remote_copy(
                    src_ref=from_sibling, dst_ref=from_sibling, send_sem=d2d_send.at[k], recv_sem=d2d_recv.at[k],
                    device_id=(x, y, 1 - c), device_id_type=pl.DeviceIdType.MESH).wait_recv]
        for done in pending:
            done()

    def finish(outs):
        me = 2 * lax.axis_index("x") + lax.axis_index("y")
        return [lax.dynamic_update_slice_in_dim(o, a[None], me, axis=0) for a, o in zip(arrays, outs)]

    out_shape = [jax.ShapeDtypeStruct((N_CHIPS,) + a.shape, a.dtype) for a in arrays]
    return _Plan(arrays, out_shape, [pltpu.SemaphoreType.DMA((na * len(chips),))] * 4, start, wait, finish)


def _send_other_half(name, arrays):
    return _run_plan(name, _other_half_plan(arrays))


def _other_half_plan(arrays):
    na = len(arrays)

    def copies(in_refs, out_refs, sems):
        send_sems, recv_sems = sems
        x, y, c = lax.axis_index("x"), lax.axis_index("y"), lax.axis_index("c")
        made = []
        for i in range(na):
            half = arrays[i].shape[1] // 2
            theirs = pl.ds(pl.multiple_of((1 - c) * half, 8), half)
            made.append(pltpu.make_async_remote_copy(
                src_ref=in_refs[i].at[:, theirs], dst_ref=out_refs[i], send_sem=send_sems.at[i],
                recv_sem=recv_sems.at[i], device_id=(x, y, 1 - c), device_id_type=pl.DeviceIdType.MESH))
        return made

    def start(in_refs, out_refs, sems):
        for copy in copies(in_refs, out_refs, sems):
            copy.start()

    def wait(in_refs, out_refs, sems):
        for copy in copies(in_refs, out_refs, sems):
            copy.wait()

    out_shape = [jax.ShapeDtypeStruct((a.shape[0], a.shape[1] // 2, a.shape[2]), a.dtype) for a in arrays]
    sems = [pltpu.SemaphoreType.DMA((na,)), pltpu.SemaphoreType.DMA((na,))]
    return _Plan(arrays, out_shape, sems, start, wait, list)


def _adam_math(w, g, m, v):
    m = ADAM_B1 * m + (1.0 - ADAM_B1) * g
    v = ADAM_B2 * v + (1.0 - ADAM_B2) * jnp.square(g)
    m_hat = m / (1.0 - ADAM_B1 ** ADAM_STEP)
    v_hat = v / (1.0 - ADAM_B2 ** ADAM_STEP)
    delta = -ADAM_LR * (m_hat / (jnp.sqrt(v_hat) + ADAM_EPS) + ADAM_WD * w)
    return delta, m, v


WHOLE_BLOCK_BYTES = 2 * 1024 * 1024


def _row_tile(r, c):
    return r if 4 * r * c <= WHOLE_BLOCK_BYTES else _tile(r, (256, 128, 64, 32, 16, 8))


def _sum_parts(name, parts):
    n, r, c = parts.shape
    tr = _row_tile(r, c)

    def body(p_ref, o_ref):
        tot = p_ref[0].astype(F32)
        for i in range(1, n):
            tot = tot + p_ref[i].astype(F32)
        o_ref[...] = tot

    return pl.pallas_call(
        body, name=name, grid=(r // tr,), in_specs=[pl.BlockSpec((n, tr, c), lambda i: (0, i, 0))],
        out_specs=pl.BlockSpec((tr, c), lambda i: (i, 0)), out_shape=jax.ShapeDtypeStruct((r, c), F32),
        compiler_params=_params(("parallel",)),
    )(parts)


def _pair_sum(name, piece, other, dtype):
    n, r, c = piece.shape
    half = r // 2
    tr = _row_tile(half, c)

    def body(lo_ref, hi_ref, other_ref, o_ref):
        own = jnp.where(lax.axis_index("c") == 0, lo_ref[...], hi_ref[...])
        o_ref[...] = (own + other_ref[...]).astype(o_ref.dtype)

    blk = pl.BlockSpec((None, tr, c), lambda j, i: (j, i, 0))
    return pl.pallas_call(
        body, name=name, grid=(n, half // tr),
        in_specs=[pl.BlockSpec((None, None, tr, c), lambda j, i: (j, 0, i, 0)),
                  pl.BlockSpec((None, None, tr, c), lambda j, i: (j, 1, i, 0)), blk],
        out_specs=blk, out_shape=jax.ShapeDtypeStruct((n, half, c), dtype),
        compiler_params=_params(("parallel", "parallel")),
    )(piece.reshape(n, 2, half, c), piece.reshape(n, 2, half, c), other)


def _adamw(name, w, parts, m, v, hosted=None):
    n, r, c = parts.shape
    tr = _row_tile(r, c)
    steps = r // tr
    plan = hosted or _NO_PLAN
    nh = len(plan.arrays)

    def body(w_ref, p_ref, m_ref, v_ref, *rest):
        host_in, (g_ref, d_ref, nm_ref, nv_ref) = rest[:nh], rest[nh:nh + 4]
        host_out, sems = rest[nh + 4:2 * nh + 4], rest[2 * nh + 4:]

        @pl.when(pl.program_id(0) == 0)
        def _():
            plan.start(host_in, host_out, sems)

        g = p_ref[0].astype(F32)
        for i in range(1, n):
            g = g + p_ref[i].astype(F32)
        delta, nm, nv = _adam_math(w_ref[...], g, m_ref[...], v_ref[...])
        g_ref[...], d_ref[...], nm_ref[...], nv_ref[...] = g, delta, nm, nv

        @pl.when(pl.program_id(0) == steps - 1)
        def _():
            plan.wait(host_in, host_out, sems)

    blk = pl.BlockSpec((tr, c), lambda i: (i, 0))
    any_spec = pl.BlockSpec(memory_space=pl.ANY)
    res = pl.pallas_call(
        body, name=name, grid=(steps,),
        in_specs=[blk, pl.BlockSpec((n, tr, c), lambda i: (0, i, 0)), blk, blk] + [any_spec] * nh,
        out_specs=[blk] * 4 + [any_spec] * nh,
        out_shape=[jax.ShapeDtypeStruct((r, c), F32)] * 4 + plan.out_shape, scratch_shapes=plan.sems,
        compiler_params=_params(("arbitrary",) if nh else ("parallel",)),
    )(w, parts, m, v, *plan.arrays)
    return (res[:4], plan.finish(res[4:])) if nh else res


def _ada_w_update(act_t, dmod, w, m, v, hosted):
    r, c = w.shape
    nb = act_t.shape[1]
    tr, tc = 256, 1024
    grid = (r // tr, c // tc)
    nh = len(hosted.arrays)

    def body(a_ref, d_ref, w_ref, m_ref, v_ref, *rest):
        host_in, (g_ref, dl_ref, nm_ref, nv_ref) = rest[:nh], rest[nh:nh + 4]
        host_out, sems = rest[nh + 4:2 * nh + 4], rest[2 * nh + 4:]
        i, j = pl.program_id(0), pl.program_id(1)

        @pl.when(jnp.logical_and(i == 0, j == 0))
        def _():
            hosted.start(host_in, host_out, sems)

        a, dm = a_ref[...], d_ref[...]
        g = a[:, 0:1] * dm[0:1, :]
        for b in range(1, nb):
            g = g + a[:, b:b + 1] * dm[b:b + 1, :]
        delta, nm, nv = _adam_math(w_ref[...], g, m_ref[...], v_ref[...])
        g_ref[...], dl_ref[...], nm_ref[...], nv_ref[...] = g, delta, nm, nv

        @pl.when(jnp.logical_and(i == grid[0] - 1, j == grid[1] - 1))
        def _():
            hosted.wait(host_in, host_out, sems)

    blk = pl.BlockSpec((tr, tc), lambda i, j: (i, j))
    any_spec = pl.BlockSpec(memory_space=pl.ANY)
    res = pl.pallas_call(
        body, name="ada_w_update", grid=grid,
        in_specs=[pl.BlockSpec((tr, nb), lambda i, j: (i, 0)), pl.BlockSpec((nb, tc), lambda i, j: (0, j)),
                  blk, blk, blk] + [any_spec] * nh,
        out_specs=[blk] * 4 + [any_spec] * nh,
        out_shape=[jax.ShapeDtypeStruct((r, c), F32)] * 4 + hosted.out_shape,
        scratch_shapes=hosted.sems,
        compiler_params=_params(("arbitrary", "arbitrary")),
    )(act_t, dmod, w, m, v, *hosted.arrays)
    return res[:4], hosted.finish(res[4:])


WEIGHTS = ['ada_w', 'ada_b', 'norm1_gain', 'norm2_gain', 'final_gain', 'w_in', 'w_out', 's5_lambda_re',
           's5_lambda_im', 's5_log_step', 's5_b_re', 's5_b_im', 's5_c_re', 's5_c_im', 's5_d', 's5_w_glu',
           's5_b_glu', 'rk_shift_prev', 'rk_shift_next', 'rk_w0', 'rk_w_up', 'rk_a0', 'rk_a_up', 'rk_g_up',
           'rk_k_k', 'rk_k_a', 'rk_r_k', 'rk_ln_gain', 'rk_ln_bias', 'ffn_w1', 'ffn_w2']
BIG_SHARDED = ['w_in', 'w_out', 's5_w_glu', 'ffn_w1', 'ffn_w2']
FFN_SHARDED = ['ffn_w1', 'ffn_w2']
RK_SHARDED = ['rk_w0', 'rk_a0', 'rk_w_up', 'rk_a_up', 'rk_g_up']
REPLICATED = ['ada_b', 'norm1_gain', 'norm2_gain', 'final_gain', 's5_lambda_re', 's5_lambda_im', 's5_log_step',
              's5_b_re', 's5_b_im', 's5_c_re', 's5_c_im', 's5_d', 's5_b_glu', 'rk_shift_prev', 'rk_shift_next',
              'rk_k_k', 'rk_k_a', 'rk_r_k', 'rk_ln_gain', 'rk_ln_bias']
PACK_COLS = 1024
N_CHIPS = 4
RK_ROWS = 420
RK_ROWS_PAD = 432


def _pack_rows(arrays, cols):
    return jnp.concatenate([a.reshape(-1, cols) for a in arrays], axis=0)


def _pack_flat(arrays):
    flat = jnp.concatenate([a.reshape(-1) for a in arrays])
    rows = -(-flat.shape[0] // PACK_COLS)
    return jnp.pad(flat, (0, rows * PACK_COLS - flat.shape[0])).reshape(rows, PACK_COLS)


def _unpack_flat(packed, like):
    flat, out, pos = packed.reshape(-1), [], 0
    for a in like:
        out.append(flat[pos:pos + a.size].reshape(a.shape))
        pos += a.size
    return out


def _cols_to_chips(full, n_rows):
    return jnp.transpose(full.reshape(n_rows, N_CHIPS, -1), (1, 0, 2))


def _chips_to_cols(parts):
    return jnp.transpose(parts, (1, 0, 2)).reshape(parts.shape[1], -1)


def kernel(x, c, ada_w, ada_b, norm1_gain, norm2_gain, final_gain, w_in, w_out, s5_lambda_re, s5_lambda_im, s5_log_step, s5_b_re, s5_b_im, s5_c_re, s5_c_im, s5_d, s5_w_glu, s5_b_glu, rk_shift_prev, rk_shift_next, rk_w0, rk_w_up, rk_a0, rk_a_up, rk_g_up, rk_k_k, rk_k_a, rk_r_k, rk_ln_gain, rk_ln_bias, ffn_w1, ffn_w2, loss_target, m_ada_w, m_ada_b, m_norm1_gain, m_norm2_gain, m_final_gain, m_w_in, m_w_out, m_s5_lambda_re, m_s5_lambda_im, m_s5_log_step, m_s5_b_re, m_s5_b_im, m_s5_c_re, m_s5_c_im, m_s5_d, m_s5_w_glu, m_s5_b_glu, m_rk_shift_prev, m_rk_shift_next, m_rk_w0, m_rk_w_up, m_rk_a0, m_rk_a_up, m_rk_g_up, m_rk_k_k, m_rk_k_a, m_rk_r_k, m_rk_ln_gain, m_rk_ln_bias, m_ffn_w1, m_ffn_w2, v_ada_w, v_ada_b, v_norm1_gain, v_norm2_gain, v_final_gain, v_w_in, v_w_out, v_s5_lambda_re, v_s5_lambda_im, v_s5_log_step, v_s5_b_re, v_s5_b_im, v_s5_c_re, v_s5_c_im, v_s5_d, v_s5_w_glu, v_s5_b_glu, v_rk_shift_prev, v_rk_shift_next, v_rk_w0, v_rk_w_up, v_rk_a0, v_rk_a_up, v_rk_g_up, v_rk_k_k, v_rk_k_a, v_rk_r_k, v_rk_ln_gain, v_rk_ln_bias, v_ffn_w1, v_ffn_w2):
    given = dict(locals())
    w = {n: given[n] for n in WEIGHTS}
    m = {n: given["m_" + n] for n in WEIGHTS}
    v = {n: given["v_" + n] for n in WEIGHTS}
    mx, my, mc = lax.axis_index("x"), lax.axis_index("y"), lax.axis_index("c")
    chip = 2 * mx + my
    dev = 2 * chip + mc
    xt, target = x[0], loss_target[0]

    def rk_rows(d):
        rows = _pack_rows([d[n] for n in RK_SHARDED], 256)
        return jnp.pad(rows, ((0, RK_ROWS_PAD - rows.shape[0]), (0, 0)))

    (c_all,), (w_in_parts,) = _run_plan("gather_first", _join_plans([
        _exchange_plan([c], ALL_PEERS, 8), _gather_halves_plan([w_in[0].astype(BF16)])]))

    (act,) = _rowwise("ada_act", lambda q: (q * _sigmoid(q),), [c_all.reshape(8, D_MODEL)], [], [(D_MODEL, F32)], 8)
    n_mod_cols = N_MOD * D_MODEL // N_CHIPS
    bias = jnp.broadcast_to(lax.dynamic_slice(ada_b, (0, chip * n_mod_cols), (1, n_mod_cols)), (8, n_mod_cols))
    mod_shard = _matmul("ada_fwd", act, ada_w[0], epilogue=_add_epilogue, extras=(bias,))
    (mod_parts,) = _exchange("gather_mod", [mod_shard], CHIP_PEERS, N_CHIPS)
    mod_all = _chips_to_cols(mod_parts)
    mod_mine = lax.dynamic_slice(mod_all, (dev, 0), (1, N_MOD * D_MODEL))
    mod = [mod_mine[:, i * D_MODEL:(i + 1) * D_MODEL] for i in range(N_MOD)]

    def mixer_weights(parts):
        w_out_parts, glu_parts, rk_full = parts

        def rk_piece(lo, hi, lead):
            return _chips_to_cols(rk_full[:, lo:hi]).reshape(lead + (RK_WIDTH,))

        zeros = jnp.zeros((LORA, RK_WIDTH), F32)
        w_up, a_up = rk_piece(4, 132, (2, LORA)), rk_piece(132, 260, (2, LORA))
        return {
            "w_out": w_out_parts.reshape(D_MODEL, D_MODEL), "s5_w_glu": glu_parts.reshape(S5_WIDTH, S5_WIDTH),
            "w0": list(rk_piece(0, 2, (2,))[:, None, :]), "a0": list(rk_piece(2, 4, (2,))[:, None, :]),
            "w_up": [jnp.concatenate([w_up[0], zeros]), jnp.concatenate([zeros, w_up[1]])],
            "a_up": [jnp.concatenate([a_up[0], zeros]), jnp.concatenate([zeros, a_up[1]])],
            "g_up": jnp.pad(rk_piece(260, 420, (GATE_LORA,)), ((0, GATE_PAD - GATE_LORA), (0, 0))),
        }

    wt = {
        "norm1_gain": norm1_gain, "norm2_gain": norm2_gain, "final_gain": final_gain.reshape(1, D_MODEL),
        "w_in": jnp.pad(_chips_to_cols(w_in_parts), ((0, 0), (0, PROJ_PAD - PROJ))),
        "mu_prev": jnp.pad(rk_shift_prev, ((0, 0), (0, RK_PAD - RK_IN))),
        "mu_next": jnp.pad(rk_shift_next, ((0, 0), (0, RK_PAD - RK_IN))),
        "lam_re": [s5_lambda_re[0, d].reshape(S5_CH, 1) for d in range(2)],
        "lam_im": [s5_lambda_im[0, d].reshape(S5_CH, 1) for d in range(2)],
        "log_step": [jnp.repeat(s5_log_step[0, d], S5_STATE).reshape(S5_CH, 1) for d in range(2)],
        "b_re": s5_b_re.reshape(S5_CH, S5_GROUP), "b_im": s5_b_im.reshape(S5_CH, S5_GROUP),
        "c_re": s5_c_re[0], "c_im": s5_c_im[0],
        "s5_d": s5_d, "s5_b_glu": s5_b_glu,
        "k_k": rk_k_k, "k_a": rk_k_a, "r_k": rk_r_k.reshape(1, RK_WIDTH),
        "ln_gain": rk_ln_gain, "ln_bias": rk_ln_bias,
    }

    ffn_shards = [w[n][0].astype(BF16) for n in FFN_SHARDED]
    mixer_shards = [w_out[0].astype(BF16), s5_w_glu[0].astype(BF16), rk_rows(w)]
    loss_rows, grad_x, g = _local_step(xt, target, mod, wt, ffn_shards=ffn_shards,
                                       mixer_shards=(mixer_shards, mixer_weights))
    loss = lax.psum(jnp.sum(loss_rows), ("x", "y", "c"))


    big_grads = {
        "w_in": _cols_to_chips(g["w_in"][:, :PROJ], D_MODEL),
        "w_out": g["w_out"].reshape(N_CHIPS, -1, D_MODEL),
        "s5_w_glu": g["s5_w_glu"].reshape(N_CHIPS, -1, S5_WIDTH),
    }
    rk_grads = jnp.concatenate([
        _cols_to_chips(jnp.concatenate(g["w0"]), 2), _cols_to_chips(jnp.concatenate(g["a0"]), 2),
        _cols_to_chips(jnp.concatenate([g["w_up"][0][:LORA], g["w_up"][1][LORA:]]), 2 * LORA),
        _cols_to_chips(jnp.concatenate([g["a_up"][0][:LORA], g["a_up"][1][LORA:]]), 2 * LORA),
        _cols_to_chips(g["g_up"][:GATE_LORA], GATE_LORA),
        jnp.zeros((N_CHIPS, RK_ROWS_PAD - RK_ROWS, 256), F32)], axis=1)
    local_small = {
        "ada_b": jnp.concatenate(g["mod"], axis=1),
        "norm1_gain": g["norm1_gain"], "norm2_gain": g["norm2_gain"], "final_gain": g["final_gain"],
        "s5_lambda_re": jnp.concatenate(g["lam_re"]), "s5_lambda_im": jnp.concatenate(g["lam_im"]),
        "s5_log_step": jnp.concatenate([q.reshape(S5_GROUPS, S5_STATE).sum(axis=1) for q in g["log_step"]]),
        "s5_b_re": g["b_re"], "s5_b_im": g["b_im"], "s5_c_re": g["c_re"], "s5_c_im": g["c_im"],
        "s5_d": g["s5_d"], "s5_b_glu": g["s5_b_glu"],
        "rk_shift_prev": g["mu_prev"][:, :RK_IN], "rk_shift_next": g["mu_next"][:, :RK_IN],
        "rk_k_k": g["k_k"], "rk_k_a": g["k_a"], "rk_r_k": g["r_k"],
        "rk_ln_gain": g["ln_gain"], "rk_ln_bias": g["ln_bias"],
    }
    late = [n for n in BIG_SHARDED if n not in FFN_SHARDED]
    late_pieces = [big_grads[n] for n in late] + [rk_grads]
    late_names = late + ["rk"]

    def whole(halves):
        return halves.reshape(1, 2 * halves.shape[1], halves.shape[2])

    ffn_halves = [_sum_parts("sum_" + n, a) for n, a in zip(FFN_SHARDED, g["ffn_arrived"])]
    from_sibling, ffn_pairs, (small_all,) = _run_plan("swap_late", _join_plans([
        _other_half_plan(late_pieces), _exchange_plan(ffn_halves, CORE_PEER, 2),
        _exchange_plan([_pack_flat([local_small[n] for n in REPLICATED]).astype(BF16)], ALL_PEERS, 8)]))
    late_sums = [_pair_sum("pair_" + n, piece, other, F32 if n == "rk" else BF16)
                 for n, piece, other in zip(late_names, late_pieces, from_sibling)]
    pairs = dict(zip(FFN_SHARDED, [whole(p) for p in ffn_pairs]))

    mod_rows = N_MOD * D_MODEL // PACK_COLS
    dmod_all = small_all[:, :mod_rows].reshape(8, N_MOD * D_MODEL).astype(F32)
    dmod = lax.dynamic_slice(dmod_all, (0, chip * n_mod_cols), (8, n_mod_cols))
    ada_res, arrived = _ada_w_update(act.T, dmod, ada_w[0], m_ada_w[0], v_ada_w[0],
                                     hosted=_exchange_plan(late_sums, CHIP_PEERS, N_CHIPS, scatter=True))
    late_halves = [_sum_parts("sum_" + n, a) for n, a in zip(late_names, arrived)]

    out = {"ada_w": [r[None] for r in ada_res]}
    first = FFN_SHARDED[0]
    res, swapped = _adamw("adamw_" + first, w[first][0], pairs[first], m[first][0], v[first][0],
                          hosted=_exchange_plan(late_halves, CORE_PEER, 2))
    out[first] = [r[None] for r in res]
    pairs.update(zip(late_names, [whole(p) for p in swapped]))
    for n in [FFN_SHARDED[1]] + late:
        out[n] = [r[None] for r in _adamw("adamw_" + n, w[n][0], pairs[n], m[n][0], v[n][0])]
    rk_res = _adamw("adamw_rk", rk_rows(w), pairs["rk"], rk_rows(m), rk_rows(v))
    for q in range(4):
        pieces, pos = [], 0
        for n in RK_SHARDED:
            rows = w[n].size // 256
            pieces.append(rk_res[q][pos:pos + rows].reshape(w[n].shape))
            pos += rows
        for n, piece in zip(RK_SHARDED, pieces):
            out.setdefault(n, []).append(piece)

    small_res = _adamw("adamw_small", _pack_flat([w[n] for n in REPLICATED]), small_all,
                       _pack_flat([m[n] for n in REPLICATED]), _pack_flat([v[n] for n in REPLICATED]))
    for q in range(4):
        for n, piece in zip(REPLICATED, _unpack_flat(small_res[q], [w[n] for n in REPLICATED])):
            out.setdefault(n, []).append(piece)

    return (loss, grad_x[None], *[out[n][0] for n in WEIGHTS], *[out[n][1] for n in WEIGHTS],
            *[out[n][2] for n in WEIGHTS], *[out[n][3] for n in WEIGHTS])
```
